```python
import jax
import jax.numpy as jnp
from jax import lax
import numpy as np

D_MODEL = 1024
BATCH = 8
SEQ = 8192
DEPTH = 1

D_LRU = 1024
LRU_BLOCKS = 4
LRU_BLOCK_W = D_LRU // LRU_BLOCKS
CONV_W = 4
LRU_C = 8.0
D_HGRN = 1024
HGRN_EXPAND = 128
HGRN_HEADS = D_HGRN // HGRN_EXPAND
HGRN_DK = HGRN_EXPAND
HGRN_DV = D_HGRN // HGRN_HEADS
CHUNK = 64
D_MIX = D_LRU + D_HGRN
D_IN = 2 * D_LRU + 4 * D_HGRN
EPS = 1e-6

kernel_name = 'hymba_style_rglru_hgrn2_block'


def rmsnorm(x, w):
    xf = x.astype(jnp.float32)
    y = xf * lax.rsqrt(jnp.mean(xf * xf, axis=-1, keepdims=True) + EPS)
    return (y * w.astype(jnp.float32)).astype(x.dtype)


def causal_depthwise_conv(x, w, b):
    seq = x.shape[1]
    xp = jnp.pad(x, ((0, 0), (CONV_W - 1, 0), (0, 0)))
    y = b
    for k in range(CONV_W):
        y = y + w[k] * xp[:, k:k + seq, :]
    return y


def rg_lru(x, w_a, b_a, w_x, b_x, lam):
    bsz, seq, _ = x.shape
    xf = x.astype(jnp.float32)
    xb = xf.reshape(bsz, seq, LRU_BLOCKS, LRU_BLOCK_W)
    r = jax.nn.sigmoid(jnp.einsum('bsnh,nhk->bsnk', xb, w_a.astype(jnp.float32)) + b_a.astype(jnp.float32)).reshape(bsz, seq, D_LRU)
    i = jax.nn.sigmoid(jnp.einsum('bsnh,nhk->bsnk', xb, w_x.astype(jnp.float32)) + b_x.astype(jnp.float32)).reshape(bsz, seq, D_LRU)
    log_a = -LRU_C * r * jax.nn.softplus(-lam.astype(jnp.float32))
    a = jnp.exp(log_a)
    u = jnp.sqrt(-jnp.expm1(2.0 * log_a)) * (i * xf)

    def combine(left, right):
        a_l, h_l = left
        a_r, h_r = right
        return a_l * a_r, a_r * h_l + h_r

    _, h = lax.associative_scan(combine, (a, u), axis=1)
    return h.astype(x.dtype)


def hgrn2_chunked(q, k, v, g):
    bsz, seq = q.shape[0], q.shape[1]
    n_chunks = seq // CHUNK

    def to_chunks(t):
        return t.reshape(bsz, n_chunks, CHUNK, HGRN_HEADS, t.shape[-1]).transpose(1, 0, 3, 2, 4)

    mask = jnp.tril(jnp.ones((CHUNK, CHUNK), dtype=bool))

    def step(state, inp):
        qc, kc, vc, gc = inp
        b = jnp.cumsum(gc, axis=2)
        o_inter = jnp.einsum('bhtk,bhkv->bhtv', qc * jnp.exp(b), state)
        diff = b[:, :, :, None, :] - b[:, :, None, :, :]
        decay = jnp.where(mask[:, :, None], jnp.exp(jnp.minimum(diff, 0.0)), 0.0)
        scores = jnp.einsum('bhtk,bhsk,bhtsk->bhts', qc, kc, decay)
        o_intra = jnp.einsum('bhts,bhsv->bhtv', scores, vc)
        b_last = b[:, :, -1:, :]
        new_state = jnp.exp(b_last[:, :, 0, :, None]) * state + jnp.einsum('bhsk,bhsv->bhkv', kc * jnp.exp(b_last - b), vc)
        return new_state, o_inter + o_intra

    s0 = jnp.zeros((bsz, HGRN_HEADS, HGRN_DK, HGRN_DV), jnp.float32)
    _, o = lax.scan(step, s0, (to_chunks(q), to_chunks(k), to_chunks(v), to_chunks(g)))
    return o.transpose(1, 0, 3, 2, 4).reshape(bsz, seq, HGRN_HEADS, HGRN_DV)


def _fwd_setup_inputs(seed: int = 0) -> dict:
    key = jax.random.key(seed)
    ks = jax.random.split(key, 14)
    f32 = jnp.float32

    def nrm(k, shape, scale):
        return scale * jax.random.normal(k, shape, f32)

    x = nrm(ks[0], (BATCH, SEQ, D_MODEL), 1.0)
    pre_norm_w = 1.0 + nrm(ks[1], (DEPTH, D_MODEL), 0.05)
    w_in = nrm(ks[2], (DEPTH, D_MODEL, D_IN), D_MODEL ** -0.5)
    conv_w = nrm(ks[3], (DEPTH, CONV_W, D_LRU), CONV_W ** -0.5)
    conv_b = nrm(ks[4], (DEPTH, D_LRU), 0.01)
    lru_w_a = nrm(ks[5], (DEPTH, LRU_BLOCKS, LRU_BLOCK_W, LRU_BLOCK_W), LRU_BLOCK_W ** -0.5)
    lru_b_a = nrm(ks[6], (DEPTH, LRU_BLOCKS, LRU_BLOCK_W), 0.01)
    lru_w_x = nrm(ks[7], (DEPTH, LRU_BLOCKS, LRU_BLOCK_W, LRU_BLOCK_W), LRU_BLOCK_W ** -0.5)
    lru_b_x = nrm(ks[8], (DEPTH, LRU_BLOCKS, LRU_BLOCK_W), 0.01)
    a_pow_c = jax.random.uniform(ks[9], (DEPTH, D_LRU), f32, 0.9, 0.999)
    s = a_pow_c ** (1.0 / LRU_C)
    lru_lambda = jnp.log(s) - jnp.log1p(-s)
    hgrn_lb_logits = nrm(ks[10], (DEPTH + 1, D_HGRN), 0.1)
    hgrn_gnorm_w = 1.0 + nrm(ks[11], (DEPTH, D_HGRN), 0.05)
    w_out = nrm(ks[12], (DEPTH, D_MIX, D_MODEL), D_MIX ** -0.5)
    post_norm_w = 1.0 + nrm(ks[13], (DEPTH, D_MODEL), 0.05)
    return {'x': x, 'pre_norm_w': pre_norm_w, 'w_in': w_in, 'conv_w': conv_w, 'conv_b': conv_b,
            'lru_w_a': lru_w_a, 'lru_b_a': lru_b_a, 'lru_w_x': lru_w_x, 'lru_b_x': lru_b_x,
            'lru_lambda': lru_lambda, 'hgrn_lb_logits': hgrn_lb_logits, 'hgrn_gnorm_w': hgrn_gnorm_w,
            'w_out': w_out, 'post_norm_w': post_norm_w}


def _fwd_reference(x, pre_norm_w, w_in, conv_w, conv_b, lru_w_a, lru_b_a, lru_w_x, lru_b_x,
              lru_lambda, hgrn_lb_logits, hgrn_gnorm_w, w_out, post_norm_w):
    bsz, seq, _ = x.shape
    dt = x.dtype
    f32 = jnp.float32
    lower_bounds = jnp.cumsum(jax.nn.softmax(hgrn_lb_logits.astype(f32), axis=0), axis=0)
    split_at = [D_LRU, 2 * D_LRU, 2 * D_LRU + D_HGRN, 2 * D_LRU + 2 * D_HGRN, 2 * D_LRU + 3 * D_HGRN]

    def heads(t):
        return t.reshape(bsz, seq, HGRN_HEADS, -1)

    h = x
    for layer in range(DEPTH):
        u = rmsnorm(h, pre_norm_w[layer])
        p = jnp.einsum('bsd,de->bse', u, w_in[layer])
        lru_x, lru_gate, q, f_raw, v, hgrn_gate = jnp.split(p, split_at, axis=-1)

        xc = causal_depthwise_conv(lru_x, conv_w[layer], conv_b[layer])
        y_lru = rg_lru(xc, lru_w_a[layer], lru_b_a[layer], lru_w_x[layer], lru_b_x[layer],
                       lru_lambda[layer]) * jax.nn.silu(lru_gate)

        lb = lower_bounds[layer]
        f = lb + (1.0 - lb) * jax.nn.sigmoid(f_raw.astype(f32))
        o = hgrn2_chunked(heads(jax.nn.silu(q.astype(f32))), heads(1.0 - f),
                          heads(v.astype(f32)), heads(jnp.log(f)))
        o = rmsnorm(o, hgrn_gnorm_w[layer].reshape(HGRN_HEADS, HGRN_DV)).reshape(bsz, seq, D_HGRN)
        y_hgrn = (o * jax.nn.silu(hgrn_gate.astype(f32))).astype(dt)

        y = jnp.einsum('bsm,md->bsd', jnp.concatenate([y_lru.astype(dt), y_hgrn], axis=-1), w_out[layer])
        h = h + rmsnorm(y, post_norm_w[layer]).astype(dt)
    return h


import jax as _jax
import jax.numpy as _jnp

TWIN_FORMAT = 'train_step'
FWD_PARAMS = ['x', 'pre_norm_w', 'w_in', 'conv_w', 'conv_b', 'lru_w_a', 'lru_b_a', 'lru_w_x', 'lru_b_x', 'lru_lambda', 'hgrn_lb_logits', 'hgrn_gnorm_w', 'w_out', 'post_norm_w']
TWIN_WEIGHTS = ['pre_norm_w', 'w_in', 'conv_w', 'conv_b', 'lru_w_a', 'lru_b_a', 'lru_w_x', 'lru_b_x', 'lru_lambda', 'hgrn_lb_logits', 'hgrn_gnorm_w', 'w_out', 'post_norm_w']
TWIN_DIFF_INPUT = 'x'
TWIN_INPUTS = ['x', 'pre_norm_w', 'w_in', 'conv_w', 'conv_b', 'lru_w_a', 'lru_b_a', 'lru_w_x', 'lru_b_x', 'lru_lambda', 'hgrn_lb_logits', 'hgrn_gnorm_w', 'w_out', 'post_norm_w', 'loss_target', 'm_pre_norm_w', 'm_w_in', 'm_conv_w', 'm_conv_b', 'm_lru_w_a', 'm_lru_b_a', 'm_lru_w_x', 'm_lru_b_x', 'm_lru_lambda', 'm_hgrn_lb_logits', 'm_hgrn_gnorm_w', 'm_w_out', 'm_post_norm_w', 'v_pre_norm_w', 'v_w_in', 'v_conv_w', 'v_conv_b', 'v_lru_w_a', 'v_lru_b_a', 'v_lru_w_x', 'v_lru_b_x', 'v_lru_lambda', 'v_hgrn_lb_logits', 'v_hgrn_gnorm_w', 'v_w_out', 'v_post_norm_w']
TWIN_OUTPUTS = ['loss', 'grad_x', 'grad_pre_norm_w', 'grad_w_in', 'grad_conv_w', 'grad_conv_b', 'grad_lru_w_a', 'grad_lru_b_a', 'grad_lru_w_x', 'grad_lru_b_x', 'grad_lru_lambda', 'grad_hgrn_lb_logits', 'grad_hgrn_gnorm_w', 'grad_w_out', 'grad_post_norm_w', 'delta_pre_norm_w', 'delta_w_in', 'delta_conv_w', 'delta_conv_b', 'delta_lru_w_a', 'delta_lru_b_a', 'delta_lru_w_x', 'delta_lru_b_x', 'delta_lru_lambda', 'delta_hgrn_lb_logits', 'delta_hgrn_gnorm_w', 'delta_w_out', 'delta_post_norm_w', 'new_m_pre_norm_w', 'new_m_w_in', 'new_m_conv_w', 'new_m_conv_b', 'new_m_lru_w_a', 'new_m_lru_b_a', 'new_m_lru_w_x', 'new_m_lru_b_x', 'new_m_lru_lambda', 'new_m_hgrn_lb_logits', 'new_m_hgrn_gnorm_w', 'new_m_w_out', 'new_m_post_norm_w', 'new_v_pre_norm_w', 'new_v_w_in', 'new_v_conv_w', 'new_v_conv_b', 'new_v_lru_w_a', 'new_v_lru_b_a', 'new_v_lru_w_x', 'new_v_lru_b_x', 'new_v_lru_lambda', 'new_v_hgrn_lb_logits', 'new_v_hgrn_gnorm_w', 'new_v_w_out', 'new_v_post_norm_w']
TWIN_LEAF_KINDS = {'loss': 'loss', 'grad_x': 'grad_x', 'grad_pre_norm_w': 'grad_w', 'grad_w_in': 'grad_w', 'grad_conv_w': 'grad_w', 'grad_conv_b': 'grad_w', 'grad_lru_w_a': 'grad_w', 'grad_lru_b_a': 'grad_w', 'grad_lru_w_x': 'grad_w', 'grad_lru_b_x': 'grad_w', 'grad_lru_lambda': 'grad_w', 'grad_hgrn_lb_logits': 'grad_w', 'grad_hgrn_gnorm_w': 'grad_w', 'grad_w_out': 'grad_w', 'grad_post_norm_w': 'grad_w', 'delta_pre_norm_w': 'delta_w', 'delta_w_in': 'delta_w', 'delta_conv_w': 'delta_w', 'delta_conv_b': 'delta_w', 'delta_lru_w_a': 'delta_w', 'delta_lru_b_a': 'delta_w', 'delta_lru_w_x': 'delta_w', 'delta_lru_b_x': 'delta_w', 'delta_lru_lambda': 'delta_w', 'delta_hgrn_lb_logits': 'delta_w', 'delta_hgrn_gnorm_w': 'delta_w', 'delta_w_out': 'delta_w', 'delta_post_norm_w': 'delta_w', 'new_m_pre_norm_w': 'new_m', 'new_m_w_in': 'new_m', 'new_m_conv_w': 'new_m', 'new_m_conv_b': 'new_m', 'new_m_lru_w_a': 'new_m', 'new_m_lru_b_a': 'new_m', 'new_m_lru_w_x': 'new_m', 'new_m_lru_b_x': 'new_m', 'new_m_lru_lambda': 'new_m', 'new_m_hgrn_lb_logits': 'new_m', 'new_m_hgrn_gnorm_w': 'new_m', 'new_m_w_out': 'new_m', 'new_m_post_norm_w': 'new_m', 'new_v_pre_norm_w': 'new_v', 'new_v_w_in': 'new_v', 'new_v_conv_w': 'new_v', 'new_v_conv_b': 'new_v', 'new_v_lru_w_a': 'new_v', 'new_v_lru_b_a': 'new_v', 'new_v_lru_w_x': 'new_v', 'new_v_lru_b_x': 'new_v', 'new_v_lru_lambda': 'new_v', 'new_v_hgrn_lb_logits': 'new_v', 'new_v_hgrn_gnorm_w': 'new_v', 'new_v_w_out': 'new_v', 'new_v_post_norm_w': 'new_v'}


def _forward(args):
    return _fwd_reference(*[args[k] for k in FWD_PARAMS])


def _output_shape():
    out = _jax.eval_shape(lambda: _forward(_fwd_setup_inputs(0)))
    return out.shape, out.dtype

N_MICROBATCH = 1
ADAM_LR = 0.001
ADAM_B1 = 0.9
ADAM_B2 = 0.999
ADAM_EPS = 1e-08
ADAM_WD = 0.01
ADAM_STEP = 10
PER_EXAMPLE_BATCH_AXIS = {'x': 0, 'loss_target': 0}
SHARED_INPUTS = []
_WEIGHT_DTYPES = {'pre_norm_w': _jnp.float32, 'w_in': _jnp.float32, 'conv_w': _jnp.float32, 'conv_b': _jnp.float32, 'lru_w_a': _jnp.float32, 'lru_b_a': _jnp.float32, 'lru_w_x': _jnp.float32, 'lru_b_x': _jnp.float32, 'lru_lambda': _jnp.float32, 'hgrn_lb_logits': _jnp.float32, 'hgrn_gnorm_w': _jnp.float32, 'w_out': _jnp.float32, 'post_norm_w': _jnp.float32}
MOMENT_SCALE = {'pre_norm_w': 6.471270e-01, 'w_in': 2.279589e-01, 'conv_w': 2.685555e-01, 'conv_b': 8.999837e+00, 'lru_w_a': 1.010121e-01, 'lru_b_a': 8.509672e-02, 'lru_w_x': 1.855524e-01, 'lru_b_x': 9.479808e-02, 'lru_lambda': 1.533695e-01, 'hgrn_lb_logits': 3.001870e-02, 'hgrn_gnorm_w': 3.702705e-01, 'w_out': 4.208168e-01, 'post_norm_w': 6.405844e+01}


def _to_microbatches(a, axis):
    t = _jnp.moveaxis(a, axis, 0)
    t = t.reshape((N_MICROBATCH, t.shape[0] // N_MICROBATCH) + t.shape[1:])
    return _jnp.moveaxis(t, 1, axis + 1)


def setup_inputs(seed: int = 0) -> dict:
    inp = _fwd_setup_inputs(seed)
    key = _jax.random.fold_in(_jax.random.key(seed), 7919)
    shape, _ = _output_shape()
    out = dict(inp)
    out["loss_target"] = _jax.random.normal(_jax.random.fold_in(key, 0), shape, _jnp.float32)
    for i, name in enumerate(TWIN_WEIGHTS):
        w = inp[name].astype(_jnp.float32)
        if MOMENT_SCALE is None:
            s = _jnp.sqrt(_jnp.mean(_jnp.square(w)) + 1e-30)
        else:
            s = MOMENT_SCALE[name]
        km, kv = _jax.random.split(_jax.random.fold_in(key, i + 1))
        out[name] = w
        out["m_" + name] = s * _jax.random.normal(km, w.shape, _jnp.float32)
        out["v_" + name] = (s * s) * _jax.random.uniform(kv, w.shape, _jnp.float32, 0.5, 1.5)
    if N_MICROBATCH > 1:
        for name, axis in PER_EXAMPLE_BATCH_AXIS.items():
            out[name] = _to_microbatches(out[name], axis)
    return {'x': out['x'], 'pre_norm_w': out['pre_norm_w'], 'w_in': out['w_in'], 'conv_w': out['conv_w'], 'conv_b': out['conv_b'], 'lru_w_a': out['lru_w_a'], 'lru_b_a': out['lru_b_a'], 'lru_w_x': out['lru_w_x'], 'lru_b_x': out['lru_b_x'], 'lru_lambda': out['lru_lambda'], 'hgrn_lb_logits': out['hgrn_lb_logits'], 'hgrn_gnorm_w': out['hgrn_gnorm_w'], 'w_out': out['w_out'], 'post_norm_w': out['post_norm_w'], 'loss_target': out['loss_target'], 'm_pre_norm_w': out['m_pre_norm_w'], 'm_w_in': out['m_w_in'], 'm_conv_w': out['m_conv_w'], 'm_conv_b': out['m_conv_b'], 'm_lru_w_a': out['m_lru_w_a'], 'm_lru_b_a': out['m_lru_b_a'], 'm_lru_w_x': out['m_lru_w_x'], 'm_lru_b_x': out['m_lru_b_x'], 'm_lru_lambda': out['m_lru_lambda'], 'm_hgrn_lb_logits': out['m_hgrn_lb_logits'], 'm_hgrn_gnorm_w': out['m_hgrn_gnorm_w'], 'm_w_out': out['m_w_out'], 'm_post_norm_w': out['m_post_norm_w'], 'v_pre_norm_w': out['v_pre_norm_w'], 'v_w_in': out['v_w_in'], 'v_conv_w': out['v_conv_w'], 'v_conv_b': out['v_conv_b'], 'v_lru_w_a': out['v_lru_w_a'], 'v_lru_b_a': out['v_lru_b_a'], 'v_lru_w_x': out['v_lru_w_x'], 'v_lru_b_x': out['v_lru_b_x'], 'v_lru_lambda': out['v_lru_lambda'], 'v_hgrn_lb_logits': out['v_hgrn_lb_logits'], 'v_hgrn_gnorm_w': out['v_hgrn_gnorm_w'], 'v_w_out': out['v_w_out'], 'v_post_norm_w': out['v_post_norm_w']}


def _loss(weights, diff, rest, loss_target):
    with _jax.named_scope("forward"):
        args = {**rest, TWIN_DIFF_INPUT: diff, **{k: w.astype(_WEIGHT_DTYPES[k]) for k, w in weights.items()}}
        y = _forward(args)
    with _jax.named_scope("loss_head"):
        err = _jnp.square(y.astype(_jnp.float32) - loss_target)
        return 0.5 * _jnp.sum(_jnp.mean(err, axis=-1)) if err.ndim else 0.5 * err


def _adamw(w, g, m, v):
    m = ADAM_B1 * m + (1.0 - ADAM_B1) * g
    v = ADAM_B2 * v + (1.0 - ADAM_B2) * _jnp.square(g)
    m_hat = m / (1.0 - ADAM_B1 ** ADAM_STEP)
    v_hat = v / (1.0 - ADAM_B2 ** ADAM_STEP)
    delta = -ADAM_LR * (m_hat / (_jnp.sqrt(v_hat) + ADAM_EPS) + ADAM_WD * w)
    return delta, m, v


def reference(x, pre_norm_w, w_in, conv_w, conv_b, lru_w_a, lru_b_a, lru_w_x, lru_b_x, lru_lambda, hgrn_lb_logits, hgrn_gnorm_w, w_out, post_norm_w, loss_target, m_pre_norm_w, m_w_in, m_conv_w, m_conv_b, m_lru_w_a, m_lru_b_a, m_lru_w_x, m_lru_b_x, m_lru_lambda, m_hgrn_lb_logits, m_hgrn_gnorm_w, m_w_out, m_post_norm_w, v_pre_norm_w, v_w_in, v_conv_w, v_conv_b, v_lru_w_a, v_lru_b_a, v_lru_w_x, v_lru_b_x, v_lru_lambda, v_hgrn_lb_logits, v_hgrn_gnorm_w, v_w_out, v_post_norm_w):
    given = dict(x=x, pre_norm_w=pre_norm_w, w_in=w_in, conv_w=conv_w, conv_b=conv_b, lru_w_a=lru_w_a, lru_b_a=lru_b_a, lru_w_x=lru_w_x, lru_b_x=lru_b_x, lru_lambda=lru_lambda, hgrn_lb_logits=hgrn_lb_logits, hgrn_gnorm_w=hgrn_gnorm_w, w_out=w_out, post_norm_w=post_norm_w, loss_target=loss_target, m_pre_norm_w=m_pre_norm_w, m_w_in=m_w_in, m_conv_w=m_conv_w, m_conv_b=m_conv_b, m_lru_w_a=m_lru_w_a, m_lru_b_a=m_lru_b_a, m_lru_w_x=m_lru_w_x, m_lru_b_x=m_lru_b_x, m_lru_lambda=m_lru_lambda, m_hgrn_lb_logits=m_hgrn_lb_logits, m_hgrn_gnorm_w=m_hgrn_gnorm_w, m_w_out=m_w_out, m_post_norm_w=m_post_norm_w, v_pre_norm_w=v_pre_norm_w, v_w_in=v_w_in, v_conv_w=v_conv_w, v_conv_b=v_conv_b, v_lru_w_a=v_lru_w_a, v_lru_b_a=v_lru_b_a, v_lru_w_x=v_lru_w_x, v_lru_b_x=v_lru_b_x, v_lru_lambda=v_lru_lambda, v_hgrn_lb_logits=v_hgrn_lb_logits, v_hgrn_gnorm_w=v_hgrn_gnorm_w, v_w_out=v_w_out, v_post_norm_w=v_post_norm_w)
    weights = {n: given[n] for n in TWIN_WEIGHTS}
    shared = {n: given[n] for n in SHARED_INPUTS}
    per_example = {n: given[n] for n in ['x']}
    grad_fn = _jax.value_and_grad(_loss, argnums=(0, 1))

    def one_microbatch(ex, loss_target):
        ex = dict(ex)
        diff = ex.pop(TWIN_DIFF_INPUT)
        return grad_fn(weights, diff, {**shared, **ex}, loss_target)

    if N_MICROBATCH == 1:
        loss, (grad_w, grad_x) = one_microbatch(per_example, given["loss_target"])
    else:
        def body(carry, xs):
            loss_sum, grad_sum = carry
            l_k, (gw_k, gx_k) = one_microbatch(xs[0], xs[1])
            with _jax.named_scope("update"):
                return (loss_sum + l_k, _jax.tree.map(_jnp.add, grad_sum, gw_k)), gx_k

        init = (_jnp.zeros((), _jnp.float32), _jax.tree.map(_jnp.zeros_like, weights))
        (loss, grad_w), grad_x = _jax.lax.scan(body, init, (per_example, given["loss_target"]))
    with _jax.named_scope("update"):
        delta_w, new_m, new_v = {}, {}, {}
        for n in TWIN_WEIGHTS:
            delta_w[n], new_m[n], new_v[n] = _adamw(weights[n], grad_w[n], given["m_" + n], given["v_" + n])
    return (loss, grad_x, *[grad_w[n] for n in TWIN_WEIGHTS], *[delta_w[n] for n in TWIN_WEIGHTS],
            *[new_m[n] for n in TWIN_WEIGHTS], *[new_v[n] for n in TWIN_WEIGHTS])
```

```python
import functools

import jax
import jax.numpy as jnp
from jax import lax
from jax.experimental import pallas as pl
from jax.experimental.pallas import tpu as pltpu

F32 = jnp.float32
BF16 = jnp.bfloat16
SDS = jax.ShapeDtypeStruct

D_MODEL = 1024
D_IN = 6144
N_DEV = 8
W_BLK = D_IN // N_DEV
D_MIX = 2048
LRU_BLOCKS = 4
LRU_BW = 256
LRU_C = 8.0
N_HEADS = 8
HEAD_D = 128
CHUNK = 128
SUB = 32
N_SUB = CHUNK // SUB
EXP_CLAMP = 80.0
EPS = 1e-6

ADAM_LR = 0.001
ADAM_B1 = 0.9
ADAM_B2 = 0.999
ADAM_EPS = 1e-08
ADAM_WD = 0.01
ADAM_STEP = 10

VMEM_LIMIT = 56 * 1024 * 1024

NN = (((1,), (0,)), ((), ()))
NT = (((1,), (1,)), ((), ()))
TN = (((0,), (0,)), ((), ()))

SM_WA = 0
SM_WX = 256
SM_CW = 512
SM_BA = 520
SM_BX = 528
SM_ROWS = 536

RP_PRE, RP_CB, RP_LAM, RP_LB0, RP_LB1, RP_GN, RP_POST, RP_LOSS = range(8)


def _dot(a, b, dims=NN):
    return lax.dot_general(a, b, dims, preferred_element_type=F32)


def _sigmoid(x):
    return 1.0 / (1.0 + jnp.exp(-x))


def _cparams(sem, vmem=VMEM_LIMIT):
    return pltpu.CompilerParams(dimension_semantics=sem, vmem_limit_bytes=vmem)


def _iota(shape, axis):
    return lax.broadcasted_iota(jnp.int32, shape, axis)


def _in_proj(x, pre_w, w_all, seq):
    tm = min(1024, seq)

    def body(x_ref, pw_ref, w_ref, p_ref, u_ref):
        @pl.when(pl.program_id(1) == 0)
        def _():
            xv = x_ref[...]
            r = lax.rsqrt(jnp.mean(xv * xv, axis=-1, keepdims=True) + EPS)
            u_ref[...] = (xv * r * pw_ref[...]).astype(BF16)

        p_ref[...] = _dot(u_ref[...], w_ref[0])

    return pl.pallas_call(
        body, name="in_proj", grid=(seq // tm, N_DEV),
        in_specs=[pl.BlockSpec((tm, D_MODEL), lambda m, n: (m, 0)),
                  pl.BlockSpec((1, D_MODEL), lambda m, n: (0, 0)),
                  pl.BlockSpec((1, D_MODEL, W_BLK), lambda m, n: (n, 0, 0))],
        out_specs=[pl.BlockSpec((tm, W_BLK), lambda m, n: (m, n)),
                   pl.BlockSpec((tm, D_MODEL), lambda m, n: (m, 0))],
        out_shape=[SDS((seq, D_IN), F32), SDS((seq, D_MODEL), BF16)],
        compiler_params=_cparams(("parallel", "arbitrary")),
    )(x, pre_w, w_all)


def _softplus_neg(lam):
    z = -lam
    e = jnp.exp(-jnp.abs(z))
    u = 1.0 + e
    log1p_e = jnp.where(u == 1.0, e, jnp.log(u) * (e / (u - 1.0)))
    sp = jnp.maximum(z, 0.0) + log1p_e
    dsp = -jnp.where(z >= 0.0, 1.0 / u, e / u)
    return sp, dsp


def _neg_expm1(x):
    poly = x * (1.0 + x * (1.0 / 2 + x * (1.0 / 6 + x * (1.0 / 24 + x * (1.0 / 120 + x * (
        1.0 / 720 + x * (1.0 / 5040 + x * (1.0 / 40320))))))))
    return jnp.where(x > -0.5, -poly, 1.0 - jnp.exp(x))


def _conv_taps(lx, prev8, cw_ref, cb_ref, tile):
    xc = cb_ref[...] + cw_ref[3:4, :] * lx
    for j in (1, 2, 3):
        xc = xc + cw_ref[3 - j:4 - j, :] * pltpu.roll(lx, j, 0)
    row8 = _iota((8, D_MODEL), 0)
    last8 = lx[tile - 8:tile, :]
    fix = jnp.zeros((8, D_MODEL), F32)
    for j in (1, 2, 3):
        wrong = pltpu.roll(last8, j, 0)
        right = pltpu.roll(prev8, j, 0)
        fix = fix + cw_ref[3 - j:4 - j, :] * jnp.where(row8 < j, right - wrong, 0.0)
    return xc, fix


def _lru_gates(xcs, wa, wx, ba, bx, sp):
    xb = xcs.astype(BF16)
    r = _sigmoid(_dot(xb, wa) + ba)
    i = _sigmoid(_dot(xb, wx) + bx)
    la = (-LRU_C * sp) * r
    a = jnp.exp(la)
    mult = jnp.sqrt(_neg_expm1(2.0 * la))
    return r, i, a, mult


def _lru_forward(p, conv_w, conv_b, wa, wx, ba, bx, lam, seq):
    tile = min(512, seq // 2)
    nblk = tile // 8

    def body(lx_ref, gt_ref, cw_ref, cb_ref, wa_ref, wx_ref, ba_ref, bx_ref, lam_ref,
             h_ref, y_ref, prev8, hcar, xc_scr, a_scr, u_scr):
        @pl.when(pl.program_id(0) == 0)
        def _():
            prev8[...] = jnp.zeros_like(prev8)
            hcar[...] = jnp.zeros_like(hcar)

        lx = lx_ref[...]
        xc, fix = _conv_taps(lx, prev8[...], cw_ref, cb_ref, tile)
        xc_scr[...] = xc
        xc_scr[0:8, :] = xc_scr[0:8, :] + fix
        prev8[...] = lx_ref[tile - 8:tile, :]
        sp, _ = _softplus_neg(lam_ref[...])
        for n in range(LRU_BLOCKS):
            sl = slice(n * LRU_BW, (n + 1) * LRU_BW)
            xcs = xc_scr[:, sl]
            _, i, a, mult = _lru_gates(xcs, wa_ref[n], wx_ref[n], ba_ref[:, sl], bx_ref[:, sl], sp[:, sl])
            a_scr[:, sl] = a
            u_scr[:, sl] = mult * (i * xcs)

        row8 = _iota((8, D_MODEL), 0)

        def blk(j, hc):
            off = pl.multiple_of(j * 8, 8)
            a = a_scr[pl.ds(off, 8), :]
            u = u_scr[pl.ds(off, 8), :]
            for k in (1, 2, 4):
                m = row8 >= k
                u = jnp.where(m, u + a * pltpu.roll(u, k, 0), u)
                a = jnp.where(m, a * pltpu.roll(a, k, 0), a)
            h = u + a * hc
            h_ref[pl.ds(off, 8), :] = h
            return jnp.broadcast_to(h[7:8, :], (8, D_MODEL))

        hcar[...] = lax.fori_loop(0, nblk, blk, hcar[...])
        g = gt_ref[...]
        y_ref[...] = (h_ref[...] * (g * _sigmoid(g))).astype(BF16)

    full = lambda shape: pl.BlockSpec(shape, lambda t: (0,) * len(shape))
    return pl.pallas_call(
        body, name="lru_fwd", grid=(seq // tile,),
        in_specs=[pl.BlockSpec((tile, D_MODEL), lambda t: (t, 0)),
                  pl.BlockSpec((tile, D_MODEL), lambda t: (t, 1)),
                  full((4, D_MODEL)), full((1, D_MODEL)),
                  full((LRU_BLOCKS, LRU_BW, LRU_BW)), full((LRU_BLOCKS, LRU_BW, LRU_BW)),
                  full((1, D_MODEL)), full((1, D_MODEL)), full((1, D_MODEL))],
        out_specs=[pl.BlockSpec((tile, D_MODEL), lambda t: (t, 0)),
                   pl.BlockSpec((tile, D_MODEL), lambda t: (t, 0))],
        out_shape=[SDS((seq, D_MODEL), F32), SDS((seq, D_MODEL), BF16)],
        scratch_shapes=[pltpu.VMEM((8, D_MODEL), F32), pltpu.VMEM((8, D_MODEL), F32),
                        pltpu.VMEM((tile, D_MODEL), F32), pltpu.VMEM((tile, D_MODEL), F32),
                        pltpu.VMEM((tile, D_MODEL), F32)],
        compiler_params=_cparams(("arbitrary",)),
    )(p, p, conv_w, conv_b, wa, wx, ba, bx, lam)


def _lru_backward(p, h, dymix, conv_w, conv_b, wa, wx, ba, bx, lam, seq):
    tile = min(256, seq // 2)
    nt = seq // tile
    nblk = tile // 8
    t8 = tile // 8

    def body(lx_ref, lxh_ref, gt_ref, h_ref, hh_ref, dy_ref, cw_ref, cb_ref, wa_ref, wx_ref, ba_ref,
             bx_ref, lam_ref, dp_ref, gwa_ref, gwx_ref, gsm_ref,
             lamcar, anext, dxc8, xc_scr, r_scr, i_scr, a_scr, m_scr, c_scr, l_scr, dxc_scr):
        step = pl.program_id(0)
        first_tile = step == nt - 1

        @pl.when(step == 0)
        def _():
            lamcar[...] = jnp.zeros_like(lamcar)
            anext[...] = jnp.zeros_like(anext)
            dxc8[...] = jnp.zeros_like(dxc8)
            gwa_ref[...] = jnp.zeros_like(gwa_ref)
            gwx_ref[...] = jnp.zeros_like(gwx_ref)
            gsm_ref[...] = jnp.zeros_like(gsm_ref)

        keep = jnp.where(first_tile, 0.0, 1.0)
        lx = lx_ref[...]
        prev8 = lxh_ref[...] * keep
        xc, fix = _conv_taps(lx, prev8, cw_ref, cb_ref, tile)
        xc_scr[...] = xc
        xc_scr[0:8, :] = xc_scr[0:8, :] + fix
        sp, dsp = _softplus_neg(lam_ref[...])
        for n in range(LRU_BLOCKS):
            sl = slice(n * LRU_BW, (n + 1) * LRU_BW)
            r, i, a, mult = _lru_gates(xc_scr[:, sl], wa_ref[n], wx_ref[n], ba_ref[:, sl], bx_ref[:, sl],
                                       sp[:, sl])
            r_scr[:, sl] = r
            i_scr[:, sl] = i
            a_scr[:, sl] = a
            m_scr[:, sl] = mult

        g = gt_ref[...]
        sg = _sigmoid(g)
        dy = dy_ref[...]
        hv = h_ref[...]
        l_scr[...] = dy * (g * sg)
        dp_ref[:, D_MODEL:2 * D_MODEL] = (dy * hv * (sg * (1.0 + g * (1.0 - sg)))).astype(BF16)

        rowt = _iota((tile, D_MODEL), 0)
        av = a_scr[...]
        c_scr[...] = jnp.where(rowt == tile - 1, anext[...][0:1, :], pltpu.roll(av, tile - 1, 0))
        anext[...] = jnp.broadcast_to(av[0:1, :], (8, D_MODEL))
        row8 = _iota((8, D_MODEL), 0)

        def blk(jj, lc):
            off = pl.multiple_of((nblk - 1 - jj) * 8, 8)
            c = c_scr[pl.ds(off, 8), :]
            u = l_scr[pl.ds(off, 8), :]
            for k in (1, 2, 4):
                m = row8 < 8 - k
                u = jnp.where(m, u + c * pltpu.roll(u, 8 - k, 0), u)
                c = jnp.where(m, c * pltpu.roll(c, 8 - k, 0), c)
            lamv = u + c * lc
            l_scr[pl.ds(off, 8), :] = lamv
            return jnp.broadcast_to(lamv[0:1, :], (8, D_MODEL))

        lamcar[...] = lax.fori_loop(0, nblk, blk, lamcar[...])

        hprev = jnp.where(rowt == 0, hh_ref[...][7:8, :] * keep, pltpu.roll(hv, 1, 0))
        dsum_la = jnp.zeros((1, D_MODEL), F32)
        for n in range(LRU_BLOCKS):
            sl = slice(n * LRU_BW, (n + 1) * LRU_BW)
            lamv = l_scr[:, sl]
            xcs = xc_scr[:, sl]
            r = r_scr[:, sl]
            i = i_scr[:, sl]
            a = a_scr[:, sl]
            mult = m_scr[:, sl]
            d_la = lamv * hprev[:, sl] * a - (lamv * i * xcs) * (a * a / mult)
            d_pr = d_la * (-LRU_C * sp[:, sl]) * r * (1.0 - r)
            d_pi = (lamv * mult * xcs) * i * (1.0 - i)
            gsm_ref[7:8, sl] += jnp.sum(d_la * r, axis=0, keepdims=True) * (-LRU_C) * dsp[:, sl]
            gsm_ref[5:6, sl] += jnp.sum(d_pr, axis=0, keepdims=True)
            gsm_ref[6:7, sl] += jnp.sum(d_pi, axis=0, keepdims=True)
            xb = xcs.astype(BF16)
            prb = d_pr.astype(BF16)
            pib = d_pi.astype(BF16)
            gwa_ref[n] += _dot(xb, prb, TN)
            gwx_ref[n] += _dot(xb, pib, TN)
            dxc_scr[:, sl] = lamv * mult * i + _dot(prb, wa_ref[n], NT) + _dot(pib, wx_ref[n], NT)

        dxc = dxc_scr[...]
        gsm_ref[4:5, :] += jnp.sum(dxc, axis=0, keepdims=True)
        last8 = lx[tile - 8:tile, :]
        first8 = dxc[0:8, :]
        dlx = cw_ref[3:4, :] * dxc
        gsm_ref[3:4, :] += jnp.sum(dxc * lx, axis=0, keepdims=True)
        fix = jnp.zeros((8, D_MODEL), F32)
        for j in (1, 2, 3):
            w = cw_ref[3 - j:4 - j, :]
            dlx = dlx + w * pltpu.roll(dxc, tile - j, 0)
            fix = fix + w * jnp.where(row8 + j >= 8,
                                      pltpu.roll(dxc8[...], 8 - j, 0) - pltpu.roll(first8, 8 - j, 0), 0.0)
            halo = jnp.where(row8 < j, pltpu.roll(prev8, j, 0) - pltpu.roll(last8, j, 0), 0.0)
            gsm_ref[3 - j:4 - j, :] += (jnp.sum(dxc * pltpu.roll(lx, j, 0), axis=0, keepdims=True)
                                        + jnp.sum(first8 * halo, axis=0, keepdims=True))
        dxc8[...] = first8
        dp_ref[:, 0:D_MODEL] = dlx.astype(BF16)
        top = tile - 8
        dp_ref[top:tile, 0:D_MODEL] = (dlx[top:tile, :] + fix).astype(BF16)

    rev = lambda t: (nt - 1 - t, 0)
    halo_idx = lambda t: (jnp.maximum((nt - 1 - t) * t8 - 1, 0), 0)
    full = lambda shape: pl.BlockSpec(shape, lambda t: (0,) * len(shape))
    big = lambda: pltpu.VMEM((tile, D_MODEL), F32)
    return pl.pallas_call(
        body, name="lru_bwd", grid=(nt,),
        in_specs=[pl.BlockSpec((tile, D_MODEL), rev),
                  pl.BlockSpec((8, D_MODEL), halo_idx),
                  pl.BlockSpec((tile, D_MODEL), lambda t: (nt - 1 - t, 1)),
                  pl.BlockSpec((tile, D_MODEL), rev),
                  pl.BlockSpec((8, D_MODEL), halo_idx),
                  pl.BlockSpec((tile, D_MODEL), rev),
                  full((4, D_MODEL)), full((1, D_MODEL)),
                  full((LRU_BLOCKS, LRU_BW, LRU_BW)), full((LRU_BLOCKS, LRU_BW, LRU_BW)),
                  full((1, D_MODEL)), full((1, D_MODEL)), full((1, D_MODEL))],
        out_specs=[pl.BlockSpec((tile, 2 * D_MODEL), rev),
                   full((LRU_BLOCKS, LRU_BW, LRU_BW)), full((LRU_BLOCKS, LRU_BW, LRU_BW)),
                   full((8, D_MODEL))],
        out_shape=[SDS((seq, 2 * D_MODEL), BF16), SDS((LRU_BLOCKS, LRU_BW, LRU_BW), F32),
                   SDS((LRU_BLOCKS, LRU_BW, LRU_BW), F32), SDS((8, D_MODEL), F32)],
        scratch_shapes=[pltpu.VMEM((8, D_MODEL), F32), pltpu.VMEM((8, D_MODEL), F32),
                        pltpu.VMEM((8, D_MODEL), F32)] + [big() for _ in range(8)],
        compiler_params=_cparams(("arbitrary",)),
    )(p, p, p, h, h, dymix, conv_w, conv_b, wa, wx, ba, bx, lam)


def _split3(g):
    hi = g.astype(BF16)
    r1 = g - hi.astype(F32)
    mid = r1.astype(BF16)
    lo = (r1 - mid.astype(F32)).astype(BF16)
    return hi, mid, lo


def _tri_matmul(tri, g):
    hi, mid, lo = _split3(g)
    return _dot(tri, lo) + _dot(tri, mid) + _dot(tri, hi)


def _hgrn_gate_terms(q, fr, lbl):
    lb = _sigmoid(lbl[0:1, :] - lbl[1:2, :])
    sig = _sigmoid(fr)
    f = lb + (1.0 - lb) * sig
    sq = _sigmoid(q)
    return lb, sig, f, sq


def _hgrn_decay(bh):
    row = _iota((CHUNK, HEAD_D), 0)
    pieces = []
    for i in range(N_SUB):
        mid = 0.5 * (bh[i * SUB:i * SUB + 1, :] + bh[(i + 1) * SUB - 1:(i + 1) * SUB, :])
        pieces.append(jnp.broadcast_to(mid, (SUB, HEAD_D)))
    mu = jnp.concatenate(pieces, axis=0)
    eq = [jnp.exp(jnp.minimum(bh - mu, EXP_CLAMP))]
    ek = [jnp.exp(jnp.minimum(mu - bh, EXP_CLAMP))]
    for j in range(1, N_SUB):
        rho = bh[j * SUB - 1:j * SUB, :]
        eq.append(jnp.where(row >= j * SUB, jnp.exp(jnp.minimum(bh - rho, 0.0)), 0.0))
        ek.append(jnp.where((row >= (j - 1) * SUB) & (row < j * SUB),
                            jnp.exp(jnp.minimum(rho - bh, 0.0)), 0.0))
    blast = bh[CHUNK - 1:CHUNK, :]
    return eq, ek, jnp.exp(bh), jnp.exp(blast - bh), jnp.exp(blast)


def _hgrn_scores(qs, k, eq, ek):
    qt = jnp.concatenate([(qs * e).astype(BF16) for e in eq], axis=1)
    kt = jnp.concatenate([(k * e).astype(BF16) for e in ek], axis=1)
    r = _iota((CHUNK, CHUNK), 0)
    c = _iota((CHUNK, CHUNK), 1)
    diag = (jnp.right_shift(r, 5) == jnp.right_shift(c, 5)) & (c <= r)
    a = (jnp.where(diag, _dot(qt[:, :HEAD_D], kt[:, :HEAD_D], NT), 0.0)
         + _dot(qt[:, HEAD_D:], kt[:, HEAD_D:], NT))
    return a, qt, kt, diag


def _hgrn_forward(p, lbl, gw, seq):
    nc = seq // CHUNK
    assert SUB == 32

    def body(q_ref, f_ref, v_ref, hg_ref, lbl_ref, gw_ref, y_ref, o_ref, st_ref, s_scr):
        @pl.when(pl.program_id(1) == 0)
        def _():
            s_scr[...] = jnp.zeros_like(s_scr)

        q = q_ref[...]
        _, _, f, sq = _hgrn_gate_terms(q, f_ref[...], lbl_ref[...])
        qs = q * sq
        k = 1.0 - f
        r = _iota((CHUNK, CHUNK), 0)
        c = _iota((CHUNK, CHUNK), 1)
        tri = jnp.where(c <= r, 1.0, 0.0).astype(BF16)
        bh = _tri_matmul(tri, jnp.log(f))
        eq, ek, eb, ekst, ebl = _hgrn_decay(bh)
        a, _, _, _ = _hgrn_scores(qs, k, eq, ek)
        st = s_scr[...]
        st_ref[0, 0] = st
        vb = v_ref[...].astype(BF16)
        o = _dot(a.astype(BF16), vb) + _dot((qs * eb).astype(BF16), st.astype(BF16), NT)
        s_scr[...] = st * ebl + _dot(vb, (k * ekst).astype(BF16), TN)
        o_ref[...] = o
        rs = lax.rsqrt(jnp.mean(o * o, axis=-1, keepdims=True) + EPS)
        hg = hg_ref[...]
        y_ref[...] = ((o * rs * gw_ref[...]) * (hg * _sigmoid(hg))).astype(BF16)

    col = lambda base: pl.BlockSpec((CHUNK, HEAD_D), lambda h, c: (c, base + h))
    par = lambda rows: pl.BlockSpec((rows, HEAD_D), lambda h, c: (0, h))
    return pl.pallas_call(
        body, name="hgrn_fwd", grid=(N_HEADS, nc),
        in_specs=[col(16), col(24), col(32), col(40), par(2), par(1)],
        out_specs=[pl.BlockSpec((CHUNK, HEAD_D), lambda h, c: (c, h)),
                   pl.BlockSpec((CHUNK, HEAD_D), lambda h, c: (c, h)),
                   pl.BlockSpec((1, 1, HEAD_D, HEAD_D), lambda h, c: (c, h, 0, 0))],
        out_shape=[SDS((seq, D_MODEL), BF16), SDS((seq, D_MODEL), F32),
                   SDS((nc, N_HEADS, HEAD_D, HEAD_D), F32)],
        scratch_shapes=[pltpu.VMEM((HEAD_D, HEAD_D), F32)],
        compiler_params=_cparams(("parallel", "arbitrary")),
    )(p, p, p, p, lbl, gw)


def _hgrn_backward(p, o, states, dymix, lbl, gw, seq):
    nc = seq // CHUNK

    def body(q_ref, f_ref, v_ref, hg_ref, o_ref, st_ref, dy_ref, lbl_ref, gw_ref,
             dq_ref, df_ref, dv_ref, dhg_ref, gsm_ref, ds_scr):
        @pl.when(pl.program_id(1) == 0)
        def _():
            ds_scr[...] = jnp.zeros_like(ds_scr)
            gsm_ref[...] = jnp.zeros_like(gsm_ref)

        q = q_ref[...]
        lb, sig, f, sq = _hgrn_gate_terms(q, f_ref[...], lbl_ref[...])
        qs = q * sq
        k = 1.0 - f
        r = _iota((CHUNK, CHUNK), 0)
        c = _iota((CHUNK, CHUNK), 1)
        tri = jnp.where(c <= r, 1.0, 0.0).astype(BF16)
        bh = _tri_matmul(tri, jnp.log(f))
        eq, ek, eb, ekst, ebl = _hgrn_decay(bh)
        a, qt, kt, diag = _hgrn_scores(qs, k, eq, ek)

        o = o_ref[...]
        gwv = gw_ref[...]
        rs = lax.rsqrt(jnp.mean(o * o, axis=-1, keepdims=True) + EPS)
        on = o * rs
        hg = hg_ref[...]
        sh = _sigmoid(hg)
        dy = dy_ref[...]
        d_onw = dy * (hg * sh)
        dhg_ref[...] = (dy * (on * gwv) * (sh * (1.0 + hg * (1.0 - sh)))).astype(BF16)
        gsm_ref[1:2, :] += jnp.sum(d_onw * on, axis=0, keepdims=True)
        d_on = d_onw * gwv
        d_o = rs * (d_on - on * jnp.mean(d_on * on, axis=-1, keepdims=True))

        dob = d_o.astype(BF16)
        v = v_ref[...]
        vb = v.astype(BF16)
        st = st_ref[0, 0]
        stb = st.astype(BF16)
        dst = ds_scr[...]
        dstb = dst.astype(BF16)
        kst = k * ekst
        qin = qs * eb

        da = _dot(dob, vb, NT)
        dab = da.astype(BF16)
        da0b = jnp.where(diag, da, 0.0).astype(BF16)
        dv_ref[...] = (_dot(a.astype(BF16), dob, TN) + _dot(kst.astype(BF16), dstb, NT)).astype(BF16)

        g_all = [_dot(da0b, kt[:, :HEAD_D])]
        h_all = [_dot(da0b, qt[:, :HEAD_D], TN)]
        g_off = _dot(dab, kt[:, HEAD_D:])
        h_off = _dot(dab, qt[:, HEAD_D:], TN)
        for j in range(1, N_SUB):
            g_all.append(g_off[:, (j - 1) * HEAD_D:j * HEAD_D])
            h_all.append(h_off[:, (j - 1) * HEAD_D:j * HEAD_D])
        dq_inter = eb * _dot(dob, stb)
        d_kst = ekst * _dot(vb, dstb)
        d_q = dq_inter
        d_k = d_kst
        db = qs * dq_inter - k * d_kst
        for j in range(N_SUB):
            sl = slice(j * HEAD_D, (j + 1) * HEAD_D)
            d_q = d_q + eq[j] * g_all[j]
            d_k = d_k + ek[j] * h_all[j]
            db = db + (qt[:, sl].astype(F32) * g_all[j] - kt[:, sl].astype(F32) * h_all[j])

        db_last = jnp.sum(k * d_kst, axis=0, keepdims=True) + ebl * jnp.sum(st * dst, axis=0, keepdims=True)
        ds_scr[...] = dst * ebl + _dot(dob, qin.astype(BF16), TN)

        row = _iota((CHUNK, HEAD_D), 0)
        db = db + jnp.where(row == CHUNK - 1, db_last, 0.0)
        triu = jnp.where(c >= r, 1.0, 0.0).astype(BF16)
        dg = _tri_matmul(triu, db)
        d_f = dg / f - d_k
        df_ref[...] = (d_f * (1.0 - lb) * sig * (1.0 - sig)).astype(BF16)
        gsm_ref[0:1, :] += jnp.sum(d_f * (1.0 - sig), axis=0, keepdims=True) * (lb * (1.0 - lb))
        dq_ref[...] = (d_q * (sq * (1.0 + q * (1.0 - sq)))).astype(BF16)

    rc = lambda c: nc - 1 - c
    col = lambda base: pl.BlockSpec((CHUNK, HEAD_D), lambda h, c: (rc(c), base + h))
    par = lambda rows: pl.BlockSpec((rows, HEAD_D), lambda h, c: (0, h))
    return pl.pallas_call(
        body, name="hgrn_bwd", grid=(N_HEADS, nc),
        in_specs=[col(16), col(24), col(32), col(40), col(0),
                  pl.BlockSpec((1, 1, HEAD_D, HEAD_D), lambda h, c: (rc(c), h, 0, 0)),
                  col(8), par(2), par(1)],
        out_specs=[col(0), col(0), col(0), col(0), par(8)],
        out_shape=[SDS((seq, D_MODEL), BF16), SDS((seq, D_MODEL), BF16), SDS((seq, D_MODEL), BF16),
                   SDS((seq, D_MODEL), BF16), SDS((8, D_MODEL), F32)],
        scratch_shapes=[pltpu.VMEM((HEAD_D, HEAD_D), F32)],
        compiler_params=_cparams(("parallel", "arbitrary")),
    )(p, p, p, p, o, states, dymix, lbl, gw)


def _out_proj(yl, yh, wo, x, tgt, post_w, seq):
    tm = 256

    def body(yl_ref, yh_ref, wo_ref, x_ref, tg_ref, pw_ref, dymix_ref, dout_ref, gwo_ref, st_ref):
        @pl.when(pl.program_id(0) == 0)
        def _():
            gwo_ref[...] = jnp.zeros_like(gwo_ref)
            st_ref[...] = jnp.zeros_like(st_ref)

        ylv = yl_ref[...]
        yhv = yh_ref[...]
        y = _dot(ylv, wo_ref[0:D_MODEL, :]) + _dot(yhv, wo_ref[D_MODEL:D_MIX, :])
        r2 = lax.rsqrt(jnp.mean(y * y, axis=-1, keepdims=True) + EPS)
        yn = y * r2
        pw = pw_ref[...]
        e = (x_ref[...] + yn * pw) - tg_ref[...]
        st_ref[1:2, :] += jnp.sum(e * e, axis=0, keepdims=True) * (0.5 / D_MODEL)
        dout = e * (1.0 / D_MODEL)
        dout_ref[...] = dout
        st_ref[0:1, :] += jnp.sum(dout * yn, axis=0, keepdims=True)
        dyn = dout * pw
        dy = r2 * (dyn - yn * jnp.mean(dyn * yn, axis=-1, keepdims=True))
        dyb = dy.astype(BF16)
        dymix_ref[...] = _dot(dyb, wo_ref[...], NT)
        gwo_ref[0:D_MODEL, :] += _dot(ylv, dyb, TN)
        gwo_ref[D_MODEL:D_MIX, :] += _dot(yhv, dyb, TN)

    row = lambda w: pl.BlockSpec((tm, w), lambda m: (m, 0))
    full = lambda shape: pl.BlockSpec(shape, lambda m: (0,) * len(shape))
    return pl.pallas_call(
        body, name="out_proj", grid=(seq // tm,),
        in_specs=[row(D_MODEL), row(D_MODEL), full((D_MIX, D_MODEL)), row(D_MODEL), row(D_MODEL),
                  full((1, D_MODEL))],
        out_specs=[row(D_MIX), row(D_MODEL), full((D_MIX, D_MODEL)), full((8, D_MODEL))],
        out_shape=[SDS((seq, D_MIX), F32), SDS((seq, D_MODEL), F32), SDS((D_MIX, D_MODEL), F32),
                   SDS((8, D_MODEL), F32)],
        compiler_params=_cparams(("arbitrary",)),
    )(yl, yh, wo, x, tgt, post_w)


def _grad_w_in(u, dp, seq):
    tk = 512

    def body(u_ref, dp_ref, g_ref):
        @pl.when(pl.program_id(1) == 0)
        def _():
            g_ref[...] = jnp.zeros_like(g_ref)

        g_ref[0] += _dot(u_ref[...], dp_ref[...], TN)

    return pl.pallas_call(
        body, name="grad_w_in", grid=(N_DEV, seq // tk),
        in_specs=[pl.BlockSpec((tk, D_MODEL), lambda n, k: (k, 0)),
                  pl.BlockSpec((tk, W_BLK), lambda n, k: (k, n))],
        out_specs=pl.BlockSpec((1, D_MODEL, W_BLK), lambda n, k: (n, 0, 0)),
        out_shape=SDS((N_DEV, D_MODEL, W_BLK), F32),
        compiler_params=_cparams(("parallel", "arbitrary")),
    )(u, dp)


def _grad_x(dp, w_all, x, pre_w, dout, seq):
    tm = 512

    def body(dp_ref, w_ref, x_ref, pw_ref, do_ref, gx_ref, gpw_ref, acc):
        m = pl.program_id(0)
        j = pl.program_id(1)

        @pl.when((m == 0) & (j == 0))
        def _():
            gpw_ref[...] = jnp.zeros_like(gpw_ref)

        @pl.when(j == 0)
        def _():
            acc[...] = jnp.zeros_like(acc)

        acc[...] += _dot(dp_ref[...], w_ref[0], NT)

        @pl.when(j == N_DEV - 1)
        def _():
            xv = x_ref[...]
            r1 = lax.rsqrt(jnp.mean(xv * xv, axis=-1, keepdims=True) + EPS)
            xn = xv * r1
            du = acc[...]
            gpw_ref[0:1, :] += jnp.sum(du * xn, axis=0, keepdims=True)
            dxn = du * pw_ref[...]
            gx_ref[...] = r1 * (dxn - xn * jnp.mean(dxn * xn, axis=-1, keepdims=True)) + do_ref[...]

    return pl.pallas_call(
        body, name="grad_x", grid=(seq // tm, N_DEV),
        in_specs=[pl.BlockSpec((tm, W_BLK), lambda m, j: (m, j)),
                  pl.BlockSpec((1, D_MODEL, W_BLK), lambda m, j: (j, 0, 0)),
                  pl.BlockSpec((tm, D_MODEL), lambda m, j: (m, 0)),
                  pl.BlockSpec((1, D_MODEL), lambda m, j: (0, 0)),
                  pl.BlockSpec((tm, D_MODEL), lambda m, j: (m, 0))],
        out_specs=[pl.BlockSpec((tm, D_MODEL), lambda m, j: (m, 0)),
                   pl.BlockSpec((8, D_MODEL), lambda m, j: (0, 0))],
        out_shape=[SDS((seq, D_MODEL), F32), SDS((8, D_MODEL), F32)],
        scratch_shapes=[pltpu.VMEM((tm, D_MODEL), F32)],
        compiler_params=_cparams(("arbitrary", "arbitrary")),
    )(dp, w_all, x, pre_w, dout)


def _local_step(x, tgt, pre_w, w_in_all, conv_w, conv_b, wa, wx, ba, bx, lam, lbl, gnorm_w, w_out, post_w):
    seq = x.shape[0]
    p, u = _in_proj(x, pre_w, w_in_all, seq)
    h, y_lru = _lru_forward(p, conv_w, conv_b, wa, wx, ba, bx, lam, seq)
    y_hgrn, o, states = _hgrn_forward(p, lbl, gnorm_w, seq)
    dymix, dout, g_w_out, stats = _out_proj(y_lru, y_hgrn, w_out, x, tgt, post_w, seq)
    dp_lru, g_wa, g_wx, lru_small = _lru_backward(p, h, dymix, conv_w, conv_b, wa, wx, ba, bx, lam, seq)
    dq, dfr, dv, dhg, hgrn_small = _hgrn_backward(p, o, states, dymix, lbl, gnorm_w, seq)
    dp = jnp.concatenate([dp_lru, dq, dfr, dv, dhg], axis=1)
    g_w_in = _grad_w_in(u, dp, seq)
    grad_x, pre_small = _grad_x(dp, w_in_all, x, pre_w, dout, seq)
    return dict(grad_x=grad_x, g_w_in=g_w_in, g_w_out=g_w_out, g_wa=g_wa, g_wx=g_wx,
                lru_small=lru_small, hgrn_small=hgrn_small, pre_small=pre_small, stats=stats)


MESH = pl.DeviceIdType.MESH
ANY = pl.BlockSpec(memory_space=pl.ANY)


def _mesh_pos():
    return lax.axis_index("x"), lax.axis_index("y"), lax.axis_index("c")


def _all_gather(parts):
    n = len(parts)

    def body(*refs):
        ins, outs = refs[:n], refs[n:2 * n]
        send_sems, recv_sems, local_sems = refs[2 * n:]
        x, y, c = _mesh_pos()
        me, sibling = (x, y, c), (x, y, 1 - c)
        chips = [(1 - x, y), (x, 1 - y), (1 - x, 1 - y)]

        def slot(a, pos):
            return outs[a].at[4 * pos[0] + 2 * pos[1] + pos[2]]

        def copy(a, k, block, to, src=None):
            dst = slot(a, block)
            return pltpu.make_async_remote_copy(
                src_ref=dst if src is None else src, dst_ref=dst,
                send_sem=send_sems.at[a, k], recv_sem=recv_sems.at[a, k],
                device_id=to, device_id_type=MESH)

        mine = [pltpu.make_async_copy(ins[a], slot(a, me), local_sems.at[a]) for a in range(n)]
        for cp in mine:
            cp.start()
        first = []
        for a in range(n):
            first.append(copy(a, 0, me, sibling, src=ins[a]))
            first += [copy(a, 1 + j, me, (*chip, c), src=ins[a]) for j, chip in enumerate(chips)]
        for cp in first:
            cp.start()
        passed = []
        for j, chip in enumerate(chips):
            for a in range(n):
                copy(a, 1 + j, (*chip, c), me).wait_recv()
                fwd = copy(a, 4 + j, (*chip, c), sibling)
                fwd.start()
                passed.append(fwd)
        for a in range(n):
            copy(a, 0, sibling, me).wait_recv()
            for j, chip in enumerate(chips):
                copy(a, 4 + j, (*chip, 1 - c), me).wait_recv()
        for cp in first + passed:
            cp.wait_send()
        for cp in mine:
            cp.wait()

    return pl.pallas_call(
        body, name="gather_weights",
        out_shape=[SDS((N_DEV,) + p.shape, p.dtype) for p in parts],
        in_specs=[ANY] * n, out_specs=[ANY] * n,
        scratch_shapes=[pltpu.SemaphoreType.DMA((n, 7)), pltpu.SemaphoreType.DMA((n, 7)),
                        pltpu.SemaphoreType.DMA((n,))],
    )(*parts)


def _exchange_grads(blocks, repl):
    nb = len(blocks)
    n = nb + 1

    def body(*refs):
        ins, outs = refs[:n], refs[n:2 * n]
        send_sems, recv_sems, local_sems = refs[2 * n:]
        x, y, c = _mesh_pos()
        me = 4 * x + 2 * y + c

        def src_for(a, dest):
            return ins[a].at[dest] if a < nb else ins[a]

        local = [pltpu.make_async_copy(src_for(a, me), outs[a].at[me], local_sems.at[a]) for a in range(n)]
        for cp in local:
            cp.start()
        sends, recvs = [], []
        for k in range(1, N_DEV):
            px = 1 - x if (k >> 2) & 1 else x
            py = 1 - y if (k >> 1) & 1 else y
            pc = 1 - c if k & 1 else c
            peer = 4 * px + 2 * py + pc
            for a in range(n):
                cp = pltpu.make_async_remote_copy(
                    src_ref=src_for(a, peer), dst_ref=outs[a].at[me],
                    send_sem=send_sems.at[a, k - 1], recv_sem=recv_sems.at[a, k - 1],
                    device_id=(px, py, pc), device_id_type=MESH)
                cp.start()
                sends.append(cp)
                recvs.append(pltpu.make_async_remote_copy(
                    src_ref=outs[a].at[peer], dst_ref=outs[a].at[peer],
                    send_sem=send_sems.at[a, k - 1], recv_sem=recv_sems.at[a, k - 1],
                    device_id=(px, py, pc), device_id_type=MESH))
        for cp in recvs:
            cp.wait_recv()
        for cp in sends:
            cp.wait_send()
        for cp in local:
            cp.wait()

    arrs = list(blocks) + [repl]
    shapes = [SDS(b.shape, b.dtype) for b in blocks] + [SDS((N_DEV,) + repl.shape, repl.dtype)]
    return pl.pallas_call(
        body, name="exchange_grads", out_shape=shapes,
        in_specs=[ANY] * n, out_specs=[ANY] * n,
        scratch_shapes=[pltpu.SemaphoreType.DMA((n, 7)), pltpu.SemaphoreType.DMA((n, 7)),
                        pltpu.SemaphoreType.DMA((n,))],
    )(*arrs)


def _pack_rows(picks, name):
    arrs = [p[0] for p in picks]

    def body(*refs):
        out = refs[-1]
        out[...] = jnp.zeros_like(out)
        at = 0
        for ref, (_, row, rows, scale) in zip(refs[:-1], picks):
            out[at:at + rows, :] = ref[row:row + rows, :] * scale
            at += rows

    return pl.pallas_call(body, name=name, out_shape=SDS((8, D_MODEL), F32))(*arrs)


def _adamw(g, w, m, v):
    m2 = ADAM_B1 * m + (1.0 - ADAM_B1) * g
    v2 = ADAM_B2 * v + (1.0 - ADAM_B2) * (g * g)
    m_hat = m2 / (1.0 - ADAM_B1 ** ADAM_STEP)
    v_hat = v2 / (1.0 - ADAM_B2 ** ADAM_STEP)
    delta = -ADAM_LR * (m_hat / (jnp.sqrt(v_hat) + ADAM_EPS) + ADAM_WD * w)
    return delta, m2, v2


def _sum_slots(r_ref):
    g = r_ref[0]
    for s in range(1, N_DEV):
        g = g + r_ref[s]
    return g


def _sum_adamw(recv, w, m, v, tr, name):
    rows, cols = w.shape

    def body(r_ref, w_ref, m_ref, v_ref, g_ref, d_ref, m2_ref, v2_ref):
        g = _sum_slots(r_ref)
        g_ref[...] = g
        d_ref[...], m2_ref[...], v2_ref[...] = _adamw(g, w_ref[...], m_ref[...], v_ref[...])

    blk = pl.BlockSpec((tr, cols), lambda i: (i, 0))
    return pl.pallas_call(
        body, name=name, grid=(rows // tr,),
        in_specs=[pl.BlockSpec((N_DEV, tr, cols), lambda i: (0, i, 0)), blk, blk, blk],
        out_specs=[blk] * 4, out_shape=[SDS((rows, cols), F32)] * 4,
        compiler_params=_cparams(("parallel",)),
    )(recv, w, m, v)


def _sum_adamw_repl(recv, w, m, v):
    def body(r_ref, w_ref, m_ref, v_ref, g_ref, d_ref, m2_ref, v2_ref, loss_ref):
        g = _sum_slots(r_ref)
        g_ref[...] = g
        d_ref[...], m2_ref[...], v2_ref[...] = _adamw(g, w_ref[...], m_ref[...], v_ref[...])
        total = jnp.sum(g[RP_LOSS:RP_LOSS + 1, :], axis=-1, keepdims=True)
        loss_ref[...] = jnp.broadcast_to(total, loss_ref.shape)

    return pl.pallas_call(
        body, name="adamw_repl",
        out_shape=[SDS((8, D_MODEL), F32)] * 4 + [SDS((8, 128), F32)],
    )(recv, w, m, v)


def _shard_rows(t, lead):
    r = t.shape[1] // N_DEV
    t = t.reshape((lead, N_DEV, r) + t.shape[2:])
    return jnp.moveaxis(t, 1, 0)


def _pad8(t):
    return jnp.pad(t, ((0, 0), (0, 8 - t.shape[1]), (0, 0)))


def _pack_small(wa, wx, cw, b_a, b_x):
    n = wa.shape[0]
    return jnp.concatenate([
        wa.reshape(n, 256, 128), wx.reshape(n, 256, 128), _pad8(cw),
        _pad8(b_a.reshape(n, 1, 128)), _pad8(b_x.reshape(n, 1, 128))], axis=1)


def _unpack_small(t):
    n = t.shape[0]
    return (t[:, SM_WA:SM_WA + 256].reshape(n, 4, 32, 256), t[:, SM_WX:SM_WX + 256].reshape(n, 4, 32, 256),
            t[:, SM_CW:SM_CW + 4], t[:, SM_BA].reshape(n, 4, 32), t[:, SM_BX].reshape(n, 4, 32))


def kernel(x, pre_norm_w, w_in, conv_w, conv_b, lru_w_a, lru_b_a, lru_w_x, lru_b_x, lru_lambda, hgrn_lb_logits, hgrn_gnorm_w, w_out, post_norm_w, loss_target, m_pre_norm_w, m_w_in, m_conv_w, m_conv_b, m_lru_w_a, m_lru_b_a, m_lru_w_x, m_lru_b_x, m_lru_lambda, m_hgrn_lb_logits, m_hgrn_gnorm_w, m_w_out, m_post_norm_w, v_pre_norm_w, v_w_in, v_conv_w, v_conv_b, v_lru_w_a, v_lru_b_a, v_lru_w_x, v_lru_b_x, v_lru_lambda, v_hgrn_lb_logits, v_hgrn_gnorm_w, v_w_out, v_post_norm_w):
    seq = x.shape[1]
    x2 = x.reshape(seq, D_MODEL)
    tgt = loss_target.reshape(seq, D_MODEL)

    small_w = _pack_small(lru_w_a, lru_w_x, conv_w, lru_b_a, lru_b_x)[0]
    w_in_all, w_out_all, small_all = _all_gather([w_in[0].astype(BF16), w_out[0].astype(BF16), small_w])
    wa_s, wx_s, cw_s, ba_s, bx_s = _unpack_small(small_all)
    wa = jnp.moveaxis(wa_s, 0, 1).reshape(LRU_BLOCKS, LRU_BW, LRU_BW).astype(BF16)
    wx = jnp.moveaxis(wx_s, 0, 1).reshape(LRU_BLOCKS, LRU_BW, LRU_BW).astype(BF16)
    cw = jnp.moveaxis(cw_s, 0, 1).reshape(4, D_MODEL)
    ba = jnp.moveaxis(ba_s, 0, 1).reshape(1, D_MODEL)
    bx = jnp.moveaxis(bx_s, 0, 1).reshape(1, D_MODEL)

    loc = _local_step(x2, tgt, pre_norm_w, w_in_all, cw, conv_b, wa, wx, ba, bx, lru_lambda,
                      hgrn_lb_logits, hgrn_gnorm_w, w_out_all.reshape(D_MIX, D_MODEL), post_norm_w)

    ls = loc["lru_small"]
    g_small = _pack_small(_shard_rows(loc["g_wa"], LRU_BLOCKS), _shard_rows(loc["g_wx"], LRU_BLOCKS),
                          _shard_rows(ls[0:4].reshape(4, D_MODEL, 1), 4).reshape(N_DEV, 4, 128),
                          _shard_rows(ls[5].reshape(4, LRU_BW, 1), 4).reshape(N_DEV, 4, 32),
                          _shard_rows(ls[6].reshape(4, LRU_BW, 1), 4).reshape(N_DEV, 4, 32))
    g_repl = _pack_rows([(loc["pre_small"], 0, 1, 1.0), (ls, 4, 1, 1.0), (ls, 7, 1, 1.0),
                         (loc["hgrn_small"], 0, 1, 1.0), (loc["hgrn_small"], 0, 1, -1.0),
                         (loc["hgrn_small"], 1, 1, 1.0), (loc["stats"], 0, 2, 1.0)], "pack_grads")
    r_in, r_out, r_small, r_repl = _exchange_grads(
        [loc["g_w_in"], loc["g_w_out"].reshape(N_DEV, D_MIX // N_DEV, D_MODEL), g_small], g_repl)

    zero = jnp.zeros((1, D_MODEL), F32)
    pack_w = lambda a, b, c_, d, e, f, name: _pack_rows(
        [(a, 0, 1, 1.0), (b, 0, 1, 1.0), (c_, 0, 1, 1.0), (d, 0, 2, 1.0), (e, 0, 1, 1.0), (f, 0, 1, 1.0),
         (zero, 0, 1, 1.0)], name)
    w_repl = pack_w(pre_norm_w, conv_b, lru_lambda, hgrn_lb_logits, hgrn_gnorm_w, post_norm_w, "pack_w")
    m_repl = pack_w(m_pre_norm_w, m_conv_b, m_lru_lambda, m_hgrn_lb_logits, m_hgrn_gnorm_w, m_post_norm_w, "pack_m")
    v_repl = pack_w(v_pre_norm_w, v_conv_b, v_lru_lambda, v_hgrn_lb_logits, v_hgrn_gnorm_w, v_post_norm_w, "pack_v")
    o_repl = _sum_adamw_repl(r_repl, w_repl, m_repl, v_repl)
    loss = o_repl[4][0, 0]

    o_in = _sum_adamw(r_in, w_in[0], m_w_in[0], v_w_in[0], 128, "adamw_w_in")
    o_out = _sum_adamw(r_out, w_out[0], m_w_out[0], v_w_out[0], 64, "adamw_w_out")
    o_small = _sum_adamw(r_small,
                         _pack_small(lru_w_a, lru_w_x, conv_w, lru_b_a, lru_b_x)[0],
                         _pack_small(m_lru_w_a, m_lru_w_x, m_conv_w, m_lru_b_a, m_lru_b_x)[0],
                         _pack_small(v_lru_w_a, v_lru_w_x, v_conv_w, v_lru_b_a, v_lru_b_x)[0],
                         SM_ROWS, "adamw_small")

    outs = [loss, loc["grad_x"].reshape(x.shape)]
    for kind in range(4):
        rp = o_repl[kind]
        swa, swx, scw, sba, sbx = _unpack_small(o_small[kind][None])
        outs += [rp[RP_PRE:RP_PRE + 1], o_in[kind][None], scw, rp[RP_CB:RP_CB + 1], swa, sba, swx, sbx,
                 rp[RP_LAM:RP_LAM + 1], rp[RP_LB0:RP_LB1 + 1], rp[RP_GN:RP_GN + 1], o_out[kind][None],
                 rp[RP_POST:RP_POST + 1]]
    return tuple(outs)
```

```python
import functools

import jax
import jax.numpy as jnp
from jax import lax
from jax.experimental import pallas as pl
from jax.experimental.pallas import tpu as pltpu

F32 = jnp.float32
BF16 = jnp.bfloat16
SDS = jax.ShapeDtypeStruct

D_MODEL = 1024
D_IN = 6144
N_DEV = 8
W_BLK = D_IN // N_DEV
D_MIX = 2048
LRU_BLOCKS = 4
LRU_BW = 256
LRU_C = 8.0
N_HEADS = 8
HEAD_D = 128
CHUNK = 128
SUB = 32
N_SUB = CHUNK // SUB
EXP_CLAMP = 80.0
EPS = 1e-6

ADAM_LR = 0.001
ADAM_B1 = 0.9
ADAM_B2 = 0.999
ADAM_EPS = 1e-08
ADAM_WD = 0.01
ADAM_STEP = 10

VMEM_LIMIT = 56 * 1024 * 1024

NN = (((1,), (0,)), ((), ()))
NT = (((1,), (1,)), ((), ()))
TN = (((0,), (0,)), ((), ()))

SM_WA = 0
SM_WX = 256
SM_CW = 512
SM_BA = 520
SM_BX = 528
SM_ROWS = 536

RP_PRE, RP_CB, RP_LAM, RP_LB0, RP_LB1, RP_GN, RP_POST, RP_LOSS = range(8)


def _dot(a, b, dims=NN):
    return lax.dot_general(a, b, dims, preferred_element_type=F32)


def _sigmoid(x):
    return 1.0 / (1.0 + jnp.exp(-x))


def _cparams(sem, vmem=VMEM_LIMIT):
    return pltpu.CompilerParams(dimension_semantics=sem, vmem_limit_bytes=vmem)


def _iota(shape, axis):
    return lax.broadcasted_iota(jnp.int32, shape, axis)


def _in_proj(x, pre_w, w_all, seq):
    tm = min(1024, seq)

    def body(x_ref, pw_ref, w_ref, p_ref, u_ref):
        @pl.when(pl.program_id(1) == 0)
        def _():
            xv = x_ref[...]
            r = lax.rsqrt(jnp.mean(xv * xv, axis=-1, keepdims=True) + EPS)
            u_ref[...] = (xv * r * pw_ref[...]).astype(BF16)

        p_ref[...] = _dot(u_ref[...], w_ref[0])

    return pl.pallas_call(
        body, name="in_proj", grid=(seq // tm, N_DEV),
        in_specs=[pl.BlockSpec((tm, D_MODEL), lambda m, n: (m, 0)),
                  pl.BlockSpec((1, D_MODEL), lambda m, n: (0, 0)),
                  pl.BlockSpec((1, D_MODEL, W_BLK), lambda m, n: (n, 0, 0))],
        out_specs=[pl.BlockSpec((tm, W_BLK), lambda m, n: (m, n)),
                   pl.BlockSpec((tm, D_MODEL), lambda m, n: (m, 0))],
        out_shape=[SDS((seq, D_IN), F32), SDS((seq, D_MODEL), BF16)],
        compiler_params=_cparams(("parallel", "arbitrary")),
    )(x, pre_w, w_all)


def _softplus_neg(lam):
    z = -lam
    e = jnp.exp(-jnp.abs(z))
    u = 1.0 + e
    log1p_e = jnp.where(u == 1.0, e, jnp.log(u) * (e / (u - 1.0)))
    sp = jnp.maximum(z, 0.0) + log1p_e
    dsp = -jnp.where(z >= 0.0, 1.0 / u, e / u)
    return sp, dsp


def _neg_expm1(x):
    poly = x * (1.0 + x * (1.0 / 2 + x * (1.0 / 6 + x * (1.0 / 24 + x * (1.0 / 120 + x * (
        1.0 / 720 + x * (1.0 / 5040 + x * (1.0 / 40320))))))))
    return jnp.where(x > -0.5, -poly, 1.0 - jnp.exp(x))


def _conv_taps(lx, prev8, cw_ref, cb_ref, tile):
    xc = cb_ref[...] + cw_ref[3:4, :] * lx
    for j in (1, 2, 3):
        xc = xc + cw_ref[3 - j:4 - j, :] * pltpu.roll(lx, j, 0)
    row8 = _iota((8, D_MODEL), 0)
    last8 = lx[tile - 8:tile, :]
    fix = jnp.zeros((8, D_MODEL), F32)
    for j in (1, 2, 3):
        wrong = pltpu.roll(last8, j, 0)
        right = pltpu.roll(prev8, j, 0)
        fix = fix + cw_ref[3 - j:4 - j, :] * jnp.where(row8 < j, right - wrong, 0.0)
    return xc, fix


def _lru_gates(xcs, wa, wx, ba, bx, sp):
    xb = xcs.astype(BF16)
    r = _sigmoid(_dot(xb, wa) + ba)
    i = _sigmoid(_dot(xb, wx) + bx)
    la = (-LRU_C * sp) * r
    a = jnp.exp(la)
    mult = jnp.sqrt(_neg_expm1(2.0 * la))
    return r, i, a, mult


def _lru_forward(p, conv_w, conv_b, wa, wx, ba, bx, lam, seq):
    tile = min(512, seq // 2)
    nblk = tile // 8

    def body(lx_ref, gt_ref, cw_ref, cb_ref, wa_ref, wx_ref, ba_ref, bx_ref, lam_ref,
             h_ref, y_ref, prev8, hcar, xc_scr, a_scr, u_scr):
        @pl.when(pl.program_id(0) == 0)
        def _():
            prev8[...] = jnp.zeros_like(prev8)
            hcar[...] = jnp.zeros_like(hcar)

        lx = lx_ref[...]
        xc, fix = _conv_taps(lx, prev8[...], cw_ref, cb_ref, tile)
        xc_scr[...] = xc
        xc_scr[0:8, :] = xc_scr[0:8, :] + fix
        prev8[...] = lx_ref[tile - 8:tile, :]
        sp, _ = _softplus_neg(lam_ref[...])
        for n in range(LRU_BLOCKS):
            sl = slice(n * LRU_BW, (n + 1) * LRU_BW)
            xcs = xc_scr[:, sl]
            _, i, a, mult = _lru_gates(xcs, wa_ref[n], wx_ref[n], ba_ref[:, sl], bx_ref[:, sl], sp[:, sl])
            a_scr[:, sl] = a
            u_scr[:, sl] = mult * (i * xcs)

        row8 = _iota((8, D_MODEL), 0)

        def blk(j, hc):
            off = pl.multiple_of(j * 8, 8)
            a = a_scr[pl.ds(off, 8), :]
            u = u_scr[pl.ds(off, 8), :]
            for k in (1, 2, 4):
                m = row8 >= k
                u = jnp.where(m, u + a * pltpu.roll(u, k, 0), u)
                a = jnp.where(m, a * pltpu.roll(a, k, 0), a)
            h = u + a * hc
            h_ref[pl.ds(off, 8), :] = h
            return jnp.broadcast_to(h[7:8, :], (8, D_MODEL))

        hcar[...] = lax.fori_loop(0, nblk, blk, hcar[...])
        g = gt_ref[...]
        y_ref[...] = (h_ref[...] * (g * _sigmoid(g))).astype(BF16)

    full = lambda shape: pl.BlockSpec(shape, lambda t: (0,) * len(shape))
    return pl.pallas_call(
        body, name="lru_fwd", grid=(seq // tile,),
        in_specs=[pl.BlockSpec((tile, D_MODEL), lambda t: (t, 0)),
                  pl.BlockSpec((tile, D_MODEL), lambda t: (t, 1)),
                  full((4, D_MODEL)), full((1, D_MODEL)),
                  full((LRU_BLOCKS, LRU_BW, LRU_BW)), full((LRU_BLOCKS, LRU_BW, LRU_BW)),
                  full((1, D_MODEL)), full((1, D_MODEL)), full((1, D_MODEL))],
        out_specs=[pl.BlockSpec((tile, D_MODEL), lambda t: (t, 0)),
                   pl.BlockSpec((tile, D_MODEL), lambda t: (t, 0))],
        out_shape=[SDS((seq, D_MODEL), F32), SDS((seq, D_MODEL), BF16)],
        scratch_shapes=[pltpu.VMEM((8, D_MODEL), F32), pltpu.VMEM((8, D_MODEL), F32),
                        pltpu.VMEM((tile, D_MODEL), F32), pltpu.VMEM((tile, D_MODEL), F32),
                        pltpu.VMEM((tile, D_MODEL), F32)],
        compiler_params=_cparams(("arbitrary",)),
    )(p, p, conv_w, conv_b, wa, wx, ba, bx, lam)


def _lru_backward(p, h, dymix, conv_w, conv_b, wa, wx, ba, bx, lam, seq):
    tile = min(256, seq // 2)
    nt = seq // tile
    nblk = tile // 8
    t8 = tile // 8

    def body(lx_ref, lxh_ref, gt_ref, h_ref, hh_ref, dy_ref, cw_ref, cb_ref, wa_ref, wx_ref, ba_ref,
             bx_ref, lam_ref, dp_ref, gwa_ref, gwx_ref, gsm_ref,
             lamcar, anext, dxc8, xc_scr, r_scr, i_scr, a_scr, m_scr, c_scr, l_scr, dxc_scr):
        step = pl.program_id(0)
        first_tile = step == nt - 1

        @pl.when(step == 0)
        def _():
            lamcar[...] = jnp.zeros_like(lamcar)
            anext[...] = jnp.zeros_like(anext)
            dxc8[...] = jnp.zeros_like(dxc8)
            gwa_ref[...] = jnp.zeros_like(gwa_ref)
            gwx_ref[...] = jnp.zeros_like(gwx_ref)
            gsm_ref[...] = jnp.zeros_like(gsm_ref)

        keep = jnp.where(first_tile, 0.0, 1.0)
        lx = lx_ref[...]
        prev8 = lxh_ref[...] * keep
        xc, fix = _conv_taps(lx, prev8, cw_ref, cb_ref, tile)
        xc_scr[...] = xc
        xc_scr[0:8, :] = xc_scr[0:8, :] + fix
        sp, dsp = _softplus_neg(lam_ref[...])
        for n in range(LRU_BLOCKS):
            sl = slice(n * LRU_BW, (n + 1) * LRU_BW)
            r, i, a, mult = _lru_gates(xc_scr[:, sl], wa_ref[n], wx_ref[n], ba_ref[:, sl], bx_ref[:, sl],
                                       sp[:, sl])
            r_scr[:, sl] = r
            i_scr[:, sl] = i
            a_scr[:, sl] = a
            m_scr[:, sl] = mult

        g = gt_ref[...]
        sg = _sigmoid(g)
        dy = dy_ref[...]
        hv = h_ref[...]
        l_scr[...] = dy * (g * sg)
        dp_ref[:, D_MODEL:2 * D_MODEL] = (dy * hv * (sg * (1.0 + g * (1.0 - sg)))).astype(BF16)

        rowt = _iota((tile, D_MODEL), 0)
        av = a_scr[...]
        c_scr[...] = jnp.where(rowt == tile - 1, anext[...][0:1, :], pltpu.roll(av, tile - 1, 0))
        anext[...] = jnp.broadcast_to(av[0:1, :], (8, D_MODEL))
        row8 = _iota((8, D_MODEL), 0)

        def blk(jj, lc):
            off = pl.multiple_of((nblk - 1 - jj) * 8, 8)
            c = c_scr[pl.ds(off, 8), :]
            u = l_scr[pl.ds(off, 8), :]
            for k in (1, 2, 4):
                m = row8 < 8 - k
                u = jnp.where(m, u + c * pltpu.roll(u, 8 - k, 0), u)
                c = jnp.where(m, c * pltpu.roll(c, 8 - k, 0), c)
            lamv = u + c * lc
            l_scr[pl.ds(off, 8), :] = lamv
            return jnp.broadcast_to(lamv[0:1, :], (8, D_MODEL))

        lamcar[...] = lax.fori_loop(0, nblk, blk, lamcar[...])

        hprev = jnp.where(rowt == 0, hh_ref[...][7:8, :] * keep, pltpu.roll(hv, 1, 0))
        dsum_la = jnp.zeros((1, D_MODEL), F32)
        for n in range(LRU_BLOCKS):
            sl = slice(n * LRU_BW, (n + 1) * LRU_BW)
            lamv = l_scr[:, sl]
            xcs = xc_scr[:, sl]
            r = r_scr[:, sl]
            i = i_scr[:, sl]
            a = a_scr[:, sl]
            mult = m_scr[:, sl]
            d_la = lamv * hprev[:, sl] * a - (lamv * i * xcs) * (a * a / mult)
            d_pr = d_la * (-LRU_C * sp[:, sl]) * r * (1.0 - r)
            d_pi = (lamv * mult * xcs) * i * (1.0 - i)
            gsm_ref[7:8, sl] += jnp.sum(d_la * r, axis=0, keepdims=True) * (-LRU_C) * dsp[:, sl]
            gsm_ref[5:6, sl] += jnp.sum(d_pr, axis=0, keepdims=True)
            gsm_ref[6:7, sl] += jnp.sum(d_pi, axis=0, keepdims=True)
            xb = xcs.astype(BF16)
            prb = d_pr.astype(BF16)
            pib = d_pi.astype(BF16)
            gwa_ref[n] += _dot(xb, prb, TN)
            gwx_ref[n] += _dot(xb, pib, TN)
            dxc_scr[:, sl] = lamv * mult * i + _dot(prb, wa_ref[n], NT) + _dot(pib, wx_ref[n], NT)

        dxc = dxc_scr[...]
        gsm_ref[4:5, :] += jnp.sum(dxc, axis=0, keepdims=True)
        last8 = lx[tile - 8:tile, :]
        first8 = dxc[0:8, :]
        dlx = cw_ref[3:4, :] * dxc
        gsm_ref[3:4, :] += jnp.sum(dxc * lx, axis=0, keepdims=True)
        fix = jnp.zeros((8, D_MODEL), F32)
        for j in (1, 2, 3):
            w = cw_ref[3 - j:4 - j, :]
            dlx = dlx + w * pltpu.roll(dxc, tile - j, 0)
            fix = fix + w * jnp.where(row8 + j >= 8,
                                      pltpu.roll(dxc8[...], 8 - j, 0) - pltpu.roll(first8, 8 - j, 0), 0.0)
            halo = jnp.where(row8 < j, pltpu.roll(prev8, j, 0) - pltpu.roll(last8, j, 0), 0.0)
            gsm_ref[3 - j:4 - j, :] += (jnp.sum(dxc * pltpu.roll(lx, j, 0), axis=0, keepdims=True)
                                        + jnp.sum(first8 * halo, axis=0, keepdims=True))
        dxc8[...] = first8
        dp_ref[:, 0:D_MODEL] = dlx.astype(BF16)
        top = tile - 8
        dp_ref[top:tile, 0:D_MODEL] = (dlx[top:tile, :] + fix).astype(BF16)

    rev = lambda t: (nt - 1 - t, 0)
    halo_idx = lambda t: (jnp.maximum((nt - 1 - t) * t8 - 1, 0), 0)
    full = lambda shape: pl.BlockSpec(shape, lambda t: (0,) * len(shape))
    big = lambda: pltpu.VMEM((tile, D_MODEL), F32)
    return pl.pallas_call(
        body, name="lru_bwd", grid=(nt,),
        in_specs=[pl.BlockSpec((tile, D_MODEL), rev),
                  pl.BlockSpec((8, D_MODEL), halo_idx),
                  pl.BlockSpec((tile, D_MODEL), lambda t: (nt - 1 - t, 1)),
                  pl.BlockSpec((tile, D_MODEL), rev),
                  pl.BlockSpec((8, D_MODEL), halo_idx),
                  pl.BlockSpec((tile, D_MODEL), rev),
                  full((4, D_MODEL)), full((1, D_MODEL)),
                  full((LRU_BLOCKS, LRU_BW, LRU_BW)), full((LRU_BLOCKS, LRU_BW, LRU_BW)),
                  full((1, D_MODEL)), full((1, D_MODEL)), full((1, D_MODEL))],
        out_specs=[pl.BlockSpec((tile, 2 * D_MODEL), rev),
                   full((LRU_BLOCKS, LRU_BW, LRU_BW)), full((LRU_BLOCKS, LRU_BW, LRU_BW)),
                   full((8, D_MODEL))],
        out_shape=[SDS((seq, 2 * D_MODEL), BF16), SDS((LRU_BLOCKS, LRU_BW, LRU_BW), F32),
                   SDS((LRU_BLOCKS, LRU_BW, LRU_BW), F32), SDS((8, D_MODEL), F32)],
        scratch_shapes=[pltpu.VMEM((8, D_MODEL), F32), pltpu.VMEM((8, D_MODEL), F32),
                        pltpu.VMEM((8, D_MODEL), F32)] + [big() for _ in range(8)],
        compiler_params=_cparams(("arbitrary",)),
    )(p, p, p, h, h, dymix, conv_w, conv_b, wa, wx, ba, bx, lam)


def _split3(g):
    hi = g.astype(BF16)
    r1 = g - hi.astype(F32)
    mid = r1.astype(BF16)
    lo = (r1 - mid.astype(F32)).astype(BF16)
    return hi, mid, lo


def _tri_matmul(tri, g):
    hi, mid, lo = _split3(g)
    return _dot(tri, lo) + _dot(tri, mid) + _dot(tri, hi)


def _hgrn_gate_terms(q, fr, lbl):
    lb = _sigmoid(lbl[0:1, :] - lbl[1:2, :])
    sig = _sigmoid(fr)
    f = lb + (1.0 - lb) * sig
    sq = _sigmoid(q)
    return lb, sig, f, sq


def _hgrn_decay(bh):
    row = _iota((CHUNK, HEAD_D), 0)
    pieces = []
    for i in range(N_SUB):
        mid = 0.5 * (bh[i * SUB:i * SUB + 1, :] + bh[(i + 1) * SUB - 1:(i + 1) * SUB, :])
        pieces.append(jnp.broadcast_to(mid, (SUB, HEAD_D)))
    mu = jnp.concatenate(pieces, axis=0)
    eq = [jnp.exp(jnp.minimum(bh - mu, EXP_CLAMP))]
    ek = [jnp.exp(jnp.minimum(mu - bh, EXP_CLAMP))]
    for j in range(1, N_SUB):
        rho = bh[j * SUB - 1:j * SUB, :]
        eq.append(jnp.where(row >= j * SUB, jnp.exp(jnp.minimum(bh - rho, 0.0)), 0.0))
        ek.append(jnp.where((row >= (j - 1) * SUB) & (row < j * SUB),
                            jnp.exp(jnp.minimum(rho - bh, 0.0)), 0.0))
    blast = bh[CHUNK - 1:CHUNK, :]
    return eq, ek, jnp.exp(bh), jnp.exp(blast - bh), jnp.exp(blast)


def _hgrn_scores(qs, k, eq, ek):
    qt = jnp.concatenate([(qs * e).astype(BF16) for e in eq], axis=1)
    kt = jnp.concatenate([(k * e).astype(BF16) for e in ek], axis=1)
    r = _iota((CHUNK, CHUNK), 0)
    c = _iota((CHUNK, CHUNK), 1)
    diag = (jnp.right_shift(r, 5) == jnp.right_shift(c, 5)) & (c <= r)
    a = (jnp.where(diag, _dot(qt[:, :HEAD_D], kt[:, :HEAD_D], NT), 0.0)
         + _dot(qt[:, HEAD_D:], kt[:, HEAD_D:], NT))
    return a, qt, kt, diag


def _hgrn_forward(p, lbl, gw, seq):
    nc = seq // CHUNK
    assert SUB == 32

    def body(q_ref, f_ref, v_ref, hg_ref, lbl_ref, gw_ref, y_ref, o_ref, st_ref, s_scr):
        @pl.when(pl.program_id(0) == 0)
        def _():
            s_scr[...] = jnp.zeros_like(s_scr)

        r = _iota((CHUNK, CHUNK), 0)
        c = _iota((CHUNK, CHUNK), 1)
        tri = jnp.where(c <= r, 1.0, 0.0).astype(BF16)
        for h in range(N_HEADS):
            sl = slice(h * HEAD_D, (h + 1) * HEAD_D)
            q = q_ref[:, sl]
            _, _, f, sq = _hgrn_gate_terms(q, f_ref[:, sl], lbl_ref[:, sl])
            qs = q * sq
            k = 1.0 - f
            bh = _tri_matmul(tri, jnp.log(f))
            eq, ek, eb, ekst, ebl = _hgrn_decay(bh)
            a, _, _, _ = _hgrn_scores(qs, k, eq, ek)
            st = s_scr[h]
            st_ref[0, h] = st
            vb = v_ref[:, sl].astype(BF16)
            o = _dot(a.astype(BF16), vb) + _dot((qs * eb).astype(BF16), st.astype(BF16), NT)
            s_scr[h] = st * ebl + _dot(vb, (k * ekst).astype(BF16), TN)
            o_ref[:, sl] = o
            rs = lax.rsqrt(jnp.mean(o * o, axis=-1, keepdims=True) + EPS)
            hg = hg_ref[:, sl]
            y_ref[:, sl] = ((o * rs * gw_ref[:, sl]) * (hg * _sigmoid(hg))).astype(BF16)

    col = lambda j: pl.BlockSpec((CHUNK, D_MODEL), lambda c: (c, j))
    par = lambda rows: pl.BlockSpec((rows, D_MODEL), lambda c: (0, 0))
    return pl.pallas_call(
        body, name="hgrn_fwd", grid=(nc,),
        in_specs=[col(2), col(3), col(4), col(5), par(2), par(1)],
        out_specs=[col(0), col(0),
                   pl.BlockSpec((1, N_HEADS, HEAD_D, HEAD_D), lambda c: (c, 0, 0, 0))],
        out_shape=[SDS((seq, D_MODEL), BF16), SDS((seq, D_MODEL), F32),
                   SDS((nc, N_HEADS, HEAD_D, HEAD_D), F32)],
        scratch_shapes=[pltpu.VMEM((N_HEADS, HEAD_D, HEAD_D), F32)],
        compiler_params=_cparams(("arbitrary",)),
    )(p, p, p, p, lbl, gw)


def _hgrn_backward(p, o, states, dymix, lbl, gw, seq):
    nc = seq // CHUNK

    def body(q_ref, f_ref, v_ref, hg_ref, o_ref, st_ref, dy_ref, lbl_ref, gw_ref,
             dp_ref, gsm_ref, ds_scr):
        @pl.when(pl.program_id(0) == 0)
        def _():
            ds_scr[...] = jnp.zeros_like(ds_scr)
            gsm_ref[...] = jnp.zeros_like(gsm_ref)

        r = _iota((CHUNK, CHUNK), 0)
        c = _iota((CHUNK, CHUNK), 1)
        tri = jnp.where(c <= r, 1.0, 0.0).astype(BF16)
        triu = jnp.where(c >= r, 1.0, 0.0).astype(BF16)
        row = _iota((CHUNK, HEAD_D), 0)
        for h in range(N_HEADS):
            sl = slice(h * HEAD_D, (h + 1) * HEAD_D)
            out = lambda j: slice(j * D_MODEL + h * HEAD_D, j * D_MODEL + (h + 1) * HEAD_D)
            q = q_ref[:, sl]
            lb, sig, f, sq = _hgrn_gate_terms(q, f_ref[:, sl], lbl_ref[:, sl])
            qs = q * sq
            k = 1.0 - f
            bh = _tri_matmul(tri, jnp.log(f))
            eq, ek, eb, ekst, ebl = _hgrn_decay(bh)
            a, qt, kt, diag = _hgrn_scores(qs, k, eq, ek)

            o = o_ref[:, sl]
            gwv = gw_ref[:, sl]
            rs = lax.rsqrt(jnp.mean(o * o, axis=-1, keepdims=True) + EPS)
            on = o * rs
            hg = hg_ref[:, sl]
            sh = _sigmoid(hg)
            dy = dy_ref[:, sl]
            d_onw = dy * (hg * sh)
            dp_ref[:, out(3)] = (dy * (on * gwv) * (sh * (1.0 + hg * (1.0 - sh)))).astype(BF16)
            gsm_ref[1:2, sl] += jnp.sum(d_onw * on, axis=0, keepdims=True)
            d_on = d_onw * gwv
            d_o = rs * (d_on - on * jnp.mean(d_on * on, axis=-1, keepdims=True))

            dob = d_o.astype(BF16)
            vb = v_ref[:, sl].astype(BF16)
            st = st_ref[0, h]
            stb = st.astype(BF16)
            dst = ds_scr[h]
            dstb = dst.astype(BF16)
            kst = k * ekst
            qin = qs * eb

            da = _dot(dob, vb, NT)
            dab = da.astype(BF16)
            da0b = jnp.where(diag, da, 0.0).astype(BF16)
            dp_ref[:, out(2)] = (_dot(a.astype(BF16), dob, TN) + _dot(kst.astype(BF16), dstb, NT)).astype(BF16)

            g_all = [_dot(da0b, kt[:, :HEAD_D])]
            h_all = [_dot(da0b, qt[:, :HEAD_D], TN)]
            g_off = _dot(dab, kt[:, HEAD_D:])
            h_off = _dot(dab, qt[:, HEAD_D:], TN)
            for j in range(1, N_SUB):
                g_all.append(g_off[:, (j - 1) * HEAD_D:j * HEAD_D])
                h_all.append(h_off[:, (j - 1) * HEAD_D:j * HEAD_D])
            dq_inter = eb * _dot(dob, stb)
            d_kst = ekst * _dot(vb, dstb)
            d_q = dq_inter
            d_k = d_kst
            db = qs * dq_inter - k * d_kst
            for j in range(N_SUB):
                sj = slice(j * HEAD_D, (j + 1) * HEAD_D)
                d_q = d_q + eq[j] * g_all[j]
                d_k = d_k + ek[j] * h_all[j]
                db = db + (qt[:, sj].astype(F32) * g_all[j] - kt[:, sj].astype(F32) * h_all[j])

            db_last = (jnp.sum(k * d_kst, axis=0, keepdims=True)
                       + ebl * jnp.sum(st * dst, axis=0, keepdims=True))
            ds_scr[h] = dst * ebl + _dot(dob, qin.astype(BF16), TN)

            db = db + jnp.where(row == CHUNK - 1, db_last, 0.0)
            dg = _tri_matmul(triu, db)
            d_f = dg / f - d_k
            dp_ref[:, out(1)] = (d_f * (1.0 - lb) * sig * (1.0 - sig)).astype(BF16)
            gsm_ref[0:1, sl] += jnp.sum(d_f * (1.0 - sig), axis=0, keepdims=True) * (lb * (1.0 - lb))
            dp_ref[:, out(0)] = (d_q * (sq * (1.0 + q * (1.0 - sq)))).astype(BF16)

    rc = lambda c: nc - 1 - c
    col = lambda j: pl.BlockSpec((CHUNK, D_MODEL), lambda c: (rc(c), j))
    par = lambda rows: pl.BlockSpec((rows, D_MODEL), lambda c: (0, 0))
    return pl.pallas_call(
        body, name="hgrn_bwd", grid=(nc,),
        in_specs=[col(2), col(3), col(4), col(5), col(0),
                  pl.BlockSpec((1, N_HEADS, HEAD_D, HEAD_D), lambda c: (rc(c), 0, 0, 0)),
                  col(1), par(2), par(1)],
        out_specs=[pl.BlockSpec((CHUNK, 4 * D_MODEL), lambda c: (rc(c), 0)), par(8)],
        out_shape=[SDS((seq, 4 * D_MODEL), BF16), SDS((8, D_MODEL), F32)],
        scratch_shapes=[pltpu.VMEM((N_HEADS, HEAD_D, HEAD_D), F32)],
        compiler_params=_cparams(("arbitrary",)),
    )(p, p, p, p, o, states, dymix, lbl, gw)


def _out_proj(yl, yh, wo, x, tgt, post_w, seq):
    tm = 256

    def body(yl_ref, yh_ref, wo_ref, x_ref, tg_ref, pw_ref, dymix_ref, dout_ref, gwo_ref, st_ref):
        @pl.when(pl.program_id(0) == 0)
        def _():
            gwo_ref[...] = jnp.zeros_like(gwo_ref)
            st_ref[...] = jnp.zeros_like(st_ref)

        ylv = yl_ref[...]
        yhv = yh_ref[...]
        y = _dot(ylv, wo_ref[0:D_MODEL, :]) + _dot(yhv, wo_ref[D_MODEL:D_MIX, :])
        r2 = lax.rsqrt(jnp.mean(y * y, axis=-1, keepdims=True) + EPS)
        yn = y * r2
        pw = pw_ref[...]
        e = (x_ref[...] + yn * pw) - tg_ref[...]
        st_ref[1:2, :] += jnp.sum(e * e, axis=0, keepdims=True) * (0.5 / D_MODEL)
        dout = e * (1.0 / D_MODEL)
        dout_ref[...] = dout
        st_ref[0:1, :] += jnp.sum(dout * yn, axis=0, keepdims=True)
        dyn = dout * pw
        dy = r2 * (dyn - yn * jnp.mean(dyn * yn, axis=-1, keepdims=True))
        dyb = dy.astype(BF16)
        dymix_ref[...] = _dot(dyb, wo_ref[...], NT)
        gwo_ref[0:D_MODEL, :] += _dot(ylv, dyb, TN)
        gwo_ref[D_MODEL:D_MIX, :] += _dot(yhv, dyb, TN)

    row = lambda w: pl.BlockSpec((tm, w), lambda m: (m, 0))
    full = lambda shape: pl.BlockSpec(shape, lambda m: (0,) * len(shape))
    return pl.pallas_call(
        body, name="out_proj", grid=(seq // tm,),
        in_specs=[row(D_MODEL), row(D_MODEL), full((D_MIX, D_MODEL)), row(D_MODEL), row(D_MODEL),
                  full((1, D_MODEL))],
        out_specs=[row(D_MIX), row(D_MODEL), full((D_MIX, D_MODEL)), full((8, D_MODEL))],
        out_shape=[SDS((seq, D_MIX), F32), SDS((seq, D_MODEL), F32), SDS((D_MIX, D_MODEL), F32),
                   SDS((8, D_MODEL), F32)],
        compiler_params=_cparams(("arbitrary",)),
    )(yl, yh, wo, x, tgt, post_w)


MESH = pl.DeviceIdType.MESH
ANY = pl.BlockSpec(memory_space=pl.ANY)
EXCHANGE_SEMS = [pltpu.SemaphoreType.DMA((N_DEV - 1,)), pltpu.SemaphoreType.DMA((N_DEV - 1,)),
                 pltpu.SemaphoreType.DMA(())]


def _mesh_pos():
    return lax.axis_index("x"), lax.axis_index("y"), lax.axis_index("c")


class _SlotExchange:
    def __init__(self, src_ref, dst_ref, send_sems, recv_sems, local_sem, blocked):
        x, y, c = _mesh_pos()
        me = 4 * x + 2 * y + c
        src = (lambda dest: src_ref.at[dest]) if blocked else (lambda dest: src_ref)
        self.local = pltpu.make_async_copy(src(me), dst_ref.at[me], local_sem)
        self.sends, self.recvs = [], []
        for k in range(1, N_DEV):
            px = 1 - x if (k >> 2) & 1 else x
            py = 1 - y if (k >> 1) & 1 else y
            pc = 1 - c if k & 1 else c
            peer = 4 * px + 2 * py + pc
            sems = dict(send_sem=send_sems.at[k - 1], recv_sem=recv_sems.at[k - 1],
                        device_id=(px, py, pc), device_id_type=MESH)
            self.sends.append(pltpu.make_async_remote_copy(src_ref=src(peer), dst_ref=dst_ref.at[me], **sems))
            self.recvs.append(pltpu.make_async_remote_copy(src_ref=dst_ref.at[peer], dst_ref=dst_ref.at[peer], **sems))

    def start(self):
        self.local.start()
        for cp in self.sends:
            cp.start()

    def wait(self):
        for cp in self.recvs:
            cp.wait_recv()
        for cp in self.sends:
            cp.wait_send()
        self.local.wait()


def _grad_w_in(u, dp, g_w_out, seq):
    tk = 512
    nk = seq // tk

    def body(u_ref, dp_ref, gsrc_ref, g_ref, recv_ref, send_sems, recv_sems, local_sem):
        n = pl.program_id(0)
        k = pl.program_id(1)
        ex = _SlotExchange(gsrc_ref, recv_ref, send_sems, recv_sems, local_sem, blocked=True)

        @pl.when((n == 0) & (k == 0))
        def _():
            ex.start()

        @pl.when(k == 0)
        def _():
            g_ref[...] = jnp.zeros_like(g_ref)

        g_ref[0] += _dot(u_ref[...], dp_ref[...], TN)

        @pl.when((n == N_DEV - 1) & (k == nk - 1))
        def _():
            ex.wait()

    return pl.pallas_call(
        body, name="grad_w_in", grid=(N_DEV, nk),
        in_specs=[pl.BlockSpec((tk, D_MODEL), lambda n, k: (k, 0)),
                  pl.BlockSpec((tk, W_BLK), lambda n, k: (k, n)), ANY],
        out_specs=[pl.BlockSpec((1, D_MODEL, W_BLK), lambda n, k: (n, 0, 0)), ANY],
        out_shape=[SDS((N_DEV, D_MODEL, W_BLK), F32), SDS(g_w_out.shape, F32)],
        scratch_shapes=EXCHANGE_SEMS,
        compiler_params=_cparams(("arbitrary", "arbitrary")),
    )(u, dp, g_w_out)


def _grad_x(dp, w_all, x, pre_w, dout, g_w_in, seq):
    tm = 512
    nm = seq // tm

    def body(dp_ref, w_ref, x_ref, pw_ref, do_ref, gsrc_ref, gx_ref, gpw_ref, recv_ref, acc,
             send_sems, recv_sems, local_sem):
        m = pl.program_id(0)
        j = pl.program_id(1)
        ex = _SlotExchange(gsrc_ref, recv_ref, send_sems, recv_sems, local_sem, blocked=True)

        @pl.when((m == 0) & (j == 0))
        def _():
            ex.start()
            gpw_ref[...] = jnp.zeros_like(gpw_ref)

        @pl.when(j == 0)
        def _():
            acc[...] = jnp.zeros_like(acc)

        acc[...] += _dot(dp_ref[...], w_ref[0], NT)

        @pl.when(j == N_DEV - 1)
        def _():
            xv = x_ref[...]
            r1 = lax.rsqrt(jnp.mean(xv * xv, axis=-1, keepdims=True) + EPS)
            xn = xv * r1
            du = acc[...]
            gpw_ref[0:1, :] += jnp.sum(du * xn, axis=0, keepdims=True)
            dxn = du * pw_ref[...]
            gx_ref[...] = r1 * (dxn - xn * jnp.mean(dxn * xn, axis=-1, keepdims=True)) + do_ref[...]

        @pl.when((m == nm - 1) & (j == N_DEV - 1))
        def _():
            ex.wait()

    return pl.pallas_call(
        body, name="grad_x", grid=(nm, N_DEV),
        in_specs=[pl.BlockSpec((tm, W_BLK), lambda m, j: (m, j)),
                  pl.BlockSpec((1, D_MODEL, W_BLK), lambda m, j: (j, 0, 0)),
                  pl.BlockSpec((tm, D_MODEL), lambda m, j: (m, 0)),
                  pl.BlockSpec((1, D_MODEL), lambda m, j: (0, 0)),
                  pl.BlockSpec((tm, D_MODEL), lambda m, j: (m, 0)), ANY],
        out_specs=[pl.BlockSpec((tm, D_MODEL), lambda m, j: (m, 0)),
                   pl.BlockSpec((8, D_MODEL), lambda m, j: (0, 0)), ANY],
        out_shape=[SDS((seq, D_MODEL), F32), SDS((8, D_MODEL), F32), SDS(g_w_in.shape, F32)],
        scratch_shapes=[pltpu.VMEM((tm, D_MODEL), F32)] + EXCHANGE_SEMS,
        compiler_params=_cparams(("arbitrary", "arbitrary")),
    )(dp, w_all, x, pre_w, dout, g_w_in)


def _local_step(x, tgt, pre_w, w_in_all, conv_w, conv_b, wa, wx, ba, bx, lam, lbl, gnorm_w, w_out, post_w):
    seq = x.shape[0]
    p, u = _in_proj(x, pre_w, w_in_all, seq)
    h, y_lru = _lru_forward(p, conv_w, conv_b, wa, wx, ba, bx, lam, seq)
    y_hgrn, o, states = _hgrn_forward(p, lbl, gnorm_w, seq)
    dymix, dout, g_w_out, stats = _out_proj(y_lru, y_hgrn, w_out, x, tgt, post_w, seq)
    dp_lru, g_wa, g_wx, lru_small = _lru_backward(p, h, dymix, conv_w, conv_b, wa, wx, ba, bx, lam, seq)
    dp_hgrn, hgrn_small = _hgrn_backward(p, o, states, dymix, lbl, gnorm_w, seq)
    dp = jnp.concatenate([dp_lru, dp_hgrn], axis=1)
    return dict(u=u, dp=dp, dout=dout, g_w_out=g_w_out, g_wa=g_wa, g_wx=g_wx,
                lru_small=lru_small, hgrn_small=hgrn_small, stats=stats)


def _all_gather(parts):
    n = len(parts)

    def body(*refs):
        ins, outs = refs[:n], refs[n:2 * n]
        send_sems, recv_sems, local_sems = refs[2 * n:]
        x, y, c = _mesh_pos()
        me, sibling = (x, y, c), (x, y, 1 - c)
        chips = [(1 - x, y), (x, 1 - y), (1 - x, 1 - y)]

        def slot(a, pos):
            return outs[a].at[4 * pos[0] + 2 * pos[1] + pos[2]]

        def copy(a, k, block, to, src=None):
            dst = slot(a, block)
            return pltpu.make_async_remote_copy(
                src_ref=dst if src is None else src, dst_ref=dst,
                send_sem=send_sems.at[a, k], recv_sem=recv_sems.at[a, k],
                device_id=to, device_id_type=MESH)

        mine = [pltpu.make_async_copy(ins[a], slot(a, me), local_sems.at[a]) for a in range(n)]
        for cp in mine:
            cp.start()
        first = []
        for a in range(n):
            first.append(copy(a, 0, me, sibling, src=ins[a]))
            first += [copy(a, 1 + j, me, (*chip, c), src=ins[a]) for j, chip in enumerate(chips)]
        for cp in first:
            cp.start()
        passed = []
        for j, chip in enumerate(chips):
            for a in range(n):
                copy(a, 1 + j, (*chip, c), me).wait_recv()
                fwd = copy(a, 4 + j, (*chip, c), sibling)
                fwd.start()
                passed.append(fwd)
        for a in range(n):
            copy(a, 0, sibling, me).wait_recv()
            for j, chip in enumerate(chips):
                copy(a, 4 + j, (*chip, 1 - c), me).wait_recv()
        for cp in first + passed:
            cp.wait_send()
        for cp in mine:
            cp.wait()

    return pl.pallas_call(
        body, name="gather_weights",
        out_shape=[SDS((N_DEV,) + p.shape, p.dtype) for p in parts],
        in_specs=[ANY] * n, out_specs=[ANY] * n,
        scratch_shapes=[pltpu.SemaphoreType.DMA((n, 7)), pltpu.SemaphoreType.DMA((n, 7)),
                        pltpu.SemaphoreType.DMA((n,))],
    )(*parts)


def _exchange_grads(blocks, repl):
    nb = len(blocks)
    n = nb + 1

    def body(*refs):
        ins, outs, sems = refs[:n], refs[n:2 * n], refs[2 * n:]
        exs = [_SlotExchange(ins[a], outs[a], *sems[3 * a:3 * a + 3], blocked=a < nb) for a in range(n)]
        for ex in exs:
            ex.start()
        for ex in exs:
            ex.wait()

    arrs = list(blocks) + [repl]
    shapes = [SDS(b.shape, b.dtype) for b in blocks] + [SDS((N_DEV,) + repl.shape, repl.dtype)]
    return pl.pallas_call(
        body, name="exchange_small", out_shape=shapes,
        in_specs=[ANY] * n, out_specs=[ANY] * n,
        scratch_shapes=EXCHANGE_SEMS * n,
    )(*arrs)


def _pack_rows(picks, name):
    arrs = [p[0] for p in picks]

    def body(*refs):
        out = refs[-1]
        out[...] = jnp.zeros_like(out)
        at = 0
        for ref, (_, row, rows, scale) in zip(refs[:-1], picks):
            out[at:at + rows, :] = ref[row:row + rows, :] * scale
            at += rows

    return pl.pallas_call(body, name=name, out_shape=SDS((8, D_MODEL), F32))(*arrs)


def _adamw(g, w, m, v):
    m2 = ADAM_B1 * m + (1.0 - ADAM_B1) * g
    v2 = ADAM_B2 * v + (1.0 - ADAM_B2) * (g * g)
    m_hat = m2 / (1.0 - ADAM_B1 ** ADAM_STEP)
    v_hat = v2 / (1.0 - ADAM_B2 ** ADAM_STEP)
    delta = -ADAM_LR * (m_hat / (jnp.sqrt(v_hat) + ADAM_EPS) + ADAM_WD * w)
    return delta, m2, v2


def _sum_slots(r_ref):
    g = r_ref[0]
    for s in range(1, N_DEV):
        g = g + r_ref[s]
    return g


def _sum_adamw(recv, w, m, v, tr, name):
    rows, cols = w.shape

    def body(r_ref, w_ref, m_ref, v_ref, g_ref, d_ref, m2_ref, v2_ref):
        g = _sum_slots(r_ref)
        g_ref[...] = g
        d_ref[...], m2_ref[...], v2_ref[...] = _adamw(g, w_ref[...], m_ref[...], v_ref[...])

    blk = pl.BlockSpec((tr, cols), lambda i: (i, 0))
    return pl.pallas_call(
        body, name=name, grid=(rows // tr,),
        in_specs=[pl.BlockSpec((N_DEV, tr, cols), lambda i: (0, i, 0)), blk, blk, blk],
        out_specs=[blk] * 4, out_shape=[SDS((rows, cols), F32)] * 4,
        compiler_params=_cparams(("parallel",)),
    )(recv, w, m, v)


def _sum_adamw_repl(recv, w, m, v):
    def body(r_ref, w_ref, m_ref, v_ref, g_ref, d_ref, m2_ref, v2_ref, loss_ref):
        g = _sum_slots(r_ref)
        g_ref[...] = g
        d_ref[...], m2_ref[...], v2_ref[...] = _adamw(g, w_ref[...], m_ref[...], v_ref[...])
        total = jnp.sum(g[RP_LOSS:RP_LOSS + 1, :], axis=-1, keepdims=True)
        loss_ref[...] = jnp.broadcast_to(total, loss_ref.shape)

    return pl.pallas_call(
        body, name="adamw_repl",
        out_shape=[SDS((8, D_MODEL), F32)] * 4 + [SDS((8, 128), F32)],
    )(recv, w, m, v)


def _shard_rows(t, lead):
    r = t.shape[1] // N_DEV
    t = t.reshape((lead, N_DEV, r) + t.shape[2:])
    return jnp.moveaxis(t, 1, 0)


def _pad8(t):
    return jnp.pad(t, ((0, 0), (0, 8 - t.shape[1]), (0, 0)))


def _pack_small(wa, wx, cw, b_a, b_x):
    n = wa.shape[0]
    return jnp.concatenate([
        wa.reshape(n, 256, 128), wx.reshape(n, 256, 128), _pad8(cw),
        _pad8(b_a.reshape(n, 1, 128)), _pad8(b_x.reshape(n, 1, 128))], axis=1)


def _unpack_small(t):
    n = t.shape[0]
    return (t[:, SM_WA:SM_WA + 256].reshape(n, 4, 32, 256), t[:, SM_WX:SM_WX + 256].reshape(n, 4, 32, 256),
            t[:, SM_CW:SM_CW + 4], t[:, SM_BA].reshape(n, 4, 32), t[:, SM_BX].reshape(n, 4, 32))


def kernel(x, pre_norm_w, w_in, conv_w, conv_b, lru_w_a, lru_b_a, lru_w_x, lru_b_x, lru_lambda, hgrn_lb_logits, hgrn_gnorm_w, w_out, post_norm_w, loss_target, m_pre_norm_w, m_w_in, m_conv_w, m_conv_b, m_lru_w_a, m_lru_b_a, m_lru_w_x, m_lru_b_x, m_lru_lambda, m_hgrn_lb_logits, m_hgrn_gnorm_w, m_w_out, m_post_norm_w, v_pre_norm_w, v_w_in, v_conv_w, v_conv_b, v_lru_w_a, v_lru_b_a, v_lru_w_x, v_lru_b_x, v_lru_lambda, v_hgrn_lb_logits, v_hgrn_gnorm_w, v_w_out, v_post_norm_w):
    seq = x.shape[1]
    x2 = x.reshape(seq, D_MODEL)
    tgt = loss_target.reshape(seq, D_MODEL)

    small_w = _pack_small(lru_w_a, lru_w_x, conv_w, lru_b_a, lru_b_x)[0]
    w_in_all, w_out_all, small_all = _all_gather([w_in[0].astype(BF16), w_out[0].astype(BF16), small_w])
    wa_s, wx_s, cw_s, ba_s, bx_s = _unpack_small(small_all)
    wa = jnp.moveaxis(wa_s, 0, 1).reshape(LRU_BLOCKS, LRU_BW, LRU_BW).astype(BF16)
    wx = jnp.moveaxis(wx_s, 0, 1).reshape(LRU_BLOCKS, LRU_BW, LRU_BW).astype(BF16)
    cw = jnp.moveaxis(cw_s, 0, 1).reshape(4, D_MODEL)
    ba = jnp.moveaxis(ba_s, 0, 1).reshape(1, D_MODEL)
    bx = jnp.moveaxis(bx_s, 0, 1).reshape(1, D_MODEL)

    loc = _local_step(x2, tgt, pre_norm_w, w_in_all, cw, conv_b, wa, wx, ba, bx, lru_lambda,
                      hgrn_lb_logits, hgrn_gnorm_w, w_out_all.reshape(D_MIX, D_MODEL), post_norm_w)

    g_w_in, r_out = _grad_w_in(loc["u"], loc["dp"], loc["g_w_out"].reshape(N_DEV, D_MIX // N_DEV, D_MODEL), seq)
    grad_x, pre_small, r_in = _grad_x(loc["dp"], w_in_all, x2, pre_norm_w, loc["dout"], g_w_in, seq)
    ls = loc["lru_small"]
    g_small = _pack_small(_shard_rows(loc["g_wa"], LRU_BLOCKS), _shard_rows(loc["g_wx"], LRU_BLOCKS),
                          _shard_rows(ls[0:4].reshape(4, D_MODEL, 1), 4).reshape(N_DEV, 4, 128),
                          _shard_rows(ls[5].reshape(4, LRU_BW, 1), 4).reshape(N_DEV, 4, 32),
                          _shard_rows(ls[6].reshape(4, LRU_BW, 1), 4).reshape(N_DEV, 4, 32))
    g_repl = _pack_rows([(pre_small, 0, 1, 1.0), (ls, 4, 1, 1.0), (ls, 7, 1, 1.0),
                         (loc["hgrn_small"], 0, 1, 1.0), (loc["hgrn_small"], 0, 1, -1.0),
                         (loc["hgrn_small"], 1, 1, 1.0), (loc["stats"], 0, 2, 1.0)], "pack_grads")
    r_small, r_repl = _exchange_grads([g_small], g_repl)

    zero = jnp.zeros((1, D_MODEL), F32)
    pack_w = lambda a, b, c_, d, e, f, name: _pack_rows(
        [(a, 0, 1, 1.0), (b, 0, 1, 1.0), (c_, 0, 1, 1.0), (d, 0, 2, 1.0), (e, 0, 1, 1.0), (f, 0, 1, 1.0),
         (zero, 0, 1, 1.0)], name)
    w_repl = pack_w(pre_norm_w, conv_b, lru_lambda, hgrn_lb_logits, hgrn_gnorm_w, post_norm_w, "pack_w")
    m_repl = pack_w(m_pre_norm_w, m_conv_b, m_lru_lambda, m_hgrn_lb_logits, m_hgrn_gnorm_w, m_post_norm_w, "pack_m")
    v_repl = pack_w(v_pre_norm_w, v_conv_b, v_lru_lambda, v_hgrn_lb_logits, v_hgrn_gnorm_w, v_post_norm_w, "pack_v")
    o_repl = _sum_adamw_repl(r_repl, w_repl, m_repl, v_repl)
    loss = o_repl[4][0, 0]

    o_in = _sum_adamw(r_in, w_in[0], m_w_in[0], v_w_in[0], 128, "adamw_w_in")
    o_out = _sum_adamw(r_out, w_out[0], m_w_out[0], v_w_out[0], 64, "adamw_w_out")
    o_small = _sum_adamw(r_small,
                         _pack_small(lru_w_a, lru_w_x, conv_w, lru_b_a, lru_b_x)[0],
                         _pack_small(m_lru_w_a, m_lru_w_x, m_conv_w, m_lru_b_a, m_lru_b_x)[0],
                         _pack_small(v_lru_w_a, v_lru_w_x, v_conv_w, v_lru_b_a, v_lru_b_x)[0],
                         SM_ROWS, "adamw_small")

    outs = [loss, grad_x.reshape(x.shape)]
    for kind in range(4):
        rp = o_repl[kind]
        swa, swx, scw, sba, sbx = _unpack_small(o_small[kind][None])
        outs += [rp[RP_PRE:RP_PRE + 1], o_in[kind][None], scw, rp[RP_CB:RP_CB + 1], swa, sba, swx, sbx,
                 rp[RP_LAM:RP_LAM + 1], rp[RP_LB0:RP_LB1 + 1], rp[RP_GN:RP_GN + 1], o_out[kind][None],
                 rp[RP_POST:RP_POST + 1]]
    return tuple(outs)
```

```python
import functools

import jax
import jax.numpy as jnp
from jax import lax
from jax.experimental import pallas as pl
from jax.experimental.pallas import tpu as pltpu

F32 = jnp.float32
BF16 = jnp.bfloat16
SDS = jax.ShapeDtypeStruct

D_MODEL = 1024
D_IN = 6144
N_DEV = 8
W_BLK = D_IN // N_DEV
D_MIX = 2048
LRU_BLOCKS = 4
LRU_BW = 256
LRU_C = 8.0
N_HEADS = 8
HEAD_D = 128
CHUNK = 128
SUB = 32
N_SUB = CHUNK // SUB
EXP_CLAMP = 80.0
EPS = 1e-6

ADAM_LR = 0.001
ADAM_B1 = 0.9
ADAM_B2 = 0.999
ADAM_EPS = 1e-08
ADAM_WD = 0.01
ADAM_STEP = 10

VMEM_LIMIT = 56 * 1024 * 1024

NN = (((1,), (0,)), ((), ()))
NT = (((1,), (1,)), ((), ()))
TN = (((0,), (0,)), ((), ()))

SM_WA = 0
SM_WX = 256
SM_CW = 512
SM_BA = 520
SM_BX = 528
SM_ROWS = 536

RP_PRE, RP_CB, RP_LAM, RP_LB0, RP_LB1, RP_GN, RP_POST, RP_LOSS = range(8)


def _dot(a, b, dims=NN):
    return lax.dot_general(a, b, dims, preferred_element_type=F32)


def _sigmoid(x):
    return 0.5 * jnp.tanh(0.5 * x) + 0.5


def _sigmoid_pos(x):
    return 1.0 / (1.0 + jnp.exp(-x))


def _cparams(sem, vmem=VMEM_LIMIT):
    return pltpu.CompilerParams(dimension_semantics=sem, vmem_limit_bytes=vmem)


def _iota(shape, axis):
    return lax.broadcasted_iota(jnp.int32, shape, axis)


def _in_proj(x, pre_w, w_all, seq):
    tm = min(1024, seq)

    def body(x_ref, pw_ref, w_ref, p_ref, u_ref):
        @pl.when(pl.program_id(1) == 0)
        def _():
            xv = x_ref[...]
            r = lax.rsqrt(jnp.mean(xv * xv, axis=-1, keepdims=True) + EPS)
            u_ref[...] = (xv * r * pw_ref[...]).astype(BF16)

        p_ref[...] = _dot(u_ref[...], w_ref[0])

    return pl.pallas_call(
        body, name="in_proj", grid=(seq // tm, N_DEV),
        in_specs=[pl.BlockSpec((tm, D_MODEL), lambda m, n: (m, 0)),
                  pl.BlockSpec((1, D_MODEL), lambda m, n: (0, 0)),
                  pl.BlockSpec((1, D_MODEL, W_BLK), lambda m, n: (n, 0, 0))],
        out_specs=[pl.BlockSpec((tm, W_BLK), lambda m, n: (m, n)),
                   pl.BlockSpec((tm, D_MODEL), lambda m, n: (m, 0))],
        out_shape=[SDS((seq, D_IN), F32), SDS((seq, D_MODEL), BF16)],
        compiler_params=_cparams(("parallel", "arbitrary")),
    )(x, pre_w, w_all)


def _softplus_neg(lam):
    z = -lam
    e = jnp.exp(-jnp.abs(z))
    u = 1.0 + e
    log1p_e = jnp.where(u == 1.0, e, jnp.log(u) * (e / (u - 1.0)))
    sp = jnp.maximum(z, 0.0) + log1p_e
    dsp = -jnp.where(z >= 0.0, 1.0 / u, e / u)
    return sp, dsp


def _neg_expm1(x):
    poly = x * (1.0 + x * (1.0 / 2 + x * (1.0 / 6 + x * (1.0 / 24 + x * (1.0 / 120)))))
    return jnp.where(x > -1.0 / 16, -poly, 1.0 - jnp.exp(x))


def _conv_taps(lx, prev8, cw_ref, cb_ref, tile):
    xc = cb_ref[...] + cw_ref[3:4, :] * lx
    for j in (1, 2, 3):
        xc = xc + cw_ref[3 - j:4 - j, :] * pltpu.roll(lx, j, 0)
    row8 = _iota((8, D_MODEL), 0)
    last8 = lx[tile - 8:tile, :]
    fix = jnp.zeros((8, D_MODEL), F32)
    for j in (1, 2, 3):
        wrong = pltpu.roll(last8, j, 0)
        right = pltpu.roll(prev8, j, 0)
        fix = fix + cw_ref[3 - j:4 - j, :] * jnp.where(row8 < j, right - wrong, 0.0)
    return xc, fix


def _lru_gates(xcs, wa, wx, ba, bx, sp):
    xb = xcs.astype(BF16)
    r = _sigmoid_pos(_dot(xb, wa) + ba)
    i = _sigmoid(_dot(xb, wx) + bx)
    la = (-LRU_C * sp) * r
    a = jnp.exp(la)
    one_minus_a2 = _neg_expm1(2.0 * la)
    return r, i, a, one_minus_a2


def _lru_forward(p, conv_w, conv_b, wa, wx, ba, bx, lam, seq):
    tile = min(512, seq // 2)
    nblk = tile // 8

    def body(lx_ref, gt_ref, cw_ref, cb_ref, wa_ref, wx_ref, ba_ref, bx_ref, lam_ref,
             h_ref, y_ref, prev8, hcar, xc_scr, a_scr, u_scr):
        @pl.when(pl.program_id(0) == 0)
        def _():
            prev8[...] = jnp.zeros_like(prev8)
            hcar[...] = jnp.zeros_like(hcar)

        lx = lx_ref[...]
        xc, fix = _conv_taps(lx, prev8[...], cw_ref, cb_ref, tile)
        xc_scr[...] = xc
        xc_scr[0:8, :] = xc_scr[0:8, :] + fix
        prev8[...] = lx_ref[tile - 8:tile, :]
        sp, _ = _softplus_neg(lam_ref[...])
        for n in range(LRU_BLOCKS):
            sl = slice(n * LRU_BW, (n + 1) * LRU_BW)
            xcs = xc_scr[:, sl]
            _, i, a, ne = _lru_gates(xcs, wa_ref[n], wx_ref[n], ba_ref[:, sl], bx_ref[:, sl], sp[:, sl])
            a_scr[:, sl] = a
            u_scr[:, sl] = jnp.sqrt(ne) * (i * xcs)

        row8 = _iota((8, D_MODEL), 0)

        def blk(j, hc):
            off = pl.multiple_of(j * 8, 8)
            a = a_scr[pl.ds(off, 8), :]
            u = u_scr[pl.ds(off, 8), :]
            for k in (1, 2, 4):
                m = row8 >= k
                u = jnp.where(m, u + a * pltpu.roll(u, k, 0), u)
                a = jnp.where(m, a * pltpu.roll(a, k, 0), a)
            h = u + a * hc
            h_ref[pl.ds(off, 8), :] = h
            return jnp.broadcast_to(h[7:8, :], (8, D_MODEL))

        hcar[...] = lax.fori_loop(0, nblk, blk, hcar[...])
        g = gt_ref[...]
        y_ref[...] = (h_ref[...] * (g * _sigmoid(g))).astype(BF16)

    full = lambda shape: pl.BlockSpec(shape, lambda t: (0,) * len(shape))
    return pl.pallas_call(
        body, name="lru_fwd", grid=(seq // tile,),
        in_specs=[pl.BlockSpec((tile, D_MODEL), lambda t: (t, 0)),
                  pl.BlockSpec((tile, D_MODEL), lambda t: (t, 1)),
                  full((4, D_MODEL)), full((1, D_MODEL)),
                  full((LRU_BLOCKS, LRU_BW, LRU_BW)), full((LRU_BLOCKS, LRU_BW, LRU_BW)),
                  full((1, D_MODEL)), full((1, D_MODEL)), full((1, D_MODEL))],
        out_specs=[pl.BlockSpec((tile, D_MODEL), lambda t: (t, 0)),
                   pl.BlockSpec((tile, D_MODEL), lambda t: (t, 0))],
        out_shape=[SDS((seq, D_MODEL), F32), SDS((seq, D_MODEL), BF16)],
        scratch_shapes=[pltpu.VMEM((8, D_MODEL), F32), pltpu.VMEM((8, D_MODEL), F32),
                        pltpu.VMEM((tile, D_MODEL), F32), pltpu.VMEM((tile, D_MODEL), F32),
                        pltpu.VMEM((tile, D_MODEL), F32)],
        compiler_params=_cparams(("arbitrary",)),
    )(p, p, conv_w, conv_b, wa, wx, ba, bx, lam)


def _lru_backward(p, h, dymix, conv_w, conv_b, wa, wx, ba, bx, lam, seq):
    tile = min(256, seq // 2)
    nt = seq // tile
    nblk = tile // 8
    t8 = tile // 8

    def body(lx_ref, lxh_ref, gt_ref, h_ref, hh_ref, dy_ref, cw_ref, cb_ref, wa_ref, wx_ref, ba_ref,
             bx_ref, lam_ref, dp_ref, gwa_ref, gwx_ref, gsm_ref,
             lamcar, anext, dxc8, xc_scr, r_scr, i_scr, a_scr, m_scr, rm_scr, c_scr, l_scr, dxc_scr):
        step = pl.program_id(0)
        first_tile = step == nt - 1

        @pl.when(step == 0)
        def _():
            lamcar[...] = jnp.zeros_like(lamcar)
            anext[...] = jnp.zeros_like(anext)
            dxc8[...] = jnp.zeros_like(dxc8)
            gwa_ref[...] = jnp.zeros_like(gwa_ref)
            gwx_ref[...] = jnp.zeros_like(gwx_ref)
            gsm_ref[...] = jnp.zeros_like(gsm_ref)

        keep = jnp.where(first_tile, 0.0, 1.0)
        lx = lx_ref[...]
        prev8 = lxh_ref[...] * keep
        xc, fix = _conv_taps(lx, prev8, cw_ref, cb_ref, tile)
        xc_scr[...] = xc
        xc_scr[0:8, :] = xc_scr[0:8, :] + fix
        sp, dsp = _softplus_neg(lam_ref[...])
        for n in range(LRU_BLOCKS):
            sl = slice(n * LRU_BW, (n + 1) * LRU_BW)
            r, i, a, ne = _lru_gates(xc_scr[:, sl], wa_ref[n], wx_ref[n], ba_ref[:, sl], bx_ref[:, sl],
                                     sp[:, sl])
            r_scr[:, sl] = r
            i_scr[:, sl] = i
            a_scr[:, sl] = a
            m_scr[:, sl] = jnp.sqrt(ne)
            rm_scr[:, sl] = lax.rsqrt(ne)

        g = gt_ref[...]
        sg = _sigmoid(g)
        dy = dy_ref[...]
        hv = h_ref[...]
        dp_ref[:, D_MODEL:2 * D_MODEL] = (dy * hv * (sg * (1.0 + g * (1.0 - sg)))).astype(BF16)

        rowt = _iota((tile, D_MODEL), 0)
        av = a_scr[...]
        l_scr[...] = dy * (g * sg)
        c_scr[...] = jnp.where(rowt == tile - 1, anext[...][0:1, :], pltpu.roll(av, tile - 1, 0))
        anext[...] = jnp.broadcast_to(av[0:1, :], (8, D_MODEL))
        row8 = _iota((8, D_MODEL), 0)

        def blk(jj, lc):
            off = pl.multiple_of((nblk - 1 - jj) * 8, 8)
            c = c_scr[pl.ds(off, 8), :]
            u = l_scr[pl.ds(off, 8), :]
            for k in (1, 2, 4):
                m = row8 < 8 - k
                u = jnp.where(m, u + c * pltpu.roll(u, 8 - k, 0), u)
                c = jnp.where(m, c * pltpu.roll(c, 8 - k, 0), c)
            lamv = u + c * lc
            l_scr[pl.ds(off, 8), :] = lamv
            return jnp.broadcast_to(lamv[0:1, :], (8, D_MODEL))

        lamcar[...] = lax.fori_loop(0, nblk, blk, lamcar[...])

        hprev = jnp.where(rowt == 0, hh_ref[...][7:8, :] * keep, pltpu.roll(hv, 1, 0))
        for n in range(LRU_BLOCKS):
            sl = slice(n * LRU_BW, (n + 1) * LRU_BW)
            lamv = l_scr[:, sl]
            xcs = xc_scr[:, sl]
            r = r_scr[:, sl]
            i = i_scr[:, sl]
            a = a_scr[:, sl]
            mult = m_scr[:, sl]
            d_la = lamv * hprev[:, sl] * a - (lamv * i * xcs) * (a * a * rm_scr[:, sl])
            d_pr = d_la * (-LRU_C * sp[:, sl]) * r * (1.0 - r)
            d_pi = (lamv * mult * xcs) * i * (1.0 - i)
            gsm_ref[7:8, sl] += jnp.sum(d_la * r, axis=0, keepdims=True) * (-LRU_C) * dsp[:, sl]
            gsm_ref[5:6, sl] += jnp.sum(d_pr, axis=0, keepdims=True)
            gsm_ref[6:7, sl] += jnp.sum(d_pi, axis=0, keepdims=True)
            xb = xcs.astype(BF16)
            prb = d_pr.astype(BF16)
            pib = d_pi.astype(BF16)
            gwa_ref[n] += _dot(xb, prb, TN)
            gwx_ref[n] += _dot(xb, pib, TN)
            dxc_scr[:, sl] = lamv * mult * i + _dot(prb, wa_ref[n], NT) + _dot(pib, wx_ref[n], NT)

        dxc = dxc_scr[...]
        gsm_ref[4:5, :] += jnp.sum(dxc, axis=0, keepdims=True)
        last8 = lx[tile - 8:tile, :]
        first8 = dxc[0:8, :]
        dlx = cw_ref[3:4, :] * dxc
        gsm_ref[3:4, :] += jnp.sum(dxc * lx, axis=0, keepdims=True)
        fix = jnp.zeros((8, D_MODEL), F32)
        for j in (1, 2, 3):
            w = cw_ref[3 - j:4 - j, :]
            dlx = dlx + w * pltpu.roll(dxc, tile - j, 0)
            fix = fix + w * jnp.where(row8 + j >= 8,
                                      pltpu.roll(dxc8[...], 8 - j, 0) - pltpu.roll(first8, 8 - j, 0), 0.0)
            halo = jnp.where(row8 < j, pltpu.roll(prev8, j, 0) - pltpu.roll(last8, j, 0), 0.0)
            gsm_ref[3 - j:4 - j, :] += (jnp.sum(dxc * pltpu.roll(lx, j, 0), axis=0, keepdims=True)
                                        + jnp.sum(first8 * halo, axis=0, keepdims=True))
        dxc8[...] = first8
        dp_ref[:, 0:D_MODEL] = dlx.astype(BF16)
        top = tile - 8
        dp_ref[top:tile, 0:D_MODEL] = (dlx[top:tile, :] + fix).astype(BF16)

    rev = lambda t: (nt - 1 - t, 0)
    halo_idx = lambda t: (jnp.maximum((nt - 1 - t) * t8 - 1, 0), 0)
    full = lambda shape: pl.BlockSpec(shape, lambda t: (0,) * len(shape))
    big = lambda: pltpu.VMEM((tile, D_MODEL), F32)
    return pl.pallas_call(
        body, name="lru_bwd", grid=(nt,),
        in_specs=[pl.BlockSpec((tile, D_MODEL), rev),
                  pl.BlockSpec((8, D_MODEL), halo_idx),
                  pl.BlockSpec((tile, D_MODEL), lambda t: (nt - 1 - t, 1)),
                  pl.BlockSpec((tile, D_MODEL), rev),
                  pl.BlockSpec((8, D_MODEL), halo_idx),
                  pl.BlockSpec((tile, D_MODEL), rev),
                  full((4, D_MODEL)), full((1, D_MODEL)),
                  full((LRU_BLOCKS, LRU_BW, LRU_BW)), full((LRU_BLOCKS, LRU_BW, LRU_BW)),
                  full((1, D_MODEL)), full((1, D_MODEL)), full((1, D_MODEL))],
        out_specs=[pl.BlockSpec((tile, 2 * D_MODEL), rev),
                   full((LRU_BLOCKS, LRU_BW, LRU_BW)), full((LRU_BLOCKS, LRU_BW, LRU_BW)),
                   full((8, D_MODEL))],
        out_shape=[SDS((seq, D_IN), BF16), SDS((LRU_BLOCKS, LRU_BW, LRU_BW), F32),
                   SDS((LRU_BLOCKS, LRU_BW, LRU_BW), F32), SDS((8, D_MODEL), F32)],
        scratch_shapes=[pltpu.VMEM((8, D_MODEL), F32), pltpu.VMEM((8, D_MODEL), F32),
                        pltpu.VMEM((8, D_MODEL), F32)] + [big() for _ in range(9)],
        compiler_params=_cparams(("arbitrary",)),
    )(p, p, p, h, h, dymix, conv_w, conv_b, wa, wx, ba, bx, lam)


def _split3(g):
    hi = g.astype(BF16)
    r1 = g - hi.astype(F32)
    mid = r1.astype(BF16)
    lo = (r1 - mid.astype(F32)).astype(BF16)
    return hi, mid, lo


def _tri_matmul(tri, g):
    hi, mid, lo = _split3(g)
    return _dot(tri, lo) + _dot(tri, mid) + _dot(tri, hi)


def _hgrn_gate_terms(q, fr, lbl):
    lb = _sigmoid_pos(lbl[0:1, :] - lbl[1:2, :])
    sig = _sigmoid(fr)
    f = lb + (1.0 - lb) * sig
    sq = _sigmoid(q)
    return lb, sig, f, sq


def _hgrn_decay(bh):
    row = _iota((CHUNK, HEAD_D), 0)
    pieces = []
    for i in range(N_SUB):
        mid = 0.5 * (bh[i * SUB:i * SUB + 1, :] + bh[(i + 1) * SUB - 1:(i + 1) * SUB, :])
        pieces.append(jnp.broadcast_to(mid, (SUB, HEAD_D)))
    mu = jnp.concatenate(pieces, axis=0)
    eq = [jnp.exp(jnp.minimum(bh - mu, EXP_CLAMP))]
    ek = [jnp.exp(jnp.minimum(mu - bh, EXP_CLAMP))]
    for j in range(1, N_SUB):
        rho = bh[j * SUB - 1:j * SUB, :]
        eq.append(jnp.where(row >= j * SUB, jnp.exp(jnp.minimum(bh - rho, 0.0)), 0.0))
        ek.append(jnp.where((row >= (j - 1) * SUB) & (row < j * SUB),
                            jnp.exp(jnp.minimum(rho - bh, 0.0)), 0.0))
    blast = bh[CHUNK - 1:CHUNK, :]
    return eq, ek, jnp.exp(bh), jnp.exp(blast - bh), jnp.exp(blast)


def _hgrn_scores(qs, k, eq, ek):
    qt = jnp.concatenate([(qs * e).astype(BF16) for e in eq], axis=1)
    kt = jnp.concatenate([(k * e).astype(BF16) for e in ek], axis=1)
    r = _iota((CHUNK, CHUNK), 0)
    c = _iota((CHUNK, CHUNK), 1)
    diag = (jnp.right_shift(r, 5) == jnp.right_shift(c, 5)) & (c <= r)
    a = (jnp.where(diag, _dot(qt[:, :HEAD_D], kt[:, :HEAD_D], NT), 0.0)
         + _dot(qt[:, HEAD_D:], kt[:, HEAD_D:], NT))
    return a, qt, kt, diag


def _hgrn_forward(p, lbl, gw, seq):
    nc = seq // CHUNK
    assert SUB == 32

    def body(q_ref, f_ref, v_ref, hg_ref, lbl_ref, gw_ref, y_ref, o_ref, st_ref, s_scr):
        @pl.when(pl.program_id(0) == 0)
        def _():
            s_scr[...] = jnp.zeros_like(s_scr)

        r = _iota((CHUNK, CHUNK), 0)
        c = _iota((CHUNK, CHUNK), 1)
        tri = jnp.where(c <= r, 1.0, 0.0).astype(BF16)
        for h in range(N_HEADS):
            sl = slice(h * HEAD_D, (h + 1) * HEAD_D)
            q = q_ref[:, sl]
            _, _, f, sq = _hgrn_gate_terms(q, f_ref[:, sl], lbl_ref[:, sl])
            qs = q * sq
            k = 1.0 - f
            bh = _tri_matmul(tri, jnp.log(f))
            eq, ek, eb, ekst, ebl = _hgrn_decay(bh)
            a, _, _, _ = _hgrn_scores(qs, k, eq, ek)
            st = s_scr[h]
            st_ref[0, h] = st
            vb = v_ref[:, sl].astype(BF16)
            o = _dot(a.astype(BF16), vb) + _dot((qs * eb).astype(BF16), st.astype(BF16), NT)
            s_scr[h] = st * ebl + _dot(vb, (k * ekst).astype(BF16), TN)
            o_ref[:, sl] = o
            rs = lax.rsqrt(jnp.mean(o * o, axis=-1, keepdims=True) + EPS)
            hg = hg_ref[:, sl]
            y_ref[:, sl] = ((o * rs * gw_ref[:, sl]) * (hg * _sigmoid(hg))).astype(BF16)

    col = lambda j: pl.BlockSpec((CHUNK, D_MODEL), lambda c: (c, j))
    par = lambda rows: pl.BlockSpec((rows, D_MODEL), lambda c: (0, 0))
    return pl.pallas_call(
        body, name="hgrn_fwd", grid=(nc,),
        in_specs=[col(2), col(3), col(4), col(5), par(2), par(1)],
        out_specs=[col(0), col(0),
                   pl.BlockSpec((1, N_HEADS, HEAD_D, HEAD_D), lambda c: (c, 0, 0, 0))],
        out_shape=[SDS((seq, D_MODEL), BF16), SDS((seq, D_MODEL), F32),
                   SDS((nc, N_HEADS, HEAD_D, HEAD_D), F32)],
        scratch_shapes=[pltpu.VMEM((N_HEADS, HEAD_D, HEAD_D), F32)],
        compiler_params=_cparams(("arbitrary",)),
    )(p, p, p, p, lbl, gw)


def _hgrn_backward(p, o, states, dymix, lbl, gw, dp_full, seq):
    nc = seq // CHUNK

    def body(q_ref, f_ref, v_ref, hg_ref, o_ref, st_ref, dy_ref, lbl_ref, gw_ref, dpin_ref,
             dpo_ref, gsm_ref, ds_scr, dp_buf, dp_sems):
        del dpin_ref
        step = pl.program_id(0)
        slot = step % 2

        def out_copy(s, chunk):
            rows = pl.ds(pl.multiple_of(chunk * CHUNK, CHUNK), CHUNK)
            return pltpu.make_async_copy(dp_buf.at[s], dpo_ref.at[rows, pl.ds(2 * D_MODEL, 4 * D_MODEL)],
                                         dp_sems.at[s])

        @pl.when(step == 0)
        def _():
            ds_scr[...] = jnp.zeros_like(ds_scr)
            gsm_ref[...] = jnp.zeros_like(gsm_ref)

        @pl.when(step >= 2)
        def _():
            out_copy(slot, nc + 1 - step).wait()

        dp_ref = dp_buf.at[slot]

        r = _iota((CHUNK, CHUNK), 0)
        c = _iota((CHUNK, CHUNK), 1)
        tri = jnp.where(c <= r, 1.0, 0.0).astype(BF16)
        triu = jnp.where(c >= r, 1.0, 0.0).astype(BF16)
        row = _iota((CHUNK, HEAD_D), 0)
        for h in range(N_HEADS):
            sl = slice(h * HEAD_D, (h + 1) * HEAD_D)
            out = lambda j: slice(j * D_MODEL + h * HEAD_D, j * D_MODEL + (h + 1) * HEAD_D)
            q = q_ref[:, sl]
            lb, sig, f, sq = _hgrn_gate_terms(q, f_ref[:, sl], lbl_ref[:, sl])
            qs = q * sq
            k = 1.0 - f
            bh = _tri_matmul(tri, jnp.log(f))
            eq, ek, eb, ekst, ebl = _hgrn_decay(bh)
            a, qt, kt, diag = _hgrn_scores(qs, k, eq, ek)

            o = o_ref[:, sl]
            gwv = gw_ref[:, sl]
            rs = lax.rsqrt(jnp.mean(o * o, axis=-1, keepdims=True) + EPS)
            on = o * rs
            hg = hg_ref[:, sl]
            sh = _sigmoid(hg)
            dy = dy_ref[:, sl]
            d_onw = dy * (hg * sh)
            dp_ref[:, out(3)] = (dy * (on * gwv) * (sh * (1.0 + hg * (1.0 - sh)))).astype(BF16)
            gsm_ref[1:2, sl] += jnp.sum(d_onw * on, axis=0, keepdims=True)
            d_on = d_onw * gwv
            d_o = rs * (d_on - on * jnp.mean(d_on * on, axis=-1, keepdims=True))

            dob = d_o.astype(BF16)
            vb = v_ref[:, sl].astype(BF16)
            st = st_ref[0, h]
            stb = st.astype(BF16)
            dst = ds_scr[h]
            dstb = dst.astype(BF16)
            kst = k * ekst
            qin = qs * eb

            da = _dot(dob, vb, NT)
            dab = da.astype(BF16)
            da0b = jnp.where(diag, da, 0.0).astype(BF16)
            dp_ref[:, out(2)] = (_dot(a.astype(BF16), dob, TN) + _dot(kst.astype(BF16), dstb, NT)).astype(BF16)

            g_all = [_dot(da0b, kt[:, :HEAD_D])]
            h_all = [_dot(da0b, qt[:, :HEAD_D], TN)]
            g_off = _dot(dab, kt[:, HEAD_D:])
            h_off = _dot(dab, qt[:, HEAD_D:], TN)
            for j in range(1, N_SUB):
                g_all.append(g_off[:, (j - 1) * HEAD_D:j * HEAD_D])
                h_all.append(h_off[:, (j - 1) * HEAD_D:j * HEAD_D])
            dq_inter = eb * _dot(dob, stb)
            d_kst = ekst * _dot(vb, dstb)
            d_q = dq_inter
            d_k = d_kst
            db = qs * dq_inter - k * d_kst
            for j in range(N_SUB):
                sj = slice(j * HEAD_D, (j + 1) * HEAD_D)
                d_q = d_q + eq[j] * g_all[j]
                d_k = d_k + ek[j] * h_all[j]
                db = db + (qt[:, sj].astype(F32) * g_all[j] - kt[:, sj].astype(F32) * h_all[j])

            db_last = (jnp.sum(k * d_kst, axis=0, keepdims=True)
                       + ebl * jnp.sum(st * dst, axis=0, keepdims=True))
            ds_scr[h] = dst * ebl + _dot(dob, qin.astype(BF16), TN)

            db = db + jnp.where(row == CHUNK - 1, db_last, 0.0)
            dg = _tri_matmul(triu, db)
            d_f = dg / f - d_k
            dp_ref[:, out(1)] = (d_f * (1.0 - lb) * sig * (1.0 - sig)).astype(BF16)
            gsm_ref[0:1, sl] += jnp.sum(d_f * (1.0 - sig), axis=0, keepdims=True) * (lb * (1.0 - lb))
            dp_ref[:, out(0)] = (d_q * (sq * (1.0 + q * (1.0 - sq)))).astype(BF16)

        out_copy(slot, nc - 1 - step).start()

        @pl.when(step == nc - 1)
        def _():
            out_copy(1 - slot, 1).wait()
            out_copy(slot, 0).wait()

    rc = lambda c: nc - 1 - c
    col = lambda j: pl.BlockSpec((CHUNK, D_MODEL), lambda c: (rc(c), j))
    par = lambda rows: pl.BlockSpec((rows, D_MODEL), lambda c: (0, 0))
    return pl.pallas_call(
        body, name="hgrn_bwd", grid=(nc,),
        in_specs=[col(2), col(3), col(4), col(5), col(0),
                  pl.BlockSpec((1, N_HEADS, HEAD_D, HEAD_D), lambda c: (rc(c), 0, 0, 0)),
                  col(1), par(2), par(1), ANY],
        out_specs=[ANY, par(8)],
        out_shape=[SDS((seq, D_IN), BF16), SDS((8, D_MODEL), F32)],
        input_output_aliases={9: 0},
        scratch_shapes=[pltpu.VMEM((N_HEADS, HEAD_D, HEAD_D), F32),
                        pltpu.VMEM((2, CHUNK, 4 * D_MODEL), BF16), pltpu.SemaphoreType.DMA((2,))],
        compiler_params=_cparams(("arbitrary",)),
    )(p, p, p, p, o, states, dymix, lbl, gw, dp_full)


def _out_proj(yl, yh, wo, x, tgt, post_w, seq):
    tm = 256

    def body(yl_ref, yh_ref, wo_ref, x_ref, tg_ref, pw_ref, dymix_ref, dout_ref, gwo_ref, st_ref):
        @pl.when(pl.program_id(0) == 0)
        def _():
            gwo_ref[...] = jnp.zeros_like(gwo_ref)
            st_ref[...] = jnp.zeros_like(st_ref)

        ylv = yl_ref[...]
        yhv = yh_ref[...]
        y = _dot(ylv, wo_ref[0:D_MODEL, :]) + _dot(yhv, wo_ref[D_MODEL:D_MIX, :])
        r2 = lax.rsqrt(jnp.mean(y * y, axis=-1, keepdims=True) + EPS)
        yn = y * r2
        pw = pw_ref[...]
        e = (x_ref[...] + yn * pw) - tg_ref[...]
        st_ref[1:2, :] += jnp.sum(e * e, axis=0, keepdims=True) * (0.5 / D_MODEL)
        dout = e * (1.0 / D_MODEL)
        dout_ref[...] = dout
        st_ref[0:1, :] += jnp.sum(dout * yn, axis=0, keepdims=True)
        dyn = dout * pw
        dy = r2 * (dyn - yn * jnp.mean(dyn * yn, axis=-1, keepdims=True))
        dyb = dy.astype(BF16)
        dymix_ref[...] = _dot(dyb, wo_ref[...], NT)
        gwo_ref[0:D_MODEL, :] += _dot(ylv, dyb, TN)
        gwo_ref[D_MODEL:D_MIX, :] += _dot(yhv, dyb, TN)

    row = lambda w: pl.BlockSpec((tm, w), lambda m: (m, 0))
    full = lambda shape: pl.BlockSpec(shape, lambda m: (0,) * len(shape))
    return pl.pallas_call(
        body, name="out_proj", grid=(seq // tm,),
        in_specs=[row(D_MODEL), row(D_MODEL), full((D_MIX, D_MODEL)), row(D_MODEL), row(D_MODEL),
                  full((1, D_MODEL))],
        out_specs=[row(D_MIX), row(D_MODEL), full((D_MIX, D_MODEL)), full((8, D_MODEL))],
        out_shape=[SDS((seq, D_MIX), F32), SDS((seq, D_MODEL), F32), SDS((D_MIX, D_MODEL), F32),
                   SDS((8, D_MODEL), F32)],
        compiler_params=_cparams(("arbitrary",)),
    )(yl, yh, wo, x, tgt, post_w)


MESH = pl.DeviceIdType.MESH
ANY = pl.BlockSpec(memory_space=pl.ANY)
EXCHANGE_SEMS = [pltpu.SemaphoreType.DMA((N_DEV - 1,)), pltpu.SemaphoreType.DMA((N_DEV - 1,)),
                 pltpu.SemaphoreType.DMA(())]


def _mesh_pos():
    return lax.axis_index("x"), lax.axis_index("y"), lax.axis_index("c")


class _SlotExchange:
    def __init__(self, src_ref, dst_ref, send_sems, recv_sems, local_sem, blocked):
        x, y, c = _mesh_pos()
        me = 4 * x + 2 * y + c
        src = (lambda dest: src_ref.at[dest]) if blocked else (lambda dest: src_ref)
        self.local = pltpu.make_async_copy(src(me), dst_ref.at[me], local_sem)
        self.sends, self.recvs = [], []
        for k in range(1, N_DEV):
            px = 1 - x if (k >> 2) & 1 else x
            py = 1 - y if (k >> 1) & 1 else y
            pc = 1 - c if k & 1 else c
            peer = 4 * px + 2 * py + pc
            sems = dict(send_sem=send_sems.at[k - 1], recv_sem=recv_sems.at[k - 1],
                        device_id=(px, py, pc), device_id_type=MESH)
            self.sends.append(pltpu.make_async_remote_copy(src_ref=src(peer), dst_ref=dst_ref.at[me], **sems))
            self.recvs.append(pltpu.make_async_remote_copy(src_ref=dst_ref.at[peer], dst_ref=dst_ref.at[peer], **sems))

    def start(self):
        self.local.start()
        for cp in self.sends:
            cp.start()

    def wait(self):
        for cp in self.recvs:
            cp.wait_recv()
        for cp in self.sends:
            cp.wait_send()
        self.local.wait()


def _grad_w_in(u, dp, g_w_out, seq):
    tk = 512
    nk = seq // tk

    def body(u_ref, dp_ref, gsrc_ref, g_ref, recv_ref, send_sems, recv_sems, local_sem):
        n = pl.program_id(0)
        k = pl.program_id(1)
        ex = _SlotExchange(gsrc_ref, recv_ref, send_sems, recv_sems, local_sem, blocked=True)

        @pl.when((n == 0) & (k == 0))
        def _():
            ex.start()

        @pl.when(k == 0)
        def _():
            g_ref[...] = jnp.zeros_like(g_ref)

        g_ref[0] += _dot(u_ref[...], dp_ref[...], TN)

        @pl.when((n == N_DEV - 1) & (k == nk - 1))
        def _():
            ex.wait()

    return pl.pallas_call(
        body, name="grad_w_in", grid=(N_DEV, nk),
        in_specs=[pl.BlockSpec((tk, D_MODEL), lambda n, k: (k, 0)),
                  pl.BlockSpec((tk, W_BLK), lambda n, k: (k, n)), ANY],
        out_specs=[pl.BlockSpec((1, D_MODEL, W_BLK), lambda n, k: (n, 0, 0)), ANY],
        out_shape=[SDS((N_DEV, D_MODEL, W_BLK), F32), SDS(g_w_out.shape, F32)],
        scratch_shapes=EXCHANGE_SEMS,
        compiler_params=_cparams(("arbitrary", "arbitrary")),
    )(u, dp, g_w_out)


def _grad_x(dp, w_all, x, pre_w, dout, g_w_in, seq):
    tm = 512
    nm = seq // tm

    def body(dp_ref, w_ref, x_ref, pw_ref, do_ref, gsrc_ref, gx_ref, gpw_ref, recv_ref, acc,
             send_sems, recv_sems, local_sem):
        m = pl.program_id(0)
        j = pl.program_id(1)
        ex = _SlotExchange(gsrc_ref, recv_ref, send_sems, recv_sems, local_sem, blocked=True)

        @pl.when((m == 0) & (j == 0))
        def _():
            ex.start()
            gpw_ref[...] = jnp.zeros_like(gpw_ref)

        @pl.when(j == 0)
        def _():
            acc[...] = jnp.zeros_like(acc)

        acc[...] += _dot(dp_ref[...], w_ref[0], NT)

        @pl.when(j == N_DEV - 1)
        def _():
            xv = x_ref[...]
            r1 = lax.rsqrt(jnp.mean(xv * xv, axis=-1, keepdims=True) + EPS)
            xn = xv * r1
            du = acc[...]
            gpw_ref[0:1, :] += jnp.sum(du * xn, axis=0, keepdims=True)
            dxn = du * pw_ref[...]
            gx_ref[...] = r1 * (dxn - xn * jnp.mean(dxn * xn, axis=-1, keepdims=True)) + do_ref[...]

        @pl.when((m == nm - 1) & (j == N_DEV - 1))
        def _():
            ex.wait()

    return pl.pallas_call(
        body, name="grad_x", grid=(nm, N_DEV),
        in_specs=[pl.BlockSpec((tm, W_BLK), lambda m, j: (m, j)),
                  pl.BlockSpec((1, D_MODEL, W_BLK), lambda m, j: (j, 0, 0)),
                  pl.BlockSpec((tm, D_MODEL), lambda m, j: (m, 0)),
                  pl.BlockSpec((1, D_MODEL), lambda m, j: (0, 0)),
                  pl.BlockSpec((tm, D_MODEL), lambda m, j: (m, 0)), ANY],
        out_specs=[pl.BlockSpec((tm, D_MODEL), lambda m, j: (m, 0)),
                   pl.BlockSpec((8, D_MODEL), lambda m, j: (0, 0)), ANY],
        out_shape=[SDS((seq, D_MODEL), F32), SDS((8, D_MODEL), F32), SDS(g_w_in.shape, F32)],
        scratch_shapes=[pltpu.VMEM((tm, D_MODEL), F32)] + EXCHANGE_SEMS,
        compiler_params=_cparams(("arbitrary", "arbitrary")),
    )(dp, w_all, x, pre_w, dout, g_w_in)


def _local_step(x, tgt, pre_w, w_in_all, conv_w, conv_b, wa, wx, ba, bx, lam, lbl, gnorm_w, w_out, post_w):
    seq = x.shape[0]
    p, u = _in_proj(x, pre_w, w_in_all, seq)
    h, y_lru = _lru_forward(p, conv_w, conv_b, wa, wx, ba, bx, lam, seq)
    y_hgrn, o, states = _hgrn_forward(p, lbl, gnorm_w, seq)
    dymix, dout, g_w_out, stats = _out_proj(y_lru, y_hgrn, w_out, x, tgt, post_w, seq)
    dp_lru, g_wa, g_wx, lru_small = _lru_backward(p, h, dymix, conv_w, conv_b, wa, wx, ba, bx, lam, seq)
    dp, hgrn_small = _hgrn_backward(p, o, states, dymix, lbl, gnorm_w, dp_lru, seq)
    return dict(u=u, dp=dp, dout=dout, g_w_out=g_w_out, g_wa=g_wa, g_wx=g_wx,
                lru_small=lru_small, hgrn_small=hgrn_small, stats=stats)


def _all_gather(parts):
    n = len(parts)

    def body(*refs):
        ins, outs = refs[:n], refs[n:2 * n]
        send_sems, recv_sems, local_sems = refs[2 * n:]
        x, y, c = _mesh_pos()
        me, sibling = (x, y, c), (x, y, 1 - c)
        chips = [(1 - x, y), (x, 1 - y), (1 - x, 1 - y)]

        def slot(a, pos):
            return outs[a].at[4 * pos[0] + 2 * pos[1] + pos[2]]

        def copy(a, k, block, to, src=None):
            dst = slot(a, block)
            return pltpu.make_async_remote_copy(
                src_ref=dst if src is None else src, dst_ref=dst,
                send_sem=send_sems.at[a, k], recv_sem=recv_sems.at[a, k],
                device_id=to, device_id_type=MESH)

        mine = [pltpu.make_async_copy(ins[a], slot(a, me), local_sems.at[a]) for a in range(n)]
        for cp in mine:
            cp.start()
        first = []
        for a in range(n):
            first.append(copy(a, 0, me, sibling, src=ins[a]))
            first += [copy(a, 1 + j, me, (*chip, c), src=ins[a]) for j, chip in enumerate(chips)]
        for cp in first:
            cp.start()
        passed = []
        for j, chip in enumerate(chips):
            for a in range(n):
                copy(a, 1 + j, (*chip, c), me).wait_recv()
                fwd = copy(a, 4 + j, (*chip, c), sibling)
                fwd.start()
                passed.append(fwd)
        for a in range(n):
            copy(a, 0, sibling, me).wait_recv()
            for j, chip in enumerate(chips):
                copy(a, 4 + j, (*chip, 1 - c), me).wait_recv()
        for cp in first + passed:
            cp.wait_send()
        for cp in mine:
            cp.wait()

    return pl.pallas_call(
        body, name="gather_weights",
        out_shape=[SDS((N_DEV,) + p.shape, p.dtype) for p in parts],
        in_specs=[ANY] * n, out_specs=[ANY] * n,
        scratch_shapes=[pltpu.SemaphoreType.DMA((n, 7)), pltpu.SemaphoreType.DMA((n, 7)),
                        pltpu.SemaphoreType.DMA((n,))],
    )(*parts)


def _exchange_grads(blocks, repl):
    nb = len(blocks)
    n = nb + 1

    def body(*refs):
        ins, outs, sems = refs[:n], refs[n:2 * n], refs[2 * n:]
        exs = [_SlotExchange(ins[a], outs[a], *sems[3 * a:3 * a + 3], blocked=a < nb) for a in range(n)]
        for ex in exs:
            ex.start()
        for ex in exs:
            ex.wait()

    arrs = list(blocks) + [repl]
    shapes = [SDS(b.shape, b.dtype) for b in blocks] + [SDS((N_DEV,) + repl.shape, repl.dtype)]
    return pl.pallas_call(
        body, name="exchange_small", out_shape=shapes,
        in_specs=[ANY] * n, out_specs=[ANY] * n,
        scratch_shapes=EXCHANGE_SEMS * n,
    )(*arrs)


def _pack_rows(picks, name):
    arrs = [p[0] for p in picks]

    def body(*refs):
        out = refs[-1]
        out[...] = jnp.zeros_like(out)
        at = 0
        for ref, (_, row, rows, scale) in zip(refs[:-1], picks):
            out[at:at + rows, :] = ref[row:row + rows, :] * scale
            at += rows

    return pl.pallas_call(body, name=name, out_shape=SDS((8, D_MODEL), F32))(*arrs)


def _adamw(g, w, m, v):
    m2 = ADAM_B1 * m + (1.0 - ADAM_B1) * g
    v2 = ADAM_B2 * v + (1.0 - ADAM_B2) * (g * g)
    m_hat = m2 / (1.0 - ADAM_B1 ** ADAM_STEP)
    v_hat = v2 / (1.0 - ADAM_B2 ** ADAM_STEP)
    delta = -ADAM_LR * (m_hat / (jnp.sqrt(v_hat) + ADAM_EPS) + ADAM_WD * w)
    return delta, m2, v2


def _sum_slots(r_ref):
    g = r_ref[0]
    for s in range(1, N_DEV):
        g = g + r_ref[s]
    return g


def _sum_adamw(recv, w, m, v, tr, name):
    rows, cols = w.shape

    def body(r_ref, w_ref, m_ref, v_ref, g_ref, d_ref, m2_ref, v2_ref):
        g = _sum_slots(r_ref)
        g_ref[...] = g
        d_ref[...], m2_ref[...], v2_ref[...] = _adamw(g, w_ref[...], m_ref[...], v_ref[...])

    blk = pl.BlockSpec((tr, cols), lambda i: (i, 0))
    return pl.pallas_call(
        body, name=name, grid=(rows // tr,),
        in_specs=[pl.BlockSpec((N_DEV, tr, cols), lambda i: (0, i, 0)), blk, blk, blk],
        out_specs=[blk] * 4, out_shape=[SDS((rows, cols), F32)] * 4,
        compiler_params=_cparams(("parallel",)),
    )(recv, w, m, v)


def _sum_adamw_repl(recv, w, m, v):
    def body(r_ref, w_ref, m_ref, v_ref, g_ref, d_ref, m2_ref, v2_ref, loss_ref):
        g = _sum_slots(r_ref)
        g_ref[...] = g
        d_ref[...], m2_ref[...], v2_ref[...] = _adamw(g, w_ref[...], m_ref[...], v_ref[...])
        total = jnp.sum(g[RP_LOSS:RP_LOSS + 1, :], axis=-1, keepdims=True)
        loss_ref[...] = jnp.broadcast_to(total, loss_ref.shape)

    return pl.pallas_call(
        body, name="adamw_repl",
        out_shape=[SDS((8, D_MODEL), F32)] * 4 + [SDS((8, 128), F32)],
    )(recv, w, m, v)


def _shard_rows(t, lead):
    r = t.shape[1] // N_DEV
    t = t.reshape((lead, N_DEV, r) + t.shape[2:])
    return jnp.moveaxis(t, 1, 0)


def _pad8(t):
    return jnp.pad(t, ((0, 0), (0, 8 - t.shape[1]), (0, 0)))


def _pack_small(wa, wx, cw, b_a, b_x):
    n = wa.shape[0]
    return jnp.concatenate([
        wa.reshape(n, 256, 128), wx.reshape(n, 256, 128), _pad8(cw),
        _pad8(b_a.reshape(n, 1, 128)), _pad8(b_x.reshape(n, 1, 128))], axis=1)


def _unpack_small(t):
    n = t.shape[0]
    return (t[:, SM_WA:SM_WA + 256].reshape(n, 4, 32, 256), t[:, SM_WX:SM_WX + 256].reshape(n, 4, 32, 256),
            t[:, SM_CW:SM_CW + 4], t[:, SM_BA].reshape(n, 4, 32), t[:, SM_BX].reshape(n, 4, 32))


def kernel(x, pre_norm_w, w_in, conv_w, conv_b, lru_w_a, lru_b_a, lru_w_x, lru_b_x, lru_lambda, hgrn_lb_logits, hgrn_gnorm_w, w_out, post_norm_w, loss_target, m_pre_norm_w, m_w_in, m_conv_w, m_conv_b, m_lru_w_a, m_lru_b_a, m_lru_w_x, m_lru_b_x, m_lru_lambda, m_hgrn_lb_logits, m_hgrn_gnorm_w, m_w_out, m_post_norm_w, v_pre_norm_w, v_w_in, v_conv_w, v_conv_b, v_lru_w_a, v_lru_b_a, v_lru_w_x, v_lru_b_x, v_lru_lambda, v_hgrn_lb_logits, v_hgrn_gnorm_w, v_w_out, v_post_norm_w):
    seq = x.shape[1]
    x2 = x.reshape(seq, D_MODEL)
    tgt = loss_target.reshape(seq, D_MODEL)

    small_w = _pack_small(lru_w_a, lru_w_x, conv_w, lru_b_a, lru_b_x)[0]
    w_in_all, w_out_all, small_all = _all_gather([w_in[0].astype(BF16), w_out[0].astype(BF16), small_w])
    wa_s, wx_s, cw_s, ba_s, bx_s = _unpack_small(small_all)
    wa = jnp.moveaxis(wa_s, 0, 1).reshape(LRU_BLOCKS, LRU_BW, LRU_BW).astype(BF16)
    wx = jnp.moveaxis(wx_s, 0, 1).reshape(LRU_BLOCKS, LRU_BW, LRU_BW).astype(BF16)
    cw = jnp.moveaxis(cw_s, 0, 1).reshape(4, D_MODEL)
    ba = jnp.moveaxis(ba_s, 0, 1).reshape(1, D_MODEL)
    bx = jnp.moveaxis(bx_s, 0, 1).reshape(1, D_MODEL)

    loc = _local_step(x2, tgt, pre_norm_w, w_in_all, cw, conv_b, wa, wx, ba, bx, lru_lambda,
                      hgrn_lb_logits, hgrn_gnorm_w, w_out_all.reshape(D_MIX, D_MODEL), post_norm_w)

    g_w_in, r_out = _grad_w_in(loc["u"], loc["dp"], loc["g_w_out"].reshape(N_DEV, D_MIX // N_DEV, D_MODEL), seq)
    grad_x, pre_small, r_in = _grad_x(loc["dp"], w_in_all, x2, pre_norm_w, loc["dout"], g_w_in, seq)
    ls = loc["lru_small"]
    g_small = _pack_small(_shard_rows(loc["g_wa"], LRU_BLOCKS), _shard_rows(loc["g_wx"], LRU_BLOCKS),
                          _shard_rows(ls[0:4].reshape(4, D_MODEL, 1), 4).reshape(N_DEV, 4, 128),
                          _shard_rows(ls[5].reshape(4, LRU_BW, 1), 4).reshape(N_DEV, 4, 32),
                          _shard_rows(ls[6].reshape(4, LRU_BW, 1), 4).reshape(N_DEV, 4, 32))
    g_repl = _pack_rows([(pre_small, 0, 1, 1.0), (ls, 4, 1, 1.0), (ls, 7, 1, 1.0),
                         (loc["hgrn_small"], 0, 1, 1.0), (loc["hgrn_small"], 0, 1, -1.0),
                         (loc["hgrn_small"], 1, 1, 1.0), (loc["stats"], 0, 2, 1.0)], "pack_grads")
    r_small, r_repl = _exchange_grads([g_small], g_repl)

    zero = jnp.zeros((1, D_MODEL), F32)
    pack_w = lambda a, b, c_, d, e, f, name: _pack_rows(
        [(a, 0, 1, 1.0), (b, 0, 1, 1.0), (c_, 0, 1, 1.0), (d, 0, 2, 1.0), (e, 0, 1, 1.0), (f, 0, 1, 1.0),
         (zero, 0, 1, 1.0)], name)
    w_repl = pack_w(pre_norm_w, conv_b, lru_lambda, hgrn_lb_logits, hgrn_gnorm_w, post_norm_w, "pack_w")
    m_repl = pack_w(m_pre_norm_w, m_conv_b, m_lru_lambda, m_hgrn_lb_logits, m_hgrn_gnorm_w, m_post_norm_w, "pack_m")
    v_repl = pack_w(v_pre_norm_w, v_conv_b, v_lru_lambda, v_hgrn_lb_logits, v_hgrn_gnorm_w, v_post_norm_w, "pack_v")
    o_repl = _sum_adamw_repl(r_repl, w_repl, m_repl, v_repl)
    loss = o_repl[4][0, 0]

    o_in = _sum_adamw(r_in, w_in[0], m_w_in[0], v_w_in[0], 128, "adamw_w_in")
    o_out = _sum_adamw(r_out, w_out[0], m_w_out[0], v_w_out[0], 64, "adamw_w_out")
    o_small = _sum_adamw(r_small,
                         _pack_small(lru_w_a, lru_w_x, conv_w, lru_b_a, lru_b_x)[0],
                         _pack_small(m_lru_w_a, m_lru_w_x, m_conv_w, m_lru_b_a, m_lru_b_x)[0],
                         _pack_small(v_lru_w_a, v_lru_w_x, v_conv_w, v_lru_b_a, v_lru_b_x)[0],
                         SM_ROWS, "adamw_small")

    outs = [loss, grad_x.reshape(x.shape)]
    for kind in range(4):
        rp = o_repl[kind]
        swa, swx, scw, sba, sbx = _unpack_small(o_small[kind][None])
        outs += [rp[RP_PRE:RP_PRE + 1], o_in[kind][None], scw, rp[RP_CB:RP_CB + 1], swa, sba, swx, sbx,
                 rp[RP_LAM:RP_LAM + 1], rp[RP_LB0:RP_LB1 + 1], rp[RP_GN:RP_GN + 1], o_out[kind][None],
                 rp[RP_POST:RP_POST + 1]]
    return tuple(outs)
```

```python
import functools

import jax
import jax.numpy as jnp
from jax import lax
from jax.experimental import pallas as pl
from jax.experimental.pallas import tpu as pltpu

F32 = jnp.float32
BF16 = jnp.bfloat16
SDS = jax.ShapeDtypeStruct

D_MODEL = 1024
D_IN = 6144
N_DEV = 8
N_CHIPS = 4
W_BLK = D_IN // N_DEV
D_MIX = 2048
LRU_BLOCKS = 4
LRU_BW = 256
LRU_C = 8.0
N_HEADS = 8
HEAD_D = 128
CHUNK = 128
SUB = 32
N_SUB = CHUNK // SUB
EXP_CLAMP = 80.0
EPS = 1e-6

ADAM_LR = 0.001
ADAM_B1 = 0.9
ADAM_B2 = 0.999
ADAM_EPS = 1e-08
ADAM_WD = 0.01
ADAM_STEP = 10

VMEM_LIMIT = 56 * 1024 * 1024

NN = (((1,), (0,)), ((), ()))
NT = (((1,), (1,)), ((), ()))
TN = (((0,), (0,)), ((), ()))

SM_WA = 0
SM_WX = 256
SM_CW = 512
SM_BA = 520
SM_BX = 528
SM_ROWS = 536

RP_PRE, RP_CB, RP_LAM, RP_LB0, RP_LB1, RP_GN, RP_POST, RP_LOSS = range(8)


def _dot(a, b, dims=NN):
    return lax.dot_general(a, b, dims, preferred_element_type=F32)


def _sigmoid(x):
    return 0.5 * jnp.tanh(0.5 * x) + 0.5


def _sigmoid_pos(x):
    return 1.0 / (1.0 + jnp.exp(-x))


def _cparams(sem, vmem=VMEM_LIMIT):
    return pltpu.CompilerParams(dimension_semantics=sem, vmem_limit_bytes=vmem)


def _iota(shape, axis):
    return lax.broadcasted_iota(jnp.int32, shape, axis)


def _in_proj(x, pre_w, w_all, seq):
    tm = min(1024, seq)

    def body(x_ref, pw_ref, w_ref, p_ref, u_ref):
        @pl.when(pl.program_id(1) == 0)
        def _():
            xv = x_ref[...]
            r = lax.rsqrt(jnp.mean(xv * xv, axis=-1, keepdims=True) + EPS)
            u_ref[...] = (xv * r * pw_ref[...]).astype(BF16)

        p_ref[...] = _dot(u_ref[...], w_ref[0])

    return pl.pallas_call(
        body, name="in_proj", grid=(seq // tm, N_DEV),
        in_specs=[pl.BlockSpec((tm, D_MODEL), lambda m, n: (m, 0)),
                  pl.BlockSpec((1, D_MODEL), lambda m, n: (0, 0)),
                  pl.BlockSpec((1, D_MODEL, W_BLK), lambda m, n: (n, 0, 0))],
        out_specs=[pl.BlockSpec((tm, W_BLK), lambda m, n: (m, n)),
                   pl.BlockSpec((tm, D_MODEL), lambda m, n: (m, 0))],
        out_shape=[SDS((seq, D_IN), F32), SDS((seq, D_MODEL), BF16)],
        compiler_params=_cparams(("parallel", "arbitrary")),
    )(x, pre_w, w_all)


def _softplus_neg(lam):
    z = -lam
    e = jnp.exp(-jnp.abs(z))
    u = 1.0 + e
    log1p_e = jnp.where(u == 1.0, e, jnp.log(u) * (e / (u - 1.0)))
    sp = jnp.maximum(z, 0.0) + log1p_e
    dsp = -jnp.where(z >= 0.0, 1.0 / u, e / u)
    return sp, dsp


def _neg_expm1(x):
    poly = x * (1.0 + x * (1.0 / 2 + x * (1.0 / 6 + x * (1.0 / 24 + x * (1.0 / 120)))))
    return jnp.where(x > -1.0 / 16, -poly, 1.0 - jnp.exp(x))


def _conv_taps(lx, prev8, cw_ref, cb_ref, tile):
    xc = cb_ref[...] + cw_ref[3:4, :] * lx
    for j in (1, 2, 3):
        xc = xc + cw_ref[3 - j:4 - j, :] * pltpu.roll(lx, j, 0)
    row8 = _iota((8, D_MODEL), 0)
    last8 = lx[tile - 8:tile, :]
    fix = jnp.zeros((8, D_MODEL), F32)
    for j in (1, 2, 3):
        wrong = pltpu.roll(last8, j, 0)
        right = pltpu.roll(prev8, j, 0)
        fix = fix + cw_ref[3 - j:4 - j, :] * jnp.where(row8 < j, right - wrong, 0.0)
    return xc, fix


def _lru_gates(xcs, wa, wx, ba, bx, sp):
    xb = xcs.astype(BF16)
    r = _sigmoid_pos(_dot(xb, wa) + ba)
    i = _sigmoid(_dot(xb, wx) + bx)
    la = (-LRU_C * sp) * r
    a = jnp.exp(la)
    one_minus_a2 = _neg_expm1(2.0 * la)
    return r, i, a, one_minus_a2


def _lru_forward(p, conv_w, conv_b, wa, wx, ba, bx, lam, seq):
    tile = min(512, seq // 2)
    nblk = tile // 8

    def body(lx_ref, gt_ref, cw_ref, cb_ref, wa_ref, wx_ref, ba_ref, bx_ref, lam_ref,
             h_ref, y_ref, prev8, hcar, xc_scr, a_scr, u_scr):
        @pl.when(pl.program_id(0) == 0)
        def _():
            prev8[...] = jnp.zeros_like(prev8)
            hcar[...] = jnp.zeros_like(hcar)

        lx = lx_ref[...]
        xc, fix = _conv_taps(lx, prev8[...], cw_ref, cb_ref, tile)
        xc_scr[...] = xc
        xc_scr[0:8, :] = xc_scr[0:8, :] + fix
        prev8[...] = lx_ref[tile - 8:tile, :]
        sp, _ = _softplus_neg(lam_ref[...])
        for n in range(LRU_BLOCKS):
            sl = slice(n * LRU_BW, (n + 1) * LRU_BW)
            xcs = xc_scr[:, sl]
            _, i, a, ne = _lru_gates(xcs, wa_ref[n], wx_ref[n], ba_ref[:, sl], bx_ref[:, sl], sp[:, sl])
            a_scr[:, sl] = a
            u_scr[:, sl] = jnp.sqrt(ne) * (i * xcs)

        row8 = _iota((8, D_MODEL), 0)

        def blk(j, hc):
            off = pl.multiple_of(j * 8, 8)
            a = a_scr[pl.ds(off, 8), :]
            u = u_scr[pl.ds(off, 8), :]
            for k in (1, 2, 4):
                m = row8 >= k
                u = jnp.where(m, u + a * pltpu.roll(u, k, 0), u)
                a = jnp.where(m, a * pltpu.roll(a, k, 0), a)
            h = u + a * hc
            h_ref[pl.ds(off, 8), :] = h
            return jnp.broadcast_to(h[7:8, :], (8, D_MODEL))

        hcar[...] = lax.fori_loop(0, nblk, blk, hcar[...])
        g = gt_ref[...]
        y_ref[...] = (h_ref[...] * (g * _sigmoid(g))).astype(BF16)

    full = lambda shape: pl.BlockSpec(shape, lambda t: (0,) * len(shape))
    return pl.pallas_call(
        body, name="lru_fwd", grid=(seq // tile,),
        in_specs=[pl.BlockSpec((tile, D_MODEL), lambda t: (t, 0)),
                  pl.BlockSpec((tile, D_MODEL), lambda t: (t, 1)),
                  full((4, D_MODEL)), full((1, D_MODEL)),
                  full((LRU_BLOCKS, LRU_BW, LRU_BW)), full((LRU_BLOCKS, LRU_BW, LRU_BW)),
                  full((1, D_MODEL)), full((1, D_MODEL)), full((1, D_MODEL))],
        out_specs=[pl.BlockSpec((tile, D_MODEL), lambda t: (t, 0)),
                   pl.BlockSpec((tile, D_MODEL), lambda t: (t, 0))],
        out_shape=[SDS((seq, D_MODEL), F32), SDS((seq, D_MODEL), BF16)],
        scratch_shapes=[pltpu.VMEM((8, D_MODEL), F32), pltpu.VMEM((8, D_MODEL), F32),
                        pltpu.VMEM((tile, D_MODEL), F32), pltpu.VMEM((tile, D_MODEL), F32),
                        pltpu.VMEM((tile, D_MODEL), F32)],
        compiler_params=_cparams(("arbitrary",)),
    )(p, p, conv_w, conv_b, wa, wx, ba, bx, lam)


def _lru_backward(p, h, dymix, conv_w, conv_b, wa, wx, ba, bx, lam, seq):
    tile = min(256, seq // 2)
    nt = seq // tile
    nblk = tile // 8
    t8 = tile // 8

    def body(lx_ref, lxh_ref, gt_ref, h_ref, hh_ref, dy_ref, cw_ref, cb_ref, wa_ref, wx_ref, ba_ref,
             bx_ref, lam_ref, dp_ref, gwa_ref, gwx_ref, gsm_ref,
             lamcar, anext, dxc8, xc_scr, r_scr, i_scr, a_scr, m_scr, rm_scr, c_scr, l_scr, dxc_scr):
        step = pl.program_id(0)
        first_tile = step == nt - 1

        @pl.when(step == 0)
        def _():
            lamcar[...] = jnp.zeros_like(lamcar)
            anext[...] = jnp.zeros_like(anext)
            dxc8[...] = jnp.zeros_like(dxc8)
            gwa_ref[...] = jnp.zeros_like(gwa_ref)
            gwx_ref[...] = jnp.zeros_like(gwx_ref)
            gsm_ref[...] = jnp.zeros_like(gsm_ref)

        keep = jnp.where(first_tile, 0.0, 1.0)
        lx = lx_ref[...]
        prev8 = lxh_ref[...] * keep
        xc, fix = _conv_taps(lx, prev8, cw_ref, cb_ref, tile)
        xc_scr[...] = xc
        xc_scr[0:8, :] = xc_scr[0:8, :] + fix
        sp, dsp = _softplus_neg(lam_ref[...])
        for n in range(LRU_BLOCKS):
            sl = slice(n * LRU_BW, (n + 1) * LRU_BW)
            r, i, a, ne = _lru_gates(xc_scr[:, sl], wa_ref[n], wx_ref[n], ba_ref[:, sl], bx_ref[:, sl],
                                     sp[:, sl])
            r_scr[:, sl] = r
            i_scr[:, sl] = i
            a_scr[:, sl] = a
            m_scr[:, sl] = jnp.sqrt(ne)
            rm_scr[:, sl] = lax.rsqrt(ne)

        g = gt_ref[...]
        sg = _sigmoid(g)
        dy = dy_ref[...]
        hv = h_ref[...]
        dp_ref[:, D_MODEL:2 * D_MODEL] = (dy * hv * (sg * (1.0 + g * (1.0 - sg)))).astype(BF16)

        rowt = _iota((tile, D_MODEL), 0)
        av = a_scr[...]
        l_scr[...] = dy * (g * sg)
        c_scr[...] = jnp.where(rowt == tile - 1, anext[...][0:1, :], pltpu.roll(av, tile - 1, 0))
        anext[...] = jnp.broadcast_to(av[0:1, :], (8, D_MODEL))
        row8 = _iota((8, D_MODEL), 0)

        def blk(jj, lc):
            off = pl.multiple_of((nblk - 1 - jj) * 8, 8)
            c = c_scr[pl.ds(off, 8), :]
            u = l_scr[pl.ds(off, 8), :]
            for k in (1, 2, 4):
                m = row8 < 8 - k
                u = jnp.where(m, u + c * pltpu.roll(u, 8 - k, 0), u)
                c = jnp.where(m, c * pltpu.roll(c, 8 - k, 0), c)
            lamv = u + c * lc
            l_scr[pl.ds(off, 8), :] = lamv
            return jnp.broadcast_to(lamv[0:1, :], (8, D_MODEL))

        lamcar[...] = lax.fori_loop(0, nblk, blk, lamcar[...])

        hprev = jnp.where(rowt == 0, hh_ref[...][7:8, :] * keep, pltpu.roll(hv, 1, 0))
        for n in range(LRU_BLOCKS):
            sl = slice(n * LRU_BW, (n + 1) * LRU_BW)
            lamv = l_scr[:, sl]
            xcs = xc_scr[:, sl]
            r = r_scr[:, sl]
            i = i_scr[:, sl]
            a = a_scr[:, sl]
            mult = m_scr[:, sl]
            d_la = lamv * hprev[:, sl] * a - (lamv * i * xcs) * (a * a * rm_scr[:, sl])
            d_pr = d_la * (-LRU_C * sp[:, sl]) * r * (1.0 - r)
            d_pi = (lamv * mult * xcs) * i * (1.0 - i)
            gsm_ref[7:8, sl] += jnp.sum(d_la * r, axis=0, keepdims=True) * (-LRU_C) * dsp[:, sl]
            gsm_ref[5:6, sl] += jnp.sum(d_pr, axis=0, keepdims=True)
            gsm_ref[6:7, sl] += jnp.sum(d_pi, axis=0, keepdims=True)
            xb = xcs.astype(BF16)
            prb = d_pr.astype(BF16)
            pib = d_pi.astype(BF16)
            gwa_ref[n] += _dot(xb, prb, TN)
            gwx_ref[n] += _dot(xb, pib, TN)
            dxc_scr[:, sl] = lamv * mult * i + _dot(prb, wa_ref[n], NT) + _dot(pib, wx_ref[n], NT)

        dxc = dxc_scr[...]
        gsm_ref[4:5, :] += jnp.sum(dxc, axis=0, keepdims=True)
        last8 = lx[tile - 8:tile, :]
        first8 = dxc[0:8, :]
        dlx = cw_ref[3:4, :] * dxc
        gsm_ref[3:4, :] += jnp.sum(dxc * lx, axis=0, keepdims=True)
        fix = jnp.zeros((8, D_MODEL), F32)
        for j in (1, 2, 3):
            w = cw_ref[3 - j:4 - j, :]
            dlx = dlx + w * pltpu.roll(dxc, tile - j, 0)
            fix = fix + w * jnp.where(row8 + j >= 8,
                                      pltpu.roll(dxc8[...], 8 - j, 0) - pltpu.roll(first8, 8 - j, 0), 0.0)
            halo = jnp.where(row8 < j, pltpu.roll(prev8, j, 0) - pltpu.roll(last8, j, 0), 0.0)
            gsm_ref[3 - j:4 - j, :] += (jnp.sum(dxc * pltpu.roll(lx, j, 0), axis=0, keepdims=True)
                                        + jnp.sum(first8 * halo, axis=0, keepdims=True))
        dxc8[...] = first8
        dp_ref[:, 0:D_MODEL] = dlx.astype(BF16)
        top = tile - 8
        dp_ref[top:tile, 0:D_MODEL] = (dlx[top:tile, :] + fix).astype(BF16)

    rev = lambda t: (nt - 1 - t, 0)
    halo_idx = lambda t: (jnp.maximum((nt - 1 - t) * t8 - 1, 0), 0)
    full = lambda shape: pl.BlockSpec(shape, lambda t: (0,) * len(shape))
    big = lambda: pltpu.VMEM((tile, D_MODEL), F32)
    return pl.pallas_call(
        body, name="lru_bwd", grid=(nt,),
        in_specs=[pl.BlockSpec((tile, D_MODEL), rev),
                  pl.BlockSpec((8, D_MODEL), halo_idx),
                  pl.BlockSpec((tile, D_MODEL), lambda t: (nt - 1 - t, 1)),
                  pl.BlockSpec((tile, D_MODEL), rev),
                  pl.BlockSpec((8, D_MODEL), halo_idx),
                  pl.BlockSpec((tile, D_MODEL), rev),
                  full((4, D_MODEL)), full((1, D_MODEL)),
                  full((LRU_BLOCKS, LRU_BW, LRU_BW)), full((LRU_BLOCKS, LRU_BW, LRU_BW)),
                  full((1, D_MODEL)), full((1, D_MODEL)), full((1, D_MODEL))],
        out_specs=[pl.BlockSpec((tile, 2 * D_MODEL), rev),
                   full((LRU_BLOCKS, LRU_BW, LRU_BW)), full((LRU_BLOCKS, LRU_BW, LRU_BW)),
                   full((8, D_MODEL))],
        out_shape=[SDS((seq, D_IN), BF16), SDS((LRU_BLOCKS, LRU_BW, LRU_BW), F32),
                   SDS((LRU_BLOCKS, LRU_BW, LRU_BW), F32), SDS((8, D_MODEL), F32)],
        scratch_shapes=[pltpu.VMEM((8, D_MODEL), F32), pltpu.VMEM((8, D_MODEL), F32),
                        pltpu.VMEM((8, D_MODEL), F32)] + [big() for _ in range(9)],
        compiler_params=_cparams(("arbitrary",)),
    )(p, p, p, h, h, dymix, conv_w, conv_b, wa, wx, ba, bx, lam)


def _split3(g):
    hi = g.astype(BF16)
    r1 = g - hi.astype(F32)
    mid = r1.astype(BF16)
    lo = (r1 - mid.astype(F32)).astype(BF16)
    return hi, mid, lo


def _tri_matmul(tri, g):
    hi, mid, lo = _split3(g)
    return _dot(tri, lo) + _dot(tri, mid) + _dot(tri, hi)


def _hgrn_gate_terms(q, fr, lbl):
    lb = _sigmoid_pos(lbl[0:1, :] - lbl[1:2, :])
    sig = _sigmoid(fr)
    f = lb + (1.0 - lb) * sig
    sq = _sigmoid(q)
    return lb, sig, f, sq


def _hgrn_decay(bh):
    row = _iota((CHUNK, HEAD_D), 0)
    pieces = []
    for i in range(N_SUB):
        mid = 0.5 * (bh[i * SUB:i * SUB + 1, :] + bh[(i + 1) * SUB - 1:(i + 1) * SUB, :])
        pieces.append(jnp.broadcast_to(mid, (SUB, HEAD_D)))
    mu = jnp.concatenate(pieces, axis=0)
    eq = [jnp.exp(jnp.minimum(bh - mu, EXP_CLAMP))]
    ek = [jnp.exp(jnp.minimum(mu - bh, EXP_CLAMP))]
    for j in range(1, N_SUB):
        rho = bh[j * SUB - 1:j * SUB, :]
        eq.append(jnp.where(row >= j * SUB, jnp.exp(jnp.minimum(bh - rho, 0.0)), 0.0))
        ek.append(jnp.where((row >= (j - 1) * SUB) & (row < j * SUB),
                            jnp.exp(jnp.minimum(rho - bh, 0.0)), 0.0))
    blast = bh[CHUNK - 1:CHUNK, :]
    return eq, ek, jnp.exp(bh), jnp.exp(blast - bh), jnp.exp(blast)


def _hgrn_scores(qs, k, eq, ek):
    qt = jnp.concatenate([(qs * e).astype(BF16) for e in eq], axis=1)
    kt = jnp.concatenate([(k * e).astype(BF16) for e in ek], axis=1)
    r = _iota((CHUNK, CHUNK), 0)
    c = _iota((CHUNK, CHUNK), 1)
    diag = (jnp.right_shift(r, 5) == jnp.right_shift(c, 5)) & (c <= r)
    a = (jnp.where(diag, _dot(qt[:, :HEAD_D], kt[:, :HEAD_D], NT), 0.0)
         + _dot(qt[:, HEAD_D:], kt[:, HEAD_D:], NT))
    return a, qt, kt, diag


def _hgrn_forward(p, lbl, gw, seq):
    nc = seq // CHUNK
    assert SUB == 32

    def body(q_ref, f_ref, v_ref, hg_ref, lbl_ref, gw_ref, y_ref, o_ref, st_ref, s_scr):
        @pl.when(pl.program_id(0) == 0)
        def _():
            s_scr[...] = jnp.zeros_like(s_scr)

        r = _iota((CHUNK, CHUNK), 0)
        c = _iota((CHUNK, CHUNK), 1)
        tri = jnp.where(c <= r, 1.0, 0.0).astype(BF16)
        for h in range(N_HEADS):
            sl = slice(h * HEAD_D, (h + 1) * HEAD_D)
            q = q_ref[:, sl]
            _, _, f, sq = _hgrn_gate_terms(q, f_ref[:, sl], lbl_ref[:, sl])
            qs = q * sq
            k = 1.0 - f
            bh = _tri_matmul(tri, jnp.log(f))
            eq, ek, eb, ekst, ebl = _hgrn_decay(bh)
            a, _, _, _ = _hgrn_scores(qs, k, eq, ek)
            st = s_scr[h]
            st_ref[0, h] = st
            vb = v_ref[:, sl].astype(BF16)
            o = _dot(a.astype(BF16), vb) + _dot((qs * eb).astype(BF16), st.astype(BF16), NT)
            s_scr[h] = st * ebl + _dot(vb, (k * ekst).astype(BF16), TN)
            o_ref[:, sl] = o
            rs = lax.rsqrt(jnp.mean(o * o, axis=-1, keepdims=True) + EPS)
            hg = hg_ref[:, sl]
            y_ref[:, sl] = ((o * rs * gw_ref[:, sl]) * (hg * _sigmoid(hg))).astype(BF16)

    col = lambda j: pl.BlockSpec((CHUNK, D_MODEL), lambda c: (c, j))
    par = lambda rows: pl.BlockSpec((rows, D_MODEL), lambda c: (0, 0))
    return pl.pallas_call(
        body, name="hgrn_fwd", grid=(nc,),
        in_specs=[col(2), col(3), col(4), col(5), par(2), par(1)],
        out_specs=[col(0), col(0),
                   pl.BlockSpec((1, N_HEADS, HEAD_D, HEAD_D), lambda c: (c, 0, 0, 0))],
        out_shape=[SDS((seq, D_MODEL), BF16), SDS((seq, D_MODEL), F32),
                   SDS((nc, N_HEADS, HEAD_D, HEAD_D), F32)],
        scratch_shapes=[pltpu.VMEM((N_HEADS, HEAD_D, HEAD_D), F32)],
        compiler_params=_cparams(("arbitrary",)),
    )(p, p, p, p, lbl, gw)


def _hgrn_backward(p, o, states, dymix, lbl, gw, dp_full, seq):
    nc = seq // CHUNK

    def body(q_ref, f_ref, v_ref, hg_ref, o_ref, st_ref, dy_ref, lbl_ref, gw_ref, dpin_ref,
             dpo_ref, gsm_ref, ds_scr, dp_buf, dp_sems):
        del dpin_ref
        step = pl.program_id(0)
        slot = step % 2

        def out_copy(s, chunk):
            rows = pl.ds(pl.multiple_of(chunk * CHUNK, CHUNK), CHUNK)
            return pltpu.make_async_copy(dp_buf.at[s], dpo_ref.at[rows, pl.ds(2 * D_MODEL, 4 * D_MODEL)],
                                         dp_sems.at[s])

        @pl.when(step == 0)
        def _():
            ds_scr[...] = jnp.zeros_like(ds_scr)
            gsm_ref[...] = jnp.zeros_like(gsm_ref)

        @pl.when(step >= 2)
        def _():
            out_copy(slot, nc + 1 - step).wait()

        dp_ref = dp_buf.at[slot]

        r = _iota((CHUNK, CHUNK), 0)
        c = _iota((CHUNK, CHUNK), 1)
        tri = jnp.where(c <= r, 1.0, 0.0).astype(BF16)
        triu = jnp.where(c >= r, 1.0, 0.0).astype(BF16)
        row = _iota((CHUNK, HEAD_D), 0)
        for h in range(N_HEADS):
            sl = slice(h * HEAD_D, (h + 1) * HEAD_D)
            out = lambda j: slice(j * D_MODEL + h * HEAD_D, j * D_MODEL + (h + 1) * HEAD_D)
            q = q_ref[:, sl]
            lb, sig, f, sq = _hgrn_gate_terms(q, f_ref[:, sl], lbl_ref[:, sl])
            qs = q * sq
            k = 1.0 - f
            bh = _tri_matmul(tri, jnp.log(f))
            eq, ek, eb, ekst, ebl = _hgrn_decay(bh)
            a, qt, kt, diag = _hgrn_scores(qs, k, eq, ek)

            o = o_ref[:, sl]
            gwv = gw_ref[:, sl]
            rs = lax.rsqrt(jnp.mean(o * o, axis=-1, keepdims=True) + EPS)
            on = o * rs
            hg = hg_ref[:, sl]
            sh = _sigmoid(hg)
            dy = dy_ref[:, sl]
            d_onw = dy * (hg * sh)
            dp_ref[:, out(3)] = (dy * (on * gwv) * (sh * (1.0 + hg * (1.0 - sh)))).astype(BF16)
            gsm_ref[1:2, sl] += jnp.sum(d_onw * on, axis=0, keepdims=True)
            d_on = d_onw * gwv
            d_o = rs * (d_on - on * jnp.mean(d_on * on, axis=-1, keepdims=True))

            dob = d_o.astype(BF16)
            vb = v_ref[:, sl].astype(BF16)
            st = st_ref[0, h]
            stb = st.astype(BF16)
            dst = ds_scr[h]
            dstb = dst.astype(BF16)
            kst = k * ekst
            qin = qs * eb

            da = _dot(dob, vb, NT)
            dab = da.astype(BF16)
            da0b = jnp.where(diag, da, 0.0).astype(BF16)
            dp_ref[:, out(2)] = (_dot(a.astype(BF16), dob, TN) + _dot(kst.astype(BF16), dstb, NT)).astype(BF16)

            g_all = [_dot(da0b, kt[:, :HEAD_D])]
            h_all = [_dot(da0b, qt[:, :HEAD_D], TN)]
            g_off = _dot(dab, kt[:, HEAD_D:])
            h_off = _dot(dab, qt[:, HEAD_D:], TN)
            for j in range(1, N_SUB):
                g_all.append(g_off[:, (j - 1) * HEAD_D:j * HEAD_D])
                h_all.append(h_off[:, (j - 1) * HEAD_D:j * HEAD_D])
            dq_inter = eb * _dot(dob, stb)
            d_kst = ekst * _dot(vb, dstb)
            d_q = dq_inter
            d_k = d_kst
            db = qs * dq_inter - k * d_kst
            for j in range(N_SUB):
                sj = slice(j * HEAD_D, (j + 1) * HEAD_D)
                d_q = d_q + eq[j] * g_all[j]
                d_k = d_k + ek[j] * h_all[j]
                db = db + (qt[:, sj].astype(F32) * g_all[j] - kt[:, sj].astype(F32) * h_all[j])

            db_last = (jnp.sum(k * d_kst, axis=0, keepdims=True)
                       + ebl * jnp.sum(st * dst, axis=0, keepdims=True))
            ds_scr[h] = dst * ebl + _dot(dob, qin.astype(BF16), TN)

            db = db + jnp.where(row == CHUNK - 1, db_last, 0.0)
            dg = _tri_matmul(triu, db)
            d_f = dg / f - d_k
            dp_ref[:, out(1)] = (d_f * (1.0 - lb) * sig * (1.0 - sig)).astype(BF16)
            gsm_ref[0:1, sl] += jnp.sum(d_f * (1.0 - sig), axis=0, keepdims=True) * (lb * (1.0 - lb))
            dp_ref[:, out(0)] = (d_q * (sq * (1.0 + q * (1.0 - sq)))).astype(BF16)

        out_copy(slot, nc - 1 - step).start()

        @pl.when(step == nc - 1)
        def _():
            out_copy(1 - slot, 1).wait()
            out_copy(slot, 0).wait()

    rc = lambda c: nc - 1 - c
    col = lambda j: pl.BlockSpec((CHUNK, D_MODEL), lambda c: (rc(c), j))
    par = lambda rows: pl.BlockSpec((rows, D_MODEL), lambda c: (0, 0))
    return pl.pallas_call(
        body, name="hgrn_bwd", grid=(nc,),
        in_specs=[col(2), col(3), col(4), col(5), col(0),
                  pl.BlockSpec((1, N_HEADS, HEAD_D, HEAD_D), lambda c: (rc(c), 0, 0, 0)),
                  col(1), par(2), par(1), ANY],
        out_specs=[ANY, par(8)],
        out_shape=[SDS((seq, D_IN), BF16), SDS((8, D_MODEL), F32)],
        input_output_aliases={9: 0},
        scratch_shapes=[pltpu.VMEM((N_HEADS, HEAD_D, HEAD_D), F32),
                        pltpu.VMEM((2, CHUNK, 4 * D_MODEL), BF16), pltpu.SemaphoreType.DMA((2,))],
        compiler_params=_cparams(("arbitrary",)),
    )(p, p, p, p, o, states, dymix, lbl, gw, dp_full)


def _out_proj(yl, yh, wo, x, tgt, post_w, seq):
    tm = 256

    def body(yl_ref, yh_ref, wo_ref, x_ref, tg_ref, pw_ref, dymix_ref, dout_ref, gwo_ref, st_ref):
        @pl.when(pl.program_id(0) == 0)
        def _():
            gwo_ref[...] = jnp.zeros_like(gwo_ref)
            st_ref[...] = jnp.zeros_like(st_ref)

        ylv = yl_ref[...]
        yhv = yh_ref[...]
        y = _dot(ylv, wo_ref[0:D_MODEL, :]) + _dot(yhv, wo_ref[D_MODEL:D_MIX, :])
        r2 = lax.rsqrt(jnp.mean(y * y, axis=-1, keepdims=True) + EPS)
        yn = y * r2
        pw = pw_ref[...]
        e = (x_ref[...] + yn * pw) - tg_ref[...]
        st_ref[1:2, :] += jnp.sum(e * e, axis=0, keepdims=True) * (0.5 / D_MODEL)
        dout = e * (1.0 / D_MODEL)
        dout_ref[...] = dout
        st_ref[0:1, :] += jnp.sum(dout * yn, axis=0, keepdims=True)
        dyn = dout * pw
        dy = r2 * (dyn - yn * jnp.mean(dyn * yn, axis=-1, keepdims=True))
        dyb = dy.astype(BF16)
        dymix_ref[...] = _dot(dyb, wo_ref[...], NT)
        gwo_ref[0:D_MODEL, :] += _dot(ylv, dyb, TN)
        gwo_ref[D_MODEL:D_MIX, :] += _dot(yhv, dyb, TN)

    row = lambda w: pl.BlockSpec((tm, w), lambda m: (m, 0))
    full = lambda shape: pl.BlockSpec(shape, lambda m: (0,) * len(shape))
    return pl.pallas_call(
        body, name="out_proj", grid=(seq // tm,),
        in_specs=[row(D_MODEL), row(D_MODEL), full((D_MIX, D_MODEL)), row(D_MODEL), row(D_MODEL),
                  full((1, D_MODEL))],
        out_specs=[row(D_MIX), row(D_MODEL), full((D_MIX, D_MODEL)), full((8, D_MODEL))],
        out_shape=[SDS((seq, D_MIX), F32), SDS((seq, D_MODEL), F32), SDS((D_MIX, D_MODEL), F32),
                   SDS((8, D_MODEL), F32)],
        compiler_params=_cparams(("arbitrary",)),
    )(yl, yh, wo, x, tgt, post_w)


MESH = pl.DeviceIdType.MESH
ANY = pl.BlockSpec(memory_space=pl.ANY)
EXCHANGE_SEMS = [pltpu.SemaphoreType.DMA((N_DEV - 1,)), pltpu.SemaphoreType.DMA((N_DEV - 1,)),
                 pltpu.SemaphoreType.DMA(())]


def _mesh_pos():
    return lax.axis_index("x"), lax.axis_index("y"), lax.axis_index("c")


class _SlotExchange:
    def __init__(self, src_ref, dst_ref, send_sems, recv_sems, local_sem, blocked):
        x, y, c = _mesh_pos()
        me = 4 * x + 2 * y + c
        src = (lambda dest: src_ref.at[dest]) if blocked else (lambda dest: src_ref)
        self.local = pltpu.make_async_copy(src(me), dst_ref.at[me], local_sem)
        self.sends, self.recvs = [], []
        for k in range(1, N_DEV):
            px = 1 - x if (k >> 2) & 1 else x
            py = 1 - y if (k >> 1) & 1 else y
            pc = 1 - c if k & 1 else c
            peer = 4 * px + 2 * py + pc
            sems = dict(send_sem=send_sems.at[k - 1], recv_sem=recv_sems.at[k - 1],
                        device_id=(px, py, pc), device_id_type=MESH)
            self.sends.append(pltpu.make_async_remote_copy(src_ref=src(peer), dst_ref=dst_ref.at[me], **sems))
            self.recvs.append(pltpu.make_async_remote_copy(src_ref=dst_ref.at[peer], dst_ref=dst_ref.at[peer], **sems))

    def start(self):
        self.local.start()
        for cp in self.sends:
            cp.start()

    def wait(self):
        for cp in self.recvs:
            cp.wait_recv()
        for cp in self.sends:
            cp.wait_send()
        self.local.wait()


class _ChipExchange:
    def __init__(self, src_ref, dst_ref, send_sems, recv_sems, local_sem):
        x, y, c = _mesh_pos()
        chip = 2 * x + y
        self.local = pltpu.make_async_copy(src_ref.at[chip], dst_ref.at[chip], local_sem)
        self.sends, self.recvs = [], []
        for k in range(1, N_CHIPS):
            px = 1 - x if (k >> 1) & 1 else x
            py = 1 - y if k & 1 else y
            peer = 2 * px + py
            sems = dict(send_sem=send_sems.at[k - 1], recv_sem=recv_sems.at[k - 1],
                        device_id=(px, py, c), device_id_type=MESH)
            self.sends.append(pltpu.make_async_remote_copy(src_ref=src_ref.at[peer], dst_ref=dst_ref.at[chip], **sems))
            self.recvs.append(pltpu.make_async_remote_copy(src_ref=dst_ref.at[peer], dst_ref=dst_ref.at[peer], **sems))

    def start(self):
        self.local.start()
        for cp in self.sends:
            cp.start()

    def wait(self):
        for cp in self.recvs:
            cp.wait_recv()
        for cp in self.sends:
            cp.wait_send()
        self.local.wait()


GRAD_W_IN_TK = 512


def _grad_w_in_sibling(u, dp, core, g_w_out, g_small, seq):
    nk = seq // GRAD_W_IN_TK

    def body(core_ref, u_ref, dp_ref, go_ref, gs_ref, g_ref, ro_ref, rs_ref, *sems):
        del core_ref
        n = pl.program_id(0)
        k = pl.program_id(1)
        exs = [_SlotExchange(go_ref, ro_ref, *sems[0:3], blocked=True),
               _SlotExchange(gs_ref, rs_ref, *sems[3:6], blocked=True)]

        @pl.when((n == 0) & (k == 0))
        def _():
            for ex in exs:
                ex.start()

        @pl.when(k == 0)
        def _():
            g_ref[...] = jnp.zeros_like(g_ref)

        g_ref[0] += _dot(u_ref[...], dp_ref[...], TN)

        @pl.when((n == N_CHIPS - 1) & (k == nk - 1))
        def _():
            for ex in exs:
                ex.wait()

    return pl.pallas_call(
        body, name="grad_w_in_sibling",
        grid_spec=pltpu.PrefetchScalarGridSpec(
            num_scalar_prefetch=1, grid=(N_CHIPS, nk),
            in_specs=[pl.BlockSpec((GRAD_W_IN_TK, D_MODEL), lambda n, k, c: (k, 0)),
                      pl.BlockSpec((GRAD_W_IN_TK, W_BLK), lambda n, k, c: (k, 2 * n + 1 - c[0])), ANY, ANY],
            out_specs=[pl.BlockSpec((1, D_MODEL, W_BLK), lambda n, k, c: (n, 0, 0)), ANY, ANY],
            scratch_shapes=EXCHANGE_SEMS * 2),
        out_shape=[SDS((N_CHIPS, D_MODEL, W_BLK), F32), SDS(g_w_out.shape, F32), SDS(g_small.shape, F32)],
        compiler_params=_cparams(("arbitrary", "arbitrary")),
    )(core, u, dp, g_w_out, g_small)


def _grad_w_in_own(u, dp, core, g_sib, seq):
    nk = seq // GRAD_W_IN_TK

    def body(core_ref, u_ref, dp_ref, gsib_ref, g_ref, land, send_sem, recv_sem):
        del core_ref
        n = pl.program_id(0)
        k = pl.program_id(1)
        x, y, c = _mesh_pos()
        swap = pltpu.make_async_remote_copy(src_ref=gsib_ref, dst_ref=land, send_sem=send_sem, recv_sem=recv_sem,
                                            device_id=(x, y, 1 - c), device_id_type=MESH)

        @pl.when((n == 0) & (k == 0))
        def _():
            swap.start()

        @pl.when(k == 0)
        def _():
            g_ref[...] = jnp.zeros_like(g_ref)

        g_ref[0] += _dot(u_ref[...], dp_ref[...], TN)

        @pl.when((n == 0) & (k == nk - 1))
        def _():
            swap.wait_recv()

        @pl.when(k == nk - 1)
        def _():
            g_ref[0] += land[n]

        @pl.when((n == N_CHIPS - 1) & (k == nk - 1))
        def _():
            swap.wait_send()

    return pl.pallas_call(
        body, name="grad_w_in_own",
        grid_spec=pltpu.PrefetchScalarGridSpec(
            num_scalar_prefetch=1, grid=(N_CHIPS, nk),
            in_specs=[pl.BlockSpec((GRAD_W_IN_TK, D_MODEL), lambda n, k, c: (k, 0)),
                      pl.BlockSpec((GRAD_W_IN_TK, W_BLK), lambda n, k, c: (k, 2 * n + c[0])), ANY],
            out_specs=pl.BlockSpec((1, D_MODEL, W_BLK), lambda n, k, c: (n, 0, 0)),
            scratch_shapes=[pltpu.VMEM((N_CHIPS, D_MODEL, W_BLK), F32), pltpu.SemaphoreType.DMA(()),
                            pltpu.SemaphoreType.DMA(())]),
        out_shape=SDS((N_CHIPS, D_MODEL, W_BLK), F32),
        compiler_params=_cparams(("arbitrary", "arbitrary")),
    )(core, u, dp, g_sib)


def _grad_x(dp, w_all, x, pre_w, dout, g_chip, seq):
    tm = 256
    nm = seq // tm

    def body(dp_ref, w_ref, x_ref, pw_ref, do_ref, gsrc_ref, gx_ref, gpw_ref, recv_ref,
             send_sems, recv_sems, local_sem):
        m = pl.program_id(0)
        ex = _ChipExchange(gsrc_ref, recv_ref, send_sems, recv_sems, local_sem)

        @pl.when(m == 0)
        def _():
            ex.start()
            gpw_ref[...] = jnp.zeros_like(gpw_ref)

        du = _dot(dp_ref[:, 0:W_BLK], w_ref[0], NT)
        for j in range(1, N_DEV):
            du = du + _dot(dp_ref[:, j * W_BLK:(j + 1) * W_BLK], w_ref[j], NT)
        xv = x_ref[...]
        r1 = lax.rsqrt(jnp.mean(xv * xv, axis=-1, keepdims=True) + EPS)
        xn = xv * r1
        gpw_ref[0:1, :] += jnp.sum(du * xn, axis=0, keepdims=True)
        dxn = du * pw_ref[...]
        gx_ref[...] = r1 * (dxn - xn * jnp.mean(dxn * xn, axis=-1, keepdims=True)) + do_ref[...]

        @pl.when(m == nm - 1)
        def _():
            ex.wait()

    row = lambda w: pl.BlockSpec((tm, w), lambda m: (m, 0))
    return pl.pallas_call(
        body, name="grad_x", grid=(nm,),
        in_specs=[row(D_IN), pl.BlockSpec((N_DEV, D_MODEL, W_BLK), lambda m: (0, 0, 0)), row(D_MODEL),
                  pl.BlockSpec((1, D_MODEL), lambda m: (0, 0)), row(D_MODEL), ANY],
        out_specs=[row(D_MODEL), pl.BlockSpec((8, D_MODEL), lambda m: (0, 0)), ANY],
        out_shape=[SDS((seq, D_MODEL), F32), SDS((8, D_MODEL), F32), SDS(g_chip.shape, F32)],
        scratch_shapes=[pltpu.SemaphoreType.DMA((N_CHIPS - 1,)), pltpu.SemaphoreType.DMA((N_CHIPS - 1,)),
                        pltpu.SemaphoreType.DMA(())],
        compiler_params=_cparams(("arbitrary",)),
    )(dp, w_all, x, pre_w, dout, g_chip)


def _local_step(x, tgt, pre_w, w_in_all, conv_w, conv_b, wa, wx, ba, bx, lam, lbl, gnorm_w, w_out, post_w):
    seq = x.shape[0]
    p, u = _in_proj(x, pre_w, w_in_all, seq)
    h, y_lru = _lru_forward(p, conv_w, conv_b, wa, wx, ba, bx, lam, seq)
    y_hgrn, o, states = _hgrn_forward(p, lbl, gnorm_w, seq)
    dymix, dout, g_w_out, stats = _out_proj(y_lru, y_hgrn, w_out, x, tgt, post_w, seq)
    dp_lru, g_wa, g_wx, lru_small = _lru_backward(p, h, dymix, conv_w, conv_b, wa, wx, ba, bx, lam, seq)
    dp, hgrn_small = _hgrn_backward(p, o, states, dymix, lbl, gnorm_w, dp_lru, seq)
    return dict(u=u, dp=dp, dout=dout, g_w_out=g_w_out, g_wa=g_wa, g_wx=g_wx,
                lru_small=lru_small, hgrn_small=hgrn_small, stats=stats)


def _all_gather(parts):
    n = len(parts)

    def body(*refs):
        ins, outs = refs[:n], refs[n:2 * n]
        send_sems, recv_sems, local_sems = refs[2 * n:]
        x, y, c = _mesh_pos()
        me, sibling = (x, y, c), (x, y, 1 - c)
        chips = [(1 - x, y), (x, 1 - y), (1 - x, 1 - y)]

        def slot(a, pos):
            return outs[a].at[4 * pos[0] + 2 * pos[1] + pos[2]]

        def copy(a, k, block, to, src=None):
            dst = slot(a, block)
            return pltpu.make_async_remote_copy(
                src_ref=dst if src is None else src, dst_ref=dst,
                send_sem=send_sems.at[a, k], recv_sem=recv_sems.at[a, k],
                device_id=to, device_id_type=MESH)

        mine = [pltpu.make_async_copy(ins[a], slot(a, me), local_sems.at[a]) for a in range(n)]
        for cp in mine:
            cp.start()
        first = []
        for a in range(n):
            first.append(copy(a, 0, me, sibling, src=ins[a]))
            first += [copy(a, 1 + j, me, (*chip, c), src=ins[a]) for j, chip in enumerate(chips)]
        for cp in first:
            cp.start()
        passed = []
        for j, chip in enumerate(chips):
            for a in range(n):
                copy(a, 1 + j, (*chip, c), me).wait_recv()
                fwd = copy(a, 4 + j, (*chip, c), sibling)
                fwd.start()
                passed.append(fwd)
        for a in range(n):
            copy(a, 0, sibling, me).wait_recv()
            for j, chip in enumerate(chips):
                copy(a, 4 + j, (*chip, 1 - c), me).wait_recv()
        for cp in first + passed:
            cp.wait_send()
        for cp in mine:
            cp.wait()

    return pl.pallas_call(
        body, name="gather_weights",
        out_shape=[SDS((N_DEV,) + p.shape, p.dtype) for p in parts],
        in_specs=[ANY] * n, out_specs=[ANY] * n,
        scratch_shapes=[pltpu.SemaphoreType.DMA((n, 7)), pltpu.SemaphoreType.DMA((n, 7)),
                        pltpu.SemaphoreType.DMA((n,))],
    )(*parts)


def _exchange_grads(blocks, repl):
    nb = len(blocks)
    n = nb + 1

    def body(*refs):
        ins, outs, sems = refs[:n], refs[n:2 * n], refs[2 * n:]
        exs = [_SlotExchange(ins[a], outs[a], *sems[3 * a:3 * a + 3], blocked=a < nb) for a in range(n)]
        for ex in exs:
            ex.start()
        for ex in exs:
            ex.wait()

    arrs = list(blocks) + [repl]
    shapes = [SDS(b.shape, b.dtype) for b in blocks] + [SDS((N_DEV,) + repl.shape, repl.dtype)]
    return pl.pallas_call(
        body, name="exchange_small", out_shape=shapes,
        in_specs=[ANY] * n, out_specs=[ANY] * n,
        scratch_shapes=EXCHANGE_SEMS * n,
    )(*arrs)


def _pack_rows(picks, name):
    arrs = [p[0] for p in picks]

    def body(*refs):
        out = refs[-1]
        out[...] = jnp.zeros_like(out)
        at = 0
        for ref, (_, row, rows, scale) in zip(refs[:-1], picks):
            out[at:at + rows, :] = ref[row:row + rows, :] * scale
            at += rows

    return pl.pallas_call(body, name=name, out_shape=SDS((8, D_MODEL), F32))(*arrs)


def _adamw(g, w, m, v):
    m2 = ADAM_B1 * m + (1.0 - ADAM_B1) * g
    v2 = ADAM_B2 * v + (1.0 - ADAM_B2) * (g * g)
    m_hat = m2 / (1.0 - ADAM_B1 ** ADAM_STEP)
    v_hat = v2 / (1.0 - ADAM_B2 ** ADAM_STEP)
    delta = -ADAM_LR * (m_hat / (jnp.sqrt(v_hat) + ADAM_EPS) + ADAM_WD * w)
    return delta, m2, v2


def _sum_slots(r_ref):
    g = r_ref[0]
    for s in range(1, r_ref.shape[0]):
        g = g + r_ref[s]
    return g


def _sum_adamw(recv, w, m, v, tr, name):
    rows, cols = w.shape

    def body(r_ref, w_ref, m_ref, v_ref, g_ref, d_ref, m2_ref, v2_ref):
        g = _sum_slots(r_ref)
        g_ref[...] = g
        d_ref[...], m2_ref[...], v2_ref[...] = _adamw(g, w_ref[...], m_ref[...], v_ref[...])

    blk = pl.BlockSpec((tr, cols), lambda i: (i, 0))
    return pl.pallas_call(
        body, name=name, grid=(rows // tr,),
        in_specs=[pl.BlockSpec((recv.shape[0], tr, cols), lambda i: (0, i, 0)), blk, blk, blk],
        out_specs=[blk] * 4, out_shape=[SDS((rows, cols), F32)] * 4,
        compiler_params=_cparams(("parallel",)),
    )(recv, w, m, v)


def _sum_adamw_repl(recv, w, m, v):
    def body(r_ref, w_ref, m_ref, v_ref, g_ref, d_ref, m2_ref, v2_ref, loss_ref):
        g = _sum_slots(r_ref)
        g_ref[...] = g
        d_ref[...], m2_ref[...], v2_ref[...] = _adamw(g, w_ref[...], m_ref[...], v_ref[...])
        total = jnp.sum(g[RP_LOSS:RP_LOSS + 1, :], axis=-1, keepdims=True)
        loss_ref[...] = jnp.broadcast_to(total, loss_ref.shape)

    return pl.pallas_call(
        body, name="adamw_repl",
        out_shape=[SDS((8, D_MODEL), F32)] * 4 + [SDS((8, 128), F32)],
    )(recv, w, m, v)


def _shard_rows(t, lead):
    r = t.shape[1] // N_DEV
    t = t.reshape((lead, N_DEV, r) + t.shape[2:])
    return jnp.moveaxis(t, 1, 0)


def _pad8(t):
    return jnp.pad(t, ((0, 0), (0, 8 - t.shape[1]), (0, 0)))


def _pack_small(wa, wx, cw, b_a, b_x):
    n = wa.shape[0]
    return jnp.concatenate([
        wa.reshape(n, 256, 128), wx.reshape(n, 256, 128), _pad8(cw),
        _pad8(b_a.reshape(n, 1, 128)), _pad8(b_x.reshape(n, 1, 128))], axis=1)


def _unpack_small(t):
    n = t.shape[0]
    return (t[:, SM_WA:SM_WA + 256].reshape(n, 4, 32, 256), t[:, SM_WX:SM_WX + 256].reshape(n, 4, 32, 256),
            t[:, SM_CW:SM_CW + 4], t[:, SM_BA].reshape(n, 4, 32), t[:, SM_BX].reshape(n, 4, 32))


def kernel(x, pre_norm_w, w_in, conv_w, conv_b, lru_w_a, lru_b_a, lru_w_x, lru_b_x, lru_lambda, hgrn_lb_logits, hgrn_gnorm_w, w_out, post_norm_w, loss_target, m_pre_norm_w, m_w_in, m_conv_w, m_conv_b, m_lru_w_a, m_lru_b_a, m_lru_w_x, m_lru_b_x, m_lru_lambda, m_hgrn_lb_logits, m_hgrn_gnorm_w, m_w_out, m_post_norm_w, v_pre_norm_w, v_w_in, v_conv_w, v_conv_b, v_lru_w_a, v_lru_b_a, v_lru_w_x, v_lru_b_x, v_lru_lambda, v_hgrn_lb_logits, v_hgrn_gnorm_w, v_w_out, v_post_norm_w):
    seq = x.shape[1]
    x2 = x.reshape(seq, D_MODEL)
    tgt = loss_target.reshape(seq, D_MODEL)

    small_w = _pack_small(lru_w_a, lru_w_x, conv_w, lru_b_a, lru_b_x)[0]
    w_in_all, w_out_all, small_all = _all_gather([w_in[0].astype(BF16), w_out[0].astype(BF16), small_w])
    wa_s, wx_s, cw_s, ba_s, bx_s = _unpack_small(small_all)
    wa = jnp.moveaxis(wa_s, 0, 1).reshape(LRU_BLOCKS, LRU_BW, LRU_BW).astype(BF16)
    wx = jnp.moveaxis(wx_s, 0, 1).reshape(LRU_BLOCKS, LRU_BW, LRU_BW).astype(BF16)
    cw = jnp.moveaxis(cw_s, 0, 1).reshape(4, D_MODEL)
    ba = jnp.moveaxis(ba_s, 0, 1).reshape(1, D_MODEL)
    bx = jnp.moveaxis(bx_s, 0, 1).reshape(1, D_MODEL)

    loc = _local_step(x2, tgt, pre_norm_w, w_in_all, cw, conv_b, wa, wx, ba, bx, lru_lambda,
                      hgrn_lb_logits, hgrn_gnorm_w, w_out_all.reshape(D_MIX, D_MODEL), post_norm_w)

    ls = loc["lru_small"]
    g_small = _pack_small(_shard_rows(loc["g_wa"], LRU_BLOCKS), _shard_rows(loc["g_wx"], LRU_BLOCKS),
                          _shard_rows(ls[0:4].reshape(4, D_MODEL, 1), 4).reshape(N_DEV, 4, 128),
                          _shard_rows(ls[5].reshape(4, LRU_BW, 1), 4).reshape(N_DEV, 4, 32),
                          _shard_rows(ls[6].reshape(4, LRU_BW, 1), 4).reshape(N_DEV, 4, 32))
    core = lax.axis_index("c").astype(jnp.int32).reshape(1)
    g_sib, r_out, r_small = _grad_w_in_sibling(
        loc["u"], loc["dp"], core, loc["g_w_out"].reshape(N_DEV, D_MIX // N_DEV, D_MODEL), g_small, seq)
    g_chip = _grad_w_in_own(loc["u"], loc["dp"], core, g_sib, seq)
    grad_x, pre_small, r_in = _grad_x(loc["dp"], w_in_all, x2, pre_norm_w, loc["dout"], g_chip, seq)
    g_repl = _pack_rows([(pre_small, 0, 1, 1.0), (ls, 4, 1, 1.0), (ls, 7, 1, 1.0),
                         (loc["hgrn_small"], 0, 1, 1.0), (loc["hgrn_small"], 0, 1, -1.0),
                         (loc["hgrn_small"], 1, 1, 1.0), (loc["stats"], 0, 2, 1.0)], "pack_grads")
    (r_repl,) = _exchange_grads([], g_repl)

    zero = jnp.zeros((1, D_MODEL), F32)
    pack_w = lambda a, b, c_, d, e, f, name: _pack_rows(
        [(a, 0, 1, 1.0), (b, 0, 1, 1.0), (c_, 0, 1, 1.0), (d, 0, 2, 1.0), (e, 0, 1, 1.0), (f, 0, 1, 1.0),
         (zero, 0, 1, 1.0)], name)
    w_repl = pack_w(pre_norm_w, conv_b, lru_lambda, hgrn_lb_logits, hgrn_gnorm_w, post_norm_w, "pack_w")
    m_repl = pack_w(m_pre_norm_w, m_conv_b, m_lru_lambda, m_hgrn_lb_logits, m_hgrn_gnorm_w, m_post_norm_w, "pack_m")
    v_repl = pack_w(v_pre_norm_w, v_conv_b, v_lru_lambda, v_hgrn_lb_logits, v_hgrn_gnorm_w, v_post_norm_w, "pack_v")
    o_repl = _sum_adamw_repl(r_repl, w_repl, m_repl, v_repl)
    loss = o_repl[4][0, 0]

    o_in = _sum_adamw(r_in, w_in[0], m_w_in[0], v_w_in[0], 128, "adamw_w_in")
    o_out = _sum_adamw(r_out, w_out[0], m_w_out[0], v_w_out[0], 64, "adamw_w_out")
    o_small = _sum_adamw(r_small,
                         _pack_small(lru_w_a, lru_w_x, conv_w, lru_b_a, lru_b_x)[0],
                         _pack_small(m_lru_w_a, m_lru_w_x, m_conv_w, m_lru_b_a, m_lru_b_x)[0],
                         _pack_small(v_lru_w_a, v_lru_w_x, v_conv_w, v_lru_b_a, v_lru_b_x)[0],
                         SM_ROWS, "adamw_small")

    outs = [loss, grad_x.reshape(x.shape)]
    for kind in range(4):
        rp = o_repl[kind]
        swa, swx, scw, sba, sbx = _unpack_small(o_small[kind][None])
        outs += [rp[RP_PRE:RP_PRE + 1], o_in[kind][None], scw, rp[RP_CB:RP_CB + 1], swa, sba, swx, sbx,
                 rp[RP_LAM:RP_LAM + 1], rp[RP_LB0:RP_LB1 + 1], rp[RP_GN:RP_GN + 1], o_out[kind][None],
                 rp[RP_POST:RP_POST + 1]]
    return tuple(outs)
```

```python
import functools

import jax
import jax.numpy as jnp
from jax import lax
from jax.experimental import pallas as pl
from jax.experimental.pallas import tpu as pltpu

F32 = jnp.float32
BF16 = jnp.bfloat16
SDS = jax.ShapeDtypeStruct

D_MODEL = 1024
D_IN = 6144
N_DEV = 8
N_CHIPS = 4
W_BLK = D_IN // N_DEV
D_MIX = 2048
LRU_BLOCKS = 4
LRU_BW = 256
LRU_C = 8.0
N_HEADS = 8
HEAD_D = 128
CHUNK = 128
SUB = 32
N_SUB = CHUNK // SUB
EXP_CLAMP = 80.0
EPS = 1e-6

ADAM_LR = 0.001
ADAM_B1 = 0.9
ADAM_B2 = 0.999
ADAM_EPS = 1e-08
ADAM_WD = 0.01
ADAM_STEP = 10

VMEM_LIMIT = 56 * 1024 * 1024

NN = (((1,), (0,)), ((), ()))
NT = (((1,), (1,)), ((), ()))
TN = (((0,), (0,)), ((), ()))

SM_WA = 0
SM_WX = 256
SM_CW = 512
SM_BA = 520
SM_BX = 528
SM_ROWS = 536

RP_PRE, RP_CB, RP_LAM, RP_LB0, RP_LB1, RP_GN, RP_POST, RP_LOSS = range(8)


def _dot(a, b, dims=NN):
    return lax.dot_general(a, b, dims, preferred_element_type=F32)


def _sigmoid(x):
    return 0.5 * jnp.tanh(0.5 * x) + 0.5


def _sigmoid_pos(x):
    return 1.0 / (1.0 + jnp.exp(-x))


def _cparams(sem, vmem=VMEM_LIMIT):
    return pltpu.CompilerParams(dimension_semantics=sem, vmem_limit_bytes=vmem)


def _iota(shape, axis):
    return lax.broadcasted_iota(jnp.int32, shape, axis)


def _softplus_neg(lam):
    z = -lam
    e = jnp.exp(-jnp.abs(z))
    u = 1.0 + e
    log1p_e = jnp.where(u == 1.0, e, jnp.log(u) * (e / (u - 1.0)))
    sp = jnp.maximum(z, 0.0) + log1p_e
    dsp = -jnp.where(z >= 0.0, 1.0 / u, e / u)
    return sp, dsp


def _neg_expm1(x):
    poly = x * (1.0 + x * (1.0 / 2 + x * (1.0 / 6 + x * (1.0 / 24 + x * (1.0 / 120)))))
    return jnp.where(x > -1.0 / 16, -poly, 1.0 - jnp.exp(x))


def _conv_taps(lx, prev8, cw_ref, cb_ref, tile):
    xc = cb_ref[...] + cw_ref[3:4, :] * lx
    for j in (1, 2, 3):
        xc = xc + cw_ref[3 - j:4 - j, :] * pltpu.roll(lx, j, 0)
    row8 = _iota((8, D_MODEL), 0)
    last8 = lx[tile - 8:tile, :]
    fix = jnp.zeros((8, D_MODEL), F32)
    for j in (1, 2, 3):
        wrong = pltpu.roll(last8, j, 0)
        right = pltpu.roll(prev8, j, 0)
        fix = fix + cw_ref[3 - j:4 - j, :] * jnp.where(row8 < j, right - wrong, 0.0)
    return xc, fix


def _lru_gates(xcs, wa, wx, ba, bx, sp):
    xb = xcs.astype(BF16)
    r = _sigmoid_pos(_dot(xb, wa) + ba)
    i = _sigmoid(_dot(xb, wx) + bx)
    la = (-LRU_C * sp) * r
    a = jnp.exp(la)
    one_minus_a2 = _neg_expm1(2.0 * la)
    return r, i, a, one_minus_a2


def _lru_forward(p, conv_w, conv_b, wa, wx, ba, bx, lam, seq):
    tile = min(512, seq // 2)
    nblk = tile // 8

    def body(lx_ref, gt_ref, cw_ref, cb_ref, wa_ref, wx_ref, ba_ref, bx_ref, lam_ref,
             h_ref, y_ref, prev8, hcar, xc_scr, a_scr, u_scr):
        @pl.when(pl.program_id(0) == 0)
        def _():
            prev8[...] = jnp.zeros_like(prev8)
            hcar[...] = jnp.zeros_like(hcar)

        lx = lx_ref[...]
        xc, fix = _conv_taps(lx, prev8[...], cw_ref, cb_ref, tile)
        xc_scr[...] = xc
        xc_scr[0:8, :] = xc_scr[0:8, :] + fix
        prev8[...] = lx_ref[tile - 8:tile, :]
        sp, _ = _softplus_neg(lam_ref[...])
        for n in range(LRU_BLOCKS):
            sl = slice(n * LRU_BW, (n + 1) * LRU_BW)
            xcs = xc_scr[:, sl]
            _, i, a, ne = _lru_gates(xcs, wa_ref[n], wx_ref[n], ba_ref[:, sl], bx_ref[:, sl], sp[:, sl])
            a_scr[:, sl] = a
            u_scr[:, sl] = jnp.sqrt(ne) * (i * xcs)

        row8 = _iota((8, D_MODEL), 0)

        def blk(j, hc):
            off = pl.multiple_of(j * 8, 8)
            a = a_scr[pl.ds(off, 8), :]
            u = u_scr[pl.ds(off, 8), :]
            for k in (1, 2, 4):
                m = row8 >= k
                u = jnp.where(m, u + a * pltpu.roll(u, k, 0), u)
                a = jnp.where(m, a * pltpu.roll(a, k, 0), a)
            h = u + a * hc
            h_ref[pl.ds(off, 8), :] = h
            return jnp.broadcast_to(h[7:8, :], (8, D_MODEL))

        hcar[...] = lax.fori_loop(0, nblk, blk, hcar[...])
        g = gt_ref[...]
        y_ref[...] = (h_ref[...] * (g * _sigmoid(g))).astype(BF16)

    full = lambda shape: pl.BlockSpec(shape, lambda t: (0,) * len(shape))
    return pl.pallas_call(
        body, name="lru_fwd", grid=(seq // tile,),
        in_specs=[pl.BlockSpec((tile, D_MODEL), lambda t: (t, 0)),
                  pl.BlockSpec((tile, D_MODEL), lambda t: (t, 1)),
                  full((4, D_MODEL)), full((1, D_MODEL)),
                  full((LRU_BLOCKS, LRU_BW, LRU_BW)), full((LRU_BLOCKS, LRU_BW, LRU_BW)),
                  full((1, D_MODEL)), full((1, D_MODEL)), full((1, D_MODEL))],
        out_specs=[pl.BlockSpec((tile, D_MODEL), lambda t: (t, 0)),
                   pl.BlockSpec((tile, D_MODEL), lambda t: (t, 0))],
        out_shape=[SDS((seq, D_MODEL), F32), SDS((seq, D_MODEL), BF16)],
        scratch_shapes=[pltpu.VMEM((8, D_MODEL), F32), pltpu.VMEM((8, D_MODEL), F32),
                        pltpu.VMEM((tile, D_MODEL), F32), pltpu.VMEM((tile, D_MODEL), F32),
                        pltpu.VMEM((tile, D_MODEL), F32)],
        compiler_params=_cparams(("arbitrary",)),
    )(p, p, conv_w, conv_b, wa, wx, ba, bx, lam)


def _lru_backward(p, h, dymix, conv_w, conv_b, wa, wx, ba, bx, lam, seq):
    tile = min(256, seq // 2)
    nt = seq // tile
    nblk = tile // 8
    t8 = tile // 8

    def body(lx_ref, lxh_ref, gt_ref, h_ref, hh_ref, dy_ref, cw_ref, cb_ref, wa_ref, wx_ref, ba_ref,
             bx_ref, lam_ref, dp_ref, gwa_ref, gwx_ref, gsm_ref,
             lamcar, anext, dxc8, xc_scr, r_scr, i_scr, a_scr, m_scr, rm_scr, c_scr, l_scr, dxc_scr):
        step = pl.program_id(0)
        first_tile = step == nt - 1

        @pl.when(step == 0)
        def _():
            lamcar[...] = jnp.zeros_like(lamcar)
            anext[...] = jnp.zeros_like(anext)
            dxc8[...] = jnp.zeros_like(dxc8)
            gwa_ref[...] = jnp.zeros_like(gwa_ref)
            gwx_ref[...] = jnp.zeros_like(gwx_ref)
            gsm_ref[...] = jnp.zeros_like(gsm_ref)

        keep = jnp.where(first_tile, 0.0, 1.0)
        lx = lx_ref[...]
        prev8 = lxh_ref[...] * keep
        xc, fix = _conv_taps(lx, prev8, cw_ref, cb_ref, tile)
        xc_scr[...] = xc
        xc_scr[0:8, :] = xc_scr[0:8, :] + fix
        sp, dsp = _softplus_neg(lam_ref[...])
        for n in range(LRU_BLOCKS):
            sl = slice(n * LRU_BW, (n + 1) * LRU_BW)
            r, i, a, ne = _lru_gates(xc_scr[:, sl], wa_ref[n], wx_ref[n], ba_ref[:, sl], bx_ref[:, sl],
                                     sp[:, sl])
            r_scr[:, sl] = r
            i_scr[:, sl] = i
            a_scr[:, sl] = a
            m_scr[:, sl] = jnp.sqrt(ne)
            rm_scr[:, sl] = lax.rsqrt(ne)

        g = gt_ref[...]
        sg = _sigmoid(g)
        dy = dy_ref[...]
        hv = h_ref[...]
        dp_ref[:, D_MODEL:2 * D_MODEL] = (dy * hv * (sg * (1.0 + g * (1.0 - sg)))).astype(BF16)

        rowt = _iota((tile, D_MODEL), 0)
        av = a_scr[...]
        l_scr[...] = dy * (g * sg)
        c_scr[...] = jnp.where(rowt == tile - 1, anext[...][0:1, :], pltpu.roll(av, tile - 1, 0))
        anext[...] = jnp.broadcast_to(av[0:1, :], (8, D_MODEL))
        row8 = _iota((8, D_MODEL), 0)

        def blk(jj, lc):
            off = pl.multiple_of((nblk - 1 - jj) * 8, 8)
            c = c_scr[pl.ds(off, 8), :]
            u = l_scr[pl.ds(off, 8), :]
            for k in (1, 2, 4):
                m = row8 < 8 - k
                u = jnp.where(m, u + c * pltpu.roll(u, 8 - k, 0), u)
                c = jnp.where(m, c * pltpu.roll(c, 8 - k, 0), c)
            lamv = u + c * lc
            l_scr[pl.ds(off, 8), :] = lamv
            return jnp.broadcast_to(lamv[0:1, :], (8, D_MODEL))

        lamcar[...] = lax.fori_loop(0, nblk, blk, lamcar[...])

        hprev = jnp.where(rowt == 0, hh_ref[...][7:8, :] * keep, pltpu.roll(hv, 1, 0))
        for n in range(LRU_BLOCKS):
            sl = slice(n * LRU_BW, (n + 1) * LRU_BW)
            lamv = l_scr[:, sl]
            xcs = xc_scr[:, sl]
            r = r_scr[:, sl]
            i = i_scr[:, sl]
            a = a_scr[:, sl]
            mult = m_scr[:, sl]
            d_la = lamv * hprev[:, sl] * a - (lamv * i * xcs) * (a * a * rm_scr[:, sl])
            d_pr = d_la * (-LRU_C * sp[:, sl]) * r * (1.0 - r)
            d_pi = (lamv * mult * xcs) * i * (1.0 - i)
            gsm_ref[7:8, sl] += jnp.sum(d_la * r, axis=0, keepdims=True) * (-LRU_C) * dsp[:, sl]
            gsm_ref[5:6, sl] += jnp.sum(d_pr, axis=0, keepdims=True)
            gsm_ref[6:7, sl] += jnp.sum(d_pi, axis=0, keepdims=True)
            xb = xcs.astype(BF16)
            prb = d_pr.astype(BF16)
            pib = d_pi.astype(BF16)
            gwa_ref[n] += _dot(xb, prb, TN)
            gwx_ref[n] += _dot(xb, pib, TN)
            dxc_scr[:, sl] = lamv * mult * i + _dot(prb, wa_ref[n], NT) + _dot(pib, wx_ref[n], NT)

        dxc = dxc_scr[...]
        gsm_ref[4:5, :] += jnp.sum(dxc, axis=0, keepdims=True)
        last8 = lx[tile - 8:tile, :]
        first8 = dxc[0:8, :]
        dlx = cw_ref[3:4, :] * dxc
        gsm_ref[3:4, :] += jnp.sum(dxc * lx, axis=0, keepdims=True)
        fix = jnp.zeros((8, D_MODEL), F32)
        for j in (1, 2, 3):
            w = cw_ref[3 - j:4 - j, :]
            dlx = dlx + w * pltpu.roll(dxc, tile - j, 0)
            fix = fix + w * jnp.where(row8 + j >= 8,
                                      pltpu.roll(dxc8[...], 8 - j, 0) - pltpu.roll(first8, 8 - j, 0), 0.0)
            halo = jnp.where(row8 < j, pltpu.roll(prev8, j, 0) - pltpu.roll(last8, j, 0), 0.0)
            gsm_ref[3 - j:4 - j, :] += (jnp.sum(dxc * pltpu.roll(lx, j, 0), axis=0, keepdims=True)
                                        + jnp.sum(first8 * halo, axis=0, keepdims=True))
        dxc8[...] = first8
        dp_ref[:, 0:D_MODEL] = dlx.astype(BF16)
        top = tile - 8
        dp_ref[top:tile, 0:D_MODEL] = (dlx[top:tile, :] + fix).astype(BF16)

    rev = lambda t: (nt - 1 - t, 0)
    halo_idx = lambda t: (jnp.maximum((nt - 1 - t) * t8 - 1, 0), 0)
    full = lambda shape: pl.BlockSpec(shape, lambda t: (0,) * len(shape))
    big = lambda: pltpu.VMEM((tile, D_MODEL), F32)
    return pl.pallas_call(
        body, name="lru_bwd", grid=(nt,),
        in_specs=[pl.BlockSpec((tile, D_MODEL), rev),
                  pl.BlockSpec((8, D_MODEL), halo_idx),
                  pl.BlockSpec((tile, D_MODEL), lambda t: (nt - 1 - t, 1)),
                  pl.BlockSpec((tile, D_MODEL), rev),
                  pl.BlockSpec((8, D_MODEL), halo_idx),
                  pl.BlockSpec((tile, D_MODEL), rev),
                  full((4, D_MODEL)), full((1, D_MODEL)),
                  full((LRU_BLOCKS, LRU_BW, LRU_BW)), full((LRU_BLOCKS, LRU_BW, LRU_BW)),
                  full((1, D_MODEL)), full((1, D_MODEL)), full((1, D_MODEL))],
        out_specs=[pl.BlockSpec((tile, 2 * D_MODEL), rev),
                   full((LRU_BLOCKS, LRU_BW, LRU_BW)), full((LRU_BLOCKS, LRU_BW, LRU_BW)),
                   full((8, D_MODEL))],
        out_shape=[SDS((seq, D_IN), BF16), SDS((LRU_BLOCKS, LRU_BW, LRU_BW), F32),
                   SDS((LRU_BLOCKS, LRU_BW, LRU_BW), F32), SDS((8, D_MODEL), F32)],
        scratch_shapes=[pltpu.VMEM((8, D_MODEL), F32), pltpu.VMEM((8, D_MODEL), F32),
                        pltpu.VMEM((8, D_MODEL), F32)] + [big() for _ in range(9)],
        compiler_params=_cparams(("arbitrary",)),
    )(p, p, p, h, h, dymix, conv_w, conv_b, wa, wx, ba, bx, lam)


def _split3(g):
    hi = g.astype(BF16)
    r1 = g - hi.astype(F32)
    mid = r1.astype(BF16)
    lo = (r1 - mid.astype(F32)).astype(BF16)
    return hi, mid, lo


def _tri_matmul(tri, g):
    hi, mid, lo = _split3(g)
    return _dot(tri, lo) + _dot(tri, mid) + _dot(tri, hi)


def _hgrn_gate_terms(q, fr, lbl):
    lb = _sigmoid_pos(lbl[0:1, :] - lbl[1:2, :])
    sig = _sigmoid(fr)
    f = lb + (1.0 - lb) * sig
    sq = _sigmoid(q)
    return lb, sig, f, sq


def _hgrn_decay(bh):
    row = _iota((CHUNK, HEAD_D), 0)
    pieces = []
    for i in range(N_SUB):
        mid = 0.5 * (bh[i * SUB:i * SUB + 1, :] + bh[(i + 1) * SUB - 1:(i + 1) * SUB, :])
        pieces.append(jnp.broadcast_to(mid, (SUB, HEAD_D)))
    mu = jnp.concatenate(pieces, axis=0)
    eq = [jnp.exp(jnp.minimum(bh - mu, EXP_CLAMP))]
    ek = [jnp.exp(jnp.minimum(mu - bh, EXP_CLAMP))]
    for j in range(1, N_SUB):
        rho = bh[j * SUB - 1:j * SUB, :]
        eq.append(jnp.where(row >= j * SUB, jnp.exp(jnp.minimum(bh - rho, 0.0)), 0.0))
        ek.append(jnp.where((row >= (j - 1) * SUB) & (row < j * SUB),
                            jnp.exp(jnp.minimum(rho - bh, 0.0)), 0.0))
    blast = bh[CHUNK - 1:CHUNK, :]
    return eq, ek, jnp.exp(bh), jnp.exp(blast - bh), jnp.exp(blast)


def _hgrn_scores(qs, k, eq, ek):
    qt = jnp.concatenate([(qs * e).astype(BF16) for e in eq], axis=1)
    kt = jnp.concatenate([(k * e).astype(BF16) for e in ek], axis=1)
    r = _iota((CHUNK, CHUNK), 0)
    c = _iota((CHUNK, CHUNK), 1)
    diag = (jnp.right_shift(r, 5) == jnp.right_shift(c, 5)) & (c <= r)
    a = (jnp.where(diag, _dot(qt[:, :HEAD_D], kt[:, :HEAD_D], NT), 0.0)
         + _dot(qt[:, HEAD_D:], kt[:, HEAD_D:], NT))
    return a, qt, kt, diag


def _hgrn_forward(p, lbl, gw, seq):
    nc = seq // CHUNK
    assert SUB == 32

    def body(q_ref, f_ref, v_ref, hg_ref, lbl_ref, gw_ref, y_ref, o_ref, st_ref, s_scr):
        @pl.when(pl.program_id(0) == 0)
        def _():
            s_scr[...] = jnp.zeros_like(s_scr)

        r = _iota((CHUNK, CHUNK), 0)
        c = _iota((CHUNK, CHUNK), 1)
        tri = jnp.where(c <= r, 1.0, 0.0).astype(BF16)
        for h in range(N_HEADS):
            sl = slice(h * HEAD_D, (h + 1) * HEAD_D)
            q = q_ref[:, sl]
            _, _, f, sq = _hgrn_gate_terms(q, f_ref[:, sl], lbl_ref[:, sl])
            qs = q * sq
            k = 1.0 - f
            bh = _tri_matmul(tri, jnp.log(f))
            eq, ek, eb, ekst, ebl = _hgrn_decay(bh)
            a, _, _, _ = _hgrn_scores(qs, k, eq, ek)
            st = s_scr[h]
            st_ref[0, h] = st
            vb = v_ref[:, sl].astype(BF16)
            o = _dot(a.astype(BF16), vb) + _dot((qs * eb).astype(BF16), st.astype(BF16), NT)
            s_scr[h] = st * ebl + _dot(vb, (k * ekst).astype(BF16), TN)
            o_ref[:, sl] = o
            rs = lax.rsqrt(jnp.mean(o * o, axis=-1, keepdims=True) + EPS)
            hg = hg_ref[:, sl]
            y_ref[:, sl] = ((o * rs * gw_ref[:, sl]) * (hg * _sigmoid(hg))).astype(BF16)

    col = lambda j: pl.BlockSpec((CHUNK, D_MODEL), lambda c: (c, j))
    par = lambda rows: pl.BlockSpec((rows, D_MODEL), lambda c: (0, 0))
    return pl.pallas_call(
        body, name="hgrn_fwd", grid=(nc,),
        in_specs=[col(2), col(3), col(4), col(5), par(2), par(1)],
        out_specs=[col(0), col(0),
                   pl.BlockSpec((1, N_HEADS, HEAD_D, HEAD_D), lambda c: (c, 0, 0, 0))],
        out_shape=[SDS((seq, D_MODEL), BF16), SDS((seq, D_MODEL), F32),
                   SDS((nc, N_HEADS, HEAD_D, HEAD_D), F32)],
        scratch_shapes=[pltpu.VMEM((N_HEADS, HEAD_D, HEAD_D), F32)],
        compiler_params=_cparams(("arbitrary",)),
    )(p, p, p, p, lbl, gw)


def _hgrn_backward(p, o, states, dymix, lbl, gw, dp_full, g_w_out, g_small, seq):
    nc = seq // CHUNK

    def body(q_ref, f_ref, v_ref, hg_ref, o_ref, st_ref, dy_ref, lbl_ref, gw_ref, dpin_ref, go_ref, gs_ref,
             dpo_ref, gsm_ref, ro_ref, rs_ref, ds_scr, dp_buf, dp_sems, *sems):
        del dpin_ref
        step = pl.program_id(0)
        slot = step % 2
        exs = [_SlotExchange(go_ref, ro_ref, *sems[0:3], blocked=True),
               _SlotExchange(gs_ref, rs_ref, *sems[3:6], blocked=True)]

        @pl.when(step == 0)
        def _():
            for ex in exs:
                ex.start()

        def out_copy(s, chunk):
            rows = pl.ds(pl.multiple_of(chunk * CHUNK, CHUNK), CHUNK)
            return pltpu.make_async_copy(dp_buf.at[s], dpo_ref.at[rows, pl.ds(2 * D_MODEL, 4 * D_MODEL)],
                                         dp_sems.at[s])

        @pl.when(step == 0)
        def _():
            ds_scr[...] = jnp.zeros_like(ds_scr)
            gsm_ref[...] = jnp.zeros_like(gsm_ref)

        @pl.when(step >= 2)
        def _():
            out_copy(slot, nc + 1 - step).wait()

        dp_ref = dp_buf.at[slot]

        r = _iota((CHUNK, CHUNK), 0)
        c = _iota((CHUNK, CHUNK), 1)
        tri = jnp.where(c <= r, 1.0, 0.0).astype(BF16)
        triu = jnp.where(c >= r, 1.0, 0.0).astype(BF16)
        row = _iota((CHUNK, HEAD_D), 0)
        for h in range(N_HEADS):
            sl = slice(h * HEAD_D, (h + 1) * HEAD_D)
            out = lambda j: slice(j * D_MODEL + h * HEAD_D, j * D_MODEL + (h + 1) * HEAD_D)
            q = q_ref[:, sl]
            lb, sig, f, sq = _hgrn_gate_terms(q, f_ref[:, sl], lbl_ref[:, sl])
            qs = q * sq
            k = 1.0 - f
            bh = _tri_matmul(tri, jnp.log(f))
            eq, ek, eb, ekst, ebl = _hgrn_decay(bh)
            a, qt, kt, diag = _hgrn_scores(qs, k, eq, ek)

            o = o_ref[:, sl]
            gwv = gw_ref[:, sl]
            rs = lax.rsqrt(jnp.mean(o * o, axis=-1, keepdims=True) + EPS)
            on = o * rs
            hg = hg_ref[:, sl]
            sh = _sigmoid(hg)
            dy = dy_ref[:, sl]
            d_onw = dy * (hg * sh)
            dp_ref[:, out(3)] = (dy * (on * gwv) * (sh * (1.0 + hg * (1.0 - sh)))).astype(BF16)
            gsm_ref[1:2, sl] += jnp.sum(d_onw * on, axis=0, keepdims=True)
            d_on = d_onw * gwv
            d_o = rs * (d_on - on * jnp.mean(d_on * on, axis=-1, keepdims=True))

            dob = d_o.astype(BF16)
            vb = v_ref[:, sl].astype(BF16)
            st = st_ref[0, h]
            stb = st.astype(BF16)
            dst = ds_scr[h]
            dstb = dst.astype(BF16)
            kst = k * ekst
            qin = qs * eb

            da = _dot(dob, vb, NT)
            dab = da.astype(BF16)
            da0b = jnp.where(diag, da, 0.0).astype(BF16)
            dp_ref[:, out(2)] = (_dot(a.astype(BF16), dob, TN) + _dot(kst.astype(BF16), dstb, NT)).astype(BF16)

            g_all = [_dot(da0b, kt[:, :HEAD_D])]
            h_all = [_dot(da0b, qt[:, :HEAD_D], TN)]
            g_off = _dot(dab, kt[:, HEAD_D:])
            h_off = _dot(dab, qt[:, HEAD_D:], TN)
            for j in range(1, N_SUB):
                g_all.append(g_off[:, (j - 1) * HEAD_D:j * HEAD_D])
                h_all.append(h_off[:, (j - 1) * HEAD_D:j * HEAD_D])
            dq_inter = eb * _dot(dob, stb)
            d_kst = ekst * _dot(vb, dstb)
            d_q = dq_inter
            d_k = d_kst
            db = qs * dq_inter - k * d_kst
            for j in range(N_SUB):
                sj = slice(j * HEAD_D, (j + 1) * HEAD_D)
                d_q = d_q + eq[j] * g_all[j]
                d_k = d_k + ek[j] * h_all[j]
                db = db + (qt[:, sj].astype(F32) * g_all[j] - kt[:, sj].astype(F32) * h_all[j])

            db_last = (jnp.sum(k * d_kst, axis=0, keepdims=True)
                       + ebl * jnp.sum(st * dst, axis=0, keepdims=True))
            ds_scr[h] = dst * ebl + _dot(dob, qin.astype(BF16), TN)

            db = db + jnp.where(row == CHUNK - 1, db_last, 0.0)
            dg = _tri_matmul(triu, db)
            d_f = dg / f - d_k
            dp_ref[:, out(1)] = (d_f * (1.0 - lb) * sig * (1.0 - sig)).astype(BF16)
            gsm_ref[0:1, sl] += jnp.sum(d_f * (1.0 - sig), axis=0, keepdims=True) * (lb * (1.0 - lb))
            dp_ref[:, out(0)] = (d_q * (sq * (1.0 + q * (1.0 - sq)))).astype(BF16)

        out_copy(slot, nc - 1 - step).start()

        @pl.when(step == nc - 1)
        def _():
            out_copy(1 - slot, 1).wait()
            out_copy(slot, 0).wait()
            for ex in exs:
                ex.wait()

    rc = lambda c: nc - 1 - c
    col = lambda j: pl.BlockSpec((CHUNK, D_MODEL), lambda c: (rc(c), j))
    par = lambda rows: pl.BlockSpec((rows, D_MODEL), lambda c: (0, 0))
    return pl.pallas_call(
        body, name="hgrn_bwd", grid=(nc,),
        in_specs=[col(2), col(3), col(4), col(5), col(0),
                  pl.BlockSpec((1, N_HEADS, HEAD_D, HEAD_D), lambda c: (rc(c), 0, 0, 0)),
                  col(1), par(2), par(1), ANY, ANY, ANY],
        out_specs=[ANY, par(8), ANY, ANY],
        out_shape=[SDS((seq, D_IN), BF16), SDS((8, D_MODEL), F32), SDS(g_w_out.shape, F32),
                   SDS(g_small.shape, F32)],
        input_output_aliases={9: 0},
        scratch_shapes=[pltpu.VMEM((N_HEADS, HEAD_D, HEAD_D), F32),
                        pltpu.VMEM((2, CHUNK, 4 * D_MODEL), BF16), pltpu.SemaphoreType.DMA((2,))]
                       + EXCHANGE_SEMS * 2,
        compiler_params=_cparams(("arbitrary",)),
    )(p, p, p, p, o, states, dymix, lbl, gw, dp_full, g_w_out, g_small)


def _out_proj(yl, yh, wo, x, tgt, post_w, seq):
    tm = 256

    def body(yl_ref, yh_ref, wo_ref, x_ref, tg_ref, pw_ref, dymix_ref, dout_ref, gwo_ref, st_ref):
        @pl.when(pl.program_id(0) == 0)
        def _():
            gwo_ref[...] = jnp.zeros_like(gwo_ref)
            st_ref[...] = jnp.zeros_like(st_ref)

        ylv = yl_ref[...]
        yhv = yh_ref[...]
        y = _dot(ylv, wo_ref[0:D_MODEL, :]) + _dot(yhv, wo_ref[D_MODEL:D_MIX, :])
        r2 = lax.rsqrt(jnp.mean(y * y, axis=-1, keepdims=True) + EPS)
        yn = y * r2
        pw = pw_ref[...]
        e = (x_ref[...] + yn * pw) - tg_ref[...]
        st_ref[1:2, :] += jnp.sum(e * e, axis=0, keepdims=True) * (0.5 / D_MODEL)
        dout = e * (1.0 / D_MODEL)
        dout_ref[...] = dout
        st_ref[0:1, :] += jnp.sum(dout * yn, axis=0, keepdims=True)
        dyn = dout * pw
        dy = r2 * (dyn - yn * jnp.mean(dyn * yn, axis=-1, keepdims=True))
        dyb = dy.astype(BF16)
        dymix_ref[...] = _dot(dyb, wo_ref[...], NT)
        gwo_ref[0:D_MODEL, :] += _dot(ylv, dyb, TN)
        gwo_ref[D_MODEL:D_MIX, :] += _dot(yhv, dyb, TN)

    row = lambda w: pl.BlockSpec((tm, w), lambda m: (m, 0))
    full = lambda shape: pl.BlockSpec(shape, lambda m: (0,) * len(shape))
    return pl.pallas_call(
        body, name="out_proj", grid=(seq // tm,),
        in_specs=[row(D_MODEL), row(D_MODEL), full((D_MIX, D_MODEL)), row(D_MODEL), row(D_MODEL),
                  full((1, D_MODEL))],
        out_specs=[row(D_MIX), row(D_MODEL), full((D_MIX, D_MODEL)), full((8, D_MODEL))],
        out_shape=[SDS((seq, D_MIX), F32), SDS((seq, D_MODEL), F32), SDS((D_MIX, D_MODEL), F32),
                   SDS((8, D_MODEL), F32)],
        compiler_params=_cparams(("arbitrary",)),
    )(yl, yh, wo, x, tgt, post_w)


MESH = pl.DeviceIdType.MESH
ANY = pl.BlockSpec(memory_space=pl.ANY)
EXCHANGE_SEMS = [pltpu.SemaphoreType.DMA((N_DEV - 1,)), pltpu.SemaphoreType.DMA((N_DEV - 1,)),
                 pltpu.SemaphoreType.DMA(())]


def _mesh_pos():
    return lax.axis_index("x"), lax.axis_index("y"), lax.axis_index("c")


class _SlotExchange:
    def __init__(self, src_ref, dst_ref, send_sems, recv_sems, local_sem, blocked):
        x, y, c = _mesh_pos()
        me = 4 * x + 2 * y + c
        src = (lambda dest: src_ref.at[dest]) if blocked else (lambda dest: src_ref)
        self.local = pltpu.make_async_copy(src(me), dst_ref.at[me], local_sem)
        self.sends, self.recvs = [], []
        for k in range(1, N_DEV):
            px = 1 - x if (k >> 2) & 1 else x
            py = 1 - y if (k >> 1) & 1 else y
            pc = 1 - c if k & 1 else c
            peer = 4 * px + 2 * py + pc
            sems = dict(send_sem=send_sems.at[k - 1], recv_sem=recv_sems.at[k - 1],
                        device_id=(px, py, pc), device_id_type=MESH)
            self.sends.append(pltpu.make_async_remote_copy(src_ref=src(peer), dst_ref=dst_ref.at[me], **sems))
            self.recvs.append(pltpu.make_async_remote_copy(src_ref=dst_ref.at[peer], dst_ref=dst_ref.at[peer], **sems))

    def start(self):
        self.local.start()
        for cp in self.sends:
            cp.start()

    def wait(self):
        for cp in self.recvs:
            cp.wait_recv()
        for cp in self.sends:
            cp.wait_send()
        self.local.wait()


class _ChipExchange:
    def __init__(self, src_ref, dst_ref, send_sems, recv_sems, local_sem):
        x, y, c = _mesh_pos()
        chip = 2 * x + y
        self.local = pltpu.make_async_copy(src_ref.at[chip], dst_ref.at[chip], local_sem)
        self.sends, self.recvs = [], []
        for k in range(1, N_CHIPS):
            px = 1 - x if (k >> 1) & 1 else x
            py = 1 - y if k & 1 else y
            peer = 2 * px + py
            sems = dict(send_sem=send_sems.at[k - 1], recv_sem=recv_sems.at[k - 1],
                        device_id=(px, py, c), device_id_type=MESH)
            self.sends.append(pltpu.make_async_remote_copy(src_ref=src_ref.at[peer], dst_ref=dst_ref.at[chip], **sems))
            self.recvs.append(pltpu.make_async_remote_copy(src_ref=dst_ref.at[peer], dst_ref=dst_ref.at[peer], **sems))

    def start(self):
        self.local.start()
        for cp in self.sends:
            cp.start()

    def wait(self):
        for cp in self.recvs:
            cp.wait_recv()
        for cp in self.sends:
            cp.wait_send()
        self.local.wait()


GRAD_W_IN_TK = 512


def _grad_w_in_sibling(u, dp, core, seq):
    nk = seq // GRAD_W_IN_TK

    def body(core_ref, u_ref, dp_ref, g_ref):
        del core_ref

        @pl.when(pl.program_id(1) == 0)
        def _():
            g_ref[...] = jnp.zeros_like(g_ref)

        g_ref[0] += _dot(u_ref[...], dp_ref[...], TN)

    return pl.pallas_call(
        body, name="grad_w_in_sibling",
        grid_spec=pltpu.PrefetchScalarGridSpec(
            num_scalar_prefetch=1, grid=(N_CHIPS, nk),
            in_specs=[pl.BlockSpec((GRAD_W_IN_TK, D_MODEL), lambda n, k, c: (k, 0)),
                      pl.BlockSpec((GRAD_W_IN_TK, W_BLK), lambda n, k, c: (k, 2 * n + 1 - c[0]))],
            out_specs=pl.BlockSpec((1, D_MODEL, W_BLK), lambda n, k, c: (n, 0, 0))),
        out_shape=SDS((N_CHIPS, D_MODEL, W_BLK), F32),
        compiler_params=_cparams(("parallel", "arbitrary")),
    )(core, u, dp)


def _grad_w_in_own(u, dp, core, g_sib, seq):
    nk = seq // GRAD_W_IN_TK

    def body(core_ref, u_ref, dp_ref, gsib_ref, g_ref, land, send_sem, recv_sem):
        del core_ref
        n = pl.program_id(0)
        k = pl.program_id(1)
        x, y, c = _mesh_pos()
        swap = pltpu.make_async_remote_copy(src_ref=gsib_ref, dst_ref=land, send_sem=send_sem, recv_sem=recv_sem,
                                            device_id=(x, y, 1 - c), device_id_type=MESH)

        @pl.when((n == 0) & (k == 0))
        def _():
            swap.start()

        @pl.when(k == 0)
        def _():
            g_ref[...] = jnp.zeros_like(g_ref)

        g_ref[0] += _dot(u_ref[...], dp_ref[...], TN)

        @pl.when((n == 0) & (k == nk - 1))
        def _():
            swap.wait_recv()

        @pl.when(k == nk - 1)
        def _():
            g_ref[0] += land[n]

        @pl.when((n == N_CHIPS - 1) & (k == nk - 1))
        def _():
            swap.wait_send()

    return pl.pallas_call(
        body, name="grad_w_in_own",
        grid_spec=pltpu.PrefetchScalarGridSpec(
            num_scalar_prefetch=1, grid=(N_CHIPS, nk),
            in_specs=[pl.BlockSpec((GRAD_W_IN_TK, D_MODEL), lambda n, k, c: (k, 0)),
                      pl.BlockSpec((GRAD_W_IN_TK, W_BLK), lambda n, k, c: (k, 2 * n + c[0])), ANY],
            out_specs=pl.BlockSpec((1, D_MODEL, W_BLK), lambda n, k, c: (n, 0, 0)),
            scratch_shapes=[pltpu.VMEM((N_CHIPS, D_MODEL, W_BLK), F32), pltpu.SemaphoreType.DMA(()),
                            pltpu.SemaphoreType.DMA(())]),
        out_shape=SDS((N_CHIPS, D_MODEL, W_BLK), F32),
        compiler_params=_cparams(("arbitrary", "arbitrary")),
    )(core, u, dp, g_sib)


def _grad_x(dp, w_all, x, pre_w, dout, g_chip, seq):
    tm = 256
    nm = seq // tm

    def body(dp_ref, w_ref, x_ref, pw_ref, do_ref, gsrc_ref, gx_ref, gpw_ref, recv_ref,
             send_sems, recv_sems, local_sem):
        m = pl.program_id(0)
        ex = _ChipExchange(gsrc_ref, recv_ref, send_sems, recv_sems, local_sem)

        @pl.when(m == 0)
        def _():
            ex.start()
            gpw_ref[...] = jnp.zeros_like(gpw_ref)

        du = _dot(dp_ref[:, 0:W_BLK], w_ref[0], NT)
        for j in range(1, N_DEV):
            du = du + _dot(dp_ref[:, j * W_BLK:(j + 1) * W_BLK], w_ref[j], NT)
        xv = x_ref[...]
        r1 = lax.rsqrt(jnp.mean(xv * xv, axis=-1, keepdims=True) + EPS)
        xn = xv * r1
        gpw_ref[0:1, :] += jnp.sum(du * xn, axis=0, keepdims=True)
        dxn = du * pw_ref[...]
        gx_ref[...] = r1 * (dxn - xn * jnp.mean(dxn * xn, axis=-1, keepdims=True)) + do_ref[...]

        @pl.when(m == nm - 1)
        def _():
            ex.wait()

    row = lambda w: pl.BlockSpec((tm, w), lambda m: (m, 0))
    return pl.pallas_call(
        body, name="grad_x", grid=(nm,),
        in_specs=[row(D_IN), pl.BlockSpec((N_DEV, D_MODEL, W_BLK), lambda m: (0, 0, 0)), row(D_MODEL),
                  pl.BlockSpec((1, D_MODEL), lambda m: (0, 0)), row(D_MODEL), ANY],
        out_specs=[row(D_MODEL), pl.BlockSpec((8, D_MODEL), lambda m: (0, 0)), ANY],
        out_shape=[SDS((seq, D_MODEL), F32), SDS((8, D_MODEL), F32), SDS(g_chip.shape, F32)],
        scratch_shapes=[pltpu.SemaphoreType.DMA((N_CHIPS - 1,)), pltpu.SemaphoreType.DMA((N_CHIPS - 1,)),
                        pltpu.SemaphoreType.DMA(())],
        compiler_params=_cparams(("arbitrary",)),
    )(dp, w_all, x, pre_w, dout, g_chip)


def _local_step(x, tgt, p, u, conv_w, conv_b, wa, wx, ba, bx, lam, lbl, gnorm_w, w_out, post_w):
    seq = x.shape[0]
    h, y_lru = _lru_forward(p, conv_w, conv_b, wa, wx, ba, bx, lam, seq)
    y_hgrn, o, states = _hgrn_forward(p, lbl, gnorm_w, seq)
    dymix, dout, g_w_out, stats = _out_proj(y_lru, y_hgrn, w_out, x, tgt, post_w, seq)
    dp_lru, g_wa, g_wx, ls = _lru_backward(p, h, dymix, conv_w, conv_b, wa, wx, ba, bx, lam, seq)
    g_small = _pack_small(_shard_rows(g_wa, LRU_BLOCKS), _shard_rows(g_wx, LRU_BLOCKS),
                          _shard_rows(ls[0:4].reshape(4, D_MODEL, 1), 4).reshape(N_DEV, 4, 128),
                          _shard_rows(ls[5].reshape(4, LRU_BW, 1), 4).reshape(N_DEV, 4, 32),
                          _shard_rows(ls[6].reshape(4, LRU_BW, 1), 4).reshape(N_DEV, 4, 32))
    dp, hgrn_small, r_out, r_small = _hgrn_backward(
        p, o, states, dymix, lbl, gnorm_w, dp_lru, g_w_out.reshape(N_DEV, D_MIX // N_DEV, D_MODEL), g_small, seq)
    return dict(u=u, dp=dp, dout=dout, r_out=r_out, r_small=r_small,
                lru_small=ls, hgrn_small=hgrn_small, stats=stats)


class _TwoLevelGather:
    def __init__(self, ins, outs, send_sems, recv_sems, local_sems):
        self.ins, self.outs = ins, outs
        self.send_sems, self.recv_sems, self.local_sems = send_sems, recv_sems, local_sems
        x, y, c = _mesh_pos()
        self.c = c
        self.me, self.sibling = (x, y, c), (x, y, 1 - c)
        self.chips = [(1 - x, y), (x, 1 - y), (1 - x, 1 - y)]
        n = len(ins)
        self.mine = [pltpu.make_async_copy(ins[a], self._slot(a, self.me), local_sems.at[a]) for a in range(n)]
        self.first = []
        for a in range(n):
            self.first.append(self._copy(a, 0, self.me, self.sibling, src=ins[a]))
            self.first += [self._copy(a, 1 + j, self.me, (*chip, c), src=ins[a])
                           for j, chip in enumerate(self.chips)]
        self.passed = [self._copy(a, 4 + j, (*chip, c), self.sibling)
                       for j, chip in enumerate(self.chips) for a in range(n)]

    def _slot(self, a, pos):
        return self.outs[a].at[4 * pos[0] + 2 * pos[1] + pos[2]]

    def _copy(self, a, k, block, to, src=None):
        dst = self._slot(a, block)
        return pltpu.make_async_remote_copy(
            src_ref=dst if src is None else src, dst_ref=dst,
            send_sem=self.send_sems.at[a, k], recv_sem=self.recv_sems.at[a, k],
            device_id=to, device_id_type=MESH)

    def start(self):
        for cp in self.mine + self.first:
            cp.start()

    def forward(self):
        n = len(self.ins)
        for j, chip in enumerate(self.chips):
            for a in range(n):
                self._copy(a, 1 + j, (*chip, self.c), self.me).wait_recv()
                self.passed[j * n + a].start()

    def finish(self):
        for a in range(len(self.ins)):
            self._copy(a, 0, self.sibling, self.me).wait_recv()
            for j, chip in enumerate(self.chips):
                self._copy(a, 4 + j, (*chip, 1 - self.c), self.me).wait_recv()
        for cp in self.first + self.passed:
            cp.wait_send()
        for cp in self.mine:
            cp.wait()


W_IN_DIRECT = (1, 2, 4, 6)
W_IN_PASSED = (2, 4, 6)


def _in_proj_gather(x, pre_w, w_in_blk, w_out_blk, small_blk, me, seq):
    tm = min(1024, seq)
    nm = seq // tm
    last = N_DEV - 1

    def body(me_ref, x_ref, pw_ref, wblk_ref, woblk_ref, smblk_ref,
             p_ref, u_ref, wall_ref, woall_ref, small_ref,
             u_all, w_vmem, own_sem, d_send, d_recv, f_send, f_recv, wb_sems, g_send, g_recv, g_local):
        i = pl.program_id(0)
        m = pl.program_id(1)
        idx = me_ref[0]
        x_, y_, c_ = _mesh_pos()
        aux = _TwoLevelGather([woblk_ref, smblk_ref], [woall_ref, small_ref], g_send, g_recv, g_local)

        def peer(k):
            return (1 - x_ if (k >> 2) & 1 else x_, 1 - y_ if (k >> 1) & 1 else y_, 1 - c_ if k & 1 else c_)

        def direct(k):
            f = W_IN_DIRECT.index(k)
            return (pltpu.make_async_remote_copy(src_ref=wblk_ref, dst_ref=w_vmem.at[idx], send_sem=d_send.at[f],
                                                 recv_sem=d_recv.at[f], device_id=peer(k), device_id_type=MESH),
                    pltpu.make_async_remote_copy(src_ref=w_vmem.at[idx ^ k], dst_ref=w_vmem.at[idx ^ k],
                                                 send_sem=d_send.at[f], recv_sem=d_recv.at[f], device_id=peer(k),
                                                 device_id_type=MESH))

        def passed(k):
            f = W_IN_PASSED.index(k)
            return (pltpu.make_async_remote_copy(src_ref=w_vmem.at[idx ^ k], dst_ref=w_vmem.at[idx ^ k],
                                                 send_sem=f_send.at[f], recv_sem=f_recv.at[f], device_id=peer(1),
                                                 device_id_type=MESH),
                    pltpu.make_async_remote_copy(src_ref=w_vmem.at[idx ^ (k + 1)], dst_ref=w_vmem.at[idx ^ (k + 1)],
                                                 send_sem=f_send.at[f], recv_sem=f_recv.at[f], device_id=peer(1),
                                                 device_id_type=MESH))

        def write_back(k):
            return pltpu.make_async_copy(w_vmem.at[idx ^ k], wall_ref.at[idx ^ k], wb_sems.at[k])

        own = pltpu.make_async_copy(wblk_ref, w_vmem.at[idx], own_sem)

        @pl.when((i == 0) & (m == 0))
        def _():
            own.start()
            for k in W_IN_DIRECT:
                direct(k)[0].start()
            aux.start()
            own.wait()
            write_back(0).start()

        for k in range(1, N_DEV):
            @pl.when((i == k) & (m == 0))
            def _(k=k):
                if k in W_IN_DIRECT:
                    direct(k)[1].wait_recv()
                    if k in W_IN_PASSED:
                        passed(k)[0].start()
                else:
                    passed(k - 1)[1].wait_recv()
                write_back(k).start()

        @pl.when((i == N_CHIPS) & (m == 0))
        def _():
            aux.forward()

        rows = pl.ds(pl.multiple_of(m * tm, tm), tm)

        @pl.when(i == 0)
        def _():
            xv = x_ref[...]
            r = lax.rsqrt(jnp.mean(xv * xv, axis=-1, keepdims=True) + EPS)
            ub = (xv * r * pw_ref[...]).astype(BF16)
            u_all[rows, :] = ub
            u_ref[...] = ub

        p_ref[...] = _dot(u_all[rows, :], w_vmem[idx ^ i])

        @pl.when((i == last) & (m == nm - 1))
        def _():
            for k in W_IN_DIRECT:
                direct(k)[0].wait_send()
            for k in W_IN_PASSED:
                passed(k)[0].wait_send()
            for k in range(N_DEV):
                write_back(k).wait()
            aux.finish()

    first_pass = lambda i, m: jnp.where(i == 0, m, nm - 1)
    return pl.pallas_call(
        body, name="in_proj_gather",
        grid_spec=pltpu.PrefetchScalarGridSpec(
            num_scalar_prefetch=1, grid=(N_DEV, nm),
            in_specs=[pl.BlockSpec((tm, D_MODEL), lambda i, m, me: (first_pass(i, m), 0)),
                      pl.BlockSpec((1, D_MODEL), lambda i, m, me: (0, 0)), ANY, ANY, ANY],
            out_specs=[pl.BlockSpec((tm, W_BLK), lambda i, m, me: (m, me[0] ^ i)),
                       pl.BlockSpec((tm, D_MODEL), lambda i, m, me: (first_pass(i, m), 0)), ANY, ANY, ANY],
            scratch_shapes=[pltpu.VMEM((seq, D_MODEL), BF16), pltpu.VMEM((N_DEV, D_MODEL, W_BLK), BF16),
                            pltpu.SemaphoreType.DMA(()),
                            pltpu.SemaphoreType.DMA((len(W_IN_DIRECT),)), pltpu.SemaphoreType.DMA((len(W_IN_DIRECT),)),
                            pltpu.SemaphoreType.DMA((len(W_IN_PASSED),)), pltpu.SemaphoreType.DMA((len(W_IN_PASSED),)),
                            pltpu.SemaphoreType.DMA((N_DEV,)),
                            pltpu.SemaphoreType.DMA((2, 7)), pltpu.SemaphoreType.DMA((2, 7)),
                            pltpu.SemaphoreType.DMA((2,))]),
        out_shape=[SDS((seq, D_IN), F32), SDS((seq, D_MODEL), BF16), SDS((N_DEV, D_MODEL, W_BLK), BF16),
                   SDS((N_DEV,) + w_out_blk.shape, w_out_blk.dtype), SDS((N_DEV,) + small_blk.shape, small_blk.dtype)],
        compiler_params=_cparams(("arbitrary", "arbitrary")),
    )(me, x, pre_w, w_in_blk, w_out_blk, small_blk)


def _exchange_grads(blocks, repl):
    nb = len(blocks)
    n = nb + 1

    def body(*refs):
        ins, outs, sems = refs[:n], refs[n:2 * n], refs[2 * n:]
        exs = [_SlotExchange(ins[a], outs[a], *sems[3 * a:3 * a + 3], blocked=a < nb) for a in range(n)]
        for ex in exs:
            ex.start()
        for ex in exs:
            ex.wait()

    arrs = list(blocks) + [repl]
    shapes = [SDS(b.shape, b.dtype) for b in blocks] + [SDS((N_DEV,) + repl.shape, repl.dtype)]
    return pl.pallas_call(
        body, name="exchange_small", out_shape=shapes,
        in_specs=[ANY] * n, out_specs=[ANY] * n,
        scratch_shapes=EXCHANGE_SEMS * n,
    )(*arrs)


def _pack_rows(picks, name):
    arrs = [p[0] for p in picks]

    def body(*refs):
        out = refs[-1]
        out[...] = jnp.zeros_like(out)
        at = 0
        for ref, (_, row, rows, scale) in zip(refs[:-1], picks):
            out[at:at + rows, :] = ref[row:row + rows, :] * scale
            at += rows

    return pl.pallas_call(body, name=name, out_shape=SDS((8, D_MODEL), F32))(*arrs)


def _adamw(g, w, m, v):
    m2 = ADAM_B1 * m + (1.0 - ADAM_B1) * g
    v2 = ADAM_B2 * v + (1.0 - ADAM_B2) * (g * g)
    m_hat = m2 / (1.0 - ADAM_B1 ** ADAM_STEP)
    v_hat = v2 / (1.0 - ADAM_B2 ** ADAM_STEP)
    delta = -ADAM_LR * (m_hat / (jnp.sqrt(v_hat) + ADAM_EPS) + ADAM_WD * w)
    return delta, m2, v2


def _sum_slots(r_ref):
    g = r_ref[0]
    for s in range(1, r_ref.shape[0]):
        g = g + r_ref[s]
    return g


def _sum_adamw(recv, w, m, v, tr, name):
    rows, cols = w.shape

    def body(r_ref, w_ref, m_ref, v_ref, g_ref, d_ref, m2_ref, v2_ref):
        g = _sum_slots(r_ref)
        g_ref[...] = g
        d_ref[...], m2_ref[...], v2_ref[...] = _adamw(g, w_ref[...], m_ref[...], v_ref[...])

    blk = pl.BlockSpec((tr, cols), lambda i: (i, 0))
    return pl.pallas_call(
        body, name=name, grid=(rows // tr,),
        in_specs=[pl.BlockSpec((recv.shape[0], tr, cols), lambda i: (0, i, 0)), blk, blk, blk],
        out_specs=[blk] * 4, out_shape=[SDS((rows, cols), F32)] * 4,
        compiler_params=_cparams(("parallel",)),
    )(recv, w, m, v)


def _sum_adamw_repl(recv, w, m, v):
    def body(r_ref, w_ref, m_ref, v_ref, g_ref, d_ref, m2_ref, v2_ref, loss_ref):
        g = _sum_slots(r_ref)
        g_ref[...] = g
        d_ref[...], m2_ref[...], v2_ref[...] = _adamw(g, w_ref[...], m_ref[...], v_ref[...])
        total = jnp.sum(g[RP_LOSS:RP_LOSS + 1, :], axis=-1, keepdims=True)
        loss_ref[...] = jnp.broadcast_to(total, loss_ref.shape)

    return pl.pallas_call(
        body, name="adamw_repl",
        out_shape=[SDS((8, D_MODEL), F32)] * 4 + [SDS((8, 128), F32)],
    )(recv, w, m, v)


def _shard_rows(t, lead):
    r = t.shape[1] // N_DEV
    t = t.reshape((lead, N_DEV, r) + t.shape[2:])
    return jnp.moveaxis(t, 1, 0)


def _pad8(t):
    return jnp.pad(t, ((0, 0), (0, 8 - t.shape[1]), (0, 0)))


def _pack_small(wa, wx, cw, b_a, b_x):
    n = wa.shape[0]
    return jnp.concatenate([
        wa.reshape(n, 256, 128), wx.reshape(n, 256, 128), _pad8(cw),
        _pad8(b_a.reshape(n, 1, 128)), _pad8(b_x.reshape(n, 1, 128))], axis=1)


def _unpack_small(t):
    n = t.shape[0]
    return (t[:, SM_WA:SM_WA + 256].reshape(n, 4, 32, 256), t[:, SM_WX:SM_WX + 256].reshape(n, 4, 32, 256),
            t[:, SM_CW:SM_CW + 4], t[:, SM_BA].reshape(n, 4, 32), t[:, SM_BX].reshape(n, 4, 32))


def kernel(x, pre_norm_w, w_in, conv_w, conv_b, lru_w_a, lru_b_a, lru_w_x, lru_b_x, lru_lambda, hgrn_lb_logits, hgrn_gnorm_w, w_out, post_norm_w, loss_target, m_pre_norm_w, m_w_in, m_conv_w, m_conv_b, m_lru_w_a, m_lru_b_a, m_lru_w_x, m_lru_b_x, m_lru_lambda, m_hgrn_lb_logits, m_hgrn_gnorm_w, m_w_out, m_post_norm_w, v_pre_norm_w, v_w_in, v_conv_w, v_conv_b, v_lru_w_a, v_lru_b_a, v_lru_w_x, v_lru_b_x, v_lru_lambda, v_hgrn_lb_logits, v_hgrn_gnorm_w, v_w_out, v_post_norm_w):
    seq = x.shape[1]
    x2 = x.reshape(seq, D_MODEL)
    tgt = loss_target.reshape(seq, D_MODEL)

    small_w = _pack_small(lru_w_a, lru_w_x, conv_w, lru_b_a, lru_b_x)[0]
    me = (4 * lax.axis_index("x") + 2 * lax.axis_index("y") + lax.axis_index("c")).astype(jnp.int32).reshape(1)
    p, u, w_in_all, w_out_all, small_all = _in_proj_gather(
        x2, pre_norm_w, w_in[0].astype(BF16), w_out[0].astype(BF16), small_w, me, seq)
    wa_s, wx_s, cw_s, ba_s, bx_s = _unpack_small(small_all)
    wa = jnp.moveaxis(wa_s, 0, 1).reshape(LRU_BLOCKS, LRU_BW, LRU_BW).astype(BF16)
    wx = jnp.moveaxis(wx_s, 0, 1).reshape(LRU_BLOCKS, LRU_BW, LRU_BW).astype(BF16)
    cw = jnp.moveaxis(cw_s, 0, 1).reshape(4, D_MODEL)
    ba = jnp.moveaxis(ba_s, 0, 1).reshape(1, D_MODEL)
    bx = jnp.moveaxis(bx_s, 0, 1).reshape(1, D_MODEL)

    loc = _local_step(x2, tgt, p, u, cw, conv_b, wa, wx, ba, bx, lru_lambda,
                      hgrn_lb_logits, hgrn_gnorm_w, w_out_all.reshape(D_MIX, D_MODEL), post_norm_w)

    ls, r_out, r_small = loc["lru_small"], loc["r_out"], loc["r_small"]
    core = lax.axis_index("c").astype(jnp.int32).reshape(1)
    g_sib = _grad_w_in_sibling(loc["u"], loc["dp"], core, seq)
    g_chip = _grad_w_in_own(loc["u"], loc["dp"], core, g_sib, seq)
    grad_x, pre_small, r_in = _grad_x(loc["dp"], w_in_all, x2, pre_norm_w, loc["dout"], g_chip, seq)
    g_repl = _pack_rows([(pre_small, 0, 1, 1.0), (ls, 4, 1, 1.0), (ls, 7, 1, 1.0),
                         (loc["hgrn_small"], 0, 1, 1.0), (loc["hgrn_small"], 0, 1, -1.0),
                         (loc["hgrn_small"], 1, 1, 1.0), (loc["stats"], 0, 2, 1.0)], "pack_grads")
    (r_repl,) = _exchange_grads([], g_repl)

    zero = jnp.zeros((1, D_MODEL), F32)
    pack_w = lambda a, b, c_, d, e, f, name: _pack_rows(
        [(a, 0, 1, 1.0), (b, 0, 1, 1.0), (c_, 0, 1, 1.0), (d, 0, 2, 1.0), (e, 0, 1, 1.0), (f, 0, 1, 1.0),
         (zero, 0, 1, 1.0)], name)
    w_repl = pack_w(pre_norm_w, conv_b, lru_lambda, hgrn_lb_logits, hgrn_gnorm_w, post_norm_w, "pack_w")
    m_repl = pack_w(m_pre_norm_w, m_conv_b, m_lru_lambda, m_hgrn_lb_logits, m_hgrn_gnorm_w, m_post_norm_w, "pack_m")
    v_repl = pack_w(v_pre_norm_w, v_conv_b, v_lru_lambda, v_hgrn_lb_logits, v_hgrn_gnorm_w, v_post_norm_w, "pack_v")
    o_repl = _sum_adamw_repl(r_repl, w_repl, m_repl, v_repl)
    loss = o_repl[4][0, 0]

    o_in = _sum_adamw(r_in, w_in[0], m_w_in[0], v_w_in[0], 128, "adamw_w_in")
    o_out = _sum_adamw(r_out, w_out[0], m_w_out[0], v_w_out[0], 64, "adamw_w_out")
    o_small = _sum_adamw(r_small,
                         _pack_small(lru_w_a, lru_w_x, conv_w, lru_b_a, lru_b_x)[0],
                         _pack_small(m_lru_w_a, m_lru_w_x, m_conv_w, m_lru_b_a, m_lru_b_x)[0],
                         _pack_small(v_lru_w_a, v_lru_w_x, v_conv_w, v_lru_b_a, v_lru_b_x)[0],
                         SM_ROWS, "adamw_small")

    outs = [loss, grad_x.reshape(x.shape)]
    for kind in range(4):
        rp = o_repl[kind]
        swa, swx, scw, sba, sbx = _unpack_small(o_small[kind][None])
        outs += [rp[RP_PRE:RP_PRE + 1], o_in[kind][None], scw, rp[RP_CB:RP_CB + 1], swa, sba, swx, sbx,
                 rp[RP_LAM:RP_LAM + 1], rp[RP_LB0:RP_LB1 + 1], rp[RP_GN:RP_GN + 1], o_out[kind][None],
                 rp[RP_POST:RP_POST + 1]]
    return tuple(outs)
```

```python
import functools

import jax
import jax.numpy as jnp
from jax import lax
from jax.experimental import pallas as pl
from jax.experimental.pallas import tpu as pltpu

F32 = jnp.float32
BF16 = jnp.bfloat16
SDS = jax.ShapeDtypeStruct

D_MODEL = 1024
D_IN = 6144
N_DEV = 8
N_CHIPS = 4
W_BLK = D_IN // N_DEV
D_MIX = 2048
LRU_BLOCKS = 4
LRU_BW = 256
LRU_C = 8.0
N_HEADS = 8
HEAD_D = 128
CHUNK = 128
SUB = 32
N_SUB = CHUNK // SUB
EXP_CLAMP = 80.0
EPS = 1e-6

ADAM_LR = 0.001
ADAM_B1 = 0.9
ADAM_B2 = 0.999
ADAM_EPS = 1e-08
ADAM_WD = 0.01
ADAM_STEP = 10

VMEM_LIMIT = 56 * 1024 * 1024

NN = (((1,), (0,)), ((), ()))
NT = (((1,), (1,)), ((), ()))
TN = (((0,), (0,)), ((), ()))

SM_WA = 0
SM_WX = 256
SM_CW = 512
SM_BA = 520
SM_BX = 528
SM_ROWS = 536

RP_PRE, RP_CB, RP_LAM, RP_LB0, RP_LB1, RP_GN, RP_POST, RP_LOSS = range(8)


def _dot(a, b, dims=NN):
    return lax.dot_general(a, b, dims, preferred_element_type=F32)


def _sigmoid(x):
    return 0.5 * jnp.tanh(0.5 * x) + 0.5


def _sigmoid_pos(x):
    return 1.0 / (1.0 + jnp.exp(-x))


def _cparams(sem, vmem=VMEM_LIMIT):
    return pltpu.CompilerParams(dimension_semantics=sem, vmem_limit_bytes=vmem)


def _iota(shape, axis):
    return lax.broadcasted_iota(jnp.int32, shape, axis)


def _softplus_neg(lam):
    z = -lam
    e = jnp.exp(-jnp.abs(z))
    u = 1.0 + e
    log1p_e = jnp.where(u == 1.0, e, jnp.log(u) * (e / (u - 1.0)))
    sp = jnp.maximum(z, 0.0) + log1p_e
    dsp = -jnp.where(z >= 0.0, 1.0 / u, e / u)
    return sp, dsp


def _neg_expm1(x):
    poly = x * (1.0 + x * (1.0 / 2 + x * (1.0 / 6 + x * (1.0 / 24 + x * (1.0 / 120)))))
    return jnp.where(x > -1.0 / 16, -poly, 1.0 - jnp.exp(x))


def _conv_taps(lx, prev8, cw_ref, cb_ref, tile):
    xc = cb_ref[...] + cw_ref[3:4, :] * lx
    for j in (1, 2, 3):
        xc = xc + cw_ref[3 - j:4 - j, :] * pltpu.roll(lx, j, 0)
    row8 = _iota((8, D_MODEL), 0)
    last8 = lx[tile - 8:tile, :]
    fix = jnp.zeros((8, D_MODEL), F32)
    for j in (1, 2, 3):
        wrong = pltpu.roll(last8, j, 0)
        right = pltpu.roll(prev8, j, 0)
        fix = fix + cw_ref[3 - j:4 - j, :] * jnp.where(row8 < j, right - wrong, 0.0)
    return xc, fix


def _lru_gates(xcs, wa, wx, ba, bx, sp):
    xb = xcs.astype(BF16)
    r = _sigmoid_pos(_dot(xb, wa) + ba)
    i = _sigmoid(_dot(xb, wx) + bx)
    la = (-LRU_C * sp) * r
    a = jnp.exp(la)
    one_minus_a2 = _neg_expm1(2.0 * la)
    return r, i, a, one_minus_a2


def _lru_forward(p, conv_w, conv_b, wa, wx, ba, bx, lam, seq):
    tile = min(512, seq // 2)
    nblk = tile // 8

    def body(lx_ref, gt_ref, cw_ref, cb_ref, wa_ref, wx_ref, ba_ref, bx_ref, lam_ref,
             h_ref, y_ref, prev8, hcar, xc_scr, a_scr, u_scr):
        @pl.when(pl.program_id(0) == 0)
        def _():
            prev8[...] = jnp.zeros_like(prev8)
            hcar[...] = jnp.zeros_like(hcar)

        lx = lx_ref[...]
        xc, fix = _conv_taps(lx, prev8[...], cw_ref, cb_ref, tile)
        xc_scr[...] = xc
        xc_scr[0:8, :] = xc_scr[0:8, :] + fix
        prev8[...] = lx_ref[tile - 8:tile, :]
        sp, _ = _softplus_neg(lam_ref[...])
        for n in range(LRU_BLOCKS):
            sl = slice(n * LRU_BW, (n + 1) * LRU_BW)
            xcs = xc_scr[:, sl]
            _, i, a, ne = _lru_gates(xcs, wa_ref[n], wx_ref[n], ba_ref[:, sl], bx_ref[:, sl], sp[:, sl])
            a_scr[:, sl] = a
            u_scr[:, sl] = jnp.sqrt(ne) * (i * xcs)

        row8 = _iota((8, D_MODEL), 0)

        def blk(j, hc):
            off = pl.multiple_of(j * 8, 8)
            a = a_scr[pl.ds(off, 8), :]
            u = u_scr[pl.ds(off, 8), :]
            for k in (1, 2, 4):
                m = row8 >= k
                u = jnp.where(m, u + a * pltpu.roll(u, k, 0), u)
                a = jnp.where(m, a * pltpu.roll(a, k, 0), a)
            h = u + a * hc
            h_ref[pl.ds(off, 8), :] = h
            return jnp.broadcast_to(h[7:8, :], (8, D_MODEL))

        hcar[...] = lax.fori_loop(0, nblk, blk, hcar[...])
        g = gt_ref[...]
        y_ref[...] = (h_ref[...] * (g * _sigmoid(g))).astype(BF16)

    full = lambda shape: pl.BlockSpec(shape, lambda t: (0,) * len(shape))
    return pl.pallas_call(
        body, name="lru_fwd", grid=(seq // tile,),
        in_specs=[pl.BlockSpec((tile, D_MODEL), lambda t: (t, 0)),
                  pl.BlockSpec((tile, D_MODEL), lambda t: (t, 1)),
                  full((4, D_MODEL)), full((1, D_MODEL)),
                  full((LRU_BLOCKS, LRU_BW, LRU_BW)), full((LRU_BLOCKS, LRU_BW, LRU_BW)),
                  full((1, D_MODEL)), full((1, D_MODEL)), full((1, D_MODEL))],
        out_specs=[pl.BlockSpec((tile, D_MODEL), lambda t: (t, 0)),
                   pl.BlockSpec((tile, D_MODEL), lambda t: (t, 0))],
        out_shape=[SDS((seq, D_MODEL), F32), SDS((seq, D_MODEL), BF16)],
        scratch_shapes=[pltpu.VMEM((8, D_MODEL), F32), pltpu.VMEM((8, D_MODEL), F32),
                        pltpu.VMEM((tile, D_MODEL), F32), pltpu.VMEM((tile, D_MODEL), F32),
                        pltpu.VMEM((tile, D_MODEL), F32)],
        compiler_params=_cparams(("arbitrary",)),
    )(p, p, conv_w, conv_b, wa, wx, ba, bx, lam)


def _lru_backward(p, h, dymix, conv_w, conv_b, wa, wx, ba, bx, lam, seq):
    tile = min(256, seq // 2)
    nt = seq // tile
    nblk = tile // 8
    t8 = tile // 8

    def body(lx_ref, lxh_ref, gt_ref, h_ref, hh_ref, dy_ref, cw_ref, cb_ref, wa_ref, wx_ref, ba_ref,
             bx_ref, lam_ref, dp_ref, gwa_ref, gwx_ref, gsm_ref,
             lamcar, anext, dxc8, xc_scr, r_scr, i_scr, a_scr, m_scr, rm_scr, c_scr, l_scr, dxc_scr):
        step = pl.program_id(0)
        first_tile = step == nt - 1

        @pl.when(step == 0)
        def _():
            lamcar[...] = jnp.zeros_like(lamcar)
            anext[...] = jnp.zeros_like(anext)
            dxc8[...] = jnp.zeros_like(dxc8)
            gwa_ref[...] = jnp.zeros_like(gwa_ref)
            gwx_ref[...] = jnp.zeros_like(gwx_ref)
            gsm_ref[...] = jnp.zeros_like(gsm_ref)

        keep = jnp.where(first_tile, 0.0, 1.0)
        lx = lx_ref[...]
        prev8 = lxh_ref[...] * keep
        xc, fix = _conv_taps(lx, prev8, cw_ref, cb_ref, tile)
        xc_scr[...] = xc
        xc_scr[0:8, :] = xc_scr[0:8, :] + fix
        sp, dsp = _softplus_neg(lam_ref[...])
        for n in range(LRU_BLOCKS):
            sl = slice(n * LRU_BW, (n + 1) * LRU_BW)
            r, i, a, ne = _lru_gates(xc_scr[:, sl], wa_ref[n], wx_ref[n], ba_ref[:, sl], bx_ref[:, sl],
                                     sp[:, sl])
            r_scr[:, sl] = r
            i_scr[:, sl] = i
            a_scr[:, sl] = a
            m_scr[:, sl] = jnp.sqrt(ne)
            rm_scr[:, sl] = lax.rsqrt(ne)

        g = gt_ref[...]
        sg = _sigmoid(g)
        dy = dy_ref[...]
        hv = h_ref[...]
        dp_ref[:, D_MODEL:2 * D_MODEL] = (dy * hv * (sg * (1.0 + g * (1.0 - sg)))).astype(BF16)

        rowt = _iota((tile, D_MODEL), 0)
        av = a_scr[...]
        l_scr[...] = dy * (g * sg)
        c_scr[...] = jnp.where(rowt == tile - 1, anext[...][0:1, :], pltpu.roll(av, tile - 1, 0))
        anext[...] = jnp.broadcast_to(av[0:1, :], (8, D_MODEL))
        row8 = _iota((8, D_MODEL), 0)

        def blk(jj, lc):
            off = pl.multiple_of((nblk - 1 - jj) * 8, 8)
            c = c_scr[pl.ds(off, 8), :]
            u = l_scr[pl.ds(off, 8), :]
            for k in (1, 2, 4):
                m = row8 < 8 - k
                u = jnp.where(m, u + c * pltpu.roll(u, 8 - k, 0), u)
                c = jnp.where(m, c * pltpu.roll(c, 8 - k, 0), c)
            lamv = u + c * lc
            l_scr[pl.ds(off, 8), :] = lamv
            return jnp.broadcast_to(lamv[0:1, :], (8, D_MODEL))

        lamcar[...] = lax.fori_loop(0, nblk, blk, lamcar[...])

        hprev = jnp.where(rowt == 0, hh_ref[...][7:8, :] * keep, pltpu.roll(hv, 1, 0))
        for n in range(LRU_BLOCKS):
            sl = slice(n * LRU_BW, (n + 1) * LRU_BW)
            lamv = l_scr[:, sl]
            xcs = xc_scr[:, sl]
            r = r_scr[:, sl]
            i = i_scr[:, sl]
            a = a_scr[:, sl]
            mult = m_scr[:, sl]
            d_la = lamv * hprev[:, sl] * a - (lamv * i * xcs) * (a * a * rm_scr[:, sl])
            d_pr = d_la * (-LRU_C * sp[:, sl]) * r * (1.0 - r)
            d_pi = (lamv * mult * xcs) * i * (1.0 - i)
            gsm_ref[7:8, sl] += jnp.sum(d_la * r, axis=0, keepdims=True) * (-LRU_C) * dsp[:, sl]
            gsm_ref[5:6, sl] += jnp.sum(d_pr, axis=0, keepdims=True)
            gsm_ref[6:7, sl] += jnp.sum(d_pi, axis=0, keepdims=True)
            xb = xcs.astype(BF16)
            prb = d_pr.astype(BF16)
            pib = d_pi.astype(BF16)
            gwa_ref[n] += _dot(xb, prb, TN)
            gwx_ref[n] += _dot(xb, pib, TN)
            dxc_scr[:, sl] = lamv * mult * i + _dot(prb, wa_ref[n], NT) + _dot(pib, wx_ref[n], NT)

        dxc = dxc_scr[...]
        gsm_ref[4:5, :] += jnp.sum(dxc, axis=0, keepdims=True)
        last8 = lx[tile - 8:tile, :]
        first8 = dxc[0:8, :]
        dlx = cw_ref[3:4, :] * dxc
        gsm_ref[3:4, :] += jnp.sum(dxc * lx, axis=0, keepdims=True)
        fix = jnp.zeros((8, D_MODEL), F32)
        for j in (1, 2, 3):
            w = cw_ref[3 - j:4 - j, :]
            dlx = dlx + w * pltpu.roll(dxc, tile - j, 0)
            fix = fix + w * jnp.where(row8 + j >= 8,
                                      pltpu.roll(dxc8[...], 8 - j, 0) - pltpu.roll(first8, 8 - j, 0), 0.0)
            halo = jnp.where(row8 < j, pltpu.roll(prev8, j, 0) - pltpu.roll(last8, j, 0), 0.0)
            gsm_ref[3 - j:4 - j, :] += (jnp.sum(dxc * pltpu.roll(lx, j, 0), axis=0, keepdims=True)
                                        + jnp.sum(first8 * halo, axis=0, keepdims=True))
        dxc8[...] = first8
        dp_ref[:, 0:D_MODEL] = dlx.astype(BF16)
        top = tile - 8
        dp_ref[top:tile, 0:D_MODEL] = (dlx[top:tile, :] + fix).astype(BF16)

    rev = lambda t: (nt - 1 - t, 0)
    halo_idx = lambda t: (jnp.maximum((nt - 1 - t) * t8 - 1, 0), 0)
    full = lambda shape: pl.BlockSpec(shape, lambda t: (0,) * len(shape))
    big = lambda: pltpu.VMEM((tile, D_MODEL), F32)
    return pl.pallas_call(
        body, name="lru_bwd", grid=(nt,),
        in_specs=[pl.BlockSpec((tile, D_MODEL), rev),
                  pl.BlockSpec((8, D_MODEL), halo_idx),
                  pl.BlockSpec((tile, D_MODEL), lambda t: (nt - 1 - t, 1)),
                  pl.BlockSpec((tile, D_MODEL), rev),
                  pl.BlockSpec((8, D_MODEL), halo_idx),
                  pl.BlockSpec((tile, D_MODEL), rev),
                  full((4, D_MODEL)), full((1, D_MODEL)),
                  full((LRU_BLOCKS, LRU_BW, LRU_BW)), full((LRU_BLOCKS, LRU_BW, LRU_BW)),
                  full((1, D_MODEL)), full((1, D_MODEL)), full((1, D_MODEL))],
        out_specs=[pl.BlockSpec((tile, 2 * D_MODEL), rev),
                   full((LRU_BLOCKS, LRU_BW, LRU_BW)), full((LRU_BLOCKS, LRU_BW, LRU_BW)),
                   full((8, D_MODEL))],
        out_shape=[SDS((seq, D_IN), BF16), SDS((LRU_BLOCKS, LRU_BW, LRU_BW), F32),
                   SDS((LRU_BLOCKS, LRU_BW, LRU_BW), F32), SDS((8, D_MODEL), F32)],
        scratch_shapes=[pltpu.VMEM((8, D_MODEL), F32), pltpu.VMEM((8, D_MODEL), F32),
                        pltpu.VMEM((8, D_MODEL), F32)] + [big() for _ in range(9)],
        compiler_params=_cparams(("arbitrary",)),
    )(p, p, p, h, h, dymix, conv_w, conv_b, wa, wx, ba, bx, lam)


def _split3(g):
    hi = g.astype(BF16)
    r1 = g - hi.astype(F32)
    mid = r1.astype(BF16)
    lo = (r1 - mid.astype(F32)).astype(BF16)
    return hi, mid, lo


def _tri_matmul(tri, g):
    hi, mid, lo = _split3(g)
    return _dot(tri, lo) + _dot(tri, mid) + _dot(tri, hi)


def _hgrn_gate_terms(q, fr, lbl):
    lb = _sigmoid_pos(lbl[0:1, :] - lbl[1:2, :])
    sig = _sigmoid(fr)
    f = lb + (1.0 - lb) * sig
    sq = _sigmoid(q)
    return lb, sig, f, sq


def _hgrn_decay(bh):
    zero = jnp.zeros((1, bh.shape[1]), F32)
    rho = [zero] + [bh[s * SUB - 1:s * SUB, :] for s in range(1, N_SUB + 1)]
    start = _sub_rows(rho[0:N_SUB])
    end = _sub_rows(rho[1:N_SUB + 1])
    mid = 0.5 * (start + end)
    blast = rho[N_SUB]
    e_on = jnp.exp(jnp.minimum(bh - start, 0.0))
    e_off = jnp.exp(jnp.minimum(end - bh, 0.0))
    scales = [_sub_rows([jnp.exp(rho[i] - rho[j + 1]) if i > j else zero for i in range(N_SUB)])
              for j in range(N_SUB - 1)]
    return dict(eq0=jnp.exp(jnp.minimum(bh - mid, EXP_CLAMP)), ek0=jnp.exp(jnp.minimum(mid - bh, EXP_CLAMP)),
                e_on=e_on, e_off=e_off, scales=scales,
                eb=e_on * _sub_rows([jnp.exp(r) for r in rho[0:N_SUB]]),
                ekst=e_off * _sub_rows([jnp.exp(blast - r) for r in rho[1:N_SUB + 1]]),
                ebl=jnp.exp(blast))


def _sub_rows(vecs):
    return jnp.concatenate([jnp.broadcast_to(v, (SUB, v.shape[1])) for v in vecs], axis=0)


def _hgrn_operands(qs, k, dec, qt_scr, kt_scr):
    sub = jnp.right_shift(_iota(qs.shape, 0), 5)
    qon = qs * dec["e_on"]
    koff = k * dec["e_off"]
    qt_scr[0] = (qs * dec["eq0"]).astype(BF16)
    kt_scr[0] = (k * dec["ek0"]).astype(BF16)
    for j in range(N_SUB - 1):
        qt_scr[j + 1] = (qon * dec["scales"][j]).astype(BF16)
        kt_scr[j + 1] = jnp.where(sub == j, koff, 0.0).astype(BF16)
    return koff


def _hgrn_head_scores(qt_scr, kt_scr, sl, diag):
    a = jnp.where(diag, _dot(qt_scr[0, :, sl], kt_scr[0, :, sl], NT), 0.0)
    for j in range(1, N_SUB):
        a = a + _dot(qt_scr[j, :, sl], kt_scr[j, :, sl], NT)
    return a


def _hgrn_forward(p, lbl, gw, seq):
    nc = seq // CHUNK
    assert SUB == 32

    def body(q_ref, f_ref, v_ref, hg_ref, lbl_ref, gw_ref, y_ref, o_ref, st_ref,
             s_scr, qt_scr, kt_scr, qin_scr, kst_scr, vb_scr, a_scr):
        @pl.when(pl.program_id(0) == 0)
        def _():
            s_scr[...] = jnp.zeros_like(s_scr)

        r = _iota((CHUNK, CHUNK), 0)
        c = _iota((CHUNK, CHUNK), 1)
        tri = jnp.where(c <= r, 1.0, 0.0).astype(BF16)
        diag = (jnp.right_shift(r, 5) == jnp.right_shift(c, 5)) & (c <= r)
        q = q_ref[...]
        _, _, f, sq = _hgrn_gate_terms(q, f_ref[...], lbl_ref[...])
        qs = q * sq
        k = 1.0 - f
        dec = _hgrn_decay(_tri_matmul(tri, jnp.log(f)))
        _hgrn_operands(qs, k, dec, qt_scr, kt_scr)
        qin_scr[...] = (qs * dec["eb"]).astype(BF16)
        kst_scr[...] = (k * dec["ekst"]).astype(BF16)
        vb_scr[...] = v_ref[...].astype(BF16)
        ebl = dec["ebl"]
        hg = hg_ref[...]
        gate = gw_ref[...] * (hg * _sigmoid(hg))
        heads = [slice(h * HEAD_D, (h + 1) * HEAD_D) for h in range(N_HEADS)]
        stb = []
        for h, sl in enumerate(heads):
            st = s_scr[h]
            st_ref[0, h] = st
            stb.append(st.astype(BF16))
            s_scr[h] = st * ebl[:, sl] + _dot(vb_scr[:, sl], kst_scr[:, sl], TN)
        for h, sl in enumerate(heads):
            a_scr[h] = _hgrn_head_scores(qt_scr, kt_scr, sl, diag).astype(BF16)
        for h, sl in enumerate(heads):
            o = _dot(a_scr[h], vb_scr[:, sl]) + _dot(qin_scr[:, sl], stb[h], NT)
            o_ref[:, sl] = o
            rs = lax.rsqrt(jnp.mean(o * o, axis=-1, keepdims=True) + EPS)
            y_ref[:, sl] = ((o * rs) * gate[:, sl]).astype(BF16)

    col = lambda j: pl.BlockSpec((CHUNK, D_MODEL), lambda c: (c, j))
    par = lambda rows: pl.BlockSpec((rows, D_MODEL), lambda c: (0, 0))
    return pl.pallas_call(
        body, name="hgrn_fwd", grid=(nc,),
        in_specs=[col(2), col(3), col(4), col(5), par(2), par(1)],
        out_specs=[col(0), col(0),
                   pl.BlockSpec((1, N_HEADS, HEAD_D, HEAD_D), lambda c: (c, 0, 0, 0))],
        out_shape=[SDS((seq, D_MODEL), BF16), SDS((seq, D_MODEL), F32),
                   SDS((nc, N_HEADS, HEAD_D, HEAD_D), F32)],
        scratch_shapes=[pltpu.VMEM((N_HEADS, HEAD_D, HEAD_D), F32),
                        pltpu.VMEM((N_SUB, CHUNK, D_MODEL), BF16), pltpu.VMEM((N_SUB, CHUNK, D_MODEL), BF16)]
                       + [pltpu.VMEM((CHUNK, D_MODEL), BF16)] * 3 + [pltpu.VMEM((N_HEADS, CHUNK, CHUNK), BF16)],
        compiler_params=_cparams(("arbitrary",)),
    )(p, p, p, p, lbl, gw)


def _hgrn_backward(p, o, states, dymix, lbl, gw, dp_full, g_w_out, g_small, seq):
    nc = seq // CHUNK

    def body(q_ref, f_ref, v_ref, hg_ref, o_ref, st_ref, dy_ref, lbl_ref, gw_ref, dpin_ref, go_ref, gs_ref,
             dpo_ref, gsm_ref, ro_ref, rs_ref, ds_scr, dp_buf, dp_sems,
             qt_scr, kt_scr, qin_scr, kst_scr, vb_scr, dob_scr, g_scr, h_scr, dqi_scr, dks_scr, sd_scr,
             a_scr, da_scr, da0_scr, *sems):
        del dpin_ref
        step = pl.program_id(0)
        slot = step % 2
        exs = [_SlotExchange(go_ref, ro_ref, *sems[0:3], blocked=True),
               _SlotExchange(gs_ref, rs_ref, *sems[3:6], blocked=True)]

        @pl.when(step == 0)
        def _():
            for ex in exs:
                ex.start()

        def out_copy(s, chunk):
            rows = pl.ds(pl.multiple_of(chunk * CHUNK, CHUNK), CHUNK)
            return pltpu.make_async_copy(dp_buf.at[s], dpo_ref.at[rows, pl.ds(2 * D_MODEL, 4 * D_MODEL)],
                                         dp_sems.at[s])

        @pl.when(step == 0)
        def _():
            ds_scr[...] = jnp.zeros_like(ds_scr)
            gsm_ref[...] = jnp.zeros_like(gsm_ref)

        @pl.when(step >= 2)
        def _():
            out_copy(slot, nc + 1 - step).wait()

        dp_ref = dp_buf.at[slot]

        r = _iota((CHUNK, CHUNK), 0)
        c = _iota((CHUNK, CHUNK), 1)
        tri = jnp.where(c <= r, 1.0, 0.0).astype(BF16)
        triu = jnp.where(c >= r, 1.0, 0.0).astype(BF16)
        diag = (jnp.right_shift(r, 5) == jnp.right_shift(c, 5)) & (c <= r)
        row = _iota((CHUNK, D_MODEL), 0)
        sub = jnp.right_shift(row, 5)

        q = q_ref[...]
        lb, sig, f, sq = _hgrn_gate_terms(q, f_ref[...], lbl_ref[...])
        qs = q * sq
        k = 1.0 - f
        dec = _hgrn_decay(_tri_matmul(tri, jnp.log(f)))
        eb, ekst, ebl = dec["eb"], dec["ekst"], dec["ebl"]
        koff = _hgrn_operands(qs, k, dec, qt_scr, kt_scr)
        qin_scr[...] = (qs * eb).astype(BF16)
        kst_scr[...] = (k * ekst).astype(BF16)
        vb_scr[...] = v_ref[...].astype(BF16)
        hg = hg_ref[...]
        sh = _sigmoid(hg)
        dy = dy_ref[...]
        gwv = gw_ref[...]
        d_onw = dy * (hg * sh)
        d_on = d_onw * gwv
        d_gate = dy * gwv * (sh * (1.0 + hg * (1.0 - sh)))

        heads = [slice(h * HEAD_D, (h + 1) * HEAD_D) for h in range(N_HEADS)]
        for h, sl in enumerate(heads):
            o = o_ref[:, sl]
            rs = lax.rsqrt(jnp.mean(o * o, axis=-1, keepdims=True) + EPS)
            on = o * rs
            dp_ref[:, 3 * D_MODEL + h * HEAD_D:3 * D_MODEL + (h + 1) * HEAD_D] = (d_gate[:, sl] * on).astype(BF16)
            gsm_ref[1:2, sl] += jnp.sum(d_onw[:, sl] * on, axis=0, keepdims=True)
            d_onh = d_on[:, sl]
            dob_scr[:, sl] = (rs * (d_onh - on * jnp.mean(d_onh * on, axis=-1, keepdims=True))).astype(BF16)
        for h, sl in enumerate(heads):
            a_scr[h] = _hgrn_head_scores(qt_scr, kt_scr, sl, diag).astype(BF16)
            da = _dot(dob_scr[:, sl], vb_scr[:, sl], NT)
            da_scr[h] = da.astype(BF16)
            da0_scr[h] = jnp.where(diag, da, 0.0).astype(BF16)
        for h, sl in enumerate(heads):
            st = st_ref[0, h]
            dst = ds_scr[h]
            dstb = dst.astype(BF16)
            dp_ref[:, 2 * D_MODEL + h * HEAD_D:2 * D_MODEL + (h + 1) * HEAD_D] = (
                _dot(a_scr[h], dob_scr[:, sl], TN) + _dot(kst_scr[:, sl], dstb, NT)).astype(BF16)
            dqi_scr[:, sl] = _dot(dob_scr[:, sl], st.astype(BF16))
            dks_scr[:, sl] = _dot(vb_scr[:, sl], dstb)
            sd_scr[0:1, sl] = jnp.sum(st * dst, axis=0, keepdims=True)
            ds_scr[h] = dst * ebl[:, sl] + _dot(dob_scr[:, sl], qin_scr[:, sl], TN)
        for h, sl in enumerate(heads):
            g_scr[0, :, sl] = _dot(da0_scr[h], kt_scr[0, :, sl])
            h_scr[0, :, sl] = _dot(da0_scr[h], qt_scr[0, :, sl], TN)
            for j in range(1, N_SUB):
                g_scr[j, :, sl] = _dot(da_scr[h], kt_scr[j, :, sl])
                h_scr[j, :, sl] = _dot(da_scr[h], qt_scr[j, :, sl], TN)

        g0 = g_scr[0]
        h0 = h_scr[0]
        dq_inter = eb * dqi_scr[...]
        d_kst = ekst * dks_scr[...]
        db = qs * dq_inter - k * d_kst + qt_scr[0].astype(F32) * g0 - kt_scr[0].astype(F32) * h0
        gq = jnp.zeros((CHUNK, D_MODEL), F32)
        hsel = jnp.zeros((CHUNK, D_MODEL), F32)
        for j in range(N_SUB - 1):
            gj = g_scr[j + 1]
            gq = gq + dec["scales"][j] * gj
            db = db + qt_scr[j + 1].astype(F32) * gj
            hsel = jnp.where(sub == j, h_scr[j + 1], hsel)
        db = db - koff.astype(BF16).astype(F32) * hsel
        d_q = dec["eq0"] * g0 + dec["e_on"] * gq + dq_inter
        d_k = dec["ek0"] * h0 + dec["e_off"] * hsel + d_kst
        db_last = jnp.sum(k * d_kst, axis=0, keepdims=True) + ebl * sd_scr[0:1, :]
        db = db + jnp.where(row == CHUNK - 1, db_last, 0.0)
        dg = _tri_matmul(triu, db)
        d_f = dg / f - d_k
        dp_ref[:, D_MODEL:2 * D_MODEL] = (d_f * (1.0 - lb) * sig * (1.0 - sig)).astype(BF16)
        gsm_ref[0:1, :] += jnp.sum(d_f * (1.0 - sig), axis=0, keepdims=True) * (lb * (1.0 - lb))
        dp_ref[:, 0:D_MODEL] = (d_q * (sq * (1.0 + q * (1.0 - sq)))).astype(BF16)

        out_copy(slot, nc - 1 - step).start()

        @pl.when(step == nc - 1)
        def _():
            out_copy(1 - slot, 1).wait()
            out_copy(slot, 0).wait()
            for ex in exs:
                ex.wait()

    rc = lambda c: nc - 1 - c
    col = lambda j: pl.BlockSpec((CHUNK, D_MODEL), lambda c: (rc(c), j))
    par = lambda rows: pl.BlockSpec((rows, D_MODEL), lambda c: (0, 0))
    return pl.pallas_call(
        body, name="hgrn_bwd", grid=(nc,),
        in_specs=[col(2), col(3), col(4), col(5), col(0),
                  pl.BlockSpec((1, N_HEADS, HEAD_D, HEAD_D), lambda c: (rc(c), 0, 0, 0)),
                  col(1), par(2), par(1), ANY, ANY, ANY],
        out_specs=[ANY, par(8), ANY, ANY],
        out_shape=[SDS((seq, D_IN), BF16), SDS((8, D_MODEL), F32), SDS(g_w_out.shape, F32),
                   SDS(g_small.shape, F32)],
        input_output_aliases={9: 0},
        scratch_shapes=[pltpu.VMEM((N_HEADS, HEAD_D, HEAD_D), F32),
                        pltpu.VMEM((2, CHUNK, 4 * D_MODEL), BF16), pltpu.SemaphoreType.DMA((2,)),
                        pltpu.VMEM((N_SUB, CHUNK, D_MODEL), BF16), pltpu.VMEM((N_SUB, CHUNK, D_MODEL), BF16)]
                       + [pltpu.VMEM((CHUNK, D_MODEL), BF16)] * 4
                       + [pltpu.VMEM((N_SUB, CHUNK, D_MODEL), F32)] * 2 + [pltpu.VMEM((CHUNK, D_MODEL), F32)] * 2
                       + [pltpu.VMEM((8, D_MODEL), F32)] + [pltpu.VMEM((N_HEADS, CHUNK, CHUNK), BF16)] * 3
                       + EXCHANGE_SEMS * 2,
        compiler_params=_cparams(("arbitrary",)),
    )(p, p, p, p, o, states, dymix, lbl, gw, dp_full, g_w_out, g_small)


def _out_proj(yl, yh, wo, x, tgt, post_w, seq):
    tm = 256

    def body(yl_ref, yh_ref, wo_ref, x_ref, tg_ref, pw_ref, dymix_ref, dout_ref, gwo_ref, st_ref):
        @pl.when(pl.program_id(0) == 0)
        def _():
            gwo_ref[...] = jnp.zeros_like(gwo_ref)
            st_ref[...] = jnp.zeros_like(st_ref)

        ylv = yl_ref[...]
        yhv = yh_ref[...]
        y = _dot(ylv, wo_ref[0:D_MODEL, :]) + _dot(yhv, wo_ref[D_MODEL:D_MIX, :])
        r2 = lax.rsqrt(jnp.mean(y * y, axis=-1, keepdims=True) + EPS)
        yn = y * r2
        pw = pw_ref[...]
        e = (x_ref[...] + yn * pw) - tg_ref[...]
        st_ref[1:2, :] += jnp.sum(e * e, axis=0, keepdims=True) * (0.5 / D_MODEL)
        dout = e * (1.0 / D_MODEL)
        dout_ref[...] = dout
        st_ref[0:1, :] += jnp.sum(dout * yn, axis=0, keepdims=True)
        dyn = dout * pw
        dy = r2 * (dyn - yn * jnp.mean(dyn * yn, axis=-1, keepdims=True))
        dyb = dy.astype(BF16)
        dymix_ref[...] = _dot(dyb, wo_ref[...], NT)
        gwo_ref[0:D_MODEL, :] += _dot(ylv, dyb, TN)
        gwo_ref[D_MODEL:D_MIX, :] += _dot(yhv, dyb, TN)

    row = lambda w: pl.BlockSpec((tm, w), lambda m: (m, 0))
    full = lambda shape: pl.BlockSpec(shape, lambda m: (0,) * len(shape))
    return pl.pallas_call(
        body, name="out_proj", grid=(seq // tm,),
        in_specs=[row(D_MODEL), row(D_MODEL), full((D_MIX, D_MODEL)), row(D_MODEL), row(D_MODEL),
                  full((1, D_MODEL))],
        out_specs=[row(D_MIX), row(D_MODEL), full((D_MIX, D_MODEL)), full((8, D_MODEL))],
        out_shape=[SDS((seq, D_MIX), F32), SDS((seq, D_MODEL), F32), SDS((D_MIX, D_MODEL), F32),
                   SDS((8, D_MODEL), F32)],
        compiler_params=_cparams(("arbitrary",)),
    )(yl, yh, wo, x, tgt, post_w)


MESH = pl.DeviceIdType.MESH
ANY = pl.BlockSpec(memory_space=pl.ANY)
EXCHANGE_SEMS = [pltpu.SemaphoreType.DMA((N_DEV - 1,)), pltpu.SemaphoreType.DMA((N_DEV - 1,)),
                 pltpu.SemaphoreType.DMA(())]


def _mesh_pos():
    return lax.axis_index("x"), lax.axis_index("y"), lax.axis_index("c")


class _SlotExchange:
    def __init__(self, src_ref, dst_ref, send_sems, recv_sems, local_sem, blocked):
        x, y, c = _mesh_pos()
        me = 4 * x + 2 * y + c
        src = (lambda dest: src_ref.at[dest]) if blocked else (lambda dest: src_ref)
        self.local = pltpu.make_async_copy(src(me), dst_ref.at[me], local_sem)
        self.sends, self.recvs = [], []
        for k in range(1, N_DEV):
            px = 1 - x if (k >> 2) & 1 else x
            py = 1 - y if (k >> 1) & 1 else y
            pc = 1 - c if k & 1 else c
            peer = 4 * px + 2 * py + pc
            sems = dict(send_sem=send_sems.at[k - 1], recv_sem=recv_sems.at[k - 1],
                        device_id=(px, py, pc), device_id_type=MESH)
            self.sends.append(pltpu.make_async_remote_copy(src_ref=src(peer), dst_ref=dst_ref.at[me], **sems))
            self.recvs.append(pltpu.make_async_remote_copy(src_ref=dst_ref.at[peer], dst_ref=dst_ref.at[peer], **sems))

    def start(self):
        self.local.start()
        for cp in self.sends:
            cp.start()

    def wait(self):
        for cp in self.recvs:
            cp.wait_recv()
        for cp in self.sends:
            cp.wait_send()
        self.local.wait()


class _ChipExchange:
    def __init__(self, src_ref, dst_ref, send_sems, recv_sems, local_sem):
        x, y, c = _mesh_pos()
        chip = 2 * x + y
        self.local = pltpu.make_async_copy(src_ref.at[chip], dst_ref.at[chip], local_sem)
        self.sends, self.recvs = [], []
        for k in range(1, N_CHIPS):
            px = 1 - x if (k >> 1) & 1 else x
            py = 1 - y if k & 1 else y
            peer = 2 * px + py
            sems = dict(send_sem=send_sems.at[k - 1], recv_sem=recv_sems.at[k - 1],
                        device_id=(px, py, c), device_id_type=MESH)
            self.sends.append(pltpu.make_async_remote_copy(src_ref=src_ref.at[peer], dst_ref=dst_ref.at[chip], **sems))
            self.recvs.append(pltpu.make_async_remote_copy(src_ref=dst_ref.at[peer], dst_ref=dst_ref.at[peer], **sems))

    def start(self):
        self.local.start()
        for cp in self.sends:
            cp.start()

    def wait(self):
        for cp in self.recvs:
            cp.wait_recv()
        for cp in self.sends:
            cp.wait_send()
        self.local.wait()


GRAD_W_IN_TK = 512


def _grad_w_in_sibling(u, dp, core, seq):
    nk = seq // GRAD_W_IN_TK

    def body(core_ref, u_ref, dp_ref, g_ref):
        del core_ref

        @pl.when(pl.program_id(1) == 0)
        def _():
            g_ref[...] = jnp.zeros_like(g_ref)

        g_ref[0] += _dot(u_ref[...], dp_ref[...], TN)

    return pl.pallas_call(
        body, name="grad_w_in_sibling",
        grid_spec=pltpu.PrefetchScalarGridSpec(
            num_scalar_prefetch=1, grid=(N_CHIPS, nk),
            in_specs=[pl.BlockSpec((GRAD_W_IN_TK, D_MODEL), lambda n, k, c: (k, 0)),
                      pl.BlockSpec((GRAD_W_IN_TK, W_BLK), lambda n, k, c: (k, 2 * n + 1 - c[0]))],
            out_specs=pl.BlockSpec((1, D_MODEL, W_BLK), lambda n, k, c: (n, 0, 0))),
        out_shape=SDS((N_CHIPS, D_MODEL, W_BLK), F32),
        compiler_params=_cparams(("parallel", "arbitrary")),
    )(core, u, dp)


def _grad_w_in_own(u, dp, core, g_sib, seq):
    nk = seq // GRAD_W_IN_TK

    def body(core_ref, u_ref, dp_ref, gsib_ref, g_ref, land, send_sem, recv_sem):
        del core_ref
        n = pl.program_id(0)
        k = pl.program_id(1)
        x, y, c = _mesh_pos()
        swap = pltpu.make_async_remote_copy(src_ref=gsib_ref, dst_ref=land, send_sem=send_sem, recv_sem=recv_sem,
                                            device_id=(x, y, 1 - c), device_id_type=MESH)

        @pl.when((n == 0) & (k == 0))
        def _():
            swap.start()

        @pl.when(k == 0)
        def _():
            g_ref[...] = jnp.zeros_like(g_ref)

        g_ref[0] += _dot(u_ref[...], dp_ref[...], TN)

        @pl.when((n == 0) & (k == nk - 1))
        def _():
            swap.wait_recv()

        @pl.when(k == nk - 1)
        def _():
            g_ref[0] += land[n]

        @pl.when((n == N_CHIPS - 1) & (k == nk - 1))
        def _():
            swap.wait_send()

    return pl.pallas_call(
        body, name="grad_w_in_own",
        grid_spec=pltpu.PrefetchScalarGridSpec(
            num_scalar_prefetch=1, grid=(N_CHIPS, nk),
            in_specs=[pl.BlockSpec((GRAD_W_IN_TK, D_MODEL), lambda n, k, c: (k, 0)),
                      pl.BlockSpec((GRAD_W_IN_TK, W_BLK), lambda n, k, c: (k, 2 * n + c[0])), ANY],
            out_specs=pl.BlockSpec((1, D_MODEL, W_BLK), lambda n, k, c: (n, 0, 0)),
            scratch_shapes=[pltpu.VMEM((N_CHIPS, D_MODEL, W_BLK), F32), pltpu.SemaphoreType.DMA(()),
                            pltpu.SemaphoreType.DMA(())]),
        out_shape=SDS((N_CHIPS, D_MODEL, W_BLK), F32),
        compiler_params=_cparams(("arbitrary", "arbitrary")),
    )(core, u, dp, g_sib)


def _grad_x(dp, w_all, x, pre_w, dout, g_chip, seq):
    tm = 256
    nm = seq // tm

    def body(dp_ref, w_ref, x_ref, pw_ref, do_ref, gsrc_ref, gx_ref, gpw_ref, recv_ref,
             send_sems, recv_sems, local_sem):
        m = pl.program_id(0)
        ex = _ChipExchange(gsrc_ref, recv_ref, send_sems, recv_sems, local_sem)

        @pl.when(m == 0)
        def _():
            ex.start()
            gpw_ref[...] = jnp.zeros_like(gpw_ref)

        du = _dot(dp_ref[:, 0:W_BLK], w_ref[0], NT)
        for j in range(1, N_DEV):
            du = du + _dot(dp_ref[:, j * W_BLK:(j + 1) * W_BLK], w_ref[j], NT)
        xv = x_ref[...]
        r1 = lax.rsqrt(jnp.mean(xv * xv, axis=-1, keepdims=True) + EPS)
        xn = xv * r1
        gpw_ref[0:1, :] += jnp.sum(du * xn, axis=0, keepdims=True)
        dxn = du * pw_ref[...]
        gx_ref[...] = r1 * (dxn - xn * jnp.mean(dxn * xn, axis=-1, keepdims=True)) + do_ref[...]

        @pl.when(m == nm - 1)
        def _():
            ex.wait()

    row = lambda w: pl.BlockSpec((tm, w), lambda m: (m, 0))
    return pl.pallas_call(
        body, name="grad_x", grid=(nm,),
        in_specs=[row(D_IN), pl.BlockSpec((N_DEV, D_MODEL, W_BLK), lambda m: (0, 0, 0)), row(D_MODEL),
                  pl.BlockSpec((1, D_MODEL), lambda m: (0, 0)), row(D_MODEL), ANY],
        out_specs=[row(D_MODEL), pl.BlockSpec((8, D_MODEL), lambda m: (0, 0)), ANY],
        out_shape=[SDS((seq, D_MODEL), F32), SDS((8, D_MODEL), F32), SDS(g_chip.shape, F32)],
        scratch_shapes=[pltpu.SemaphoreType.DMA((N_CHIPS - 1,)), pltpu.SemaphoreType.DMA((N_CHIPS - 1,)),
                        pltpu.SemaphoreType.DMA(())],
        compiler_params=_cparams(("arbitrary",)),
    )(dp, w_all, x, pre_w, dout, g_chip)


def _local_step(x, tgt, p, u, conv_w, conv_b, wa, wx, ba, bx, lam, lbl, gnorm_w, w_out, post_w):
    seq = x.shape[0]
    h, y_lru = _lru_forward(p, conv_w, conv_b, wa, wx, ba, bx, lam, seq)
    y_hgrn, o, states = _hgrn_forward(p, lbl, gnorm_w, seq)
    dymix, dout, g_w_out, stats = _out_proj(y_lru, y_hgrn, w_out, x, tgt, post_w, seq)
    dp_lru, g_wa, g_wx, ls = _lru_backward(p, h, dymix, conv_w, conv_b, wa, wx, ba, bx, lam, seq)
    g_small = _pack_small(_shard_rows(g_wa, LRU_BLOCKS), _shard_rows(g_wx, LRU_BLOCKS),
                          _shard_rows(ls[0:4].reshape(4, D_MODEL, 1), 4).reshape(N_DEV, 4, 128),
                          _shard_rows(ls[5].reshape(4, LRU_BW, 1), 4).reshape(N_DEV, 4, 32),
                          _shard_rows(ls[6].reshape(4, LRU_BW, 1), 4).reshape(N_DEV, 4, 32))
    dp, hgrn_small, r_out, r_small = _hgrn_backward(
        p, o, states, dymix, lbl, gnorm_w, dp_lru, g_w_out.reshape(N_DEV, D_MIX // N_DEV, D_MODEL), g_small, seq)
    return dict(u=u, dp=dp, dout=dout, r_out=r_out, r_small=r_small,
                lru_small=ls, hgrn_small=hgrn_small, stats=stats)


class _TwoLevelGather:
    def __init__(self, ins, outs, send_sems, recv_sems, local_sems):
        self.ins, self.outs = ins, outs
        self.send_sems, self.recv_sems, self.local_sems = send_sems, recv_sems, local_sems
        x, y, c = _mesh_pos()
        self.c = c
        self.me, self.sibling = (x, y, c), (x, y, 1 - c)
        self.chips = [(1 - x, y), (x, 1 - y), (1 - x, 1 - y)]
        n = len(ins)
        self.mine = [pltpu.make_async_copy(ins[a], self._slot(a, self.me), local_sems.at[a]) for a in range(n)]
        self.first = []
        for a in range(n):
            self.first.append(self._copy(a, 0, self.me, self.sibling, src=ins[a]))
            self.first += [self._copy(a, 1 + j, self.me, (*chip, c), src=ins[a])
                           for j, chip in enumerate(self.chips)]
        self.passed = [self._copy(a, 4 + j, (*chip, c), self.sibling)
                       for j, chip in enumerate(self.chips) for a in range(n)]

    def _slot(self, a, pos):
        return self.outs[a].at[4 * pos[0] + 2 * pos[1] + pos[2]]

    def _copy(self, a, k, block, to, src=None):
        dst = self._slot(a, block)
        return pltpu.make_async_remote_copy(
            src_ref=dst if src is None else src, dst_ref=dst,
            send_sem=self.send_sems.at[a, k], recv_sem=self.recv_sems.at[a, k],
            device_id=to, device_id_type=MESH)

    def start(self):
        for cp in self.mine + self.first:
            cp.start()

    def forward(self):
        n = len(self.ins)
        for j, chip in enumerate(self.chips):
            for a in range(n):
                self._copy(a, 1 + j, (*chip, self.c), self.me).wait_recv()
                self.passed[j * n + a].start()

    def finish(self):
        for a in range(len(self.ins)):
            self._copy(a, 0, self.sibling, self.me).wait_recv()
            for j, chip in enumerate(self.chips):
                self._copy(a, 4 + j, (*chip, 1 - self.c), self.me).wait_recv()
        for cp in self.first + self.passed:
            cp.wait_send()
        for cp in self.mine:
            cp.wait()


W_IN_DIRECT = (1, 2, 4, 6)
W_IN_PASSED = (2, 4, 6)


def _in_proj_gather(x, pre_w, w_in_blk, w_out_blk, small_blk, me, seq):
    tm = min(1024, seq)
    nm = seq // tm
    last = N_DEV - 1

    def body(me_ref, x_ref, pw_ref, wblk_ref, woblk_ref, smblk_ref,
             p_ref, u_ref, wall_ref, woall_ref, small_ref,
             u_all, w_vmem, own_sem, d_send, d_recv, f_send, f_recv, wb_sems, g_send, g_recv, g_local):
        i = pl.program_id(0)
        m = pl.program_id(1)
        idx = me_ref[0]
        x_, y_, c_ = _mesh_pos()
        aux = _TwoLevelGather([woblk_ref, smblk_ref], [woall_ref, small_ref], g_send, g_recv, g_local)

        def peer(k):
            return (1 - x_ if (k >> 2) & 1 else x_, 1 - y_ if (k >> 1) & 1 else y_, 1 - c_ if k & 1 else c_)

        def direct(k):
            f = W_IN_DIRECT.index(k)
            return (pltpu.make_async_remote_copy(src_ref=wblk_ref, dst_ref=w_vmem.at[idx], send_sem=d_send.at[f],
                                                 recv_sem=d_recv.at[f], device_id=peer(k), device_id_type=MESH),
                    pltpu.make_async_remote_copy(src_ref=w_vmem.at[idx ^ k], dst_ref=w_vmem.at[idx ^ k],
                                                 send_sem=d_send.at[f], recv_sem=d_recv.at[f], device_id=peer(k),
                                                 device_id_type=MESH))

        def passed(k):
            f = W_IN_PASSED.index(k)
            return (pltpu.make_async_remote_copy(src_ref=w_vmem.at[idx ^ k], dst_ref=w_vmem.at[idx ^ k],
                                                 send_sem=f_send.at[f], recv_sem=f_recv.at[f], device_id=peer(1),
                                                 device_id_type=MESH),
                    pltpu.make_async_remote_copy(src_ref=w_vmem.at[idx ^ (k + 1)], dst_ref=w_vmem.at[idx ^ (k + 1)],
                                                 send_sem=f_send.at[f], recv_sem=f_recv.at[f], device_id=peer(1),
                                                 device_id_type=MESH))

        def write_back(k):
            return pltpu.make_async_copy(w_vmem.at[idx ^ k], wall_ref.at[idx ^ k], wb_sems.at[k])

        own = pltpu.make_async_copy(wblk_ref, w_vmem.at[idx], own_sem)

        @pl.when((i == 0) & (m == 0))
        def _():
            own.start()
            for k in W_IN_DIRECT:
                direct(k)[0].start()
            aux.start()
            own.wait()
            write_back(0).start()

        for k in range(1, N_DEV):
            @pl.when((i == k) & (m == 0))
            def _(k=k):
                if k in W_IN_DIRECT:
                    direct(k)[1].wait_recv()
                    if k in W_IN_PASSED:
                        passed(k)[0].start()
                else:
                    passed(k - 1)[1].wait_recv()
                write_back(k).start()

        @pl.when((i == N_CHIPS) & (m == 0))
        def _():
            aux.forward()

        rows = pl.ds(pl.multiple_of(m * tm, tm), tm)

        @pl.when(i == 0)
        def _():
            xv = x_ref[...]
            r = lax.rsqrt(jnp.mean(xv * xv, axis=-1, keepdims=True) + EPS)
            ub = (xv * r * pw_ref[...]).astype(BF16)
            u_all[rows, :] = ub
            u_ref[...] = ub

        p_ref[...] = _dot(u_all[rows, :], w_vmem[idx ^ i])

        @pl.when((i == last) & (m == nm - 1))
        def _():
            for k in W_IN_DIRECT:
                direct(k)[0].wait_send()
            for k in W_IN_PASSED:
                passed(k)[0].wait_send()
            for k in range(N_DEV):
                write_back(k).wait()
            aux.finish()

    first_pass = lambda i, m: jnp.where(i == 0, m, nm - 1)
    return pl.pallas_call(
        body, name="in_proj_gather",
        grid_spec=pltpu.PrefetchScalarGridSpec(
            num_scalar_prefetch=1, grid=(N_DEV, nm),
            in_specs=[pl.BlockSpec((tm, D_MODEL), lambda i, m, me: (first_pass(i, m), 0)),
                      pl.BlockSpec((1, D_MODEL), lambda i, m, me: (0, 0)), ANY, ANY, ANY],
            out_specs=[pl.BlockSpec((tm, W_BLK), lambda i, m, me: (m, me[0] ^ i)),
                       pl.BlockSpec((tm, D_MODEL), lambda i, m, me: (first_pass(i, m), 0)), ANY, ANY, ANY],
            scratch_shapes=[pltpu.VMEM((seq, D_MODEL), BF16), pltpu.VMEM((N_DEV, D_MODEL, W_BLK), BF16),
                            pltpu.SemaphoreType.DMA(()),
                            pltpu.SemaphoreType.DMA((len(W_IN_DIRECT),)), pltpu.SemaphoreType.DMA((len(W_IN_DIRECT),)),
                            pltpu.SemaphoreType.DMA((len(W_IN_PASSED),)), pltpu.SemaphoreType.DMA((len(W_IN_PASSED),)),
                            pltpu.SemaphoreType.DMA((N_DEV,)),
                            pltpu.SemaphoreType.DMA((2, 7)), pltpu.SemaphoreType.DMA((2, 7)),
                            pltpu.SemaphoreType.DMA((2,))]),
        out_shape=[SDS((seq, D_IN), F32), SDS((seq, D_MODEL), BF16), SDS((N_DEV, D_MODEL, W_BLK), BF16),
                   SDS((N_DEV,) + w_out_blk.shape, w_out_blk.dtype), SDS((N_DEV,) + small_blk.shape, small_blk.dtype)],
        compiler_params=_cparams(("arbitrary", "arbitrary")),
    )(me, x, pre_w, w_in_blk, w_out_blk, small_blk)


def _exchange_grads(blocks, repl):
    nb = len(blocks)
    n = nb + 1

    def body(*refs):
        ins, outs, sems = refs[:n], refs[n:2 * n], refs[2 * n:]
        exs = [_SlotExchange(ins[a], outs[a], *sems[3 * a:3 * a + 3], blocked=a < nb) for a in range(n)]
        for ex in exs:
            ex.start()
        for ex in exs:
            ex.wait()

    arrs = list(blocks) + [repl]
    shapes = [SDS(b.shape, b.dtype) for b in blocks] + [SDS((N_DEV,) + repl.shape, repl.dtype)]
    return pl.pallas_call(
        body, name="exchange_small", out_shape=shapes,
        in_specs=[ANY] * n, out_specs=[ANY] * n,
        scratch_shapes=EXCHANGE_SEMS * n,
    )(*arrs)


def _pack_rows(picks, name):
    arrs = [p[0] for p in picks]

    def body(*refs):
        out = refs[-1]
        out[...] = jnp.zeros_like(out)
        at = 0
        for ref, (_, row, rows, scale) in zip(refs[:-1], picks):
            out[at:at + rows, :] = ref[row:row + rows, :] * scale
            at += rows

    return pl.pallas_call(body, name=name, out_shape=SDS((8, D_MODEL), F32))(*arrs)


def _adamw(g, w, m, v):
    m2 = ADAM_B1 * m + (1.0 - ADAM_B1) * g
    v2 = ADAM_B2 * v + (1.0 - ADAM_B2) * (g * g)
    m_hat = m2 / (1.0 - ADAM_B1 ** ADAM_STEP)
    v_hat = v2 / (1.0 - ADAM_B2 ** ADAM_STEP)
    delta = -ADAM_LR * (m_hat / (jnp.sqrt(v_hat) + ADAM_EPS) + ADAM_WD * w)
    return delta, m2, v2


def _sum_slots(r_ref):
    g = r_ref[0]
    for s in range(1, r_ref.shape[0]):
        g = g + r_ref[s]
    return g


def _sum_adamw(recv, w, m, v, tr, name):
    rows, cols = w.shape

    def body(r_ref, w_ref, m_ref, v_ref, g_ref, d_ref, m2_ref, v2_ref):
        g = _sum_slots(r_ref)
        g_ref[...] = g
        d_ref[...], m2_ref[...], v2_ref[...] = _adamw(g, w_ref[...], m_ref[...], v_ref[...])

    blk = pl.BlockSpec((tr, cols), lambda i: (i, 0))
    return pl.pallas_call(
        body, name=name, grid=(rows // tr,),
        in_specs=[pl.BlockSpec((recv.shape[0], tr, cols), lambda i: (0, i, 0)), blk, blk, blk],
        out_specs=[blk] * 4, out_shape=[SDS((rows, cols), F32)] * 4,
        compiler_params=_cparams(("parallel",)),
    )(recv, w, m, v)


def _sum_adamw_repl(recv, w, m, v):
    def body(r_ref, w_ref, m_ref, v_ref, g_ref, d_ref, m2_ref, v2_ref, loss_ref):
        g = _sum_slots(r_ref)
        g_ref[...] = g
        d_ref[...], m2_ref[...], v2_ref[...] = _adamw(g, w_ref[...], m_ref[...], v_ref[...])
        total = jnp.sum(g[RP_LOSS:RP_LOSS + 1, :], axis=-1, keepdims=True)
        loss_ref[...] = jnp.broadcast_to(total, loss_ref.shape)

    return pl.pallas_call(
        body, name="adamw_repl",
        out_shape=[SDS((8, D_MODEL), F32)] * 4 + [SDS((8, 128), F32)],
    )(recv, w, m, v)


def _shard_rows(t, lead):
    r = t.shape[1] // N_DEV
    t = t.reshape((lead, N_DEV, r) + t.shape[2:])
    return jnp.moveaxis(t, 1, 0)


def _pad8(t):
    return jnp.pad(t, ((0, 0), (0, 8 - t.shape[1]), (0, 0)))


def _pack_small(wa, wx, cw, b_a, b_x):
    n = wa.shape[0]
    return jnp.concatenate([
        wa.reshape(n, 256, 128), wx.reshape(n, 256, 128), _pad8(cw),
        _pad8(b_a.reshape(n, 1, 128)), _pad8(b_x.reshape(n, 1, 128))], axis=1)


def _unpack_small(t):
    n = t.shape[0]
    return (t[:, SM_WA:SM_WA + 256].reshape(n, 4, 32, 256), t[:, SM_WX:SM_WX + 256].reshape(n, 4, 32, 256),
            t[:, SM_CW:SM_CW + 4], t[:, SM_BA].reshape(n, 4, 32), t[:, SM_BX].reshape(n, 4, 32))


def kernel(x, pre_norm_w, w_in, conv_w, conv_b, lru_w_a, lru_b_a, lru_w_x, lru_b_x, lru_lambda, hgrn_lb_logits, hgrn_gnorm_w, w_out, post_norm_w, loss_target, m_pre_norm_w, m_w_in, m_conv_w, m_conv_b, m_lru_w_a, m_lru_b_a, m_lru_w_x, m_lru_b_x, m_lru_lambda, m_hgrn_lb_logits, m_hgrn_gnorm_w, m_w_out, m_post_norm_w, v_pre_norm_w, v_w_in, v_conv_w, v_conv_b, v_lru_w_a, v_lru_b_a, v_lru_w_x, v_lru_b_x, v_lru_lambda, v_hgrn_lb_logits, v_hgrn_gnorm_w, v_w_out, v_post_norm_w):
    seq = x.shape[1]
    x2 = x.reshape(seq, D_MODEL)
    tgt = loss_target.reshape(seq, D_MODEL)

    small_w = _pack_small(lru_w_a, lru_w_x, conv_w, lru_b_a, lru_b_x)[0]
    me = (4 * lax.axis_index("x") + 2 * lax.axis_index("y") + lax.axis_index("c")).astype(jnp.int32).reshape(1)
    p, u, w_in_all, w_out_all, small_all = _in_proj_gather(
        x2, pre_norm_w, w_in[0].astype(BF16), w_out[0].astype(BF16), small_w, me, seq)
    wa_s, wx_s, cw_s, ba_s, bx_s = _unpack_small(small_all)
    wa = jnp.moveaxis(wa_s, 0, 1).reshape(LRU_BLOCKS, LRU_BW, LRU_BW).astype(BF16)
    wx = jnp.moveaxis(wx_s, 0, 1).reshape(LRU_BLOCKS, LRU_BW, LRU_BW).astype(BF16)
    cw = jnp.moveaxis(cw_s, 0, 1).reshape(4, D_MODEL)
    ba = jnp.moveaxis(ba_s, 0, 1).reshape(1, D_MODEL)
    bx = jnp.moveaxis(bx_s, 0, 1).reshape(1, D_MODEL)

    loc = _local_step(x2, tgt, p, u, cw, conv_b, wa, wx, ba, bx, lru_lambda,
                      hgrn_lb_logits, hgrn_gnorm_w, w_out_all.reshape(D_MIX, D_MODEL), post_norm_w)

    ls, r_out, r_small = loc["lru_small"], loc["r_out"], loc["r_small"]
    core = lax.axis_index("c").astype(jnp.int32).reshape(1)
    g_sib = _grad_w_in_sibling(loc["u"], loc["dp"], core, seq)
    g_chip = _grad_w_in_own(loc["u"], loc["dp"], core, g_sib, seq)
    grad_x, pre_small, r_in = _grad_x(loc["dp"], w_in_all, x2, pre_norm_w, loc["dout"], g_chip, seq)
    g_repl = _pack_rows([(pre_small, 0, 1, 1.0), (ls, 4, 1, 1.0), (ls, 7, 1, 1.0),
                         (loc["hgrn_small"], 0, 1, 1.0), (loc["hgrn_small"], 0, 1, -1.0),
                         (loc["hgrn_small"], 1, 1, 1.0), (loc["stats"], 0, 2, 1.0)], "pack_grads")
    (r_repl,) = _exchange_grads([], g_repl)

    zero = jnp.zeros((1, D_MODEL), F32)
    pack_w = lambda a, b, c_, d, e, f, name: _pack_rows(
        [(a, 0, 1, 1.0), (b, 0, 1, 1.0), (c_, 0, 1, 1.0), (d, 0, 2, 1.0), (e, 0, 1, 1.0), (f, 0, 1, 1.0),
         (zero, 0, 1, 1.0)], name)
    w_repl = pack_w(pre_norm_w, conv_b, lru_lambda, hgrn_lb_logits, hgrn_gnorm_w, post_norm_w, "pack_w")
    m_repl = pack_w(m_pre_norm_w, m_conv_b, m_lru_lambda, m_hgrn_lb_logits, m_hgrn_gnorm_w, m_post_norm_w, "pack_m")
    v_repl = pack_w(v_pre_norm_w, v_conv_b, v_lru_lambda, v_hgrn_lb_logits, v_hgrn_gnorm_w, v_post_norm_w, "pack_v")
    o_repl = _sum_adamw_repl(r_repl, w_repl, m_repl, v_repl)
    loss = o_repl[4][0, 0]

    o_in = _sum_adamw(r_in, w_in[0], m_w_in[0], v_w_in[0], 128, "adamw_w_in")
    o_out = _sum_adamw(r_out, w_out[0], m_w_out[0], v_w_out[0], 64, "adamw_w_out")
    o_small = _sum_adamw(r_small,
                         _pack_small(lru_w_a, lru_w_x, conv_w, lru_b_a, lru_b_x)[0],
                         _pack_small(m_lru_w_a, m_lru_w_x, m_conv_w, m_lru_b_a, m_lru_b_x)[0],
                         _pack_small(v_lru_w_a, v_lru_w_x, v_conv_w, v_lru_b_a, v_lru_b_x)[0],
                         SM_ROWS, "adamw_small")

    outs = [loss, grad_x.reshape(x.shape)]
    for kind in range(4):
        rp = o_repl[kind]
        swa, swx, scw, sba, sbx = _unpack_small(o_small[kind][None])
        outs += [rp[RP_PRE:RP_PRE + 1], o_in[kind][None], scw, rp[RP_CB:RP_CB + 1], swa, sba, swx, sbx,
                 rp[RP_LAM:RP_LAM + 1], rp[RP_LB0:RP_LB1 + 1], rp[RP_GN:RP_GN + 1], o_out[kind][None],
                 rp[RP_POST:RP_POST + 1]]
    return tuple(outs)
```

```python
import functools

import jax
import jax.numpy as jnp
from jax import lax
from jax.experimental import pallas as pl
from jax.experimental.pallas import tpu as pltpu

F32 = jnp.float32
BF16 = jnp.bfloat16
SDS = jax.ShapeDtypeStruct

D_MODEL = 1024
D_IN = 6144
N_DEV = 8
N_CHIPS = 4
W_BLK = D_IN // N_DEV
D_MIX = 2048
LRU_BLOCKS = 4
LRU_BW = 256
LRU_C = 8.0
N_HEADS = 8
HEAD_D = 128
CHUNK = 128
SUB = 32
N_SUB = CHUNK // SUB
EXP_CLAMP = 80.0
EPS = 1e-6

ADAM_LR = 0.001
ADAM_B1 = 0.9
ADAM_B2 = 0.999
ADAM_EPS = 1e-08
ADAM_WD = 0.01
ADAM_STEP = 10

VMEM_LIMIT = 56 * 1024 * 1024

NN = (((1,), (0,)), ((), ()))
NT = (((1,), (1,)), ((), ()))
TN = (((0,), (0,)), ((), ()))

SM_WA = 0
SM_WX = 256
SM_CW = 512
SM_BA = 520
SM_BX = 528
SM_ROWS = 536

RP_PRE, RP_CB, RP_LAM, RP_LB0, RP_LB1, RP_GN, RP_POST, RP_LOSS = range(8)


def _dot(a, b, dims=NN):
    return lax.dot_general(a, b, dims, preferred_element_type=F32)


def _sigmoid(x):
    return 0.5 * jnp.tanh(0.5 * x) + 0.5


def _sigmoid_pos(x):
    return 1.0 / (1.0 + jnp.exp(-x))


def _cparams(sem, vmem=VMEM_LIMIT):
    return pltpu.CompilerParams(dimension_semantics=sem, vmem_limit_bytes=vmem)


def _iota(shape, axis):
    return lax.broadcasted_iota(jnp.int32, shape, axis)


def _softplus_neg(lam):
    z = -lam
    e = jnp.exp(-jnp.abs(z))
    u = 1.0 + e
    log1p_e = jnp.where(u == 1.0, e, jnp.log(u) * (e / (u - 1.0)))
    sp = jnp.maximum(z, 0.0) + log1p_e
    dsp = -jnp.where(z >= 0.0, 1.0 / u, e / u)
    return sp, dsp


def _neg_expm1(x):
    poly = x * (1.0 + x * (1.0 / 2 + x * (1.0 / 6 + x * (1.0 / 24 + x * (1.0 / 120)))))
    return jnp.where(x > -1.0 / 16, -poly, 1.0 - jnp.exp(x))


def _conv_taps(lx, prev8, cw_ref, cb_ref, tile):
    xc = cb_ref[...] + cw_ref[3:4, :] * lx
    for j in (1, 2, 3):
        xc = xc + cw_ref[3 - j:4 - j, :] * pltpu.roll(lx, j, 0)
    row8 = _iota((8, D_MODEL), 0)
    last8 = lx[tile - 8:tile, :]
    fix = jnp.zeros((8, D_MODEL), F32)
    for j in (1, 2, 3):
        wrong = pltpu.roll(last8, j, 0)
        right = pltpu.roll(prev8, j, 0)
        fix = fix + cw_ref[3 - j:4 - j, :] * jnp.where(row8 < j, right - wrong, 0.0)
    return xc, fix


def _lru_gates(xcs, wa, wx, ba, bx, sp):
    xb = xcs.astype(BF16)
    r = _sigmoid_pos(_dot(xb, wa) + ba)
    i = _sigmoid(_dot(xb, wx) + bx)
    la = (-LRU_C * sp) * r
    a = jnp.exp(la)
    one_minus_a2 = _neg_expm1(2.0 * la)
    return r, i, a, one_minus_a2


def _lru_forward(p, conv_w, conv_b, wa, wx, ba, bx, lam, seq):
    tile = min(512, seq // 2)
    nblk = tile // 8

    def body(lx_ref, gt_ref, cw_ref, cb_ref, wa_ref, wx_ref, ba_ref, bx_ref, lam_ref,
             h_ref, y_ref, prev8, hcar, xc_scr, a_scr, u_scr):
        @pl.when(pl.program_id(0) == 0)
        def _():
            prev8[...] = jnp.zeros_like(prev8)
            hcar[...] = jnp.zeros_like(hcar)

        lx = lx_ref[...]
        xc, fix = _conv_taps(lx, prev8[...], cw_ref, cb_ref, tile)
        xc_scr[...] = xc
        xc_scr[0:8, :] = xc_scr[0:8, :] + fix
        prev8[...] = lx_ref[tile - 8:tile, :]
        sp, _ = _softplus_neg(lam_ref[...])
        for n in range(LRU_BLOCKS):
            sl = slice(n * LRU_BW, (n + 1) * LRU_BW)
            xcs = xc_scr[:, sl]
            _, i, a, ne = _lru_gates(xcs, wa_ref[n], wx_ref[n], ba_ref[:, sl], bx_ref[:, sl], sp[:, sl])
            a_scr[:, sl] = a
            u_scr[:, sl] = jnp.sqrt(ne) * (i * xcs)

        row8 = _iota((8, D_MODEL), 0)

        def blk(j, hc):
            off = pl.multiple_of(j * 8, 8)
            a = a_scr[pl.ds(off, 8), :]
            u = u_scr[pl.ds(off, 8), :]
            for k in (1, 2, 4):
                m = row8 >= k
                u = jnp.where(m, u + a * pltpu.roll(u, k, 0), u)
                a = jnp.where(m, a * pltpu.roll(a, k, 0), a)
            h = u + a * hc
            h_ref[pl.ds(off, 8), :] = h
            return jnp.broadcast_to(h[7:8, :], (8, D_MODEL))

        hcar[...] = lax.fori_loop(0, nblk, blk, hcar[...])
        g = gt_ref[...]
        y_ref[...] = (h_ref[...] * (g * _sigmoid(g))).astype(BF16)

    full = lambda shape: pl.BlockSpec(shape, lambda t: (0,) * len(shape))
    return pl.pallas_call(
        body, name="lru_fwd", grid=(seq // tile,),
        in_specs=[pl.BlockSpec((tile, D_MODEL), lambda t: (t, 0)),
                  pl.BlockSpec((tile, D_MODEL), lambda t: (t, 1)),
                  full((4, D_MODEL)), full((1, D_MODEL)),
                  full((LRU_BLOCKS, LRU_BW, LRU_BW)), full((LRU_BLOCKS, LRU_BW, LRU_BW)),
                  full((1, D_MODEL)), full((1, D_MODEL)), full((1, D_MODEL))],
        out_specs=[pl.BlockSpec((tile, D_MODEL), lambda t: (t, 0)),
                   pl.BlockSpec((tile, D_MODEL), lambda t: (t, 0))],
        out_shape=[SDS((seq, D_MODEL), F32), SDS((seq, D_MODEL), BF16)],
        scratch_shapes=[pltpu.VMEM((8, D_MODEL), F32), pltpu.VMEM((8, D_MODEL), F32),
                        pltpu.VMEM((tile, D_MODEL), F32), pltpu.VMEM((tile, D_MODEL), F32),
                        pltpu.VMEM((tile, D_MODEL), F32)],
        compiler_params=_cparams(("arbitrary",)),
    )(p, p, conv_w, conv_b, wa, wx, ba, bx, lam)


def _lru_backward(p, h, dymix, conv_w, conv_b, wa, wx, ba, bx, lam, seq):
    tile = min(256, seq // 2)
    nt = seq // tile
    nblk = tile // 8
    t8 = tile // 8

    def body(lx_ref, lxh_ref, gt_ref, h_ref, hh_ref, dy_ref, cw_ref, cb_ref, wa_ref, wx_ref, ba_ref,
             bx_ref, lam_ref, dp_ref, gwa_ref, gwx_ref, gsm_ref,
             lamcar, anext, dxc8, xc_scr, r_scr, i_scr, a_scr, m_scr, rm_scr, c_scr, l_scr, dxc_scr):
        step = pl.program_id(0)
        first_tile = step == nt - 1

        @pl.when(step == 0)
        def _():
            lamcar[...] = jnp.zeros_like(lamcar)
            anext[...] = jnp.zeros_like(anext)
            dxc8[...] = jnp.zeros_like(dxc8)
            gwa_ref[...] = jnp.zeros_like(gwa_ref)
            gwx_ref[...] = jnp.zeros_like(gwx_ref)
            gsm_ref[...] = jnp.zeros_like(gsm_ref)

        keep = jnp.where(first_tile, 0.0, 1.0)
        lx = lx_ref[...]
        prev8 = lxh_ref[...] * keep
        xc, fix = _conv_taps(lx, prev8, cw_ref, cb_ref, tile)
        xc_scr[...] = xc
        xc_scr[0:8, :] = xc_scr[0:8, :] + fix
        sp, dsp = _softplus_neg(lam_ref[...])
        for n in range(LRU_BLOCKS):
            sl = slice(n * LRU_BW, (n + 1) * LRU_BW)
            r, i, a, ne = _lru_gates(xc_scr[:, sl], wa_ref[n], wx_ref[n], ba_ref[:, sl], bx_ref[:, sl],
                                     sp[:, sl])
            r_scr[:, sl] = r
            i_scr[:, sl] = i
            a_scr[:, sl] = a
            m_scr[:, sl] = jnp.sqrt(ne)
            rm_scr[:, sl] = lax.rsqrt(ne)

        g = gt_ref[...]
        sg = _sigmoid(g)
        dy = dy_ref[...]
        hv = h_ref[...]
        dp_ref[:, D_MODEL:2 * D_MODEL] = (dy * hv * (sg * (1.0 + g * (1.0 - sg)))).astype(BF16)

        rowt = _iota((tile, D_MODEL), 0)
        av = a_scr[...]
        l_scr[...] = dy * (g * sg)
        c_scr[...] = jnp.where(rowt == tile - 1, anext[...][0:1, :], pltpu.roll(av, tile - 1, 0))
        anext[...] = jnp.broadcast_to(av[0:1, :], (8, D_MODEL))
        row8 = _iota((8, D_MODEL), 0)

        def blk(jj, lc):
            off = pl.multiple_of((nblk - 1 - jj) * 8, 8)
            c = c_scr[pl.ds(off, 8), :]
            u = l_scr[pl.ds(off, 8), :]
            for k in (1, 2, 4):
                m = row8 < 8 - k
                u = jnp.where(m, u + c * pltpu.roll(u, 8 - k, 0), u)
                c = jnp.where(m, c * pltpu.roll(c, 8 - k, 0), c)
            lamv = u + c * lc
            l_scr[pl.ds(off, 8), :] = lamv
            return jnp.broadcast_to(lamv[0:1, :], (8, D_MODEL))

        lamcar[...] = lax.fori_loop(0, nblk, blk, lamcar[...])

        hprev = jnp.where(rowt == 0, hh_ref[...][7:8, :] * keep, pltpu.roll(hv, 1, 0))
        for n in range(LRU_BLOCKS):
            sl = slice(n * LRU_BW, (n + 1) * LRU_BW)
            lamv = l_scr[:, sl]
            xcs = xc_scr[:, sl]
            r = r_scr[:, sl]
            i = i_scr[:, sl]
            a = a_scr[:, sl]
            mult = m_scr[:, sl]
            d_la = lamv * hprev[:, sl] * a - (lamv * i * xcs) * (a * a * rm_scr[:, sl])
            d_pr = d_la * (-LRU_C * sp[:, sl]) * r * (1.0 - r)
            d_pi = (lamv * mult * xcs) * i * (1.0 - i)
            gsm_ref[7:8, sl] += jnp.sum(d_la * r, axis=0, keepdims=True) * (-LRU_C) * dsp[:, sl]
            gsm_ref[5:6, sl] += jnp.sum(d_pr, axis=0, keepdims=True)
            gsm_ref[6:7, sl] += jnp.sum(d_pi, axis=0, keepdims=True)
            xb = xcs.astype(BF16)
            prb = d_pr.astype(BF16)
            pib = d_pi.astype(BF16)
            gwa_ref[n] += _dot(xb, prb, TN)
            gwx_ref[n] += _dot(xb, pib, TN)
            dxc_scr[:, sl] = lamv * mult * i + _dot(prb, wa_ref[n], NT) + _dot(pib, wx_ref[n], NT)

        dxc = dxc_scr[...]
        gsm_ref[4:5, :] += jnp.sum(dxc, axis=0, keepdims=True)
        last8 = lx[tile - 8:tile, :]
        first8 = dxc[0:8, :]
        dlx = cw_ref[3:4, :] * dxc
        gsm_ref[3:4, :] += jnp.sum(dxc * lx, axis=0, keepdims=True)
        fix = jnp.zeros((8, D_MODEL), F32)
        for j in (1, 2, 3):
            w = cw_ref[3 - j:4 - j, :]
            dlx = dlx + w * pltpu.roll(dxc, tile - j, 0)
            fix = fix + w * jnp.where(row8 + j >= 8,
                                      pltpu.roll(dxc8[...], 8 - j, 0) - pltpu.roll(first8, 8 - j, 0), 0.0)
            halo = jnp.where(row8 < j, pltpu.roll(prev8, j, 0) - pltpu.roll(last8, j, 0), 0.0)
            gsm_ref[3 - j:4 - j, :] += (jnp.sum(dxc * pltpu.roll(lx, j, 0), axis=0, keepdims=True)
                                        + jnp.sum(first8 * halo, axis=0, keepdims=True))
        dxc8[...] = first8
        dp_ref[:, 0:D_MODEL] = dlx.astype(BF16)
        top = tile - 8
        dp_ref[top:tile, 0:D_MODEL] = (dlx[top:tile, :] + fix).astype(BF16)

    rev = lambda t: (nt - 1 - t, 0)
    halo_idx = lambda t: (jnp.maximum((nt - 1 - t) * t8 - 1, 0), 0)
    full = lambda shape: pl.BlockSpec(shape, lambda t: (0,) * len(shape))
    big = lambda: pltpu.VMEM((tile, D_MODEL), F32)
    return pl.pallas_call(
        body, name="lru_bwd", grid=(nt,),
        in_specs=[pl.BlockSpec((tile, D_MODEL), rev),
                  pl.BlockSpec((8, D_MODEL), halo_idx),
                  pl.BlockSpec((tile, D_MODEL), lambda t: (nt - 1 - t, 1)),
                  pl.BlockSpec((tile, D_MODEL), rev),
                  pl.BlockSpec((8, D_MODEL), halo_idx),
                  pl.BlockSpec((tile, D_MODEL), rev),
                  full((4, D_MODEL)), full((1, D_MODEL)),
                  full((LRU_BLOCKS, LRU_BW, LRU_BW)), full((LRU_BLOCKS, LRU_BW, LRU_BW)),
                  full((1, D_MODEL)), full((1, D_MODEL)), full((1, D_MODEL))],
        out_specs=[pl.BlockSpec((tile, 2 * D_MODEL), rev),
                   full((LRU_BLOCKS, LRU_BW, LRU_BW)), full((LRU_BLOCKS, LRU_BW, LRU_BW)),
                   full((8, D_MODEL))],
        out_shape=[SDS((seq, D_IN), BF16), SDS((LRU_BLOCKS, LRU_BW, LRU_BW), F32),
                   SDS((LRU_BLOCKS, LRU_BW, LRU_BW), F32), SDS((8, D_MODEL), F32)],
        scratch_shapes=[pltpu.VMEM((8, D_MODEL), F32), pltpu.VMEM((8, D_MODEL), F32),
                        pltpu.VMEM((8, D_MODEL), F32)] + [big() for _ in range(9)],
        compiler_params=_cparams(("arbitrary",)),
    )(p, p, p, h, h, dymix, conv_w, conv_b, wa, wx, ba, bx, lam)


def _tri_matmul(tri, g):
    hi = g.astype(BF16)
    lo = (g - hi.astype(F32)).astype(BF16)
    return _dot(tri, lo) + _dot(tri, hi)


def _hgrn_gate_terms(q, fr, lbl):
    lb = _sigmoid_pos(lbl[0:1, :] - lbl[1:2, :])
    sig = _sigmoid(fr)
    f = lb + (1.0 - lb) * sig
    sq = _sigmoid(q)
    return lb, sig, f, sq


def _hgrn_decay(bh):
    zero = jnp.zeros((1, bh.shape[1]), F32)
    rho = [zero] + [bh[s * SUB - 1:s * SUB, :] for s in range(1, N_SUB + 1)]
    start = _sub_rows(rho[0:N_SUB])
    end = _sub_rows(rho[1:N_SUB + 1])
    mid = 0.5 * (start + end)
    blast = rho[N_SUB]
    e_on = jnp.exp(jnp.minimum(bh - start, 0.0))
    e_off = jnp.exp(jnp.minimum(end - bh, 0.0))
    scales = [_sub_rows([jnp.exp(rho[i] - rho[j + 1]) if i > j else zero for i in range(N_SUB)])
              for j in range(N_SUB - 1)]
    return dict(eq0=jnp.exp(jnp.minimum(bh - mid, EXP_CLAMP)), ek0=jnp.exp(jnp.minimum(mid - bh, EXP_CLAMP)),
                e_on=e_on, e_off=e_off, scales=scales,
                eb=e_on * _sub_rows([jnp.exp(r) for r in rho[0:N_SUB]]),
                ekst=e_off * _sub_rows([jnp.exp(blast - r) for r in rho[1:N_SUB + 1]]),
                ebl=jnp.exp(blast))


def _sub_rows(vecs):
    return jnp.concatenate([jnp.broadcast_to(v, (SUB, v.shape[1])) for v in vecs], axis=0)


def _hgrn_operands(qs, k, dec, qt_scr, kt_scr):
    sub = jnp.right_shift(_iota(qs.shape, 0), 5)
    qon = qs * dec["e_on"]
    koff = k * dec["e_off"]
    qt_scr[0] = (qs * dec["eq0"]).astype(BF16)
    kt_scr[0] = (k * dec["ek0"]).astype(BF16)
    for j in range(N_SUB - 1):
        qt_scr[j + 1] = (qon * dec["scales"][j]).astype(BF16)
        kt_scr[j + 1] = jnp.where(sub == j, koff, 0.0).astype(BF16)
    return koff


def _hgrn_head_scores(qt_scr, kt_scr, sl, diag):
    a = jnp.where(diag, _dot(qt_scr[0, :, sl], kt_scr[0, :, sl], NT), 0.0)
    for j in range(1, N_SUB):
        a = a + _dot(qt_scr[j, :, sl], kt_scr[j, :, sl], NT)
    return a


def _hgrn_forward(p, lbl, gw, seq):
    nc = seq // CHUNK
    assert SUB == 32

    def body(q_ref, f_ref, v_ref, hg_ref, lbl_ref, gw_ref, y_ref, o_ref, st_ref,
             s_scr, qt_scr, kt_scr, qin_scr, kst_scr, vb_scr, a_scr):
        @pl.when(pl.program_id(0) == 0)
        def _():
            s_scr[...] = jnp.zeros_like(s_scr)

        r = _iota((CHUNK, CHUNK), 0)
        c = _iota((CHUNK, CHUNK), 1)
        tri = jnp.where(c <= r, 1.0, 0.0).astype(BF16)
        diag = (jnp.right_shift(r, 5) == jnp.right_shift(c, 5)) & (c <= r)
        q = q_ref[...]
        _, _, f, sq = _hgrn_gate_terms(q, f_ref[...], lbl_ref[...])
        qs = q * sq
        k = 1.0 - f
        dec = _hgrn_decay(_tri_matmul(tri, jnp.log(f)))
        _hgrn_operands(qs, k, dec, qt_scr, kt_scr)
        qin_scr[...] = (qs * dec["eb"]).astype(BF16)
        kst_scr[...] = (k * dec["ekst"]).astype(BF16)
        vb_scr[...] = v_ref[...].astype(BF16)
        ebl = dec["ebl"]
        hg = hg_ref[...]
        gate = gw_ref[...] * (hg * _sigmoid(hg))
        heads = [slice(h * HEAD_D, (h + 1) * HEAD_D) for h in range(N_HEADS)]
        stb = []
        for h, sl in enumerate(heads):
            st = s_scr[h]
            st_ref[0, h] = st
            stb.append(st.astype(BF16))
            s_scr[h] = st * ebl[:, sl] + _dot(vb_scr[:, sl], kst_scr[:, sl], TN)
        for h, sl in enumerate(heads):
            a_scr[h] = _hgrn_head_scores(qt_scr, kt_scr, sl, diag).astype(BF16)
        for h, sl in enumerate(heads):
            o = _dot(a_scr[h], vb_scr[:, sl]) + _dot(qin_scr[:, sl], stb[h], NT)
            o_ref[:, sl] = o
            rs = lax.rsqrt(jnp.mean(o * o, axis=-1, keepdims=True) + EPS)
            y_ref[:, sl] = ((o * rs) * gate[:, sl]).astype(BF16)

    col = lambda j: pl.BlockSpec((CHUNK, D_MODEL), lambda c: (c, j))
    par = lambda rows: pl.BlockSpec((rows, D_MODEL), lambda c: (0, 0))
    return pl.pallas_call(
        body, name="hgrn_fwd", grid=(nc,),
        in_specs=[col(2), col(3), col(4), col(5), par(2), par(1)],
        out_specs=[col(0), col(0),
                   pl.BlockSpec((1, N_HEADS, HEAD_D, HEAD_D), lambda c: (c, 0, 0, 0))],
        out_shape=[SDS((seq, D_MODEL), BF16), SDS((seq, D_MODEL), F32),
                   SDS((nc, N_HEADS, HEAD_D, HEAD_D), F32)],
        scratch_shapes=[pltpu.VMEM((N_HEADS, HEAD_D, HEAD_D), F32),
                        pltpu.VMEM((N_SUB, CHUNK, D_MODEL), BF16), pltpu.VMEM((N_SUB, CHUNK, D_MODEL), BF16)]
                       + [pltpu.VMEM((CHUNK, D_MODEL), BF16)] * 3 + [pltpu.VMEM((N_HEADS, CHUNK, CHUNK), BF16)],
        compiler_params=_cparams(("arbitrary",)),
    )(p, p, p, p, lbl, gw)


def _hgrn_backward(p, o, states, dymix, lbl, gw, dp_full, g_w_out, g_small, seq):
    nc = seq // CHUNK

    def body(q_ref, f_ref, v_ref, hg_ref, o_ref, st_ref, dy_ref, lbl_ref, gw_ref, dpin_ref, go_ref, gs_ref,
             dpo_ref, gsm_ref, ro_ref, rs_ref, ds_scr, dp_buf, dp_sems,
             qt_scr, kt_scr, qin_scr, kst_scr, vb_scr, dob_scr, g_scr, h_scr, dqi_scr, dks_scr, sd_scr,
             a_scr, da_scr, da0_scr, *sems):
        del dpin_ref
        step = pl.program_id(0)
        slot = step % 2
        exs = [_SlotExchange(go_ref, ro_ref, *sems[0:3], blocked=True),
               _SlotExchange(gs_ref, rs_ref, *sems[3:6], blocked=True)]

        @pl.when(step == 0)
        def _():
            for ex in exs:
                ex.start()

        def out_copy(s, chunk):
            rows = pl.ds(pl.multiple_of(chunk * CHUNK, CHUNK), CHUNK)
            return pltpu.make_async_copy(dp_buf.at[s], dpo_ref.at[rows, pl.ds(2 * D_MODEL, 4 * D_MODEL)],
                                         dp_sems.at[s])

        @pl.when(step == 0)
        def _():
            ds_scr[...] = jnp.zeros_like(ds_scr)
            gsm_ref[...] = jnp.zeros_like(gsm_ref)

        @pl.when(step >= 2)
        def _():
            out_copy(slot, nc + 1 - step).wait()

        dp_ref = dp_buf.at[slot]

        r = _iota((CHUNK, CHUNK), 0)
        c = _iota((CHUNK, CHUNK), 1)
        tri = jnp.where(c <= r, 1.0, 0.0).astype(BF16)
        triu = jnp.where(c >= r, 1.0, 0.0).astype(BF16)
        diag = (jnp.right_shift(r, 5) == jnp.right_shift(c, 5)) & (c <= r)
        row = _iota((CHUNK, D_MODEL), 0)
        sub = jnp.right_shift(row, 5)

        q = q_ref[...]
        lb, sig, f, sq = _hgrn_gate_terms(q, f_ref[...], lbl_ref[...])
        qs = q * sq
        k = 1.0 - f
        dec = _hgrn_decay(_tri_matmul(tri, jnp.log(f)))
        eb, ekst, ebl = dec["eb"], dec["ekst"], dec["ebl"]
        koff = _hgrn_operands(qs, k, dec, qt_scr, kt_scr)
        qin_scr[...] = (qs * eb).astype(BF16)
        kst_scr[...] = (k * ekst).astype(BF16)
        vb_scr[...] = v_ref[...].astype(BF16)
        hg = hg_ref[...]
        sh = _sigmoid(hg)
        dy = dy_ref[...]
        gwv = gw_ref[...]
        d_onw = dy * (hg * sh)
        d_on = d_onw * gwv
        d_gate = dy * gwv * (sh * (1.0 + hg * (1.0 - sh)))

        heads = [slice(h * HEAD_D, (h + 1) * HEAD_D) for h in range(N_HEADS)]
        for h, sl in enumerate(heads):
            o = o_ref[:, sl]
            rs = lax.rsqrt(jnp.mean(o * o, axis=-1, keepdims=True) + EPS)
            on = o * rs
            dp_ref[:, 3 * D_MODEL + h * HEAD_D:3 * D_MODEL + (h + 1) * HEAD_D] = (d_gate[:, sl] * on).astype(BF16)
            gsm_ref[1:2, sl] += jnp.sum(d_onw[:, sl] * on, axis=0, keepdims=True)
            d_onh = d_on[:, sl]
            dob_scr[:, sl] = (rs * (d_onh - on * jnp.mean(d_onh * on, axis=-1, keepdims=True))).astype(BF16)
        for h, sl in enumerate(heads):
            a_scr[h] = _hgrn_head_scores(qt_scr, kt_scr, sl, diag).astype(BF16)
            da = _dot(dob_scr[:, sl], vb_scr[:, sl], NT)
            da_scr[h] = da.astype(BF16)
            da0_scr[h] = jnp.where(diag, da, 0.0).astype(BF16)
        for h, sl in enumerate(heads):
            st = st_ref[0, h]
            dst = ds_scr[h]
            dstb = dst.astype(BF16)
            dp_ref[:, 2 * D_MODEL + h * HEAD_D:2 * D_MODEL + (h + 1) * HEAD_D] = (
                _dot(a_scr[h], dob_scr[:, sl], TN) + _dot(kst_scr[:, sl], dstb, NT)).astype(BF16)
            dqi_scr[:, sl] = _dot(dob_scr[:, sl], st.astype(BF16))
            dks_scr[:, sl] = _dot(vb_scr[:, sl], dstb)
            sd_scr[0:1, sl] = jnp.sum(st * dst, axis=0, keepdims=True)
            ds_scr[h] = dst * ebl[:, sl] + _dot(dob_scr[:, sl], qin_scr[:, sl], TN)
        for h, sl in enumerate(heads):
            g_scr[0, :, sl] = _dot(da0_scr[h], kt_scr[0, :, sl])
            h_scr[0, :, sl] = _dot(da0_scr[h], qt_scr[0, :, sl], TN)
            for j in range(1, N_SUB):
                g_scr[j, :, sl] = _dot(da_scr[h], kt_scr[j, :, sl])
                h_scr[j, :, sl] = _dot(da_scr[h], qt_scr[j, :, sl], TN)

        g0 = g_scr[0]
        h0 = h_scr[0]
        dq_inter = eb * dqi_scr[...]
        d_kst = ekst * dks_scr[...]
        db = qs * dq_inter - k * d_kst + qt_scr[0].astype(F32) * g0 - kt_scr[0].astype(F32) * h0
        gq = jnp.zeros((CHUNK, D_MODEL), F32)
        hsel = jnp.zeros((CHUNK, D_MODEL), F32)
        for j in range(N_SUB - 1):
            gj = g_scr[j + 1]
            gq = gq + dec["scales"][j] * gj
            db = db + qt_scr[j + 1].astype(F32) * gj
            hsel = jnp.where(sub == j, h_scr[j + 1], hsel)
        db = db - koff.astype(BF16).astype(F32) * hsel
        d_q = dec["eq0"] * g0 + dec["e_on"] * gq + dq_inter
        d_k = dec["ek0"] * h0 + dec["e_off"] * hsel + d_kst
        db_last = jnp.sum(k * d_kst, axis=0, keepdims=True) + ebl * sd_scr[0:1, :]
        db = db + jnp.where(row == CHUNK - 1, db_last, 0.0)
        dg = _tri_matmul(triu, db)
        d_f = dg / f - d_k
        dp_ref[:, D_MODEL:2 * D_MODEL] = (d_f * (1.0 - lb) * sig * (1.0 - sig)).astype(BF16)
        gsm_ref[0:1, :] += jnp.sum(d_f * (1.0 - sig), axis=0, keepdims=True) * (lb * (1.0 - lb))
        dp_ref[:, 0:D_MODEL] = (d_q * (sq * (1.0 + q * (1.0 - sq)))).astype(BF16)

        out_copy(slot, nc - 1 - step).start()

        @pl.when(step == nc - 1)
        def _():
            out_copy(1 - slot, 1).wait()
            out_copy(slot, 0).wait()
            for ex in exs:
                ex.wait()

    rc = lambda c: nc - 1 - c
    col = lambda j: pl.BlockSpec((CHUNK, D_MODEL), lambda c: (rc(c), j))
    par = lambda rows: pl.BlockSpec((rows, D_MODEL), lambda c: (0, 0))
    return pl.pallas_call(
        body, name="hgrn_bwd", grid=(nc,),
        in_specs=[col(2), col(3), col(4), col(5), col(0),
                  pl.BlockSpec((1, N_HEADS, HEAD_D, HEAD_D), lambda c: (rc(c), 0, 0, 0)),
                  col(1), par(2), par(1), ANY, ANY, ANY],
        out_specs=[ANY, par(8), ANY, ANY],
        out_shape=[SDS((seq, D_IN), BF16), SDS((8, D_MODEL), F32), SDS(g_w_out.shape, F32),
                   SDS(g_small.shape, F32)],
        input_output_aliases={9: 0},
        scratch_shapes=[pltpu.VMEM((N_HEADS, HEAD_D, HEAD_D), F32),
                        pltpu.VMEM((2, CHUNK, 4 * D_MODEL), BF16), pltpu.SemaphoreType.DMA((2,)),
                        pltpu.VMEM((N_SUB, CHUNK, D_MODEL), BF16), pltpu.VMEM((N_SUB, CHUNK, D_MODEL), BF16)]
                       + [pltpu.VMEM((CHUNK, D_MODEL), BF16)] * 4
                       + [pltpu.VMEM((N_SUB, CHUNK, D_MODEL), F32)] * 2 + [pltpu.VMEM((CHUNK, D_MODEL), F32)] * 2
                       + [pltpu.VMEM((8, D_MODEL), F32)] + [pltpu.VMEM((N_HEADS, CHUNK, CHUNK), BF16)] * 3
                       + EXCHANGE_SEMS * 2,
        compiler_params=_cparams(("arbitrary",)),
    )(p, p, p, p, o, states, dymix, lbl, gw, dp_full, g_w_out, g_small)


def _out_proj(yl, yh, wo, x, tgt, post_w, seq):
    tm = 512

    def body(yl_ref, yh_ref, wo_ref, x_ref, tg_ref, pw_ref, dymix_ref, dout_ref, gwo_ref, st_ref):
        @pl.when(pl.program_id(0) == 0)
        def _():
            gwo_ref[...] = jnp.zeros_like(gwo_ref)
            st_ref[...] = jnp.zeros_like(st_ref)

        ylv = yl_ref[...]
        yhv = yh_ref[...]
        y = _dot(ylv, wo_ref[0:D_MODEL, :]) + _dot(yhv, wo_ref[D_MODEL:D_MIX, :])
        r2 = lax.rsqrt(jnp.mean(y * y, axis=-1, keepdims=True) + EPS)
        yn = y * r2
        pw = pw_ref[...]
        e = (x_ref[...] + yn * pw) - tg_ref[...]
        st_ref[1:2, :] += jnp.sum(e * e, axis=0, keepdims=True) * (0.5 / D_MODEL)
        dout = e * (1.0 / D_MODEL)
        dout_ref[...] = dout
        st_ref[0:1, :] += jnp.sum(dout * yn, axis=0, keepdims=True)
        dyn = dout * pw
        dy = r2 * (dyn - yn * jnp.mean(dyn * yn, axis=-1, keepdims=True))
        dyb = dy.astype(BF16)
        dymix_ref[...] = _dot(dyb, wo_ref[...], NT)
        gwo_ref[0:D_MODEL, :] += _dot(ylv, dyb, TN)
        gwo_ref[D_MODEL:D_MIX, :] += _dot(yhv, dyb, TN)

    row = lambda w: pl.BlockSpec((tm, w), lambda m: (m, 0))
    full = lambda shape: pl.BlockSpec(shape, lambda m: (0,) * len(shape))
    once = lambda shape: pl.BlockSpec(shape, lambda m: (0,) * len(shape), pipeline_mode=pl.Buffered(1))
    return pl.pallas_call(
        body, name="out_proj", grid=(seq // tm,),
        in_specs=[row(D_MODEL), row(D_MODEL), once((D_MIX, D_MODEL)), row(D_MODEL), row(D_MODEL),
                  full((1, D_MODEL))],
        out_specs=[row(D_MIX), row(D_MODEL), once((D_MIX, D_MODEL)), full((8, D_MODEL))],
        out_shape=[SDS((seq, D_MIX), F32), SDS((seq, D_MODEL), F32), SDS((D_MIX, D_MODEL), F32),
                   SDS((8, D_MODEL), F32)],
        compiler_params=_cparams(("arbitrary",)),
    )(yl, yh, wo, x, tgt, post_w)


MESH = pl.DeviceIdType.MESH
ANY = pl.BlockSpec(memory_space=pl.ANY)
EXCHANGE_SEMS = [pltpu.SemaphoreType.DMA((N_DEV - 1,)), pltpu.SemaphoreType.DMA((N_DEV - 1,)),
                 pltpu.SemaphoreType.DMA(())]


def _mesh_pos():
    return lax.axis_index("x"), lax.axis_index("y"), lax.axis_index("c")


class _SlotExchange:
    def __init__(self, src_ref, dst_ref, send_sems, recv_sems, local_sem, blocked):
        x, y, c = _mesh_pos()
        me = 4 * x + 2 * y + c
        src = (lambda dest: src_ref.at[dest]) if blocked else (lambda dest: src_ref)
        self.local = pltpu.make_async_copy(src(me), dst_ref.at[me], local_sem)
        self.sends, self.recvs = [], []
        for k in range(1, N_DEV):
            px = 1 - x if (k >> 2) & 1 else x
            py = 1 - y if (k >> 1) & 1 else y
            pc = 1 - c if k & 1 else c
            peer = 4 * px + 2 * py + pc
            sems = dict(send_sem=send_sems.at[k - 1], recv_sem=recv_sems.at[k - 1],
                        device_id=(px, py, pc), device_id_type=MESH)
            self.sends.append(pltpu.make_async_remote_copy(src_ref=src(peer), dst_ref=dst_ref.at[me], **sems))
            self.recvs.append(pltpu.make_async_remote_copy(src_ref=dst_ref.at[peer], dst_ref=dst_ref.at[peer], **sems))

    def start(self):
        self.local.start()
        for cp in self.sends:
            cp.start()

    def wait(self):
        for cp in self.recvs:
            cp.wait_recv()
        for cp in self.sends:
            cp.wait_send()
        self.local.wait()


class _ChipExchange:
    def __init__(self, src_ref, dst_ref, send_sems, recv_sems, local_sem):
        x, y, c = _mesh_pos()
        chip = 2 * x + y
        self.local = pltpu.make_async_copy(src_ref.at[chip], dst_ref.at[chip], local_sem)
        self.sends, self.recvs = [], []
        for k in range(1, N_CHIPS):
            px = 1 - x if (k >> 1) & 1 else x
            py = 1 - y if k & 1 else y
            peer = 2 * px + py
            sems = dict(send_sem=send_sems.at[k - 1], recv_sem=recv_sems.at[k - 1],
                        device_id=(px, py, c), device_id_type=MESH)
            self.sends.append(pltpu.make_async_remote_copy(src_ref=src_ref.at[peer], dst_ref=dst_ref.at[chip], **sems))
            self.recvs.append(pltpu.make_async_remote_copy(src_ref=dst_ref.at[peer], dst_ref=dst_ref.at[peer], **sems))

    def start(self):
        self.local.start()
        for cp in self.sends:
            cp.start()

    def wait(self):
        for cp in self.recvs:
            cp.wait_recv()
        for cp in self.sends:
            cp.wait_send()
        self.local.wait()


GRAD_W_IN_TK = 2048


def _grad_w_in_sibling(u, dp, core, seq):
    tk = min(GRAD_W_IN_TK, seq)
    nk = seq // tk

    def body(core_ref, u_ref, dp_ref, g_ref):
        del core_ref

        @pl.when(pl.program_id(1) == 0)
        def _():
            g_ref[...] = jnp.zeros_like(g_ref)

        g_ref[0] += _dot(u_ref[...], dp_ref[...], TN)

    return pl.pallas_call(
        body, name="grad_w_in_sibling",
        grid_spec=pltpu.PrefetchScalarGridSpec(
            num_scalar_prefetch=1, grid=(N_CHIPS, nk),
            in_specs=[pl.BlockSpec((tk, D_MODEL), lambda n, k, c: (k, 0)),
                      pl.BlockSpec((tk, W_BLK), lambda n, k, c: (k, 2 * n + 1 - c[0]))],
            out_specs=pl.BlockSpec((1, D_MODEL, W_BLK), lambda n, k, c: (n, 0, 0))),
        out_shape=SDS((N_CHIPS, D_MODEL, W_BLK), F32),
        compiler_params=_cparams(("parallel", "arbitrary")),
    )(core, u, dp)


def _grad_w_in_own(u, dp, core, g_sib, seq):
    tk = min(GRAD_W_IN_TK, seq)
    nk = seq // tk

    def body(core_ref, u_ref, dp_ref, gsib_ref, g_ref, land, send_sem, recv_sem):
        del core_ref
        n = pl.program_id(0)
        k = pl.program_id(1)
        x, y, c = _mesh_pos()
        swap = pltpu.make_async_remote_copy(src_ref=gsib_ref, dst_ref=land, send_sem=send_sem, recv_sem=recv_sem,
                                            device_id=(x, y, 1 - c), device_id_type=MESH)

        @pl.when((n == 0) & (k == 0))
        def _():
            swap.start()

        @pl.when(k == 0)
        def _():
            g_ref[...] = jnp.zeros_like(g_ref)

        g_ref[0] += _dot(u_ref[...], dp_ref[...], TN)

        @pl.when((n == 0) & (k == nk - 1))
        def _():
            swap.wait_recv()

        @pl.when(k == nk - 1)
        def _():
            g_ref[0] += land[n]

        @pl.when((n == N_CHIPS - 1) & (k == nk - 1))
        def _():
            swap.wait_send()

    return pl.pallas_call(
        body, name="grad_w_in_own",
        grid_spec=pltpu.PrefetchScalarGridSpec(
            num_scalar_prefetch=1, grid=(N_CHIPS, nk),
            in_specs=[pl.BlockSpec((tk, D_MODEL), lambda n, k, c: (k, 0)),
                      pl.BlockSpec((tk, W_BLK), lambda n, k, c: (k, 2 * n + c[0])), ANY],
            out_specs=pl.BlockSpec((1, D_MODEL, W_BLK), lambda n, k, c: (n, 0, 0)),
            scratch_shapes=[pltpu.VMEM((N_CHIPS, D_MODEL, W_BLK), F32), pltpu.SemaphoreType.DMA(()),
                            pltpu.SemaphoreType.DMA(())]),
        out_shape=SDS((N_CHIPS, D_MODEL, W_BLK), F32),
        compiler_params=_cparams(("arbitrary", "arbitrary")),
    )(core, u, dp, g_sib)


def _grad_x(dp, w_all, x, pre_w, dout, g_chip, seq):
    tm = 256
    nm = seq // tm

    def body(dp_ref, w_ref, x_ref, pw_ref, do_ref, gsrc_ref, gx_ref, gpw_ref, recv_ref,
             send_sems, recv_sems, local_sem):
        m = pl.program_id(0)
        ex = _ChipExchange(gsrc_ref, recv_ref, send_sems, recv_sems, local_sem)

        @pl.when(m == 0)
        def _():
            ex.start()
            gpw_ref[...] = jnp.zeros_like(gpw_ref)

        du = _dot(dp_ref[:, 0:W_BLK], w_ref[0], NT)
        for j in range(1, N_DEV):
            du = du + _dot(dp_ref[:, j * W_BLK:(j + 1) * W_BLK], w_ref[j], NT)
        xv = x_ref[...]
        r1 = lax.rsqrt(jnp.mean(xv * xv, axis=-1, keepdims=True) + EPS)
        xn = xv * r1
        gpw_ref[0:1, :] += jnp.sum(du * xn, axis=0, keepdims=True)
        dxn = du * pw_ref[...]
        gx_ref[...] = r1 * (dxn - xn * jnp.mean(dxn * xn, axis=-1, keepdims=True)) + do_ref[...]

        @pl.when(m == nm - 1)
        def _():
            ex.wait()

    row = lambda w: pl.BlockSpec((tm, w), lambda m: (m, 0))
    return pl.pallas_call(
        body, name="grad_x", grid=(nm,),
        in_specs=[row(D_IN), pl.BlockSpec((N_DEV, D_MODEL, W_BLK), lambda m: (0, 0, 0)), row(D_MODEL),
                  pl.BlockSpec((1, D_MODEL), lambda m: (0, 0)), row(D_MODEL), ANY],
        out_specs=[row(D_MODEL), pl.BlockSpec((8, D_MODEL), lambda m: (0, 0)), ANY],
        out_shape=[SDS((seq, D_MODEL), F32), SDS((8, D_MODEL), F32), SDS(g_chip.shape, F32)],
        scratch_shapes=[pltpu.SemaphoreType.DMA((N_CHIPS - 1,)), pltpu.SemaphoreType.DMA((N_CHIPS - 1,)),
                        pltpu.SemaphoreType.DMA(())],
        compiler_params=_cparams(("arbitrary",)),
    )(dp, w_all, x, pre_w, dout, g_chip)


def _local_step(x, tgt, p, u, conv_w, conv_b, wa, wx, ba, bx, lam, lbl, gnorm_w, w_out, post_w):
    seq = x.shape[0]
    h, y_lru = _lru_forward(p, conv_w, conv_b, wa, wx, ba, bx, lam, seq)
    y_hgrn, o, states = _hgrn_forward(p, lbl, gnorm_w, seq)
    dymix, dout, g_w_out, stats = _out_proj(y_lru, y_hgrn, w_out, x, tgt, post_w, seq)
    dp_lru, g_wa, g_wx, ls = _lru_backward(p, h, dymix, conv_w, conv_b, wa, wx, ba, bx, lam, seq)
    g_small = _pack_small(_shard_rows(g_wa, LRU_BLOCKS), _shard_rows(g_wx, LRU_BLOCKS),
                          _shard_rows(ls[0:4].reshape(4, D_MODEL, 1), 4).reshape(N_DEV, 4, 128),
                          _shard_rows(ls[5].reshape(4, LRU_BW, 1), 4).reshape(N_DEV, 4, 32),
                          _shard_rows(ls[6].reshape(4, LRU_BW, 1), 4).reshape(N_DEV, 4, 32))
    dp, hgrn_small, r_out, r_small = _hgrn_backward(
        p, o, states, dymix, lbl, gnorm_w, dp_lru, g_w_out.reshape(N_DEV, D_MIX // N_DEV, D_MODEL), g_small, seq)
    return dict(u=u, dp=dp, dout=dout, r_out=r_out, r_small=r_small,
                lru_small=ls, hgrn_small=hgrn_small, stats=stats)


class _TwoLevelGather:
    def __init__(self, ins, outs, send_sems, recv_sems, local_sems):
        self.ins, self.outs = ins, outs
        self.send_sems, self.recv_sems, self.local_sems = send_sems, recv_sems, local_sems
        x, y, c = _mesh_pos()
        self.c = c
        self.me, self.sibling = (x, y, c), (x, y, 1 - c)
        self.chips = [(1 - x, y), (x, 1 - y), (1 - x, 1 - y)]
        n = len(ins)
        self.mine = [pltpu.make_async_copy(ins[a], self._slot(a, self.me), local_sems.at[a]) for a in range(n)]
        self.first = []
        for a in range(n):
            self.first.append(self._copy(a, 0, self.me, self.sibling, src=ins[a]))
            self.first += [self._copy(a, 1 + j, self.me, (*chip, c), src=ins[a])
                           for j, chip in enumerate(self.chips)]
        self.passed = [self._copy(a, 4 + j, (*chip, c), self.sibling)
                       for j, chip in enumerate(self.chips) for a in range(n)]

    def _slot(self, a, pos):
        return self.outs[a].at[4 * pos[0] + 2 * pos[1] + pos[2]]

    def _copy(self, a, k, block, to, src=None):
        dst = self._slot(a, block)
        return pltpu.make_async_remote_copy(
            src_ref=dst if src is None else src, dst_ref=dst,
            send_sem=self.send_sems.at[a, k], recv_sem=self.recv_sems.at[a, k],
            device_id=to, device_id_type=MESH)

    def start(self):
        for cp in self.mine + self.first:
            cp.start()

    def forward(self):
        n = len(self.ins)
        for j, chip in enumerate(self.chips):
            for a in range(n):
                self._copy(a, 1 + j, (*chip, self.c), self.me).wait_recv()
                self.passed[j * n + a].start()

    def finish(self):
        for a in range(len(self.ins)):
            self._copy(a, 0, self.sibling, self.me).wait_recv()
            for j, chip in enumerate(self.chips):
                self._copy(a, 4 + j, (*chip, 1 - self.c), self.me).wait_recv()
        for cp in self.first + self.passed:
            cp.wait_send()
        for cp in self.mine:
            cp.wait()


W_IN_DIRECT = (1, 2, 4, 6)
W_IN_PASSED = (2, 4, 6)


def _in_proj_gather(x, pre_w, w_in_blk, w_out_blk, small_blk, me, seq):
    tm = min(1024, seq)
    nm = seq // tm
    last = N_DEV - 1

    def body(me_ref, x_ref, pw_ref, wblk_ref, woblk_ref, smblk_ref,
             p_ref, u_ref, wall_ref, woall_ref, small_ref,
             u_all, w_vmem, own_sem, d_send, d_recv, f_send, f_recv, wb_sems, g_send, g_recv, g_local):
        i = pl.program_id(0)
        m = pl.program_id(1)
        idx = me_ref[0]
        x_, y_, c_ = _mesh_pos()
        aux = _TwoLevelGather([woblk_ref, smblk_ref], [woall_ref, small_ref], g_send, g_recv, g_local)

        def peer(k):
            return (1 - x_ if (k >> 2) & 1 else x_, 1 - y_ if (k >> 1) & 1 else y_, 1 - c_ if k & 1 else c_)

        def direct(k):
            f = W_IN_DIRECT.index(k)
            return (pltpu.make_async_remote_copy(src_ref=wblk_ref, dst_ref=w_vmem.at[idx], send_sem=d_send.at[f],
                                                 recv_sem=d_recv.at[f], device_id=peer(k), device_id_type=MESH),
                    pltpu.make_async_remote_copy(src_ref=w_vmem.at[idx ^ k], dst_ref=w_vmem.at[idx ^ k],
                                                 send_sem=d_send.at[f], recv_sem=d_recv.at[f], device_id=peer(k),
                                                 device_id_type=MESH))

        def passed(k):
            f = W_IN_PASSED.index(k)
            return (pltpu.make_async_remote_copy(src_ref=w_vmem.at[idx ^ k], dst_ref=w_vmem.at[idx ^ k],
                                                 send_sem=f_send.at[f], recv_sem=f_recv.at[f], device_id=peer(1),
                                                 device_id_type=MESH),
                    pltpu.make_async_remote_copy(src_ref=w_vmem.at[idx ^ (k + 1)], dst_ref=w_vmem.at[idx ^ (k + 1)],
                                                 send_sem=f_send.at[f], recv_sem=f_recv.at[f], device_id=peer(1),
                                                 device_id_type=MESH))

        def write_back(k):
            return pltpu.make_async_copy(w_vmem.at[idx ^ k], wall_ref.at[idx ^ k], wb_sems.at[k])

        own = pltpu.make_async_copy(wblk_ref, w_vmem.at[idx], own_sem)

        @pl.when((i == 0) & (m == 0))
        def _():
            own.start()
            for k in W_IN_DIRECT:
                direct(k)[0].start()
            aux.start()
            own.wait()
            write_back(0).start()

        for k in range(1, N_DEV):
            @pl.when((i == k) & (m == 0))
            def _(k=k):
                if k in W_IN_DIRECT:
                    direct(k)[1].wait_recv()
                    if k in W_IN_PASSED:
                        passed(k)[0].start()
                else:
                    passed(k - 1)[1].wait_recv()
                write_back(k).start()

        @pl.when((i == N_CHIPS) & (m == 0))
        def _():
            aux.forward()

        rows = pl.ds(pl.multiple_of(m * tm, tm), tm)

        @pl.when(i == 0)
        def _():
            xv = x_ref[...]
            r = lax.rsqrt(jnp.mean(xv * xv, axis=-1, keepdims=True) + EPS)
            ub = (xv * r * pw_ref[...]).astype(BF16)
            u_all[rows, :] = ub
            u_ref[...] = ub

        p_ref[...] = _dot(u_all[rows, :], w_vmem[idx ^ i])

        @pl.when((i == last) & (m == nm - 1))
        def _():
            for k in W_IN_DIRECT:
                direct(k)[0].wait_send()
            for k in W_IN_PASSED:
                passed(k)[0].wait_send()
            for k in range(N_DEV):
                write_back(k).wait()
            aux.finish()

    first_pass = lambda i, m: jnp.where(i == 0, m, nm - 1)
    return pl.pallas_call(
        body, name="in_proj_gather",
        grid_spec=pltpu.PrefetchScalarGridSpec(
            num_scalar_prefetch=1, grid=(N_DEV, nm),
            in_specs=[pl.BlockSpec((tm, D_MODEL), lambda i, m, me: (first_pass(i, m), 0)),
                      pl.BlockSpec((1, D_MODEL), lambda i, m, me: (0, 0)), ANY, ANY, ANY],
            out_specs=[pl.BlockSpec((tm, W_BLK), lambda i, m, me: (m, me[0] ^ i)),
                       pl.BlockSpec((tm, D_MODEL), lambda i, m, me: (first_pass(i, m), 0)), ANY, ANY, ANY],
            scratch_shapes=[pltpu.VMEM((seq, D_MODEL), BF16), pltpu.VMEM((N_DEV, D_MODEL, W_BLK), BF16),
                            pltpu.SemaphoreType.DMA(()),
                            pltpu.SemaphoreType.DMA((len(W_IN_DIRECT),)), pltpu.SemaphoreType.DMA((len(W_IN_DIRECT),)),
                            pltpu.SemaphoreType.DMA((len(W_IN_PASSED),)), pltpu.SemaphoreType.DMA((len(W_IN_PASSED),)),
                            pltpu.SemaphoreType.DMA((N_DEV,)),
                            pltpu.SemaphoreType.DMA((2, 7)), pltpu.SemaphoreType.DMA((2, 7)),
                            pltpu.SemaphoreType.DMA((2,))]),
        out_shape=[SDS((seq, D_IN), F32), SDS((seq, D_MODEL), BF16), SDS((N_DEV, D_MODEL, W_BLK), BF16),
                   SDS((N_DEV,) + w_out_blk.shape, w_out_blk.dtype), SDS((N_DEV,) + small_blk.shape, small_blk.dtype)],
        compiler_params=_cparams(("arbitrary", "arbitrary")),
    )(me, x, pre_w, w_in_blk, w_out_blk, small_blk)


def _exchange_grads(blocks, repl):
    nb = len(blocks)
    n = nb + 1

    def body(*refs):
        ins, outs, sems = refs[:n], refs[n:2 * n], refs[2 * n:]
        exs = [_SlotExchange(ins[a], outs[a], *sems[3 * a:3 * a + 3], blocked=a < nb) for a in range(n)]
        for ex in exs:
            ex.start()
        for ex in exs:
            ex.wait()

    arrs = list(blocks) + [repl]
    shapes = [SDS(b.shape, b.dtype) for b in blocks] + [SDS((N_DEV,) + repl.shape, repl.dtype)]
    return pl.pallas_call(
        body, name="exchange_small", out_shape=shapes,
        in_specs=[ANY] * n, out_specs=[ANY] * n,
        scratch_shapes=EXCHANGE_SEMS * n,
    )(*arrs)


def _pack_rows(picks, name):
    arrs = [p[0] for p in picks]

    def body(*refs):
        out = refs[-1]
        out[...] = jnp.zeros_like(out)
        at = 0
        for ref, (_, row, rows, scale) in zip(refs[:-1], picks):
            out[at:at + rows, :] = ref[row:row + rows, :] * scale
            at += rows

    return pl.pallas_call(body, name=name, out_shape=SDS((8, D_MODEL), F32))(*arrs)


def _adamw(g, w, m, v):
    m2 = ADAM_B1 * m + (1.0 - ADAM_B1) * g
    v2 = ADAM_B2 * v + (1.0 - ADAM_B2) * (g * g)
    m_hat = m2 / (1.0 - ADAM_B1 ** ADAM_STEP)
    v_hat = v2 / (1.0 - ADAM_B2 ** ADAM_STEP)
    delta = -ADAM_LR * (m_hat / (jnp.sqrt(v_hat) + ADAM_EPS) + ADAM_WD * w)
    return delta, m2, v2


def _sum_slots(r_ref):
    g = r_ref[0]
    for s in range(1, r_ref.shape[0]):
        g = g + r_ref[s]
    return g


def _sum_adamw(recv, w, m, v, tr, name):
    rows, cols = w.shape

    def body(r_ref, w_ref, m_ref, v_ref, g_ref, d_ref, m2_ref, v2_ref):
        g = _sum_slots(r_ref)
        g_ref[...] = g
        d_ref[...], m2_ref[...], v2_ref[...] = _adamw(g, w_ref[...], m_ref[...], v_ref[...])

    blk = pl.BlockSpec((tr, cols), lambda i: (i, 0))
    return pl.pallas_call(
        body, name=name, grid=(rows // tr,),
        in_specs=[pl.BlockSpec((recv.shape[0], tr, cols), lambda i: (0, i, 0)), blk, blk, blk],
        out_specs=[blk] * 4, out_shape=[SDS((rows, cols), F32)] * 4,
        compiler_params=_cparams(("parallel",)),
    )(recv, w, m, v)


def _sum_adamw_repl(recv, w, m, v):
    def body(r_ref, w_ref, m_ref, v_ref, g_ref, d_ref, m2_ref, v2_ref, loss_ref):
        g = _sum_slots(r_ref)
        g_ref[...] = g
        d_ref[...], m2_ref[...], v2_ref[...] = _adamw(g, w_ref[...], m_ref[...], v_ref[...])
        total = jnp.sum(g[RP_LOSS:RP_LOSS + 1, :], axis=-1, keepdims=True)
        loss_ref[...] = jnp.broadcast_to(total, loss_ref.shape)

    return pl.pallas_call(
        body, name="adamw_repl",
        out_shape=[SDS((8, D_MODEL), F32)] * 4 + [SDS((8, 128), F32)],
    )(recv, w, m, v)


def _shard_rows(t, lead):
    r = t.shape[1] // N_DEV
    t = t.reshape((lead, N_DEV, r) + t.shape[2:])
    return jnp.moveaxis(t, 1, 0)


def _pad8(t):
    return jnp.pad(t, ((0, 0), (0, 8 - t.shape[1]), (0, 0)))


def _pack_small(wa, wx, cw, b_a, b_x):
    n = wa.shape[0]
    return jnp.concatenate([
        wa.reshape(n, 256, 128), wx.reshape(n, 256, 128), _pad8(cw),
        _pad8(b_a.reshape(n, 1, 128)), _pad8(b_x.reshape(n, 1, 128))], axis=1)


def _unpack_small(t):
    n = t.shape[0]
    return (t[:, SM_WA:SM_WA + 256].reshape(n, 4, 32, 256), t[:, SM_WX:SM_WX + 256].reshape(n, 4, 32, 256),
            t[:, SM_CW:SM_CW + 4], t[:, SM_BA].reshape(n, 4, 32), t[:, SM_BX].reshape(n, 4, 32))


def kernel(x, pre_norm_w, w_in, conv_w, conv_b, lru_w_a, lru_b_a, lru_w_x, lru_b_x, lru_lambda, hgrn_lb_logits, hgrn_gnorm_w, w_out, post_norm_w, loss_target, m_pre_norm_w, m_w_in, m_conv_w, m_conv_b, m_lru_w_a, m_lru_b_a, m_lru_w_x, m_lru_b_x, m_lru_lambda, m_hgrn_lb_logits, m_hgrn_gnorm_w, m_w_out, m_post_norm_w, v_pre_norm_w, v_w_in, v_conv_w, v_conv_b, v_lru_w_a, v_lru_b_a, v_lru_w_x, v_lru_b_x, v_lru_lambda, v_hgrn_lb_logits, v_hgrn_gnorm_w, v_w_out, v_post_norm_w):
    seq = x.shape[1]
    x2 = x.reshape(seq, D_MODEL)
    tgt = loss_target.reshape(seq, D_MODEL)

    small_w = _pack_small(lru_w_a, lru_w_x, conv_w, lru_b_a, lru_b_x)[0]
    me = (4 * lax.axis_index("x") + 2 * lax.axis_index("y") + lax.axis_index("c")).astype(jnp.int32).reshape(1)
    p, u, w_in_all, w_out_all, small_all = _in_proj_gather(
        x2, pre_norm_w, w_in[0].astype(BF16), w_out[0].astype(BF16), small_w, me, seq)
    wa_s, wx_s, cw_s, ba_s, bx_s = _unpack_small(small_all)
    wa = jnp.moveaxis(wa_s, 0, 1).reshape(LRU_BLOCKS, LRU_BW, LRU_BW).astype(BF16)
    wx = jnp.moveaxis(wx_s, 0, 1).reshape(LRU_BLOCKS, LRU_BW, LRU_BW).astype(BF16)
    cw = jnp.moveaxis(cw_s, 0, 1).reshape(4, D_MODEL)
    ba = jnp.moveaxis(ba_s, 0, 1).reshape(1, D_MODEL)
    bx = jnp.moveaxis(bx_s, 0, 1).reshape(1, D_MODEL)

    loc = _local_step(x2, tgt, p, u, cw, conv_b, wa, wx, ba, bx, lru_lambda,
                      hgrn_lb_logits, hgrn_gnorm_w, w_out_all.reshape(D_MIX, D_MODEL), post_norm_w)

    ls, r_out, r_small = loc["lru_small"], loc["r_out"], loc["r_small"]
    core = lax.axis_index("c").astype(jnp.int32).reshape(1)
    g_sib = _grad_w_in_sibling(loc["u"], loc["dp"], core, seq)
    g_chip = _grad_w_in_own(loc["u"], loc["dp"], core, g_sib, seq)
    grad_x, pre_small, r_in = _grad_x(loc["dp"], w_in_all, x2, pre_norm_w, loc["dout"], g_chip, seq)
    g_repl = _pack_rows([(pre_small, 0, 1, 1.0), (ls, 4, 1, 1.0), (ls, 7, 1, 1.0),
                         (loc["hgrn_small"], 0, 1, 1.0), (loc["hgrn_small"], 0, 1, -1.0),
                         (loc["hgrn_small"], 1, 1, 1.0), (loc["stats"], 0, 2, 1.0)], "pack_grads")
    (r_repl,) = _exchange_grads([], g_repl)

    zero = jnp.zeros((1, D_MODEL), F32)
    pack_w = lambda a, b, c_, d, e, f, name: _pack_rows(
        [(a, 0, 1, 1.0), (b, 0, 1, 1.0), (c_, 0, 1, 1.0), (d, 0, 2, 1.0), (e, 0, 1, 1.0), (f, 0, 1, 1.0),
         (zero, 0, 1, 1.0)], name)
    w_repl = pack_w(pre_norm_w, conv_b, lru_lambda, hgrn_lb_logits, hgrn_gnorm_w, post_norm_w, "pack_w")
    m_repl = pack_w(m_pre_norm_w, m_conv_b, m_lru_lambda, m_hgrn_lb_logits, m_hgrn_gnorm_w, m_post_norm_w, "pack_m")
    v_repl = pack_w(v_pre_norm_w, v_conv_b, v_lru_lambda, v_hgrn_lb_logits, v_hgrn_gnorm_w, v_post_norm_w, "pack_v")
    o_repl = _sum_adamw_repl(r_repl, w_repl, m_repl, v_repl)
    loss = o_repl[4][0, 0]

    o_in = _sum_adamw(r_in, w_in[0], m_w_in[0], v_w_in[0], 128, "adamw_w_in")
    o_out = _sum_adamw(r_out, w_out[0], m_w_out[0], v_w_out[0], 64, "adamw_w_out")
    o_small = _sum_adamw(r_small,
                         _pack_small(lru_w_a, lru_w_x, conv_w, lru_b_a, lru_b_x)[0],
                         _pack_small(m_lru_w_a, m_lru_w_x, m_conv_w, m_lru_b_a, m_lru_b_x)[0],
                         _pack_small(v_lru_w_a, v_lru_w_x, v_conv_w, v_lru_b_a, v_lru_b_x)[0],
                         SM_ROWS, "adamw_small")

    outs = [loss, grad_x.reshape(x.shape)]
    for kind in range(4):
        rp = o_repl[kind]
        swa, swx, scw, sba, sbx = _unpack_small(o_small[kind][None])
        outs += [rp[RP_PRE:RP_PRE + 1], o_in[kind][None], scw, rp[RP_CB:RP_CB + 1], swa, sba, swx, sbx,
                 rp[RP_LAM:RP_LAM + 1], rp[RP_LB0:RP_LB1 + 1], rp[RP_GN:RP_GN + 1], o_out[kind][None],
                 rp[RP_POST:RP_POST + 1]]
    return tuple(outs)
```

```python
import functools

import jax
import jax.numpy as jnp
from jax import lax
from jax.experimental import pallas as pl
from jax.experimental.pallas import tpu as pltpu

F32 = jnp.float32
BF16 = jnp.bfloat16
SDS = jax.ShapeDtypeStruct

D_MODEL = 1024
D_IN = 6144
N_DEV = 8
N_CHIPS = 4
W_BLK = D_IN // N_DEV
D_MIX = 2048
LRU_BLOCKS = 4
LRU_BW = 256
LRU_C = 8.0
N_HEADS = 8
HEAD_D = 128
CHUNK = 128
SUB = 32
N_SUB = CHUNK // SUB
EXP_CLAMP = 80.0
EPS = 1e-6

ADAM_LR = 0.001
ADAM_B1 = 0.9
ADAM_B2 = 0.999
ADAM_EPS = 1e-08
ADAM_WD = 0.01
ADAM_STEP = 10

VMEM_LIMIT = 56 * 1024 * 1024

NN = (((1,), (0,)), ((), ()))
NT = (((1,), (1,)), ((), ()))
TN = (((0,), (0,)), ((), ()))

SM_WA = 0
SM_WX = 256
SM_CW = 512
SM_BA = 520
SM_BX = 528
SM_ROWS = 536

RP_PRE, RP_CB, RP_LAM, RP_LB0, RP_LB1, RP_GN, RP_POST, RP_LOSS = range(8)


def _dot(a, b, dims=NN):
    return lax.dot_general(a, b, dims, preferred_element_type=F32)


def _sigmoid(x):
    return 0.5 * jnp.tanh(0.5 * x) + 0.5


def _sigmoid_pos(x):
    return 1.0 / (1.0 + jnp.exp(-x))


def _cparams(sem, vmem=VMEM_LIMIT):
    return pltpu.CompilerParams(dimension_semantics=sem, vmem_limit_bytes=vmem)


def _iota(shape, axis):
    return lax.broadcasted_iota(jnp.int32, shape, axis)


def _softplus_neg(lam):
    z = -lam
    e = jnp.exp(-jnp.abs(z))
    u = 1.0 + e
    log1p_e = jnp.where(u == 1.0, e, jnp.log(u) * (e / (u - 1.0)))
    sp = jnp.maximum(z, 0.0) + log1p_e
    dsp = -jnp.where(z >= 0.0, 1.0 / u, e / u)
    return sp, dsp


def _neg_expm1(x):
    poly = x * (1.0 + x * (1.0 / 2 + x * (1.0 / 6 + x * (1.0 / 24 + x * (1.0 / 120)))))
    return jnp.where(x > -1.0 / 16, -poly, 1.0 - jnp.exp(x))


def _conv_taps(lx, prev8, cw_ref, cb_ref, tile):
    xc = cb_ref[...] + cw_ref[3:4, :] * lx
    for j in (1, 2, 3):
        xc = xc + cw_ref[3 - j:4 - j, :] * pltpu.roll(lx, j, 0)
    row8 = _iota((8, D_MODEL), 0)
    last8 = lx[tile - 8:tile, :]
    fix = jnp.zeros((8, D_MODEL), F32)
    for j in (1, 2, 3):
        wrong = pltpu.roll(last8, j, 0)
        right = pltpu.roll(prev8, j, 0)
        fix = fix + cw_ref[3 - j:4 - j, :] * jnp.where(row8 < j, right - wrong, 0.0)
    return xc, fix


def _lru_gates(xcs, wa, wx, ba, bx, sp):
    xb = xcs.astype(BF16)
    r = _sigmoid_pos(_dot(xb, wa) + ba)
    i = _sigmoid(_dot(xb, wx) + bx)
    la = (-LRU_C * sp) * r
    a = jnp.exp(la)
    one_minus_a2 = _neg_expm1(2.0 * la)
    return r, i, a, one_minus_a2


def _lru_forward(p, conv_w, conv_b, wa, wx, ba, bx, lam, seq):
    tile = min(512, seq // 2)
    nblk = tile // 8

    def body(lx_ref, gt_ref, cw_ref, cb_ref, wa_ref, wx_ref, ba_ref, bx_ref, lam_ref,
             h_ref, y_ref, ext, hcar, xc_scr, a_scr, u_scr):
        @pl.when(pl.program_id(0) == 0)
        def _():
            ext[0:8, :] = jnp.zeros((8, D_MODEL), F32)
            hcar[...] = jnp.zeros_like(hcar)

        lx = lx_ref[...]
        ext[8:8 + tile, :] = lx
        xc = cb_ref[...] + cw_ref[3:4, :] * lx
        for j in (1, 2, 3):
            xc = xc + cw_ref[3 - j:4 - j, :] * ext[8 - j:8 - j + tile, :]
        xc_scr[...] = xc
        ext[0:8, :] = lx_ref[tile - 8:tile, :]
        sp, _ = _softplus_neg(lam_ref[...])
        for n in range(LRU_BLOCKS):
            sl = slice(n * LRU_BW, (n + 1) * LRU_BW)
            xcs = xc_scr[:, sl]
            _, i, a, ne = _lru_gates(xcs, wa_ref[n], wx_ref[n], ba_ref[:, sl], bx_ref[:, sl], sp[:, sl])
            a_scr[:, sl] = a
            u_scr[:, sl] = jnp.sqrt(ne) * (i * xcs)

        row8 = _iota((8, D_MODEL), 0)

        def blk(j, hc):
            off = pl.multiple_of(j * 8, 8)
            a = a_scr[pl.ds(off, 8), :]
            u = u_scr[pl.ds(off, 8), :]
            for k in (1, 2, 4):
                m = row8 >= k
                u = jnp.where(m, u + a * pltpu.roll(u, k, 0), u)
                a = jnp.where(m, a * pltpu.roll(a, k, 0), a)
            h = u + a * hc
            h_ref[pl.ds(off, 8), :] = h
            return jnp.broadcast_to(h[7:8, :], (8, D_MODEL))

        hcar[...] = lax.fori_loop(0, nblk, blk, hcar[...])
        g = gt_ref[...]
        y_ref[...] = (h_ref[...] * (g * _sigmoid(g))).astype(BF16)

    full = lambda shape: pl.BlockSpec(shape, lambda t: (0,) * len(shape))
    return pl.pallas_call(
        body, name="lru_fwd", grid=(seq // tile,),
        in_specs=[pl.BlockSpec((tile, D_MODEL), lambda t: (t, 0)),
                  pl.BlockSpec((tile, D_MODEL), lambda t: (t, 1)),
                  full((4, D_MODEL)), full((1, D_MODEL)),
                  full((LRU_BLOCKS, LRU_BW, LRU_BW)), full((LRU_BLOCKS, LRU_BW, LRU_BW)),
                  full((1, D_MODEL)), full((1, D_MODEL)), full((1, D_MODEL))],
        out_specs=[pl.BlockSpec((tile, D_MODEL), lambda t: (t, 0)),
                   pl.BlockSpec((tile, D_MODEL), lambda t: (t, 0))],
        out_shape=[SDS((seq, D_MODEL), F32), SDS((seq, D_MODEL), BF16)],
        scratch_shapes=[pltpu.VMEM((tile + 8, D_MODEL), F32), pltpu.VMEM((8, D_MODEL), F32),
                        pltpu.VMEM((tile, D_MODEL), F32), pltpu.VMEM((tile, D_MODEL), F32),
                        pltpu.VMEM((tile, D_MODEL), F32)],
        compiler_params=_cparams(("arbitrary",)),
    )(p, p, conv_w, conv_b, wa, wx, ba, bx, lam)


def _lru_backward(p, h, dymix, conv_w, conv_b, wa, wx, ba, bx, lam, seq):
    tile = min(256, seq // 2)
    nt = seq // tile
    nblk = tile // 8
    t8 = tile // 8

    def body(lx_ref, lxh_ref, gt_ref, h_ref, hh_ref, dy_ref, cw_ref, cb_ref, wa_ref, wx_ref, ba_ref,
             bx_ref, lam_ref, dp_ref, gwa_ref, gwx_ref, gsm_ref,
             lamcar, anext, dxc8, xc_scr, r_scr, i_scr, a_scr, m_scr, rm_scr, c_scr, l_scr, dxc_scr):
        step = pl.program_id(0)
        first_tile = step == nt - 1

        @pl.when(step == 0)
        def _():
            lamcar[...] = jnp.zeros_like(lamcar)
            anext[...] = jnp.zeros_like(anext)
            dxc8[...] = jnp.zeros_like(dxc8)
            gwa_ref[...] = jnp.zeros_like(gwa_ref)
            gwx_ref[...] = jnp.zeros_like(gwx_ref)
            gsm_ref[...] = jnp.zeros_like(gsm_ref)

        keep = jnp.where(first_tile, 0.0, 1.0)
        lx = lx_ref[...]
        prev8 = lxh_ref[...] * keep
        xc, fix = _conv_taps(lx, prev8, cw_ref, cb_ref, tile)
        xc_scr[...] = xc
        xc_scr[0:8, :] = xc_scr[0:8, :] + fix
        sp, dsp = _softplus_neg(lam_ref[...])
        for n in range(LRU_BLOCKS):
            sl = slice(n * LRU_BW, (n + 1) * LRU_BW)
            r, i, a, ne = _lru_gates(xc_scr[:, sl], wa_ref[n], wx_ref[n], ba_ref[:, sl], bx_ref[:, sl],
                                     sp[:, sl])
            r_scr[:, sl] = r
            i_scr[:, sl] = i
            a_scr[:, sl] = a
            m_scr[:, sl] = jnp.sqrt(ne)
            rm_scr[:, sl] = lax.rsqrt(ne)

        g = gt_ref[...]
        sg = _sigmoid(g)
        dy = dy_ref[...]
        hv = h_ref[...]
        dp_ref[:, D_MODEL:2 * D_MODEL] = (dy * hv * (sg * (1.0 + g * (1.0 - sg)))).astype(BF16)

        rowt = _iota((tile, D_MODEL), 0)
        av = a_scr[...]
        l_scr[...] = dy * (g * sg)
        c_scr[...] = jnp.where(rowt == tile - 1, anext[...][0:1, :], pltpu.roll(av, tile - 1, 0))
        anext[...] = jnp.broadcast_to(av[0:1, :], (8, D_MODEL))
        row8 = _iota((8, D_MODEL), 0)

        def blk(jj, lc):
            off = pl.multiple_of((nblk - 1 - jj) * 8, 8)
            c = c_scr[pl.ds(off, 8), :]
            u = l_scr[pl.ds(off, 8), :]
            for k in (1, 2, 4):
                m = row8 < 8 - k
                u = jnp.where(m, u + c * pltpu.roll(u, 8 - k, 0), u)
                c = jnp.where(m, c * pltpu.roll(c, 8 - k, 0), c)
            lamv = u + c * lc
            l_scr[pl.ds(off, 8), :] = lamv
            return jnp.broadcast_to(lamv[0:1, :], (8, D_MODEL))

        lamcar[...] = lax.fori_loop(0, nblk, blk, lamcar[...])

        hprev = jnp.where(rowt == 0, hh_ref[...][7:8, :] * keep, pltpu.roll(hv, 1, 0))
        for n in range(LRU_BLOCKS):
            sl = slice(n * LRU_BW, (n + 1) * LRU_BW)
            lamv = l_scr[:, sl]
            xcs = xc_scr[:, sl]
            r = r_scr[:, sl]
            i = i_scr[:, sl]
            a = a_scr[:, sl]
            mult = m_scr[:, sl]
            d_la = lamv * hprev[:, sl] * a - (lamv * i * xcs) * (a * a * rm_scr[:, sl])
            d_pr = d_la * (-LRU_C * sp[:, sl]) * r * (1.0 - r)
            d_pi = (lamv * mult * xcs) * i * (1.0 - i)
            gsm_ref[7:8, sl] += jnp.sum(d_la * r, axis=0, keepdims=True) * (-LRU_C) * dsp[:, sl]
            gsm_ref[5:6, sl] += jnp.sum(d_pr, axis=0, keepdims=True)
            gsm_ref[6:7, sl] += jnp.sum(d_pi, axis=0, keepdims=True)
            xb = xcs.astype(BF16)
            prb = d_pr.astype(BF16)
            pib = d_pi.astype(BF16)
            gwa_ref[n] += _dot(xb, prb, TN)
            gwx_ref[n] += _dot(xb, pib, TN)
            dxc_scr[:, sl] = lamv * mult * i + _dot(prb, wa_ref[n], NT) + _dot(pib, wx_ref[n], NT)

        dxc = dxc_scr[...]
        gsm_ref[4:5, :] += jnp.sum(dxc, axis=0, keepdims=True)
        last8 = lx[tile - 8:tile, :]
        first8 = dxc[0:8, :]
        dlx = cw_ref[3:4, :] * dxc
        gsm_ref[3:4, :] += jnp.sum(dxc * lx, axis=0, keepdims=True)
        fix = jnp.zeros((8, D_MODEL), F32)
        for j in (1, 2, 3):
            w = cw_ref[3 - j:4 - j, :]
            dlx = dlx + w * pltpu.roll(dxc, tile - j, 0)
            fix = fix + w * jnp.where(row8 + j >= 8,
                                      pltpu.roll(dxc8[...], 8 - j, 0) - pltpu.roll(first8, 8 - j, 0), 0.0)
            halo = jnp.where(row8 < j, pltpu.roll(prev8, j, 0) - pltpu.roll(last8, j, 0), 0.0)
            gsm_ref[3 - j:4 - j, :] += (jnp.sum(dxc * pltpu.roll(lx, j, 0), axis=0, keepdims=True)
                                        + jnp.sum(first8 * halo, axis=0, keepdims=True))
        dxc8[...] = first8
        dp_ref[:, 0:D_MODEL] = dlx.astype(BF16)
        top = tile - 8
        dp_ref[top:tile, 0:D_MODEL] = (dlx[top:tile, :] + fix).astype(BF16)

    rev = lambda t: (nt - 1 - t, 0)
    halo_idx = lambda t: (jnp.maximum((nt - 1 - t) * t8 - 1, 0), 0)
    full = lambda shape: pl.BlockSpec(shape, lambda t: (0,) * len(shape))
    big = lambda: pltpu.VMEM((tile, D_MODEL), F32)
    return pl.pallas_call(
        body, name="lru_bwd", grid=(nt,),
        in_specs=[pl.BlockSpec((tile, D_MODEL), rev),
                  pl.BlockSpec((8, D_MODEL), halo_idx),
                  pl.BlockSpec((tile, D_MODEL), lambda t: (nt - 1 - t, 1)),
                  pl.BlockSpec((tile, D_MODEL), rev),
                  pl.BlockSpec((8, D_MODEL), halo_idx),
                  pl.BlockSpec((tile, D_MODEL), rev),
                  full((4, D_MODEL)), full((1, D_MODEL)),
                  full((LRU_BLOCKS, LRU_BW, LRU_BW)), full((LRU_BLOCKS, LRU_BW, LRU_BW)),
                  full((1, D_MODEL)), full((1, D_MODEL)), full((1, D_MODEL))],
        out_specs=[pl.BlockSpec((tile, 2 * D_MODEL), rev),
                   full((LRU_BLOCKS, LRU_BW, LRU_BW)), full((LRU_BLOCKS, LRU_BW, LRU_BW)),
                   full((8, D_MODEL))],
        out_shape=[SDS((seq, D_IN), BF16), SDS((LRU_BLOCKS, LRU_BW, LRU_BW), F32),
                   SDS((LRU_BLOCKS, LRU_BW, LRU_BW), F32), SDS((8, D_MODEL), F32)],
        scratch_shapes=[pltpu.VMEM((8, D_MODEL), F32), pltpu.VMEM((8, D_MODEL), F32),
                        pltpu.VMEM((8, D_MODEL), F32)] + [big() for _ in range(9)],
        compiler_params=_cparams(("arbitrary",)),
    )(p, p, p, h, h, dymix, conv_w, conv_b, wa, wx, ba, bx, lam)


def _tri_matmul(tri, g):
    hi = g.astype(BF16)
    lo = (g - hi.astype(F32)).astype(BF16)
    return _dot(tri, lo) + _dot(tri, hi)


def _hgrn_gate_terms(q, fr, lbl):
    lb = _sigmoid_pos(lbl[0:1, :] - lbl[1:2, :])
    sig = _sigmoid(fr)
    f = lb + (1.0 - lb) * sig
    sq = _sigmoid(q)
    return lb, sig, f, sq


def _hgrn_decay(bh):
    zero = jnp.zeros((1, bh.shape[1]), F32)
    rho = [zero] + [bh[s * SUB - 1:s * SUB, :] for s in range(1, N_SUB + 1)]
    start = _sub_rows(rho[0:N_SUB])
    end = _sub_rows(rho[1:N_SUB + 1])
    mid = 0.5 * (start + end)
    blast = rho[N_SUB]
    e_on = jnp.exp(jnp.minimum(bh - start, 0.0))
    e_off = jnp.exp(jnp.minimum(end - bh, 0.0))
    scales = [_sub_rows([jnp.exp(rho[i] - rho[j + 1]) if i > j else zero for i in range(N_SUB)])
              for j in range(N_SUB - 1)]
    return dict(eq0=jnp.exp(jnp.minimum(bh - mid, EXP_CLAMP)), ek0=jnp.exp(jnp.minimum(mid - bh, EXP_CLAMP)),
                e_on=e_on, e_off=e_off, scales=scales,
                eb=e_on * _sub_rows([jnp.exp(r) for r in rho[0:N_SUB]]),
                ekst=e_off * _sub_rows([jnp.exp(blast - r) for r in rho[1:N_SUB + 1]]),
                ebl=jnp.exp(blast))


def _sub_rows(vecs):
    return jnp.concatenate([jnp.broadcast_to(v, (SUB, v.shape[1])) for v in vecs], axis=0)


def _hgrn_operands(qs, k, dec, qt_scr, kt_scr):
    sub = jnp.right_shift(_iota(qs.shape, 0), 5)
    qon = qs * dec["e_on"]
    koff = k * dec["e_off"]
    qt_scr[0] = (qs * dec["eq0"]).astype(BF16)
    kt_scr[0] = (k * dec["ek0"]).astype(BF16)
    for j in range(N_SUB - 1):
        qt_scr[j + 1] = (qon * dec["scales"][j]).astype(BF16)
        kt_scr[j + 1] = jnp.where(sub == j, koff, 0.0).astype(BF16)
    return koff


def _hgrn_head_scores(qt_scr, kt_scr, sl, diag):
    a = jnp.where(diag, _dot(qt_scr[0, :, sl], kt_scr[0, :, sl], NT), 0.0)
    for j in range(1, N_SUB):
        a = a + _dot(qt_scr[j, :, sl], kt_scr[j, :, sl], NT)
    return a


def _hgrn_forward(p, lbl, gw, seq):
    nc = seq // CHUNK
    assert SUB == 32

    def body(q_ref, f_ref, v_ref, hg_ref, lbl_ref, gw_ref, y_ref, o_ref, st_ref,
             s_scr, qt_scr, kt_scr, qin_scr, kst_scr, vb_scr, a_scr):
        @pl.when(pl.program_id(0) == 0)
        def _():
            s_scr[...] = jnp.zeros_like(s_scr)

        r = _iota((CHUNK, CHUNK), 0)
        c = _iota((CHUNK, CHUNK), 1)
        tri = jnp.where(c <= r, 1.0, 0.0).astype(BF16)
        diag = (jnp.right_shift(r, 5) == jnp.right_shift(c, 5)) & (c <= r)
        q = q_ref[...]
        _, _, f, sq = _hgrn_gate_terms(q, f_ref[...], lbl_ref[...])
        qs = q * sq
        k = 1.0 - f
        dec = _hgrn_decay(_tri_matmul(tri, jnp.log(f)))
        _hgrn_operands(qs, k, dec, qt_scr, kt_scr)
        qin_scr[...] = (qs * dec["eb"]).astype(BF16)
        kst_scr[...] = (k * dec["ekst"]).astype(BF16)
        vb_scr[...] = v_ref[...].astype(BF16)
        ebl = dec["ebl"]
        hg = hg_ref[...]
        gate = gw_ref[...] * (hg * _sigmoid(hg))
        heads = [slice(h * HEAD_D, (h + 1) * HEAD_D) for h in range(N_HEADS)]
        stb = []
        for h, sl in enumerate(heads):
            st = s_scr[h]
            st_ref[0, h] = st
            stb.append(st.astype(BF16))
            s_scr[h] = st * ebl[:, sl] + _dot(vb_scr[:, sl], kst_scr[:, sl], TN)
        for h, sl in enumerate(heads):
            a_scr[h] = _hgrn_head_scores(qt_scr, kt_scr, sl, diag).astype(BF16)
        for h, sl in enumerate(heads):
            o = _dot(a_scr[h], vb_scr[:, sl]) + _dot(qin_scr[:, sl], stb[h], NT)
            o_ref[:, sl] = o
            rs = lax.rsqrt(jnp.mean(o * o, axis=-1, keepdims=True) + EPS)
            y_ref[:, sl] = ((o * rs) * gate[:, sl]).astype(BF16)

    col = lambda j: pl.BlockSpec((CHUNK, D_MODEL), lambda c: (c, j))
    par = lambda rows: pl.BlockSpec((rows, D_MODEL), lambda c: (0, 0))
    return pl.pallas_call(
        body, name="hgrn_fwd", grid=(nc,),
        in_specs=[col(2), col(3), col(4), col(5), par(2), par(1)],
        out_specs=[col(0), col(0),
                   pl.BlockSpec((1, N_HEADS, HEAD_D, HEAD_D), lambda c: (c, 0, 0, 0))],
        out_shape=[SDS((seq, D_MODEL), BF16), SDS((seq, D_MODEL), F32),
                   SDS((nc, N_HEADS, HEAD_D, HEAD_D), F32)],
        scratch_shapes=[pltpu.VMEM((N_HEADS, HEAD_D, HEAD_D), F32),
                        pltpu.VMEM((N_SUB, CHUNK, D_MODEL), BF16), pltpu.VMEM((N_SUB, CHUNK, D_MODEL), BF16)]
                       + [pltpu.VMEM((CHUNK, D_MODEL), BF16)] * 3 + [pltpu.VMEM((N_HEADS, CHUNK, CHUNK), BF16)],
        compiler_params=_cparams(("arbitrary",)),
    )(p, p, p, p, lbl, gw)


def _hgrn_backward(p, o, states, dymix, lbl, gw, dp_full, g_w_out, g_small, seq):
    nc = seq // CHUNK

    def body(q_ref, f_ref, v_ref, hg_ref, o_ref, st_ref, dy_ref, lbl_ref, gw_ref, dpin_ref, go_ref, gs_ref,
             dpo_ref, gsm_ref, ro_ref, rs_ref, ds_scr, dp_buf, dp_sems,
             qt_scr, kt_scr, qin_scr, kst_scr, vb_scr, dob_scr, g_scr, h_scr, dqi_scr, dks_scr, sd_scr,
             a_scr, da_scr, da0_scr, *sems):
        del dpin_ref
        step = pl.program_id(0)
        slot = step % 2
        exs = [_SlotExchange(go_ref, ro_ref, *sems[0:3], blocked=True),
               _SlotExchange(gs_ref, rs_ref, *sems[3:6], blocked=True)]

        @pl.when(step == 0)
        def _():
            for ex in exs:
                ex.start()

        def out_copy(s, chunk):
            rows = pl.ds(pl.multiple_of(chunk * CHUNK, CHUNK), CHUNK)
            return pltpu.make_async_copy(dp_buf.at[s], dpo_ref.at[rows, pl.ds(2 * D_MODEL, 4 * D_MODEL)],
                                         dp_sems.at[s])

        @pl.when(step == 0)
        def _():
            ds_scr[...] = jnp.zeros_like(ds_scr)
            gsm_ref[...] = jnp.zeros_like(gsm_ref)

        @pl.when(step >= 2)
        def _():
            out_copy(slot, nc + 1 - step).wait()

        dp_ref = dp_buf.at[slot]

        r = _iota((CHUNK, CHUNK), 0)
        c = _iota((CHUNK, CHUNK), 1)
        tri = jnp.where(c <= r, 1.0, 0.0).astype(BF16)
        triu = jnp.where(c >= r, 1.0, 0.0).astype(BF16)
        diag = (jnp.right_shift(r, 5) == jnp.right_shift(c, 5)) & (c <= r)
        row = _iota((CHUNK, D_MODEL), 0)
        sub = jnp.right_shift(row, 5)

        q = q_ref[...]
        lb, sig, f, sq = _hgrn_gate_terms(q, f_ref[...], lbl_ref[...])
        qs = q * sq
        k = 1.0 - f
        dec = _hgrn_decay(_tri_matmul(tri, jnp.log(f)))
        eb, ekst, ebl = dec["eb"], dec["ekst"], dec["ebl"]
        koff = _hgrn_operands(qs, k, dec, qt_scr, kt_scr)
        qin_scr[...] = (qs * eb).astype(BF16)
        kst_scr[...] = (k * ekst).astype(BF16)
        vb_scr[...] = v_ref[...].astype(BF16)
        hg = hg_ref[...]
        sh = _sigmoid(hg)
        dy = dy_ref[...]
        gwv = gw_ref[...]
        d_onw = dy * (hg * sh)
        d_on = d_onw * gwv
        d_gate = dy * gwv * (sh * (1.0 + hg * (1.0 - sh)))

        heads = [slice(h * HEAD_D, (h + 1) * HEAD_D) for h in range(N_HEADS)]
        for h, sl in enumerate(heads):
            o = o_ref[:, sl]
            rs = lax.rsqrt(jnp.mean(o * o, axis=-1, keepdims=True) + EPS)
            on = o * rs
            dp_ref[:, 3 * D_MODEL + h * HEAD_D:3 * D_MODEL + (h + 1) * HEAD_D] = (d_gate[:, sl] * on).astype(BF16)
            gsm_ref[1:2, sl] += jnp.sum(d_onw[:, sl] * on, axis=0, keepdims=True)
            d_onh = d_on[:, sl]
            dob_scr[:, sl] = (rs * (d_onh - on * jnp.mean(d_onh * on, axis=-1, keepdims=True))).astype(BF16)
        for h, sl in enumerate(heads):
            a_scr[h] = _hgrn_head_scores(qt_scr, kt_scr, sl, diag).astype(BF16)
            da = _dot(dob_scr[:, sl], vb_scr[:, sl], NT)
            da_scr[h] = da.astype(BF16)
            da0_scr[h] = jnp.where(diag, da, 0.0).astype(BF16)
        for h, sl in enumerate(heads):
            st = st_ref[0, h]
            dst = ds_scr[h]
            dstb = dst.astype(BF16)
            dp_ref[:, 2 * D_MODEL + h * HEAD_D:2 * D_MODEL + (h + 1) * HEAD_D] = (
                _dot(a_scr[h], dob_scr[:, sl], TN) + _dot(kst_scr[:, sl], dstb, NT)).astype(BF16)
            dqi_scr[:, sl] = _dot(dob_scr[:, sl], st.astype(BF16))
            dks_scr[:, sl] = _dot(vb_scr[:, sl], dstb)
            sd_scr[0:1, sl] = jnp.sum(st * dst, axis=0, keepdims=True)
            ds_scr[h] = dst * ebl[:, sl] + _dot(dob_scr[:, sl], qin_scr[:, sl], TN)
        for h, sl in enumerate(heads):
            g_scr[0, :, sl] = _dot(da0_scr[h], kt_scr[0, :, sl])
            h_scr[0, :, sl] = _dot(da0_scr[h], qt_scr[0, :, sl], TN)
            for j in range(1, N_SUB):
                g_scr[j, :, sl] = _dot(da_scr[h], kt_scr[j, :, sl])
                h_scr[j, :, sl] = _dot(da_scr[h], qt_scr[j, :, sl], TN)

        g0 = g_scr[0]
        h0 = h_scr[0]
        dq_inter = eb * dqi_scr[...]
        d_kst = ekst * dks_scr[...]
        db = qs * dq_inter - k * d_kst + qt_scr[0].astype(F32) * g0 - kt_scr[0].astype(F32) * h0
        gq = jnp.zeros((CHUNK, D_MODEL), F32)
        hsel = jnp.zeros((CHUNK, D_MODEL), F32)
        for j in range(N_SUB - 1):
            gj = g_scr[j + 1]
            gq = gq + dec["scales"][j] * gj
            db = db + qt_scr[j + 1].astype(F32) * gj
            hsel = jnp.where(sub == j, h_scr[j + 1], hsel)
        db = db - koff.astype(BF16).astype(F32) * hsel
        d_q = dec["eq0"] * g0 + dec["e_on"] * gq + dq_inter
        d_k = dec["ek0"] * h0 + dec["e_off"] * hsel + d_kst
        db_last = jnp.sum(k * d_kst, axis=0, keepdims=True) + ebl * sd_scr[0:1, :]
        db = db + jnp.where(row == CHUNK - 1, db_last, 0.0)
        dg = _tri_matmul(triu, db)
        d_f = dg / f - d_k
        dp_ref[:, D_MODEL:2 * D_MODEL] = (d_f * (1.0 - lb) * sig * (1.0 - sig)).astype(BF16)
        gsm_ref[0:1, :] += jnp.sum(d_f * (1.0 - sig), axis=0, keepdims=True) * (lb * (1.0 - lb))
        dp_ref[:, 0:D_MODEL] = (d_q * (sq * (1.0 + q * (1.0 - sq)))).astype(BF16)

        out_copy(slot, nc - 1 - step).start()

        @pl.when(step == nc - 1)
        def _():
            out_copy(1 - slot, 1).wait()
            out_copy(slot, 0).wait()
            for ex in exs:
                ex.wait()

    rc = lambda c: nc - 1 - c
    col = lambda j: pl.BlockSpec((CHUNK, D_MODEL), lambda c: (rc(c), j))
    par = lambda rows: pl.BlockSpec((rows, D_MODEL), lambda c: (0, 0))
    return pl.pallas_call(
        body, name="hgrn_bwd", grid=(nc,),
        in_specs=[col(2), col(3), col(4), col(5), col(0),
                  pl.BlockSpec((1, N_HEADS, HEAD_D, HEAD_D), lambda c: (rc(c), 0, 0, 0)),
                  col(1), par(2), par(1), ANY, ANY, ANY],
        out_specs=[ANY, par(8), ANY, ANY],
        out_shape=[SDS((seq, D_IN), BF16), SDS((8, D_MODEL), F32), SDS(g_w_out.shape, F32),
                   SDS(g_small.shape, F32)],
        input_output_aliases={9: 0},
        scratch_shapes=[pltpu.VMEM((N_HEADS, HEAD_D, HEAD_D), F32),
                        pltpu.VMEM((2, CHUNK, 4 * D_MODEL), BF16), pltpu.SemaphoreType.DMA((2,)),
                        pltpu.VMEM((N_SUB, CHUNK, D_MODEL), BF16), pltpu.VMEM((N_SUB, CHUNK, D_MODEL), BF16)]
                       + [pltpu.VMEM((CHUNK, D_MODEL), BF16)] * 4
                       + [pltpu.VMEM((N_SUB, CHUNK, D_MODEL), F32)] * 2 + [pltpu.VMEM((CHUNK, D_MODEL), F32)] * 2
                       + [pltpu.VMEM((8, D_MODEL), F32)] + [pltpu.VMEM((N_HEADS, CHUNK, CHUNK), BF16)] * 3
                       + EXCHANGE_SEMS * 2,
        compiler_params=_cparams(("arbitrary",)),
    )(p, p, p, p, o, states, dymix, lbl, gw, dp_full, g_w_out, g_small)


def _out_proj(yl, yh, wo, x, tgt, post_w, seq):
    tm = 512

    def body(yl_ref, yh_ref, wo_ref, x_ref, tg_ref, pw_ref, dymix_ref, dout_ref, gwo_ref, st_ref):
        @pl.when(pl.program_id(0) == 0)
        def _():
            gwo_ref[...] = jnp.zeros_like(gwo_ref)
            st_ref[...] = jnp.zeros_like(st_ref)

        ylv = yl_ref[...]
        yhv = yh_ref[...]
        y = _dot(ylv, wo_ref[0:D_MODEL, :]) + _dot(yhv, wo_ref[D_MODEL:D_MIX, :])
        r2 = lax.rsqrt(jnp.mean(y * y, axis=-1, keepdims=True) + EPS)
        yn = y * r2
        pw = pw_ref[...]
        e = (x_ref[...] + yn * pw) - tg_ref[...]
        st_ref[1:2, :] += jnp.sum(e * e, axis=0, keepdims=True) * (0.5 / D_MODEL)
        dout = e * (1.0 / D_MODEL)
        dout_ref[...] = dout
        st_ref[0:1, :] += jnp.sum(dout * yn, axis=0, keepdims=True)
        dyn = dout * pw
        dy = r2 * (dyn - yn * jnp.mean(dyn * yn, axis=-1, keepdims=True))
        dyb = dy.astype(BF16)
        dymix_ref[...] = _dot(dyb, wo_ref[...], NT)
        gwo_ref[0:D_MODEL, :] += _dot(ylv, dyb, TN)
        gwo_ref[D_MODEL:D_MIX, :] += _dot(yhv, dyb, TN)

    row = lambda w: pl.BlockSpec((tm, w), lambda m: (m, 0))
    full = lambda shape: pl.BlockSpec(shape, lambda m: (0,) * len(shape))
    once = lambda shape: pl.BlockSpec(shape, lambda m: (0,) * len(shape), pipeline_mode=pl.Buffered(1))
    return pl.pallas_call(
        body, name="out_proj", grid=(seq // tm,),
        in_specs=[row(D_MODEL), row(D_MODEL), once((D_MIX, D_MODEL)), row(D_MODEL), row(D_MODEL),
                  full((1, D_MODEL))],
        out_specs=[row(D_MIX), row(D_MODEL), once((D_MIX, D_MODEL)), full((8, D_MODEL))],
        out_shape=[SDS((seq, D_MIX), F32), SDS((seq, D_MODEL), F32), SDS((D_MIX, D_MODEL), F32),
                   SDS((8, D_MODEL), F32)],
        compiler_params=_cparams(("arbitrary",)),
    )(yl, yh, wo, x, tgt, post_w)


MESH = pl.DeviceIdType.MESH
ANY = pl.BlockSpec(memory_space=pl.ANY)
EXCHANGE_SEMS = [pltpu.SemaphoreType.DMA((N_DEV - 1,)), pltpu.SemaphoreType.DMA((N_DEV - 1,)),
                 pltpu.SemaphoreType.DMA(())]


def _mesh_pos():
    return lax.axis_index("x"), lax.axis_index("y"), lax.axis_index("c")


class _SlotExchange:
    def __init__(self, src_ref, dst_ref, send_sems, recv_sems, local_sem, blocked):
        x, y, c = _mesh_pos()
        me = 4 * x + 2 * y + c
        src = (lambda dest: src_ref.at[dest]) if blocked else (lambda dest: src_ref)
        self.local = pltpu.make_async_copy(src(me), dst_ref.at[me], local_sem)
        self.sends, self.recvs = [], []
        for k in range(1, N_DEV):
            px = 1 - x if (k >> 2) & 1 else x
            py = 1 - y if (k >> 1) & 1 else y
            pc = 1 - c if k & 1 else c
            peer = 4 * px + 2 * py + pc
            sems = dict(send_sem=send_sems.at[k - 1], recv_sem=recv_sems.at[k - 1],
                        device_id=(px, py, pc), device_id_type=MESH)
            self.sends.append(pltpu.make_async_remote_copy(src_ref=src(peer), dst_ref=dst_ref.at[me], **sems))
            self.recvs.append(pltpu.make_async_remote_copy(src_ref=dst_ref.at[peer], dst_ref=dst_ref.at[peer], **sems))

    def start(self):
        self.local.start()
        for cp in self.sends:
            cp.start()

    def wait(self):
        for cp in self.recvs:
            cp.wait_recv()
        for cp in self.sends:
            cp.wait_send()
        self.local.wait()


class _ChipExchange:
    def __init__(self, src_ref, dst_ref, send_sems, recv_sems, local_sem):
        x, y, c = _mesh_pos()
        chip = 2 * x + y
        self.local = pltpu.make_async_copy(src_ref.at[chip], dst_ref.at[chip], local_sem)
        self.sends, self.recvs = [], []
        for k in range(1, N_CHIPS):
            px = 1 - x if (k >> 1) & 1 else x
            py = 1 - y if k & 1 else y
            peer = 2 * px + py
            sems = dict(send_sem=send_sems.at[k - 1], recv_sem=recv_sems.at[k - 1],
                        device_id=(px, py, c), device_id_type=MESH)
            self.sends.append(pltpu.make_async_remote_copy(src_ref=src_ref.at[peer], dst_ref=dst_ref.at[chip], **sems))
            self.recvs.append(pltpu.make_async_remote_copy(src_ref=dst_ref.at[peer], dst_ref=dst_ref.at[peer], **sems))

    def start(self):
        self.local.start()
        for cp in self.sends:
            cp.start()

    def wait(self):
        for cp in self.recvs:
            cp.wait_recv()
        for cp in self.sends:
            cp.wait_send()
        self.local.wait()


GRAD_W_IN_TK = 2048


def _grad_w_in_sibling(u, dp, core, seq):
    tk = min(GRAD_W_IN_TK, seq)
    nk = seq // tk

    def body(core_ref, u_ref, dp_ref, g_ref):
        del core_ref

        @pl.when(pl.program_id(1) == 0)
        def _():
            g_ref[...] = jnp.zeros_like(g_ref)

        g_ref[0] += _dot(u_ref[...], dp_ref[...], TN)

    return pl.pallas_call(
        body, name="grad_w_in_sibling",
        grid_spec=pltpu.PrefetchScalarGridSpec(
            num_scalar_prefetch=1, grid=(N_CHIPS, nk),
            in_specs=[pl.BlockSpec((tk, D_MODEL), lambda n, k, c: (k, 0)),
                      pl.BlockSpec((tk, W_BLK), lambda n, k, c: (k, 2 * n + 1 - c[0]))],
            out_specs=pl.BlockSpec((1, D_MODEL, W_BLK), lambda n, k, c: (n, 0, 0))),
        out_shape=SDS((N_CHIPS, D_MODEL, W_BLK), F32),
        compiler_params=_cparams(("parallel", "arbitrary")),
    )(core, u, dp)


def _grad_w_in_own(u, dp, core, g_sib, seq):
    tk = min(GRAD_W_IN_TK, seq)
    nk = seq // tk

    def body(core_ref, u_ref, dp_ref, gsib_ref, g_ref, land, send_sem, recv_sem):
        del core_ref
        n = pl.program_id(0)
        k = pl.program_id(1)
        x, y, c = _mesh_pos()
        swap = pltpu.make_async_remote_copy(src_ref=gsib_ref, dst_ref=land, send_sem=send_sem, recv_sem=recv_sem,
                                            device_id=(x, y, 1 - c), device_id_type=MESH)

        @pl.when((n == 0) & (k == 0))
        def _():
            swap.start()

        @pl.when(k == 0)
        def _():
            g_ref[...] = jnp.zeros_like(g_ref)

        g_ref[0] += _dot(u_ref[...], dp_ref[...], TN)

        @pl.when((n == 0) & (k == nk - 1))
        def _():
            swap.wait_recv()

        @pl.when(k == nk - 1)
        def _():
            g_ref[0] += land[n]

        @pl.when((n == N_CHIPS - 1) & (k == nk - 1))
        def _():
            swap.wait_send()

    return pl.pallas_call(
        body, name="grad_w_in_own",
        grid_spec=pltpu.PrefetchScalarGridSpec(
            num_scalar_prefetch=1, grid=(N_CHIPS, nk),
            in_specs=[pl.BlockSpec((tk, D_MODEL), lambda n, k, c: (k, 0)),
                      pl.BlockSpec((tk, W_BLK), lambda n, k, c: (k, 2 * n + c[0])), ANY],
            out_specs=pl.BlockSpec((1, D_MODEL, W_BLK), lambda n, k, c: (n, 0, 0)),
            scratch_shapes=[pltpu.VMEM((N_CHIPS, D_MODEL, W_BLK), F32), pltpu.SemaphoreType.DMA(()),
                            pltpu.SemaphoreType.DMA(())]),
        out_shape=SDS((N_CHIPS, D_MODEL, W_BLK), F32),
        compiler_params=_cparams(("arbitrary", "arbitrary")),
    )(core, u, dp, g_sib)


def _grad_x(dp, w_all, x, pre_w, dout, g_chip, seq):
    tm = 512
    nm = seq // tm

    def body(dp_ref, w_ref, x_ref, pw_ref, do_ref, gsrc_ref, gx_ref, gpw_ref, recv_ref,
             send_sems, recv_sems, local_sem):
        m = pl.program_id(0)
        ex = _ChipExchange(gsrc_ref, recv_ref, send_sems, recv_sems, local_sem)

        @pl.when(m == 0)
        def _():
            ex.start()
            gpw_ref[...] = jnp.zeros_like(gpw_ref)

        du = _dot(dp_ref[:, 0:W_BLK], w_ref[0], NT)
        for j in range(1, N_DEV):
            du = du + _dot(dp_ref[:, j * W_BLK:(j + 1) * W_BLK], w_ref[j], NT)
        xv = x_ref[...]
        r1 = lax.rsqrt(jnp.mean(xv * xv, axis=-1, keepdims=True) + EPS)
        xn = xv * r1
        gpw_ref[0:1, :] += jnp.sum(du * xn, axis=0, keepdims=True)
        dxn = du * pw_ref[...]
        gx_ref[...] = r1 * (dxn - xn * jnp.mean(dxn * xn, axis=-1, keepdims=True)) + do_ref[...]

        @pl.when(m == nm - 1)
        def _():
            ex.wait()

    row = lambda w: pl.BlockSpec((tm, w), lambda m: (m, 0))
    return pl.pallas_call(
        body, name="grad_x", grid=(nm,),
        in_specs=[row(D_IN), pl.BlockSpec((N_DEV, D_MODEL, W_BLK), lambda m: (0, 0, 0), pipeline_mode=pl.Buffered(1)),
                  row(D_MODEL),
                  pl.BlockSpec((1, D_MODEL), lambda m: (0, 0)), row(D_MODEL), ANY],
        out_specs=[row(D_MODEL), pl.BlockSpec((8, D_MODEL), lambda m: (0, 0)), ANY],
        out_shape=[SDS((seq, D_MODEL), F32), SDS((8, D_MODEL), F32), SDS(g_chip.shape, F32)],
        scratch_shapes=[pltpu.SemaphoreType.DMA((N_CHIPS - 1,)), pltpu.SemaphoreType.DMA((N_CHIPS - 1,)),
                        pltpu.SemaphoreType.DMA(())],
        compiler_params=_cparams(("arbitrary",)),
    )(dp, w_all, x, pre_w, dout, g_chip)


def _local_step(x, tgt, p, u, conv_w, conv_b, wa, wx, ba, bx, lam, lbl, gnorm_w, w_out, post_w):
    seq = x.shape[0]
    h, y_lru = _lru_forward(p, conv_w, conv_b, wa, wx, ba, bx, lam, seq)
    y_hgrn, o, states = _hgrn_forward(p, lbl, gnorm_w, seq)
    dymix, dout, g_w_out, stats = _out_proj(y_lru, y_hgrn, w_out, x, tgt, post_w, seq)
    dp_lru, g_wa, g_wx, ls = _lru_backward(p, h, dymix, conv_w, conv_b, wa, wx, ba, bx, lam, seq)
    g_small = _pack_small(_shard_rows(g_wa, LRU_BLOCKS), _shard_rows(g_wx, LRU_BLOCKS),
                          _shard_rows(ls[0:4].reshape(4, D_MODEL, 1), 4).reshape(N_DEV, 4, 128),
                          _shard_rows(ls[5].reshape(4, LRU_BW, 1), 4).reshape(N_DEV, 4, 32),
                          _shard_rows(ls[6].reshape(4, LRU_BW, 1), 4).reshape(N_DEV, 4, 32))
    dp, hgrn_small, r_out, r_small = _hgrn_backward(
        p, o, states, dymix, lbl, gnorm_w, dp_lru, g_w_out.reshape(N_DEV, D_MIX // N_DEV, D_MODEL), g_small, seq)
    return dict(u=u, dp=dp, dout=dout, r_out=r_out, r_small=r_small,
                lru_small=ls, hgrn_small=hgrn_small, stats=stats)


class _TwoLevelGather:
    def __init__(self, ins, outs, send_sems, recv_sems, local_sems):
        self.ins, self.outs = ins, outs
        self.send_sems, self.recv_sems, self.local_sems = send_sems, recv_sems, local_sems
        x, y, c = _mesh_pos()
        self.c = c
        self.me, self.sibling = (x, y, c), (x, y, 1 - c)
        self.chips = [(1 - x, y), (x, 1 - y), (1 - x, 1 - y)]
        n = len(ins)
        self.mine = [pltpu.make_async_copy(ins[a], self._slot(a, self.me), local_sems.at[a]) for a in range(n)]
        self.first = []
        for a in range(n):
            self.first.append(self._copy(a, 0, self.me, self.sibling, src=ins[a]))
            self.first += [self._copy(a, 1 + j, self.me, (*chip, c), src=ins[a])
                           for j, chip in enumerate(self.chips)]
        self.passed = [self._copy(a, 4 + j, (*chip, c), self.sibling)
                       for j, chip in enumerate(self.chips) for a in range(n)]

    def _slot(self, a, pos):
        return self.outs[a].at[4 * pos[0] + 2 * pos[1] + pos[2]]

    def _copy(self, a, k, block, to, src=None):
        dst = self._slot(a, block)
        return pltpu.make_async_remote_copy(
            src_ref=dst if src is None else src, dst_ref=dst,
            send_sem=self.send_sems.at[a, k], recv_sem=self.recv_sems.at[a, k],
            device_id=to, device_id_type=MESH)

    def start(self):
        for cp in self.mine + self.first:
            cp.start()

    def forward(self):
        n = len(self.ins)
        for j, chip in enumerate(self.chips):
            for a in range(n):
                self._copy(a, 1 + j, (*chip, self.c), self.me).wait_recv()
                self.passed[j * n + a].start()

    def finish(self):
        for a in range(len(self.ins)):
            self._copy(a, 0, self.sibling, self.me).wait_recv()
            for j, chip in enumerate(self.chips):
                self._copy(a, 4 + j, (*chip, 1 - self.c), self.me).wait_recv()
        for cp in self.first + self.passed:
            cp.wait_send()
        for cp in self.mine:
            cp.wait()


W_IN_DIRECT = (1, 2, 4, 6)
W_IN_PASSED = (2, 4, 6)


def _in_proj_gather(x, pre_w, w_in_blk, w_out_blk, small_blk, me, seq):
    tm = min(1024, seq)
    nm = seq // tm
    last = N_DEV - 1

    def body(me_ref, x_ref, pw_ref, wblk_ref, woblk_ref, smblk_ref,
             p_ref, u_ref, wall_ref, woall_ref, small_ref,
             u_all, w_vmem, own_sem, d_send, d_recv, f_send, f_recv, wb_sems, g_send, g_recv, g_local):
        i = pl.program_id(0)
        m = pl.program_id(1)
        idx = me_ref[0]
        x_, y_, c_ = _mesh_pos()
        aux = _TwoLevelGather([woblk_ref, smblk_ref], [woall_ref, small_ref], g_send, g_recv, g_local)

        def peer(k):
            return (1 - x_ if (k >> 2) & 1 else x_, 1 - y_ if (k >> 1) & 1 else y_, 1 - c_ if k & 1 else c_)

        def direct(k):
            f = W_IN_DIRECT.index(k)
            return (pltpu.make_async_remote_copy(src_ref=wblk_ref, dst_ref=w_vmem.at[idx], send_sem=d_send.at[f],
                                                 recv_sem=d_recv.at[f], device_id=peer(k), device_id_type=MESH),
                    pltpu.make_async_remote_copy(src_ref=w_vmem.at[idx ^ k], dst_ref=w_vmem.at[idx ^ k],
                                                 send_sem=d_send.at[f], recv_sem=d_recv.at[f], device_id=peer(k),
                                                 device_id_type=MESH))

        def passed(k):
            f = W_IN_PASSED.index(k)
            return (pltpu.make_async_remote_copy(src_ref=w_vmem.at[idx ^ k], dst_ref=w_vmem.at[idx ^ k],
                                                 send_sem=f_send.at[f], recv_sem=f_recv.at[f], device_id=peer(1),
                                                 device_id_type=MESH),
                    pltpu.make_async_remote_copy(src_ref=w_vmem.at[idx ^ (k + 1)], dst_ref=w_vmem.at[idx ^ (k + 1)],
                                                 send_sem=f_send.at[f], recv_sem=f_recv.at[f], device_id=peer(1),
                                                 device_id_type=MESH))

        def write_back(k):
            return pltpu.make_async_copy(w_vmem.at[idx ^ k], wall_ref.at[idx ^ k], wb_sems.at[k])

        own = pltpu.make_async_copy(wblk_ref, w_vmem.at[idx], own_sem)

        @pl.when((i == 0) & (m == 0))
        def _():
            own.start()
            for k in W_IN_DIRECT:
                direct(k)[0].start()
            aux.start()
            own.wait()
            write_back(0).start()

        for k in range(1, N_DEV):
            @pl.when((i == k) & (m == 0))
            def _(k=k):
                if k in W_IN_DIRECT:
                    direct(k)[1].wait_recv()
                    if k in W_IN_PASSED:
                        passed(k)[0].start()
                else:
                    passed(k - 1)[1].wait_recv()
                write_back(k).start()

        @pl.when((i == N_CHIPS) & (m == 0))
        def _():
            aux.forward()

        rows = pl.ds(pl.multiple_of(m * tm, tm), tm)

        @pl.when(i == 0)
        def _():
            xv = x_ref[...]
            r = lax.rsqrt(jnp.mean(xv * xv, axis=-1, keepdims=True) + EPS)
            ub = (xv * r * pw_ref[...]).astype(BF16)
            u_all[rows, :] = ub
            u_ref[...] = ub

        p_ref[...] = _dot(u_all[rows, :], w_vmem[idx ^ i])

        @pl.when((i == last) & (m == nm - 1))
        def _():
            for k in W_IN_DIRECT:
                direct(k)[0].wait_send()
            for k in W_IN_PASSED:
                passed(k)[0].wait_send()
            for k in range(N_DEV):
                write_back(k).wait()
            aux.finish()

    first_pass = lambda i, m: jnp.where(i == 0, m, nm - 1)
    return pl.pallas_call(
        body, name="in_proj_gather",
        grid_spec=pltpu.PrefetchScalarGridSpec(
            num_scalar_prefetch=1, grid=(N_DEV, nm),
            in_specs=[pl.BlockSpec((tm, D_MODEL), lambda i, m, me: (first_pass(i, m), 0)),
                      pl.BlockSpec((1, D_MODEL), lambda i, m, me: (0, 0)), ANY, ANY, ANY],
            out_specs=[pl.BlockSpec((tm, W_BLK), lambda i, m, me: (m, me[0] ^ i)),
                       pl.BlockSpec((tm, D_MODEL), lambda i, m, me: (first_pass(i, m), 0)), ANY, ANY, ANY],
            scratch_shapes=[pltpu.VMEM((seq, D_MODEL), BF16), pltpu.VMEM((N_DEV, D_MODEL, W_BLK), BF16),
                            pltpu.SemaphoreType.DMA(()),
                            pltpu.SemaphoreType.DMA((len(W_IN_DIRECT),)), pltpu.SemaphoreType.DMA((len(W_IN_DIRECT),)),
                            pltpu.SemaphoreType.DMA((len(W_IN_PASSED),)), pltpu.SemaphoreType.DMA((len(W_IN_PASSED),)),
                            pltpu.SemaphoreType.DMA((N_DEV,)),
                            pltpu.SemaphoreType.DMA((2, 7)), pltpu.SemaphoreType.DMA((2, 7)),
                            pltpu.SemaphoreType.DMA((2,))]),
        out_shape=[SDS((seq, D_IN), F32), SDS((seq, D_MODEL), BF16), SDS((N_DEV, D_MODEL, W_BLK), BF16),
                   SDS((N_DEV,) + w_out_blk.shape, w_out_blk.dtype), SDS((N_DEV,) + small_blk.shape, small_blk.dtype)],
        compiler_params=_cparams(("arbitrary", "arbitrary")),
    )(me, x, pre_w, w_in_blk, w_out_blk, small_blk)


def _exchange_grads(blocks, repl):
    nb = len(blocks)
    n = nb + 1

    def body(*refs):
        ins, outs, sems = refs[:n], refs[n:2 * n], refs[2 * n:]
        exs = [_SlotExchange(ins[a], outs[a], *sems[3 * a:3 * a + 3], blocked=a < nb) for a in range(n)]
        for ex in exs:
            ex.start()
        for ex in exs:
            ex.wait()

    arrs = list(blocks) + [repl]
    shapes = [SDS(b.shape, b.dtype) for b in blocks] + [SDS((N_DEV,) + repl.shape, repl.dtype)]
    return pl.pallas_call(
        body, name="exchange_small", out_shape=shapes,
        in_specs=[ANY] * n, out_specs=[ANY] * n,
        scratch_shapes=EXCHANGE_SEMS * n,
    )(*arrs)


def _pack_rows(picks, name):
    arrs = [p[0] for p in picks]

    def body(*refs):
        out = refs[-1]
        out[...] = jnp.zeros_like(out)
        at = 0
        for ref, (_, row, rows, scale) in zip(refs[:-1], picks):
            out[at:at + rows, :] = ref[row:row + rows, :] * scale
            at += rows

    return pl.pallas_call(body, name=name, out_shape=SDS((8, D_MODEL), F32))(*arrs)


def _adamw(g, w, m, v):
    m2 = ADAM_B1 * m + (1.0 - ADAM_B1) * g
    v2 = ADAM_B2 * v + (1.0 - ADAM_B2) * (g * g)
    m_hat = m2 / (1.0 - ADAM_B1 ** ADAM_STEP)
    v_hat = v2 / (1.0 - ADAM_B2 ** ADAM_STEP)
    delta = -ADAM_LR * (m_hat / (jnp.sqrt(v_hat) + ADAM_EPS) + ADAM_WD * w)
    return delta, m2, v2


def _sum_slots(r_ref):
    g = r_ref[0]
    for s in range(1, r_ref.shape[0]):
        g = g + r_ref[s]
    return g


def _sum_adamw(recv, w, m, v, tr, name):
    rows, cols = w.shape

    def body(r_ref, w_ref, m_ref, v_ref, g_ref, d_ref, m2_ref, v2_ref):
        g = _sum_slots(r_ref)
        g_ref[...] = g
        d_ref[...], m2_ref[...], v2_ref[...] = _adamw(g, w_ref[...], m_ref[...], v_ref[...])

    blk = pl.BlockSpec((tr, cols), lambda i: (i, 0))
    return pl.pallas_call(
        body, name=name, grid=(rows // tr,),
        in_specs=[pl.BlockSpec((recv.shape[0], tr, cols), lambda i: (0, i, 0)), blk, blk, blk],
        out_specs=[blk] * 4, out_shape=[SDS((rows, cols), F32)] * 4,
        compiler_params=_cparams(("parallel",)),
    )(recv, w, m, v)


def _sum_adamw_repl(recv, w, m, v):
    def body(r_ref, w_ref, m_ref, v_ref, g_ref, d_ref, m2_ref, v2_ref, loss_ref):
        g = _sum_slots(r_ref)
        g_ref[...] = g
        d_ref[...], m2_ref[...], v2_ref[...] = _adamw(g, w_ref[...], m_ref[...], v_ref[...])
        total = jnp.sum(g[RP_LOSS:RP_LOSS + 1, :], axis=-1, keepdims=True)
        loss_ref[...] = jnp.broadcast_to(total, loss_ref.shape)

    return pl.pallas_call(
        body, name="adamw_repl",
        out_shape=[SDS((8, D_MODEL), F32)] * 4 + [SDS((8, 128), F32)],
    )(recv, w, m, v)


def _shard_rows(t, lead):
    r = t.shape[1] // N_DEV
    t = t.reshape((lead, N_DEV, r) + t.shape[2:])
    return jnp.moveaxis(t, 1, 0)


def _pad8(t):
    return jnp.pad(t, ((0, 0), (0, 8 - t.shape[1]), (0, 0)))


def _pack_small(wa, wx, cw, b_a, b_x):
    n = wa.shape[0]
    return jnp.concatenate([
        wa.reshape(n, 256, 128), wx.reshape(n, 256, 128), _pad8(cw),
        _pad8(b_a.reshape(n, 1, 128)), _pad8(b_x.reshape(n, 1, 128))], axis=1)


def _unpack_small(t):
    n = t.shape[0]
    return (t[:, SM_WA:SM_WA + 256].reshape(n, 4, 32, 256), t[:, SM_WX:SM_WX + 256].reshape(n, 4, 32, 256),
            t[:, SM_CW:SM_CW + 4], t[:, SM_BA].reshape(n, 4, 32), t[:, SM_BX].reshape(n, 4, 32))


def kernel(x, pre_norm_w, w_in, conv_w, conv_b, lru_w_a, lru_b_a, lru_w_x, lru_b_x, lru_lambda, hgrn_lb_logits, hgrn_gnorm_w, w_out, post_norm_w, loss_target, m_pre_norm_w, m_w_in, m_conv_w, m_conv_b, m_lru_w_a, m_lru_b_a, m_lru_w_x, m_lru_b_x, m_lru_lambda, m_hgrn_lb_logits, m_hgrn_gnorm_w, m_w_out, m_post_norm_w, v_pre_norm_w, v_w_in, v_conv_w, v_conv_b, v_lru_w_a, v_lru_b_a, v_lru_w_x, v_lru_b_x, v_lru_lambda, v_hgrn_lb_logits, v_hgrn_gnorm_w, v_w_out, v_post_norm_w):
    seq = x.shape[1]
    x2 = x.reshape(seq, D_MODEL)
    tgt = loss_target.reshape(seq, D_MODEL)

    small_w = _pack_small(lru_w_a, lru_w_x, conv_w, lru_b_a, lru_b_x)[0]
    me = (4 * lax.axis_index("x") + 2 * lax.axis_index("y") + lax.axis_index("c")).astype(jnp.int32).reshape(1)
    p, u, w_in_all, w_out_all, small_all = _in_proj_gather(
        x2, pre_norm_w, w_in[0].astype(BF16), w_out[0].astype(BF16), small_w, me, seq)
    wa_s, wx_s, cw_s, ba_s, bx_s = _unpack_small(small_all)
    wa = jnp.moveaxis(wa_s, 0, 1).reshape(LRU_BLOCKS, LRU_BW, LRU_BW).astype(BF16)
    wx = jnp.moveaxis(wx_s, 0, 1).reshape(LRU_BLOCKS, LRU_BW, LRU_BW).astype(BF16)
    cw = jnp.moveaxis(cw_s, 0, 1).reshape(4, D_MODEL)
    ba = jnp.moveaxis(ba_s, 0, 1).reshape(1, D_MODEL)
    bx = jnp.moveaxis(bx_s, 0, 1).reshape(1, D_MODEL)

    loc = _local_step(x2, tgt, p, u, cw, conv_b, wa, wx, ba, bx, lru_lambda,
                      hgrn_lb_logits, hgrn_gnorm_w, w_out_all.reshape(D_MIX, D_MODEL), post_norm_w)

    ls, r_out, r_small = loc["lru_small"], loc["r_out"], loc["r_small"]
    core = lax.axis_index("c").astype(jnp.int32).reshape(1)
    g_sib = _grad_w_in_sibling(loc["u"], loc["dp"], core, seq)
    g_chip = _grad_w_in_own(loc["u"], loc["dp"], core, g_sib, seq)
    grad_x, pre_small, r_in = _grad_x(loc["dp"], w_in_all, x2, pre_norm_w, loc["dout"], g_chip, seq)
    g_repl = _pack_rows([(pre_small, 0, 1, 1.0), (ls, 4, 1, 1.0), (ls, 7, 1, 1.0),
                         (loc["hgrn_small"], 0, 1, 1.0), (loc["hgrn_small"], 0, 1, -1.0),
                         (loc["hgrn_small"], 1, 1, 1.0), (loc["stats"], 0, 2, 1.0)], "pack_grads")
    (r_repl,) = _exchange_grads([], g_repl)

    zero = jnp.zeros((1, D_MODEL), F32)
    pack_w = lambda a, b, c_, d, e, f, name: _pack_rows(
        [(a, 0, 1, 1.0), (b, 0, 1, 1.0), (c_, 0, 1, 1.0), (d, 0, 2, 1.0), (e, 0, 1, 1.0), (f, 0, 1, 1.0),
         (zero, 0, 1, 1.0)], name)
    w_repl = pack_w(pre_norm_w, conv_b, lru_lambda, hgrn_lb_logits, hgrn_gnorm_w, post_norm_w, "pack_w")
    m_repl = pack_w(m_pre_norm_w, m_conv_b, m_lru_lambda, m_hgrn_lb_logits, m_hgrn_gnorm_w, m_post_norm_w, "pack_m")
    v_repl = pack_w(v_pre_norm_w, v_conv_b, v_lru_lambda, v_hgrn_lb_logits, v_hgrn_gnorm_w, v_post_norm_w, "pack_v")
    o_repl = _sum_adamw_repl(r_repl, w_repl, m_repl, v_repl)
    loss = o_repl[4][0, 0]

    o_in = _sum_adamw(r_in, w_in[0], m_w_in[0], v_w_in[0], 128, "adamw_w_in")
    o_out = _sum_adamw(r_out, w_out[0], m_w_out[0], v_w_out[0], 64, "adamw_w_out")
    o_small = _sum_adamw(r_small,
                         _pack_small(lru_w_a, lru_w_x, conv_w, lru_b_a, lru_b_x)[0],
                         _pack_small(m_lru_w_a, m_lru_w_x, m_conv_w, m_lru_b_a, m_lru_b_x)[0],
                         _pack_small(v_lru_w_a, v_lru_w_x, v_conv_w, v_lru_b_a, v_lru_b_x)[0],
                         SM_ROWS, "adamw_small")

    outs = [loss, grad_x.reshape(x.shape)]
    for kind in range(4):
        rp = o_repl[kind]
        swa, swx, scw, sba, sbx = _unpack_small(o_small[kind][None])
        outs += [rp[RP_PRE:RP_PRE + 1], o_in[kind][None], scw, rp[RP_CB:RP_CB + 1], swa, sba, swx, sbx,
                 rp[RP_LAM:RP_LAM + 1], rp[RP_LB0:RP_LB1 + 1], rp[RP_GN:RP_GN + 1], o_out[kind][None],
                 rp[RP_POST:RP_POST + 1]]
    return tuple(outs)
```

```python
import functools

import jax
import jax.numpy as jnp
from jax import lax
from jax.experimental import pallas as pl
from jax.experimental.pallas import tpu as pltpu

F32 = jnp.float32
BF16 = jnp.bfloat16
SDS = jax.ShapeDtypeStruct

D_MODEL = 1024
D_IN = 6144
N_DEV = 8
N_CHIPS = 4
W_BLK = D_IN // N_DEV
D_MIX = 2048
LRU_BLOCKS = 4
LRU_BW = 256
LRU_C = 8.0
N_HEADS = 8
HEAD_D = 128
CHUNK = 128
SUB = 32
N_SUB = CHUNK // SUB
HGRN_STEP_CHUNKS = 2
EXP_CLAMP = 80.0
EPS = 1e-6

ADAM_LR = 0.001
ADAM_B1 = 0.9
ADAM_B2 = 0.999
ADAM_EPS = 1e-08
ADAM_WD = 0.01
ADAM_STEP = 10

VMEM_LIMIT = 56 * 1024 * 1024

NN = (((1,), (0,)), ((), ()))
NT = (((1,), (1,)), ((), ()))
TN = (((0,), (0,)), ((), ()))

SM_WA = 0
SM_WX = 256
SM_CW = 512
SM_BA = 520
SM_BX = 528
SM_ROWS = 536

RP_PRE, RP_CB, RP_LAM, RP_LB0, RP_LB1, RP_GN, RP_POST, RP_LOSS = range(8)


def _dot(a, b, dims=NN):
    return lax.dot_general(a, b, dims, preferred_element_type=F32)


def _sigmoid(x):
    return 0.5 * jnp.tanh(0.5 * x) + 0.5


def _sigmoid_pos(x):
    return 1.0 / (1.0 + jnp.exp(-x))


def _cparams(sem, vmem=VMEM_LIMIT):
    return pltpu.CompilerParams(dimension_semantics=sem, vmem_limit_bytes=vmem)


def _iota(shape, axis):
    return lax.broadcasted_iota(jnp.int32, shape, axis)


def _softplus_neg(lam):
    z = -lam
    e = jnp.exp(-jnp.abs(z))
    u = 1.0 + e
    log1p_e = jnp.where(u == 1.0, e, jnp.log(u) * (e / (u - 1.0)))
    sp = jnp.maximum(z, 0.0) + log1p_e
    dsp = -jnp.where(z >= 0.0, 1.0 / u, e / u)
    return sp, dsp


def _neg_expm1(x):
    poly = x * (1.0 + x * (1.0 / 2 + x * (1.0 / 6 + x * (1.0 / 24 + x * (1.0 / 120)))))
    return jnp.where(x > -1.0 / 16, -poly, 1.0 - jnp.exp(x))


def _conv_taps(lx, prev8, cw_ref, cb_ref, tile):
    xc = cb_ref[...] + cw_ref[3:4, :] * lx
    for j in (1, 2, 3):
        xc = xc + cw_ref[3 - j:4 - j, :] * pltpu.roll(lx, j, 0)
    row8 = _iota((8, D_MODEL), 0)
    last8 = lx[tile - 8:tile, :]
    fix = jnp.zeros((8, D_MODEL), F32)
    for j in (1, 2, 3):
        wrong = pltpu.roll(last8, j, 0)
        right = pltpu.roll(prev8, j, 0)
        fix = fix + cw_ref[3 - j:4 - j, :] * jnp.where(row8 < j, right - wrong, 0.0)
    return xc, fix


def _lru_gates(xcs, wa, wx, ba, bx, sp):
    xb = xcs.astype(BF16)
    r = _sigmoid_pos(_dot(xb, wa) + ba)
    i = _sigmoid(_dot(xb, wx) + bx)
    la = (-LRU_C * sp) * r
    a = jnp.exp(la)
    one_minus_a2 = _neg_expm1(2.0 * la)
    return r, i, a, one_minus_a2


def _lru_forward(p, conv_w, conv_b, wa, wx, ba, bx, lam, seq):
    tile = min(512, seq // 2)
    nblk = tile // 8

    def body(lx_ref, gt_ref, cw_ref, cb_ref, wa_ref, wx_ref, ba_ref, bx_ref, lam_ref,
             h_ref, y_ref, ext, hcar, xc_scr, a_scr, u_scr):
        @pl.when(pl.program_id(0) == 0)
        def _():
            ext[0:8, :] = jnp.zeros((8, D_MODEL), F32)
            hcar[...] = jnp.zeros_like(hcar)

        lx = lx_ref[...]
        ext[8:8 + tile, :] = lx
        xc = cb_ref[...] + cw_ref[3:4, :] * lx
        for j in (1, 2, 3):
            xc = xc + cw_ref[3 - j:4 - j, :] * ext[8 - j:8 - j + tile, :]
        xc_scr[...] = xc
        ext[0:8, :] = lx_ref[tile - 8:tile, :]
        sp, _ = _softplus_neg(lam_ref[...])
        for n in range(LRU_BLOCKS):
            sl = slice(n * LRU_BW, (n + 1) * LRU_BW)
            xcs = xc_scr[:, sl]
            _, i, a, ne = _lru_gates(xcs, wa_ref[n], wx_ref[n], ba_ref[:, sl], bx_ref[:, sl], sp[:, sl])
            a_scr[:, sl] = a
            u_scr[:, sl] = jnp.sqrt(ne) * (i * xcs)

        row8 = _iota((8, D_MODEL), 0)

        def blk(j, hc):
            off = pl.multiple_of(j * 8, 8)
            a = a_scr[pl.ds(off, 8), :]
            u = u_scr[pl.ds(off, 8), :]
            for k in (1, 2, 4):
                m = row8 >= k
                u = jnp.where(m, u + a * pltpu.roll(u, k, 0), u)
                a = jnp.where(m, a * pltpu.roll(a, k, 0), a)
            h = u + a * hc
            h_ref[pl.ds(off, 8), :] = h
            return jnp.broadcast_to(h[7:8, :], (8, D_MODEL))

        hcar[...] = lax.fori_loop(0, nblk, blk, hcar[...])
        g = gt_ref[...]
        y_ref[...] = (h_ref[...] * (g * _sigmoid(g))).astype(BF16)

    full = lambda shape: pl.BlockSpec(shape, lambda t: (0,) * len(shape))
    return pl.pallas_call(
        body, name="lru_fwd", grid=(seq // tile,),
        in_specs=[pl.BlockSpec((tile, D_MODEL), lambda t: (t, 0)),
                  pl.BlockSpec((tile, D_MODEL), lambda t: (t, 1)),
                  full((4, D_MODEL)), full((1, D_MODEL)),
                  full((LRU_BLOCKS, LRU_BW, LRU_BW)), full((LRU_BLOCKS, LRU_BW, LRU_BW)),
                  full((1, D_MODEL)), full((1, D_MODEL)), full((1, D_MODEL))],
        out_specs=[pl.BlockSpec((tile, D_MODEL), lambda t: (t, 0)),
                   pl.BlockSpec((tile, D_MODEL), lambda t: (t, 0))],
        out_shape=[SDS((seq, D_MODEL), F32), SDS((seq, D_MODEL), BF16)],
        scratch_shapes=[pltpu.VMEM((tile + 8, D_MODEL), F32), pltpu.VMEM((8, D_MODEL), F32),
                        pltpu.VMEM((tile, D_MODEL), F32), pltpu.VMEM((tile, D_MODEL), F32),
                        pltpu.VMEM((tile, D_MODEL), F32)],
        compiler_params=_cparams(("arbitrary",)),
    )(p, p, conv_w, conv_b, wa, wx, ba, bx, lam)


def _lru_backward(p, h, dymix, conv_w, conv_b, wa, wx, ba, bx, lam, seq):
    tile = min(256, seq // 2)
    nt = seq // tile
    nblk = tile // 8
    t8 = tile // 8

    def body(lx_ref, lxh_ref, gt_ref, h_ref, hh_ref, dy_ref, cw_ref, cb_ref, wa_ref, wx_ref, ba_ref,
             bx_ref, lam_ref, dp_ref, gwa_ref, gwx_ref, gsm_ref,
             lamcar, anext, dxc8, xc_scr, r_scr, i_scr, a_scr, m_scr, rm_scr, c_scr, l_scr, dxc_scr):
        step = pl.program_id(0)
        first_tile = step == nt - 1

        @pl.when(step == 0)
        def _():
            lamcar[...] = jnp.zeros_like(lamcar)
            anext[...] = jnp.zeros_like(anext)
            dxc8[...] = jnp.zeros_like(dxc8)
            gwa_ref[...] = jnp.zeros_like(gwa_ref)
            gwx_ref[...] = jnp.zeros_like(gwx_ref)
            gsm_ref[...] = jnp.zeros_like(gsm_ref)

        keep = jnp.where(first_tile, 0.0, 1.0)
        lx = lx_ref[...]
        prev8 = lxh_ref[...] * keep
        xc, fix = _conv_taps(lx, prev8, cw_ref, cb_ref, tile)
        xc_scr[...] = xc
        xc_scr[0:8, :] = xc_scr[0:8, :] + fix
        sp, dsp = _softplus_neg(lam_ref[...])
        for n in range(LRU_BLOCKS):
            sl = slice(n * LRU_BW, (n + 1) * LRU_BW)
            r, i, a, ne = _lru_gates(xc_scr[:, sl], wa_ref[n], wx_ref[n], ba_ref[:, sl], bx_ref[:, sl],
                                     sp[:, sl])
            r_scr[:, sl] = r
            i_scr[:, sl] = i
            a_scr[:, sl] = a
            m_scr[:, sl] = jnp.sqrt(ne)
            rm_scr[:, sl] = lax.rsqrt(ne)

        g = gt_ref[...]
        sg = _sigmoid(g)
        dy = dy_ref[...]
        hv = h_ref[...]
        dp_ref[:, D_MODEL:2 * D_MODEL] = (dy * hv * (sg * (1.0 + g * (1.0 - sg)))).astype(BF16)

        rowt = _iota((tile, D_MODEL), 0)
        av = a_scr[...]
        l_scr[...] = dy * (g * sg)
        c_scr[...] = jnp.where(rowt == tile - 1, anext[...][0:1, :], pltpu.roll(av, tile - 1, 0))
        anext[...] = jnp.broadcast_to(av[0:1, :], (8, D_MODEL))
        row8 = _iota((8, D_MODEL), 0)

        def blk(jj, lc):
            off = pl.multiple_of((nblk - 1 - jj) * 8, 8)
            c = c_scr[pl.ds(off, 8), :]
            u = l_scr[pl.ds(off, 8), :]
            for k in (1, 2, 4):
                m = row8 < 8 - k
                u = jnp.where(m, u + c * pltpu.roll(u, 8 - k, 0), u)
                c = jnp.where(m, c * pltpu.roll(c, 8 - k, 0), c)
            lamv = u + c * lc
            l_scr[pl.ds(off, 8), :] = lamv
            return jnp.broadcast_to(lamv[0:1, :], (8, D_MODEL))

        lamcar[...] = lax.fori_loop(0, nblk, blk, lamcar[...])

        hprev = jnp.where(rowt == 0, hh_ref[...][7:8, :] * keep, pltpu.roll(hv, 1, 0))
        for n in range(LRU_BLOCKS):
            sl = slice(n * LRU_BW, (n + 1) * LRU_BW)
            lamv = l_scr[:, sl]
            xcs = xc_scr[:, sl]
            r = r_scr[:, sl]
            i = i_scr[:, sl]
            a = a_scr[:, sl]
            mult = m_scr[:, sl]
            d_la = lamv * hprev[:, sl] * a - (lamv * i * xcs) * (a * a * rm_scr[:, sl])
            d_pr = d_la * (-LRU_C * sp[:, sl]) * r * (1.0 - r)
            d_pi = (lamv * mult * xcs) * i * (1.0 - i)
            gsm_ref[7:8, sl] += jnp.sum(d_la * r, axis=0, keepdims=True) * (-LRU_C) * dsp[:, sl]
            gsm_ref[5:6, sl] += jnp.sum(d_pr, axis=0, keepdims=True)
            gsm_ref[6:7, sl] += jnp.sum(d_pi, axis=0, keepdims=True)
            xb = xcs.astype(BF16)
            prb = d_pr.astype(BF16)
            pib = d_pi.astype(BF16)
            gwa_ref[n] += _dot(xb, prb, TN)
            gwx_ref[n] += _dot(xb, pib, TN)
            dxc_scr[:, sl] = lamv * mult * i + _dot(prb, wa_ref[n], NT) + _dot(pib, wx_ref[n], NT)

        dxc = dxc_scr[...]
        gsm_ref[4:5, :] += jnp.sum(dxc, axis=0, keepdims=True)
        last8 = lx[tile - 8:tile, :]
        first8 = dxc[0:8, :]
        dlx = cw_ref[3:4, :] * dxc
        gsm_ref[3:4, :] += jnp.sum(dxc * lx, axis=0, keepdims=True)
        fix = jnp.zeros((8, D_MODEL), F32)
        for j in (1, 2, 3):
            w = cw_ref[3 - j:4 - j, :]
            dlx = dlx + w * pltpu.roll(dxc, tile - j, 0)
            fix = fix + w * jnp.where(row8 + j >= 8,
                                      pltpu.roll(dxc8[...], 8 - j, 0) - pltpu.roll(first8, 8 - j, 0), 0.0)
            halo = jnp.where(row8 < j, pltpu.roll(prev8, j, 0) - pltpu.roll(last8, j, 0), 0.0)
            gsm_ref[3 - j:4 - j, :] += (jnp.sum(dxc * pltpu.roll(lx, j, 0), axis=0, keepdims=True)
                                        + jnp.sum(first8 * halo, axis=0, keepdims=True))
        dxc8[...] = first8
        dp_ref[:, 0:D_MODEL] = dlx.astype(BF16)
        top = tile - 8
        dp_ref[top:tile, 0:D_MODEL] = (dlx[top:tile, :] + fix).astype(BF16)

    rev = lambda t: (nt - 1 - t, 0)
    halo_idx = lambda t: (jnp.maximum((nt - 1 - t) * t8 - 1, 0), 0)
    full = lambda shape: pl.BlockSpec(shape, lambda t: (0,) * len(shape))
    big = lambda: pltpu.VMEM((tile, D_MODEL), F32)
    return pl.pallas_call(
        body, name="lru_bwd", grid=(nt,),
        in_specs=[pl.BlockSpec((tile, D_MODEL), rev),
                  pl.BlockSpec((8, D_MODEL), halo_idx),
                  pl.BlockSpec((tile, D_MODEL), lambda t: (nt - 1 - t, 1)),
                  pl.BlockSpec((tile, D_MODEL), rev),
                  pl.BlockSpec((8, D_MODEL), halo_idx),
                  pl.BlockSpec((tile, D_MODEL), rev),
                  full((4, D_MODEL)), full((1, D_MODEL)),
                  full((LRU_BLOCKS, LRU_BW, LRU_BW)), full((LRU_BLOCKS, LRU_BW, LRU_BW)),
                  full((1, D_MODEL)), full((1, D_MODEL)), full((1, D_MODEL))],
        out_specs=[pl.BlockSpec((tile, 2 * D_MODEL), rev),
                   full((LRU_BLOCKS, LRU_BW, LRU_BW)), full((LRU_BLOCKS, LRU_BW, LRU_BW)),
                   full((8, D_MODEL))],
        out_shape=[SDS((seq, D_IN), BF16), SDS((LRU_BLOCKS, LRU_BW, LRU_BW), F32),
                   SDS((LRU_BLOCKS, LRU_BW, LRU_BW), F32), SDS((8, D_MODEL), F32)],
        scratch_shapes=[pltpu.VMEM((8, D_MODEL), F32), pltpu.VMEM((8, D_MODEL), F32),
                        pltpu.VMEM((8, D_MODEL), F32)] + [big() for _ in range(9)],
        compiler_params=_cparams(("arbitrary",)),
    )(p, p, p, h, h, dymix, conv_w, conv_b, wa, wx, ba, bx, lam)


def _tri_matmul(tri, g):
    hi = g.astype(BF16)
    lo = (g - hi.astype(F32)).astype(BF16)
    return _dot(tri, lo) + _dot(tri, hi)


def _hgrn_gate_terms(q, fr, lbl):
    lb = _sigmoid_pos(lbl[0:1, :] - lbl[1:2, :])
    sig = _sigmoid(fr)
    f = lb + (1.0 - lb) * sig
    sq = _sigmoid(q)
    return lb, sig, f, sq


def _hgrn_decay(bh):
    zero = jnp.zeros((1, bh.shape[1]), F32)
    rho = [zero] + [bh[s * SUB - 1:s * SUB, :] for s in range(1, N_SUB + 1)]
    start = _sub_rows(rho[0:N_SUB])
    end = _sub_rows(rho[1:N_SUB + 1])
    mid = 0.5 * (start + end)
    blast = rho[N_SUB]
    e_on = jnp.exp(jnp.minimum(bh - start, 0.0))
    e_off = jnp.exp(jnp.minimum(end - bh, 0.0))
    scales = [_sub_rows([jnp.exp(rho[i] - rho[j + 1]) if i > j else zero for i in range(N_SUB)])
              for j in range(N_SUB - 1)]
    return dict(eq0=jnp.exp(jnp.minimum(bh - mid, EXP_CLAMP)), ek0=jnp.exp(jnp.minimum(mid - bh, EXP_CLAMP)),
                e_on=e_on, e_off=e_off, scales=scales,
                eb=e_on * _sub_rows([jnp.exp(r) for r in rho[0:N_SUB]]),
                ekst=e_off * _sub_rows([jnp.exp(blast - r) for r in rho[1:N_SUB + 1]]),
                ebl=jnp.exp(blast))


def _sub_rows(vecs):
    return jnp.concatenate([jnp.broadcast_to(v, (SUB, v.shape[1])) for v in vecs], axis=0)


def _hgrn_operands(qs, k, dec, qt_scr, kt_scr):
    sub = jnp.right_shift(_iota(qs.shape, 0), 5)
    qon = qs * dec["e_on"]
    koff = k * dec["e_off"]
    qt_scr[0] = (qs * dec["eq0"]).astype(BF16)
    kt_scr[0] = (k * dec["ek0"]).astype(BF16)
    for j in range(N_SUB - 1):
        qt_scr[j + 1] = (qon * dec["scales"][j]).astype(BF16)
        kt_scr[j + 1] = jnp.where(sub == j, koff, 0.0).astype(BF16)
    return koff


def _hgrn_head_scores(qt_scr, kt_scr, sl, diag):
    a = jnp.where(diag, _dot(qt_scr[0, :, sl], kt_scr[0, :, sl], NT), 0.0)
    for j in range(1, N_SUB):
        a = a + _dot(qt_scr[j, :, sl], kt_scr[j, :, sl], NT)
    return a


def _hgrn_forward(p, lbl, gw, seq):
    nc = seq // CHUNK
    assert SUB == 32

    def body(q_ref, f_ref, v_ref, hg_ref, lbl_ref, gw_ref, y_ref, o_ref, st_ref,
             s_scr, qt_scr, kt_scr, qin_scr, kst_scr, vb_scr, a_scr):
        @pl.when(pl.program_id(0) == 0)
        def _():
            s_scr[...] = jnp.zeros_like(s_scr)

        r = _iota((CHUNK, CHUNK), 0)
        c = _iota((CHUNK, CHUNK), 1)
        tri = jnp.where(c <= r, 1.0, 0.0).astype(BF16)
        diag = (jnp.right_shift(r, 5) == jnp.right_shift(c, 5)) & (c <= r)
        heads = [slice(h * HEAD_D, (h + 1) * HEAD_D) for h in range(N_HEADS)]
        for cc in range(HGRN_STEP_CHUNKS):
            rows = slice(cc * CHUNK, (cc + 1) * CHUNK)
            q = q_ref[rows, :]
            _, _, f, sq = _hgrn_gate_terms(q, f_ref[rows, :], lbl_ref[...])
            qs = q * sq
            k = 1.0 - f
            dec = _hgrn_decay(_tri_matmul(tri, jnp.log(f)))
            _hgrn_operands(qs, k, dec, qt_scr, kt_scr)
            qin_scr[...] = (qs * dec["eb"]).astype(BF16)
            kst_scr[...] = (k * dec["ekst"]).astype(BF16)
            vb_scr[...] = v_ref[rows, :].astype(BF16)
            ebl = dec["ebl"]
            hg = hg_ref[rows, :]
            gate = gw_ref[...] * (hg * _sigmoid(hg))
            stb = []
            for h, sl in enumerate(heads):
                st = s_scr[h]
                st_ref[cc, h] = st
                stb.append(st.astype(BF16))
                s_scr[h] = st * ebl[:, sl] + _dot(vb_scr[:, sl], kst_scr[:, sl], TN)
            for h, sl in enumerate(heads):
                a_scr[h] = _hgrn_head_scores(qt_scr, kt_scr, sl, diag).astype(BF16)
            for h, sl in enumerate(heads):
                o = _dot(a_scr[h], vb_scr[:, sl]) + _dot(qin_scr[:, sl], stb[h], NT)
                o_ref[rows, sl] = o
                rs = lax.rsqrt(jnp.mean(o * o, axis=-1, keepdims=True) + EPS)
                y_ref[rows, sl] = ((o * rs) * gate[:, sl]).astype(BF16)

    col = lambda j: pl.BlockSpec((HGRN_STEP_CHUNKS * CHUNK, D_MODEL), lambda c: (c, j))
    par = lambda rows: pl.BlockSpec((rows, D_MODEL), lambda c: (0, 0))
    return pl.pallas_call(
        body, name="hgrn_fwd", grid=(nc // HGRN_STEP_CHUNKS,),
        in_specs=[col(2), col(3), col(4), col(5), par(2), par(1)],
        out_specs=[col(0), col(0),
                   pl.BlockSpec((HGRN_STEP_CHUNKS, N_HEADS, HEAD_D, HEAD_D), lambda c: (c, 0, 0, 0))],
        out_shape=[SDS((seq, D_MODEL), BF16), SDS((seq, D_MODEL), F32),
                   SDS((nc, N_HEADS, HEAD_D, HEAD_D), F32)],
        scratch_shapes=[pltpu.VMEM((N_HEADS, HEAD_D, HEAD_D), F32),
                        pltpu.VMEM((N_SUB, CHUNK, D_MODEL), BF16), pltpu.VMEM((N_SUB, CHUNK, D_MODEL), BF16)]
                       + [pltpu.VMEM((CHUNK, D_MODEL), BF16)] * 3 + [pltpu.VMEM((N_HEADS, CHUNK, CHUNK), BF16)],
        compiler_params=_cparams(("arbitrary",)),
    )(p, p, p, p, lbl, gw)


def _hgrn_backward(p, o, states, dymix, lbl, gw, dp_full, g_w_out, g_small, seq):
    step_rows = HGRN_STEP_CHUNKS * CHUNK
    ns = seq // step_rows

    def body(q_ref, f_ref, v_ref, hg_ref, o_ref, st_ref, dy_ref, lbl_ref, gw_ref, dpin_ref, go_ref, gs_ref,
             dpo_ref, gsm_ref, ro_ref, rs_ref, ds_scr, dp_buf, dp_sems, *rest):
        del dpin_ref
        scratch, sems = rest[:14], rest[14:]
        step = pl.program_id(0)
        slot = step % 2
        exs = [_SlotExchange(go_ref, ro_ref, *sems[0:3], blocked=True),
               _SlotExchange(gs_ref, rs_ref, *sems[3:6], blocked=True)]

        @pl.when(step == 0)
        def _():
            for ex in exs:
                ex.start()

        def out_copy(s, blk):
            rows = pl.ds(pl.multiple_of(blk * step_rows, step_rows), step_rows)
            return pltpu.make_async_copy(dp_buf.at[s], dpo_ref.at[rows, pl.ds(2 * D_MODEL, 4 * D_MODEL)],
                                         dp_sems.at[s])

        @pl.when(step == 0)
        def _():
            ds_scr[...] = jnp.zeros_like(ds_scr)
            gsm_ref[...] = jnp.zeros_like(gsm_ref)

        @pl.when(step >= 2)
        def _():
            out_copy(slot, ns + 1 - step).wait()

        for cc in reversed(range(HGRN_STEP_CHUNKS)):
            chunk(cc, q_ref, f_ref, v_ref, hg_ref, o_ref, st_ref, dy_ref, lbl_ref, gw_ref, gsm_ref, ds_scr,
                  dp_buf.at[slot], *scratch)

        out_copy(slot, ns - 1 - step).start()

        @pl.when(step == ns - 1)
        def _():
            out_copy(1 - slot, 1).wait()
            out_copy(slot, 0).wait()
            for ex in exs:
                ex.wait()

    def chunk(cc, q_ref, f_ref, v_ref, hg_ref, o_ref, st_ref, dy_ref, lbl_ref, gw_ref, gsm_ref, ds_scr, dp_ref,
              qt_scr, kt_scr, qin_scr, kst_scr, vb_scr, dob_scr, g_scr, h_scr, dqi_scr, dks_scr, sd_scr,
              a_scr, da_scr, da0_scr):
        rows = slice(cc * CHUNK, (cc + 1) * CHUNK)
        r = _iota((CHUNK, CHUNK), 0)
        c = _iota((CHUNK, CHUNK), 1)
        tri = jnp.where(c <= r, 1.0, 0.0).astype(BF16)
        triu = jnp.where(c >= r, 1.0, 0.0).astype(BF16)
        diag = (jnp.right_shift(r, 5) == jnp.right_shift(c, 5)) & (c <= r)
        row = _iota((CHUNK, D_MODEL), 0)
        sub = jnp.right_shift(row, 5)

        q = q_ref[rows, :]
        lb, sig, f, sq = _hgrn_gate_terms(q, f_ref[rows, :], lbl_ref[...])
        qs = q * sq
        k = 1.0 - f
        dec = _hgrn_decay(_tri_matmul(tri, jnp.log(f)))
        eb, ekst, ebl = dec["eb"], dec["ekst"], dec["ebl"]
        koff = _hgrn_operands(qs, k, dec, qt_scr, kt_scr)
        qin_scr[...] = (qs * eb).astype(BF16)
        kst_scr[...] = (k * ekst).astype(BF16)
        vb_scr[...] = v_ref[rows, :].astype(BF16)
        hg = hg_ref[rows, :]
        sh = _sigmoid(hg)
        dy = dy_ref[rows, :]
        gwv = gw_ref[...]
        d_onw = dy * (hg * sh)
        d_on = d_onw * gwv
        d_gate = dy * gwv * (sh * (1.0 + hg * (1.0 - sh)))

        heads = [slice(h * HEAD_D, (h + 1) * HEAD_D) for h in range(N_HEADS)]
        for h, sl in enumerate(heads):
            o = o_ref[rows, sl]
            rs = lax.rsqrt(jnp.mean(o * o, axis=-1, keepdims=True) + EPS)
            on = o * rs
            dp_ref[rows, 3 * D_MODEL + h * HEAD_D:3 * D_MODEL + (h + 1) * HEAD_D] = (d_gate[:, sl] * on).astype(BF16)
            gsm_ref[1:2, sl] += jnp.sum(d_onw[:, sl] * on, axis=0, keepdims=True)
            d_onh = d_on[:, sl]
            dob_scr[:, sl] = (rs * (d_onh - on * jnp.mean(d_onh * on, axis=-1, keepdims=True))).astype(BF16)
        for h, sl in enumerate(heads):
            a_scr[h] = _hgrn_head_scores(qt_scr, kt_scr, sl, diag).astype(BF16)
            da = _dot(dob_scr[:, sl], vb_scr[:, sl], NT)
            da_scr[h] = da.astype(BF16)
            da0_scr[h] = jnp.where(diag, da, 0.0).astype(BF16)
        for h, sl in enumerate(heads):
            st = st_ref[cc, h]
            dst = ds_scr[h]
            dstb = dst.astype(BF16)
            dp_ref[rows, 2 * D_MODEL + h * HEAD_D:2 * D_MODEL + (h + 1) * HEAD_D] = (
                _dot(a_scr[h], dob_scr[:, sl], TN) + _dot(kst_scr[:, sl], dstb, NT)).astype(BF16)
            dqi_scr[:, sl] = _dot(dob_scr[:, sl], st.astype(BF16))
            dks_scr[:, sl] = _dot(vb_scr[:, sl], dstb)
            sd_scr[0:1, sl] = jnp.sum(st * dst, axis=0, keepdims=True)
            ds_scr[h] = dst * ebl[:, sl] + _dot(dob_scr[:, sl], qin_scr[:, sl], TN)
        for h, sl in enumerate(heads):
            g_scr[0, :, sl] = _dot(da0_scr[h], kt_scr[0, :, sl])
            h_scr[0, :, sl] = _dot(da0_scr[h], qt_scr[0, :, sl], TN)
            for j in range(1, N_SUB):
                g_scr[j, :, sl] = _dot(da_scr[h], kt_scr[j, :, sl])
                h_scr[j, :, sl] = _dot(da_scr[h], qt_scr[j, :, sl], TN)

        g0 = g_scr[0]
        h0 = h_scr[0]
        dq_inter = eb * dqi_scr[...]
        d_kst = ekst * dks_scr[...]
        db = qs * dq_inter - k * d_kst + qt_scr[0].astype(F32) * g0 - kt_scr[0].astype(F32) * h0
        gq = jnp.zeros((CHUNK, D_MODEL), F32)
        hsel = jnp.zeros((CHUNK, D_MODEL), F32)
        for j in range(N_SUB - 1):
            gj = g_scr[j + 1]
            gq = gq + dec["scales"][j] * gj
            db = db + qt_scr[j + 1].astype(F32) * gj
            hsel = jnp.where(sub == j, h_scr[j + 1], hsel)
        db = db - koff.astype(BF16).astype(F32) * hsel
        d_q = dec["eq0"] * g0 + dec["e_on"] * gq + dq_inter
        d_k = dec["ek0"] * h0 + dec["e_off"] * hsel + d_kst
        db_last = jnp.sum(k * d_kst, axis=0, keepdims=True) + ebl * sd_scr[0:1, :]
        db = db + jnp.where(row == CHUNK - 1, db_last, 0.0)
        dg = _tri_matmul(triu, db)
        d_f = dg / f - d_k
        dp_ref[rows, D_MODEL:2 * D_MODEL] = (d_f * (1.0 - lb) * sig * (1.0 - sig)).astype(BF16)
        gsm_ref[0:1, :] += jnp.sum(d_f * (1.0 - sig), axis=0, keepdims=True) * (lb * (1.0 - lb))
        dp_ref[rows, 0:D_MODEL] = (d_q * (sq * (1.0 + q * (1.0 - sq)))).astype(BF16)

    rc = lambda c: ns - 1 - c
    col = lambda j: pl.BlockSpec((step_rows, D_MODEL), lambda c: (rc(c), j))
    par = lambda rows: pl.BlockSpec((rows, D_MODEL), lambda c: (0, 0))
    return pl.pallas_call(
        body, name="hgrn_bwd", grid=(ns,),
        in_specs=[col(2), col(3), col(4), col(5), col(0),
                  pl.BlockSpec((HGRN_STEP_CHUNKS, N_HEADS, HEAD_D, HEAD_D), lambda c: (rc(c), 0, 0, 0)),
                  col(1), par(2), par(1), ANY, ANY, ANY],
        out_specs=[ANY, par(8), ANY, ANY],
        out_shape=[SDS((seq, D_IN), BF16), SDS((8, D_MODEL), F32), SDS(g_w_out.shape, F32),
                   SDS(g_small.shape, F32)],
        input_output_aliases={9: 0},
        scratch_shapes=[pltpu.VMEM((N_HEADS, HEAD_D, HEAD_D), F32),
                        pltpu.VMEM((2, step_rows, 4 * D_MODEL), BF16), pltpu.SemaphoreType.DMA((2,)),
                        pltpu.VMEM((N_SUB, CHUNK, D_MODEL), BF16), pltpu.VMEM((N_SUB, CHUNK, D_MODEL), BF16)]
                       + [pltpu.VMEM((CHUNK, D_MODEL), BF16)] * 4
                       + [pltpu.VMEM((N_SUB, CHUNK, D_MODEL), F32)] * 2 + [pltpu.VMEM((CHUNK, D_MODEL), F32)] * 2
                       + [pltpu.VMEM((8, D_MODEL), F32)] + [pltpu.VMEM((N_HEADS, CHUNK, CHUNK), BF16)] * 3
                       + EXCHANGE_SEMS * 2,
        compiler_params=_cparams(("arbitrary",)),
    )(p, p, p, p, o, states, dymix, lbl, gw, dp_full, g_w_out, g_small)


def _out_proj(yl, yh, wo, x, tgt, post_w, seq):
    tm = 512

    def body(yl_ref, yh_ref, wo_ref, x_ref, tg_ref, pw_ref, dymix_ref, dout_ref, gwo_ref, st_ref):
        @pl.when(pl.program_id(0) == 0)
        def _():
            gwo_ref[...] = jnp.zeros_like(gwo_ref)
            st_ref[...] = jnp.zeros_like(st_ref)

        ylv = yl_ref[...]
        yhv = yh_ref[...]
        y = _dot(ylv, wo_ref[0:D_MODEL, :]) + _dot(yhv, wo_ref[D_MODEL:D_MIX, :])
        r2 = lax.rsqrt(jnp.mean(y * y, axis=-1, keepdims=True) + EPS)
        yn = y * r2
        pw = pw_ref[...]
        e = (x_ref[...] + yn * pw) - tg_ref[...]
        st_ref[1:2, :] += jnp.sum(e * e, axis=0, keepdims=True) * (0.5 / D_MODEL)
        dout = e * (1.0 / D_MODEL)
        dout_ref[...] = dout
        st_ref[0:1, :] += jnp.sum(dout * yn, axis=0, keepdims=True)
        dyn = dout * pw
        dy = r2 * (dyn - yn * jnp.mean(dyn * yn, axis=-1, keepdims=True))
        dyb = dy.astype(BF16)
        dymix_ref[...] = _dot(dyb, wo_ref[...], NT)
        gwo_ref[0:D_MODEL, :] += _dot(ylv, dyb, TN)
        gwo_ref[D_MODEL:D_MIX, :] += _dot(yhv, dyb, TN)

    row = lambda w: pl.BlockSpec((tm, w), lambda m: (m, 0))
    full = lambda shape: pl.BlockSpec(shape, lambda m: (0,) * len(shape))
    once = lambda shape: pl.BlockSpec(shape, lambda m: (0,) * len(shape), pipeline_mode=pl.Buffered(1))
    return pl.pallas_call(
        body, name="out_proj", grid=(seq // tm,),
        in_specs=[row(D_MODEL), row(D_MODEL), once((D_MIX, D_MODEL)), row(D_MODEL), row(D_MODEL),
                  full((1, D_MODEL))],
        out_specs=[row(D_MIX), row(D_MODEL), once((D_MIX, D_MODEL)), full((8, D_MODEL))],
        out_shape=[SDS((seq, D_MIX), F32), SDS((seq, D_MODEL), F32), SDS((D_MIX, D_MODEL), F32),
                   SDS((8, D_MODEL), F32)],
        compiler_params=_cparams(("arbitrary",)),
    )(yl, yh, wo, x, tgt, post_w)


MESH = pl.DeviceIdType.MESH
ANY = pl.BlockSpec(memory_space=pl.ANY)
EXCHANGE_SEMS = [pltpu.SemaphoreType.DMA((N_DEV - 1,)), pltpu.SemaphoreType.DMA((N_DEV - 1,)),
                 pltpu.SemaphoreType.DMA(())]


def _mesh_pos():
    return lax.axis_index("x"), lax.axis_index("y"), lax.axis_index("c")


class _SlotExchange:
    def __init__(self, src_ref, dst_ref, send_sems, recv_sems, local_sem, blocked):
        x, y, c = _mesh_pos()
        me = 4 * x + 2 * y + c
        src = (lambda dest: src_ref.at[dest]) if blocked else (lambda dest: src_ref)
        self.local = pltpu.make_async_copy(src(me), dst_ref.at[me], local_sem)
        self.sends, self.recvs = [], []
        for k in range(1, N_DEV):
            px = 1 - x if (k >> 2) & 1 else x
            py = 1 - y if (k >> 1) & 1 else y
            pc = 1 - c if k & 1 else c
            peer = 4 * px + 2 * py + pc
            sems = dict(send_sem=send_sems.at[k - 1], recv_sem=recv_sems.at[k - 1],
                        device_id=(px, py, pc), device_id_type=MESH)
            self.sends.append(pltpu.make_async_remote_copy(src_ref=src(peer), dst_ref=dst_ref.at[me], **sems))
            self.recvs.append(pltpu.make_async_remote_copy(src_ref=dst_ref.at[peer], dst_ref=dst_ref.at[peer], **sems))

    def start(self):
        self.local.start()
        for cp in self.sends:
            cp.start()

    def wait(self):
        for cp in self.recvs:
            cp.wait_recv()
        for cp in self.sends:
            cp.wait_send()
        self.local.wait()


class _ChipExchange:
    def __init__(self, src_ref, dst_ref, send_sems, recv_sems, local_sem):
        x, y, c = _mesh_pos()
        chip = 2 * x + y
        self.local = pltpu.make_async_copy(src_ref.at[chip], dst_ref.at[chip], local_sem)
        self.sends, self.recvs = [], []
        for k in range(1, N_CHIPS):
            px = 1 - x if (k >> 1) & 1 else x
            py = 1 - y if k & 1 else y
            peer = 2 * px + py
            sems = dict(send_sem=send_sems.at[k - 1], recv_sem=recv_sems.at[k - 1],
                        device_id=(px, py, c), device_id_type=MESH)
            self.sends.append(pltpu.make_async_remote_copy(src_ref=src_ref.at[peer], dst_ref=dst_ref.at[chip], **sems))
            self.recvs.append(pltpu.make_async_remote_copy(src_ref=dst_ref.at[peer], dst_ref=dst_ref.at[peer], **sems))

    def start(self):
        self.local.start()
        for cp in self.sends:
            cp.start()

    def wait(self):
        for cp in self.recvs:
            cp.wait_recv()
        for cp in self.sends:
            cp.wait_send()
        self.local.wait()


GRAD_W_IN_TK = 2048


def _grad_w_in_sibling(u, dp, core, seq):
    tk = min(GRAD_W_IN_TK, seq)
    nk = seq // tk

    def body(core_ref, u_ref, dp_ref, g_ref):
        del core_ref

        @pl.when(pl.program_id(1) == 0)
        def _():
            g_ref[...] = jnp.zeros_like(g_ref)

        g_ref[0] += _dot(u_ref[...], dp_ref[...], TN)

    return pl.pallas_call(
        body, name="grad_w_in_sibling",
        grid_spec=pltpu.PrefetchScalarGridSpec(
            num_scalar_prefetch=1, grid=(N_CHIPS, nk),
            in_specs=[pl.BlockSpec((tk, D_MODEL), lambda n, k, c: (k, 0)),
                      pl.BlockSpec((tk, W_BLK), lambda n, k, c: (k, 2 * n + 1 - c[0]))],
            out_specs=pl.BlockSpec((1, D_MODEL, W_BLK), lambda n, k, c: (n, 0, 0))),
        out_shape=SDS((N_CHIPS, D_MODEL, W_BLK), F32),
        compiler_params=_cparams(("parallel", "arbitrary")),
    )(core, u, dp)


def _grad_w_in_own(u, dp, core, g_sib, seq):
    tk = min(GRAD_W_IN_TK, seq)
    nk = seq // tk

    def body(core_ref, u_ref, dp_ref, gsib_ref, g_ref, land, send_sem, recv_sem):
        del core_ref
        n = pl.program_id(0)
        k = pl.program_id(1)
        x, y, c = _mesh_pos()
        swap = pltpu.make_async_remote_copy(src_ref=gsib_ref, dst_ref=land, send_sem=send_sem, recv_sem=recv_sem,
                                            device_id=(x, y, 1 - c), device_id_type=MESH)

        @pl.when((n == 0) & (k == 0))
        def _():
            swap.start()

        @pl.when(k == 0)
        def _():
            g_ref[...] = jnp.zeros_like(g_ref)

        g_ref[0] += _dot(u_ref[...], dp_ref[...], TN)

        @pl.when((n == 0) & (k == nk - 1))
        def _():
            swap.wait_recv()

        @pl.when(k == nk - 1)
        def _():
            g_ref[0] += land[n]

        @pl.when((n == N_CHIPS - 1) & (k == nk - 1))
        def _():
            swap.wait_send()

    return pl.pallas_call(
        body, name="grad_w_in_own",
        grid_spec=pltpu.PrefetchScalarGridSpec(
            num_scalar_prefetch=1, grid=(N_CHIPS, nk),
            in_specs=[pl.BlockSpec((tk, D_MODEL), lambda n, k, c: (k, 0)),
                      pl.BlockSpec((tk, W_BLK), lambda n, k, c: (k, 2 * n + c[0])), ANY],
            out_specs=pl.BlockSpec((1, D_MODEL, W_BLK), lambda n, k, c: (n, 0, 0)),
            scratch_shapes=[pltpu.VMEM((N_CHIPS, D_MODEL, W_BLK), F32), pltpu.SemaphoreType.DMA(()),
                            pltpu.SemaphoreType.DMA(())]),
        out_shape=SDS((N_CHIPS, D_MODEL, W_BLK), F32),
        compiler_params=_cparams(("arbitrary", "arbitrary")),
    )(core, u, dp, g_sib)


def _grad_x(dp, w_all, x, pre_w, dout, g_chip, seq):
    tm = 512
    nm = seq // tm

    def body(dp_ref, w_ref, x_ref, pw_ref, do_ref, gsrc_ref, gx_ref, gpw_ref, recv_ref,
             send_sems, recv_sems, local_sem):
        m = pl.program_id(0)
        ex = _ChipExchange(gsrc_ref, recv_ref, send_sems, recv_sems, local_sem)

        @pl.when(m == 0)
        def _():
            ex.start()
            gpw_ref[...] = jnp.zeros_like(gpw_ref)

        du = _dot(dp_ref[:, 0:W_BLK], w_ref[0], NT)
        for j in range(1, N_DEV):
            du = du + _dot(dp_ref[:, j * W_BLK:(j + 1) * W_BLK], w_ref[j], NT)
        xv = x_ref[...]
        r1 = lax.rsqrt(jnp.mean(xv * xv, axis=-1, keepdims=True) + EPS)
        xn = xv * r1
        gpw_ref[0:1, :] += jnp.sum(du * xn, axis=0, keepdims=True)
        dxn = du * pw_ref[...]
        gx_ref[...] = r1 * (dxn - xn * jnp.mean(dxn * xn, axis=-1, keepdims=True)) + do_ref[...]

        @pl.when(m == nm - 1)
        def _():
            ex.wait()

    row = lambda w: pl.BlockSpec((tm, w), lambda m: (m, 0))
    return pl.pallas_call(
        body, name="grad_x", grid=(nm,),
        in_specs=[row(D_IN), pl.BlockSpec((N_DEV, D_MODEL, W_BLK), lambda m: (0, 0, 0), pipeline_mode=pl.Buffered(1)),
                  row(D_MODEL),
                  pl.BlockSpec((1, D_MODEL), lambda m: (0, 0)), row(D_MODEL), ANY],
        out_specs=[row(D_MODEL), pl.BlockSpec((8, D_MODEL), lambda m: (0, 0)), ANY],
        out_shape=[SDS((seq, D_MODEL), F32), SDS((8, D_MODEL), F32), SDS(g_chip.shape, F32)],
        scratch_shapes=[pltpu.SemaphoreType.DMA((N_CHIPS - 1,)), pltpu.SemaphoreType.DMA((N_CHIPS - 1,)),
                        pltpu.SemaphoreType.DMA(())],
        compiler_params=_cparams(("arbitrary",)),
    )(dp, w_all, x, pre_w, dout, g_chip)


def _local_step(x, tgt, p, u, conv_w, conv_b, wa, wx, ba, bx, lam, lbl, gnorm_w, w_out, post_w):
    seq = x.shape[0]
    h, y_lru = _lru_forward(p, conv_w, conv_b, wa, wx, ba, bx, lam, seq)
    y_hgrn, o, states = _hgrn_forward(p, lbl, gnorm_w, seq)
    dymix, dout, g_w_out, stats = _out_proj(y_lru, y_hgrn, w_out, x, tgt, post_w, seq)
    dp_lru, g_wa, g_wx, ls = _lru_backward(p, h, dymix, conv_w, conv_b, wa, wx, ba, bx, lam, seq)
    g_small = _pack_small(_shard_rows(g_wa, LRU_BLOCKS), _shard_rows(g_wx, LRU_BLOCKS),
                          _shard_rows(ls[0:4].reshape(4, D_MODEL, 1), 4).reshape(N_DEV, 4, 128),
                          _shard_rows(ls[5].reshape(4, LRU_BW, 1), 4).reshape(N_DEV, 4, 32),
                          _shard_rows(ls[6].reshape(4, LRU_BW, 1), 4).reshape(N_DEV, 4, 32))
    dp, hgrn_small, r_out, r_small = _hgrn_backward(
        p, o, states, dymix, lbl, gnorm_w, dp_lru, g_w_out.reshape(N_DEV, D_MIX // N_DEV, D_MODEL), g_small, seq)
    return dict(u=u, dp=dp, dout=dout, r_out=r_out, r_small=r_small,
                lru_small=ls, hgrn_small=hgrn_small, stats=stats)


class _TwoLevelGather:
    def __init__(self, ins, outs, send_sems, recv_sems, local_sems):
        self.ins, self.outs = ins, outs
        self.send_sems, self.recv_sems, self.local_sems = send_sems, recv_sems, local_sems
        x, y, c = _mesh_pos()
        self.c = c
        self.me, self.sibling = (x, y, c), (x, y, 1 - c)
        self.chips = [(1 - x, y), (x, 1 - y), (1 - x, 1 - y)]
        n = len(ins)
        self.mine = [pltpu.make_async_copy(ins[a], self._slot(a, self.me), local_sems.at[a]) for a in range(n)]
        self.first = []
        for a in range(n):
            self.first.append(self._copy(a, 0, self.me, self.sibling, src=ins[a]))
            self.first += [self._copy(a, 1 + j, self.me, (*chip, c), src=ins[a])
                           for j, chip in enumerate(self.chips)]
        self.passed = [self._copy(a, 4 + j, (*chip, c), self.sibling)
                       for j, chip in enumerate(self.chips) for a in range(n)]

    def _slot(self, a, pos):
        return self.outs[a].at[4 * pos[0] + 2 * pos[1] + pos[2]]

    def _copy(self, a, k, block, to, src=None):
        dst = self._slot(a, block)
        return pltpu.make_async_remote_copy(
            src_ref=dst if src is None else src, dst_ref=dst,
            send_sem=self.send_sems.at[a, k], recv_sem=self.recv_sems.at[a, k],
            device_id=to, device_id_type=MESH)

    def start(self):
        for cp in self.mine + self.first:
            cp.start()

    def forward(self):
        n = len(self.ins)
        for j, chip in enumerate(self.chips):
            for a in range(n):
                self._copy(a, 1 + j, (*chip, self.c), self.me).wait_recv()
                self.passed[j * n + a].start()

    def finish(self):
        for a in range(len(self.ins)):
            self._copy(a, 0, self.sibling, self.me).wait_recv()
            for j, chip in enumerate(self.chips):
                self._copy(a, 4 + j, (*chip, 1 - self.c), self.me).wait_recv()
        for cp in self.first + self.passed:
            cp.wait_send()
        for cp in self.mine:
            cp.wait()


W_IN_DIRECT = (1, 2, 4, 6)
W_IN_PASSED = (2, 4, 6)


def _in_proj_gather(x, pre_w, w_in_blk, w_out_blk, small_blk, me, seq):
    tm = min(1024, seq)
    nm = seq // tm
    last = N_DEV - 1

    def body(me_ref, x_ref, pw_ref, wblk_ref, woblk_ref, smblk_ref,
             p_ref, u_ref, wall_ref, woall_ref, small_ref,
             u_all, w_vmem, own_sem, d_send, d_recv, f_send, f_recv, wb_sems, g_send, g_recv, g_local):
        i = pl.program_id(0)
        m = pl.program_id(1)
        idx = me_ref[0]
        x_, y_, c_ = _mesh_pos()
        aux = _TwoLevelGather([woblk_ref, smblk_ref], [woall_ref, small_ref], g_send, g_recv, g_local)

        def peer(k):
            return (1 - x_ if (k >> 2) & 1 else x_, 1 - y_ if (k >> 1) & 1 else y_, 1 - c_ if k & 1 else c_)

        def direct(k):
            f = W_IN_DIRECT.index(k)
            return (pltpu.make_async_remote_copy(src_ref=wblk_ref, dst_ref=w_vmem.at[idx], send_sem=d_send.at[f],
                                                 recv_sem=d_recv.at[f], device_id=peer(k), device_id_type=MESH),
                    pltpu.make_async_remote_copy(src_ref=w_vmem.at[idx ^ k], dst_ref=w_vmem.at[idx ^ k],
                                                 send_sem=d_send.at[f], recv_sem=d_recv.at[f], device_id=peer(k),
                                                 device_id_type=MESH))

        def passed(k):
            f = W_IN_PASSED.index(k)
            return (pltpu.make_async_remote_copy(src_ref=w_vmem.at[idx ^ k], dst_ref=w_vmem.at[idx ^ k],
                                                 send_sem=f_send.at[f], recv_sem=f_recv.at[f], device_id=peer(1),
                                                 device_id_type=MESH),
                    pltpu.make_async_remote_copy(src_ref=w_vmem.at[idx ^ (k + 1)], dst_ref=w_vmem.at[idx ^ (k + 1)],
                                                 send_sem=f_send.at[f], recv_sem=f_recv.at[f], device_id=peer(1),
                                                 device_id_type=MESH))

        def write_back(k):
            return pltpu.make_async_copy(w_vmem.at[idx ^ k], wall_ref.at[idx ^ k], wb_sems.at[k])

        own = pltpu.make_async_copy(wblk_ref, w_vmem.at[idx], own_sem)

        @pl.when((i == 0) & (m == 0))
        def _():
            own.start()
            for k in W_IN_DIRECT:
                direct(k)[0].start()
            aux.start()
            own.wait()
            write_back(0).start()

        for k in range(1, N_DEV):
            @pl.when((i == k) & (m == 0))
            def _(k=k):
                if k in W_IN_DIRECT:
                    direct(k)[1].wait_recv()
                    if k in W_IN_PASSED:
                        passed(k)[0].start()
                else:
                    passed(k - 1)[1].wait_recv()
                write_back(k).start()

        @pl.when((i == N_CHIPS) & (m == 0))
        def _():
            aux.forward()

        rows = pl.ds(pl.multiple_of(m * tm, tm), tm)

        @pl.when(i == 0)
        def _():
            xv = x_ref[...]
            r = lax.rsqrt(jnp.mean(xv * xv, axis=-1, keepdims=True) + EPS)
            ub = (xv * r * pw_ref[...]).astype(BF16)
            u_all[rows, :] = ub
            u_ref[...] = ub

        p_ref[...] = _dot(u_all[rows, :], w_vmem[idx ^ i])

        @pl.when((i == last) & (m == nm - 1))
        def _():
            for k in W_IN_DIRECT:
                direct(k)[0].wait_send()
            for k in W_IN_PASSED:
                passed(k)[0].wait_send()
            for k in range(N_DEV):
                write_back(k).wait()
            aux.finish()

    first_pass = lambda i, m: jnp.where(i == 0, m, nm - 1)
    return pl.pallas_call(
        body, name="in_proj_gather",
        grid_spec=pltpu.PrefetchScalarGridSpec(
            num_scalar_prefetch=1, grid=(N_DEV, nm),
            in_specs=[pl.BlockSpec((tm, D_MODEL), lambda i, m, me: (first_pass(i, m), 0)),
                      pl.BlockSpec((1, D_MODEL), lambda i, m, me: (0, 0)), ANY, ANY, ANY],
            out_specs=[pl.BlockSpec((tm, W_BLK), lambda i, m, me: (m, me[0] ^ i)),
                       pl.BlockSpec((tm, D_MODEL), lambda i, m, me: (first_pass(i, m), 0)), ANY, ANY, ANY],
            scratch_shapes=[pltpu.VMEM((seq, D_MODEL), BF16), pltpu.VMEM((N_DEV, D_MODEL, W_BLK), BF16),
                            pltpu.SemaphoreType.DMA(()),
                            pltpu.SemaphoreType.DMA((len(W_IN_DIRECT),)), pltpu.SemaphoreType.DMA((len(W_IN_DIRECT),)),
                            pltpu.SemaphoreType.DMA((len(W_IN_PASSED),)), pltpu.SemaphoreType.DMA((len(W_IN_PASSED),)),
                            pltpu.SemaphoreType.DMA((N_DEV,)),
                            pltpu.SemaphoreType.DMA((2, 7)), pltpu.SemaphoreType.DMA((2, 7)),
                            pltpu.SemaphoreType.DMA((2,))]),
        out_shape=[SDS((seq, D_IN), F32), SDS((seq, D_MODEL), BF16), SDS((N_DEV, D_MODEL, W_BLK), BF16),
                   SDS((N_DEV,) + w_out_blk.shape, w_out_blk.dtype), SDS((N_DEV,) + small_blk.shape, small_blk.dtype)],
        compiler_params=_cparams(("arbitrary", "arbitrary")),
    )(me, x, pre_w, w_in_blk, w_out_blk, small_blk)


def _exchange_grads(blocks, repl):
    nb = len(blocks)
    n = nb + 1

    def body(*refs):
        ins, outs, sems = refs[:n], refs[n:2 * n], refs[2 * n:]
        exs = [_SlotExchange(ins[a], outs[a], *sems[3 * a:3 * a + 3], blocked=a < nb) for a in range(n)]
        for ex in exs:
            ex.start()
        for ex in exs:
            ex.wait()

    arrs = list(blocks) + [repl]
    shapes = [SDS(b.shape, b.dtype) for b in blocks] + [SDS((N_DEV,) + repl.shape, repl.dtype)]
    return pl.pallas_call(
        body, name="exchange_small", out_shape=shapes,
        in_specs=[ANY] * n, out_specs=[ANY] * n,
        scratch_shapes=EXCHANGE_SEMS * n,
    )(*arrs)


def _pack_rows(picks, name):
    arrs = [p[0] for p in picks]

    def body(*refs):
        out = refs[-1]
        out[...] = jnp.zeros_like(out)
        at = 0
        for ref, (_, row, rows, scale) in zip(refs[:-1], picks):
            out[at:at + rows, :] = ref[row:row + rows, :] * scale
            at += rows

    return pl.pallas_call(body, name=name, out_shape=SDS((8, D_MODEL), F32))(*arrs)


def _adamw(g, w, m, v):
    m2 = ADAM_B1 * m + (1.0 - ADAM_B1) * g
    v2 = ADAM_B2 * v + (1.0 - ADAM_B2) * (g * g)
    m_hat = m2 / (1.0 - ADAM_B1 ** ADAM_STEP)
    v_hat = v2 / (1.0 - ADAM_B2 ** ADAM_STEP)
    delta = -ADAM_LR * (m_hat / (jnp.sqrt(v_hat) + ADAM_EPS) + ADAM_WD * w)
    return delta, m2, v2


def _sum_slots(r_ref):
    g = r_ref[0]
    for s in range(1, r_ref.shape[0]):
        g = g + r_ref[s]
    return g


def _sum_adamw(recv, w, m, v, tr, name):
    rows, cols = w.shape

    def body(r_ref, w_ref, m_ref, v_ref, g_ref, d_ref, m2_ref, v2_ref):
        g = _sum_slots(r_ref)
        g_ref[...] = g
        d_ref[...], m2_ref[...], v2_ref[...] = _adamw(g, w_ref[...], m_ref[...], v_ref[...])

    blk = pl.BlockSpec((tr, cols), lambda i: (i, 0))
    return pl.pallas_call(
        body, name=name, grid=(rows // tr,),
        in_specs=[pl.BlockSpec((recv.shape[0], tr, cols), lambda i: (0, i, 0)), blk, blk, blk],
        out_specs=[blk] * 4, out_shape=[SDS((rows, cols), F32)] * 4,
        compiler_params=_cparams(("parallel",)),
    )(recv, w, m, v)


def _sum_adamw_repl(recv, w, m, v):
    def body(r_ref, w_ref, m_ref, v_ref, g_ref, d_ref, m2_ref, v2_ref, loss_ref):
        g = _sum_slots(r_ref)
        g_ref[...] = g
        d_ref[...], m2_ref[...], v2_ref[...] = _adamw(g, w_ref[...], m_ref[...], v_ref[...])
        total = jnp.sum(g[RP_LOSS:RP_LOSS + 1, :], axis=-1, keepdims=True)
        loss_ref[...] = jnp.broadcast_to(total, loss_ref.shape)

    return pl.pallas_call(
        body, name="adamw_repl",
        out_shape=[SDS((8, D_MODEL), F32)] * 4 + [SDS((8, 128), F32)],
    )(recv, w, m, v)


def _shard_rows(t, lead):
    r = t.shape[1] // N_DEV
    t = t.reshape((lead, N_DEV, r) + t.shape[2:])
    return jnp.moveaxis(t, 1, 0)


def _pad8(t):
    return jnp.pad(t, ((0, 0), (0, 8 - t.shape[1]), (0, 0)))


def _pack_small(wa, wx, cw, b_a, b_x):
    n = wa.shape[0]
    return jnp.concatenate([
        wa.reshape(n, 256, 128), wx.reshape(n, 256, 128), _pad8(cw),
        _pad8(b_a.reshape(n, 1, 128)), _pad8(b_x.reshape(n, 1, 128))], axis=1)


def _unpack_small(t):
    n = t.shape[0]
    return (t[:, SM_WA:SM_WA + 256].reshape(n, 4, 32, 256), t[:, SM_WX:SM_WX + 256].reshape(n, 4, 32, 256),
            t[:, SM_CW:SM_CW + 4], t[:, SM_BA].reshape(n, 4, 32), t[:, SM_BX].reshape(n, 4, 32))


def kernel(x, pre_norm_w, w_in, conv_w, conv_b, lru_w_a, lru_b_a, lru_w_x, lru_b_x, lru_lambda, hgrn_lb_logits, hgrn_gnorm_w, w_out, post_norm_w, loss_target, m_pre_norm_w, m_w_in, m_conv_w, m_conv_b, m_lru_w_a, m_lru_b_a, m_lru_w_x, m_lru_b_x, m_lru_lambda, m_hgrn_lb_logits, m_hgrn_gnorm_w, m_w_out, m_post_norm_w, v_pre_norm_w, v_w_in, v_conv_w, v_conv_b, v_lru_w_a, v_lru_b_a, v_lru_w_x, v_lru_b_x, v_lru_lambda, v_hgrn_lb_logits, v_hgrn_gnorm_w, v_w_out, v_post_norm_w):
    seq = x.shape[1]
    x2 = x.reshape(seq, D_MODEL)
    tgt = loss_target.reshape(seq, D_MODEL)

    small_w = _pack_small(lru_w_a, lru_w_x, conv_w, lru_b_a, lru_b_x)[0]
    me = (4 * lax.axis_index("x") + 2 * lax.axis_index("y") + lax.axis_index("c")).astype(jnp.int32).reshape(1)
    p, u, w_in_all, w_out_all, small_all = _in_proj_gather(
        x2, pre_norm_w, w_in[0].astype(BF16), w_out[0].astype(BF16), small_w, me, seq)
    wa_s, wx_s, cw_s, ba_s, bx_s = _unpack_small(small_all)
    wa = jnp.moveaxis(wa_s, 0, 1).reshape(LRU_BLOCKS, LRU_BW, LRU_BW).astype(BF16)
    wx = jnp.moveaxis(wx_s, 0, 1).reshape(LRU_BLOCKS, LRU_BW, LRU_BW).astype(BF16)
    cw = jnp.moveaxis(cw_s, 0, 1).reshape(4, D_MODEL)
    ba = jnp.moveaxis(ba_s, 0, 1).reshape(1, D_MODEL)
    bx = jnp.moveaxis(bx_s, 0, 1).reshape(1, D_MODEL)

    loc = _local_step(x2, tgt, p, u, cw, conv_b, wa, wx, ba, bx, lru_lambda,
                      hgrn_lb_logits, hgrn_gnorm_w, w_out_all.reshape(D_MIX, D_MODEL), post_norm_w)

    ls, r_out, r_small = loc["lru_small"], loc["r_out"], loc["r_small"]
    core = lax.axis_index("c").astype(jnp.int32).reshape(1)
    g_sib = _grad_w_in_sibling(loc["u"], loc["dp"], core, seq)
    g_chip = _grad_w_in_own(loc["u"], loc["dp"], core, g_sib, seq)
    grad_x, pre_small, r_in = _grad_x(loc["dp"], w_in_all, x2, pre_norm_w, loc["dout"], g_chip, seq)
    g_repl = _pack_rows([(pre_small, 0, 1, 1.0), (ls, 4, 1, 1.0), (ls, 7, 1, 1.0),
                         (loc["hgrn_small"], 0, 1, 1.0), (loc["hgrn_small"], 0, 1, -1.0),
                         (loc["hgrn_small"], 1, 1, 1.0), (loc["stats"], 0, 2, 1.0)], "pack_grads")
    (r_repl,) = _exchange_grads([], g_repl)

    zero = jnp.zeros((1, D_MODEL), F32)
    pack_w = lambda a, b, c_, d, e, f, name: _pack_rows(
        [(a, 0, 1, 1.0), (b, 0, 1, 1.0), (c_, 0, 1, 1.0), (d, 0, 2, 1.0), (e, 0, 1, 1.0), (f, 0, 1, 1.0),
         (zero, 0, 1, 1.0)], name)
    w_repl = pack_w(pre_norm_w, conv_b, lru_lambda, hgrn_lb_logits, hgrn_gnorm_w, post_norm_w, "pack_w")
    m_repl = pack_w(m_pre_norm_w, m_conv_b, m_lru_lambda, m_hgrn_lb_logits, m_hgrn_gnorm_w, m_post_norm_w, "pack_m")
    v_repl = pack_w(v_pre_norm_w, v_conv_b, v_lru_lambda, v_hgrn_lb_logits, v_hgrn_gnorm_w, v_post_norm_w, "pack_v")
    o_repl = _sum_adamw_repl(r_repl, w_repl, m_repl, v_repl)
    loss = o_repl[4][0, 0]

    o_in = _sum_adamw(r_in, w_in[0], m_w_in[0], v_w_in[0], 128, "adamw_w_in")
    o_out = _sum_adamw(r_out, w_out[0], m_w_out[0], v_w_out[0], 64, "adamw_w_out")
    o_small = _sum_adamw(r_small,
                         _pack_small(lru_w_a, lru_w_x, conv_w, lru_b_a, lru_b_x)[0],
                         _pack_small(m_lru_w_a, m_lru_w_x, m_conv_w, m_lru_b_a, m_lru_b_x)[0],
                         _pack_small(v_lru_w_a, v_lru_w_x, v_conv_w, v_lru_b_a, v_lru_b_x)[0],
                         SM_ROWS, "adamw_small")

    outs = [loss, grad_x.reshape(x.shape)]
    for kind in range(4):
        rp = o_repl[kind]
        swa, swx, scw, sba, sbx = _unpack_small(o_small[kind][None])
        outs += [rp[RP_PRE:RP_PRE + 1], o_in[kind][None], scw, rp[RP_CB:RP_CB + 1], swa, sba, swx, sbx,
                 rp[RP_LAM:RP_LAM + 1], rp[RP_LB0:RP_LB1 + 1], rp[RP_GN:RP_GN + 1], o_out[kind][None],
                 rp[RP_POST:RP_POST + 1]]
    return tuple(outs)
```

```python
import functools

import jax
import jax.numpy as jnp
from jax import lax
from jax.experimental import pallas as pl
from jax.experimental.pallas import tpu as pltpu

F32 = jnp.float32
BF16 = jnp.bfloat16
SDS = jax.ShapeDtypeStruct

D_MODEL = 1024
D_IN = 6144
N_DEV = 8
N_CHIPS = 4
W_BLK = D_IN // N_DEV
D_MIX = 2048
LRU_BLOCKS = 4
LRU_BW = 256
LRU_C = 8.0
N_HEADS = 8
HEAD_D = 128
CHUNK = 128
SUB = 32
N_SUB = CHUNK // SUB
HGRN_FWD_STEP_CHUNKS = 4
HGRN_STEP_CHUNKS = 2
EXP_CLAMP = 80.0
EPS = 1e-6

ADAM_LR = 0.001
ADAM_B1 = 0.9
ADAM_B2 = 0.999
ADAM_EPS = 1e-08
ADAM_WD = 0.01
ADAM_STEP = 10

VMEM_LIMIT = 56 * 1024 * 1024

NN = (((1,), (0,)), ((), ()))
NT = (((1,), (1,)), ((), ()))
TN = (((0,), (0,)), ((), ()))

SM_WA = 0
SM_WX = 256
SM_CW = 512
SM_BA = 520
SM_BX = 528
SM_ROWS = 536

RP_PRE, RP_CB, RP_LAM, RP_LB0, RP_LB1, RP_GN, RP_POST, RP_LOSS = range(8)


def _dot(a, b, dims=NN):
    return lax.dot_general(a, b, dims, preferred_element_type=F32)


def _sigmoid(x):
    return 0.5 * jnp.tanh(0.5 * x) + 0.5


def _sigmoid_pos(x):
    return 1.0 / (1.0 + jnp.exp(-x))


def _cparams(sem, vmem=VMEM_LIMIT):
    return pltpu.CompilerParams(dimension_semantics=sem, vmem_limit_bytes=vmem)


def _iota(shape, axis):
    return lax.broadcasted_iota(jnp.int32, shape, axis)


def _softplus_neg(lam):
    z = -lam
    e = jnp.exp(-jnp.abs(z))
    u = 1.0 + e
    log1p_e = jnp.where(u == 1.0, e, jnp.log(u) * (e / (u - 1.0)))
    sp = jnp.maximum(z, 0.0) + log1p_e
    dsp = -jnp.where(z >= 0.0, 1.0 / u, e / u)
    return sp, dsp


def _neg_expm1(x):
    poly = x * (1.0 + x * (1.0 / 2 + x * (1.0 / 6 + x * (1.0 / 24 + x * (1.0 / 120)))))
    return jnp.where(x > -1.0 / 16, -poly, 1.0 - jnp.exp(x))


def _conv_taps(lx, prev8, cw_ref, cb_ref, tile):
    xc = cb_ref[...] + cw_ref[3:4, :] * lx
    for j in (1, 2, 3):
        xc = xc + cw_ref[3 - j:4 - j, :] * pltpu.roll(lx, j, 0)
    row8 = _iota((8, D_MODEL), 0)
    last8 = lx[tile - 8:tile, :]
    fix = jnp.zeros((8, D_MODEL), F32)
    for j in (1, 2, 3):
        wrong = pltpu.roll(last8, j, 0)
        right = pltpu.roll(prev8, j, 0)
        fix = fix + cw_ref[3 - j:4 - j, :] * jnp.where(row8 < j, right - wrong, 0.0)
    return xc, fix


def _lru_gates(xcs, wa, wx, ba, bx, sp):
    xb = xcs.astype(BF16)
    r = _sigmoid_pos(_dot(xb, wa) + ba)
    i = _sigmoid(_dot(xb, wx) + bx)
    la = (-LRU_C * sp) * r
    a = jnp.exp(la)
    one_minus_a2 = _neg_expm1(2.0 * la)
    return r, i, a, one_minus_a2


def _lru_forward(p, conv_w, conv_b, wa, wx, ba, bx, lam, seq):
    tile = min(512, seq // 2)
    nblk = tile // 8

    def body(lx_ref, gt_ref, cw_ref, cb_ref, wa_ref, wx_ref, ba_ref, bx_ref, lam_ref,
             h_ref, y_ref, ext, hcar, xc_scr, a_scr, u_scr):
        @pl.when(pl.program_id(0) == 0)
        def _():
            ext[0:8, :] = jnp.zeros((8, D_MODEL), F32)
            hcar[...] = jnp.zeros_like(hcar)

        lx = lx_ref[...]
        ext[8:8 + tile, :] = lx
        xc = cb_ref[...] + cw_ref[3:4, :] * lx
        for j in (1, 2, 3):
            xc = xc + cw_ref[3 - j:4 - j, :] * ext[8 - j:8 - j + tile, :]
        xc_scr[...] = xc
        ext[0:8, :] = lx_ref[tile - 8:tile, :]
        sp, _ = _softplus_neg(lam_ref[...])
        for n in range(LRU_BLOCKS):
            sl = slice(n * LRU_BW, (n + 1) * LRU_BW)
            xcs = xc_scr[:, sl]
            _, i, a, ne = _lru_gates(xcs, wa_ref[n], wx_ref[n], ba_ref[:, sl], bx_ref[:, sl], sp[:, sl])
            a_scr[:, sl] = a
            u_scr[:, sl] = jnp.sqrt(ne) * (i * xcs)

        row8 = _iota((8, D_MODEL), 0)

        def blk(j, hc):
            off = pl.multiple_of(j * 8, 8)
            a = a_scr[pl.ds(off, 8), :]
            u = u_scr[pl.ds(off, 8), :]
            for k in (1, 2, 4):
                m = row8 >= k
                u = jnp.where(m, u + a * pltpu.roll(u, k, 0), u)
                a = jnp.where(m, a * pltpu.roll(a, k, 0), a)
            h = u + a * hc
            h_ref[pl.ds(off, 8), :] = h
            return jnp.broadcast_to(h[7:8, :], (8, D_MODEL))

        hcar[...] = lax.fori_loop(0, nblk, blk, hcar[...])
        g = gt_ref[...]
        y_ref[...] = (h_ref[...] * (g * _sigmoid(g))).astype(BF16)

    full = lambda shape: pl.BlockSpec(shape, lambda t: (0,) * len(shape))
    return pl.pallas_call(
        body, name="lru_fwd", grid=(seq // tile,),
        in_specs=[pl.BlockSpec((tile, D_MODEL), lambda t: (t, 0)),
                  pl.BlockSpec((tile, D_MODEL), lambda t: (t, 1)),
                  full((4, D_MODEL)), full((1, D_MODEL)),
                  full((LRU_BLOCKS, LRU_BW, LRU_BW)), full((LRU_BLOCKS, LRU_BW, LRU_BW)),
                  full((1, D_MODEL)), full((1, D_MODEL)), full((1, D_MODEL))],
        out_specs=[pl.BlockSpec((tile, D_MODEL), lambda t: (t, 0)),
                   pl.BlockSpec((tile, D_MODEL), lambda t: (t, 0))],
        out_shape=[SDS((seq, D_MODEL), F32), SDS((seq, D_MODEL), BF16)],
        scratch_shapes=[pltpu.VMEM((tile + 8, D_MODEL), F32), pltpu.VMEM((8, D_MODEL), F32),
                        pltpu.VMEM((tile, D_MODEL), F32), pltpu.VMEM((tile, D_MODEL), F32),
                        pltpu.VMEM((tile, D_MODEL), F32)],
        compiler_params=_cparams(("arbitrary",)),
    )(p, p, conv_w, conv_b, wa, wx, ba, bx, lam)


def _lru_backward(p, h, dymix, conv_w, conv_b, wa, wx, ba, bx, lam, seq):
    tile = min(256, seq // 2)
    nt = seq // tile
    nblk = tile // 8
    t8 = tile // 8

    def body(lx_ref, lxh_ref, gt_ref, h_ref, hh_ref, dy_ref, cw_ref, cb_ref, wa_ref, wx_ref, ba_ref,
             bx_ref, lam_ref, dp_ref, gwa_ref, gwx_ref, gsm_ref,
             lamcar, anext, dxc8, xc_scr, r_scr, i_scr, a_scr, m_scr, rm_scr, c_scr, l_scr, dxc_scr):
        step = pl.program_id(0)
        first_tile = step == nt - 1

        @pl.when(step == 0)
        def _():
            lamcar[...] = jnp.zeros_like(lamcar)
            anext[...] = jnp.zeros_like(anext)
            dxc8[...] = jnp.zeros_like(dxc8)
            gwa_ref[...] = jnp.zeros_like(gwa_ref)
            gwx_ref[...] = jnp.zeros_like(gwx_ref)
            gsm_ref[...] = jnp.zeros_like(gsm_ref)

        keep = jnp.where(first_tile, 0.0, 1.0)
        lx = lx_ref[...]
        prev8 = lxh_ref[...] * keep
        xc, fix = _conv_taps(lx, prev8, cw_ref, cb_ref, tile)
        xc_scr[...] = xc
        xc_scr[0:8, :] = xc_scr[0:8, :] + fix
        sp, dsp = _softplus_neg(lam_ref[...])
        for n in range(LRU_BLOCKS):
            sl = slice(n * LRU_BW, (n + 1) * LRU_BW)
            r, i, a, ne = _lru_gates(xc_scr[:, sl], wa_ref[n], wx_ref[n], ba_ref[:, sl], bx_ref[:, sl],
                                     sp[:, sl])
            r_scr[:, sl] = r
            i_scr[:, sl] = i
            a_scr[:, sl] = a
            m_scr[:, sl] = jnp.sqrt(ne)
            rm_scr[:, sl] = lax.rsqrt(ne)

        g = gt_ref[...]
        sg = _sigmoid(g)
        dy = dy_ref[...]
        hv = h_ref[...]
        dp_ref[:, D_MODEL:2 * D_MODEL] = (dy * hv * (sg * (1.0 + g * (1.0 - sg)))).astype(BF16)

        rowt = _iota((tile, D_MODEL), 0)
        av = a_scr[...]
        l_scr[...] = dy * (g * sg)
        c_scr[...] = jnp.where(rowt == tile - 1, anext[...][0:1, :], pltpu.roll(av, tile - 1, 0))
        anext[...] = jnp.broadcast_to(av[0:1, :], (8, D_MODEL))
        row8 = _iota((8, D_MODEL), 0)

        def blk(jj, lc):
            off = pl.multiple_of((nblk - 1 - jj) * 8, 8)
            c = c_scr[pl.ds(off, 8), :]
            u = l_scr[pl.ds(off, 8), :]
            for k in (1, 2, 4):
                m = row8 < 8 - k
                u = jnp.where(m, u + c * pltpu.roll(u, 8 - k, 0), u)
                c = jnp.where(m, c * pltpu.roll(c, 8 - k, 0), c)
            lamv = u + c * lc
            l_scr[pl.ds(off, 8), :] = lamv
            return jnp.broadcast_to(lamv[0:1, :], (8, D_MODEL))

        lamcar[...] = lax.fori_loop(0, nblk, blk, lamcar[...])

        hprev = jnp.where(rowt == 0, hh_ref[...][7:8, :] * keep, pltpu.roll(hv, 1, 0))
        for n in range(LRU_BLOCKS):
            sl = slice(n * LRU_BW, (n + 1) * LRU_BW)
            lamv = l_scr[:, sl]
            xcs = xc_scr[:, sl]
            r = r_scr[:, sl]
            i = i_scr[:, sl]
            a = a_scr[:, sl]
            mult = m_scr[:, sl]
            d_la = lamv * hprev[:, sl] * a - (lamv * i * xcs) * (a * a * rm_scr[:, sl])
            d_pr = d_la * (-LRU_C * sp[:, sl]) * r * (1.0 - r)
            d_pi = (lamv * mult * xcs) * i * (1.0 - i)
            gsm_ref[7:8, sl] += jnp.sum(d_la * r, axis=0, keepdims=True) * (-LRU_C) * dsp[:, sl]
            gsm_ref[5:6, sl] += jnp.sum(d_pr, axis=0, keepdims=True)
            gsm_ref[6:7, sl] += jnp.sum(d_pi, axis=0, keepdims=True)
            xb = xcs.astype(BF16)
            prb = d_pr.astype(BF16)
            pib = d_pi.astype(BF16)
            gwa_ref[n] += _dot(xb, prb, TN)
            gwx_ref[n] += _dot(xb, pib, TN)
            dxc_scr[:, sl] = lamv * mult * i + _dot(prb, wa_ref[n], NT) + _dot(pib, wx_ref[n], NT)

        dxc = dxc_scr[...]
        gsm_ref[4:5, :] += jnp.sum(dxc, axis=0, keepdims=True)
        last8 = lx[tile - 8:tile, :]
        first8 = dxc[0:8, :]
        dlx = cw_ref[3:4, :] * dxc
        gsm_ref[3:4, :] += jnp.sum(dxc * lx, axis=0, keepdims=True)
        fix = jnp.zeros((8, D_MODEL), F32)
        for j in (1, 2, 3):
            w = cw_ref[3 - j:4 - j, :]
            dlx = dlx + w * pltpu.roll(dxc, tile - j, 0)
            fix = fix + w * jnp.where(row8 + j >= 8,
                                      pltpu.roll(dxc8[...], 8 - j, 0) - pltpu.roll(first8, 8 - j, 0), 0.0)
            halo = jnp.where(row8 < j, pltpu.roll(prev8, j, 0) - pltpu.roll(last8, j, 0), 0.0)
            gsm_ref[3 - j:4 - j, :] += (jnp.sum(dxc * pltpu.roll(lx, j, 0), axis=0, keepdims=True)
                                        + jnp.sum(first8 * halo, axis=0, keepdims=True))
        dxc8[...] = first8
        dp_ref[:, 0:D_MODEL] = dlx.astype(BF16)
        top = tile - 8
        dp_ref[top:tile, 0:D_MODEL] = (dlx[top:tile, :] + fix).astype(BF16)

    rev = lambda t: (nt - 1 - t, 0)
    halo_idx = lambda t: (jnp.maximum((nt - 1 - t) * t8 - 1, 0), 0)
    full = lambda shape: pl.BlockSpec(shape, lambda t: (0,) * len(shape))
    big = lambda: pltpu.VMEM((tile, D_MODEL), F32)
    return pl.pallas_call(
        body, name="lru_bwd", grid=(nt,),
        in_specs=[pl.BlockSpec((tile, D_MODEL), rev),
                  pl.BlockSpec((8, D_MODEL), halo_idx),
                  pl.BlockSpec((tile, D_MODEL), lambda t: (nt - 1 - t, 1)),
                  pl.BlockSpec((tile, D_MODEL), rev),
                  pl.BlockSpec((8, D_MODEL), halo_idx),
                  pl.BlockSpec((tile, D_MODEL), rev),
                  full((4, D_MODEL)), full((1, D_MODEL)),
                  full((LRU_BLOCKS, LRU_BW, LRU_BW)), full((LRU_BLOCKS, LRU_BW, LRU_BW)),
                  full((1, D_MODEL)), full((1, D_MODEL)), full((1, D_MODEL))],
        out_specs=[pl.BlockSpec((tile, 2 * D_MODEL), rev),
                   full((LRU_BLOCKS, LRU_BW, LRU_BW)), full((LRU_BLOCKS, LRU_BW, LRU_BW)),
                   full((8, D_MODEL))],
        out_shape=[SDS((seq, D_IN), BF16), SDS((LRU_BLOCKS, LRU_BW, LRU_BW), F32),
                   SDS((LRU_BLOCKS, LRU_BW, LRU_BW), F32), SDS((8, D_MODEL), F32)],
        scratch_shapes=[pltpu.VMEM((8, D_MODEL), F32), pltpu.VMEM((8, D_MODEL), F32),
                        pltpu.VMEM((8, D_MODEL), F32)] + [big() for _ in range(9)],
        compiler_params=_cparams(("arbitrary",)),
    )(p, p, p, h, h, dymix, conv_w, conv_b, wa, wx, ba, bx, lam)


def _tri_matmul(tri, g):
    hi = g.astype(BF16)
    lo = (g - hi.astype(F32)).astype(BF16)
    return _dot(tri, lo) + _dot(tri, hi)


def _hgrn_gate_terms(q, fr, lbl):
    lb = _sigmoid_pos(lbl[0:1, :] - lbl[1:2, :])
    half = 0.5 * (1.0 - lb)
    tf = jnp.tanh(0.5 * fr)
    f = (lb + half) + half * tf
    hq = 0.5 * q
    tq = jnp.tanh(hq)
    return lb, tf, f, tq, hq * tq + hq


def _hgrn_decay(bh):
    zero = jnp.zeros((1, bh.shape[1]), F32)
    rho = [zero] + [bh[s * SUB - 1:s * SUB, :] for s in range(1, N_SUB + 1)]
    start = _sub_rows(rho[0:N_SUB])
    end = _sub_rows(rho[1:N_SUB + 1])
    mid = 0.5 * (start + end)
    blast = rho[N_SUB]
    e_on = jnp.exp(bh - start)
    e_off = jnp.exp(end - bh)
    scales = [_sub_rows([jnp.exp(rho[i] - rho[j + 1]) if i > j else zero for i in range(N_SUB)])
              for j in range(N_SUB - 1)]
    return dict(eq0=jnp.exp(jnp.minimum(bh - mid, EXP_CLAMP)), ek0=jnp.exp(jnp.minimum(mid - bh, EXP_CLAMP)),
                e_on=e_on, e_off=e_off, scales=scales,
                eb=e_on * _sub_rows([jnp.exp(r) for r in rho[0:N_SUB]]),
                ekst=e_off * _sub_rows([jnp.exp(blast - r) for r in rho[1:N_SUB + 1]]),
                ebl=jnp.exp(blast))


def _sub_rows(vecs):
    return jnp.concatenate([jnp.broadcast_to(v, (SUB, v.shape[1])) for v in vecs], axis=0)


def _hgrn_operands(qs, k, dec, qt_scr, kt_scr):
    sub = jnp.right_shift(_iota(qs.shape, 0), 5)
    qon = qs * dec["e_on"]
    koff = k * dec["e_off"]
    qt_scr[0] = (qs * dec["eq0"]).astype(BF16)
    kt_scr[0] = (k * dec["ek0"]).astype(BF16)
    for j in range(N_SUB - 1):
        qt_scr[j + 1] = (qon * dec["scales"][j]).astype(BF16)
        kt_scr[j + 1] = jnp.where(sub == j, koff, 0.0).astype(BF16)
    return koff


def _hgrn_head_scores(qt_scr, kt_scr, sl, diag):
    a = jnp.where(diag, _dot(qt_scr[0, :, sl], kt_scr[0, :, sl], NT), 0.0)
    for j in range(1, N_SUB):
        a = a + _dot(qt_scr[j, :, sl], kt_scr[j, :, sl], NT)
    return a


def _hgrn_forward(p, lbl, gw, seq):
    nc = seq // CHUNK
    assert SUB == 32

    def body(q_ref, f_ref, v_ref, hg_ref, lbl_ref, gw_ref, y_ref, o_ref, st_ref,
             s_scr, qt_scr, kt_scr, qin_scr, kst_scr, vb_scr, a_scr):
        @pl.when(pl.program_id(0) == 0)
        def _():
            s_scr[...] = jnp.zeros_like(s_scr)

        r = _iota((CHUNK, CHUNK), 0)
        c = _iota((CHUNK, CHUNK), 1)
        tri = jnp.where(c <= r, 1.0, 0.0).astype(BF16)
        diag = (jnp.right_shift(r, 5) == jnp.right_shift(c, 5)) & (c <= r)
        heads = [slice(h * HEAD_D, (h + 1) * HEAD_D) for h in range(N_HEADS)]
        for cc in range(HGRN_FWD_STEP_CHUNKS):
            rows = slice(cc * CHUNK, (cc + 1) * CHUNK)
            q = q_ref[rows, :]
            _, _, f, _, qs = _hgrn_gate_terms(q, f_ref[rows, :], lbl_ref[...])
            k = 1.0 - f
            dec = _hgrn_decay(_tri_matmul(tri, jnp.log(f)))
            _hgrn_operands(qs, k, dec, qt_scr, kt_scr)
            qin_scr[...] = (qs * dec["eb"]).astype(BF16)
            kst_scr[...] = (k * dec["ekst"]).astype(BF16)
            vb_scr[...] = v_ref[rows, :].astype(BF16)
            ebl = dec["ebl"]
            hg = hg_ref[rows, :]
            gate = gw_ref[...] * (hg * _sigmoid(hg))
            stb = []
            for h, sl in enumerate(heads):
                st = s_scr[h]
                st_ref[cc, h] = st
                stb.append(st.astype(BF16))
                s_scr[h] = st * ebl[:, sl] + _dot(vb_scr[:, sl], kst_scr[:, sl], TN)
            for h, sl in enumerate(heads):
                a_scr[h] = _hgrn_head_scores(qt_scr, kt_scr, sl, diag).astype(BF16)
            for h, sl in enumerate(heads):
                o = _dot(a_scr[h], vb_scr[:, sl]) + _dot(qin_scr[:, sl], stb[h], NT)
                o_ref[rows, sl] = o
                rs = lax.rsqrt(jnp.mean(o * o, axis=-1, keepdims=True) + EPS)
                y_ref[rows, sl] = ((o * rs) * gate[:, sl]).astype(BF16)

    col = lambda j: pl.BlockSpec((HGRN_FWD_STEP_CHUNKS * CHUNK, D_MODEL), lambda c: (c, j))
    par = lambda rows: pl.BlockSpec((rows, D_MODEL), lambda c: (0, 0))
    return pl.pallas_call(
        body, name="hgrn_fwd", grid=(nc // HGRN_FWD_STEP_CHUNKS,),
        in_specs=[col(2), col(3), col(4), col(5), par(2), par(1)],
        out_specs=[col(0), col(0),
                   pl.BlockSpec((HGRN_FWD_STEP_CHUNKS, N_HEADS, HEAD_D, HEAD_D), lambda c: (c, 0, 0, 0))],
        out_shape=[SDS((seq, D_MODEL), BF16), SDS((seq, D_MODEL), F32),
                   SDS((nc, N_HEADS, HEAD_D, HEAD_D), F32)],
        scratch_shapes=[pltpu.VMEM((N_HEADS, HEAD_D, HEAD_D), F32),
                        pltpu.VMEM((N_SUB, CHUNK, D_MODEL), BF16), pltpu.VMEM((N_SUB, CHUNK, D_MODEL), BF16)]
                       + [pltpu.VMEM((CHUNK, D_MODEL), BF16)] * 3 + [pltpu.VMEM((N_HEADS, CHUNK, CHUNK), BF16)],
        compiler_params=_cparams(("arbitrary",)),
    )(p, p, p, p, lbl, gw)


def _hgrn_backward(p, o, states, dymix, lbl, gw, dp_full, g_w_out, g_small, seq):
    step_rows = HGRN_STEP_CHUNKS * CHUNK
    ns = seq // step_rows

    def body(q_ref, f_ref, v_ref, hg_ref, o_ref, st_ref, dy_ref, lbl_ref, gw_ref, dpin_ref, go_ref, gs_ref,
             dpo_ref, gsm_ref, ro_ref, rs_ref, ds_scr, dp_buf, dp_sems, *rest):
        del dpin_ref
        scratch, sems = rest[:14], rest[14:]
        step = pl.program_id(0)
        slot = step % 2
        exs = [_SlotExchange(go_ref, ro_ref, *sems[0:3], blocked=True),
               _SlotExchange(gs_ref, rs_ref, *sems[3:6], blocked=True)]

        @pl.when(step == 0)
        def _():
            for ex in exs:
                ex.start()

        def out_copy(s, blk):
            rows = pl.ds(pl.multiple_of(blk * step_rows, step_rows), step_rows)
            return pltpu.make_async_copy(dp_buf.at[s], dpo_ref.at[rows, pl.ds(2 * D_MODEL, 4 * D_MODEL)],
                                         dp_sems.at[s])

        @pl.when(step == 0)
        def _():
            ds_scr[...] = jnp.zeros_like(ds_scr)
            gsm_ref[...] = jnp.zeros_like(gsm_ref)

        @pl.when(step >= 2)
        def _():
            out_copy(slot, ns + 1 - step).wait()

        for cc in reversed(range(HGRN_STEP_CHUNKS)):
            chunk(cc, q_ref, f_ref, v_ref, hg_ref, o_ref, st_ref, dy_ref, lbl_ref, gw_ref, gsm_ref, ds_scr,
                  dp_buf.at[slot], *scratch)

        out_copy(slot, ns - 1 - step).start()

        @pl.when(step == ns - 1)
        def _():
            out_copy(1 - slot, 1).wait()
            out_copy(slot, 0).wait()
            for ex in exs:
                ex.wait()

    def chunk(cc, q_ref, f_ref, v_ref, hg_ref, o_ref, st_ref, dy_ref, lbl_ref, gw_ref, gsm_ref, ds_scr, dp_ref,
              qt_scr, kt_scr, qin_scr, kst_scr, vb_scr, dob_scr, g_scr, h_scr, dqi_scr, dks_scr, sd_scr,
              a_scr, da_scr, da0_scr):
        rows = slice(cc * CHUNK, (cc + 1) * CHUNK)
        r = _iota((CHUNK, CHUNK), 0)
        c = _iota((CHUNK, CHUNK), 1)
        tri = jnp.where(c <= r, 1.0, 0.0).astype(BF16)
        triu = jnp.where(c >= r, 1.0, 0.0).astype(BF16)
        diag = (jnp.right_shift(r, 5) == jnp.right_shift(c, 5)) & (c <= r)
        row = _iota((CHUNK, D_MODEL), 0)
        sub = jnp.right_shift(row, 5)

        q = q_ref[rows, :]
        lb, tf, f, tq, qs = _hgrn_gate_terms(q, f_ref[rows, :], lbl_ref[...])
        sig = 0.5 * tf + 0.5
        sq = 0.5 * tq + 0.5
        k = 1.0 - f
        dec = _hgrn_decay(_tri_matmul(tri, jnp.log(f)))
        eb, ekst, ebl = dec["eb"], dec["ekst"], dec["ebl"]
        koff = _hgrn_operands(qs, k, dec, qt_scr, kt_scr)
        qin_scr[...] = (qs * eb).astype(BF16)
        kst_scr[...] = (k * ekst).astype(BF16)
        vb_scr[...] = v_ref[rows, :].astype(BF16)
        hg = hg_ref[rows, :]
        sh = _sigmoid(hg)
        dy = dy_ref[rows, :]
        gwv = gw_ref[...]
        d_onw = dy * (hg * sh)
        d_on = d_onw * gwv
        d_gate = dy * gwv * (sh * (1.0 + hg * (1.0 - sh)))

        heads = [slice(h * HEAD_D, (h + 1) * HEAD_D) for h in range(N_HEADS)]
        for h, sl in enumerate(heads):
            o = o_ref[rows, sl]
            rs = lax.rsqrt(jnp.mean(o * o, axis=-1, keepdims=True) + EPS)
            on = o * rs
            dp_ref[rows, 3 * D_MODEL + h * HEAD_D:3 * D_MODEL + (h + 1) * HEAD_D] = (d_gate[:, sl] * on).astype(BF16)
            gsm_ref[1:2, sl] += jnp.sum(d_onw[:, sl] * on, axis=0, keepdims=True)
            d_onh = d_on[:, sl]
            dob_scr[:, sl] = (rs * (d_onh - on * jnp.mean(d_onh * on, axis=-1, keepdims=True))).astype(BF16)
        for h, sl in enumerate(heads):
            a_scr[h] = _hgrn_head_scores(qt_scr, kt_scr, sl, diag).astype(BF16)
            da = _dot(dob_scr[:, sl], vb_scr[:, sl], NT)
            da_scr[h] = da.astype(BF16)
            da0_scr[h] = jnp.where(diag, da, 0.0).astype(BF16)
        for h, sl in enumerate(heads):
            st = st_ref[cc, h]
            dst = ds_scr[h]
            dstb = dst.astype(BF16)
            dp_ref[rows, 2 * D_MODEL + h * HEAD_D:2 * D_MODEL + (h + 1) * HEAD_D] = (
                _dot(a_scr[h], dob_scr[:, sl], TN) + _dot(kst_scr[:, sl], dstb, NT)).astype(BF16)
            dqi_scr[:, sl] = _dot(dob_scr[:, sl], st.astype(BF16))
            dks_scr[:, sl] = _dot(vb_scr[:, sl], dstb)
            sd_scr[0:1, sl] = jnp.sum(st * dst, axis=0, keepdims=True)
            ds_scr[h] = dst * ebl[:, sl] + _dot(dob_scr[:, sl], qin_scr[:, sl], TN)
        for h, sl in enumerate(heads):
            g_scr[0, :, sl] = _dot(da0_scr[h], kt_scr[0, :, sl])
            h_scr[0, :, sl] = _dot(da0_scr[h], qt_scr[0, :, sl], TN)
            for j in range(1, N_SUB):
                g_scr[j, :, sl] = _dot(da_scr[h], kt_scr[j, :, sl])
                h_scr[j, :, sl] = _dot(da_scr[h], qt_scr[j, :, sl], TN)

        g0 = g_scr[0]
        h0 = h_scr[0]
        dq_inter = eb * dqi_scr[...]
        d_kst = ekst * dks_scr[...]
        db = qs * dq_inter - k * d_kst + qt_scr[0].astype(F32) * g0 - kt_scr[0].astype(F32) * h0
        gq = jnp.zeros((CHUNK, D_MODEL), F32)
        hsel = jnp.zeros((CHUNK, D_MODEL), F32)
        for j in range(N_SUB - 1):
            gj = g_scr[j + 1]
            gq = gq + dec["scales"][j] * gj
            db = db + qt_scr[j + 1].astype(F32) * gj
            hsel = jnp.where(sub == j, h_scr[j + 1], hsel)
        db = db - koff.astype(BF16).astype(F32) * hsel
        d_q = dec["eq0"] * g0 + dec["e_on"] * gq + dq_inter
        d_k = dec["ek0"] * h0 + dec["e_off"] * hsel + d_kst
        db_last = jnp.sum(k * d_kst, axis=0, keepdims=True) + ebl * sd_scr[0:1, :]
        db = db + jnp.where(row == CHUNK - 1, db_last, 0.0)
        dg = _tri_matmul(triu, db)
        d_f = dg / f - d_k
        dp_ref[rows, D_MODEL:2 * D_MODEL] = (d_f * (1.0 - lb) * sig * (1.0 - sig)).astype(BF16)
        gsm_ref[0:1, :] += jnp.sum(d_f * (1.0 - sig), axis=0, keepdims=True) * (lb * (1.0 - lb))
        dp_ref[rows, 0:D_MODEL] = (d_q * (sq * (1.0 + q * (1.0 - sq)))).astype(BF16)

    rc = lambda c: ns - 1 - c
    col = lambda j: pl.BlockSpec((step_rows, D_MODEL), lambda c: (rc(c), j))
    par = lambda rows: pl.BlockSpec((rows, D_MODEL), lambda c: (0, 0))
    return pl.pallas_call(
        body, name="hgrn_bwd", grid=(ns,),
        in_specs=[col(2), col(3), col(4), col(5), col(0),
                  pl.BlockSpec((HGRN_STEP_CHUNKS, N_HEADS, HEAD_D, HEAD_D), lambda c: (rc(c), 0, 0, 0)),
                  col(1), par(2), par(1), ANY, ANY, ANY],
        out_specs=[ANY, par(8), ANY, ANY],
        out_shape=[SDS((seq, D_IN), BF16), SDS((8, D_MODEL), F32), SDS(g_w_out.shape, F32),
                   SDS(g_small.shape, F32)],
        input_output_aliases={9: 0},
        scratch_shapes=[pltpu.VMEM((N_HEADS, HEAD_D, HEAD_D), F32),
                        pltpu.VMEM((2, step_rows, 4 * D_MODEL), BF16), pltpu.SemaphoreType.DMA((2,)),
                        pltpu.VMEM((N_SUB, CHUNK, D_MODEL), BF16), pltpu.VMEM((N_SUB, CHUNK, D_MODEL), BF16)]
                       + [pltpu.VMEM((CHUNK, D_MODEL), BF16)] * 4
                       + [pltpu.VMEM((N_SUB, CHUNK, D_MODEL), F32)] * 2 + [pltpu.VMEM((CHUNK, D_MODEL), F32)] * 2
                       + [pltpu.VMEM((8, D_MODEL), F32)] + [pltpu.VMEM((N_HEADS, CHUNK, CHUNK), BF16)] * 3
                       + EXCHANGE_SEMS * 2,
        compiler_params=_cparams(("arbitrary",)),
    )(p, p, p, p, o, states, dymix, lbl, gw, dp_full, g_w_out, g_small)


def _out_proj(yl, yh, wo, x, tgt, post_w, seq):
    tm = 512

    def body(yl_ref, yh_ref, wo_ref, x_ref, tg_ref, pw_ref, dymix_ref, dout_ref, gwo_ref, st_ref):
        @pl.when(pl.program_id(0) == 0)
        def _():
            gwo_ref[...] = jnp.zeros_like(gwo_ref)
            st_ref[...] = jnp.zeros_like(st_ref)

        ylv = yl_ref[...]
        yhv = yh_ref[...]
        y = _dot(ylv, wo_ref[0:D_MODEL, :]) + _dot(yhv, wo_ref[D_MODEL:D_MIX, :])
        r2 = lax.rsqrt(jnp.mean(y * y, axis=-1, keepdims=True) + EPS)
        yn = y * r2
        pw = pw_ref[...]
        e = (x_ref[...] + yn * pw) - tg_ref[...]
        st_ref[1:2, :] += jnp.sum(e * e, axis=0, keepdims=True) * (0.5 / D_MODEL)
        dout = e * (1.0 / D_MODEL)
        dout_ref[...] = dout
        st_ref[0:1, :] += jnp.sum(dout * yn, axis=0, keepdims=True)
        dyn = dout * pw
        dy = r2 * (dyn - yn * jnp.mean(dyn * yn, axis=-1, keepdims=True))
        dyb = dy.astype(BF16)
        dymix_ref[...] = _dot(dyb, wo_ref[...], NT)
        gwo_ref[0:D_MODEL, :] += _dot(ylv, dyb, TN)
        gwo_ref[D_MODEL:D_MIX, :] += _dot(yhv, dyb, TN)

    row = lambda w: pl.BlockSpec((tm, w), lambda m: (m, 0))
    full = lambda shape: pl.BlockSpec(shape, lambda m: (0,) * len(shape))
    once = lambda shape: pl.BlockSpec(shape, lambda m: (0,) * len(shape), pipeline_mode=pl.Buffered(1))
    return pl.pallas_call(
        body, name="out_proj", grid=(seq // tm,),
        in_specs=[row(D_MODEL), row(D_MODEL), once((D_MIX, D_MODEL)), row(D_MODEL), row(D_MODEL),
                  full((1, D_MODEL))],
        out_specs=[row(D_MIX), row(D_MODEL), once((D_MIX, D_MODEL)), full((8, D_MODEL))],
        out_shape=[SDS((seq, D_MIX), F32), SDS((seq, D_MODEL), F32), SDS((D_MIX, D_MODEL), F32),
                   SDS((8, D_MODEL), F32)],
        compiler_params=_cparams(("arbitrary",)),
    )(yl, yh, wo, x, tgt, post_w)


MESH = pl.DeviceIdType.MESH
ANY = pl.BlockSpec(memory_space=pl.ANY)
EXCHANGE_SEMS = [pltpu.SemaphoreType.DMA((N_DEV - 1,)), pltpu.SemaphoreType.DMA((N_DEV - 1,)),
                 pltpu.SemaphoreType.DMA(())]


def _mesh_pos():
    return lax.axis_index("x"), lax.axis_index("y"), lax.axis_index("c")


class _SlotExchange:
    def __init__(self, src_ref, dst_ref, send_sems, recv_sems, local_sem, blocked):
        x, y, c = _mesh_pos()
        me = 4 * x + 2 * y + c
        src = (lambda dest: src_ref.at[dest]) if blocked else (lambda dest: src_ref)
        self.local = pltpu.make_async_copy(src(me), dst_ref.at[me], local_sem)
        self.sends, self.recvs = [], []
        for k in range(1, N_DEV):
            px = 1 - x if (k >> 2) & 1 else x
            py = 1 - y if (k >> 1) & 1 else y
            pc = 1 - c if k & 1 else c
            peer = 4 * px + 2 * py + pc
            sems = dict(send_sem=send_sems.at[k - 1], recv_sem=recv_sems.at[k - 1],
                        device_id=(px, py, pc), device_id_type=MESH)
            self.sends.append(pltpu.make_async_remote_copy(src_ref=src(peer), dst_ref=dst_ref.at[me], **sems))
            self.recvs.append(pltpu.make_async_remote_copy(src_ref=dst_ref.at[peer], dst_ref=dst_ref.at[peer], **sems))

    def start(self):
        self.local.start()
        for cp in self.sends:
            cp.start()

    def wait(self):
        for cp in self.recvs:
            cp.wait_recv()
        for cp in self.sends:
            cp.wait_send()
        self.local.wait()


class _ChipExchange:
    def __init__(self, src_ref, dst_ref, send_sems, recv_sems, local_sem):
        x, y, c = _mesh_pos()
        chip = 2 * x + y
        self.local = pltpu.make_async_copy(src_ref.at[chip], dst_ref.at[chip], local_sem)
        self.sends, self.recvs = [], []
        for k in range(1, N_CHIPS):
            px = 1 - x if (k >> 1) & 1 else x
            py = 1 - y if k & 1 else y
            peer = 2 * px + py
            sems = dict(send_sem=send_sems.at[k - 1], recv_sem=recv_sems.at[k - 1],
                        device_id=(px, py, c), device_id_type=MESH)
            self.sends.append(pltpu.make_async_remote_copy(src_ref=src_ref.at[peer], dst_ref=dst_ref.at[chip], **sems))
            self.recvs.append(pltpu.make_async_remote_copy(src_ref=dst_ref.at[peer], dst_ref=dst_ref.at[peer], **sems))

    def start(self):
        self.local.start()
        for cp in self.sends:
            cp.start()

    def wait(self):
        for cp in self.recvs:
            cp.wait_recv()
        for cp in self.sends:
            cp.wait_send()
        self.local.wait()


GRAD_W_IN_TK = 2048


def _grad_w_in_sibling(u, dp, core, seq):
    tk = min(GRAD_W_IN_TK, seq)
    nk = seq // tk

    def body(core_ref, u_ref, dp_ref, g_ref):
        del core_ref

        @pl.when(pl.program_id(1) == 0)
        def _():
            g_ref[...] = jnp.zeros_like(g_ref)

        g_ref[0] += _dot(u_ref[...], dp_ref[...], TN)

    return pl.pallas_call(
        body, name="grad_w_in_sibling",
        grid_spec=pltpu.PrefetchScalarGridSpec(
            num_scalar_prefetch=1, grid=(N_CHIPS, nk),
            in_specs=[pl.BlockSpec((tk, D_MODEL), lambda n, k, c: (k, 0)),
                      pl.BlockSpec((tk, W_BLK), lambda n, k, c: (k, 2 * n + 1 - c[0]))],
            out_specs=pl.BlockSpec((1, D_MODEL, W_BLK), lambda n, k, c: (n, 0, 0))),
        out_shape=SDS((N_CHIPS, D_MODEL, W_BLK), F32),
        compiler_params=_cparams(("parallel", "arbitrary")),
    )(core, u, dp)


def _grad_w_in_own(u, dp, core, g_sib, seq):
    tk = min(GRAD_W_IN_TK, seq)
    nk = seq // tk

    def body(core_ref, u_ref, dp_ref, gsib_ref, g_ref, land, send_sem, recv_sem):
        del core_ref
        n = pl.program_id(0)
        k = pl.program_id(1)
        x, y, c = _mesh_pos()
        swap = pltpu.make_async_remote_copy(src_ref=gsib_ref, dst_ref=land, send_sem=send_sem, recv_sem=recv_sem,
                                            device_id=(x, y, 1 - c), device_id_type=MESH)

        @pl.when((n == 0) & (k == 0))
        def _():
            swap.start()

        @pl.when(k == 0)
        def _():
            g_ref[...] = jnp.zeros_like(g_ref)

        g_ref[0] += _dot(u_ref[...], dp_ref[...], TN)

        @pl.when((n == 0) & (k == nk - 1))
        def _():
            swap.wait_recv()

        @pl.when(k == nk - 1)
        def _():
            g_ref[0] += land[n]

        @pl.when((n == N_CHIPS - 1) & (k == nk - 1))
        def _():
            swap.wait_send()

    return pl.pallas_call(
        body, name="grad_w_in_own",
        grid_spec=pltpu.PrefetchScalarGridSpec(
            num_scalar_prefetch=1, grid=(N_CHIPS, nk),
            in_specs=[pl.BlockSpec((tk, D_MODEL), lambda n, k, c: (k, 0)),
                      pl.BlockSpec((tk, W_BLK), lambda n, k, c: (k, 2 * n + c[0])), ANY],
            out_specs=pl.BlockSpec((1, D_MODEL, W_BLK), lambda n, k, c: (n, 0, 0)),
            scratch_shapes=[pltpu.VMEM((N_CHIPS, D_MODEL, W_BLK), F32), pltpu.SemaphoreType.DMA(()),
                            pltpu.SemaphoreType.DMA(())]),
        out_shape=SDS((N_CHIPS, D_MODEL, W_BLK), F32),
        compiler_params=_cparams(("arbitrary", "arbitrary")),
    )(core, u, dp, g_sib)


def _grad_x(dp, w_all, x, pre_w, dout, g_chip, seq):
    tm = 512
    nm = seq // tm

    def body(dp_ref, w_ref, x_ref, pw_ref, do_ref, gsrc_ref, gx_ref, gpw_ref, recv_ref,
             send_sems, recv_sems, local_sem):
        m = pl.program_id(0)
        ex = _ChipExchange(gsrc_ref, recv_ref, send_sems, recv_sems, local_sem)

        @pl.when(m == 0)
        def _():
            ex.start()
            gpw_ref[...] = jnp.zeros_like(gpw_ref)

        du = _dot(dp_ref[:, 0:W_BLK], w_ref[0], NT)
        for j in range(1, N_DEV):
            du = du + _dot(dp_ref[:, j * W_BLK:(j + 1) * W_BLK], w_ref[j], NT)
        xv = x_ref[...]
        r1 = lax.rsqrt(jnp.mean(xv * xv, axis=-1, keepdims=True) + EPS)
        xn = xv * r1
        gpw_ref[0:1, :] += jnp.sum(du * xn, axis=0, keepdims=True)
        dxn = du * pw_ref[...]
        gx_ref[...] = r1 * (dxn - xn * jnp.mean(dxn * xn, axis=-1, keepdims=True)) + do_ref[...]

        @pl.when(m == nm - 1)
        def _():
            ex.wait()

    row = lambda w: pl.BlockSpec((tm, w), lambda m: (m, 0))
    return pl.pallas_call(
        body, name="grad_x", grid=(nm,),
        in_specs=[row(D_IN), pl.BlockSpec((N_DEV, D_MODEL, W_BLK), lambda m: (0, 0, 0), pipeline_mode=pl.Buffered(1)),
                  row(D_MODEL),
                  pl.BlockSpec((1, D_MODEL), lambda m: (0, 0)), row(D_MODEL), ANY],
        out_specs=[row(D_MODEL), pl.BlockSpec((8, D_MODEL), lambda m: (0, 0)), ANY],
        out_shape=[SDS((seq, D_MODEL), F32), SDS((8, D_MODEL), F32), SDS(g_chip.shape, F32)],
        scratch_shapes=[pltpu.SemaphoreType.DMA((N_CHIPS - 1,)), pltpu.SemaphoreType.DMA((N_CHIPS - 1,)),
                        pltpu.SemaphoreType.DMA(())],
        compiler_params=_cparams(("arbitrary",)),
    )(dp, w_all, x, pre_w, dout, g_chip)


def _local_step(x, tgt, p, u, conv_w, conv_b, wa, wx, ba, bx, lam, lbl, gnorm_w, w_out, post_w):
    seq = x.shape[0]
    h, y_lru = _lru_forward(p, conv_w, conv_b, wa, wx, ba, bx, lam, seq)
    y_hgrn, o, states = _hgrn_forward(p, lbl, gnorm_w, seq)
    dymix, dout, g_w_out, stats = _out_proj(y_lru, y_hgrn, w_out, x, tgt, post_w, seq)
    dp_lru, g_wa, g_wx, ls = _lru_backward(p, h, dymix, conv_w, conv_b, wa, wx, ba, bx, lam, seq)
    g_small = _pack_small(_shard_rows(g_wa, LRU_BLOCKS), _shard_rows(g_wx, LRU_BLOCKS),
                          _shard_rows(ls[0:4].reshape(4, D_MODEL, 1), 4).reshape(N_DEV, 4, 128),
                          _shard_rows(ls[5].reshape(4, LRU_BW, 1), 4).reshape(N_DEV, 4, 32),
                          _shard_rows(ls[6].reshape(4, LRU_BW, 1), 4).reshape(N_DEV, 4, 32))
    dp, hgrn_small, r_out, r_small = _hgrn_backward(
        p, o, states, dymix, lbl, gnorm_w, dp_lru, g_w_out.reshape(N_DEV, D_MIX // N_DEV, D_MODEL), g_small, seq)
    return dict(u=u, dp=dp, dout=dout, r_out=r_out, r_small=r_small,
                lru_small=ls, hgrn_small=hgrn_small, stats=stats)


class _TwoLevelGather:
    def __init__(self, ins, outs, send_sems, recv_sems, local_sems):
        self.ins, self.outs = ins, outs
        self.send_sems, self.recv_sems, self.local_sems = send_sems, recv_sems, local_sems
        x, y, c = _mesh_pos()
        self.c = c
        self.me, self.sibling = (x, y, c), (x, y, 1 - c)
        self.chips = [(1 - x, y), (x, 1 - y), (1 - x, 1 - y)]
        n = len(ins)
        self.mine = [pltpu.make_async_copy(ins[a], self._slot(a, self.me), local_sems.at[a]) for a in range(n)]
        self.first = []
        for a in range(n):
            self.first.append(self._copy(a, 0, self.me, self.sibling, src=ins[a]))
            self.first += [self._copy(a, 1 + j, self.me, (*chip, c), src=ins[a])
                           for j, chip in enumerate(self.chips)]
        self.passed = [self._copy(a, 4 + j, (*chip, c), self.sibling)
                       for j, chip in enumerate(self.chips) for a in range(n)]

    def _slot(self, a, pos):
        return self.outs[a].at[4 * pos[0] + 2 * pos[1] + pos[2]]

    def _copy(self, a, k, block, to, src=None):
        dst = self._slot(a, block)
        return pltpu.make_async_remote_copy(
            src_ref=dst if src is None else src, dst_ref=dst,
            send_sem=self.send_sems.at[a, k], recv_sem=self.recv_sems.at[a, k],
            device_id=to, device_id_type=MESH)

    def start(self):
        for cp in self.mine + self.first:
            cp.start()

    def forward(self):
        n = len(self.ins)
        for j, chip in enumerate(self.chips):
            for a in range(n):
                self._copy(a, 1 + j, (*chip, self.c), self.me).wait_recv()
                self.passed[j * n + a].start()

    def finish(self):
        for a in range(len(self.ins)):
            self._copy(a, 0, self.sibling, self.me).wait_recv()
            for j, chip in enumerate(self.chips):
                self._copy(a, 4 + j, (*chip, 1 - self.c), self.me).wait_recv()
        for cp in self.first + self.passed:
            cp.wait_send()
        for cp in self.mine:
            cp.wait()


W_IN_DIRECT = (1, 2, 4, 6)
W_IN_PASSED = (2, 4, 6)


def _in_proj_gather(x, pre_w, w_in_blk, w_out_blk, small_blk, me, seq):
    tm = min(1024, seq)
    nm = seq // tm
    last = N_DEV - 1

    def body(me_ref, x_ref, pw_ref, wblk_ref, woblk_ref, smblk_ref,
             p_ref, u_ref, wall_ref, woall_ref, small_ref,
             u_all, w_vmem, own_sem, d_send, d_recv, f_send, f_recv, wb_sems, g_send, g_recv, g_local):
        i = pl.program_id(0)
        m = pl.program_id(1)
        idx = me_ref[0]
        x_, y_, c_ = _mesh_pos()
        aux = _TwoLevelGather([woblk_ref, smblk_ref], [woall_ref, small_ref], g_send, g_recv, g_local)

        def peer(k):
            return (1 - x_ if (k >> 2) & 1 else x_, 1 - y_ if (k >> 1) & 1 else y_, 1 - c_ if k & 1 else c_)

        def direct(k):
            f = W_IN_DIRECT.index(k)
            return (pltpu.make_async_remote_copy(src_ref=wblk_ref, dst_ref=w_vmem.at[idx], send_sem=d_send.at[f],
                                                 recv_sem=d_recv.at[f], device_id=peer(k), device_id_type=MESH),
                    pltpu.make_async_remote_copy(src_ref=w_vmem.at[idx ^ k], dst_ref=w_vmem.at[idx ^ k],
                                                 send_sem=d_send.at[f], recv_sem=d_recv.at[f], device_id=peer(k),
                                                 device_id_type=MESH))

        def passed(k):
            f = W_IN_PASSED.index(k)
            return (pltpu.make_async_remote_copy(src_ref=w_vmem.at[idx ^ k], dst_ref=w_vmem.at[idx ^ k],
                                                 send_sem=f_send.at[f], recv_sem=f_recv.at[f], device_id=peer(1),
                                                 device_id_type=MESH),
                    pltpu.make_async_remote_copy(src_ref=w_vmem.at[idx ^ (k + 1)], dst_ref=w_vmem.at[idx ^ (k + 1)],
                                                 send_sem=f_send.at[f], recv_sem=f_recv.at[f], device_id=peer(1),
                                                 device_id_type=MESH))

        def write_back(k):
            return pltpu.make_async_copy(w_vmem.at[idx ^ k], wall_ref.at[idx ^ k], wb_sems.at[k])

        own = pltpu.make_async_copy(wblk_ref, w_vmem.at[idx], own_sem)

        @pl.when((i == 0) & (m == 0))
        def _():
            own.start()
            for k in W_IN_DIRECT:
                direct(k)[0].start()
            aux.start()
            own.wait()
            write_back(0).start()

        for k in range(1, N_DEV):
            @pl.when((i == k) & (m == 0))
            def _(k=k):
                if k in W_IN_DIRECT:
                    direct(k)[1].wait_recv()
                    if k in W_IN_PASSED:
                        passed(k)[0].start()
                else:
                    passed(k - 1)[1].wait_recv()
                write_back(k).start()

        @pl.when((i == N_CHIPS) & (m == 0))
        def _():
            aux.forward()

        rows = pl.ds(pl.multiple_of(m * tm, tm), tm)

        @pl.when(i == 0)
        def _():
            xv = x_ref[...]
            r = lax.rsqrt(jnp.mean(xv * xv, axis=-1, keepdims=True) + EPS)
            ub = (xv * r * pw_ref[...]).astype(BF16)
            u_all[rows, :] = ub
            u_ref[...] = ub

        p_ref[...] = _dot(u_all[rows, :], w_vmem[idx ^ i])

        @pl.when((i == last) & (m == nm - 1))
        def _():
            for k in W_IN_DIRECT:
                direct(k)[0].wait_send()
            for k in W_IN_PASSED:
                passed(k)[0].wait_send()
            for k in range(N_DEV):
                write_back(k).wait()
            aux.finish()

    first_pass = lambda i, m: jnp.where(i == 0, m, nm - 1)
    return pl.pallas_call(
        body, name="in_proj_gather",
        grid_spec=pltpu.PrefetchScalarGridSpec(
            num_scalar_prefetch=1, grid=(N_DEV, nm),
            in_specs=[pl.BlockSpec((tm, D_MODEL), lambda i, m, me: (first_pass(i, m), 0)),
                      pl.BlockSpec((1, D_MODEL), lambda i, m, me: (0, 0)), ANY, ANY, ANY],
            out_specs=[pl.BlockSpec((tm, W_BLK), lambda i, m, me: (m, me[0] ^ i)),
                       pl.BlockSpec((tm, D_MODEL), lambda i, m, me: (first_pass(i, m), 0)), ANY, ANY, ANY],
            scratch_shapes=[pltpu.VMEM((seq, D_MODEL), BF16), pltpu.VMEM((N_DEV, D_MODEL, W_BLK), BF16),
                            pltpu.SemaphoreType.DMA(()),
                            pltpu.SemaphoreType.DMA((len(W_IN_DIRECT),)), pltpu.SemaphoreType.DMA((len(W_IN_DIRECT),)),
                            pltpu.SemaphoreType.DMA((len(W_IN_PASSED),)), pltpu.SemaphoreType.DMA((len(W_IN_PASSED),)),
                            pltpu.SemaphoreType.DMA((N_DEV,)),
                            pltpu.SemaphoreType.DMA((2, 7)), pltpu.SemaphoreType.DMA((2, 7)),
                            pltpu.SemaphoreType.DMA((2,))]),
        out_shape=[SDS((seq, D_IN), F32), SDS((seq, D_MODEL), BF16), SDS((N_DEV, D_MODEL, W_BLK), BF16),
                   SDS((N_DEV,) + w_out_blk.shape, w_out_blk.dtype), SDS((N_DEV,) + small_blk.shape, small_blk.dtype)],
        compiler_params=_cparams(("arbitrary", "arbitrary")),
    )(me, x, pre_w, w_in_blk, w_out_blk, small_blk)


def _exchange_grads(blocks, repl):
    nb = len(blocks)
    n = nb + 1

    def body(*refs):
        ins, outs, sems = refs[:n], refs[n:2 * n], refs[2 * n:]
        exs = [_SlotExchange(ins[a], outs[a], *sems[3 * a:3 * a + 3], blocked=a < nb) for a in range(n)]
        for ex in exs:
            ex.start()
        for ex in exs:
            ex.wait()

    arrs = list(blocks) + [repl]
    shapes = [SDS(b.shape, b.dtype) for b in blocks] + [SDS((N_DEV,) + repl.shape, repl.dtype)]
    return pl.pallas_call(
        body, name="exchange_small", out_shape=shapes,
        in_specs=[ANY] * n, out_specs=[ANY] * n,
        scratch_shapes=EXCHANGE_SEMS * n,
    )(*arrs)


def _pack_rows(picks, name):
    arrs = [p[0] for p in picks]

    def body(*refs):
        out = refs[-1]
        out[...] = jnp.zeros_like(out)
        at = 0
        for ref, (_, row, rows, scale) in zip(refs[:-1], picks):
            out[at:at + rows, :] = ref[row:row + rows, :] * scale
            at += rows

    return pl.pallas_call(body, name=name, out_shape=SDS((8, D_MODEL), F32))(*arrs)


def _adamw(g, w, m, v):
    m2 = ADAM_B1 * m + (1.0 - ADAM_B1) * g
    v2 = ADAM_B2 * v + (1.0 - ADAM_B2) * (g * g)
    m_hat = m2 / (1.0 - ADAM_B1 ** ADAM_STEP)
    v_hat = v2 / (1.0 - ADAM_B2 ** ADAM_STEP)
    delta = -ADAM_LR * (m_hat / (jnp.sqrt(v_hat) + ADAM_EPS) + ADAM_WD * w)
    return delta, m2, v2


def _sum_slots(r_ref):
    g = r_ref[0]
    for s in range(1, r_ref.shape[0]):
        g = g + r_ref[s]
    return g


def _sum_adamw(recv, w, m, v, tr, name):
    rows, cols = w.shape

    def body(r_ref, w_ref, m_ref, v_ref, g_ref, d_ref, m2_ref, v2_ref):
        g = _sum_slots(r_ref)
        g_ref[...] = g
        d_ref[...], m2_ref[...], v2_ref[...] = _adamw(g, w_ref[...], m_ref[...], v_ref[...])

    blk = pl.BlockSpec((tr, cols), lambda i: (i, 0))
    return pl.pallas_call(
        body, name=name, grid=(rows // tr,),
        in_specs=[pl.BlockSpec((recv.shape[0], tr, cols), lambda i: (0, i, 0)), blk, blk, blk],
        out_specs=[blk] * 4, out_shape=[SDS((rows, cols), F32)] * 4,
        compiler_params=_cparams(("parallel",)),
    )(recv, w, m, v)


def _sum_adamw_pieces(recv, rows, ws, ms, vs, name, loss_row=None):
    n = len(ws)

    def body(r_ref, *refs):
        w_refs, m_refs, v_refs, outs = refs[:n], refs[n:2 * n], refs[2 * n:3 * n], refs[3 * n:]
        g = _sum_slots(r_ref)
        for i, (row, nrows) in enumerate(rows):
            gi = g[row:row + nrows, :]
            outs[i][...] = gi
            outs[n + i][...], outs[2 * n + i][...], outs[3 * n + i][...] = _adamw(
                gi, w_refs[i][...], m_refs[i][...], v_refs[i][...])
        if loss_row is not None:
            total = jnp.sum(g[loss_row:loss_row + 1, :], axis=-1, keepdims=True)
            outs[4 * n][...] = jnp.broadcast_to(total, outs[4 * n].shape)

    shapes = [SDS(w.shape, F32) for w in ws] * 4 + ([SDS((8, 128), F32)] if loss_row is not None else [])
    out = pl.pallas_call(body, name=name, out_shape=shapes)(recv, *ws, *ms, *vs)
    return [out[k * n:(k + 1) * n] for k in range(4)] + list(out[4 * n:])


def _shard_rows(t, lead):
    r = t.shape[1] // N_DEV
    t = t.reshape((lead, N_DEV, r) + t.shape[2:])
    return jnp.moveaxis(t, 1, 0)


def _pad8(t):
    return jnp.pad(t, ((0, 0), (0, 8 - t.shape[1]), (0, 0)))


def _pack_small(wa, wx, cw, b_a, b_x):
    n = wa.shape[0]
    return jnp.concatenate([
        wa.reshape(n, 256, 128), wx.reshape(n, 256, 128), _pad8(cw),
        _pad8(b_a.reshape(n, 1, 128)), _pad8(b_x.reshape(n, 1, 128))], axis=1)


def _unpack_small(t):
    n = t.shape[0]
    return (t[:, SM_WA:SM_WA + 256].reshape(n, 4, 32, 256), t[:, SM_WX:SM_WX + 256].reshape(n, 4, 32, 256),
            t[:, SM_CW:SM_CW + 4], t[:, SM_BA].reshape(n, 4, 32), t[:, SM_BX].reshape(n, 4, 32))


def kernel(x, pre_norm_w, w_in, conv_w, conv_b, lru_w_a, lru_b_a, lru_w_x, lru_b_x, lru_lambda, hgrn_lb_logits, hgrn_gnorm_w, w_out, post_norm_w, loss_target, m_pre_norm_w, m_w_in, m_conv_w, m_conv_b, m_lru_w_a, m_lru_b_a, m_lru_w_x, m_lru_b_x, m_lru_lambda, m_hgrn_lb_logits, m_hgrn_gnorm_w, m_w_out, m_post_norm_w, v_pre_norm_w, v_w_in, v_conv_w, v_conv_b, v_lru_w_a, v_lru_b_a, v_lru_w_x, v_lru_b_x, v_lru_lambda, v_hgrn_lb_logits, v_hgrn_gnorm_w, v_w_out, v_post_norm_w):
    seq = x.shape[1]
    x2 = x.reshape(seq, D_MODEL)
    tgt = loss_target.reshape(seq, D_MODEL)

    small_w = _pack_small(lru_w_a, lru_w_x, conv_w, lru_b_a, lru_b_x)[0]
    me = (4 * lax.axis_index("x") + 2 * lax.axis_index("y") + lax.axis_index("c")).astype(jnp.int32).reshape(1)
    p, u, w_in_all, w_out_all, small_all = _in_proj_gather(
        x2, pre_norm_w, w_in[0].astype(BF16), w_out[0].astype(BF16), small_w, me, seq)
    wa_s, wx_s, cw_s, ba_s, bx_s = _unpack_small(small_all)
    wa = jnp.moveaxis(wa_s, 0, 1).reshape(LRU_BLOCKS, LRU_BW, LRU_BW).astype(BF16)
    wx = jnp.moveaxis(wx_s, 0, 1).reshape(LRU_BLOCKS, LRU_BW, LRU_BW).astype(BF16)
    cw = jnp.moveaxis(cw_s, 0, 1).reshape(4, D_MODEL)
    ba = jnp.moveaxis(ba_s, 0, 1).reshape(1, D_MODEL)
    bx = jnp.moveaxis(bx_s, 0, 1).reshape(1, D_MODEL)

    loc = _local_step(x2, tgt, p, u, cw, conv_b, wa, wx, ba, bx, lru_lambda,
                      hgrn_lb_logits, hgrn_gnorm_w, w_out_all.reshape(D_MIX, D_MODEL), post_norm_w)

    ls, r_out, r_small = loc["lru_small"], loc["r_out"], loc["r_small"]
    core = lax.axis_index("c").astype(jnp.int32).reshape(1)
    g_sib = _grad_w_in_sibling(loc["u"], loc["dp"], core, seq)
    g_chip = _grad_w_in_own(loc["u"], loc["dp"], core, g_sib, seq)
    grad_x, pre_small, r_in = _grad_x(loc["dp"], w_in_all, x2, pre_norm_w, loc["dout"], g_chip, seq)
    g_repl = _pack_rows([(pre_small, 0, 1, 1.0), (ls, 4, 1, 1.0), (ls, 7, 1, 1.0),
                         (loc["hgrn_small"], 0, 1, 1.0), (loc["hgrn_small"], 0, 1, -1.0),
                         (loc["hgrn_small"], 1, 1, 1.0), (loc["stats"], 0, 2, 1.0)], "pack_grads")
    (r_repl,) = _exchange_grads([], g_repl)

    repl_rows = [(RP_PRE, 1), (RP_CB, 1), (RP_LAM, 1), (RP_LB0, 2), (RP_GN, 1), (RP_POST, 1)]
    o_repl = _sum_adamw_pieces(
        r_repl, repl_rows,
        [pre_norm_w, conv_b, lru_lambda, hgrn_lb_logits, hgrn_gnorm_w, post_norm_w],
        [m_pre_norm_w, m_conv_b, m_lru_lambda, m_hgrn_lb_logits, m_hgrn_gnorm_w, m_post_norm_w],
        [v_pre_norm_w, v_conv_b, v_lru_lambda, v_hgrn_lb_logits, v_hgrn_gnorm_w, v_post_norm_w],
        "adamw_repl", loss_row=RP_LOSS)
    loss = o_repl[4][0, 0]

    small_rows = [(SM_WA, 256), (SM_WX, 256), (SM_CW, 4), (SM_BA, 1), (SM_BX, 1)]
    as_rows = lambda wa_, wx_, cw_, ba_, bx_: [wa_.reshape(256, 128), wx_.reshape(256, 128), cw_.reshape(4, 128),
                                               ba_.reshape(1, 128), bx_.reshape(1, 128)]
    o_small = _sum_adamw_pieces(
        r_small, small_rows, as_rows(lru_w_a, lru_w_x, conv_w, lru_b_a, lru_b_x),
        as_rows(m_lru_w_a, m_lru_w_x, m_conv_w, m_lru_b_a, m_lru_b_x),
        as_rows(v_lru_w_a, v_lru_w_x, v_conv_w, v_lru_b_a, v_lru_b_x), "adamw_small")

    o_in = _sum_adamw(r_in, w_in[0], m_w_in[0], v_w_in[0], 128, "adamw_w_in")
    o_out = _sum_adamw(r_out, w_out[0], m_w_out[0], v_w_out[0], 64, "adamw_w_out")

    outs = [loss, grad_x.reshape(x.shape)]
    for kind in range(4):
        pre, cb, lam, lb, gn, post = o_repl[kind]
        swa, swx, scw, sba, sbx = o_small[kind]
        outs += [pre, o_in[kind][None], scw.reshape(conv_w.shape), cb, swa.reshape(lru_w_a.shape),
                 sba.reshape(lru_b_a.shape), swx.reshape(lru_w_x.shape), sbx.reshape(lru_b_x.shape),
                 lam, lb, gn, o_out[kind][None], post]
    return tuple(outs)
```

```python
import functools

import jax
import jax.numpy as jnp
from jax import lax
from jax.experimental import pallas as pl
from jax.experimental.pallas import tpu as pltpu

F32 = jnp.float32
BF16 = jnp.bfloat16
SDS = jax.ShapeDtypeStruct

D_MODEL = 1024
D_IN = 6144
N_DEV = 8
N_CHIPS = 4
W_BLK = D_IN // N_DEV
D_MIX = 2048
LRU_BLOCKS = 4
LRU_BW = 256
LRU_C = 8.0
N_HEADS = 8
HEAD_D = 128
CHUNK = 128
SUB = 32
N_SUB = CHUNK // SUB
HGRN_FWD_STEP_CHUNKS = 4
HGRN_STEP_CHUNKS = 2
EXP_CLAMP = 80.0
EPS = 1e-6

ADAM_LR = 0.001
ADAM_B1 = 0.9
ADAM_B2 = 0.999
ADAM_EPS = 1e-08
ADAM_WD = 0.01
ADAM_STEP = 10

VMEM_LIMIT = 56 * 1024 * 1024

NN = (((1,), (0,)), ((), ()))
NT = (((1,), (1,)), ((), ()))
TN = (((0,), (0,)), ((), ()))

SM_WA = 0
SM_WX = 256
SM_CW = 512
SM_BA = 520
SM_BX = 528
SM_ROWS = 536

RP_PRE, RP_CB, RP_LAM, RP_LB0, RP_LB1, RP_GN, RP_POST, RP_LOSS = range(8)


def _dot(a, b, dims=NN):
    return lax.dot_general(a, b, dims, preferred_element_type=F32)


def _sigmoid(x):
    return 0.5 * jnp.tanh(0.5 * x) + 0.5


def _sigmoid_pos(x):
    return 1.0 / (1.0 + jnp.exp(-x))


def _cparams(sem, vmem=VMEM_LIMIT):
    return pltpu.CompilerParams(dimension_semantics=sem, vmem_limit_bytes=vmem)


def _iota(shape, axis):
    return lax.broadcasted_iota(jnp.int32, shape, axis)


def _softplus_neg(lam):
    z = -lam
    e = jnp.exp(-jnp.abs(z))
    u = 1.0 + e
    log1p_e = jnp.where(u == 1.0, e, jnp.log(u) * (e / (u - 1.0)))
    sp = jnp.maximum(z, 0.0) + log1p_e
    dsp = -jnp.where(z >= 0.0, 1.0 / u, e / u)
    return sp, dsp


def _neg_expm1(x):
    poly = x * (1.0 + x * (1.0 / 2 + x * (1.0 / 6 + x * (1.0 / 24 + x * (1.0 / 120)))))
    return jnp.where(x > -1.0 / 16, -poly, 1.0 - jnp.exp(x))


def _conv_taps(lx, prev8, cw_ref, cb_ref, tile):
    xc = cb_ref[...] + cw_ref[3:4, :] * lx
    for j in (1, 2, 3):
        xc = xc + cw_ref[3 - j:4 - j, :] * pltpu.roll(lx, j, 0)
    row8 = _iota((8, D_MODEL), 0)
    last8 = lx[tile - 8:tile, :]
    fix = jnp.zeros((8, D_MODEL), F32)
    for j in (1, 2, 3):
        wrong = pltpu.roll(last8, j, 0)
        right = pltpu.roll(prev8, j, 0)
        fix = fix + cw_ref[3 - j:4 - j, :] * jnp.where(row8 < j, right - wrong, 0.0)
    return xc, fix


def _lru_gates(xcs, wa, wx, ba, bx, sp):
    xb = xcs.astype(BF16)
    r = _sigmoid_pos(_dot(xb, wa) + ba)
    i = _sigmoid(_dot(xb, wx) + bx)
    la = (-LRU_C * sp) * r
    a = jnp.exp(la)
    one_minus_a2 = _neg_expm1(2.0 * la)
    return r, i, a, one_minus_a2


def _lru_forward(p, conv_w, conv_b, wa, wx, ba, bx, lam, seq):
    tile = min(512, seq // 2)
    nblk = tile // 8

    def body(lx_ref, gt_ref, cw_ref, cb_ref, wa_ref, wx_ref, ba_ref, bx_ref, lam_ref,
             h_ref, y_ref, ext, hcar, xc_scr, a_scr, u_scr):
        @pl.when(pl.program_id(0) == 0)
        def _():
            ext[0:8, :] = jnp.zeros((8, D_MODEL), F32)
            hcar[...] = jnp.zeros_like(hcar)

        lx = lx_ref[...]
        ext[8:8 + tile, :] = lx
        xc = cb_ref[...] + cw_ref[3:4, :] * lx
        for j in (1, 2, 3):
            xc = xc + cw_ref[3 - j:4 - j, :] * ext[8 - j:8 - j + tile, :]
        xc_scr[...] = xc
        ext[0:8, :] = lx_ref[tile - 8:tile, :]
        sp, _ = _softplus_neg(lam_ref[...])
        for n in range(LRU_BLOCKS):
            sl = slice(n * LRU_BW, (n + 1) * LRU_BW)
            xcs = xc_scr[:, sl]
            _, i, a, ne = _lru_gates(xcs, wa_ref[n], wx_ref[n], ba_ref[:, sl], bx_ref[:, sl], sp[:, sl])
            a_scr[:, sl] = a
            u_scr[:, sl] = jnp.sqrt(ne) * (i * xcs)

        row8 = _iota((8, D_MODEL), 0)

        def blk(j, hc):
            off = pl.multiple_of(j * 8, 8)
            a = a_scr[pl.ds(off, 8), :]
            u = u_scr[pl.ds(off, 8), :]
            for k in (1, 2, 4):
                m = row8 >= k
                u = jnp.where(m, u + a * pltpu.roll(u, k, 0), u)
                a = jnp.where(m, a * pltpu.roll(a, k, 0), a)
            h = u + a * hc
            h_ref[pl.ds(off, 8), :] = h
            return jnp.broadcast_to(h[7:8, :], (8, D_MODEL))

        hcar[...] = lax.fori_loop(0, nblk, blk, hcar[...])
        g = gt_ref[...]
        y_ref[...] = (h_ref[...] * (g * _sigmoid(g))).astype(BF16)

    full = lambda shape: pl.BlockSpec(shape, lambda t: (0,) * len(shape))
    return pl.pallas_call(
        body, name="lru_fwd", grid=(seq // tile,),
        in_specs=[pl.BlockSpec((tile, D_MODEL), lambda t: (t, 0)),
                  pl.BlockSpec((tile, D_MODEL), lambda t: (t, 1)),
                  full((4, D_MODEL)), full((1, D_MODEL)),
                  full((LRU_BLOCKS, LRU_BW, LRU_BW)), full((LRU_BLOCKS, LRU_BW, LRU_BW)),
                  full((1, D_MODEL)), full((1, D_MODEL)), full((1, D_MODEL))],
        out_specs=[pl.BlockSpec((tile, D_MODEL), lambda t: (t, 0)),
                   pl.BlockSpec((tile, D_MODEL), lambda t: (t, 0))],
        out_shape=[SDS((seq, D_MODEL), F32), SDS((seq, D_MODEL), BF16)],
        scratch_shapes=[pltpu.VMEM((tile + 8, D_MODEL), F32), pltpu.VMEM((8, D_MODEL), F32),
                        pltpu.VMEM((tile, D_MODEL), F32), pltpu.VMEM((tile, D_MODEL), F32),
                        pltpu.VMEM((tile, D_MODEL), F32)],
        compiler_params=_cparams(("arbitrary",)),
    )(p, p, conv_w, conv_b, wa, wx, ba, bx, lam)


def _lru_backward(p, h, dymix, conv_w, conv_b, wa, wx, ba, bx, lam, seq):
    tile = min(256, seq // 2)
    nt = seq // tile
    nblk = tile // 8
    t8 = tile // 8

    def body(lx_ref, lxh_ref, gt_ref, h_ref, hh_ref, dy_ref, cw_ref, cb_ref, wa_ref, wx_ref, ba_ref,
             bx_ref, lam_ref, dp_ref, gwa_ref, gwx_ref, gsm_ref,
             lamcar, anext, dxc8, xc_scr, r_scr, i_scr, a_scr, m_scr, rm_scr, c_scr, l_scr, dxc_scr):
        step = pl.program_id(0)
        first_tile = step == nt - 1

        @pl.when(step == 0)
        def _():
            lamcar[...] = jnp.zeros_like(lamcar)
            anext[...] = jnp.zeros_like(anext)
            dxc8[...] = jnp.zeros_like(dxc8)
            gwa_ref[...] = jnp.zeros_like(gwa_ref)
            gwx_ref[...] = jnp.zeros_like(gwx_ref)
            gsm_ref[...] = jnp.zeros_like(gsm_ref)

        keep = jnp.where(first_tile, 0.0, 1.0)
        lx = lx_ref[...]
        prev8 = lxh_ref[...] * keep
        xc, fix = _conv_taps(lx, prev8, cw_ref, cb_ref, tile)
        xc_scr[...] = xc
        xc_scr[0:8, :] = xc_scr[0:8, :] + fix
        sp, dsp = _softplus_neg(lam_ref[...])
        for n in range(LRU_BLOCKS):
            sl = slice(n * LRU_BW, (n + 1) * LRU_BW)
            r, i, a, ne = _lru_gates(xc_scr[:, sl], wa_ref[n], wx_ref[n], ba_ref[:, sl], bx_ref[:, sl],
                                     sp[:, sl])
            r_scr[:, sl] = r
            i_scr[:, sl] = i
            a_scr[:, sl] = a
            m_scr[:, sl] = jnp.sqrt(ne)
            rm_scr[:, sl] = lax.rsqrt(ne)

        g = gt_ref[...]
        sg = _sigmoid(g)
        dy = dy_ref[...]
        hv = h_ref[...]
        dp_ref[:, D_MODEL:2 * D_MODEL] = (dy * hv * (sg * (1.0 + g * (1.0 - sg)))).astype(BF16)

        rowt = _iota((tile, D_MODEL), 0)
        av = a_scr[...]
        l_scr[...] = dy * (g * sg)
        c_scr[...] = jnp.where(rowt == tile - 1, anext[...][0:1, :], pltpu.roll(av, tile - 1, 0))
        anext[...] = jnp.broadcast_to(av[0:1, :], (8, D_MODEL))
        row8 = _iota((8, D_MODEL), 0)

        def blk(jj, lc):
            off = pl.multiple_of((nblk - 1 - jj) * 8, 8)
            c = c_scr[pl.ds(off, 8), :]
            u = l_scr[pl.ds(off, 8), :]
            for k in (1, 2, 4):
                m = row8 < 8 - k
                u = jnp.where(m, u + c * pltpu.roll(u, 8 - k, 0), u)
                c = jnp.where(m, c * pltpu.roll(c, 8 - k, 0), c)
            lamv = u + c * lc
            l_scr[pl.ds(off, 8), :] = lamv
            return jnp.broadcast_to(lamv[0:1, :], (8, D_MODEL))

        lamcar[...] = lax.fori_loop(0, nblk, blk, lamcar[...])

        hprev = jnp.where(rowt == 0, hh_ref[...][7:8, :] * keep, pltpu.roll(hv, 1, 0))
        for n in range(LRU_BLOCKS):
            sl = slice(n * LRU_BW, (n + 1) * LRU_BW)
            lamv = l_scr[:, sl]
            xcs = xc_scr[:, sl]
            r = r_scr[:, sl]
            i = i_scr[:, sl]
            a = a_scr[:, sl]
            mult = m_scr[:, sl]
            d_la = lamv * hprev[:, sl] * a - (lamv * i * xcs) * (a * a * rm_scr[:, sl])
            d_pr = d_la * (-LRU_C * sp[:, sl]) * r * (1.0 - r)
            d_pi = (lamv * mult * xcs) * i * (1.0 - i)
            gsm_ref[7:8, sl] += jnp.sum(d_la * r, axis=0, keepdims=True) * (-LRU_C) * dsp[:, sl]
            gsm_ref[5:6, sl] += jnp.sum(d_pr, axis=0, keepdims=True)
            gsm_ref[6:7, sl] += jnp.sum(d_pi, axis=0, keepdims=True)
            xb = xcs.astype(BF16)
            prb = d_pr.astype(BF16)
            pib = d_pi.astype(BF16)
            gwa_ref[n] += _dot(xb, prb, TN)
            gwx_ref[n] += _dot(xb, pib, TN)
            dxc_scr[:, sl] = lamv * mult * i + _dot(prb, wa_ref[n], NT) + _dot(pib, wx_ref[n], NT)

        dxc = dxc_scr[...]
        gsm_ref[4:5, :] += jnp.sum(dxc, axis=0, keepdims=True)
        last8 = lx[tile - 8:tile, :]
        first8 = dxc[0:8, :]
        dlx = cw_ref[3:4, :] * dxc
        gsm_ref[3:4, :] += jnp.sum(dxc * lx, axis=0, keepdims=True)
        fix = jnp.zeros((8, D_MODEL), F32)
        for j in (1, 2, 3):
            w = cw_ref[3 - j:4 - j, :]
            dlx = dlx + w * pltpu.roll(dxc, tile - j, 0)
            fix = fix + w * jnp.where(row8 + j >= 8,
                                      pltpu.roll(dxc8[...], 8 - j, 0) - pltpu.roll(first8, 8 - j, 0), 0.0)
            halo = jnp.where(row8 < j, pltpu.roll(prev8, j, 0) - pltpu.roll(last8, j, 0), 0.0)
            gsm_ref[3 - j:4 - j, :] += (jnp.sum(dxc * pltpu.roll(lx, j, 0), axis=0, keepdims=True)
                                        + jnp.sum(first8 * halo, axis=0, keepdims=True))
        dxc8[...] = first8
        dp_ref[:, 0:D_MODEL] = dlx.astype(BF16)
        top = tile - 8
        dp_ref[top:tile, 0:D_MODEL] = (dlx[top:tile, :] + fix).astype(BF16)

    rev = lambda t: (nt - 1 - t, 0)
    halo_idx = lambda t: (jnp.maximum((nt - 1 - t) * t8 - 1, 0), 0)
    full = lambda shape: pl.BlockSpec(shape, lambda t: (0,) * len(shape))
    big = lambda: pltpu.VMEM((tile, D_MODEL), F32)
    return pl.pallas_call(
        body, name="lru_bwd", grid=(nt,),
        in_specs=[pl.BlockSpec((tile, D_MODEL), rev),
                  pl.BlockSpec((8, D_MODEL), halo_idx),
                  pl.BlockSpec((tile, D_MODEL), lambda t: (nt - 1 - t, 1)),
                  pl.BlockSpec((tile, D_MODEL), rev),
                  pl.BlockSpec((8, D_MODEL), halo_idx),
                  pl.BlockSpec((tile, D_MODEL), rev),
                  full((4, D_MODEL)), full((1, D_MODEL)),
                  full((LRU_BLOCKS, LRU_BW, LRU_BW)), full((LRU_BLOCKS, LRU_BW, LRU_BW)),
                  full((1, D_MODEL)), full((1, D_MODEL)), full((1, D_MODEL))],
        out_specs=[pl.BlockSpec((tile, 2 * D_MODEL), rev),
                   full((LRU_BLOCKS, LRU_BW, LRU_BW)), full((LRU_BLOCKS, LRU_BW, LRU_BW)),
                   full((8, D_MODEL))],
        out_shape=[SDS((seq, D_IN), BF16), SDS((LRU_BLOCKS, LRU_BW, LRU_BW), F32),
                   SDS((LRU_BLOCKS, LRU_BW, LRU_BW), F32), SDS((8, D_MODEL), F32)],
        scratch_shapes=[pltpu.VMEM((8, D_MODEL), F32), pltpu.VMEM((8, D_MODEL), F32),
                        pltpu.VMEM((8, D_MODEL), F32)] + [big() for _ in range(9)],
        compiler_params=_cparams(("arbitrary",)),
    )(p, p, p, h, h, dymix, conv_w, conv_b, wa, wx, ba, bx, lam)


def _tri_matmul(tri, g):
    hi = g.astype(BF16)
    lo = (g - hi.astype(F32)).astype(BF16)
    return _dot(tri, lo) + _dot(tri, hi)


def _hgrn_gate_terms(q, fr, lbl):
    lb = _sigmoid_pos(lbl[0:1, :] - lbl[1:2, :])
    half = 0.5 * (1.0 - lb)
    tf = jnp.tanh(0.5 * fr)
    f = (lb + half) + half * tf
    hq = 0.5 * q
    tq = jnp.tanh(hq)
    return lb, tf, f, tq, hq * tq + hq


def _hgrn_decay(bh):
    zero = jnp.zeros((1, bh.shape[1]), F32)
    rho = [zero] + [bh[s * SUB - 1:s * SUB, :] for s in range(1, N_SUB + 1)]
    start = _sub_rows(rho[0:N_SUB])
    end = _sub_rows(rho[1:N_SUB + 1])
    mid = 0.5 * (start + end)
    blast = rho[N_SUB]
    e_on = jnp.exp(bh - start)
    e_off = jnp.exp(end - bh)
    scales = [_sub_rows([jnp.exp(rho[i] - rho[j + 1]) if i > j else zero for i in range(N_SUB)])
              for j in range(N_SUB - 1)]
    return dict(eq0=jnp.exp(jnp.minimum(bh - mid, EXP_CLAMP)), ek0=jnp.exp(jnp.minimum(mid - bh, EXP_CLAMP)),
                e_on=e_on, e_off=e_off, scales=scales,
                eb=e_on * _sub_rows([jnp.exp(r) for r in rho[0:N_SUB]]),
                ekst=e_off * _sub_rows([jnp.exp(blast - r) for r in rho[1:N_SUB + 1]]),
                ebl=jnp.exp(blast))


def _sub_rows(vecs):
    return jnp.concatenate([jnp.broadcast_to(v, (SUB, v.shape[1])) for v in vecs], axis=0)


def _hgrn_operands(qs, k, dec, qt_scr, kt_scr):
    sub = jnp.right_shift(_iota(qs.shape, 0), 5)
    qon = qs * dec["e_on"]
    koff = k * dec["e_off"]
    qt_scr[0] = (qs * dec["eq0"]).astype(BF16)
    kt_scr[0] = (k * dec["ek0"]).astype(BF16)
    for j in range(N_SUB - 1):
        qt_scr[j + 1] = (qon * dec["scales"][j]).astype(BF16)
        kt_scr[j + 1] = jnp.where(sub == j, koff, 0.0).astype(BF16)
    return koff


def _hgrn_head_scores(qt_scr, kt_scr, sl, diag):
    a = jnp.where(diag, _dot(qt_scr[0, :, sl], kt_scr[0, :, sl], NT), 0.0)
    for j in range(1, N_SUB):
        a = a + _dot(qt_scr[j, :, sl], kt_scr[j, :, sl], NT)
    return a


def _hgrn_forward(p, lbl, gw, seq):
    nc = seq // CHUNK
    assert SUB == 32

    def body(q_ref, f_ref, v_ref, hg_ref, lbl_ref, gw_ref, y_ref, o_ref, st_ref,
             s_scr, qt_scr, kt_scr, qin_scr, kst_scr, vb_scr, a_scr):
        @pl.when(pl.program_id(0) == 0)
        def _():
            s_scr[...] = jnp.zeros_like(s_scr)

        r = _iota((CHUNK, CHUNK), 0)
        c = _iota((CHUNK, CHUNK), 1)
        tri = jnp.where(c <= r, 1.0, 0.0).astype(BF16)
        diag = (jnp.right_shift(r, 5) == jnp.right_shift(c, 5)) & (c <= r)
        heads = [slice(h * HEAD_D, (h + 1) * HEAD_D) for h in range(N_HEADS)]
        for cc in range(HGRN_FWD_STEP_CHUNKS):
            rows = slice(cc * CHUNK, (cc + 1) * CHUNK)
            q = q_ref[rows, :]
            _, _, f, _, qs = _hgrn_gate_terms(q, f_ref[rows, :], lbl_ref[...])
            k = 1.0 - f
            dec = _hgrn_decay(_tri_matmul(tri, jnp.log(f)))
            _hgrn_operands(qs, k, dec, qt_scr, kt_scr)
            qin_scr[...] = (qs * dec["eb"]).astype(BF16)
            kst_scr[...] = (k * dec["ekst"]).astype(BF16)
            vb_scr[...] = v_ref[rows, :].astype(BF16)
            ebl = dec["ebl"]
            hg = hg_ref[rows, :]
            gate = gw_ref[...] * (hg * _sigmoid(hg))
            stb = []
            for h, sl in enumerate(heads):
                st = s_scr[h]
                st_ref[cc, h] = st
                stb.append(st.astype(BF16))
                s_scr[h] = st * ebl[:, sl] + _dot(vb_scr[:, sl], kst_scr[:, sl], TN)
            for h, sl in enumerate(heads):
                a_scr[h] = _hgrn_head_scores(qt_scr, kt_scr, sl, diag).astype(BF16)
            for h, sl in enumerate(heads):
                o = _dot(a_scr[h], vb_scr[:, sl]) + _dot(qin_scr[:, sl], stb[h], NT)
                o_ref[rows, sl] = o
                rs = lax.rsqrt(jnp.mean(o * o, axis=-1, keepdims=True) + EPS)
                y_ref[rows, sl] = ((o * rs) * gate[:, sl]).astype(BF16)

    col = lambda j: pl.BlockSpec((HGRN_FWD_STEP_CHUNKS * CHUNK, D_MODEL), lambda c: (c, j))
    par = lambda rows: pl.BlockSpec((rows, D_MODEL), lambda c: (0, 0))
    return pl.pallas_call(
        body, name="hgrn_fwd", grid=(nc // HGRN_FWD_STEP_CHUNKS,),
        in_specs=[col(2), col(3), col(4), col(5), par(2), par(1)],
        out_specs=[col(0), col(0),
                   pl.BlockSpec((HGRN_FWD_STEP_CHUNKS, N_HEADS, HEAD_D, HEAD_D), lambda c: (c, 0, 0, 0))],
        out_shape=[SDS((seq, D_MODEL), BF16), SDS((seq, D_MODEL), F32),
                   SDS((nc, N_HEADS, HEAD_D, HEAD_D), F32)],
        scratch_shapes=[pltpu.VMEM((N_HEADS, HEAD_D, HEAD_D), F32),
                        pltpu.VMEM((N_SUB, CHUNK, D_MODEL), BF16), pltpu.VMEM((N_SUB, CHUNK, D_MODEL), BF16)]
                       + [pltpu.VMEM((CHUNK, D_MODEL), BF16)] * 3 + [pltpu.VMEM((N_HEADS, CHUNK, CHUNK), BF16)],
        compiler_params=_cparams(("arbitrary",)),
    )(p, p, p, p, lbl, gw)


def _hgrn_backward(p, o, states, dymix, lbl, gw, dp_full, g_w_out, g_small, seq):
    step_rows = HGRN_STEP_CHUNKS * CHUNK
    ns = seq // step_rows

    def body(q_ref, f_ref, v_ref, hg_ref, o_ref, st_ref, dy_ref, lbl_ref, gw_ref, dpin_ref, go_ref, gs_ref,
             dpo_ref, gsm_ref, ro_ref, rs_ref, ds_scr, dp_buf, dp_sems, *rest):
        del dpin_ref
        scratch, sems = rest[:14], rest[14:]
        step = pl.program_id(0)
        slot = step % 2
        exs = [_SlotExchange(go_ref, ro_ref, *sems[0:3], blocked=True),
               _SlotExchange(gs_ref, rs_ref, *sems[3:6], blocked=True)]

        @pl.when(step == 0)
        def _():
            for ex in exs:
                ex.start()

        def out_copy(s, blk):
            rows = pl.ds(pl.multiple_of(blk * step_rows, step_rows), step_rows)
            return pltpu.make_async_copy(dp_buf.at[s], dpo_ref.at[rows, pl.ds(2 * D_MODEL, 4 * D_MODEL)],
                                         dp_sems.at[s])

        @pl.when(step == 0)
        def _():
            ds_scr[...] = jnp.zeros_like(ds_scr)
            gsm_ref[...] = jnp.zeros_like(gsm_ref)

        @pl.when(step >= 2)
        def _():
            out_copy(slot, ns + 1 - step).wait()

        for cc in reversed(range(HGRN_STEP_CHUNKS)):
            chunk(cc, q_ref, f_ref, v_ref, hg_ref, o_ref, st_ref, dy_ref, lbl_ref, gw_ref, gsm_ref, ds_scr,
                  dp_buf.at[slot], *scratch)

        out_copy(slot, ns - 1 - step).start()

        @pl.when(step == ns - 1)
        def _():
            out_copy(1 - slot, 1).wait()
            out_copy(slot, 0).wait()
            for ex in exs:
                ex.wait()

    def chunk(cc, q_ref, f_ref, v_ref, hg_ref, o_ref, st_ref, dy_ref, lbl_ref, gw_ref, gsm_ref, ds_scr, dp_ref,
              qt_scr, kt_scr, qin_scr, kst_scr, vb_scr, dob_scr, g_scr, h_scr, dqi_scr, dks_scr, sd_scr,
              a_scr, da_scr, da0_scr):
        rows = slice(cc * CHUNK, (cc + 1) * CHUNK)
        r = _iota((CHUNK, CHUNK), 0)
        c = _iota((CHUNK, CHUNK), 1)
        tri = jnp.where(c <= r, 1.0, 0.0).astype(BF16)
        triu = jnp.where(c >= r, 1.0, 0.0).astype(BF16)
        diag = (jnp.right_shift(r, 5) == jnp.right_shift(c, 5)) & (c <= r)
        row = _iota((CHUNK, D_MODEL), 0)
        sub = jnp.right_shift(row, 5)

        q = q_ref[rows, :]
        lb, tf, f, tq, qs = _hgrn_gate_terms(q, f_ref[rows, :], lbl_ref[...])
        sig = 0.5 * tf + 0.5
        sq = 0.5 * tq + 0.5
        k = 1.0 - f
        dec = _hgrn_decay(_tri_matmul(tri, jnp.log(f)))
        eb, ekst, ebl = dec["eb"], dec["ekst"], dec["ebl"]
        koff = _hgrn_operands(qs, k, dec, qt_scr, kt_scr)
        qin_scr[...] = (qs * eb).astype(BF16)
        kst_scr[...] = (k * ekst).astype(BF16)
        vb_scr[...] = v_ref[rows, :].astype(BF16)
        hg = hg_ref[rows, :]
        sh = _sigmoid(hg)
        dy = dy_ref[rows, :]
        gwv = gw_ref[...]
        d_onw = dy * (hg * sh)
        d_on = d_onw * gwv
        d_gate = dy * gwv * (sh * (1.0 + hg * (1.0 - sh)))

        heads = [slice(h * HEAD_D, (h + 1) * HEAD_D) for h in range(N_HEADS)]
        for h, sl in enumerate(heads):
            o = o_ref[rows, sl]
            rs = lax.rsqrt(jnp.mean(o * o, axis=-1, keepdims=True) + EPS)
            on = o * rs
            dp_ref[rows, 3 * D_MODEL + h * HEAD_D:3 * D_MODEL + (h + 1) * HEAD_D] = (d_gate[:, sl] * on).astype(BF16)
            gsm_ref[1:2, sl] += jnp.sum(d_onw[:, sl] * on, axis=0, keepdims=True)
            d_onh = d_on[:, sl]
            dob_scr[:, sl] = (rs * (d_onh - on * jnp.mean(d_onh * on, axis=-1, keepdims=True))).astype(BF16)
        for h, sl in enumerate(heads):
            a_scr[h] = _hgrn_head_scores(qt_scr, kt_scr, sl, diag).astype(BF16)
            da = _dot(dob_scr[:, sl], vb_scr[:, sl], NT)
            da_scr[h] = da.astype(BF16)
            da0_scr[h] = jnp.where(diag, da, 0.0).astype(BF16)
        for h, sl in enumerate(heads):
            st = st_ref[cc, h]
            dst = ds_scr[h]
            dstb = dst.astype(BF16)
            dp_ref[rows, 2 * D_MODEL + h * HEAD_D:2 * D_MODEL + (h + 1) * HEAD_D] = (
                _dot(a_scr[h], dob_scr[:, sl], TN) + _dot(kst_scr[:, sl], dstb, NT)).astype(BF16)
            dqi_scr[:, sl] = _dot(dob_scr[:, sl], st.astype(BF16))
            dks_scr[:, sl] = _dot(vb_scr[:, sl], dstb)
            sd_scr[0:1, sl] = jnp.sum(st * dst, axis=0, keepdims=True)
            ds_scr[h] = dst * ebl[:, sl] + _dot(dob_scr[:, sl], qin_scr[:, sl], TN)
        for h, sl in enumerate(heads):
            g_scr[0, :, sl] = _dot(da0_scr[h], kt_scr[0, :, sl])
            h_scr[0, :, sl] = _dot(da0_scr[h], qt_scr[0, :, sl], TN)
            for j in range(1, N_SUB):
                g_scr[j, :, sl] = _dot(da_scr[h], kt_scr[j, :, sl])
                h_scr[j, :, sl] = _dot(da_scr[h], qt_scr[j, :, sl], TN)

        g0 = g_scr[0]
        h0 = h_scr[0]
        dq_inter = eb * dqi_scr[...]
        d_kst = ekst * dks_scr[...]
        db = qs * dq_inter - k * d_kst + qt_scr[0].astype(F32) * g0 - kt_scr[0].astype(F32) * h0
        gq = jnp.zeros((CHUNK, D_MODEL), F32)
        hsel = jnp.zeros((CHUNK, D_MODEL), F32)
        for j in range(N_SUB - 1):
            gj = g_scr[j + 1]
            gq = gq + dec["scales"][j] * gj
            db = db + qt_scr[j + 1].astype(F32) * gj
            hsel = jnp.where(sub == j, h_scr[j + 1], hsel)
        db = db - koff.astype(BF16).astype(F32) * hsel
        d_q = dec["eq0"] * g0 + dec["e_on"] * gq + dq_inter
        d_k = dec["ek0"] * h0 + dec["e_off"] * hsel + d_kst
        db_last = jnp.sum(k * d_kst, axis=0, keepdims=True) + ebl * sd_scr[0:1, :]
        db = db + jnp.where(row == CHUNK - 1, db_last, 0.0)
        dg = _tri_matmul(triu, db)
        d_f = dg / f - d_k
        dp_ref[rows, D_MODEL:2 * D_MODEL] = (d_f * (1.0 - lb) * sig * (1.0 - sig)).astype(BF16)
        gsm_ref[0:1, :] += jnp.sum(d_f * (1.0 - sig), axis=0, keepdims=True) * (lb * (1.0 - lb))
        dp_ref[rows, 0:D_MODEL] = (d_q * (sq * (1.0 + q * (1.0 - sq)))).astype(BF16)

    rc = lambda c: ns - 1 - c
    col = lambda j: pl.BlockSpec((step_rows, D_MODEL), lambda c: (rc(c), j))
    par = lambda rows: pl.BlockSpec((rows, D_MODEL), lambda c: (0, 0))
    return pl.pallas_call(
        body, name="hgrn_bwd", grid=(ns,),
        in_specs=[col(2), col(3), col(4), col(5), col(0),
                  pl.BlockSpec((HGRN_STEP_CHUNKS, N_HEADS, HEAD_D, HEAD_D), lambda c: (rc(c), 0, 0, 0)),
                  col(1), par(2), par(1), ANY, ANY, ANY],
        out_specs=[ANY, par(8), ANY, ANY],
        out_shape=[SDS((seq, D_IN), BF16), SDS((8, D_MODEL), F32), SDS(g_w_out.shape, F32),
                   SDS(g_small.shape, F32)],
        input_output_aliases={9: 0},
        scratch_shapes=[pltpu.VMEM((N_HEADS, HEAD_D, HEAD_D), F32),
                        pltpu.VMEM((2, step_rows, 4 * D_MODEL), BF16), pltpu.SemaphoreType.DMA((2,)),
                        pltpu.VMEM((N_SUB, CHUNK, D_MODEL), BF16), pltpu.VMEM((N_SUB, CHUNK, D_MODEL), BF16)]
                       + [pltpu.VMEM((CHUNK, D_MODEL), BF16)] * 4
                       + [pltpu.VMEM((N_SUB, CHUNK, D_MODEL), F32)] * 2 + [pltpu.VMEM((CHUNK, D_MODEL), F32)] * 2
                       + [pltpu.VMEM((8, D_MODEL), F32)] + [pltpu.VMEM((N_HEADS, CHUNK, CHUNK), BF16)] * 3
                       + EXCHANGE_SEMS * 2,
        compiler_params=_cparams(("arbitrary",)),
    )(p, p, p, p, o, states, dymix, lbl, gw, dp_full, g_w_out, g_small)


def _out_proj(yl, yh, wo, x, tgt, post_w, seq):
    tm = 512

    def body(yl_ref, yh_ref, wo_ref, x_ref, tg_ref, pw_ref, dymix_ref, dout_ref, gwo_ref, st_ref):
        @pl.when(pl.program_id(0) == 0)
        def _():
            gwo_ref[...] = jnp.zeros_like(gwo_ref)
            st_ref[...] = jnp.zeros_like(st_ref)

        ylv = yl_ref[...]
        yhv = yh_ref[...]
        y = _dot(ylv, wo_ref[0:D_MODEL, :]) + _dot(yhv, wo_ref[D_MODEL:D_MIX, :])
        r2 = lax.rsqrt(jnp.mean(y * y, axis=-1, keepdims=True) + EPS)
        yn = y * r2
        pw = pw_ref[...]
        e = (x_ref[...] + yn * pw) - tg_ref[...]
        st_ref[1:2, :] += jnp.sum(e * e, axis=0, keepdims=True) * (0.5 / D_MODEL)
        dout = e * (1.0 / D_MODEL)
        dout_ref[...] = dout
        st_ref[0:1, :] += jnp.sum(dout * yn, axis=0, keepdims=True)
        dyn = dout * pw
        dy = r2 * (dyn - yn * jnp.mean(dyn * yn, axis=-1, keepdims=True))
        dyb = dy.astype(BF16)
        dymix_ref[...] = _dot(dyb, wo_ref[...], NT)
        gwo_ref[0:D_MODEL, :] += _dot(ylv, dyb, TN)
        gwo_ref[D_MODEL:D_MIX, :] += _dot(yhv, dyb, TN)

    row = lambda w: pl.BlockSpec((tm, w), lambda m: (m, 0))
    full = lambda shape: pl.BlockSpec(shape, lambda m: (0,) * len(shape))
    once = lambda shape: pl.BlockSpec(shape, lambda m: (0,) * len(shape), pipeline_mode=pl.Buffered(1))
    return pl.pallas_call(
        body, name="out_proj", grid=(seq // tm,),
        in_specs=[row(D_MODEL), row(D_MODEL), once((D_MIX, D_MODEL)), row(D_MODEL), row(D_MODEL),
                  full((1, D_MODEL))],
        out_specs=[row(D_MIX), row(D_MODEL), once((D_MIX, D_MODEL)), full((8, D_MODEL))],
        out_shape=[SDS((seq, D_MIX), F32), SDS((seq, D_MODEL), F32), SDS((D_MIX, D_MODEL), F32),
                   SDS((8, D_MODEL), F32)],
        compiler_params=_cparams(("arbitrary",)),
    )(yl, yh, wo, x, tgt, post_w)


MESH = pl.DeviceIdType.MESH
ANY = pl.BlockSpec(memory_space=pl.ANY)
EXCHANGE_SEMS = [pltpu.SemaphoreType.DMA((N_DEV - 1,)), pltpu.SemaphoreType.DMA((N_DEV - 1,)),
                 pltpu.SemaphoreType.DMA(())]


def _mesh_pos():
    return lax.axis_index("x"), lax.axis_index("y"), lax.axis_index("c")


class _SlotExchange:
    def __init__(self, src_ref, dst_ref, send_sems, recv_sems, local_sem, blocked):
        x, y, c = _mesh_pos()
        me = 4 * x + 2 * y + c
        src = (lambda dest: src_ref.at[dest]) if blocked else (lambda dest: src_ref)
        self.local = pltpu.make_async_copy(src(me), dst_ref.at[me], local_sem)
        self.sends, self.recvs = [], []
        for k in range(1, N_DEV):
            px = 1 - x if (k >> 2) & 1 else x
            py = 1 - y if (k >> 1) & 1 else y
            pc = 1 - c if k & 1 else c
            peer = 4 * px + 2 * py + pc
            sems = dict(send_sem=send_sems.at[k - 1], recv_sem=recv_sems.at[k - 1],
                        device_id=(px, py, pc), device_id_type=MESH)
            self.sends.append(pltpu.make_async_remote_copy(src_ref=src(peer), dst_ref=dst_ref.at[me], **sems))
            self.recvs.append(pltpu.make_async_remote_copy(src_ref=dst_ref.at[peer], dst_ref=dst_ref.at[peer], **sems))

    def start(self):
        self.local.start()
        for cp in self.sends:
            cp.start()

    def wait(self):
        for cp in self.recvs:
            cp.wait_recv()
        for cp in self.sends:
            cp.wait_send()
        self.local.wait()


class _ChipExchange:
    def __init__(self, src_ref, dst_ref, send_sems, recv_sems, local_sem):
        x, y, c = _mesh_pos()
        chip = 2 * x + y
        self.local = pltpu.make_async_copy(src_ref.at[chip], dst_ref.at[chip], local_sem)
        self.sends, self.recvs = [], []
        for k in range(1, N_CHIPS):
            px = 1 - x if (k >> 1) & 1 else x
            py = 1 - y if k & 1 else y
            peer = 2 * px + py
            sems = dict(send_sem=send_sems.at[k - 1], recv_sem=recv_sems.at[k - 1],
                        device_id=(px, py, c), device_id_type=MESH)
            self.sends.append(pltpu.make_async_remote_copy(src_ref=src_ref.at[peer], dst_ref=dst_ref.at[chip], **sems))
            self.recvs.append(pltpu.make_async_remote_copy(src_ref=dst_ref.at[peer], dst_ref=dst_ref.at[peer], **sems))

    def start(self):
        self.local.start()
        for cp in self.sends:
            cp.start()

    def wait(self):
        for cp in self.recvs:
            cp.wait_recv()
        for cp in self.sends:
            cp.wait_send()
        self.local.wait()


GRAD_W_IN_TK = 2048


def _grad_w_in_sibling(u, dp, core, seq):
    tk = min(GRAD_W_IN_TK, seq)
    nk = seq // tk

    def body(core_ref, u_ref, dp_ref, g_ref):
        del core_ref

        @pl.when(pl.program_id(1) == 0)
        def _():
            g_ref[...] = jnp.zeros_like(g_ref)

        g_ref[0] += _dot(u_ref[...], dp_ref[...], TN)

    return pl.pallas_call(
        body, name="grad_w_in_sibling",
        grid_spec=pltpu.PrefetchScalarGridSpec(
            num_scalar_prefetch=1, grid=(N_CHIPS, nk),
            in_specs=[pl.BlockSpec((tk, D_MODEL), lambda n, k, c: (k, 0)),
                      pl.BlockSpec((tk, W_BLK), lambda n, k, c: (k, 2 * n + 1 - c[0]))],
            out_specs=pl.BlockSpec((1, D_MODEL, W_BLK), lambda n, k, c: (n, 0, 0))),
        out_shape=SDS((N_CHIPS, D_MODEL, W_BLK), F32),
        compiler_params=_cparams(("parallel", "arbitrary")),
    )(core, u, dp)


def _grad_w_in_own(u, dp, core, g_sib, seq):
    tk = min(GRAD_W_IN_TK, seq)
    nk = seq // tk

    def body(core_ref, u_ref, dp_ref, gsib_ref, g_ref, land, send_sem, recv_sem):
        del core_ref
        n = pl.program_id(0)
        k = pl.program_id(1)
        x, y, c = _mesh_pos()
        swap = pltpu.make_async_remote_copy(src_ref=gsib_ref, dst_ref=land, send_sem=send_sem, recv_sem=recv_sem,
                                            device_id=(x, y, 1 - c), device_id_type=MESH)

        @pl.when((n == 0) & (k == 0))
        def _():
            swap.start()

        @pl.when(k == 0)
        def _():
            g_ref[...] = jnp.zeros_like(g_ref)

        g_ref[0] += _dot(u_ref[...], dp_ref[...], TN)

        @pl.when((n == 0) & (k == nk - 1))
        def _():
            swap.wait_recv()

        @pl.when(k == nk - 1)
        def _():
            g_ref[0] += land[n]

        @pl.when((n == N_CHIPS - 1) & (k == nk - 1))
        def _():
            swap.wait_send()

    return pl.pallas_call(
        body, name="grad_w_in_own",
        grid_spec=pltpu.PrefetchScalarGridSpec(
            num_scalar_prefetch=1, grid=(N_CHIPS, nk),
            in_specs=[pl.BlockSpec((tk, D_MODEL), lambda n, k, c: (k, 0)),
                      pl.BlockSpec((tk, W_BLK), lambda n, k, c: (k, 2 * n + c[0])), ANY],
            out_specs=pl.BlockSpec((1, D_MODEL, W_BLK), lambda n, k, c: (n, 0, 0)),
            scratch_shapes=[pltpu.VMEM((N_CHIPS, D_MODEL, W_BLK), F32), pltpu.SemaphoreType.DMA(()),
                            pltpu.SemaphoreType.DMA(())]),
        out_shape=SDS((N_CHIPS, D_MODEL, W_BLK), F32),
        compiler_params=_cparams(("arbitrary", "arbitrary")),
    )(core, u, dp, g_sib)


def _grad_x(dp, w_all, x, pre_w, dout, g_chip, seq):
    tm = 512
    nm = seq // tm

    def body(dp_ref, w_ref, x_ref, pw_ref, do_ref, gsrc_ref, gx_ref, gpw_ref, recv_ref,
             send_sems, recv_sems, local_sem):
        m = pl.program_id(0)
        ex = _ChipExchange(gsrc_ref, recv_ref, send_sems, recv_sems, local_sem)

        @pl.when(m == 0)
        def _():
            ex.start()
            gpw_ref[...] = jnp.zeros_like(gpw_ref)

        du = _dot(dp_ref[:, 0:W_BLK], w_ref[0], NT)
        for j in range(1, N_DEV):
            du = du + _dot(dp_ref[:, j * W_BLK:(j + 1) * W_BLK], w_ref[j], NT)
        xv = x_ref[...]
        r1 = lax.rsqrt(jnp.mean(xv * xv, axis=-1, keepdims=True) + EPS)
        xn = xv * r1
        gpw_ref[0:1, :] += jnp.sum(du * xn, axis=0, keepdims=True)
        dxn = du * pw_ref[...]
        gx_ref[...] = r1 * (dxn - xn * jnp.mean(dxn * xn, axis=-1, keepdims=True)) + do_ref[...]

        @pl.when(m == nm - 1)
        def _():
            ex.wait()

    row = lambda w: pl.BlockSpec((tm, w), lambda m: (m, 0))
    return pl.pallas_call(
        body, name="grad_x", grid=(nm,),
        in_specs=[row(D_IN), pl.BlockSpec((N_DEV, D_MODEL, W_BLK), lambda m: (0, 0, 0), pipeline_mode=pl.Buffered(1)),
                  row(D_MODEL),
                  pl.BlockSpec((1, D_MODEL), lambda m: (0, 0)), row(D_MODEL), ANY],
        out_specs=[row(D_MODEL), pl.BlockSpec((8, D_MODEL), lambda m: (0, 0)), ANY],
        out_shape=[SDS((seq, D_MODEL), F32), SDS((8, D_MODEL), F32), SDS(g_chip.shape, F32)],
        scratch_shapes=[pltpu.SemaphoreType.DMA((N_CHIPS - 1,)), pltpu.SemaphoreType.DMA((N_CHIPS - 1,)),
                        pltpu.SemaphoreType.DMA(())],
        compiler_params=_cparams(("arbitrary",)),
    )(dp, w_all, x, pre_w, dout, g_chip)


def _local_step(x, tgt, p, u, conv_w, conv_b, wa, wx, ba, bx, lam, lbl, gnorm_w, w_out, post_w):
    seq = x.shape[0]
    h, y_lru = _lru_forward(p, conv_w, conv_b, wa, wx, ba, bx, lam, seq)
    y_hgrn, o, states = _hgrn_forward(p, lbl, gnorm_w, seq)
    dymix, dout, g_w_out, stats = _out_proj(y_lru, y_hgrn, w_out, x, tgt, post_w, seq)
    dp_lru, g_wa, g_wx, ls = _lru_backward(p, h, dymix, conv_w, conv_b, wa, wx, ba, bx, lam, seq)
    g_small = _pack_small(_shard_rows(g_wa, LRU_BLOCKS), _shard_rows(g_wx, LRU_BLOCKS),
                          _shard_rows(ls[0:4].reshape(4, D_MODEL, 1), 4).reshape(N_DEV, 4, 128),
                          _shard_rows(ls[5].reshape(4, LRU_BW, 1), 4).reshape(N_DEV, 4, 32),
                          _shard_rows(ls[6].reshape(4, LRU_BW, 1), 4).reshape(N_DEV, 4, 32))
    dp, hgrn_small, r_out, r_small = _hgrn_backward(
        p, o, states, dymix, lbl, gnorm_w, dp_lru, g_w_out.reshape(N_DEV, D_MIX // N_DEV, D_MODEL), g_small, seq)
    return dict(u=u, dp=dp, dout=dout, r_out=r_out, r_small=r_small,
                lru_small=ls, hgrn_small=hgrn_small, stats=stats)


class _TwoLevelGather:
    def __init__(self, ins, outs, send_sems, recv_sems, local_sems):
        self.ins, self.outs = ins, outs
        self.send_sems, self.recv_sems, self.local_sems = send_sems, recv_sems, local_sems
        x, y, c = _mesh_pos()
        self.c = c
        self.me, self.sibling = (x, y, c), (x, y, 1 - c)
        self.chips = [(1 - x, y), (x, 1 - y), (1 - x, 1 - y)]
        n = len(ins)
        self.mine = [pltpu.make_async_copy(ins[a], self._slot(a, self.me), local_sems.at[a]) for a in range(n)]
        self.first = []
        for a in range(n):
            self.first.append(self._copy(a, 0, self.me, self.sibling, src=ins[a]))
            self.first += [self._copy(a, 1 + j, self.me, (*chip, c), src=ins[a])
                           for j, chip in enumerate(self.chips)]
        self.passed = [self._copy(a, 4 + j, (*chip, c), self.sibling)
                       for j, chip in enumerate(self.chips) for a in range(n)]

    def _slot(self, a, pos):
        return self.outs[a].at[4 * pos[0] + 2 * pos[1] + pos[2]]

    def _copy(self, a, k, block, to, src=None):
        dst = self._slot(a, block)
        return pltpu.make_async_remote_copy(
            src_ref=dst if src is None else src, dst_ref=dst,
            send_sem=self.send_sems.at[a, k], recv_sem=self.recv_sems.at[a, k],
            device_id=to, device_id_type=MESH)

    def start(self):
        for cp in self.mine + self.first:
            cp.start()

    def forward(self):
        n = len(self.ins)
        for j, chip in enumerate(self.chips):
            for a in range(n):
                self._copy(a, 1 + j, (*chip, self.c), self.me).wait_recv()
                self.passed[j * n + a].start()

    def finish(self):
        for a in range(len(self.ins)):
            self._copy(a, 0, self.sibling, self.me).wait_recv()
            for j, chip in enumerate(self.chips):
                self._copy(a, 4 + j, (*chip, 1 - self.c), self.me).wait_recv()
        for cp in self.first + self.passed:
            cp.wait_send()
        for cp in self.mine:
            cp.wait()


W_IN_DIRECT = (1, 2, 4, 6)
W_IN_PASSED = (2, 4, 6)
P_STREAMS = 4


def _in_proj_gather(x, pre_w, w_in_blk, w_out_blk, small_blk, me, seq):
    tm = min(1024, seq)
    nm = seq // tm
    last = N_DEV - 1

    def body(me_ref, x_ref, pw_ref, wblk_ref, woblk_ref, smblk_ref,
             p_ref, u_ref, wall_ref, woall_ref, small_ref,
             u_all, w_vmem, own_sem, d_send, d_recv, f_send, f_recv, wb_sems, g_send, g_recv, g_local,
             p_buf, p_sems):
        i = pl.program_id(0)
        m = pl.program_id(1)
        idx = me_ref[0]
        x_, y_, c_ = _mesh_pos()
        aux = _TwoLevelGather([woblk_ref, smblk_ref], [woall_ref, small_ref], g_send, g_recv, g_local)

        def peer(k):
            return (1 - x_ if (k >> 2) & 1 else x_, 1 - y_ if (k >> 1) & 1 else y_, 1 - c_ if k & 1 else c_)

        def direct(k):
            f = W_IN_DIRECT.index(k)
            return (pltpu.make_async_remote_copy(src_ref=wblk_ref, dst_ref=w_vmem.at[idx], send_sem=d_send.at[f],
                                                 recv_sem=d_recv.at[f], device_id=peer(k), device_id_type=MESH),
                    pltpu.make_async_remote_copy(src_ref=w_vmem.at[idx ^ k], dst_ref=w_vmem.at[idx ^ k],
                                                 send_sem=d_send.at[f], recv_sem=d_recv.at[f], device_id=peer(k),
                                                 device_id_type=MESH))

        def passed(k):
            f = W_IN_PASSED.index(k)
            return (pltpu.make_async_remote_copy(src_ref=w_vmem.at[idx ^ k], dst_ref=w_vmem.at[idx ^ k],
                                                 send_sem=f_send.at[f], recv_sem=f_recv.at[f], device_id=peer(1),
                                                 device_id_type=MESH),
                    pltpu.make_async_remote_copy(src_ref=w_vmem.at[idx ^ (k + 1)], dst_ref=w_vmem.at[idx ^ (k + 1)],
                                                 send_sem=f_send.at[f], recv_sem=f_recv.at[f], device_id=peer(1),
                                                 device_id_type=MESH))

        def write_back(k):
            return pltpu.make_async_copy(w_vmem.at[idx ^ k], wall_ref.at[idx ^ k], wb_sems.at[k])

        own = pltpu.make_async_copy(wblk_ref, w_vmem.at[idx], own_sem)

        @pl.when((i == 0) & (m == 0))
        def _():
            own.start()
            for k in W_IN_DIRECT:
                direct(k)[0].start()
            aux.start()
            own.wait()
            write_back(0).start()

        for k in range(1, N_DEV):
            @pl.when((i == k) & (m == 0))
            def _(k=k):
                if k in W_IN_DIRECT:
                    direct(k)[1].wait_recv()
                    if k in W_IN_PASSED:
                        passed(k)[0].start()
                else:
                    passed(k - 1)[1].wait_recv()
                write_back(k).start()

        @pl.when((i == N_CHIPS) & (m == 0))
        def _():
            aux.forward()

        rows = pl.ds(pl.multiple_of(m * tm, tm), tm)

        @pl.when(i == 0)
        def _():
            xv = x_ref[...]
            r = lax.rsqrt(jnp.mean(xv * xv, axis=-1, keepdims=True) + EPS)
            ub = (xv * r * pw_ref[...]).astype(BF16)
            u_all[rows, :] = ub
            u_ref[...] = ub

        step = i * nm + m
        slot = step % 2
        part = tm // P_STREAMS

        def p_copy(s, part_i, m_, blk):
            dst_rows = pl.ds(pl.multiple_of(m_ * tm + part_i * part, part), part)
            dst_cols = pl.ds(pl.multiple_of(blk * W_BLK, W_BLK), W_BLK)
            return pltpu.make_async_copy(p_buf.at[s, part_i * part:(part_i + 1) * part, :],
                                         p_ref.at[dst_rows, dst_cols], p_sems.at[s, part_i])

        @pl.when(step >= 2)
        def _():
            for j in range(P_STREAMS):
                p_copy(slot, j, m, idx ^ i).wait()

        p_buf[slot] = _dot(u_all[rows, :], w_vmem[idx ^ i])
        for j in range(P_STREAMS):
            p_copy(slot, j, m, idx ^ i).start()

        @pl.when((i == last) & (m == nm - 1))
        def _():
            for j in range(P_STREAMS):
                p_copy(1 - slot, j, m, idx ^ i).wait()
                p_copy(slot, j, m, idx ^ i).wait()
            for k in W_IN_DIRECT:
                direct(k)[0].wait_send()
            for k in W_IN_PASSED:
                passed(k)[0].wait_send()
            for k in range(N_DEV):
                write_back(k).wait()
            aux.finish()

    first_pass = lambda i, m: jnp.where(i == 0, m, nm - 1)
    return pl.pallas_call(
        body, name="in_proj_gather",
        grid_spec=pltpu.PrefetchScalarGridSpec(
            num_scalar_prefetch=1, grid=(N_DEV, nm),
            in_specs=[pl.BlockSpec((tm, D_MODEL), lambda i, m, me: (first_pass(i, m), 0)),
                      pl.BlockSpec((1, D_MODEL), lambda i, m, me: (0, 0)), ANY, ANY, ANY],
            out_specs=[ANY, pl.BlockSpec((tm, D_MODEL), lambda i, m, me: (first_pass(i, m), 0)), ANY, ANY, ANY],
            scratch_shapes=[pltpu.VMEM((seq, D_MODEL), BF16), pltpu.VMEM((N_DEV, D_MODEL, W_BLK), BF16),
                            pltpu.SemaphoreType.DMA(()),
                            pltpu.SemaphoreType.DMA((len(W_IN_DIRECT),)), pltpu.SemaphoreType.DMA((len(W_IN_DIRECT),)),
                            pltpu.SemaphoreType.DMA((len(W_IN_PASSED),)), pltpu.SemaphoreType.DMA((len(W_IN_PASSED),)),
                            pltpu.SemaphoreType.DMA((N_DEV,)),
                            pltpu.SemaphoreType.DMA((2, 7)), pltpu.SemaphoreType.DMA((2, 7)),
                            pltpu.SemaphoreType.DMA((2,)),
                            pltpu.VMEM((2, tm, W_BLK), F32), pltpu.SemaphoreType.DMA((2, P_STREAMS))]),
        out_shape=[SDS((seq, D_IN), F32), SDS((seq, D_MODEL), BF16), SDS((N_DEV, D_MODEL, W_BLK), BF16),
                   SDS((N_DEV,) + w_out_blk.shape, w_out_blk.dtype), SDS((N_DEV,) + small_blk.shape, small_blk.dtype)],
        compiler_params=_cparams(("arbitrary", "arbitrary")),
    )(me, x, pre_w, w_in_blk, w_out_blk, small_blk)


def _exchange_grads(blocks, repl):
    nb = len(blocks)
    n = nb + 1

    def body(*refs):
        ins, outs, sems = refs[:n], refs[n:2 * n], refs[2 * n:]
        exs = [_SlotExchange(ins[a], outs[a], *sems[3 * a:3 * a + 3], blocked=a < nb) for a in range(n)]
        for ex in exs:
            ex.start()
        for ex in exs:
            ex.wait()

    arrs = list(blocks) + [repl]
    shapes = [SDS(b.shape, b.dtype) for b in blocks] + [SDS((N_DEV,) + repl.shape, repl.dtype)]
    return pl.pallas_call(
        body, name="exchange_small", out_shape=shapes,
        in_specs=[ANY] * n, out_specs=[ANY] * n,
        scratch_shapes=EXCHANGE_SEMS * n,
    )(*arrs)


def _pack_rows(picks, name):
    arrs = [p[0] for p in picks]

    def body(*refs):
        out = refs[-1]
        out[...] = jnp.zeros_like(out)
        at = 0
        for ref, (_, row, rows, scale) in zip(refs[:-1], picks):
            out[at:at + rows, :] = ref[row:row + rows, :] * scale
            at += rows

    return pl.pallas_call(body, name=name, out_shape=SDS((8, D_MODEL), F32))(*arrs)


def _adamw(g, w, m, v):
    m2 = ADAM_B1 * m + (1.0 - ADAM_B1) * g
    v2 = ADAM_B2 * v + (1.0 - ADAM_B2) * (g * g)
    m_hat = m2 / (1.0 - ADAM_B1 ** ADAM_STEP)
    v_hat = v2 / (1.0 - ADAM_B2 ** ADAM_STEP)
    delta = -ADAM_LR * (m_hat / (jnp.sqrt(v_hat) + ADAM_EPS) + ADAM_WD * w)
    return delta, m2, v2


def _sum_slots(r_ref):
    g = r_ref[0]
    for s in range(1, r_ref.shape[0]):
        g = g + r_ref[s]
    return g


def _sum_adamw(recv, w, m, v, tr, name):
    rows, cols = w.shape

    def body(r_ref, w_ref, m_ref, v_ref, g_ref, d_ref, m2_ref, v2_ref):
        g = _sum_slots(r_ref)
        g_ref[...] = g
        d_ref[...], m2_ref[...], v2_ref[...] = _adamw(g, w_ref[...], m_ref[...], v_ref[...])

    blk = pl.BlockSpec((tr, cols), lambda i: (i, 0))
    return pl.pallas_call(
        body, name=name, grid=(rows // tr,),
        in_specs=[pl.BlockSpec((recv.shape[0], tr, cols), lambda i: (0, i, 0)), blk, blk, blk],
        out_specs=[blk] * 4, out_shape=[SDS((rows, cols), F32)] * 4,
        compiler_params=_cparams(("parallel",)),
    )(recv, w, m, v)


def _sum_adamw_pieces(recv, rows, ws, ms, vs, name, loss_row=None):
    n = len(ws)

    def body(r_ref, *refs):
        w_refs, m_refs, v_refs, outs = refs[:n], refs[n:2 * n], refs[2 * n:3 * n], refs[3 * n:]
        g = _sum_slots(r_ref)
        for i, (row, nrows) in enumerate(rows):
            gi = g[row:row + nrows, :]
            outs[i][...] = gi
            outs[n + i][...], outs[2 * n + i][...], outs[3 * n + i][...] = _adamw(
                gi, w_refs[i][...], m_refs[i][...], v_refs[i][...])
        if loss_row is not None:
            total = jnp.sum(g[loss_row:loss_row + 1, :], axis=-1, keepdims=True)
            outs[4 * n][...] = jnp.broadcast_to(total, outs[4 * n].shape)

    shapes = [SDS(w.shape, F32) for w in ws] * 4 + ([SDS((8, 128), F32)] if loss_row is not None else [])
    out = pl.pallas_call(body, name=name, out_shape=shapes)(recv, *ws, *ms, *vs)
    return [out[k * n:(k + 1) * n] for k in range(4)] + list(out[4 * n:])


def _shard_rows(t, lead):
    r = t.shape[1] // N_DEV
    t = t.reshape((lead, N_DEV, r) + t.shape[2:])
    return jnp.moveaxis(t, 1, 0)


def _pad8(t):
    return jnp.pad(t, ((0, 0), (0, 8 - t.shape[1]), (0, 0)))


def _pack_small(wa, wx, cw, b_a, b_x):
    n = wa.shape[0]
    return jnp.concatenate([
        wa.reshape(n, 256, 128), wx.reshape(n, 256, 128), _pad8(cw),
        _pad8(b_a.reshape(n, 1, 128)), _pad8(b_x.reshape(n, 1, 128))], axis=1)


def _unpack_small(t):
    n = t.shape[0]
    return (t[:, SM_WA:SM_WA + 256].reshape(n, 4, 32, 256), t[:, SM_WX:SM_WX + 256].reshape(n, 4, 32, 256),
            t[:, SM_CW:SM_CW + 4], t[:, SM_BA].reshape(n, 4, 32), t[:, SM_BX].reshape(n, 4, 32))


def kernel(x, pre_norm_w, w_in, conv_w, conv_b, lru_w_a, lru_b_a, lru_w_x, lru_b_x, lru_lambda, hgrn_lb_logits, hgrn_gnorm_w, w_out, post_norm_w, loss_target, m_pre_norm_w, m_w_in, m_conv_w, m_conv_b, m_lru_w_a, m_lru_b_a, m_lru_w_x, m_lru_b_x, m_lru_lambda, m_hgrn_lb_logits, m_hgrn_gnorm_w, m_w_out, m_post_norm_w, v_pre_norm_w, v_w_in, v_conv_w, v_conv_b, v_lru_w_a, v_lru_b_a, v_lru_w_x, v_lru_b_x, v_lru_lambda, v_hgrn_lb_logits, v_hgrn_gnorm_w, v_w_out, v_post_norm_w):
    seq = x.shape[1]
    x2 = x.reshape(seq, D_MODEL)
    tgt = loss_target.reshape(seq, D_MODEL)

    small_w = _pack_small(lru_w_a, lru_w_x, conv_w, lru_b_a, lru_b_x)[0]
    me = (4 * lax.axis_index("x") + 2 * lax.axis_index("y") + lax.axis_index("c")).astype(jnp.int32).reshape(1)
    p, u, w_in_all, w_out_all, small_all = _in_proj_gather(
        x2, pre_norm_w, w_in[0].astype(BF16), w_out[0].astype(BF16), small_w, me, seq)
    wa_s, wx_s, cw_s, ba_s, bx_s = _unpack_small(small_all)
    wa = jnp.moveaxis(wa_s, 0, 1).reshape(LRU_BLOCKS, LRU_BW, LRU_BW).astype(BF16)
    wx = jnp.moveaxis(wx_s, 0, 1).reshape(LRU_BLOCKS, LRU_BW, LRU_BW).astype(BF16)
    cw = jnp.moveaxis(cw_s, 0, 1).reshape(4, D_MODEL)
    ba = jnp.moveaxis(ba_s, 0, 1).reshape(1, D_MODEL)
    bx = jnp.moveaxis(bx_s, 0, 1).reshape(1, D_MODEL)

    loc = _local_step(x2, tgt, p, u, cw, conv_b, wa, wx, ba, bx, lru_lambda,
                      hgrn_lb_logits, hgrn_gnorm_w, w_out_all.reshape(D_MIX, D_MODEL), post_norm_w)

    ls, r_out, r_small = loc["lru_small"], loc["r_out"], loc["r_small"]
    core = lax.axis_index("c").astype(jnp.int32).reshape(1)
    g_sib = _grad_w_in_sibling(loc["u"], loc["dp"], core, seq)
    g_chip = _grad_w_in_own(loc["u"], loc["dp"], core, g_sib, seq)
    grad_x, pre_small, r_in = _grad_x(loc["dp"], w_in_all, x2, pre_norm_w, loc["dout"], g_chip, seq)
    g_repl = _pack_rows([(pre_small, 0, 1, 1.0), (ls, 4, 1, 1.0), (ls, 7, 1, 1.0),
                         (loc["hgrn_small"], 0, 1, 1.0), (loc["hgrn_small"], 0, 1, -1.0),
                         (loc["hgrn_small"], 1, 1, 1.0), (loc["stats"], 0, 2, 1.0)], "pack_grads")
    (r_repl,) = _exchange_grads([], g_repl)

    repl_rows = [(RP_PRE, 1), (RP_CB, 1), (RP_LAM, 1), (RP_LB0, 2), (RP_GN, 1), (RP_POST, 1)]
    o_repl = _sum_adamw_pieces(
        r_repl, repl_rows,
        [pre_norm_w, conv_b, lru_lambda, hgrn_lb_logits, hgrn_gnorm_w, post_norm_w],
        [m_pre_norm_w, m_conv_b, m_lru_lambda, m_hgrn_lb_logits, m_hgrn_gnorm_w, m_post_norm_w],
        [v_pre_norm_w, v_conv_b, v_lru_lambda, v_hgrn_lb_logits, v_hgrn_gnorm_w, v_post_norm_w],
        "adamw_repl", loss_row=RP_LOSS)
    loss = o_repl[4][0, 0]

    small_rows = [(SM_WA, 256), (SM_WX, 256), (SM_CW, 4), (SM_BA, 1), (SM_BX, 1)]
    as_rows = lambda wa_, wx_, cw_, ba_, bx_: [wa_.reshape(256, 128), wx_.reshape(256, 128), cw_.reshape(4, 128),
                                               ba_.reshape(1, 128), bx_.reshape(1, 128)]
    o_small = _sum_adamw_pieces(
        r_small, small_rows, as_rows(lru_w_a, lru_w_x, conv_w, lru_b_a, lru_b_x),
        as_rows(m_lru_w_a, m_lru_w_x, m_conv_w, m_lru_b_a, m_lru_b_x),
        as_rows(v_lru_w_a, v_lru_w_x, v_conv_w, v_lru_b_a, v_lru_b_x), "adamw_small")

    o_in = _sum_adamw(r_in, w_in[0], m_w_in[0], v_w_in[0], 128, "adamw_w_in")
    o_out = _sum_adamw(r_out, w_out[0], m_w_out[0], v_w_out[0], 64, "adamw_w_out")

    outs = [loss, grad_x.reshape(x.shape)]
    for kind in range(4):
        pre, cb, lam, lb, gn, post = o_repl[kind]
        swa, swx, scw, sba, sbx = o_small[kind]
        outs += [pre, o_in[kind][None], scw.reshape(conv_w.shape), cb, swa.reshape(lru_w_a.shape),
                 sba.reshape(lru_b_a.shape), swx.reshape(lru_w_x.shape), sbx.reshape(lru_b_x.shape),
                 lam, lb, gn, o_out[kind][None], post]
    return tuple(outs)
```

```python
import functools

import jax
import jax.numpy as jnp
from jax import lax
from jax.experimental import pallas as pl
from jax.experimental.pallas import tpu as pltpu

F32 = jnp.float32
BF16 = jnp.bfloat16
SDS = jax.ShapeDtypeStruct

D_MODEL = 1024
D_IN = 6144
N_DEV = 8
N_CHIPS = 4
W_BLK = D_IN // N_DEV
D_MIX = 2048
LRU_BLOCKS = 4
LRU_BW = 256
LRU_C = 8.0
N_HEADS = 8
HEAD_D = 128
CHUNK = 128
SUB = 32
N_SUB = CHUNK // SUB
HGRN_FWD_STEP_CHUNKS = 4
HGRN_STEP_CHUNKS = 2
EXP_CLAMP = 80.0
EPS = 1e-6

ADAM_LR = 0.001
ADAM_B1 = 0.9
ADAM_B2 = 0.999
ADAM_EPS = 1e-08
ADAM_WD = 0.01
ADAM_STEP = 10

VMEM_LIMIT = 56 * 1024 * 1024

NN = (((1,), (0,)), ((), ()))
NT = (((1,), (1,)), ((), ()))
TN = (((0,), (0,)), ((), ()))

SM_WA = 0
SM_WX = 256
SM_CW = 512
SM_BA = 520
SM_BX = 528
SM_ROWS = 536

RP_PRE, RP_CB, RP_LAM, RP_LB0, RP_LB1, RP_GN, RP_POST, RP_LOSS = range(8)


def _dot(a, b, dims=NN):
    return lax.dot_general(a, b, dims, preferred_element_type=F32)


def _sigmoid(x):
    return 0.5 * jnp.tanh(0.5 * x) + 0.5


def _sigmoid_pos(x):
    return 1.0 / (1.0 + jnp.exp(-x))


def _cparams(sem, vmem=VMEM_LIMIT):
    return pltpu.CompilerParams(dimension_semantics=sem, vmem_limit_bytes=vmem)


def _iota(shape, axis):
    return lax.broadcasted_iota(jnp.int32, shape, axis)


def _softplus_neg(lam):
    z = -lam
    e = jnp.exp(-jnp.abs(z))
    u = 1.0 + e
    log1p_e = jnp.where(u == 1.0, e, jnp.log(u) * (e / (u - 1.0)))
    sp = jnp.maximum(z, 0.0) + log1p_e
    dsp = -jnp.where(z >= 0.0, 1.0 / u, e / u)
    return sp, dsp


def _neg_expm1(x):
    poly = x * (1.0 + x * (1.0 / 2 + x * (1.0 / 6 + x * (1.0 / 24 + x * (1.0 / 120)))))
    return jnp.where(x > -1.0 / 16, -poly, 1.0 - jnp.exp(x))


def _conv_taps(lx, prev8, cw_ref, cb_ref, tile):
    xc = cb_ref[...] + cw_ref[3:4, :] * lx
    for j in (1, 2, 3):
        xc = xc + cw_ref[3 - j:4 - j, :] * pltpu.roll(lx, j, 0)
    row8 = _iota((8, D_MODEL), 0)
    last8 = lx[tile - 8:tile, :]
    fix = jnp.zeros((8, D_MODEL), F32)
    for j in (1, 2, 3):
        wrong = pltpu.roll(last8, j, 0)
        right = pltpu.roll(prev8, j, 0)
        fix = fix + cw_ref[3 - j:4 - j, :] * jnp.where(row8 < j, right - wrong, 0.0)
    return xc, fix


def _lru_gates(xcs, wa, wx, ba, bx, sp):
    xb = xcs.astype(BF16)
    r = _sigmoid_pos(_dot(xb, wa) + ba)
    i = _sigmoid(_dot(xb, wx) + bx)
    la = (-LRU_C * sp) * r
    a = jnp.exp(la)
    one_minus_a2 = _neg_expm1(2.0 * la)
    return r, i, a, one_minus_a2


def _lru_forward(p, conv_w, conv_b, wa, wx, ba, bx, lam, seq):
    tile = min(512, seq // 2)
    nblk = tile // 8

    def body(lx_ref, gt_ref, cw_ref, cb_ref, wa_ref, wx_ref, ba_ref, bx_ref, lam_ref,
             h_ref, y_ref, ext, hcar, xc_scr, a_scr, u_scr):
        @pl.when(pl.program_id(0) == 0)
        def _():
            ext[0:8, :] = jnp.zeros((8, D_MODEL), F32)
            hcar[...] = jnp.zeros_like(hcar)

        lx = lx_ref[...]
        ext[8:8 + tile, :] = lx
        xc = cb_ref[...] + cw_ref[3:4, :] * lx
        for j in (1, 2, 3):
            xc = xc + cw_ref[3 - j:4 - j, :] * ext[8 - j:8 - j + tile, :]
        xc_scr[...] = xc
        ext[0:8, :] = lx_ref[tile - 8:tile, :]
        sp, _ = _softplus_neg(lam_ref[...])
        for n in range(LRU_BLOCKS):
            sl = slice(n * LRU_BW, (n + 1) * LRU_BW)
            xcs = xc_scr[:, sl]
            _, i, a, ne = _lru_gates(xcs, wa_ref[n], wx_ref[n], ba_ref[:, sl], bx_ref[:, sl], sp[:, sl])
            a_scr[:, sl] = a
            u_scr[:, sl] = jnp.sqrt(ne) * (i * xcs)

        row8 = _iota((8, D_MODEL), 0)

        def blk(j, hc):
            off = pl.multiple_of(j * 8, 8)
            a = a_scr[pl.ds(off, 8), :]
            u = u_scr[pl.ds(off, 8), :]
            for k in (1, 2, 4):
                m = row8 >= k
                u = jnp.where(m, u + a * pltpu.roll(u, k, 0), u)
                a = jnp.where(m, a * pltpu.roll(a, k, 0), a)
            h = u + a * hc
            h_ref[pl.ds(off, 8), :] = h
            return jnp.broadcast_to(h[7:8, :], (8, D_MODEL))

        hcar[...] = lax.fori_loop(0, nblk, blk, hcar[...])
        g = gt_ref[...]
        y_ref[...] = (h_ref[...] * (g * _sigmoid(g))).astype(BF16)

    full = lambda shape: pl.BlockSpec(shape, lambda t: (0,) * len(shape))
    return pl.pallas_call(
        body, name="lru_fwd", grid=(seq // tile,),
        in_specs=[pl.BlockSpec((tile, D_MODEL), lambda t: (t, 0)),
                  pl.BlockSpec((tile, D_MODEL), lambda t: (t, 1)),
                  full((4, D_MODEL)), full((1, D_MODEL)),
                  full((LRU_BLOCKS, LRU_BW, LRU_BW)), full((LRU_BLOCKS, LRU_BW, LRU_BW)),
                  full((1, D_MODEL)), full((1, D_MODEL)), full((1, D_MODEL))],
        out_specs=[pl.BlockSpec((tile, D_MODEL), lambda t: (t, 0)),
                   pl.BlockSpec((tile, D_MODEL), lambda t: (t, 0))],
        out_shape=[SDS((seq, D_MODEL), F32), SDS((seq, D_MODEL), BF16)],
        scratch_shapes=[pltpu.VMEM((tile + 8, D_MODEL), F32), pltpu.VMEM((8, D_MODEL), F32),
                        pltpu.VMEM((tile, D_MODEL), F32), pltpu.VMEM((tile, D_MODEL), F32),
                        pltpu.VMEM((tile, D_MODEL), F32)],
        compiler_params=_cparams(("arbitrary",)),
    )(p, p, conv_w, conv_b, wa, wx, ba, bx, lam)


def _lru_backward(p, h, dymix, conv_w, conv_b, wa, wx, ba, bx, lam, seq):
    tile = min(256, seq // 2)
    nt = seq // tile
    nblk = tile // 8
    t8 = tile // 8

    def body(lx_ref, lxh_ref, gt_ref, h_ref, hh_ref, dy_ref, cw_ref, cb_ref, wa_ref, wx_ref, ba_ref,
             bx_ref, lam_ref, dp_ref, gwa_ref, gwx_ref, gsm_ref,
             lamcar, anext, dxc8, xc_scr, r_scr, i_scr, a_scr, m_scr, rm_scr, c_scr, l_scr, dxc_scr):
        step = pl.program_id(0)
        first_tile = step == nt - 1

        @pl.when(step == 0)
        def _():
            lamcar[...] = jnp.zeros_like(lamcar)
            anext[...] = jnp.zeros_like(anext)
            dxc8[...] = jnp.zeros_like(dxc8)
            gwa_ref[...] = jnp.zeros_like(gwa_ref)
            gwx_ref[...] = jnp.zeros_like(gwx_ref)
            gsm_ref[...] = jnp.zeros_like(gsm_ref)

        keep = jnp.where(first_tile, 0.0, 1.0)
        lx = lx_ref[...]
        prev8 = lxh_ref[...] * keep
        xc, fix = _conv_taps(lx, prev8, cw_ref, cb_ref, tile)
        xc_scr[...] = xc
        xc_scr[0:8, :] = xc_scr[0:8, :] + fix
        sp, dsp = _softplus_neg(lam_ref[...])
        for n in range(LRU_BLOCKS):
            sl = slice(n * LRU_BW, (n + 1) * LRU_BW)
            r, i, a, ne = _lru_gates(xc_scr[:, sl], wa_ref[n], wx_ref[n], ba_ref[:, sl], bx_ref[:, sl],
                                     sp[:, sl])
            r_scr[:, sl] = r
            i_scr[:, sl] = i
            a_scr[:, sl] = a
            m_scr[:, sl] = jnp.sqrt(ne)
            rm_scr[:, sl] = lax.rsqrt(ne)

        g = gt_ref[...]
        sg = _sigmoid(g)
        dy = dy_ref[...]
        hv = h_ref[...]
        dp_ref[:, D_MODEL:2 * D_MODEL] = (dy * hv * (sg * (1.0 + g * (1.0 - sg)))).astype(BF16)

        rowt = _iota((tile, D_MODEL), 0)
        av = a_scr[...]
        l_scr[...] = dy * (g * sg)
        c_scr[...] = jnp.where(rowt == tile - 1, anext[...][0:1, :], pltpu.roll(av, tile - 1, 0))
        anext[...] = jnp.broadcast_to(av[0:1, :], (8, D_MODEL))
        row8 = _iota((8, D_MODEL), 0)

        def blk(jj, lc):
            off = pl.multiple_of((nblk - 1 - jj) * 8, 8)
            c = c_scr[pl.ds(off, 8), :]
            u = l_scr[pl.ds(off, 8), :]
            for k in (1, 2, 4):
                m = row8 < 8 - k
                u = jnp.where(m, u + c * pltpu.roll(u, 8 - k, 0), u)
                c = jnp.where(m, c * pltpu.roll(c, 8 - k, 0), c)
            lamv = u + c * lc
            l_scr[pl.ds(off, 8), :] = lamv
            return jnp.broadcast_to(lamv[0:1, :], (8, D_MODEL))

        lamcar[...] = lax.fori_loop(0, nblk, blk, lamcar[...])

        hprev = jnp.where(rowt == 0, hh_ref[...][7:8, :] * keep, pltpu.roll(hv, 1, 0))
        for n in range(LRU_BLOCKS):
            sl = slice(n * LRU_BW, (n + 1) * LRU_BW)
            lamv = l_scr[:, sl]
            xcs = xc_scr[:, sl]
            r = r_scr[:, sl]
            i = i_scr[:, sl]
            a = a_scr[:, sl]
            mult = m_scr[:, sl]
            d_la = lamv * hprev[:, sl] * a - (lamv * i * xcs) * (a * a * rm_scr[:, sl])
            d_pr = d_la * (-LRU_C * sp[:, sl]) * r * (1.0 - r)
            d_pi = (lamv * mult * xcs) * i * (1.0 - i)
            gsm_ref[7:8, sl] += jnp.sum(d_la * r, axis=0, keepdims=True) * (-LRU_C) * dsp[:, sl]
            gsm_ref[5:6, sl] += jnp.sum(d_pr, axis=0, keepdims=True)
            gsm_ref[6:7, sl] += jnp.sum(d_pi, axis=0, keepdims=True)
            xb = xcs.astype(BF16)
            prb = d_pr.astype(BF16)
            pib = d_pi.astype(BF16)
            gwa_ref[n] += _dot(xb, prb, TN)
            gwx_ref[n] += _dot(xb, pib, TN)
            dxc_scr[:, sl] = lamv * mult * i + _dot(prb, wa_ref[n], NT) + _dot(pib, wx_ref[n], NT)

        dxc = dxc_scr[...]
        gsm_ref[4:5, :] += jnp.sum(dxc, axis=0, keepdims=True)
        last8 = lx[tile - 8:tile, :]
        first8 = dxc[0:8, :]
        dlx = cw_ref[3:4, :] * dxc
        gsm_ref[3:4, :] += jnp.sum(dxc * lx, axis=0, keepdims=True)
        fix = jnp.zeros((8, D_MODEL), F32)
        for j in (1, 2, 3):
            w = cw_ref[3 - j:4 - j, :]
            dlx = dlx + w * pltpu.roll(dxc, tile - j, 0)
            fix = fix + w * jnp.where(row8 + j >= 8,
                                      pltpu.roll(dxc8[...], 8 - j, 0) - pltpu.roll(first8, 8 - j, 0), 0.0)
            halo = jnp.where(row8 < j, pltpu.roll(prev8, j, 0) - pltpu.roll(last8, j, 0), 0.0)
            gsm_ref[3 - j:4 - j, :] += (jnp.sum(dxc * pltpu.roll(lx, j, 0), axis=0, keepdims=True)
                                        + jnp.sum(first8 * halo, axis=0, keepdims=True))
        dxc8[...] = first8
        dp_ref[:, 0:D_MODEL] = dlx.astype(BF16)
        top = tile - 8
        dp_ref[top:tile, 0:D_MODEL] = (dlx[top:tile, :] + fix).astype(BF16)

    rev = lambda t: (nt - 1 - t, 0)
    halo_idx = lambda t: (jnp.maximum((nt - 1 - t) * t8 - 1, 0), 0)
    full = lambda shape: pl.BlockSpec(shape, lambda t: (0,) * len(shape))
    big = lambda: pltpu.VMEM((tile, D_MODEL), F32)
    return pl.pallas_call(
        body, name="lru_bwd", grid=(nt,),
        in_specs=[pl.BlockSpec((tile, D_MODEL), rev),
                  pl.BlockSpec((8, D_MODEL), halo_idx),
                  pl.BlockSpec((tile, D_MODEL), lambda t: (nt - 1 - t, 1)),
                  pl.BlockSpec((tile, D_MODEL), rev),
                  pl.BlockSpec((8, D_MODEL), halo_idx),
                  pl.BlockSpec((tile, D_MODEL), rev),
                  full((4, D_MODEL)), full((1, D_MODEL)),
                  full((LRU_BLOCKS, LRU_BW, LRU_BW)), full((LRU_BLOCKS, LRU_BW, LRU_BW)),
                  full((1, D_MODEL)), full((1, D_MODEL)), full((1, D_MODEL))],
        out_specs=[pl.BlockSpec((tile, 2 * D_MODEL), rev),
                   full((LRU_BLOCKS, LRU_BW, LRU_BW)), full((LRU_BLOCKS, LRU_BW, LRU_BW)),
                   full((8, D_MODEL))],
        out_shape=[SDS((seq, D_IN), BF16), SDS((LRU_BLOCKS, LRU_BW, LRU_BW), F32),
                   SDS((LRU_BLOCKS, LRU_BW, LRU_BW), F32), SDS((8, D_MODEL), F32)],
        scratch_shapes=[pltpu.VMEM((8, D_MODEL), F32), pltpu.VMEM((8, D_MODEL), F32),
                        pltpu.VMEM((8, D_MODEL), F32)] + [big() for _ in range(9)],
        compiler_params=_cparams(("arbitrary",)),
    )(p, p, p, h, h, dymix, conv_w, conv_b, wa, wx, ba, bx, lam)


def _tri_matmul(tri, g):
    hi = g.astype(BF16)
    lo = (g - hi.astype(F32)).astype(BF16)
    return _dot(tri, lo) + _dot(tri, hi)


def _hgrn_gate_terms(q, fr, lbl):
    lb = _sigmoid_pos(lbl[0:1, :] - lbl[1:2, :])
    half = 0.5 * (1.0 - lb)
    tf = jnp.tanh(0.5 * fr)
    f = (lb + half) + half * tf
    hq = 0.5 * q
    tq = jnp.tanh(hq)
    return lb, tf, f, tq, hq * tq + hq


def _hgrn_decay(bh):
    zero = jnp.zeros((1, bh.shape[1]), F32)
    rho = [zero] + [bh[s * SUB - 1:s * SUB, :] for s in range(1, N_SUB + 1)]
    start = _sub_rows(rho[0:N_SUB])
    end = _sub_rows(rho[1:N_SUB + 1])
    mid = 0.5 * (start + end)
    blast = rho[N_SUB]
    e_on = jnp.exp(bh - start)
    e_off = jnp.exp(end - bh)
    scales = [_sub_rows([jnp.exp(rho[i] - rho[j + 1]) if i > j else zero for i in range(N_SUB)])
              for j in range(N_SUB - 1)]
    return dict(eq0=jnp.exp(jnp.minimum(bh - mid, EXP_CLAMP)), ek0=jnp.exp(jnp.minimum(mid - bh, EXP_CLAMP)),
                e_on=e_on, e_off=e_off, scales=scales,
                eb=e_on * _sub_rows([jnp.exp(r) for r in rho[0:N_SUB]]),
                ekst=e_off * _sub_rows([jnp.exp(blast - r) for r in rho[1:N_SUB + 1]]),
                ebl=jnp.exp(blast))


def _sub_rows(vecs):
    return jnp.concatenate([jnp.broadcast_to(v, (SUB, v.shape[1])) for v in vecs], axis=0)


def _hgrn_operands(qs, k, dec, qt_scr, kt_scr):
    sub = jnp.right_shift(_iota(qs.shape, 0), 5)
    qon = qs * dec["e_on"]
    koff = k * dec["e_off"]
    qt_scr[0] = (qs * dec["eq0"]).astype(BF16)
    kt_scr[0] = (k * dec["ek0"]).astype(BF16)
    for j in range(N_SUB - 1):
        qt_scr[j + 1] = (qon * dec["scales"][j]).astype(BF16)
        kt_scr[j + 1] = jnp.where(sub == j, koff, 0.0).astype(BF16)
    return koff


def _hgrn_head_scores(qt_scr, kt_scr, sl, diag):
    a = jnp.where(diag, _dot(qt_scr[0, :, sl], kt_scr[0, :, sl], NT), 0.0)
    for j in range(1, N_SUB):
        a = a + _dot(qt_scr[j, :, sl], kt_scr[j, :, sl], NT)
    return a


def _hgrn_forward(p, lbl, gw, seq):
    nc = seq // CHUNK
    assert SUB == 32

    def body(q_ref, f_ref, v_ref, hg_ref, lbl_ref, gw_ref, y_ref, o_ref, st_ref,
             s_scr, qt_scr, kt_scr, qin_scr, kst_scr, vb_scr, a_scr):
        @pl.when(pl.program_id(0) == 0)
        def _():
            s_scr[...] = jnp.zeros_like(s_scr)

        r = _iota((CHUNK, CHUNK), 0)
        c = _iota((CHUNK, CHUNK), 1)
        tri = jnp.where(c <= r, 1.0, 0.0).astype(BF16)
        diag = (jnp.right_shift(r, 5) == jnp.right_shift(c, 5)) & (c <= r)
        heads = [slice(h * HEAD_D, (h + 1) * HEAD_D) for h in range(N_HEADS)]
        for cc in range(HGRN_FWD_STEP_CHUNKS):
            rows = slice(cc * CHUNK, (cc + 1) * CHUNK)
            q = q_ref[rows, :]
            _, _, f, _, qs = _hgrn_gate_terms(q, f_ref[rows, :], lbl_ref[...])
            k = 1.0 - f
            dec = _hgrn_decay(_tri_matmul(tri, jnp.log(f)))
            _hgrn_operands(qs, k, dec, qt_scr, kt_scr)
            qin_scr[...] = (qs * dec["eb"]).astype(BF16)
            kst_scr[...] = (k * dec["ekst"]).astype(BF16)
            vb_scr[...] = v_ref[rows, :].astype(BF16)
            ebl = dec["ebl"]
            hg = hg_ref[rows, :]
            gate = gw_ref[...] * (hg * _sigmoid(hg))
            stb = []
            for h, sl in enumerate(heads):
                st = s_scr[h]
                st_ref[cc, h] = st
                stb.append(st.astype(BF16))
                s_scr[h] = st * ebl[:, sl] + _dot(vb_scr[:, sl], kst_scr[:, sl], TN)
            for h, sl in enumerate(heads):
                a_scr[h] = _hgrn_head_scores(qt_scr, kt_scr, sl, diag).astype(BF16)
            for h, sl in enumerate(heads):
                o = _dot(a_scr[h], vb_scr[:, sl]) + _dot(qin_scr[:, sl], stb[h], NT)
                o_ref[rows, sl] = o
                rs = lax.rsqrt(jnp.mean(o * o, axis=-1, keepdims=True) + EPS)
                y_ref[rows, sl] = ((o * rs) * gate[:, sl]).astype(BF16)

    col = lambda j: pl.BlockSpec((HGRN_FWD_STEP_CHUNKS * CHUNK, D_MODEL), lambda c: (c, j))
    par = lambda rows: pl.BlockSpec((rows, D_MODEL), lambda c: (0, 0))
    return pl.pallas_call(
        body, name="hgrn_fwd", grid=(nc // HGRN_FWD_STEP_CHUNKS,),
        in_specs=[col(2), col(3), col(4), col(5), par(2), par(1)],
        out_specs=[col(0), col(0),
                   pl.BlockSpec((HGRN_FWD_STEP_CHUNKS, N_HEADS, HEAD_D, HEAD_D), lambda c: (c, 0, 0, 0))],
        out_shape=[SDS((seq, D_MODEL), BF16), SDS((seq, D_MODEL), F32),
                   SDS((nc, N_HEADS, HEAD_D, HEAD_D), F32)],
        scratch_shapes=[pltpu.VMEM((N_HEADS, HEAD_D, HEAD_D), F32),
                        pltpu.VMEM((N_SUB, CHUNK, D_MODEL), BF16), pltpu.VMEM((N_SUB, CHUNK, D_MODEL), BF16)]
                       + [pltpu.VMEM((CHUNK, D_MODEL), BF16)] * 3 + [pltpu.VMEM((N_HEADS, CHUNK, CHUNK), BF16)],
        compiler_params=_cparams(("arbitrary",)),
    )(p, p, p, p, lbl, gw)


def _hgrn_backward(p, o, states, dymix, lbl, gw, dp_full, g_w_out, g_small, seq):
    step_rows = HGRN_STEP_CHUNKS * CHUNK
    ns = seq // step_rows

    def body(q_ref, f_ref, v_ref, hg_ref, o_ref, st_ref, dy_ref, lbl_ref, gw_ref, dpin_ref, go_ref, gs_ref,
             dpo_ref, gsm_ref, ro_ref, rs_ref, ds_scr, dp_buf, dp_sems, *rest):
        del dpin_ref
        scratch, sems = rest[:14], rest[14:]
        step = pl.program_id(0)
        slot = step % 2
        exs = [_SlotExchange(go_ref, ro_ref, *sems[0:3], blocked=True),
               _SlotExchange(gs_ref, rs_ref, *sems[3:6], blocked=True)]

        @pl.when(step == 0)
        def _():
            for ex in exs:
                ex.start()

        def out_copy(s, blk):
            rows = pl.ds(pl.multiple_of(blk * step_rows, step_rows), step_rows)
            return pltpu.make_async_copy(dp_buf.at[s], dpo_ref.at[rows, pl.ds(2 * D_MODEL, 4 * D_MODEL)],
                                         dp_sems.at[s])

        @pl.when(step == 0)
        def _():
            ds_scr[...] = jnp.zeros_like(ds_scr)
            gsm_ref[...] = jnp.zeros_like(gsm_ref)

        @pl.when(step >= 2)
        def _():
            out_copy(slot, ns + 1 - step).wait()

        for cc in reversed(range(HGRN_STEP_CHUNKS)):
            chunk(cc, q_ref, f_ref, v_ref, hg_ref, o_ref, st_ref, dy_ref, lbl_ref, gw_ref, gsm_ref, ds_scr,
                  dp_buf.at[slot], *scratch)

        out_copy(slot, ns - 1 - step).start()

        @pl.when(step == ns - 1)
        def _():
            out_copy(1 - slot, 1).wait()
            out_copy(slot, 0).wait()
            for ex in exs:
                ex.wait()

    def chunk(cc, q_ref, f_ref, v_ref, hg_ref, o_ref, st_ref, dy_ref, lbl_ref, gw_ref, gsm_ref, ds_scr, dp_ref,
              qt_scr, kt_scr, qin_scr, kst_scr, vb_scr, dob_scr, g_scr, h_scr, dqi_scr, dks_scr, sd_scr,
              a_scr, da_scr, da0_scr):
        rows = slice(cc * CHUNK, (cc + 1) * CHUNK)
        r = _iota((CHUNK, CHUNK), 0)
        c = _iota((CHUNK, CHUNK), 1)
        tri = jnp.where(c <= r, 1.0, 0.0).astype(BF16)
        triu = jnp.where(c >= r, 1.0, 0.0).astype(BF16)
        diag = (jnp.right_shift(r, 5) == jnp.right_shift(c, 5)) & (c <= r)
        row = _iota((CHUNK, D_MODEL), 0)
        sub = jnp.right_shift(row, 5)

        q = q_ref[rows, :]
        lb, tf, f, tq, qs = _hgrn_gate_terms(q, f_ref[rows, :], lbl_ref[...])
        sig = 0.5 * tf + 0.5
        sq = 0.5 * tq + 0.5
        k = 1.0 - f
        dec = _hgrn_decay(_tri_matmul(tri, jnp.log(f)))
        eb, ekst, ebl = dec["eb"], dec["ekst"], dec["ebl"]
        koff = _hgrn_operands(qs, k, dec, qt_scr, kt_scr)
        qin_scr[...] = (qs * eb).astype(BF16)
        kst_scr[...] = (k * ekst).astype(BF16)
        vb_scr[...] = v_ref[rows, :].astype(BF16)
        hg = hg_ref[rows, :]
        sh = _sigmoid(hg)
        dy = dy_ref[rows, :]
        gwv = gw_ref[...]
        d_onw = dy * (hg * sh)
        d_on = d_onw * gwv
        d_gate = dy * gwv * (sh * (1.0 + hg * (1.0 - sh)))

        heads = [slice(h * HEAD_D, (h + 1) * HEAD_D) for h in range(N_HEADS)]
        for h, sl in enumerate(heads):
            o = o_ref[rows, sl]
            rs = lax.rsqrt(jnp.mean(o * o, axis=-1, keepdims=True) + EPS)
            on = o * rs
            dp_ref[rows, 3 * D_MODEL + h * HEAD_D:3 * D_MODEL + (h + 1) * HEAD_D] = (d_gate[:, sl] * on).astype(BF16)
            gsm_ref[1:2, sl] += jnp.sum(d_onw[:, sl] * on, axis=0, keepdims=True)
            d_onh = d_on[:, sl]
            dob_scr[:, sl] = (rs * (d_onh - on * jnp.mean(d_onh * on, axis=-1, keepdims=True))).astype(BF16)
        for h, sl in enumerate(heads):
            a_scr[h] = _hgrn_head_scores(qt_scr, kt_scr, sl, diag).astype(BF16)
            da = _dot(dob_scr[:, sl], vb_scr[:, sl], NT)
            da_scr[h] = da.astype(BF16)
            da0_scr[h] = jnp.where(diag, da, 0.0).astype(BF16)
        for h, sl in enumerate(heads):
            st = st_ref[cc, h]
            dst = ds_scr[h]
            dstb = dst.astype(BF16)
            dp_ref[rows, 2 * D_MODEL + h * HEAD_D:2 * D_MODEL + (h + 1) * HEAD_D] = (
                _dot(a_scr[h], dob_scr[:, sl], TN) + _dot(kst_scr[:, sl], dstb, NT)).astype(BF16)
            dqi_scr[:, sl] = _dot(dob_scr[:, sl], st.astype(BF16))
            dks_scr[:, sl] = _dot(vb_scr[:, sl], dstb)
            sd_scr[0:1, sl] = jnp.sum(st * dst, axis=0, keepdims=True)
            ds_scr[h] = dst * ebl[:, sl] + _dot(dob_scr[:, sl], qin_scr[:, sl], TN)
        for h, sl in enumerate(heads):
            g_scr[0, :, sl] = _dot(da0_scr[h], kt_scr[0, :, sl])
            h_scr[0, :, sl] = _dot(da0_scr[h], qt_scr[0, :, sl], TN)
            for j in range(1, N_SUB):
                g_scr[j, :, sl] = _dot(da_scr[h], kt_scr[j, :, sl])
                h_scr[j, :, sl] = _dot(da_scr[h], qt_scr[j, :, sl], TN)

        g0 = g_scr[0]
        h0 = h_scr[0]
        dq_inter = eb * dqi_scr[...]
        d_kst = ekst * dks_scr[...]
        db = qs * dq_inter - k * d_kst + qt_scr[0].astype(F32) * g0 - kt_scr[0].astype(F32) * h0
        gq = jnp.zeros((CHUNK, D_MODEL), F32)
        hsel = jnp.zeros((CHUNK, D_MODEL), F32)
        for j in range(N_SUB - 1):
            gj = g_scr[j + 1]
            gq = gq + dec["scales"][j] * gj
            db = db + qt_scr[j + 1].astype(F32) * gj
            hsel = jnp.where(sub == j, h_scr[j + 1], hsel)
        db = db - koff.astype(BF16).astype(F32) * hsel
        d_q = dec["eq0"] * g0 + dec["e_on"] * gq + dq_inter
        d_k = dec["ek0"] * h0 + dec["e_off"] * hsel + d_kst
        db_last = jnp.sum(k * d_kst, axis=0, keepdims=True) + ebl * sd_scr[0:1, :]
        db = db + jnp.where(row == CHUNK - 1, db_last, 0.0)
        dg = _tri_matmul(triu, db)
        d_f = dg / f - d_k
        dp_ref[rows, D_MODEL:2 * D_MODEL] = (d_f * (1.0 - lb) * sig * (1.0 - sig)).astype(BF16)
        gsm_ref[0:1, :] += jnp.sum(d_f * (1.0 - sig), axis=0, keepdims=True) * (lb * (1.0 - lb))
        dp_ref[rows, 0:D_MODEL] = (d_q * (sq * (1.0 + q * (1.0 - sq)))).astype(BF16)

    rc = lambda c: ns - 1 - c
    col = lambda j: pl.BlockSpec((step_rows, D_MODEL), lambda c: (rc(c), j))
    par = lambda rows: pl.BlockSpec((rows, D_MODEL), lambda c: (0, 0))
    return pl.pallas_call(
        body, name="hgrn_bwd", grid=(ns,),
        in_specs=[col(2), col(3), col(4), col(5), col(0),
                  pl.BlockSpec((HGRN_STEP_CHUNKS, N_HEADS, HEAD_D, HEAD_D), lambda c: (rc(c), 0, 0, 0)),
                  col(1), par(2), par(1), ANY, ANY, ANY],
        out_specs=[ANY, par(8), ANY, ANY],
        out_shape=[SDS((seq, D_IN), BF16), SDS((8, D_MODEL), F32), SDS(g_w_out.shape, F32),
                   SDS(g_small.shape, F32)],
        input_output_aliases={9: 0},
        scratch_shapes=[pltpu.VMEM((N_HEADS, HEAD_D, HEAD_D), F32),
                        pltpu.VMEM((2, step_rows, 4 * D_MODEL), BF16), pltpu.SemaphoreType.DMA((2,)),
                        pltpu.VMEM((N_SUB, CHUNK, D_MODEL), BF16), pltpu.VMEM((N_SUB, CHUNK, D_MODEL), BF16)]
                       + [pltpu.VMEM((CHUNK, D_MODEL), BF16)] * 4
                       + [pltpu.VMEM((N_SUB, CHUNK, D_MODEL), F32)] * 2 + [pltpu.VMEM((CHUNK, D_MODEL), F32)] * 2
                       + [pltpu.VMEM((8, D_MODEL), F32)] + [pltpu.VMEM((N_HEADS, CHUNK, CHUNK), BF16)] * 3
                       + EXCHANGE_SEMS * 2,
        compiler_params=_cparams(("arbitrary",)),
    )(p, p, p, p, o, states, dymix, lbl, gw, dp_full, g_w_out, g_small)


def _out_proj(yl, yh, wo, x, tgt, post_w, seq):
    tm = 512

    def body(yl_ref, yh_ref, wo_ref, x_ref, tg_ref, pw_ref, dymix_ref, dout_ref, gwo_ref, st_ref):
        @pl.when(pl.program_id(0) == 0)
        def _():
            gwo_ref[...] = jnp.zeros_like(gwo_ref)
            st_ref[...] = jnp.zeros_like(st_ref)

        ylv = yl_ref[...]
        yhv = yh_ref[...]
        y = _dot(ylv, wo_ref[0:D_MODEL, :]) + _dot(yhv, wo_ref[D_MODEL:D_MIX, :])
        r2 = lax.rsqrt(jnp.mean(y * y, axis=-1, keepdims=True) + EPS)
        yn = y * r2
        pw = pw_ref[...]
        e = (x_ref[...] + yn * pw) - tg_ref[...]
        st_ref[1:2, :] += jnp.sum(e * e, axis=0, keepdims=True) * (0.5 / D_MODEL)
        dout = e * (1.0 / D_MODEL)
        dout_ref[...] = dout
        st_ref[0:1, :] += jnp.sum(dout * yn, axis=0, keepdims=True)
        dyn = dout * pw
        dy = r2 * (dyn - yn * jnp.mean(dyn * yn, axis=-1, keepdims=True))
        dyb = dy.astype(BF16)
        dymix_ref[...] = _dot(dyb, wo_ref[...], NT)
        gwo_ref[0:D_MODEL, :] += _dot(ylv, dyb, TN)
        gwo_ref[D_MODEL:D_MIX, :] += _dot(yhv, dyb, TN)

    row = lambda w: pl.BlockSpec((tm, w), lambda m: (m, 0))
    full = lambda shape: pl.BlockSpec(shape, lambda m: (0,) * len(shape))
    once = lambda shape: pl.BlockSpec(shape, lambda m: (0,) * len(shape), pipeline_mode=pl.Buffered(1))
    return pl.pallas_call(
        body, name="out_proj", grid=(seq // tm,),
        in_specs=[row(D_MODEL), row(D_MODEL), once((D_MIX, D_MODEL)), row(D_MODEL), row(D_MODEL),
                  full((1, D_MODEL))],
        out_specs=[row(D_MIX), row(D_MODEL), once((D_MIX, D_MODEL)), full((8, D_MODEL))],
        out_shape=[SDS((seq, D_MIX), F32), SDS((seq, D_MODEL), F32), SDS((D_MIX, D_MODEL), F32),
                   SDS((8, D_MODEL), F32)],
        compiler_params=_cparams(("arbitrary",)),
    )(yl, yh, wo, x, tgt, post_w)


MESH = pl.DeviceIdType.MESH
ANY = pl.BlockSpec(memory_space=pl.ANY)
EXCHANGE_SEMS = [pltpu.SemaphoreType.DMA((N_DEV - 1,)), pltpu.SemaphoreType.DMA((N_DEV - 1,)),
                 pltpu.SemaphoreType.DMA(())]


def _mesh_pos():
    return lax.axis_index("x"), lax.axis_index("y"), lax.axis_index("c")


class _SlotExchange:
    def __init__(self, src_ref, dst_ref, send_sems, recv_sems, local_sem, blocked):
        x, y, c = _mesh_pos()
        me = 4 * x + 2 * y + c
        src = (lambda dest: src_ref.at[dest]) if blocked else (lambda dest: src_ref)
        self.local = pltpu.make_async_copy(src(me), dst_ref.at[me], local_sem)
        self.sends, self.recvs = [], []
        for k in range(1, N_DEV):
            px = 1 - x if (k >> 2) & 1 else x
            py = 1 - y if (k >> 1) & 1 else y
            pc = 1 - c if k & 1 else c
            peer = 4 * px + 2 * py + pc
            sems = dict(send_sem=send_sems.at[k - 1], recv_sem=recv_sems.at[k - 1],
                        device_id=(px, py, pc), device_id_type=MESH)
            self.sends.append(pltpu.make_async_remote_copy(src_ref=src(peer), dst_ref=dst_ref.at[me], **sems))
            self.recvs.append(pltpu.make_async_remote_copy(src_ref=dst_ref.at[peer], dst_ref=dst_ref.at[peer], **sems))

    def start(self):
        self.local.start()
        for cp in self.sends:
            cp.start()

    def wait(self):
        for cp in self.recvs:
            cp.wait_recv()
        for cp in self.sends:
            cp.wait_send()
        self.local.wait()


class _ChipExchange:
    def __init__(self, src_ref, dst_ref, send_sems, recv_sems, local_sem):
        x, y, c = _mesh_pos()
        chip = 2 * x + y
        self.local = pltpu.make_async_copy(src_ref.at[chip], dst_ref.at[chip], local_sem)
        self.sends, self.recvs = [], []
        for k in range(1, N_CHIPS):
            px = 1 - x if (k >> 1) & 1 else x
            py = 1 - y if k & 1 else y
            peer = 2 * px + py
            sems = dict(send_sem=send_sems.at[k - 1], recv_sem=recv_sems.at[k - 1],
                        device_id=(px, py, c), device_id_type=MESH)
            self.sends.append(pltpu.make_async_remote_copy(src_ref=src_ref.at[peer], dst_ref=dst_ref.at[chip], **sems))
            self.recvs.append(pltpu.make_async_remote_copy(src_ref=dst_ref.at[peer], dst_ref=dst_ref.at[peer], **sems))

    def start(self):
        self.local.start()
        for cp in self.sends:
            cp.start()

    def wait(self):
        for cp in self.recvs:
            cp.wait_recv()
        for cp in self.sends:
            cp.wait_send()
        self.local.wait()


GRAD_W_IN_TK = 2048


def _grad_w_in_sibling(u, dp, core, seq):
    tk = min(GRAD_W_IN_TK, seq)
    nk = seq // tk

    def body(core_ref, u_ref, dp_ref, g_ref):
        del core_ref

        @pl.when(pl.program_id(1) == 0)
        def _():
            g_ref[...] = jnp.zeros_like(g_ref)

        g_ref[0] += _dot(u_ref[...], dp_ref[...], TN)

    return pl.pallas_call(
        body, name="grad_w_in_sibling",
        grid_spec=pltpu.PrefetchScalarGridSpec(
            num_scalar_prefetch=1, grid=(N_CHIPS, nk),
            in_specs=[pl.BlockSpec((tk, D_MODEL), lambda n, k, c: (k, 0)),
                      pl.BlockSpec((tk, W_BLK), lambda n, k, c: (k, 2 * n + 1 - c[0]))],
            out_specs=pl.BlockSpec((1, D_MODEL, W_BLK), lambda n, k, c: (n, 0, 0))),
        out_shape=SDS((N_CHIPS, D_MODEL, W_BLK), F32),
        compiler_params=_cparams(("parallel", "arbitrary")),
    )(core, u, dp)


def _grad_w_in_own(u, dp, core, g_sib, seq):
    tk = min(GRAD_W_IN_TK, seq)
    nk = seq // tk

    def body(core_ref, u_ref, dp_ref, gsib_ref, g_ref, land, send_sem, recv_sem):
        del core_ref
        n = pl.program_id(0)
        k = pl.program_id(1)
        x, y, c = _mesh_pos()
        swap = pltpu.make_async_remote_copy(src_ref=gsib_ref, dst_ref=land, send_sem=send_sem, recv_sem=recv_sem,
                                            device_id=(x, y, 1 - c), device_id_type=MESH)

        @pl.when((n == 0) & (k == 0))
        def _():
            swap.start()

        @pl.when(k == 0)
        def _():
            g_ref[...] = jnp.zeros_like(g_ref)

        g_ref[0] += _dot(u_ref[...], dp_ref[...], TN)

        @pl.when((n == 0) & (k == nk - 1))
        def _():
            swap.wait_recv()

        @pl.when(k == nk - 1)
        def _():
            g_ref[0] += land[n]

        @pl.when((n == N_CHIPS - 1) & (k == nk - 1))
        def _():
            swap.wait_send()

    return pl.pallas_call(
        body, name="grad_w_in_own",
        grid_spec=pltpu.PrefetchScalarGridSpec(
            num_scalar_prefetch=1, grid=(N_CHIPS, nk),
            in_specs=[pl.BlockSpec((tk, D_MODEL), lambda n, k, c: (k, 0)),
                      pl.BlockSpec((tk, W_BLK), lambda n, k, c: (k, 2 * n + c[0])), ANY],
            out_specs=pl.BlockSpec((1, D_MODEL, W_BLK), lambda n, k, c: (n, 0, 0)),
            scratch_shapes=[pltpu.VMEM((N_CHIPS, D_MODEL, W_BLK), F32), pltpu.SemaphoreType.DMA(()),
                            pltpu.SemaphoreType.DMA(())]),
        out_shape=SDS((N_CHIPS, D_MODEL, W_BLK), F32),
        compiler_params=_cparams(("arbitrary", "arbitrary")),
    )(core, u, dp, g_sib)


def _grad_x(dp, w_all, x, pre_w, dout, g_chip, seq):
    tm = 512
    nm = seq // tm

    def body(dp_ref, w_ref, x_ref, pw_ref, do_ref, gsrc_ref, gx_ref, gpw_ref, recv_ref,
             send_sems, recv_sems, local_sem):
        m = pl.program_id(0)
        ex = _ChipExchange(gsrc_ref, recv_ref, send_sems, recv_sems, local_sem)

        @pl.when(m == 0)
        def _():
            ex.start()
            gpw_ref[...] = jnp.zeros_like(gpw_ref)

        du = _dot(dp_ref[:, 0:W_BLK], w_ref[0], NT)
        for j in range(1, N_DEV):
            du = du + _dot(dp_ref[:, j * W_BLK:(j + 1) * W_BLK], w_ref[j], NT)
        xv = x_ref[...]
        r1 = lax.rsqrt(jnp.mean(xv * xv, axis=-1, keepdims=True) + EPS)
        xn = xv * r1
        gpw_ref[0:1, :] += jnp.sum(du * xn, axis=0, keepdims=True)
        dxn = du * pw_ref[...]
        gx_ref[...] = r1 * (dxn - xn * jnp.mean(dxn * xn, axis=-1, keepdims=True)) + do_ref[...]

        @pl.when(m == nm - 1)
        def _():
            ex.wait()

    row = lambda w: pl.BlockSpec((tm, w), lambda m: (m, 0))
    return pl.pallas_call(
        body, name="grad_x", grid=(nm,),
        in_specs=[row(D_IN), pl.BlockSpec((N_DEV, D_MODEL, W_BLK), lambda m: (0, 0, 0), pipeline_mode=pl.Buffered(1)),
                  row(D_MODEL),
                  pl.BlockSpec((1, D_MODEL), lambda m: (0, 0)), row(D_MODEL), ANY],
        out_specs=[row(D_MODEL), pl.BlockSpec((8, D_MODEL), lambda m: (0, 0)), ANY],
        out_shape=[SDS((seq, D_MODEL), F32), SDS((8, D_MODEL), F32), SDS(g_chip.shape, F32)],
        scratch_shapes=[pltpu.SemaphoreType.DMA((N_CHIPS - 1,)), pltpu.SemaphoreType.DMA((N_CHIPS - 1,)),
                        pltpu.SemaphoreType.DMA(())],
        compiler_params=_cparams(("arbitrary",)),
    )(dp, w_all, x, pre_w, dout, g_chip)


def _local_step(x, tgt, p, u, conv_w, conv_b, wa, wx, ba, bx, lam, lbl, gnorm_w, w_out, post_w):
    seq = x.shape[0]
    h, y_lru = _lru_forward(p, conv_w, conv_b, wa, wx, ba, bx, lam, seq)
    y_hgrn, o, states = _hgrn_forward(p, lbl, gnorm_w, seq)
    dymix, dout, g_w_out, stats = _out_proj(y_lru, y_hgrn, w_out, x, tgt, post_w, seq)
    dp_lru, g_wa, g_wx, ls = _lru_backward(p, h, dymix, conv_w, conv_b, wa, wx, ba, bx, lam, seq)
    g_small = _pack_small(_shard_rows(g_wa, LRU_BLOCKS), _shard_rows(g_wx, LRU_BLOCKS),
                          _shard_rows(ls[0:4].reshape(4, D_MODEL, 1), 4).reshape(N_DEV, 4, 128),
                          _shard_rows(ls[5].reshape(4, LRU_BW, 1), 4).reshape(N_DEV, 4, 32),
                          _shard_rows(ls[6].reshape(4, LRU_BW, 1), 4).reshape(N_DEV, 4, 32))
    dp, hgrn_small, r_out, r_small = _hgrn_backward(
        p, o, states, dymix, lbl, gnorm_w, dp_lru, g_w_out.reshape(N_DEV, D_MIX // N_DEV, D_MODEL), g_small, seq)
    return dict(u=u, dp=dp, dout=dout, r_out=r_out, r_small=r_small,
                lru_small=ls, hgrn_small=hgrn_small, stats=stats)


class _TwoLevelGather:
    def __init__(self, ins, outs, send_sems, recv_sems, local_sems):
        self.ins, self.outs = ins, outs
        self.send_sems, self.recv_sems, self.local_sems = send_sems, recv_sems, local_sems
        x, y, c = _mesh_pos()
        self.c = c
        self.me, self.sibling = (x, y, c), (x, y, 1 - c)
        self.chips = [(1 - x, y), (x, 1 - y), (1 - x, 1 - y)]
        n = len(ins)
        self.mine = [pltpu.make_async_copy(ins[a], self._slot(a, self.me), local_sems.at[a]) for a in range(n)]
        self.first = []
        for a in range(n):
            self.first.append(self._copy(a, 0, self.me, self.sibling, src=ins[a]))
            self.first += [self._copy(a, 1 + j, self.me, (*chip, c), src=ins[a])
                           for j, chip in enumerate(self.chips)]
        self.passed = [self._copy(a, 4 + j, (*chip, c), self.sibling)
                       for j, chip in enumerate(self.chips) for a in range(n)]

    def _slot(self, a, pos):
        return self.outs[a].at[4 * pos[0] + 2 * pos[1] + pos[2]]

    def _copy(self, a, k, block, to, src=None):
        dst = self._slot(a, block)
        return pltpu.make_async_remote_copy(
            src_ref=dst if src is None else src, dst_ref=dst,
            send_sem=self.send_sems.at[a, k], recv_sem=self.recv_sems.at[a, k],
            device_id=to, device_id_type=MESH)

    def start(self):
        for cp in self.mine + self.first:
            cp.start()

    def forward(self):
        n = len(self.ins)
        for j, chip in enumerate(self.chips):
            for a in range(n):
                self._copy(a, 1 + j, (*chip, self.c), self.me).wait_recv()
                self.passed[j * n + a].start()

    def finish(self):
        for a in range(len(self.ins)):
            self._copy(a, 0, self.sibling, self.me).wait_recv()
            for j, chip in enumerate(self.chips):
                self._copy(a, 4 + j, (*chip, 1 - self.c), self.me).wait_recv()
        for cp in self.first + self.passed:
            cp.wait_send()
        for cp in self.mine:
            cp.wait()


W_IN_DIRECT = (1, 2, 4, 6)
W_IN_PASSED = (2, 4, 6)
P_STREAMS = 4


def _in_proj_gather(x, pre_w, w_in_blk, w_out_blk, small_blk, me, seq):
    tm = min(1024, seq)
    nm = seq // tm
    last = N_DEV - 1

    def body(me_ref, x_ref, pw_ref, wblk_ref, woblk_ref, smblk_ref,
             p_ref, u_ref, wall_ref, woall_ref, small_ref,
             u_all, w_vmem, own_sem, d_send, d_recv, f_send, f_recv, wb_sems, g_send, g_recv, g_local,
             p_buf, p_sems):
        i = pl.program_id(0)
        m = pl.program_id(1)
        idx = me_ref[0]
        x_, y_, c_ = _mesh_pos()
        aux = _TwoLevelGather([woblk_ref, smblk_ref], [woall_ref, small_ref], g_send, g_recv, g_local)

        def peer(k):
            return (1 - x_ if (k >> 2) & 1 else x_, 1 - y_ if (k >> 1) & 1 else y_, 1 - c_ if k & 1 else c_)

        def direct(k):
            f = W_IN_DIRECT.index(k)
            return (pltpu.make_async_remote_copy(src_ref=wblk_ref, dst_ref=w_vmem.at[idx], send_sem=d_send.at[f],
                                                 recv_sem=d_recv.at[f], device_id=peer(k), device_id_type=MESH),
                    pltpu.make_async_remote_copy(src_ref=w_vmem.at[idx ^ k], dst_ref=w_vmem.at[idx ^ k],
                                                 send_sem=d_send.at[f], recv_sem=d_recv.at[f], device_id=peer(k),
                                                 device_id_type=MESH))

        def passed(k):
            f = W_IN_PASSED.index(k)
            return (pltpu.make_async_remote_copy(src_ref=w_vmem.at[idx ^ k], dst_ref=w_vmem.at[idx ^ k],
                                                 send_sem=f_send.at[f], recv_sem=f_recv.at[f], device_id=peer(1),
                                                 device_id_type=MESH),
                    pltpu.make_async_remote_copy(src_ref=w_vmem.at[idx ^ (k + 1)], dst_ref=w_vmem.at[idx ^ (k + 1)],
                                                 send_sem=f_send.at[f], recv_sem=f_recv.at[f], device_id=peer(1),
                                                 device_id_type=MESH))

        def write_back(k):
            return pltpu.make_async_copy(w_vmem.at[idx ^ k], wall_ref.at[idx ^ k], wb_sems.at[k])

        own = pltpu.make_async_copy(wblk_ref, w_vmem.at[idx], own_sem)

        @pl.when((i == 0) & (m == 0))
        def _():
            own.start()
            for k in (1, 2, 4):
                direct(k)[0].start()
            own.wait()
            write_back(0).start()

        for k in range(1, N_DEV):
            @pl.when((i == k) & (m == 0))
            def _(k=k):
                if k in W_IN_DIRECT:
                    direct(k)[1].wait_recv()
                    if k in W_IN_PASSED:
                        passed(k)[0].start()
                else:
                    passed(k - 1)[1].wait_recv()
                write_back(k).start()
                if k == 2:
                    direct(6)[0].start()
                if k == 4:
                    aux.start()
                if k == N_DEV - 1:
                    aux.forward()

        rows = pl.ds(pl.multiple_of(m * tm, tm), tm)

        @pl.when(i == 0)
        def _():
            xv = x_ref[...]
            r = lax.rsqrt(jnp.mean(xv * xv, axis=-1, keepdims=True) + EPS)
            ub = (xv * r * pw_ref[...]).astype(BF16)
            u_all[rows, :] = ub
            u_ref[...] = ub

        step = i * nm + m
        slot = step % 2
        part = tm // P_STREAMS

        def p_copy(s, part_i, m_, blk):
            dst_rows = pl.ds(pl.multiple_of(m_ * tm + part_i * part, part), part)
            dst_cols = pl.ds(pl.multiple_of(blk * W_BLK, W_BLK), W_BLK)
            return pltpu.make_async_copy(p_buf.at[s, part_i * part:(part_i + 1) * part, :],
                                         p_ref.at[dst_rows, dst_cols], p_sems.at[s, part_i])

        @pl.when(step >= 2)
        def _():
            for j in range(P_STREAMS):
                p_copy(slot, j, m, idx ^ i).wait()

        p_buf[slot] = _dot(u_all[rows, :], w_vmem[idx ^ i])
        for j in range(P_STREAMS):
            p_copy(slot, j, m, idx ^ i).start()

        @pl.when((i == last) & (m == nm - 1))
        def _():
            for j in range(P_STREAMS):
                p_copy(1 - slot, j, m, idx ^ i).wait()
                p_copy(slot, j, m, idx ^ i).wait()
            for k in W_IN_DIRECT:
                direct(k)[0].wait_send()
            for k in W_IN_PASSED:
                passed(k)[0].wait_send()
            for k in range(N_DEV):
                write_back(k).wait()
            aux.finish()

    first_pass = lambda i, m: jnp.where(i == 0, m, nm - 1)
    return pl.pallas_call(
        body, name="in_proj_gather",
        grid_spec=pltpu.PrefetchScalarGridSpec(
            num_scalar_prefetch=1, grid=(N_DEV, nm),
            in_specs=[pl.BlockSpec((tm, D_MODEL), lambda i, m, me: (first_pass(i, m), 0)),
                      pl.BlockSpec((1, D_MODEL), lambda i, m, me: (0, 0)), ANY, ANY, ANY],
            out_specs=[ANY, pl.BlockSpec((tm, D_MODEL), lambda i, m, me: (first_pass(i, m), 0)), ANY, ANY, ANY],
            scratch_shapes=[pltpu.VMEM((seq, D_MODEL), BF16), pltpu.VMEM((N_DEV, D_MODEL, W_BLK), BF16),
                            pltpu.SemaphoreType.DMA(()),
                            pltpu.SemaphoreType.DMA((len(W_IN_DIRECT),)), pltpu.SemaphoreType.DMA((len(W_IN_DIRECT),)),
                            pltpu.SemaphoreType.DMA((len(W_IN_PASSED),)), pltpu.SemaphoreType.DMA((len(W_IN_PASSED),)),
                            pltpu.SemaphoreType.DMA((N_DEV,)),
                            pltpu.SemaphoreType.DMA((2, 7)), pltpu.SemaphoreType.DMA((2, 7)),
                            pltpu.SemaphoreType.DMA((2,)),
                            pltpu.VMEM((2, tm, W_BLK), F32), pltpu.SemaphoreType.DMA((2, P_STREAMS))]),
        out_shape=[SDS((seq, D_IN), F32), SDS((seq, D_MODEL), BF16), SDS((N_DEV, D_MODEL, W_BLK), BF16),
                   SDS((N_DEV,) + w_out_blk.shape, w_out_blk.dtype), SDS((N_DEV,) + small_blk.shape, small_blk.dtype)],
        compiler_params=_cparams(("arbitrary", "arbitrary")),
    )(me, x, pre_w, w_in_blk, w_out_blk, small_blk)


def _exchange_grads(blocks, repl):
    nb = len(blocks)
    n = nb + 1

    def body(*refs):
        ins, outs, sems = refs[:n], refs[n:2 * n], refs[2 * n:]
        exs = [_SlotExchange(ins[a], outs[a], *sems[3 * a:3 * a + 3], blocked=a < nb) for a in range(n)]
        for ex in exs:
            ex.start()
        for ex in exs:
            ex.wait()

    arrs = list(blocks) + [repl]
    shapes = [SDS(b.shape, b.dtype) for b in blocks] + [SDS((N_DEV,) + repl.shape, repl.dtype)]
    return pl.pallas_call(
        body, name="exchange_small", out_shape=shapes,
        in_specs=[ANY] * n, out_specs=[ANY] * n,
        scratch_shapes=EXCHANGE_SEMS * n,
    )(*arrs)


def _pack_rows(picks, name):
    arrs = [p[0] for p in picks]

    def body(*refs):
        out = refs[-1]
        out[...] = jnp.zeros_like(out)
        at = 0
        for ref, (_, row, rows, scale) in zip(refs[:-1], picks):
            out[at:at + rows, :] = ref[row:row + rows, :] * scale
            at += rows

    return pl.pallas_call(body, name=name, out_shape=SDS((8, D_MODEL), F32))(*arrs)


def _adamw(g, w, m, v):
    m2 = ADAM_B1 * m + (1.0 - ADAM_B1) * g
    v2 = ADAM_B2 * v + (1.0 - ADAM_B2) * (g * g)
    m_hat = m2 / (1.0 - ADAM_B1 ** ADAM_STEP)
    v_hat = v2 / (1.0 - ADAM_B2 ** ADAM_STEP)
    delta = -ADAM_LR * (m_hat / (jnp.sqrt(v_hat) + ADAM_EPS) + ADAM_WD * w)
    return delta, m2, v2


def _sum_slots(r_ref):
    g = r_ref[0]
    for s in range(1, r_ref.shape[0]):
        g = g + r_ref[s]
    return g


def _sum_adamw(recv, w, m, v, tr, name):
    rows, cols = w.shape

    def body(r_ref, w_ref, m_ref, v_ref, g_ref, d_ref, m2_ref, v2_ref):
        g = _sum_slots(r_ref)
        g_ref[...] = g
        d_ref[...], m2_ref[...], v2_ref[...] = _adamw(g, w_ref[...], m_ref[...], v_ref[...])

    blk = pl.BlockSpec((tr, cols), lambda i: (i, 0))
    return pl.pallas_call(
        body, name=name, grid=(rows // tr,),
        in_specs=[pl.BlockSpec((recv.shape[0], tr, cols), lambda i: (0, i, 0)), blk, blk, blk],
        out_specs=[blk] * 4, out_shape=[SDS((rows, cols), F32)] * 4,
        compiler_params=_cparams(("parallel",)),
    )(recv, w, m, v)


def _sum_adamw_pieces(recv, rows, ws, ms, vs, name, loss_row=None):
    n = len(ws)

    def body(r_ref, *refs):
        w_refs, m_refs, v_refs, outs = refs[:n], refs[n:2 * n], refs[2 * n:3 * n], refs[3 * n:]
        g = _sum_slots(r_ref)
        for i, (row, nrows) in enumerate(rows):
            gi = g[row:row + nrows, :]
            outs[i][...] = gi
            outs[n + i][...], outs[2 * n + i][...], outs[3 * n + i][...] = _adamw(
                gi, w_refs[i][...], m_refs[i][...], v_refs[i][...])
        if loss_row is not None:
            total = jnp.sum(g[loss_row:loss_row + 1, :], axis=-1, keepdims=True)
            outs[4 * n][...] = jnp.broadcast_to(total, outs[4 * n].shape)

    shapes = [SDS(w.shape, F32) for w in ws] * 4 + ([SDS((8, 128), F32)] if loss_row is not None else [])
    out = pl.pallas_call(body, name=name, out_shape=shapes)(recv, *ws, *ms, *vs)
    return [out[k * n:(k + 1) * n] for k in range(4)] + list(out[4 * n:])


def _shard_rows(t, lead):
    r = t.shape[1] // N_DEV
    t = t.reshape((lead, N_DEV, r) + t.shape[2:])
    return jnp.moveaxis(t, 1, 0)


def _pad8(t):
    return jnp.pad(t, ((0, 0), (0, 8 - t.shape[1]), (0, 0)))


def _pack_small(wa, wx, cw, b_a, b_x):
    n = wa.shape[0]
    return jnp.concatenate([
        wa.reshape(n, 256, 128), wx.reshape(n, 256, 128), _pad8(cw),
        _pad8(b_a.reshape(n, 1, 128)), _pad8(b_x.reshape(n, 1, 128))], axis=1)


def _unpack_small(t):
    n = t.shape[0]
    return (t[:, SM_WA:SM_WA + 256].reshape(n, 4, 32, 256), t[:, SM_WX:SM_WX + 256].reshape(n, 4, 32, 256),
            t[:, SM_CW:SM_CW + 4], t[:, SM_BA].reshape(n, 4, 32), t[:, SM_BX].reshape(n, 4, 32))


def kernel(x, pre_norm_w, w_in, conv_w, conv_b, lru_w_a, lru_b_a, lru_w_x, lru_b_x, lru_lambda, hgrn_lb_logits, hgrn_gnorm_w, w_out, post_norm_w, loss_target, m_pre_norm_w, m_w_in, m_conv_w, m_conv_b, m_lru_w_a, m_lru_b_a, m_lru_w_x, m_lru_b_x, m_lru_lambda, m_hgrn_lb_logits, m_hgrn_gnorm_w, m_w_out, m_post_norm_w, v_pre_norm_w, v_w_in, v_conv_w, v_conv_b, v_lru_w_a, v_lru_b_a, v_lru_w_x, v_lru_b_x, v_lru_lambda, v_hgrn_lb_logits, v_hgrn_gnorm_w, v_w_out, v_post_norm_w):
    seq = x.shape[1]
    x2 = x.reshape(seq, D_MODEL)
    tgt = loss_target.reshape(seq, D_MODEL)

    small_w = _pack_small(lru_w_a, lru_w_x, conv_w, lru_b_a, lru_b_x)[0]
    me = (4 * lax.axis_index("x") + 2 * lax.axis_index("y") + lax.axis_index("c")).astype(jnp.int32).reshape(1)
    p, u, w_in_all, w_out_all, small_all = _in_proj_gather(
        x2, pre_norm_w, w_in[0].astype(BF16), w_out[0].astype(BF16), small_w, me, seq)
    wa_s, wx_s, cw_s, ba_s, bx_s = _unpack_small(small_all)
    wa = jnp.moveaxis(wa_s, 0, 1).reshape(LRU_BLOCKS, LRU_BW, LRU_BW).astype(BF16)
    wx = jnp.moveaxis(wx_s, 0, 1).reshape(LRU_BLOCKS, LRU_BW, LRU_BW).astype(BF16)
    cw = jnp.moveaxis(cw_s, 0, 1).reshape(4, D_MODEL)
    ba = jnp.moveaxis(ba_s, 0, 1).reshape(1, D_MODEL)
    bx = jnp.moveaxis(bx_s, 0, 1).reshape(1, D_MODEL)

    loc = _local_step(x2, tgt, p, u, cw, conv_b, wa, wx, ba, bx, lru_lambda,
                      hgrn_lb_logits, hgrn_gnorm_w, w_out_all.reshape(D_MIX, D_MODEL), post_norm_w)

    ls, r_out, r_small = loc["lru_small"], loc["r_out"], loc["r_small"]
    core = lax.axis_index("c").astype(jnp.int32).reshape(1)
    g_sib = _grad_w_in_sibling(loc["u"], loc["dp"], core, seq)
    g_chip = _grad_w_in_own(loc["u"], loc["dp"], core, g_sib, seq)
    grad_x, pre_small, r_in = _grad_x(loc["dp"], w_in_all, x2, pre_norm_w, loc["dout"], g_chip, seq)
    g_repl = _pack_rows([(pre_small, 0, 1, 1.0), (ls, 4, 1, 1.0), (ls, 7, 1, 1.0),
                         (loc["hgrn_small"], 0, 1, 1.0), (loc["hgrn_small"], 0, 1, -1.0),
                         (loc["hgrn_small"], 1, 1, 1.0), (loc["stats"], 0, 2, 1.0)], "pack_grads")
    (r_repl,) = _exchange_grads([], g_repl)

    repl_rows = [(RP_PRE, 1), (RP_CB, 1), (RP_LAM, 1), (RP_LB0, 2), (RP_GN, 1), (RP_POST, 1)]
    o_repl = _sum_adamw_pieces(
        r_repl, repl_rows,
        [pre_norm_w, conv_b, lru_lambda, hgrn_lb_logits, hgrn_gnorm_w, post_norm_w],
        [m_pre_norm_w, m_conv_b, m_lru_lambda, m_hgrn_lb_logits, m_hgrn_gnorm_w, m_post_norm_w],
        [v_pre_norm_w, v_conv_b, v_lru_lambda, v_hgrn_lb_logits, v_hgrn_gnorm_w, v_post_norm_w],
        "adamw_repl", loss_row=RP_LOSS)
    loss = o_repl[4][0, 0]

    small_rows = [(SM_WA, 256), (SM_WX, 256), (SM_CW, 4), (SM_BA, 1), (SM_BX, 1)]
    as_rows = lambda wa_, wx_, cw_, ba_, bx_: [wa_.reshape(256, 128), wx_.reshape(256, 128), cw_.reshape(4, 128),
                                               ba_.reshape(1, 128), bx_.reshape(1, 128)]
    o_small = _sum_adamw_pieces(
        r_small, small_rows, as_rows(lru_w_a, lru_w_x, conv_w, lru_b_a, lru_b_x),
        as_rows(m_lru_w_a, m_lru_w_x, m_conv_w, m_lru_b_a, m_lru_b_x),
        as_rows(v_lru_w_a, v_lru_w_x, v_conv_w, v_lru_b_a, v_lru_b_x), "adamw_small")

    o_in = _sum_adamw(r_in, w_in[0], m_w_in[0], v_w_in[0], 128, "adamw_w_in")
    o_out = _sum_adamw(r_out, w_out[0], m_w_out[0], v_w_out[0], 64, "adamw_w_out")

    outs = [loss, grad_x.reshape(x.shape)]
    for kind in range(4):
        pre, cb, lam, lb, gn, post = o_repl[kind]
        swa, swx, scw, sba, sbx = o_small[kind]
        outs += [pre, o_in[kind][None], scw.reshape(conv_w.shape), cb, swa.reshape(lru_w_a.shape),
                 sba.reshape(lru_b_a.shape), swx.reshape(lru_w_x.shape), sbx.reshape(lru_b_x.shape),
                 lam, lb, gn, o_out[kind][None], post]
    return tuple(outs)
```

```python
import functools

import jax
import jax.numpy as jnp
from jax import lax
from jax.experimental import pallas as pl
from jax.experimental.pallas import tpu as pltpu

F32 = jnp.float32
BF16 = jnp.bfloat16
SDS = jax.ShapeDtypeStruct

D_MODEL = 1024
D_IN = 6144
N_DEV = 8
N_CHIPS = 4
W_BLK = D_IN // N_DEV
D_MIX = 2048
LRU_BLOCKS = 4
LRU_BW = 256
LRU_C = 8.0
N_HEADS = 8
HEAD_D = 128
CHUNK = 128
SUB = 32
N_SUB = CHUNK // SUB
HGRN_FWD_STEP_CHUNKS = 4
HGRN_STEP_CHUNKS = 2
EXP_CLAMP = 80.0
EPS = 1e-6

ADAM_LR = 0.001
ADAM_B1 = 0.9
ADAM_B2 = 0.999
ADAM_EPS = 1e-08
ADAM_WD = 0.01
ADAM_STEP = 10

VMEM_LIMIT = 56 * 1024 * 1024

NN = (((1,), (0,)), ((), ()))
NT = (((1,), (1,)), ((), ()))
TN = (((0,), (0,)), ((), ()))

SM_WA = 0
SM_WX = 256
SM_CW = 512
SM_BA = 520
SM_BX = 528
SM_ROWS = 536

RP_PRE, RP_CB, RP_LAM, RP_LB0, RP_LB1, RP_GN, RP_POST, RP_LOSS = range(8)


def _dot(a, b, dims=NN):
    return lax.dot_general(a, b, dims, preferred_element_type=F32)


def _sigmoid(x):
    return 0.5 * jnp.tanh(0.5 * x) + 0.5


def _sigmoid_pos(x):
    return 1.0 / (1.0 + jnp.exp(-x))


def _cparams(sem, vmem=VMEM_LIMIT):
    return pltpu.CompilerParams(dimension_semantics=sem, vmem_limit_bytes=vmem)


def _iota(shape, axis):
    return lax.broadcasted_iota(jnp.int32, shape, axis)


def _softplus_neg(lam):
    z = -lam
    e = jnp.exp(-jnp.abs(z))
    u = 1.0 + e
    log1p_e = jnp.where(u == 1.0, e, jnp.log(u) * (e / (u - 1.0)))
    sp = jnp.maximum(z, 0.0) + log1p_e
    dsp = -jnp.where(z >= 0.0, 1.0 / u, e / u)
    return sp, dsp


def _neg_expm1(x):
    poly = x * (1.0 + x * (1.0 / 2 + x * (1.0 / 6 + x * (1.0 / 24 + x * (1.0 / 120)))))
    return jnp.where(x > -1.0 / 16, -poly, 1.0 - jnp.exp(x))


def _conv_taps(lx, prev8, cw_ref, cb_ref, tile):
    xc = cb_ref[...] + cw_ref[3:4, :] * lx
    for j in (1, 2, 3):
        xc = xc + cw_ref[3 - j:4 - j, :] * pltpu.roll(lx, j, 0)
    row8 = _iota((8, D_MODEL), 0)
    last8 = lx[tile - 8:tile, :]
    fix = jnp.zeros((8, D_MODEL), F32)
    for j in (1, 2, 3):
        wrong = pltpu.roll(last8, j, 0)
        right = pltpu.roll(prev8, j, 0)
        fix = fix + cw_ref[3 - j:4 - j, :] * jnp.where(row8 < j, right - wrong, 0.0)
    return xc, fix


def _lru_gates(xcs, wa, wx, ba, bx, sp):
    xb = xcs.astype(BF16)
    r = _sigmoid_pos(_dot(xb, wa) + ba)
    i = _sigmoid(_dot(xb, wx) + bx)
    la = (-LRU_C * sp) * r
    a = jnp.exp(la)
    one_minus_a2 = _neg_expm1(2.0 * la)
    return r, i, a, one_minus_a2


def _lru_forward(p, conv_w, conv_b, wa, wx, ba, bx, lam, seq):
    tile = min(512, seq // 2)
    nblk = tile // 8

    def body(lx_ref, gt_ref, cw_ref, cb_ref, wa_ref, wx_ref, ba_ref, bx_ref, lam_ref,
             h_ref, y_ref, ext, hcar, xc_scr, a_scr, u_scr):
        @pl.when(pl.program_id(0) == 0)
        def _():
            ext[0:8, :] = jnp.zeros((8, D_MODEL), F32)
            hcar[...] = jnp.zeros_like(hcar)

        lx = lx_ref[...]
        ext[8:8 + tile, :] = lx
        xc = cb_ref[...] + cw_ref[3:4, :] * lx
        for j in (1, 2, 3):
            xc = xc + cw_ref[3 - j:4 - j, :] * ext[8 - j:8 - j + tile, :]
        xc_scr[...] = xc
        ext[0:8, :] = lx_ref[tile - 8:tile, :]
        sp, _ = _softplus_neg(lam_ref[...])
        for n in range(LRU_BLOCKS):
            sl = slice(n * LRU_BW, (n + 1) * LRU_BW)
            xcs = xc_scr[:, sl]
            _, i, a, ne = _lru_gates(xcs, wa_ref[n], wx_ref[n], ba_ref[:, sl], bx_ref[:, sl], sp[:, sl])
            a_scr[:, sl] = a
            u_scr[:, sl] = jnp.sqrt(ne) * (i * xcs)

        row8 = _iota((8, D_MODEL), 0)

        def blk(j, hc):
            off = pl.multiple_of(j * 8, 8)
            a = a_scr[pl.ds(off, 8), :]
            u = u_scr[pl.ds(off, 8), :]
            for k in (1, 2, 4):
                m = row8 >= k
                u = jnp.where(m, u + a * pltpu.roll(u, k, 0), u)
                a = jnp.where(m, a * pltpu.roll(a, k, 0), a)
            h = u + a * hc
            h_ref[pl.ds(off, 8), :] = h
            return jnp.broadcast_to(h[7:8, :], (8, D_MODEL))

        hcar[...] = lax.fori_loop(0, nblk, blk, hcar[...])
        g = gt_ref[...]
        y_ref[...] = (h_ref[...] * (g * _sigmoid(g))).astype(BF16)

    full = lambda shape: pl.BlockSpec(shape, lambda t: (0,) * len(shape))
    return pl.pallas_call(
        body, name="lru_fwd", grid=(seq // tile,),
        in_specs=[pl.BlockSpec((tile, D_MODEL), lambda t: (t, 0)),
                  pl.BlockSpec((tile, D_MODEL), lambda t: (t, 1)),
                  full((4, D_MODEL)), full((1, D_MODEL)),
                  full((LRU_BLOCKS, LRU_BW, LRU_BW)), full((LRU_BLOCKS, LRU_BW, LRU_BW)),
                  full((1, D_MODEL)), full((1, D_MODEL)), full((1, D_MODEL))],
        out_specs=[pl.BlockSpec((tile, D_MODEL), lambda t: (t, 0)),
                   pl.BlockSpec((tile, D_MODEL), lambda t: (t, 0))],
        out_shape=[SDS((seq, D_MODEL), F32), SDS((seq, D_MODEL), BF16)],
        scratch_shapes=[pltpu.VMEM((tile + 8, D_MODEL), F32), pltpu.VMEM((8, D_MODEL), F32),
                        pltpu.VMEM((tile, D_MODEL), F32), pltpu.VMEM((tile, D_MODEL), F32),
                        pltpu.VMEM((tile, D_MODEL), F32)],
        compiler_params=_cparams(("arbitrary",)),
    )(p, p, conv_w, conv_b, wa, wx, ba, bx, lam)


def _lru_backward(p, h, dymix, conv_w, conv_b, wa, wx, ba, bx, lam, seq):
    tile = min(256, seq // 2)
    nt = seq // tile
    nblk = tile // 8
    t8 = tile // 8

    def body(lx_ref, lxh_ref, gt_ref, h_ref, hh_ref, dy_ref, cw_ref, cb_ref, wa_ref, wx_ref, ba_ref,
             bx_ref, lam_ref, dp_ref, gwa_ref, gwx_ref, gsm_ref,
             lamcar, anext, dxc8, xc_scr, r_scr, i_scr, a_scr, m_scr, rm_scr, c_scr, l_scr, dxc_scr):
        step = pl.program_id(0)
        first_tile = step == nt - 1

        @pl.when(step == 0)
        def _():
            lamcar[...] = jnp.zeros_like(lamcar)
            anext[...] = jnp.zeros_like(anext)
            dxc8[...] = jnp.zeros_like(dxc8)
            gwa_ref[...] = jnp.zeros_like(gwa_ref)
            gwx_ref[...] = jnp.zeros_like(gwx_ref)
            gsm_ref[...] = jnp.zeros_like(gsm_ref)

        keep = jnp.where(first_tile, 0.0, 1.0)
        lx = lx_ref[...]
        prev8 = lxh_ref[...] * keep
        xc, fix = _conv_taps(lx, prev8, cw_ref, cb_ref, tile)
        xc_scr[...] = xc
        xc_scr[0:8, :] = xc_scr[0:8, :] + fix
        sp, dsp = _softplus_neg(lam_ref[...])
        for n in range(LRU_BLOCKS):
            sl = slice(n * LRU_BW, (n + 1) * LRU_BW)
            r, i, a, ne = _lru_gates(xc_scr[:, sl], wa_ref[n], wx_ref[n], ba_ref[:, sl], bx_ref[:, sl],
                                     sp[:, sl])
            r_scr[:, sl] = r
            i_scr[:, sl] = i
            a_scr[:, sl] = a
            m_scr[:, sl] = jnp.sqrt(ne)
            rm_scr[:, sl] = lax.rsqrt(ne)

        g = gt_ref[...]
        sg = _sigmoid(g)
        dy = dy_ref[...]
        hv = h_ref[...]
        dp_ref[:, D_MODEL:2 * D_MODEL] = (dy * hv * (sg * (1.0 + g * (1.0 - sg)))).astype(BF16)

        rowt = _iota((tile, D_MODEL), 0)
        av = a_scr[...]
        l_scr[...] = dy * (g * sg)
        c_scr[...] = jnp.where(rowt == tile - 1, anext[...][0:1, :], pltpu.roll(av, tile - 1, 0))
        anext[...] = jnp.broadcast_to(av[0:1, :], (8, D_MODEL))
        row8 = _iota((8, D_MODEL), 0)

        def blk(jj, lc):
            off = pl.multiple_of((nblk - 1 - jj) * 8, 8)
            c = c_scr[pl.ds(off, 8), :]
            u = l_scr[pl.ds(off, 8), :]
            for k in (1, 2, 4):
                m = row8 < 8 - k
                u = jnp.where(m, u + c * pltpu.roll(u, 8 - k, 0), u)
                c = jnp.where(m, c * pltpu.roll(c, 8 - k, 0), c)
            lamv = u + c * lc
            l_scr[pl.ds(off, 8), :] = lamv
            return jnp.broadcast_to(lamv[0:1, :], (8, D_MODEL))

        lamcar[...] = lax.fori_loop(0, nblk, blk, lamcar[...])

        hprev = jnp.where(rowt == 0, hh_ref[...][7:8, :] * keep, pltpu.roll(hv, 1, 0))
        for n in range(LRU_BLOCKS):
            sl = slice(n * LRU_BW, (n + 1) * LRU_BW)
            lamv = l_scr[:, sl]
            xcs = xc_scr[:, sl]
            r = r_scr[:, sl]
            i = i_scr[:, sl]
            a = a_scr[:, sl]
            mult = m_scr[:, sl]
            d_la = lamv * hprev[:, sl] * a - (lamv * i * xcs) * (a * a * rm_scr[:, sl])
            d_pr = d_la * (-LRU_C * sp[:, sl]) * r * (1.0 - r)
            d_pi = (lamv * mult * xcs) * i * (1.0 - i)
            gsm_ref[7:8, sl] += jnp.sum(d_la * r, axis=0, keepdims=True) * (-LRU_C) * dsp[:, sl]
            gsm_ref[5:6, sl] += jnp.sum(d_pr, axis=0, keepdims=True)
            gsm_ref[6:7, sl] += jnp.sum(d_pi, axis=0, keepdims=True)
            xb = xcs.astype(BF16)
            prb = d_pr.astype(BF16)
            pib = d_pi.astype(BF16)
            gwa_ref[n] += _dot(xb, prb, TN)
            gwx_ref[n] += _dot(xb, pib, TN)
            dxc_scr[:, sl] = lamv * mult * i + _dot(prb, wa_ref[n], NT) + _dot(pib, wx_ref[n], NT)

        dxc = dxc_scr[...]
        gsm_ref[4:5, :] += jnp.sum(dxc, axis=0, keepdims=True)
        last8 = lx[tile - 8:tile, :]
        first8 = dxc[0:8, :]
        dlx = cw_ref[3:4, :] * dxc
        gsm_ref[3:4, :] += jnp.sum(dxc * lx, axis=0, keepdims=True)
        fix = jnp.zeros((8, D_MODEL), F32)
        for j in (1, 2, 3):
            w = cw_ref[3 - j:4 - j, :]
            dlx = dlx + w * pltpu.roll(dxc, tile - j, 0)
            fix = fix + w * jnp.where(row8 + j >= 8,
                                      pltpu.roll(dxc8[...], 8 - j, 0) - pltpu.roll(first8, 8 - j, 0), 0.0)
            halo = jnp.where(row8 < j, pltpu.roll(prev8, j, 0) - pltpu.roll(last8, j, 0), 0.0)
            gsm_ref[3 - j:4 - j, :] += (jnp.sum(dxc * pltpu.roll(lx, j, 0), axis=0, keepdims=True)
                                        + jnp.sum(first8 * halo, axis=0, keepdims=True))
        dxc8[...] = first8
        dp_ref[:, 0:D_MODEL] = dlx.astype(BF16)
        top = tile - 8
        dp_ref[top:tile, 0:D_MODEL] = (dlx[top:tile, :] + fix).astype(BF16)

    rev = lambda t: (nt - 1 - t, 0)
    halo_idx = lambda t: (jnp.maximum((nt - 1 - t) * t8 - 1, 0), 0)
    full = lambda shape: pl.BlockSpec(shape, lambda t: (0,) * len(shape))
    big = lambda: pltpu.VMEM((tile, D_MODEL), F32)
    return pl.pallas_call(
        body, name="lru_bwd", grid=(nt,),
        in_specs=[pl.BlockSpec((tile, D_MODEL), rev),
                  pl.BlockSpec((8, D_MODEL), halo_idx),
                  pl.BlockSpec((tile, D_MODEL), lambda t: (nt - 1 - t, 1)),
                  pl.BlockSpec((tile, D_MODEL), rev),
                  pl.BlockSpec((8, D_MODEL), halo_idx),
                  pl.BlockSpec((tile, D_MODEL), rev),
                  full((4, D_MODEL)), full((1, D_MODEL)),
                  full((LRU_BLOCKS, LRU_BW, LRU_BW)), full((LRU_BLOCKS, LRU_BW, LRU_BW)),
                  full((1, D_MODEL)), full((1, D_MODEL)), full((1, D_MODEL))],
        out_specs=[pl.BlockSpec((tile, 2 * D_MODEL), rev),
                   full((LRU_BLOCKS, LRU_BW, LRU_BW)), full((LRU_BLOCKS, LRU_BW, LRU_BW)),
                   full((8, D_MODEL))],
        out_shape=[SDS((seq, D_IN), BF16), SDS((LRU_BLOCKS, LRU_BW, LRU_BW), F32),
                   SDS((LRU_BLOCKS, LRU_BW, LRU_BW), F32), SDS((8, D_MODEL), F32)],
        scratch_shapes=[pltpu.VMEM((8, D_MODEL), F32), pltpu.VMEM((8, D_MODEL), F32),
                        pltpu.VMEM((8, D_MODEL), F32)] + [big() for _ in range(9)],
        compiler_params=_cparams(("arbitrary",)),
    )(p, p, p, h, h, dymix, conv_w, conv_b, wa, wx, ba, bx, lam)


def _tri_matmul(tri, g):
    hi = g.astype(BF16)
    lo = (g - hi.astype(F32)).astype(BF16)
    return _dot(tri, lo) + _dot(tri, hi)


def _hgrn_gate_terms(q, fr, lbl):
    lb = _sigmoid_pos(lbl[0:1, :] - lbl[1:2, :])
    half = 0.5 * (1.0 - lb)
    tf = jnp.tanh(0.5 * fr)
    f = (lb + half) + half * tf
    hq = 0.5 * q
    tq = jnp.tanh(hq)
    return lb, tf, f, tq, hq * tq + hq


def _hgrn_decay(bh):
    zero = jnp.zeros((1, bh.shape[1]), F32)
    rho = [zero] + [bh[s * SUB - 1:s * SUB, :] for s in range(1, N_SUB + 1)]
    start = _sub_rows(rho[0:N_SUB])
    end = _sub_rows(rho[1:N_SUB + 1])
    mid = 0.5 * (start + end)
    blast = rho[N_SUB]
    e_on = jnp.exp(bh - start)
    e_off = jnp.exp(end - bh)
    scales = [_sub_rows([jnp.exp(rho[i] - rho[j + 1]) if i > j else zero for i in range(N_SUB)])
              for j in range(N_SUB - 1)]
    return dict(eq0=jnp.exp(jnp.minimum(bh - mid, EXP_CLAMP)), ek0=jnp.exp(jnp.minimum(mid - bh, EXP_CLAMP)),
                e_on=e_on, e_off=e_off, scales=scales,
                eb=e_on * _sub_rows([jnp.exp(r) for r in rho[0:N_SUB]]),
                ekst=e_off * _sub_rows([jnp.exp(blast - r) for r in rho[1:N_SUB + 1]]),
                ebl=jnp.exp(blast))


def _sub_rows(vecs):
    return jnp.concatenate([jnp.broadcast_to(v, (SUB, v.shape[1])) for v in vecs], axis=0)


def _hgrn_operands(qs, k, dec, qt_scr, kt_scr):
    sub = jnp.right_shift(_iota(qs.shape, 0), 5)
    qon = qs * dec["e_on"]
    koff = k * dec["e_off"]
    qt_scr[0] = (qs * dec["eq0"]).astype(BF16)
    kt_scr[0] = (k * dec["ek0"]).astype(BF16)
    for j in range(N_SUB - 1):
        qt_scr[j + 1] = (qon * dec["scales"][j]).astype(BF16)
        kt_scr[j + 1] = jnp.where(sub == j, koff, 0.0).astype(BF16)
    return koff


def _hgrn_head_scores(qt_scr, kt_scr, sl, diag):
    a = jnp.where(diag, _dot(qt_scr[0, :, sl], kt_scr[0, :, sl], NT), 0.0)
    for j in range(1, N_SUB):
        a = a + _dot(qt_scr[j, :, sl], kt_scr[j, :, sl], NT)
    return a


def _hgrn_forward(p, lbl, gw, seq):
    nc = seq // CHUNK
    assert SUB == 32

    def body(q_ref, f_ref, v_ref, hg_ref, lbl_ref, gw_ref, y_ref, o_ref, st_ref,
             s_scr, qt_scr, kt_scr, qin_scr, kst_scr, vb_scr, a_scr):
        @pl.when(pl.program_id(0) == 0)
        def _():
            s_scr[...] = jnp.zeros_like(s_scr)

        r = _iota((CHUNK, CHUNK), 0)
        c = _iota((CHUNK, CHUNK), 1)
        tri = jnp.where(c <= r, 1.0, 0.0).astype(BF16)
        diag = (jnp.right_shift(r, 5) == jnp.right_shift(c, 5)) & (c <= r)
        heads = [slice(h * HEAD_D, (h + 1) * HEAD_D) for h in range(N_HEADS)]
        for cc in range(HGRN_FWD_STEP_CHUNKS):
            rows = slice(cc * CHUNK, (cc + 1) * CHUNK)
            q = q_ref[rows, :]
            _, _, f, _, qs = _hgrn_gate_terms(q, f_ref[rows, :], lbl_ref[...])
            k = 1.0 - f
            dec = _hgrn_decay(_tri_matmul(tri, jnp.log(f)))
            _hgrn_operands(qs, k, dec, qt_scr, kt_scr)
            qin_scr[...] = (qs * dec["eb"]).astype(BF16)
            kst_scr[...] = (k * dec["ekst"]).astype(BF16)
            vb_scr[...] = v_ref[rows, :].astype(BF16)
            ebl = dec["ebl"]
            hg = hg_ref[rows, :]
            gate = gw_ref[...] * (hg * _sigmoid(hg))
            stb = []
            for h, sl in enumerate(heads):
                st = s_scr[h]
                st_ref[cc, h] = st
                stb.append(st.astype(BF16))
                s_scr[h] = st * ebl[:, sl] + _dot(vb_scr[:, sl], kst_scr[:, sl], TN)
            for h, sl in enumerate(heads):
                a_scr[h] = _hgrn_head_scores(qt_scr, kt_scr, sl, diag).astype(BF16)
            for h, sl in enumerate(heads):
                o = _dot(a_scr[h], vb_scr[:, sl]) + _dot(qin_scr[:, sl], stb[h], NT)
                o_ref[rows, sl] = o
                rs = lax.rsqrt(jnp.mean(o * o, axis=-1, keepdims=True) + EPS)
                y_ref[rows, sl] = ((o * rs) * gate[:, sl]).astype(BF16)

    col = lambda j: pl.BlockSpec((HGRN_FWD_STEP_CHUNKS * CHUNK, D_MODEL), lambda c: (c, j))
    par = lambda rows: pl.BlockSpec((rows, D_MODEL), lambda c: (0, 0))
    return pl.pallas_call(
        body, name="hgrn_fwd", grid=(nc // HGRN_FWD_STEP_CHUNKS,),
        in_specs=[col(2), col(3), col(4), col(5), par(2), par(1)],
        out_specs=[col(0), col(0),
                   pl.BlockSpec((HGRN_FWD_STEP_CHUNKS, N_HEADS, HEAD_D, HEAD_D), lambda c: (c, 0, 0, 0))],
        out_shape=[SDS((seq, D_MODEL), BF16), SDS((seq, D_MODEL), F32),
                   SDS((nc, N_HEADS, HEAD_D, HEAD_D), F32)],
        scratch_shapes=[pltpu.VMEM((N_HEADS, HEAD_D, HEAD_D), F32),
                        pltpu.VMEM((N_SUB, CHUNK, D_MODEL), BF16), pltpu.VMEM((N_SUB, CHUNK, D_MODEL), BF16)]
                       + [pltpu.VMEM((CHUNK, D_MODEL), BF16)] * 3 + [pltpu.VMEM((N_HEADS, CHUNK, CHUNK), BF16)],
        compiler_params=_cparams(("arbitrary",)),
    )(p, p, p, p, lbl, gw)


def _hgrn_backward(p, o, states, dymix, lbl, gw, dp_full, g_w_out, g_small, seq):
    step_rows = HGRN_STEP_CHUNKS * CHUNK
    ns = seq // step_rows

    def body(q_ref, f_ref, v_ref, hg_ref, o_ref, st_ref, dy_ref, lbl_ref, gw_ref, dpin_ref, go_ref, gs_ref,
             dpo_ref, gsm_ref, ro_ref, rs_ref, ds_scr, dp_buf, dp_sems, *rest):
        del dpin_ref
        scratch, sems = rest[:14], rest[14:]
        step = pl.program_id(0)
        slot = step % 2
        exs = [_SlotExchange(go_ref, ro_ref, *sems[0:3], blocked=True),
               _SlotExchange(gs_ref, rs_ref, *sems[3:6], blocked=True)]

        @pl.when(step == 0)
        def _():
            for ex in exs:
                ex.start()

        def out_copy(s, blk):
            rows = pl.ds(pl.multiple_of(blk * step_rows, step_rows), step_rows)
            return pltpu.make_async_copy(dp_buf.at[s], dpo_ref.at[rows, pl.ds(2 * D_MODEL, 4 * D_MODEL)],
                                         dp_sems.at[s])

        @pl.when(step == 0)
        def _():
            ds_scr[...] = jnp.zeros_like(ds_scr)
            gsm_ref[...] = jnp.zeros_like(gsm_ref)

        @pl.when(step >= 2)
        def _():
            out_copy(slot, ns + 1 - step).wait()

        for cc in reversed(range(HGRN_STEP_CHUNKS)):
            chunk(cc, q_ref, f_ref, v_ref, hg_ref, o_ref, st_ref, dy_ref, lbl_ref, gw_ref, gsm_ref, ds_scr,
                  dp_buf.at[slot], *scratch)

        out_copy(slot, ns - 1 - step).start()

        @pl.when(step == ns - 1)
        def _():
            out_copy(1 - slot, 1).wait()
            out_copy(slot, 0).wait()
            for ex in exs:
                ex.wait()

    def chunk(cc, q_ref, f_ref, v_ref, hg_ref, o_ref, st_ref, dy_ref, lbl_ref, gw_ref, gsm_ref, ds_scr, dp_ref,
              qt_scr, kt_scr, qin_scr, kst_scr, vb_scr, dob_scr, g_scr, h_scr, dqi_scr, dks_scr, sd_scr,
              a_scr, da_scr, da0_scr):
        rows = slice(cc * CHUNK, (cc + 1) * CHUNK)
        r = _iota((CHUNK, CHUNK), 0)
        c = _iota((CHUNK, CHUNK), 1)
        tri = jnp.where(c <= r, 1.0, 0.0).astype(BF16)
        triu = jnp.where(c >= r, 1.0, 0.0).astype(BF16)
        diag = (jnp.right_shift(r, 5) == jnp.right_shift(c, 5)) & (c <= r)
        row = _iota((CHUNK, D_MODEL), 0)
        sub = jnp.right_shift(row, 5)

        q = q_ref[rows, :]
        lb, tf, f, tq, qs = _hgrn_gate_terms(q, f_ref[rows, :], lbl_ref[...])
        sig = 0.5 * tf + 0.5
        sq = 0.5 * tq + 0.5
        k = 1.0 - f
        dec = _hgrn_decay(_tri_matmul(tri, jnp.log(f)))
        eb, ekst, ebl = dec["eb"], dec["ekst"], dec["ebl"]
        koff = _hgrn_operands(qs, k, dec, qt_scr, kt_scr)
        qin_scr[...] = (qs * eb).astype(BF16)
        kst_scr[...] = (k * ekst).astype(BF16)
        vb_scr[...] = v_ref[rows, :].astype(BF16)
        hg = hg_ref[rows, :]
        sh = _sigmoid(hg)
        dy = dy_ref[rows, :]
        gwv = gw_ref[...]
        d_onw = dy * (hg * sh)
        d_on = d_onw * gwv
        d_gate = dy * gwv * (sh * (1.0 + hg * (1.0 - sh)))

        heads = [slice(h * HEAD_D, (h + 1) * HEAD_D) for h in range(N_HEADS)]
        for h, sl in enumerate(heads):
            o = o_ref[rows, sl]
            rs = lax.rsqrt(jnp.mean(o * o, axis=-1, keepdims=True) + EPS)
            on = o * rs
            dp_ref[rows, 3 * D_MODEL + h * HEAD_D:3 * D_MODEL + (h + 1) * HEAD_D] = (d_gate[:, sl] * on).astype(BF16)
            gsm_ref[1:2, sl] += jnp.sum(d_onw[:, sl] * on, axis=0, keepdims=True)
            d_onh = d_on[:, sl]
            dob_scr[:, sl] = (rs * (d_onh - on * jnp.mean(d_onh * on, axis=-1, keepdims=True))).astype(BF16)
        for h, sl in enumerate(heads):
            a_scr[h] = _hgrn_head_scores(qt_scr, kt_scr, sl, diag).astype(BF16)
            da = _dot(dob_scr[:, sl], vb_scr[:, sl], NT)
            da_scr[h] = da.astype(BF16)
            da0_scr[h] = jnp.where(diag, da, 0.0).astype(BF16)
        for h, sl in enumerate(heads):
            st = st_ref[cc, h]
            dst = ds_scr[h]
            dstb = dst.astype(BF16)
            dp_ref[rows, 2 * D_MODEL + h * HEAD_D:2 * D_MODEL + (h + 1) * HEAD_D] = (
                _dot(a_scr[h], dob_scr[:, sl], TN) + _dot(kst_scr[:, sl], dstb, NT)).astype(BF16)
            dqi_scr[:, sl] = _dot(dob_scr[:, sl], st.astype(BF16))
            dks_scr[:, sl] = _dot(vb_scr[:, sl], dstb)
            sd_scr[0:1, sl] = jnp.sum(st * dst, axis=0, keepdims=True)
            ds_scr[h] = dst * ebl[:, sl] + _dot(dob_scr[:, sl], qin_scr[:, sl], TN)
        for h, sl in enumerate(heads):
            g_scr[0, :, sl] = _dot(da0_scr[h], kt_scr[0, :, sl])
            h_scr[0, :, sl] = _dot(da0_scr[h], qt_scr[0, :, sl], TN)
            for j in range(1, N_SUB):
                g_scr[j, :, sl] = _dot(da_scr[h], kt_scr[j, :, sl])
                h_scr[j, :, sl] = _dot(da_scr[h], qt_scr[j, :, sl], TN)

        g0 = g_scr[0]
        h0 = h_scr[0]
        dq_inter = eb * dqi_scr[...]
        d_kst = ekst * dks_scr[...]
        db = qs * dq_inter - k * d_kst + qt_scr[0].astype(F32) * g0 - kt_scr[0].astype(F32) * h0
        gq = jnp.zeros((CHUNK, D_MODEL), F32)
        hsel = jnp.zeros((CHUNK, D_MODEL), F32)
        for j in range(N_SUB - 1):
            gj = g_scr[j + 1]
            gq = gq + dec["scales"][j] * gj
            db = db + qt_scr[j + 1].astype(F32) * gj
            hsel = jnp.where(sub == j, h_scr[j + 1], hsel)
        db = db - koff.astype(BF16).astype(F32) * hsel
        d_q = dec["eq0"] * g0 + dec["e_on"] * gq + dq_inter
        d_k = dec["ek0"] * h0 + dec["e_off"] * hsel + d_kst
        db_last = jnp.sum(k * d_kst, axis=0, keepdims=True) + ebl * sd_scr[0:1, :]
        db = db + jnp.where(row == CHUNK - 1, db_last, 0.0)
        dg = _tri_matmul(triu, db)
        d_f = dg / f - d_k
        dp_ref[rows, D_MODEL:2 * D_MODEL] = (d_f * (1.0 - lb) * sig * (1.0 - sig)).astype(BF16)
        gsm_ref[0:1, :] += jnp.sum(d_f * (1.0 - sig), axis=0, keepdims=True) * (lb * (1.0 - lb))
        dp_ref[rows, 0:D_MODEL] = (d_q * (sq * (1.0 + q * (1.0 - sq)))).astype(BF16)

    rc = lambda c: ns - 1 - c
    col = lambda j: pl.BlockSpec((step_rows, D_MODEL), lambda c: (rc(c), j))
    par = lambda rows: pl.BlockSpec((rows, D_MODEL), lambda c: (0, 0))
    return pl.pallas_call(
        body, name="hgrn_bwd", grid=(ns,),
        in_specs=[col(2), col(3), col(4), col(5), col(0),
                  pl.BlockSpec((HGRN_STEP_CHUNKS, N_HEADS, HEAD_D, HEAD_D), lambda c: (rc(c), 0, 0, 0)),
                  col(1), par(2), par(1), ANY, ANY, ANY],
        out_specs=[ANY, par(8), ANY, ANY],
        out_shape=[SDS((seq, D_IN), BF16), SDS((8, D_MODEL), F32), SDS(g_w_out.shape, F32),
                   SDS(g_small.shape, F32)],
        input_output_aliases={9: 0},
        scratch_shapes=[pltpu.VMEM((N_HEADS, HEAD_D, HEAD_D), F32),
                        pltpu.VMEM((2, step_rows, 4 * D_MODEL), BF16), pltpu.SemaphoreType.DMA((2,)),
                        pltpu.VMEM((N_SUB, CHUNK, D_MODEL), BF16), pltpu.VMEM((N_SUB, CHUNK, D_MODEL), BF16)]
                       + [pltpu.VMEM((CHUNK, D_MODEL), BF16)] * 4
                       + [pltpu.VMEM((N_SUB, CHUNK, D_MODEL), F32)] * 2 + [pltpu.VMEM((CHUNK, D_MODEL), F32)] * 2
                       + [pltpu.VMEM((8, D_MODEL), F32)] + [pltpu.VMEM((N_HEADS, CHUNK, CHUNK), BF16)] * 3
                       + EXCHANGE_SEMS * 2,
        compiler_params=_cparams(("arbitrary",)),
    )(p, p, p, p, o, states, dymix, lbl, gw, dp_full, g_w_out, g_small)


def _out_proj(yl, yh, wo, x, tgt, post_w, seq):
    tm = 512

    def body(yl_ref, yh_ref, wo_ref, x_ref, tg_ref, pw_ref, dymix_ref, dout_ref, gwo_ref, st_ref):
        @pl.when(pl.program_id(0) == 0)
        def _():
            gwo_ref[...] = jnp.zeros_like(gwo_ref)
            st_ref[...] = jnp.zeros_like(st_ref)

        ylv = yl_ref[...]
        yhv = yh_ref[...]
        y = _dot(ylv, wo_ref[0:D_MODEL, :]) + _dot(yhv, wo_ref[D_MODEL:D_MIX, :])
        r2 = lax.rsqrt(jnp.mean(y * y, axis=-1, keepdims=True) + EPS)
        yn = y * r2
        pw = pw_ref[...]
        e = (x_ref[...] + yn * pw) - tg_ref[...]
        st_ref[1:2, :] += jnp.sum(e * e, axis=0, keepdims=True) * (0.5 / D_MODEL)
        dout = e * (1.0 / D_MODEL)
        dout_ref[...] = dout
        st_ref[0:1, :] += jnp.sum(dout * yn, axis=0, keepdims=True)
        dyn = dout * pw
        dy = r2 * (dyn - yn * jnp.mean(dyn * yn, axis=-1, keepdims=True))
        dyb = dy.astype(BF16)
        dymix_ref[...] = _dot(dyb, wo_ref[...], NT)
        gwo_ref[0:D_MODEL, :] += _dot(ylv, dyb, TN)
        gwo_ref[D_MODEL:D_MIX, :] += _dot(yhv, dyb, TN)

    row = lambda w: pl.BlockSpec((tm, w), lambda m: (m, 0))
    full = lambda shape: pl.BlockSpec(shape, lambda m: (0,) * len(shape))
    once = lambda shape: pl.BlockSpec(shape, lambda m: (0,) * len(shape), pipeline_mode=pl.Buffered(1))
    return pl.pallas_call(
        body, name="out_proj", grid=(seq // tm,),
        in_specs=[row(D_MODEL), row(D_MODEL), once((D_MIX, D_MODEL)), row(D_MODEL), row(D_MODEL),
                  full((1, D_MODEL))],
        out_specs=[row(D_MIX), row(D_MODEL), once((D_MIX, D_MODEL)), full((8, D_MODEL))],
        out_shape=[SDS((seq, D_MIX), F32), SDS((seq, D_MODEL), F32), SDS((D_MIX, D_MODEL), F32),
                   SDS((8, D_MODEL), F32)],
        compiler_params=_cparams(("arbitrary",)),
    )(yl, yh, wo, x, tgt, post_w)


MESH = pl.DeviceIdType.MESH
ANY = pl.BlockSpec(memory_space=pl.ANY)
EXCHANGE_SEMS = [pltpu.SemaphoreType.DMA((N_DEV - 1,)), pltpu.SemaphoreType.DMA((N_DEV - 1,)),
                 pltpu.SemaphoreType.DMA(())]


def _mesh_pos():
    return lax.axis_index("x"), lax.axis_index("y"), lax.axis_index("c")


class _SlotExchange:
    def __init__(self, src_ref, dst_ref, send_sems, recv_sems, local_sem, blocked):
        x, y, c = _mesh_pos()
        me = 4 * x + 2 * y + c
        src = (lambda dest: src_ref.at[dest]) if blocked else (lambda dest: src_ref)
        self.local = pltpu.make_async_copy(src(me), dst_ref.at[me], local_sem)
        self.sends, self.recvs = [], []
        for k in range(1, N_DEV):
            px = 1 - x if (k >> 2) & 1 else x
            py = 1 - y if (k >> 1) & 1 else y
            pc = 1 - c if k & 1 else c
            peer = 4 * px + 2 * py + pc
            sems = dict(send_sem=send_sems.at[k - 1], recv_sem=recv_sems.at[k - 1],
                        device_id=(px, py, pc), device_id_type=MESH)
            self.sends.append(pltpu.make_async_remote_copy(src_ref=src(peer), dst_ref=dst_ref.at[me], **sems))
            self.recvs.append(pltpu.make_async_remote_copy(src_ref=dst_ref.at[peer], dst_ref=dst_ref.at[peer], **sems))

    def start(self):
        self.local.start()
        for cp in self.sends:
            cp.start()

    def wait(self):
        for cp in self.recvs:
            cp.wait_recv()
        for cp in self.sends:
            cp.wait_send()
        self.local.wait()


class _ChipExchange:
    def __init__(self, src_ref, dst_ref, send_sems, recv_sems, local_sem):
        x, y, c = _mesh_pos()
        chip = 2 * x + y
        self.local = pltpu.make_async_copy(src_ref.at[chip], dst_ref.at[chip], local_sem)
        self.sends, self.recvs = [], []
        for k in range(1, N_CHIPS):
            px = 1 - x if (k >> 1) & 1 else x
            py = 1 - y if k & 1 else y
            peer = 2 * px + py
            sems = dict(send_sem=send_sems.at[k - 1], recv_sem=recv_sems.at[k - 1],
                        device_id=(px, py, c), device_id_type=MESH)
            self.sends.append(pltpu.make_async_remote_copy(src_ref=src_ref.at[peer], dst_ref=dst_ref.at[chip], **sems))
            self.recvs.append(pltpu.make_async_remote_copy(src_ref=dst_ref.at[peer], dst_ref=dst_ref.at[peer], **sems))

    def start(self):
        self.local.start()
        for cp in self.sends:
            cp.start()

    def wait(self):
        for cp in self.recvs:
            cp.wait_recv()
        for cp in self.sends:
            cp.wait_send()
        self.local.wait()


GRAD_W_IN_TK = 2048


def _grad_w_in_sibling(u, dp, core, seq):
    tk = min(GRAD_W_IN_TK, seq)
    nk = seq // tk

    def body(core_ref, u_ref, dp_ref, g_ref):
        del core_ref

        @pl.when(pl.program_id(1) == 0)
        def _():
            g_ref[...] = jnp.zeros_like(g_ref)

        g_ref[0] += _dot(u_ref[...], dp_ref[...], TN)

    return pl.pallas_call(
        body, name="grad_w_in_sibling",
        grid_spec=pltpu.PrefetchScalarGridSpec(
            num_scalar_prefetch=1, grid=(N_CHIPS, nk),
            in_specs=[pl.BlockSpec((tk, D_MODEL), lambda n, k, c: (k, 0)),
                      pl.BlockSpec((tk, W_BLK), lambda n, k, c: (k, 2 * n + 1 - c[0]))],
            out_specs=pl.BlockSpec((1, D_MODEL, W_BLK), lambda n, k, c: (n, 0, 0))),
        out_shape=SDS((N_CHIPS, D_MODEL, W_BLK), F32),
        compiler_params=_cparams(("parallel", "arbitrary")),
    )(core, u, dp)


def _grad_w_in_own(u, dp, core, g_sib, seq):
    tk = min(GRAD_W_IN_TK, seq)
    nk = seq // tk

    def body(core_ref, u_ref, dp_ref, gsib_ref, g_ref, land, send_sem, recv_sem):
        del core_ref
        n = pl.program_id(0)
        k = pl.program_id(1)
        x, y, c = _mesh_pos()
        swap = pltpu.make_async_remote_copy(src_ref=gsib_ref, dst_ref=land, send_sem=send_sem, recv_sem=recv_sem,
                                            device_id=(x, y, 1 - c), device_id_type=MESH)

        @pl.when((n == 0) & (k == 0))
        def _():
            swap.start()

        @pl.when(k == 0)
        def _():
            g_ref[...] = jnp.zeros_like(g_ref)

        g_ref[0] += _dot(u_ref[...], dp_ref[...], TN)

        @pl.when((n == 0) & (k == nk - 1))
        def _():
            swap.wait_recv()

        @pl.when(k == nk - 1)
        def _():
            g_ref[0] += land[n]

        @pl.when((n == N_CHIPS - 1) & (k == nk - 1))
        def _():
            swap.wait_send()

    return pl.pallas_call(
        body, name="grad_w_in_own",
        grid_spec=pltpu.PrefetchScalarGridSpec(
            num_scalar_prefetch=1, grid=(N_CHIPS, nk),
            in_specs=[pl.BlockSpec((tk, D_MODEL), lambda n, k, c: (k, 0)),
                      pl.BlockSpec((tk, W_BLK), lambda n, k, c: (k, 2 * n + c[0])), ANY],
            out_specs=pl.BlockSpec((1, D_MODEL, W_BLK), lambda n, k, c: (n, 0, 0)),
            scratch_shapes=[pltpu.VMEM((N_CHIPS, D_MODEL, W_BLK), F32), pltpu.SemaphoreType.DMA(()),
                            pltpu.SemaphoreType.DMA(())]),
        out_shape=SDS((N_CHIPS, D_MODEL, W_BLK), F32),
        compiler_params=_cparams(("arbitrary", "arbitrary")),
    )(core, u, dp, g_sib)


def _grad_x(dp, w_all, x, pre_w, dout, g_chip, seq):
    tm = 512
    nm = seq // tm

    def body(dp_ref, w_ref, x_ref, pw_ref, do_ref, gsrc_ref, gx_ref, gpw_ref, recv_ref,
             send_sems, recv_sems, local_sem):
        m = pl.program_id(0)
        ex = _ChipExchange(gsrc_ref, recv_ref, send_sems, recv_sems, local_sem)

        @pl.when(m == 0)
        def _():
            ex.start()
            gpw_ref[...] = jnp.zeros_like(gpw_ref)

        du = _dot(dp_ref[:, 0:W_BLK], w_ref[0], NT)
        for j in range(1, N_DEV):
            du = du + _dot(dp_ref[:, j * W_BLK:(j + 1) * W_BLK], w_ref[j], NT)
        xv = x_ref[...]
        r1 = lax.rsqrt(jnp.mean(xv * xv, axis=-1, keepdims=True) + EPS)
        xn = xv * r1
        gpw_ref[0:1, :] += jnp.sum(du * xn, axis=0, keepdims=True)
        dxn = du * pw_ref[...]
        gx_ref[...] = r1 * (dxn - xn * jnp.mean(dxn * xn, axis=-1, keepdims=True)) + do_ref[...]

        @pl.when(m == nm - 1)
        def _():
            ex.wait()

    row = lambda w: pl.BlockSpec((tm, w), lambda m: (m, 0))
    return pl.pallas_call(
        body, name="grad_x", grid=(nm,),
        in_specs=[row(D_IN), pl.BlockSpec((N_DEV, D_MODEL, W_BLK), lambda m: (0, 0, 0), pipeline_mode=pl.Buffered(1)),
                  row(D_MODEL),
                  pl.BlockSpec((1, D_MODEL), lambda m: (0, 0)), row(D_MODEL), ANY],
        out_specs=[row(D_MODEL), pl.BlockSpec((8, D_MODEL), lambda m: (0, 0)), ANY],
        out_shape=[SDS((seq, D_MODEL), F32), SDS((8, D_MODEL), F32), SDS(g_chip.shape, F32)],
        scratch_shapes=[pltpu.SemaphoreType.DMA((N_CHIPS - 1,)), pltpu.SemaphoreType.DMA((N_CHIPS - 1,)),
                        pltpu.SemaphoreType.DMA(())],
        compiler_params=_cparams(("arbitrary",)),
    )(dp, w_all, x, pre_w, dout, g_chip)


def _local_step(x, tgt, p, u, conv_w, conv_b, wa, wx, ba, bx, lam, lbl, gnorm_w, w_out, post_w):
    seq = x.shape[0]
    h, y_lru = _lru_forward(p, conv_w, conv_b, wa, wx, ba, bx, lam, seq)
    y_hgrn, o, states = _hgrn_forward(p, lbl, gnorm_w, seq)
    dymix, dout, g_w_out, stats = _out_proj(y_lru, y_hgrn, w_out, x, tgt, post_w, seq)
    dp_lru, g_wa, g_wx, ls = _lru_backward(p, h, dymix, conv_w, conv_b, wa, wx, ba, bx, lam, seq)
    g_small = _pack_small(_shard_rows(g_wa, LRU_BLOCKS), _shard_rows(g_wx, LRU_BLOCKS),
                          _shard_rows(ls[0:4].reshape(4, D_MODEL, 1), 4).reshape(N_DEV, 4, 128),
                          _shard_rows(ls[5].reshape(4, LRU_BW, 1), 4).reshape(N_DEV, 4, 32),
                          _shard_rows(ls[6].reshape(4, LRU_BW, 1), 4).reshape(N_DEV, 4, 32))
    dp, hgrn_small, r_out, r_small = _hgrn_backward(
        p, o, states, dymix, lbl, gnorm_w, dp_lru, g_w_out.reshape(N_DEV, D_MIX // N_DEV, D_MODEL), g_small, seq)
    return dict(u=u, dp=dp, dout=dout, r_out=r_out, r_small=r_small,
                lru_small=ls, hgrn_small=hgrn_small, stats=stats)


class _TwoLevelGather:
    def __init__(self, ins, outs, send_sems, recv_sems, local_sems):
        self.ins, self.outs = ins, outs
        self.send_sems, self.recv_sems, self.local_sems = send_sems, recv_sems, local_sems
        x, y, c = _mesh_pos()
        self.c = c
        self.me, self.sibling = (x, y, c), (x, y, 1 - c)
        self.chips = [(1 - x, y), (x, 1 - y), (1 - x, 1 - y)]
        n = len(ins)
        self.mine = [pltpu.make_async_copy(ins[a], self._slot(a, self.me), local_sems.at[a]) for a in range(n)]
        self.first = []
        for a in range(n):
            self.first.append(self._copy(a, 0, self.me, self.sibling, src=ins[a]))
            self.first += [self._copy(a, 1 + j, self.me, (*chip, c), src=ins[a])
                           for j, chip in enumerate(self.chips)]
        self.passed = [self._copy(a, 4 + j, (*chip, c), self.sibling)
                       for j, chip in enumerate(self.chips) for a in range(n)]

    def _slot(self, a, pos):
        return self.outs[a].at[4 * pos[0] + 2 * pos[1] + pos[2]]

    def _copy(self, a, k, block, to, src=None):
        dst = self._slot(a, block)
        return pltpu.make_async_remote_copy(
            src_ref=dst if src is None else src, dst_ref=dst,
            send_sem=self.send_sems.at[a, k], recv_sem=self.recv_sems.at[a, k],
            device_id=to, device_id_type=MESH)

    def start(self):
        for cp in self.mine + self.first:
            cp.start()

    def forward(self):
        n = len(self.ins)
        for j, chip in enumerate(self.chips):
            for a in range(n):
                self._copy(a, 1 + j, (*chip, self.c), self.me).wait_recv()
                self.passed[j * n + a].start()

    def finish(self):
        for a in range(len(self.ins)):
            self._copy(a, 0, self.sibling, self.me).wait_recv()
            for j, chip in enumerate(self.chips):
                self._copy(a, 4 + j, (*chip, 1 - self.c), self.me).wait_recv()
        for cp in self.first + self.passed:
            cp.wait_send()
        for cp in self.mine:
            cp.wait()


W_IN_DIRECT = (1, 2, 4, 6)
W_IN_PASSED = (2, 4, 6)
P_STREAMS = 4


def _in_proj_gather(x, pre_w, w_in_blk, w_out_blk, small_blk, me, seq):
    tm = min(1024, seq)
    nm = seq // tm
    last = N_DEV - 1

    def body(me_ref, x_ref, pw_ref, wblk_ref, woblk_ref, smblk_ref,
             p_ref, u_ref, wall_ref, woall_ref, small_ref,
             u_all, w_vmem, own_sem, d_send, d_recv, f_send, f_recv, wb_sems, g_send, g_recv, g_local,
             p_buf, p_sems):
        i = pl.program_id(0)
        m = pl.program_id(1)
        idx = me_ref[0]
        x_, y_, c_ = _mesh_pos()
        aux = _TwoLevelGather([woblk_ref, smblk_ref], [woall_ref, small_ref], g_send, g_recv, g_local)

        def peer(k):
            return (1 - x_ if (k >> 2) & 1 else x_, 1 - y_ if (k >> 1) & 1 else y_, 1 - c_ if k & 1 else c_)

        def direct(k):
            f = W_IN_DIRECT.index(k)
            return (pltpu.make_async_remote_copy(src_ref=wblk_ref, dst_ref=w_vmem.at[idx], send_sem=d_send.at[f],
                                                 recv_sem=d_recv.at[f], device_id=peer(k), device_id_type=MESH),
                    pltpu.make_async_remote_copy(src_ref=w_vmem.at[idx ^ k], dst_ref=w_vmem.at[idx ^ k],
                                                 send_sem=d_send.at[f], recv_sem=d_recv.at[f], device_id=peer(k),
                                                 device_id_type=MESH))

        def passed(k):
            f = W_IN_PASSED.index(k)
            return (pltpu.make_async_remote_copy(src_ref=w_vmem.at[idx ^ k], dst_ref=w_vmem.at[idx ^ k],
                                                 send_sem=f_send.at[f], recv_sem=f_recv.at[f], device_id=peer(1),
                                                 device_id_type=MESH),
                    pltpu.make_async_remote_copy(src_ref=w_vmem.at[idx ^ (k + 1)], dst_ref=w_vmem.at[idx ^ (k + 1)],
                                                 send_sem=f_send.at[f], recv_sem=f_recv.at[f], device_id=peer(1),
                                                 device_id_type=MESH))

        def write_back(k):
            return pltpu.make_async_copy(w_vmem.at[idx ^ k], wall_ref.at[idx ^ k], wb_sems.at[k])

        own = pltpu.make_async_copy(wblk_ref, w_vmem.at[idx], own_sem)

        @pl.when((i == 0) & (m == 0))
        def _():
            own.start()
            for k in (1, 2, 4):
                direct(k)[0].start()
            own.wait()
            write_back(0).start()

        for k in range(1, N_DEV):
            @pl.when((i == k) & (m == 0))
            def _(k=k):
                if k in W_IN_DIRECT:
                    direct(k)[1].wait_recv()
                    if k in W_IN_PASSED:
                        passed(k)[0].start()
                else:
                    passed(k - 1)[1].wait_recv()
                write_back(k).start()
                if k == 2:
                    direct(6)[0].start()
                if k == 4:
                    aux.start()
                if k == N_DEV - 1:
                    aux.forward()

        rows = pl.ds(pl.multiple_of(m * tm, tm), tm)

        @pl.when(i == 0)
        def _():
            xv = x_ref[...]
            r = lax.rsqrt(jnp.mean(xv * xv, axis=-1, keepdims=True) + EPS)
            ub = (xv * r * pw_ref[...]).astype(BF16)
            u_all[rows, :] = ub
            u_ref[...] = ub

        step = i * nm + m
        slot = step % 2
        part = tm // P_STREAMS

        def p_copy(s, part_i, m_, blk):
            dst_rows = pl.ds(pl.multiple_of(m_ * tm + part_i * part, part), part)
            dst_cols = pl.ds(pl.multiple_of(blk * W_BLK, W_BLK), W_BLK)
            return pltpu.make_async_copy(p_buf.at[s, part_i * part:(part_i + 1) * part, :],
                                         p_ref.at[dst_rows, dst_cols], p_sems.at[s, part_i])

        @pl.when(step >= 2)
        def _():
            for j in range(P_STREAMS):
                p_copy(slot, j, m, idx ^ i).wait()

        p_buf[slot] = _dot(u_all[rows, :], w_vmem[idx ^ i])
        for j in range(P_STREAMS):
            p_copy(slot, j, m, idx ^ i).start()

        @pl.when((i == last) & (m == nm - 1))
        def _():
            for j in range(P_STREAMS):
                p_copy(1 - slot, j, m, idx ^ i).wait()
                p_copy(slot, j, m, idx ^ i).wait()
            for k in W_IN_DIRECT:
                direct(k)[0].wait_send()
            for k in W_IN_PASSED:
                passed(k)[0].wait_send()
            for k in range(N_DEV):
                write_back(k).wait()
            aux.finish()

    first_pass = lambda i, m: jnp.where(i == 0, m, nm - 1)
    return pl.pallas_call(
        body, name="in_proj_gather",
        grid_spec=pltpu.PrefetchScalarGridSpec(
            num_scalar_prefetch=1, grid=(N_DEV, nm),
            in_specs=[pl.BlockSpec((tm, D_MODEL), lambda i, m, me: (first_pass(i, m), 0)),
                      pl.BlockSpec((1, D_MODEL), lambda i, m, me: (0, 0)), ANY, ANY, ANY],
            out_specs=[ANY, pl.BlockSpec((tm, D_MODEL), lambda i, m, me: (first_pass(i, m), 0)), ANY, ANY, ANY],
            scratch_shapes=[pltpu.VMEM((seq, D_MODEL), BF16), pltpu.VMEM((N_DEV, D_MODEL, W_BLK), BF16),
                            pltpu.SemaphoreType.DMA(()),
                            pltpu.SemaphoreType.DMA((len(W_IN_DIRECT),)), pltpu.SemaphoreType.DMA((len(W_IN_DIRECT),)),
                            pltpu.SemaphoreType.DMA((len(W_IN_PASSED),)), pltpu.SemaphoreType.DMA((len(W_IN_PASSED),)),
                            pltpu.SemaphoreType.DMA((N_DEV,)),
                            pltpu.SemaphoreType.DMA((2, 7)), pltpu.SemaphoreType.DMA((2, 7)),
                            pltpu.SemaphoreType.DMA((2,)),
                            pltpu.VMEM((2, tm, W_BLK), F32), pltpu.SemaphoreType.DMA((2, P_STREAMS))]),
        out_shape=[SDS((seq, D_IN), F32), SDS((seq, D_MODEL), BF16), SDS((N_DEV, D_MODEL, W_BLK), BF16),
                   SDS((N_DEV,) + w_out_blk.shape, w_out_blk.dtype), SDS((N_DEV,) + small_blk.shape, small_blk.dtype)],
        compiler_params=_cparams(("arbitrary", "arbitrary")),
    )(me, x, pre_w, w_in_blk, w_out_blk, small_blk)


def _exchange_grads(blocks, repl):
    nb = len(blocks)
    n = nb + 1

    def body(*refs):
        ins, outs, sems = refs[:n], refs[n:2 * n], refs[2 * n:]
        exs = [_SlotExchange(ins[a], outs[a], *sems[3 * a:3 * a + 3], blocked=a < nb) for a in range(n)]
        for ex in exs:
            ex.start()
        for ex in exs:
            ex.wait()

    arrs = list(blocks) + [repl]
    shapes = [SDS(b.shape, b.dtype) for b in blocks] + [SDS((N_DEV,) + repl.shape, repl.dtype)]
    return pl.pallas_call(
        body, name="exchange_small", out_shape=shapes,
        in_specs=[ANY] * n, out_specs=[ANY] * n,
        scratch_shapes=EXCHANGE_SEMS * n,
    )(*arrs)


def _pack_rows(picks, name):
    arrs = [p[0] for p in picks]

    def body(*refs):
        out = refs[-1]
        out[...] = jnp.zeros_like(out)
        at = 0
        for ref, (_, row, rows, scale) in zip(refs[:-1], picks):
            out[at:at + rows, :] = ref[row:row + rows, :] * scale
            at += rows

    return pl.pallas_call(body, name=name, out_shape=SDS((8, D_MODEL), F32))(*arrs)


def _adamw(g, w, m, v):
    m2 = ADAM_B1 * m + (1.0 - ADAM_B1) * g
    v2 = ADAM_B2 * v + (1.0 - ADAM_B2) * (g * g)
    m_hat = m2 / (1.0 - ADAM_B1 ** ADAM_STEP)
    v_hat = v2 / (1.0 - ADAM_B2 ** ADAM_STEP)
    delta = -ADAM_LR * (m_hat / (jnp.sqrt(v_hat) + ADAM_EPS) + ADAM_WD * w)
    return delta, m2, v2


def _sum_slots(r_ref):
    g = r_ref[0]
    for s in range(1, r_ref.shape[0]):
        g = g + r_ref[s]
    return g


def _sum_adamw(recv, w, m, v, tr, name, through=None):
    rows, cols = w.shape
    extra = [] if through is None else [through]

    def body(r_ref, w_ref, m_ref, v_ref, *refs):
        g_ref, d_ref, m2_ref, v2_ref = refs[len(extra):len(extra) + 4]
        g = _sum_slots(r_ref)
        g_ref[...] = g
        d_ref[...], m2_ref[...], v2_ref[...] = _adamw(g, w_ref[...], m_ref[...], v_ref[...])

    blk = pl.BlockSpec((tr, cols), lambda i: (i, 0))
    return pl.pallas_call(
        body, name=name, grid=(rows // tr,),
        in_specs=[pl.BlockSpec((recv.shape[0], tr, cols), lambda i: (0, i, 0)), blk, blk, blk] + [ANY] * len(extra),
        out_specs=[blk] * 4 + [ANY] * len(extra),
        out_shape=[SDS((rows, cols), F32)] * 4 + [SDS(t.shape, t.dtype) for t in extra],
        input_output_aliases={4 + j: 4 + j for j in range(len(extra))},
        compiler_params=_cparams(("parallel",)),
    )(recv, w, m, v, *extra)


def _sum_adamw_pieces(recv, rows, ws, ms, vs, name, loss_row=None):
    n = len(ws)

    def body(r_ref, *refs):
        w_refs, m_refs, v_refs, outs = refs[:n], refs[n:2 * n], refs[2 * n:3 * n], refs[3 * n:]
        g = _sum_slots(r_ref)
        for i, (row, nrows) in enumerate(rows):
            gi = g[row:row + nrows, :]
            outs[i][...] = gi
            outs[n + i][...], outs[2 * n + i][...], outs[3 * n + i][...] = _adamw(
                gi, w_refs[i][...], m_refs[i][...], v_refs[i][...])
        if loss_row is not None:
            total = jnp.sum(g[loss_row:loss_row + 1, :], axis=-1, keepdims=True)
            outs[4 * n][...] = jnp.broadcast_to(total, outs[4 * n].shape)

    shapes = [SDS(w.shape, F32) for w in ws] * 4 + ([SDS((8, 128), F32)] if loss_row is not None else [])
    out = pl.pallas_call(body, name=name, out_shape=shapes)(recv, *ws, *ms, *vs)
    return [out[k * n:(k + 1) * n] for k in range(4)] + list(out[4 * n:])


def _shard_rows(t, lead):
    r = t.shape[1] // N_DEV
    t = t.reshape((lead, N_DEV, r) + t.shape[2:])
    return jnp.moveaxis(t, 1, 0)


def _pad8(t):
    return jnp.pad(t, ((0, 0), (0, 8 - t.shape[1]), (0, 0)))


def _pack_small(wa, wx, cw, b_a, b_x):
    n = wa.shape[0]
    return jnp.concatenate([
        wa.reshape(n, 256, 128), wx.reshape(n, 256, 128), _pad8(cw),
        _pad8(b_a.reshape(n, 1, 128)), _pad8(b_x.reshape(n, 1, 128))], axis=1)


def _unpack_small(t):
    n = t.shape[0]
    return (t[:, SM_WA:SM_WA + 256].reshape(n, 4, 32, 256), t[:, SM_WX:SM_WX + 256].reshape(n, 4, 32, 256),
            t[:, SM_CW:SM_CW + 4], t[:, SM_BA].reshape(n, 4, 32), t[:, SM_BX].reshape(n, 4, 32))


def kernel(x, pre_norm_w, w_in, conv_w, conv_b, lru_w_a, lru_b_a, lru_w_x, lru_b_x, lru_lambda, hgrn_lb_logits, hgrn_gnorm_w, w_out, post_norm_w, loss_target, m_pre_norm_w, m_w_in, m_conv_w, m_conv_b, m_lru_w_a, m_lru_b_a, m_lru_w_x, m_lru_b_x, m_lru_lambda, m_hgrn_lb_logits, m_hgrn_gnorm_w, m_w_out, m_post_norm_w, v_pre_norm_w, v_w_in, v_conv_w, v_conv_b, v_lru_w_a, v_lru_b_a, v_lru_w_x, v_lru_b_x, v_lru_lambda, v_hgrn_lb_logits, v_hgrn_gnorm_w, v_w_out, v_post_norm_w):
    seq = x.shape[1]
    x2 = x.reshape(seq, D_MODEL)
    tgt = loss_target.reshape(seq, D_MODEL)

    small_w = _pack_small(lru_w_a, lru_w_x, conv_w, lru_b_a, lru_b_x)[0]
    me = (4 * lax.axis_index("x") + 2 * lax.axis_index("y") + lax.axis_index("c")).astype(jnp.int32).reshape(1)
    p, u, w_in_all, w_out_all, small_all = _in_proj_gather(
        x2, pre_norm_w, w_in[0].astype(BF16), w_out[0].astype(BF16), small_w, me, seq)
    wa_s, wx_s, cw_s, ba_s, bx_s = _unpack_small(small_all)
    wa = jnp.moveaxis(wa_s, 0, 1).reshape(LRU_BLOCKS, LRU_BW, LRU_BW).astype(BF16)
    wx = jnp.moveaxis(wx_s, 0, 1).reshape(LRU_BLOCKS, LRU_BW, LRU_BW).astype(BF16)
    cw = jnp.moveaxis(cw_s, 0, 1).reshape(4, D_MODEL)
    ba = jnp.moveaxis(ba_s, 0, 1).reshape(1, D_MODEL)
    bx = jnp.moveaxis(bx_s, 0, 1).reshape(1, D_MODEL)

    loc = _local_step(x2, tgt, p, u, cw, conv_b, wa, wx, ba, bx, lru_lambda,
                      hgrn_lb_logits, hgrn_gnorm_w, w_out_all.reshape(D_MIX, D_MODEL), post_norm_w)

    ls, r_out, r_small = loc["lru_small"], loc["r_out"], loc["r_small"]
    core = lax.axis_index("c").astype(jnp.int32).reshape(1)
    g_sib = _grad_w_in_sibling(loc["u"], loc["dp"], core, seq)
    g_chip = _grad_w_in_own(loc["u"], loc["dp"], core, g_sib, seq)
    grad_x, pre_small, r_in = _grad_x(loc["dp"], w_in_all, x2, pre_norm_w, loc["dout"], g_chip, seq)
    g_repl = _pack_rows([(pre_small, 0, 1, 1.0), (ls, 4, 1, 1.0), (ls, 7, 1, 1.0),
                         (loc["hgrn_small"], 0, 1, 1.0), (loc["hgrn_small"], 0, 1, -1.0),
                         (loc["hgrn_small"], 1, 1, 1.0), (loc["stats"], 0, 2, 1.0)], "pack_grads")
    (r_repl,) = _exchange_grads([], g_repl)

    repl_rows = [(RP_PRE, 1), (RP_CB, 1), (RP_LAM, 1), (RP_LB0, 2), (RP_GN, 1), (RP_POST, 1)]
    o_repl = _sum_adamw_pieces(
        r_repl, repl_rows,
        [pre_norm_w, conv_b, lru_lambda, hgrn_lb_logits, hgrn_gnorm_w, post_norm_w],
        [m_pre_norm_w, m_conv_b, m_lru_lambda, m_hgrn_lb_logits, m_hgrn_gnorm_w, m_post_norm_w],
        [v_pre_norm_w, v_conv_b, v_lru_lambda, v_hgrn_lb_logits, v_hgrn_gnorm_w, v_post_norm_w],
        "adamw_repl", loss_row=RP_LOSS)
    loss = o_repl[4][0, 0]

    small_rows = [(SM_WA, 256), (SM_WX, 256), (SM_CW, 4), (SM_BA, 1), (SM_BX, 1)]
    as_rows = lambda wa_, wx_, cw_, ba_, bx_: [wa_.reshape(256, 128), wx_.reshape(256, 128), cw_.reshape(4, 128),
                                               ba_.reshape(1, 128), bx_.reshape(1, 128)]
    o_small = _sum_adamw_pieces(
        r_small, small_rows, as_rows(lru_w_a, lru_w_x, conv_w, lru_b_a, lru_b_x),
        as_rows(m_lru_w_a, m_lru_w_x, m_conv_w, m_lru_b_a, m_lru_b_x),
        as_rows(v_lru_w_a, v_lru_w_x, v_conv_w, v_lru_b_a, v_lru_b_x), "adamw_small")

    o_in = _sum_adamw(r_in, w_in[0], m_w_in[0], v_w_in[0], 128, "adamw_w_in", through=grad_x)
    o_out = _sum_adamw(r_out, w_out[0], m_w_out[0], v_w_out[0], 64, "adamw_w_out")

    outs = [loss, o_in[4].reshape(x.shape)]
    for kind in range(4):
        pre, cb, lam, lb, gn, post = o_repl[kind]
        swa, swx, scw, sba, sbx = o_small[kind]
        outs += [pre, o_in[kind][None], scw.reshape(conv_w.shape), cb, swa.reshape(lru_w_a.shape),
                 sba.reshape(lru_b_a.shape), swx.reshape(lru_w_x.shape), sbx.reshape(lru_b_x.shape),
                 lam, lb, gn, o_out[kind][None], post]
    return tuple(outs)
```

```python
import functools

import jax
import jax.numpy as jnp
from jax import lax
from jax.experimental import pallas as pl
from jax.experimental.pallas import tpu as pltpu

F32 = jnp.float32
BF16 = jnp.bfloat16
SDS = jax.ShapeDtypeStruct

D_MODEL = 1024
D_IN = 6144
N_DEV = 8
N_CHIPS = 4
W_BLK = D_IN // N_DEV
D_MIX = 2048
LRU_BLOCKS = 4
LRU_BW = 256
LRU_C = 8.0
N_HEADS = 8
HEAD_D = 128
CHUNK = 128
SUB = 32
N_SUB = CHUNK // SUB
HGRN_FWD_STEP_CHUNKS = 4
HGRN_STEP_CHUNKS = 2
EXP_CLAMP = 80.0
EPS = 1e-6

ADAM_LR = 0.001
ADAM_B1 = 0.9
ADAM_B2 = 0.999
ADAM_EPS = 1e-08
ADAM_WD = 0.01
ADAM_STEP = 10

VMEM_LIMIT = 56 * 1024 * 1024

NN = (((1,), (0,)), ((), ()))
NT = (((1,), (1,)), ((), ()))
TN = (((0,), (0,)), ((), ()))

SM_LANES = 256
SM_WA = 0
SM_WX = 128
SM_CW = 256
SM_BA = 264
SM_BX = 272
SM_ROWS = 280

RP_PRE, RP_CB, RP_LAM, RP_LB0, RP_LB1, RP_GN, RP_POST, RP_LOSS = range(8)


def _dot(a, b, dims=NN):
    return lax.dot_general(a, b, dims, preferred_element_type=F32)


def _sigmoid(x):
    return 0.5 * jnp.tanh(0.5 * x) + 0.5


def _sigmoid_pos(x):
    return 1.0 / (1.0 + jnp.exp(-x))


def _cparams(sem, vmem=VMEM_LIMIT):
    return pltpu.CompilerParams(dimension_semantics=sem, vmem_limit_bytes=vmem)


def _iota(shape, axis):
    return lax.broadcasted_iota(jnp.int32, shape, axis)


def _softplus_neg(lam):
    z = -lam
    e = jnp.exp(-jnp.abs(z))
    u = 1.0 + e
    log1p_e = jnp.where(u == 1.0, e, jnp.log(u) * (e / (u - 1.0)))
    sp = jnp.maximum(z, 0.0) + log1p_e
    dsp = -jnp.where(z >= 0.0, 1.0 / u, e / u)
    return sp, dsp


def _neg_expm1(x):
    poly = x * (1.0 + x * (1.0 / 2 + x * (1.0 / 6 + x * (1.0 / 24 + x * (1.0 / 120)))))
    return jnp.where(x > -1.0 / 16, -poly, 1.0 - jnp.exp(x))


def _conv_taps(lx, prev8, cw_ref, cb_ref, tile):
    xc = cb_ref[...] + cw_ref[3:4, :] * lx
    for j in (1, 2, 3):
        xc = xc + cw_ref[3 - j:4 - j, :] * pltpu.roll(lx, j, 0)
    row8 = _iota((8, D_MODEL), 0)
    last8 = lx[tile - 8:tile, :]
    fix = jnp.zeros((8, D_MODEL), F32)
    for j in (1, 2, 3):
        wrong = pltpu.roll(last8, j, 0)
        right = pltpu.roll(prev8, j, 0)
        fix = fix + cw_ref[3 - j:4 - j, :] * jnp.where(row8 < j, right - wrong, 0.0)
    return xc, fix


def _lru_gates(xcs, wa, wx, ba, bx, sp):
    xb = xcs.astype(BF16)
    r = _sigmoid_pos(_dot(xb, wa) + ba)
    i = _sigmoid(_dot(xb, wx) + bx)
    la = (-LRU_C * sp) * r
    a = jnp.exp(la)
    one_minus_a2 = _neg_expm1(2.0 * la)
    return r, i, a, one_minus_a2


def _lru_forward(p, conv_w, conv_b, wa, wx, ba, bx, lam, seq):
    tile = min(512, seq // 2)
    nblk = tile // 8

    def body(lx_ref, gt_ref, cw_ref, cb_ref, wa_ref, wx_ref, ba_ref, bx_ref, lam_ref,
             h_ref, y_ref, ext, hcar, xc_scr, a_scr, u_scr):
        @pl.when(pl.program_id(0) == 0)
        def _():
            ext[0:8, :] = jnp.zeros((8, D_MODEL), F32)
            hcar[...] = jnp.zeros_like(hcar)

        lx = lx_ref[...]
        ext[8:8 + tile, :] = lx
        xc = cb_ref[...] + cw_ref[3:4, :] * lx
        for j in (1, 2, 3):
            xc = xc + cw_ref[3 - j:4 - j, :] * ext[8 - j:8 - j + tile, :]
        xc_scr[...] = xc
        ext[0:8, :] = lx_ref[tile - 8:tile, :]
        sp, _ = _softplus_neg(lam_ref[...])
        for n in range(LRU_BLOCKS):
            sl = slice(n * LRU_BW, (n + 1) * LRU_BW)
            xcs = xc_scr[:, sl]
            _, i, a, ne = _lru_gates(xcs, wa_ref[n], wx_ref[n], ba_ref[:, sl], bx_ref[:, sl], sp[:, sl])
            a_scr[:, sl] = a
            u_scr[:, sl] = jnp.sqrt(ne) * (i * xcs)

        row8 = _iota((8, D_MODEL), 0)

        def blk(j, hc):
            off = pl.multiple_of(j * 8, 8)
            a = a_scr[pl.ds(off, 8), :]
            u = u_scr[pl.ds(off, 8), :]
            for k in (1, 2, 4):
                m = row8 >= k
                u = jnp.where(m, u + a * pltpu.roll(u, k, 0), u)
                a = jnp.where(m, a * pltpu.roll(a, k, 0), a)
            h = u + a * hc
            h_ref[pl.ds(off, 8), :] = h
            return jnp.broadcast_to(h[7:8, :], (8, D_MODEL))

        hcar[...] = lax.fori_loop(0, nblk, blk, hcar[...])
        g = gt_ref[...]
        y_ref[...] = (h_ref[...] * (g * _sigmoid(g))).astype(BF16)

    full = lambda shape: pl.BlockSpec(shape, lambda t: (0,) * len(shape))
    return pl.pallas_call(
        body, name="lru_fwd", grid=(seq // tile,),
        in_specs=[pl.BlockSpec((tile, D_MODEL), lambda t: (t, 0)),
                  pl.BlockSpec((tile, D_MODEL), lambda t: (t, 1)),
                  full((4, D_MODEL)), full((1, D_MODEL)),
                  full((LRU_BLOCKS, LRU_BW, LRU_BW)), full((LRU_BLOCKS, LRU_BW, LRU_BW)),
                  full((1, D_MODEL)), full((1, D_MODEL)), full((1, D_MODEL))],
        out_specs=[pl.BlockSpec((tile, D_MODEL), lambda t: (t, 0)),
                   pl.BlockSpec((tile, D_MODEL), lambda t: (t, 0))],
        out_shape=[SDS((seq, D_MODEL), F32), SDS((seq, D_MODEL), BF16)],
        scratch_shapes=[pltpu.VMEM((tile + 8, D_MODEL), F32), pltpu.VMEM((8, D_MODEL), F32),
                        pltpu.VMEM((tile, D_MODEL), F32), pltpu.VMEM((tile, D_MODEL), F32),
                        pltpu.VMEM((tile, D_MODEL), F32)],
        compiler_params=_cparams(("arbitrary",)),
    )(p, p, conv_w, conv_b, wa, wx, ba, bx, lam)


def _lru_backward(p, h, dymix, conv_w, conv_b, wa, wx, ba, bx, lam, seq):
    tile = min(256, seq // 2)
    nt = seq // tile
    nblk = tile // 8
    t8 = tile // 8

    def body(lx_ref, lxh_ref, gt_ref, h_ref, hh_ref, dy_ref, cw_ref, cb_ref, wa_ref, wx_ref, ba_ref,
             bx_ref, lam_ref, dp_ref, gwa_ref, gwx_ref, gsm_ref,
             lamcar, anext, dxc8, xc_scr, r_scr, i_scr, a_scr, m_scr, rm_scr, c_scr, l_scr, dxc_scr):
        step = pl.program_id(0)
        first_tile = step == nt - 1

        @pl.when(step == 0)
        def _():
            lamcar[...] = jnp.zeros_like(lamcar)
            anext[...] = jnp.zeros_like(anext)
            dxc8[...] = jnp.zeros_like(dxc8)
            gwa_ref[...] = jnp.zeros_like(gwa_ref)
            gwx_ref[...] = jnp.zeros_like(gwx_ref)
            gsm_ref[...] = jnp.zeros_like(gsm_ref)

        keep = jnp.where(first_tile, 0.0, 1.0)
        lx = lx_ref[...]
        prev8 = lxh_ref[...] * keep
        xc, fix = _conv_taps(lx, prev8, cw_ref, cb_ref, tile)
        xc_scr[...] = xc
        xc_scr[0:8, :] = xc_scr[0:8, :] + fix
        sp, dsp = _softplus_neg(lam_ref[...])
        for n in range(LRU_BLOCKS):
            sl = slice(n * LRU_BW, (n + 1) * LRU_BW)
            r, i, a, ne = _lru_gates(xc_scr[:, sl], wa_ref[n], wx_ref[n], ba_ref[:, sl], bx_ref[:, sl],
                                     sp[:, sl])
            r_scr[:, sl] = r
            i_scr[:, sl] = i
            a_scr[:, sl] = a
            m_scr[:, sl] = jnp.sqrt(ne)
            rm_scr[:, sl] = lax.rsqrt(ne)

        g = gt_ref[...]
        sg = _sigmoid(g)
        dy = dy_ref[...]
        hv = h_ref[...]
        dp_ref[:, D_MODEL:2 * D_MODEL] = (dy * hv * (sg * (1.0 + g * (1.0 - sg)))).astype(BF16)

        rowt = _iota((tile, D_MODEL), 0)
        av = a_scr[...]
        l_scr[...] = dy * (g * sg)
        c_scr[...] = jnp.where(rowt == tile - 1, anext[...][0:1, :], pltpu.roll(av, tile - 1, 0))
        anext[...] = jnp.broadcast_to(av[0:1, :], (8, D_MODEL))
        row8 = _iota((8, D_MODEL), 0)

        def blk(jj, lc):
            off = pl.multiple_of((nblk - 1 - jj) * 8, 8)
            c = c_scr[pl.ds(off, 8), :]
            u = l_scr[pl.ds(off, 8), :]
            for k in (1, 2, 4):
                m = row8 < 8 - k
                u = jnp.where(m, u + c * pltpu.roll(u, 8 - k, 0), u)
                c = jnp.where(m, c * pltpu.roll(c, 8 - k, 0), c)
            lamv = u + c * lc
            l_scr[pl.ds(off, 8), :] = lamv
            return jnp.broadcast_to(lamv[0:1, :], (8, D_MODEL))

        lamcar[...] = lax.fori_loop(0, nblk, blk, lamcar[...])

        hprev = jnp.where(rowt == 0, hh_ref[...][7:8, :] * keep, pltpu.roll(hv, 1, 0))
        for n in range(LRU_BLOCKS):
            sl = slice(n * LRU_BW, (n + 1) * LRU_BW)
            lamv = l_scr[:, sl]
            xcs = xc_scr[:, sl]
            r = r_scr[:, sl]
            i = i_scr[:, sl]
            a = a_scr[:, sl]
            mult = m_scr[:, sl]
            d_la = lamv * hprev[:, sl] * a - (lamv * i * xcs) * (a * a * rm_scr[:, sl])
            d_pr = d_la * (-LRU_C * sp[:, sl]) * r * (1.0 - r)
            d_pi = (lamv * mult * xcs) * i * (1.0 - i)
            gsm_ref[7:8, sl] += jnp.sum(d_la * r, axis=0, keepdims=True) * (-LRU_C) * dsp[:, sl]
            gsm_ref[5:6, sl] += jnp.sum(d_pr, axis=0, keepdims=True)
            gsm_ref[6:7, sl] += jnp.sum(d_pi, axis=0, keepdims=True)
            xb = xcs.astype(BF16)
            prb = d_pr.astype(BF16)
            pib = d_pi.astype(BF16)
            gwa_ref[n] += _dot(xb, prb, TN)
            gwx_ref[n] += _dot(xb, pib, TN)
            dxc_scr[:, sl] = lamv * mult * i + _dot(prb, wa_ref[n], NT) + _dot(pib, wx_ref[n], NT)

        dxc = dxc_scr[...]
        gsm_ref[4:5, :] += jnp.sum(dxc, axis=0, keepdims=True)
        last8 = lx[tile - 8:tile, :]
        first8 = dxc[0:8, :]
        dlx = cw_ref[3:4, :] * dxc
        gsm_ref[3:4, :] += jnp.sum(dxc * lx, axis=0, keepdims=True)
        fix = jnp.zeros((8, D_MODEL), F32)
        for j in (1, 2, 3):
            w = cw_ref[3 - j:4 - j, :]
            dlx = dlx + w * pltpu.roll(dxc, tile - j, 0)
            fix = fix + w * jnp.where(row8 + j >= 8,
                                      pltpu.roll(dxc8[...], 8 - j, 0) - pltpu.roll(first8, 8 - j, 0), 0.0)
            halo = jnp.where(row8 < j, pltpu.roll(prev8, j, 0) - pltpu.roll(last8, j, 0), 0.0)
            gsm_ref[3 - j:4 - j, :] += (jnp.sum(dxc * pltpu.roll(lx, j, 0), axis=0, keepdims=True)
                                        + jnp.sum(first8 * halo, axis=0, keepdims=True))
        dxc8[...] = first8
        dp_ref[:, 0:D_MODEL] = dlx.astype(BF16)
        top = tile - 8
        dp_ref[top:tile, 0:D_MODEL] = (dlx[top:tile, :] + fix).astype(BF16)

    rev = lambda t: (nt - 1 - t, 0)
    halo_idx = lambda t: (jnp.maximum((nt - 1 - t) * t8 - 1, 0), 0)
    full = lambda shape: pl.BlockSpec(shape, lambda t: (0,) * len(shape))
    big = lambda: pltpu.VMEM((tile, D_MODEL), F32)
    return pl.pallas_call(
        body, name="lru_bwd", grid=(nt,),
        in_specs=[pl.BlockSpec((tile, D_MODEL), rev),
                  pl.BlockSpec((8, D_MODEL), halo_idx),
                  pl.BlockSpec((tile, D_MODEL), lambda t: (nt - 1 - t, 1)),
                  pl.BlockSpec((tile, D_MODEL), rev),
                  pl.BlockSpec((8, D_MODEL), halo_idx),
                  pl.BlockSpec((tile, D_MODEL), rev),
                  full((4, D_MODEL)), full((1, D_MODEL)),
                  full((LRU_BLOCKS, LRU_BW, LRU_BW)), full((LRU_BLOCKS, LRU_BW, LRU_BW)),
                  full((1, D_MODEL)), full((1, D_MODEL)), full((1, D_MODEL))],
        out_specs=[pl.BlockSpec((tile, 2 * D_MODEL), rev),
                   full((LRU_BLOCKS, LRU_BW, LRU_BW)), full((LRU_BLOCKS, LRU_BW, LRU_BW)),
                   full((8, D_MODEL))],
        out_shape=[SDS((seq, D_IN), BF16), SDS((LRU_BLOCKS, LRU_BW, LRU_BW), F32),
                   SDS((LRU_BLOCKS, LRU_BW, LRU_BW), F32), SDS((8, D_MODEL), F32)],
        scratch_shapes=[pltpu.VMEM((8, D_MODEL), F32), pltpu.VMEM((8, D_MODEL), F32),
                        pltpu.VMEM((8, D_MODEL), F32)] + [big() for _ in range(9)],
        compiler_params=_cparams(("arbitrary",)),
    )(p, p, p, h, h, dymix, conv_w, conv_b, wa, wx, ba, bx, lam)


def _tri_matmul(tri, g):
    hi = g.astype(BF16)
    lo = (g - hi.astype(F32)).astype(BF16)
    return _dot(tri, lo) + _dot(tri, hi)


def _hgrn_gate_terms(q, fr, lbl):
    lb = _sigmoid_pos(lbl[0:1, :] - lbl[1:2, :])
    half = 0.5 * (1.0 - lb)
    tf = jnp.tanh(0.5 * fr)
    f = (lb + half) + half * tf
    hq = 0.5 * q
    tq = jnp.tanh(hq)
    return lb, tf, f, tq, hq * tq + hq


def _hgrn_decay(bh):
    zero = jnp.zeros((1, bh.shape[1]), F32)
    rho = [zero] + [bh[s * SUB - 1:s * SUB, :] for s in range(1, N_SUB + 1)]
    start = _sub_rows(rho[0:N_SUB])
    end = _sub_rows(rho[1:N_SUB + 1])
    mid = 0.5 * (start + end)
    blast = rho[N_SUB]
    e_on = jnp.exp(bh - start)
    e_off = jnp.exp(end - bh)
    scales = [_sub_rows([jnp.exp(rho[i] - rho[j + 1]) if i > j else zero for i in range(N_SUB)])
              for j in range(N_SUB - 1)]
    return dict(eq0=jnp.exp(jnp.minimum(bh - mid, EXP_CLAMP)), ek0=jnp.exp(jnp.minimum(mid - bh, EXP_CLAMP)),
                e_on=e_on, e_off=e_off, scales=scales,
                eb=e_on * _sub_rows([jnp.exp(r) for r in rho[0:N_SUB]]),
                ekst=e_off * _sub_rows([jnp.exp(blast - r) for r in rho[1:N_SUB + 1]]),
                ebl=jnp.exp(blast))


def _sub_rows(vecs):
    return jnp.concatenate([jnp.broadcast_to(v, (SUB, v.shape[1])) for v in vecs], axis=0)


def _hgrn_operands(qs, k, dec, qt_scr, kt_scr):
    sub = jnp.right_shift(_iota(qs.shape, 0), 5)
    qon = qs * dec["e_on"]
    koff = k * dec["e_off"]
    qt_scr[0] = (qs * dec["eq0"]).astype(BF16)
    kt_scr[0] = (k * dec["ek0"]).astype(BF16)
    for j in range(N_SUB - 1):
        qt_scr[j + 1] = (qon * dec["scales"][j]).astype(BF16)
        kt_scr[j + 1] = jnp.where(sub == j, koff, 0.0).astype(BF16)
    return koff


def _hgrn_head_scores(qt_scr, kt_scr, sl, diag):
    a = jnp.where(diag, _dot(qt_scr[0, :, sl], kt_scr[0, :, sl], NT), 0.0)
    for j in range(1, N_SUB):
        a = a + _dot(qt_scr[j, :, sl], kt_scr[j, :, sl], NT)
    return a


def _hgrn_forward(p, lbl, gw, seq):
    nc = seq // CHUNK
    assert SUB == 32

    def body(q_ref, f_ref, v_ref, hg_ref, lbl_ref, gw_ref, y_ref, o_ref, st_ref,
             s_scr, qt_scr, kt_scr, qin_scr, kst_scr, vb_scr, a_scr):
        @pl.when(pl.program_id(0) == 0)
        def _():
            s_scr[...] = jnp.zeros_like(s_scr)

        r = _iota((CHUNK, CHUNK), 0)
        c = _iota((CHUNK, CHUNK), 1)
        tri = jnp.where(c <= r, 1.0, 0.0).astype(BF16)
        diag = (jnp.right_shift(r, 5) == jnp.right_shift(c, 5)) & (c <= r)
        heads = [slice(h * HEAD_D, (h + 1) * HEAD_D) for h in range(N_HEADS)]
        for cc in range(HGRN_FWD_STEP_CHUNKS):
            rows = slice(cc * CHUNK, (cc + 1) * CHUNK)
            q = q_ref[rows, :]
            _, _, f, _, qs = _hgrn_gate_terms(q, f_ref[rows, :], lbl_ref[...])
            k = 1.0 - f
            dec = _hgrn_decay(_tri_matmul(tri, jnp.log(f)))
            _hgrn_operands(qs, k, dec, qt_scr, kt_scr)
            qin_scr[...] = (qs * dec["eb"]).astype(BF16)
            kst_scr[...] = (k * dec["ekst"]).astype(BF16)
            vb_scr[...] = v_ref[rows, :].astype(BF16)
            ebl = dec["ebl"]
            hg = hg_ref[rows, :]
            gate = gw_ref[...] * (hg * _sigmoid(hg))
            stb = []
            for h, sl in enumerate(heads):
                st = s_scr[h]
                st_ref[cc, h] = st
                stb.append(st.astype(BF16))
                s_scr[h] = st * ebl[:, sl] + _dot(vb_scr[:, sl], kst_scr[:, sl], TN)
            for h, sl in enumerate(heads):
                a_scr[h] = _hgrn_head_scores(qt_scr, kt_scr, sl, diag).astype(BF16)
            for h, sl in enumerate(heads):
                o = _dot(a_scr[h], vb_scr[:, sl]) + _dot(qin_scr[:, sl], stb[h], NT)
                o_ref[rows, sl] = o
                rs = lax.rsqrt(jnp.mean(o * o, axis=-1, keepdims=True) + EPS)
                y_ref[rows, sl] = ((o * rs) * gate[:, sl]).astype(BF16)

    col = lambda j: pl.BlockSpec((HGRN_FWD_STEP_CHUNKS * CHUNK, D_MODEL), lambda c: (c, j))
    par = lambda rows: pl.BlockSpec((rows, D_MODEL), lambda c: (0, 0))
    return pl.pallas_call(
        body, name="hgrn_fwd", grid=(nc // HGRN_FWD_STEP_CHUNKS,),
        in_specs=[col(2), col(3), col(4), col(5), par(2), par(1)],
        out_specs=[col(0), col(0),
                   pl.BlockSpec((HGRN_FWD_STEP_CHUNKS, N_HEADS, HEAD_D, HEAD_D), lambda c: (c, 0, 0, 0))],
        out_shape=[SDS((seq, D_MODEL), BF16), SDS((seq, D_MODEL), F32),
                   SDS((nc, N_HEADS, HEAD_D, HEAD_D), F32)],
        scratch_shapes=[pltpu.VMEM((N_HEADS, HEAD_D, HEAD_D), F32),
                        pltpu.VMEM((N_SUB, CHUNK, D_MODEL), BF16), pltpu.VMEM((N_SUB, CHUNK, D_MODEL), BF16)]
                       + [pltpu.VMEM((CHUNK, D_MODEL), BF16)] * 3 + [pltpu.VMEM((N_HEADS, CHUNK, CHUNK), BF16)],
        compiler_params=_cparams(("arbitrary",)),
    )(p, p, p, p, lbl, gw)


def _hgrn_backward(p, o, states, dymix, lbl, gw, dp_full, g_w_out, g_small, seq):
    step_rows = HGRN_STEP_CHUNKS * CHUNK
    ns = seq // step_rows

    def body(q_ref, f_ref, v_ref, hg_ref, o_ref, st_ref, dy_ref, lbl_ref, gw_ref, dpin_ref, go_ref, gs_ref,
             dpo_ref, gsm_ref, ro_ref, rs_ref, ds_scr, dp_buf, dp_sems, *rest):
        del dpin_ref
        scratch, sems = rest[:14], rest[14:]
        step = pl.program_id(0)
        slot = step % 2
        exs = [_SlotExchange(go_ref, ro_ref, *sems[0:3], blocked=True),
               _SlotExchange(gs_ref, rs_ref, *sems[3:6], blocked=True)]

        @pl.when(step == 0)
        def _():
            for ex in exs:
                ex.start()

        def out_copy(s, blk):
            rows = pl.ds(pl.multiple_of(blk * step_rows, step_rows), step_rows)
            return pltpu.make_async_copy(dp_buf.at[s], dpo_ref.at[rows, pl.ds(2 * D_MODEL, 4 * D_MODEL)],
                                         dp_sems.at[s])

        @pl.when(step == 0)
        def _():
            ds_scr[...] = jnp.zeros_like(ds_scr)
            gsm_ref[...] = jnp.zeros_like(gsm_ref)

        @pl.when(step >= 2)
        def _():
            out_copy(slot, ns + 1 - step).wait()

        for cc in reversed(range(HGRN_STEP_CHUNKS)):
            chunk(cc, q_ref, f_ref, v_ref, hg_ref, o_ref, st_ref, dy_ref, lbl_ref, gw_ref, gsm_ref, ds_scr,
                  dp_buf.at[slot], *scratch)

        out_copy(slot, ns - 1 - step).start()

        @pl.when(step == ns - 1)
        def _():
            out_copy(1 - slot, 1).wait()
            out_copy(slot, 0).wait()
            for ex in exs:
                ex.wait()

    def chunk(cc, q_ref, f_ref, v_ref, hg_ref, o_ref, st_ref, dy_ref, lbl_ref, gw_ref, gsm_ref, ds_scr, dp_ref,
              qt_scr, kt_scr, qin_scr, kst_scr, vb_scr, dob_scr, g_scr, h_scr, dqi_scr, dks_scr, sd_scr,
              a_scr, da_scr, da0_scr):
        rows = slice(cc * CHUNK, (cc + 1) * CHUNK)
        r = _iota((CHUNK, CHUNK), 0)
        c = _iota((CHUNK, CHUNK), 1)
        tri = jnp.where(c <= r, 1.0, 0.0).astype(BF16)
        triu = jnp.where(c >= r, 1.0, 0.0).astype(BF16)
        diag = (jnp.right_shift(r, 5) == jnp.right_shift(c, 5)) & (c <= r)
        row = _iota((CHUNK, D_MODEL), 0)
        sub = jnp.right_shift(row, 5)

        q = q_ref[rows, :]
        lb, tf, f, tq, qs = _hgrn_gate_terms(q, f_ref[rows, :], lbl_ref[...])
        sig = 0.5 * tf + 0.5
        sq = 0.5 * tq + 0.5
        k = 1.0 - f
        dec = _hgrn_decay(_tri_matmul(tri, jnp.log(f)))
        eb, ekst, ebl = dec["eb"], dec["ekst"], dec["ebl"]
        koff = _hgrn_operands(qs, k, dec, qt_scr, kt_scr)
        qin_scr[...] = (qs * eb).astype(BF16)
        kst_scr[...] = (k * ekst).astype(BF16)
        vb_scr[...] = v_ref[rows, :].astype(BF16)
        hg = hg_ref[rows, :]
        sh = _sigmoid(hg)
        dy = dy_ref[rows, :]
        gwv = gw_ref[...]
        d_onw = dy * (hg * sh)
        d_on = d_onw * gwv
        d_gate = dy * gwv * (sh * (1.0 + hg * (1.0 - sh)))

        heads = [slice(h * HEAD_D, (h + 1) * HEAD_D) for h in range(N_HEADS)]
        for h, sl in enumerate(heads):
            o = o_ref[rows, sl]
            rs = lax.rsqrt(jnp.mean(o * o, axis=-1, keepdims=True) + EPS)
            on = o * rs
            dp_ref[rows, 3 * D_MODEL + h * HEAD_D:3 * D_MODEL + (h + 1) * HEAD_D] = (d_gate[:, sl] * on).astype(BF16)
            gsm_ref[1:2, sl] += jnp.sum(d_onw[:, sl] * on, axis=0, keepdims=True)
            d_onh = d_on[:, sl]
            dob_scr[:, sl] = (rs * (d_onh - on * jnp.mean(d_onh * on, axis=-1, keepdims=True))).astype(BF16)
        for h, sl in enumerate(heads):
            a_scr[h] = _hgrn_head_scores(qt_scr, kt_scr, sl, diag).astype(BF16)
            da = _dot(dob_scr[:, sl], vb_scr[:, sl], NT)
            da_scr[h] = da.astype(BF16)
            da0_scr[h] = jnp.where(diag, da, 0.0).astype(BF16)
        for h, sl in enumerate(heads):
            st = st_ref[cc, h]
            dst = ds_scr[h]
            dstb = dst.astype(BF16)
            dp_ref[rows, 2 * D_MODEL + h * HEAD_D:2 * D_MODEL + (h + 1) * HEAD_D] = (
                _dot(a_scr[h], dob_scr[:, sl], TN) + _dot(kst_scr[:, sl], dstb, NT)).astype(BF16)
            dqi_scr[:, sl] = _dot(dob_scr[:, sl], st.astype(BF16))
            dks_scr[:, sl] = _dot(vb_scr[:, sl], dstb)
            sd_scr[0:1, sl] = jnp.sum(st * dst, axis=0, keepdims=True)
            ds_scr[h] = dst * ebl[:, sl] + _dot(dob_scr[:, sl], qin_scr[:, sl], TN)
        for h, sl in enumerate(heads):
            g_scr[0, :, sl] = _dot(da0_scr[h], kt_scr[0, :, sl])
            h_scr[0, :, sl] = _dot(da0_scr[h], qt_scr[0, :, sl], TN)
            for j in range(1, N_SUB):
                g_scr[j, :, sl] = _dot(da_scr[h], kt_scr[j, :, sl])
                h_scr[j, :, sl] = _dot(da_scr[h], qt_scr[j, :, sl], TN)

        g0 = g_scr[0]
        h0 = h_scr[0]
        dq_inter = eb * dqi_scr[...]
        d_kst = ekst * dks_scr[...]
        db = qs * dq_inter - k * d_kst + qt_scr[0].astype(F32) * g0 - kt_scr[0].astype(F32) * h0
        gq = jnp.zeros((CHUNK, D_MODEL), F32)
        hsel = jnp.zeros((CHUNK, D_MODEL), F32)
        for j in range(N_SUB - 1):
            gj = g_scr[j + 1]
            gq = gq + dec["scales"][j] * gj
            db = db + qt_scr[j + 1].astype(F32) * gj
            hsel = jnp.where(sub == j, h_scr[j + 1], hsel)
        db = db - koff.astype(BF16).astype(F32) * hsel
        d_q = dec["eq0"] * g0 + dec["e_on"] * gq + dq_inter
        d_k = dec["ek0"] * h0 + dec["e_off"] * hsel + d_kst
        db_last = jnp.sum(k * d_kst, axis=0, keepdims=True) + ebl * sd_scr[0:1, :]
        db = db + jnp.where(row == CHUNK - 1, db_last, 0.0)
        dg = _tri_matmul(triu, db)
        d_f = dg / f - d_k
        dp_ref[rows, D_MODEL:2 * D_MODEL] = (d_f * (1.0 - lb) * sig * (1.0 - sig)).astype(BF16)
        gsm_ref[0:1, :] += jnp.sum(d_f * (1.0 - sig), axis=0, keepdims=True) * (lb * (1.0 - lb))
        dp_ref[rows, 0:D_MODEL] = (d_q * (sq * (1.0 + q * (1.0 - sq)))).astype(BF16)

    rc = lambda c: ns - 1 - c
    col = lambda j: pl.BlockSpec((step_rows, D_MODEL), lambda c: (rc(c), j))
    par = lambda rows: pl.BlockSpec((rows, D_MODEL), lambda c: (0, 0))
    return pl.pallas_call(
        body, name="hgrn_bwd", grid=(ns,),
        in_specs=[col(2), col(3), col(4), col(5), col(0),
                  pl.BlockSpec((HGRN_STEP_CHUNKS, N_HEADS, HEAD_D, HEAD_D), lambda c: (rc(c), 0, 0, 0)),
                  col(1), par(2), par(1), ANY, ANY, ANY],
        out_specs=[ANY, par(8), ANY, ANY],
        out_shape=[SDS((seq, D_IN), BF16), SDS((8, D_MODEL), F32), SDS(g_w_out.shape, F32),
                   SDS(g_small.shape, F32)],
        input_output_aliases={9: 0},
        scratch_shapes=[pltpu.VMEM((N_HEADS, HEAD_D, HEAD_D), F32),
                        pltpu.VMEM((2, step_rows, 4 * D_MODEL), BF16), pltpu.SemaphoreType.DMA((2,)),
                        pltpu.VMEM((N_SUB, CHUNK, D_MODEL), BF16), pltpu.VMEM((N_SUB, CHUNK, D_MODEL), BF16)]
                       + [pltpu.VMEM((CHUNK, D_MODEL), BF16)] * 4
                       + [pltpu.VMEM((N_SUB, CHUNK, D_MODEL), F32)] * 2 + [pltpu.VMEM((CHUNK, D_MODEL), F32)] * 2
                       + [pltpu.VMEM((8, D_MODEL), F32)] + [pltpu.VMEM((N_HEADS, CHUNK, CHUNK), BF16)] * 3
                       + EXCHANGE_SEMS * 2,
        compiler_params=_cparams(("arbitrary",)),
    )(p, p, p, p, o, states, dymix, lbl, gw, dp_full, g_w_out, g_small)


def _out_proj(yl, yh, wo, x, tgt, post_w, seq):
    tm = 512

    def body(yl_ref, yh_ref, wo_ref, x_ref, tg_ref, pw_ref, dymix_ref, dout_ref, gwo_ref, st_ref):
        @pl.when(pl.program_id(0) == 0)
        def _():
            gwo_ref[...] = jnp.zeros_like(gwo_ref)
            st_ref[...] = jnp.zeros_like(st_ref)

        ylv = yl_ref[...]
        yhv = yh_ref[...]
        y = _dot(ylv, wo_ref[0:D_MODEL, :]) + _dot(yhv, wo_ref[D_MODEL:D_MIX, :])
        r2 = lax.rsqrt(jnp.mean(y * y, axis=-1, keepdims=True) + EPS)
        yn = y * r2
        pw = pw_ref[...]
        e = (x_ref[...] + yn * pw) - tg_ref[...]
        st_ref[1:2, :] += jnp.sum(e * e, axis=0, keepdims=True) * (0.5 / D_MODEL)
        dout = e * (1.0 / D_MODEL)
        dout_ref[...] = dout
        st_ref[0:1, :] += jnp.sum(dout * yn, axis=0, keepdims=True)
        dyn = dout * pw
        dy = r2 * (dyn - yn * jnp.mean(dyn * yn, axis=-1, keepdims=True))
        dyb = dy.astype(BF16)
        dymix_ref[...] = _dot(dyb, wo_ref[...], NT)
        gwo_ref[0:D_MODEL, :] += _dot(ylv, dyb, TN)
        gwo_ref[D_MODEL:D_MIX, :] += _dot(yhv, dyb, TN)

    row = lambda w: pl.BlockSpec((tm, w), lambda m: (m, 0))
    full = lambda shape: pl.BlockSpec(shape, lambda m: (0,) * len(shape))
    once = lambda shape: pl.BlockSpec(shape, lambda m: (0,) * len(shape), pipeline_mode=pl.Buffered(1))
    return pl.pallas_call(
        body, name="out_proj", grid=(seq // tm,),
        in_specs=[row(D_MODEL), row(D_MODEL), once((D_MIX, D_MODEL)), row(D_MODEL), row(D_MODEL),
                  full((1, D_MODEL))],
        out_specs=[row(D_MIX), row(D_MODEL), once((D_MIX, D_MODEL)), full((8, D_MODEL))],
        out_shape=[SDS((seq, D_MIX), F32), SDS((seq, D_MODEL), F32), SDS((D_MIX, D_MODEL), F32),
                   SDS((8, D_MODEL), F32)],
        compiler_params=_cparams(("arbitrary",)),
    )(yl, yh, wo, x, tgt, post_w)


MESH = pl.DeviceIdType.MESH
ANY = pl.BlockSpec(memory_space=pl.ANY)
EXCHANGE_SEMS = [pltpu.SemaphoreType.DMA((N_DEV - 1,)), pltpu.SemaphoreType.DMA((N_DEV - 1,)),
                 pltpu.SemaphoreType.DMA(())]


def _mesh_pos():
    return lax.axis_index("x"), lax.axis_index("y"), lax.axis_index("c")


class _SlotExchange:
    def __init__(self, src_ref, dst_ref, send_sems, recv_sems, local_sem, blocked):
        x, y, c = _mesh_pos()
        me = 4 * x + 2 * y + c
        src = (lambda dest: src_ref.at[dest]) if blocked else (lambda dest: src_ref)
        self.local = pltpu.make_async_copy(src(me), dst_ref.at[me], local_sem)
        self.sends, self.recvs = [], []
        for k in range(1, N_DEV):
            px = 1 - x if (k >> 2) & 1 else x
            py = 1 - y if (k >> 1) & 1 else y
            pc = 1 - c if k & 1 else c
            peer = 4 * px + 2 * py + pc
            sems = dict(send_sem=send_sems.at[k - 1], recv_sem=recv_sems.at[k - 1],
                        device_id=(px, py, pc), device_id_type=MESH)
            self.sends.append(pltpu.make_async_remote_copy(src_ref=src(peer), dst_ref=dst_ref.at[me], **sems))
            self.recvs.append(pltpu.make_async_remote_copy(src_ref=dst_ref.at[peer], dst_ref=dst_ref.at[peer], **sems))

    def start(self):
        self.local.start()
        for cp in self.sends:
            cp.start()

    def wait(self):
        for cp in self.recvs:
            cp.wait_recv()
        for cp in self.sends:
            cp.wait_send()
        self.local.wait()


class _ChipExchange:
    def __init__(self, src_ref, dst_ref, send_sems, recv_sems, local_sem):
        x, y, c = _mesh_pos()
        chip = 2 * x + y
        self.local = pltpu.make_async_copy(src_ref.at[chip], dst_ref.at[chip], local_sem)
        self.sends, self.recvs = [], []
        for k in range(1, N_CHIPS):
            px = 1 - x if (k >> 1) & 1 else x
            py = 1 - y if k & 1 else y
            peer = 2 * px + py
            sems = dict(send_sem=send_sems.at[k - 1], recv_sem=recv_sems.at[k - 1],
                        device_id=(px, py, c), device_id_type=MESH)
            self.sends.append(pltpu.make_async_remote_copy(src_ref=src_ref.at[peer], dst_ref=dst_ref.at[chip], **sems))
            self.recvs.append(pltpu.make_async_remote_copy(src_ref=dst_ref.at[peer], dst_ref=dst_ref.at[peer], **sems))

    def start(self):
        self.local.start()
        for cp in self.sends:
            cp.start()

    def wait(self):
        for cp in self.recvs:
            cp.wait_recv()
        for cp in self.sends:
            cp.wait_send()
        self.local.wait()


GRAD_W_IN_TK = 2048


def _grad_w_in_sibling(u, dp, core, seq):
    tk = min(GRAD_W_IN_TK, seq)
    nk = seq // tk

    def body(core_ref, u_ref, dp_ref, g_ref):
        del core_ref

        @pl.when(pl.program_id(1) == 0)
        def _():
            g_ref[...] = jnp.zeros_like(g_ref)

        g_ref[0] += _dot(u_ref[...], dp_ref[...], TN)

    return pl.pallas_call(
        body, name="grad_w_in_sibling",
        grid_spec=pltpu.PrefetchScalarGridSpec(
            num_scalar_prefetch=1, grid=(N_CHIPS, nk),
            in_specs=[pl.BlockSpec((tk, D_MODEL), lambda n, k, c: (k, 0)),
                      pl.BlockSpec((tk, W_BLK), lambda n, k, c: (k, 2 * n + 1 - c[0]))],
            out_specs=pl.BlockSpec((1, D_MODEL, W_BLK), lambda n, k, c: (n, 0, 0))),
        out_shape=SDS((N_CHIPS, D_MODEL, W_BLK), F32),
        compiler_params=_cparams(("parallel", "arbitrary")),
    )(core, u, dp)


def _grad_w_in_own(u, dp, core, g_sib, seq):
    tk = min(GRAD_W_IN_TK, seq)
    nk = seq // tk

    def body(core_ref, u_ref, dp_ref, gsib_ref, g_ref, land, send_sem, recv_sem):
        del core_ref
        n = pl.program_id(0)
        k = pl.program_id(1)
        x, y, c = _mesh_pos()
        swap = pltpu.make_async_remote_copy(src_ref=gsib_ref, dst_ref=land, send_sem=send_sem, recv_sem=recv_sem,
                                            device_id=(x, y, 1 - c), device_id_type=MESH)

        @pl.when((n == 0) & (k == 0))
        def _():
            swap.start()

        @pl.when(k == 0)
        def _():
            g_ref[...] = jnp.zeros_like(g_ref)

        g_ref[0] += _dot(u_ref[...], dp_ref[...], TN)

        @pl.when((n == 0) & (k == nk - 1))
        def _():
            swap.wait_recv()

        @pl.when(k == nk - 1)
        def _():
            g_ref[0] += land[n]

        @pl.when((n == N_CHIPS - 1) & (k == nk - 1))
        def _():
            swap.wait_send()

    return pl.pallas_call(
        body, name="grad_w_in_own",
        grid_spec=pltpu.PrefetchScalarGridSpec(
            num_scalar_prefetch=1, grid=(N_CHIPS, nk),
            in_specs=[pl.BlockSpec((tk, D_MODEL), lambda n, k, c: (k, 0)),
                      pl.BlockSpec((tk, W_BLK), lambda n, k, c: (k, 2 * n + c[0])), ANY],
            out_specs=pl.BlockSpec((1, D_MODEL, W_BLK), lambda n, k, c: (n, 0, 0)),
            scratch_shapes=[pltpu.VMEM((N_CHIPS, D_MODEL, W_BLK), F32), pltpu.SemaphoreType.DMA(()),
                            pltpu.SemaphoreType.DMA(())]),
        out_shape=SDS((N_CHIPS, D_MODEL, W_BLK), F32),
        compiler_params=_cparams(("arbitrary", "arbitrary")),
    )(core, u, dp, g_sib)


def _grad_x(dp, w_all, x, pre_w, dout, g_chip, seq):
    tm = 512
    nm = seq // tm

    def body(dp_ref, w_ref, x_ref, pw_ref, do_ref, gsrc_ref, gx_ref, gpw_ref, recv_ref,
             send_sems, recv_sems, local_sem):
        m = pl.program_id(0)
        ex = _ChipExchange(gsrc_ref, recv_ref, send_sems, recv_sems, local_sem)

        @pl.when(m == 0)
        def _():
            ex.start()
            gpw_ref[...] = jnp.zeros_like(gpw_ref)

        du = _dot(dp_ref[:, 0:W_BLK], w_ref[0], NT)
        for j in range(1, N_DEV):
            du = du + _dot(dp_ref[:, j * W_BLK:(j + 1) * W_BLK], w_ref[j], NT)
        xv = x_ref[...]
        r1 = lax.rsqrt(jnp.mean(xv * xv, axis=-1, keepdims=True) + EPS)
        xn = xv * r1
        gpw_ref[0:1, :] += jnp.sum(du * xn, axis=0, keepdims=True)
        dxn = du * pw_ref[...]
        gx_ref[...] = r1 * (dxn - xn * jnp.mean(dxn * xn, axis=-1, keepdims=True)) + do_ref[...]

        @pl.when(m == nm - 1)
        def _():
            ex.wait()

    row = lambda w: pl.BlockSpec((tm, w), lambda m: (m, 0))
    return pl.pallas_call(
        body, name="grad_x", grid=(nm,),
        in_specs=[row(D_IN), pl.BlockSpec((N_DEV, D_MODEL, W_BLK), lambda m: (0, 0, 0), pipeline_mode=pl.Buffered(1)),
                  row(D_MODEL),
                  pl.BlockSpec((1, D_MODEL), lambda m: (0, 0)), row(D_MODEL), ANY],
        out_specs=[row(D_MODEL), pl.BlockSpec((8, D_MODEL), lambda m: (0, 0)), ANY],
        out_shape=[SDS((seq, D_MODEL), F32), SDS((8, D_MODEL), F32), SDS(g_chip.shape, F32)],
        scratch_shapes=[pltpu.SemaphoreType.DMA((N_CHIPS - 1,)), pltpu.SemaphoreType.DMA((N_CHIPS - 1,)),
                        pltpu.SemaphoreType.DMA(())],
        compiler_params=_cparams(("arbitrary",)),
    )(dp, w_all, x, pre_w, dout, g_chip)


def _local_step(x, tgt, p, u, conv_w, conv_b, wa, wx, ba, bx, lam, lbl, gnorm_w, w_out, post_w):
    seq = x.shape[0]
    h, y_lru = _lru_forward(p, conv_w, conv_b, wa, wx, ba, bx, lam, seq)
    y_hgrn, o, states = _hgrn_forward(p, lbl, gnorm_w, seq)
    dymix, dout, g_w_out, stats = _out_proj(y_lru, y_hgrn, w_out, x, tgt, post_w, seq)
    dp_lru, g_wa, g_wx, ls = _lru_backward(p, h, dymix, conv_w, conv_b, wa, wx, ba, bx, lam, seq)
    g_small = _pack_small(_shard_rows(g_wa, LRU_BLOCKS), _shard_rows(g_wx, LRU_BLOCKS),
                          _shard_rows(ls[0:4].reshape(4, D_MODEL, 1), 4).reshape(N_DEV, 4, 128),
                          _shard_rows(ls[5].reshape(4, LRU_BW, 1), 4).reshape(N_DEV, 4, 32),
                          _shard_rows(ls[6].reshape(4, LRU_BW, 1), 4).reshape(N_DEV, 4, 32))
    dp, hgrn_small, r_out, r_small = _hgrn_backward(
        p, o, states, dymix, lbl, gnorm_w, dp_lru, g_w_out.reshape(N_DEV, D_MIX // N_DEV, D_MODEL), g_small, seq)
    return dict(u=u, dp=dp, dout=dout, r_out=r_out, r_small=r_small,
                lru_small=ls, hgrn_small=hgrn_small, stats=stats)


class _TwoLevelGather:
    def __init__(self, ins, outs, send_sems, recv_sems, local_sems):
        self.ins, self.outs = ins, outs
        self.send_sems, self.recv_sems, self.local_sems = send_sems, recv_sems, local_sems
        x, y, c = _mesh_pos()
        self.c = c
        self.me, self.sibling = (x, y, c), (x, y, 1 - c)
        self.chips = [(1 - x, y), (x, 1 - y), (1 - x, 1 - y)]
        n = len(ins)
        self.mine = [pltpu.make_async_copy(ins[a], self._slot(a, self.me), local_sems.at[a]) for a in range(n)]
        self.first = []
        for a in range(n):
            self.first.append(self._copy(a, 0, self.me, self.sibling, src=ins[a]))
            self.first += [self._copy(a, 1 + j, self.me, (*chip, c), src=ins[a])
                           for j, chip in enumerate(self.chips)]
        self.passed = [self._copy(a, 4 + j, (*chip, c), self.sibling)
                       for j, chip in enumerate(self.chips) for a in range(n)]

    def _slot(self, a, pos):
        return self.outs[a].at[4 * pos[0] + 2 * pos[1] + pos[2]]

    def _copy(self, a, k, block, to, src=None):
        dst = self._slot(a, block)
        return pltpu.make_async_remote_copy(
            src_ref=dst if src is None else src, dst_ref=dst,
            send_sem=self.send_sems.at[a, k], recv_sem=self.recv_sems.at[a, k],
            device_id=to, device_id_type=MESH)

    def start(self):
        for cp in self.mine + self.first:
            cp.start()

    def forward(self):
        n = len(self.ins)
        for j, chip in enumerate(self.chips):
            for a in range(n):
                self._copy(a, 1 + j, (*chip, self.c), self.me).wait_recv()
                self.passed[j * n + a].start()

    def finish(self):
        for a in range(len(self.ins)):
            self._copy(a, 0, self.sibling, self.me).wait_recv()
            for j, chip in enumerate(self.chips):
                self._copy(a, 4 + j, (*chip, 1 - self.c), self.me).wait_recv()
        for cp in self.first + self.passed:
            cp.wait_send()
        for cp in self.mine:
            cp.wait()


W_IN_DIRECT = (1, 2, 4, 6)
W_IN_PASSED = (2, 4, 6)

def _in_proj_gather(x, pre_w, w_in_blk, w_out_blk, small_blk, me, seq):
    tm = min(1024, seq)
    nm = seq // tm
    last = N_DEV - 1

    def body(me_ref, x_ref, pw_ref, wblk_ref, woblk_ref, smblk_ref,
             p_ref, u_ref, wall_ref, woall_ref, small_ref,
             u_all, w_vmem, own_sem, d_send, d_recv, f_send, f_recv, wb_sems, g_send, g_recv, g_local):
        i = pl.program_id(0)
        m = pl.program_id(1)
        idx = me_ref[0]
        x_, y_, c_ = _mesh_pos()
        aux = _TwoLevelGather([woblk_ref, smblk_ref], [woall_ref, small_ref], g_send, g_recv, g_local)

        def peer(k):
            return (1 - x_ if (k >> 2) & 1 else x_, 1 - y_ if (k >> 1) & 1 else y_, 1 - c_ if k & 1 else c_)

        def direct(k):
            f = W_IN_DIRECT.index(k)
            return (pltpu.make_async_remote_copy(src_ref=wblk_ref, dst_ref=w_vmem.at[idx], send_sem=d_send.at[f],
                                                 recv_sem=d_recv.at[f], device_id=peer(k), device_id_type=MESH),
                    pltpu.make_async_remote_copy(src_ref=w_vmem.at[idx ^ k], dst_ref=w_vmem.at[idx ^ k],
                                                 send_sem=d_send.at[f], recv_sem=d_recv.at[f], device_id=peer(k),
                                                 device_id_type=MESH))

        def passed(k):
            f = W_IN_PASSED.index(k)
            return (pltpu.make_async_remote_copy(src_ref=w_vmem.at[idx ^ k], dst_ref=w_vmem.at[idx ^ k],
                                                 send_sem=f_send.at[f], recv_sem=f_recv.at[f], device_id=peer(1),
                                                 device_id_type=MESH),
                    pltpu.make_async_remote_copy(src_ref=w_vmem.at[idx ^ (k + 1)], dst_ref=w_vmem.at[idx ^ (k + 1)],
                                                 send_sem=f_send.at[f], recv_sem=f_recv.at[f], device_id=peer(1),
                                                 device_id_type=MESH))

        def write_back(k):
            return pltpu.make_async_copy(w_vmem.at[idx ^ k], wall_ref.at[idx ^ k], wb_sems.at[k])

        own = pltpu.make_async_copy(wblk_ref, w_vmem.at[idx], own_sem)

        @pl.when((i == 0) & (m == 0))
        def _():
            own.start()
            for k in (1, 2, 4):
                direct(k)[0].start()
            own.wait()
            write_back(0).start()

        for k in range(1, N_DEV):
            @pl.when((i == k) & (m == 0))
            def _(k=k):
                if k in W_IN_DIRECT:
                    direct(k)[1].wait_recv()
                    if k in W_IN_PASSED:
                        passed(k)[0].start()
                else:
                    passed(k - 1)[1].wait_recv()
                write_back(k).start()
                if k == 2:
                    direct(6)[0].start()
                if k == 4:
                    aux.start()
                if k == N_DEV - 1:
                    aux.forward()

        rows = pl.ds(pl.multiple_of(m * tm, tm), tm)

        @pl.when(i == 0)
        def _():
            xv = x_ref[...]
            r = lax.rsqrt(jnp.mean(xv * xv, axis=-1, keepdims=True) + EPS)
            ub = (xv * r * pw_ref[...]).astype(BF16)
            u_all[rows, :] = ub
            u_ref[...] = ub

        p_ref[...] = _dot(u_all[rows, :], w_vmem[idx ^ i])

        @pl.when((i == last) & (m == nm - 1))
        def _():
            for k in W_IN_DIRECT:
                direct(k)[0].wait_send()
            for k in W_IN_PASSED:
                passed(k)[0].wait_send()
            for k in range(N_DEV):
                write_back(k).wait()
            aux.finish()

    first_pass = lambda i, m: jnp.where(i == 0, m, nm - 1)
    return pl.pallas_call(
        body, name="in_proj_gather",
        grid_spec=pltpu.PrefetchScalarGridSpec(
            num_scalar_prefetch=1, grid=(N_DEV, nm),
            in_specs=[pl.BlockSpec((tm, D_MODEL), lambda i, m, me: (first_pass(i, m), 0)),
                      pl.BlockSpec((1, D_MODEL), lambda i, m, me: (0, 0)), ANY, ANY, ANY],
            out_specs=[pl.BlockSpec((tm, W_BLK), lambda i, m, me: (m, me[0] ^ i)),
                       pl.BlockSpec((tm, D_MODEL), lambda i, m, me: (first_pass(i, m), 0)), ANY, ANY, ANY],
            scratch_shapes=[pltpu.VMEM((seq, D_MODEL), BF16), pltpu.VMEM((N_DEV, D_MODEL, W_BLK), BF16),
                            pltpu.SemaphoreType.DMA(()),
                            pltpu.SemaphoreType.DMA((len(W_IN_DIRECT),)), pltpu.SemaphoreType.DMA((len(W_IN_DIRECT),)),
                            pltpu.SemaphoreType.DMA((len(W_IN_PASSED),)), pltpu.SemaphoreType.DMA((len(W_IN_PASSED),)),
                            pltpu.SemaphoreType.DMA((N_DEV,)),
                            pltpu.SemaphoreType.DMA((2, 7)), pltpu.SemaphoreType.DMA((2, 7)),
                            pltpu.SemaphoreType.DMA((2,))]),
        out_shape=[SDS((seq, D_IN), F32), SDS((seq, D_MODEL), BF16), SDS((N_DEV, D_MODEL, W_BLK), BF16),
                   SDS((N_DEV,) + w_out_blk.shape, w_out_blk.dtype), SDS((N_DEV,) + small_blk.shape, small_blk.dtype)],
        compiler_params=_cparams(("arbitrary", "arbitrary")),
    )(me, x, pre_w, w_in_blk, w_out_blk, small_blk)


def _exchange_grads(blocks, repl):
    nb = len(blocks)
    n = nb + 1

    def body(*refs):
        ins, outs, sems = refs[:n], refs[n:2 * n], refs[2 * n:]
        exs = [_SlotExchange(ins[a], outs[a], *sems[3 * a:3 * a + 3], blocked=a < nb) for a in range(n)]
        for ex in exs:
            ex.start()
        for ex in exs:
            ex.wait()

    arrs = list(blocks) + [repl]
    shapes = [SDS(b.shape, b.dtype) for b in blocks] + [SDS((N_DEV,) + repl.shape, repl.dtype)]
    return pl.pallas_call(
        body, name="exchange_small", out_shape=shapes,
        in_specs=[ANY] * n, out_specs=[ANY] * n,
        scratch_shapes=EXCHANGE_SEMS * n,
    )(*arrs)


def _pack_rows(picks, name):
    arrs = [p[0] for p in picks]

    def body(*refs):
        out = refs[-1]
        out[...] = jnp.zeros_like(out)
        at = 0
        for ref, (_, row, rows, scale) in zip(refs[:-1], picks):
            out[at:at + rows, :] = ref[row:row + rows, :] * scale
            at += rows

    return pl.pallas_call(body, name=name, out_shape=SDS((8, D_MODEL), F32))(*arrs)


def _adamw(g, w, m, v):
    m2 = ADAM_B1 * m + (1.0 - ADAM_B1) * g
    v2 = ADAM_B2 * v + (1.0 - ADAM_B2) * (g * g)
    m_hat = m2 / (1.0 - ADAM_B1 ** ADAM_STEP)
    v_hat = v2 / (1.0 - ADAM_B2 ** ADAM_STEP)
    delta = -ADAM_LR * (m_hat / (jnp.sqrt(v_hat) + ADAM_EPS) + ADAM_WD * w)
    return delta, m2, v2


def _sum_slots(r_ref):
    g = r_ref[0]
    for s in range(1, r_ref.shape[0]):
        g = g + r_ref[s]
    return g


def _sum_adamw(recv, w, m, v, tr, name):
    rows, cols = w.shape

    def body(r_ref, w_ref, m_ref, v_ref, g_ref, d_ref, m2_ref, v2_ref):
        g = _sum_slots(r_ref)
        g_ref[...] = g
        d_ref[...], m2_ref[...], v2_ref[...] = _adamw(g, w_ref[...], m_ref[...], v_ref[...])

    blk = pl.BlockSpec((tr, cols), lambda i: (i, 0))
    return pl.pallas_call(
        body, name=name, grid=(rows // tr,),
        in_specs=[pl.BlockSpec((recv.shape[0], tr, cols), lambda i: (0, i, 0)), blk, blk, blk],
        out_specs=[blk] * 4, out_shape=[SDS((rows, cols), F32)] * 4,
        compiler_params=_cparams(("parallel",)),
    )(recv, w, m, v)


def _sum_adamw_pieces(recv, rows, ws, ms, vs, name, loss_row=None):
    n = len(ws)

    def body(r_ref, *refs):
        w_refs, m_refs, v_refs, outs = refs[:n], refs[n:2 * n], refs[2 * n:3 * n], refs[3 * n:]
        g = _sum_slots(r_ref)
        for i, (row, nrows) in enumerate(rows):
            gi = g[row:row + nrows, 0:ws[i].shape[1]]
            outs[i][...] = gi
            outs[n + i][...], outs[2 * n + i][...], outs[3 * n + i][...] = _adamw(
                gi, w_refs[i][...], m_refs[i][...], v_refs[i][...])
        if loss_row is not None:
            total = jnp.sum(g[loss_row:loss_row + 1, :], axis=-1, keepdims=True)
            outs[4 * n][...] = jnp.broadcast_to(total, outs[4 * n].shape)

    shapes = [SDS(w.shape, F32) for w in ws] * 4 + ([SDS((8, 128), F32)] if loss_row is not None else [])
    out = pl.pallas_call(body, name=name, out_shape=shapes)(recv, *ws, *ms, *vs)
    return [out[k * n:(k + 1) * n] for k in range(4)] + list(out[4 * n:])


def _shard_rows(t, lead):
    r = t.shape[1] // N_DEV
    t = t.reshape((lead, N_DEV, r) + t.shape[2:])
    return jnp.moveaxis(t, 1, 0)


def _pad_tile(t):
    return jnp.pad(t, ((0, 0), (0, 8 - t.shape[1]), (0, SM_LANES - t.shape[2])))


def _pack_small(wa, wx, cw, b_a, b_x):
    n = wa.shape[0]
    return jnp.concatenate([wa.reshape(n, 128, SM_LANES), wx.reshape(n, 128, SM_LANES),
                            _pad_tile(cw), _pad_tile(b_a), _pad_tile(b_x)], axis=1)


def _unpack_small(t):
    n = t.shape[0]
    return (t[:, SM_WA:SM_WA + 128].reshape(n, 4, 32, 256), t[:, SM_WX:SM_WX + 128].reshape(n, 4, 32, 256),
            t[:, SM_CW:SM_CW + 4, 0:128], t[:, SM_BA:SM_BA + 4, 0:32], t[:, SM_BX:SM_BX + 4, 0:32])


def kernel(x, pre_norm_w, w_in, conv_w, conv_b, lru_w_a, lru_b_a, lru_w_x, lru_b_x, lru_lambda, hgrn_lb_logits, hgrn_gnorm_w, w_out, post_norm_w, loss_target, m_pre_norm_w, m_w_in, m_conv_w, m_conv_b, m_lru_w_a, m_lru_b_a, m_lru_w_x, m_lru_b_x, m_lru_lambda, m_hgrn_lb_logits, m_hgrn_gnorm_w, m_w_out, m_post_norm_w, v_pre_norm_w, v_w_in, v_conv_w, v_conv_b, v_lru_w_a, v_lru_b_a, v_lru_w_x, v_lru_b_x, v_lru_lambda, v_hgrn_lb_logits, v_hgrn_gnorm_w, v_w_out, v_post_norm_w):
    seq = x.shape[1]
    x2 = x.reshape(seq, D_MODEL)
    tgt = loss_target.reshape(seq, D_MODEL)

    small_w = _pack_small(lru_w_a, lru_w_x, conv_w, lru_b_a, lru_b_x)[0]
    me = (4 * lax.axis_index("x") + 2 * lax.axis_index("y") + lax.axis_index("c")).astype(jnp.int32).reshape(1)
    p, u, w_in_all, w_out_all, small_all = _in_proj_gather(
        x2, pre_norm_w, w_in[0].astype(BF16), w_out[0].astype(BF16), small_w, me, seq)
    wa_s, wx_s, cw_s, ba_s, bx_s = _unpack_small(small_all)
    wa = jnp.moveaxis(wa_s, 0, 1).reshape(LRU_BLOCKS, LRU_BW, LRU_BW).astype(BF16)
    wx = jnp.moveaxis(wx_s, 0, 1).reshape(LRU_BLOCKS, LRU_BW, LRU_BW).astype(BF16)
    cw = jnp.moveaxis(cw_s, 0, 1).reshape(4, D_MODEL)
    ba = jnp.moveaxis(ba_s, 0, 1).reshape(1, D_MODEL)
    bx = jnp.moveaxis(bx_s, 0, 1).reshape(1, D_MODEL)

    loc = _local_step(x2, tgt, p, u, cw, conv_b, wa, wx, ba, bx, lru_lambda,
                      hgrn_lb_logits, hgrn_gnorm_w, w_out_all.reshape(D_MIX, D_MODEL), post_norm_w)

    ls, r_out, r_small = loc["lru_small"], loc["r_out"], loc["r_small"]
    core = lax.axis_index("c").astype(jnp.int32).reshape(1)
    g_sib = _grad_w_in_sibling(loc["u"], loc["dp"], core, seq)
    g_chip = _grad_w_in_own(loc["u"], loc["dp"], core, g_sib, seq)
    grad_x, pre_small, r_in = _grad_x(loc["dp"], w_in_all, x2, pre_norm_w, loc["dout"], g_chip, seq)
    g_repl = _pack_rows([(pre_small, 0, 1, 1.0), (ls, 4, 1, 1.0), (ls, 7, 1, 1.0),
                         (loc["hgrn_small"], 0, 1, 1.0), (loc["hgrn_small"], 0, 1, -1.0),
                         (loc["hgrn_small"], 1, 1, 1.0), (loc["stats"], 0, 2, 1.0)], "pack_grads")
    (r_repl,) = _exchange_grads([], g_repl)

    repl_rows = [(RP_PRE, 1), (RP_CB, 1), (RP_LAM, 1), (RP_LB0, 2), (RP_GN, 1), (RP_POST, 1)]
    o_repl = _sum_adamw_pieces(
        r_repl, repl_rows,
        [pre_norm_w, conv_b, lru_lambda, hgrn_lb_logits, hgrn_gnorm_w, post_norm_w],
        [m_pre_norm_w, m_conv_b, m_lru_lambda, m_hgrn_lb_logits, m_hgrn_gnorm_w, m_post_norm_w],
        [v_pre_norm_w, v_conv_b, v_lru_lambda, v_hgrn_lb_logits, v_hgrn_gnorm_w, v_post_norm_w],
        "adamw_repl", loss_row=RP_LOSS)
    loss = o_repl[4][0, 0]

    small_rows = [(SM_WA, 128), (SM_WX, 128), (SM_CW, 4), (SM_BA, 4), (SM_BX, 4)]
    as_rows = lambda wa_, wx_, cw_, ba_, bx_: [wa_.reshape(128, 256), wx_.reshape(128, 256), cw_.reshape(4, 128),
                                               ba_.reshape(4, 32), bx_.reshape(4, 32)]
    o_small = _sum_adamw_pieces(
        r_small, small_rows, as_rows(lru_w_a, lru_w_x, conv_w, lru_b_a, lru_b_x),
        as_rows(m_lru_w_a, m_lru_w_x, m_conv_w, m_lru_b_a, m_lru_b_x),
        as_rows(v_lru_w_a, v_lru_w_x, v_conv_w, v_lru_b_a, v_lru_b_x), "adamw_small")

    o_in = _sum_adamw(r_in, w_in[0], m_w_in[0], v_w_in[0], 128, "adamw_w_in")
    o_out = _sum_adamw(r_out, w_out[0], m_w_out[0], v_w_out[0], 64, "adamw_w_out")

    outs = [loss, grad_x.reshape(x.shape)]
    for kind in range(4):
        pre, cb, lam, lb, gn, post = o_repl[kind]
        swa, swx, scw, sba, sbx = o_small[kind]
        outs += [pre, o_in[kind][None], scw.reshape(conv_w.shape), cb, swa.reshape(lru_w_a.shape),
                 sba.reshape(lru_b_a.shape), swx.reshape(lru_w_x.shape), sbx.reshape(lru_b_x.shape),
                 lam, lb, gn, o_out[kind][None], post]
    return tuple(outs)
```

```python
import functools

import jax
import jax.numpy as jnp
from jax import lax
from jax.experimental import pallas as pl
from jax.experimental.pallas import tpu as pltpu

F32 = jnp.float32
BF16 = jnp.bfloat16
SDS = jax.ShapeDtypeStruct

D_MODEL = 1024
D_IN = 6144
N_DEV = 8
N_CHIPS = 4
W_BLK = D_IN // N_DEV
D_MIX = 2048
LRU_BLOCKS = 4
LRU_BW = 256
LRU_C = 8.0
N_HEADS = 8
HEAD_D = 128
CHUNK = 128
SUB = 32
N_SUB = CHUNK // SUB
HGRN_FWD_STEP_CHUNKS = 4
HGRN_STEP_CHUNKS = 2
EXP_CLAMP = 80.0
EPS = 1e-6

ADAM_LR = 0.001
ADAM_B1 = 0.9
ADAM_B2 = 0.999
ADAM_EPS = 1e-08
ADAM_WD = 0.01
ADAM_STEP = 10

VMEM_LIMIT = 56 * 1024 * 1024

NN = (((1,), (0,)), ((), ()))
NT = (((1,), (1,)), ((), ()))
TN = (((0,), (0,)), ((), ()))

SM_LANES = 256
SM_WA = 0
SM_WX = 128
SM_CW = 256
SM_BA = 264
SM_BX = 272
SM_ROWS = 280

RP_PRE, RP_CB, RP_LAM, RP_LB0, RP_LB1, RP_GN, RP_POST, RP_LOSS = range(8)


def _dot(a, b, dims=NN):
    return lax.dot_general(a, b, dims, preferred_element_type=F32)


def _sigmoid(x):
    return 0.5 * jnp.tanh(0.5 * x) + 0.5


def _sigmoid_pos(x):
    return 1.0 / (1.0 + jnp.exp(-x))


def _cparams(sem, vmem=VMEM_LIMIT):
    return pltpu.CompilerParams(dimension_semantics=sem, vmem_limit_bytes=vmem)


def _iota(shape, axis):
    return lax.broadcasted_iota(jnp.int32, shape, axis)


def _softplus_neg(lam):
    z = -lam
    e = jnp.exp(-jnp.abs(z))
    u = 1.0 + e
    log1p_e = jnp.where(u == 1.0, e, jnp.log(u) * (e / (u - 1.0)))
    sp = jnp.maximum(z, 0.0) + log1p_e
    dsp = -jnp.where(z >= 0.0, 1.0 / u, e / u)
    return sp, dsp


def _neg_expm1(x):
    poly = x * (1.0 + x * (1.0 / 2 + x * (1.0 / 6 + x * (1.0 / 24 + x * (1.0 / 120)))))
    return jnp.where(x > -1.0 / 16, -poly, 1.0 - jnp.exp(x))


def _conv_taps(lx, prev8, cw_ref, cb_ref, tile):
    xc = cb_ref[...] + cw_ref[3:4, :] * lx
    for j in (1, 2, 3):
        xc = xc + cw_ref[3 - j:4 - j, :] * pltpu.roll(lx, j, 0)
    row8 = _iota((8, D_MODEL), 0)
    last8 = lx[tile - 8:tile, :]
    fix = jnp.zeros((8, D_MODEL), F32)
    for j in (1, 2, 3):
        wrong = pltpu.roll(last8, j, 0)
        right = pltpu.roll(prev8, j, 0)
        fix = fix + cw_ref[3 - j:4 - j, :] * jnp.where(row8 < j, right - wrong, 0.0)
    return xc, fix


def _lru_gates(xcs, wa, wx, ba, bx, sp):
    xb = xcs.astype(BF16)
    r = _sigmoid_pos(_dot(xb, wa) + ba)
    i = _sigmoid(_dot(xb, wx) + bx)
    la = (-LRU_C * sp) * r
    a = jnp.exp(la)
    one_minus_a2 = _neg_expm1(2.0 * la)
    return r, i, a, one_minus_a2


def _lru_forward(p, conv_w, conv_b, wa, wx, ba, bx, lam, seq):
    tile = min(512, seq // 2)
    nblk = tile // 8

    def body(lx_ref, gt_ref, cw_ref, cb_ref, wa_ref, wx_ref, ba_ref, bx_ref, lam_ref,
             h_ref, y_ref, ext, hcar, xc_scr, a_scr, u_scr):
        @pl.when(pl.program_id(0) == 0)
        def _():
            ext[0:8, :] = jnp.zeros((8, D_MODEL), F32)
            hcar[...] = jnp.zeros_like(hcar)

        lx = lx_ref[...]
        ext[8:8 + tile, :] = lx
        xc = cb_ref[...] + cw_ref[3:4, :] * lx
        for j in (1, 2, 3):
            xc = xc + cw_ref[3 - j:4 - j, :] * ext[8 - j:8 - j + tile, :]
        xc_scr[...] = xc
        ext[0:8, :] = lx_ref[tile - 8:tile, :]
        sp, _ = _softplus_neg(lam_ref[...])
        for n in range(LRU_BLOCKS):
            sl = slice(n * LRU_BW, (n + 1) * LRU_BW)
            xcs = xc_scr[:, sl]
            _, i, a, ne = _lru_gates(xcs, wa_ref[n], wx_ref[n], ba_ref[:, sl], bx_ref[:, sl], sp[:, sl])
            a_scr[:, sl] = a
            u_scr[:, sl] = jnp.sqrt(ne) * (i * xcs)

        row8 = _iota((8, D_MODEL), 0)

        def blk(j, hc):
            off = pl.multiple_of(j * 8, 8)
            a = a_scr[pl.ds(off, 8), :]
            u = u_scr[pl.ds(off, 8), :]
            for k in (1, 2, 4):
                m = row8 >= k
                u = jnp.where(m, u + a * pltpu.roll(u, k, 0), u)
                a = jnp.where(m, a * pltpu.roll(a, k, 0), a)
            h = u + a * hc
            h_ref[pl.ds(off, 8), :] = h
            return jnp.broadcast_to(h[7:8, :], (8, D_MODEL))

        hcar[...] = lax.fori_loop(0, nblk, blk, hcar[...])
        g = gt_ref[...]
        y_ref[...] = (h_ref[...] * (g * _sigmoid(g))).astype(BF16)

    full = lambda shape: pl.BlockSpec(shape, lambda t: (0,) * len(shape))
    return pl.pallas_call(
        body, name="lru_fwd", grid=(seq // tile,),
        in_specs=[pl.BlockSpec((tile, D_MODEL), lambda t: (t, 0)),
                  pl.BlockSpec((tile, D_MODEL), lambda t: (t, 1)),
                  full((4, D_MODEL)), full((1, D_MODEL)),
                  full((LRU_BLOCKS, LRU_BW, LRU_BW)), full((LRU_BLOCKS, LRU_BW, LRU_BW)),
                  full((1, D_MODEL)), full((1, D_MODEL)), full((1, D_MODEL))],
        out_specs=[pl.BlockSpec((tile, D_MODEL), lambda t: (t, 0)),
                   pl.BlockSpec((tile, D_MODEL), lambda t: (t, 0))],
        out_shape=[SDS((seq, D_MODEL), F32), SDS((seq, D_MODEL), BF16)],
        scratch_shapes=[pltpu.VMEM((tile + 8, D_MODEL), F32), pltpu.VMEM((8, D_MODEL), F32),
                        pltpu.VMEM((tile, D_MODEL), F32), pltpu.VMEM((tile, D_MODEL), F32),
                        pltpu.VMEM((tile, D_MODEL), F32)],
        compiler_params=_cparams(("arbitrary",)),
    )(p, p, conv_w, conv_b, wa, wx, ba, bx, lam)


def _lru_backward(p, h, dymix, conv_w, conv_b, wa, wx, ba, bx, lam, g_w_out, seq):
    tile = min(256, seq // 2)
    nt = seq // tile
    nblk = tile // 8
    t8 = tile // 8

    def body(lx_ref, lxh_ref, gt_ref, h_ref, hh_ref, dy_ref, cw_ref, cb_ref, wa_ref, wx_ref, ba_ref,
             bx_ref, lam_ref, go_ref, dp_ref, gwa_ref, gwx_ref, gsm_ref, ro_ref,
             lamcar, anext, dxc8, xc_scr, r_scr, i_scr, a_scr, m_scr, rm_scr, c_scr, l_scr, dxc_scr, *sems):
        step = pl.program_id(0)
        first_tile = step == nt - 1
        ex = _SlotExchange(go_ref, ro_ref, *sems, blocked=True)

        @pl.when(step == 0)
        def _():
            ex.start()
            lamcar[...] = jnp.zeros_like(lamcar)
            anext[...] = jnp.zeros_like(anext)
            dxc8[...] = jnp.zeros_like(dxc8)
            gwa_ref[...] = jnp.zeros_like(gwa_ref)
            gwx_ref[...] = jnp.zeros_like(gwx_ref)
            gsm_ref[...] = jnp.zeros_like(gsm_ref)

        keep = jnp.where(first_tile, 0.0, 1.0)
        lx = lx_ref[...]
        prev8 = lxh_ref[...] * keep
        xc, fix = _conv_taps(lx, prev8, cw_ref, cb_ref, tile)
        xc_scr[...] = xc
        xc_scr[0:8, :] = xc_scr[0:8, :] + fix
        sp, dsp = _softplus_neg(lam_ref[...])
        for n in range(LRU_BLOCKS):
            sl = slice(n * LRU_BW, (n + 1) * LRU_BW)
            r, i, a, ne = _lru_gates(xc_scr[:, sl], wa_ref[n], wx_ref[n], ba_ref[:, sl], bx_ref[:, sl],
                                     sp[:, sl])
            r_scr[:, sl] = r
            i_scr[:, sl] = i
            a_scr[:, sl] = a
            m_scr[:, sl] = jnp.sqrt(ne)
            rm_scr[:, sl] = lax.rsqrt(ne)

        g = gt_ref[...]
        sg = _sigmoid(g)
        dy = dy_ref[...]
        hv = h_ref[...]
        dp_ref[:, D_MODEL:2 * D_MODEL] = (dy * hv * (sg * (1.0 + g * (1.0 - sg)))).astype(BF16)

        rowt = _iota((tile, D_MODEL), 0)
        av = a_scr[...]
        l_scr[...] = dy * (g * sg)
        c_scr[...] = jnp.where(rowt == tile - 1, anext[...][0:1, :], pltpu.roll(av, tile - 1, 0))
        anext[...] = jnp.broadcast_to(av[0:1, :], (8, D_MODEL))
        row8 = _iota((8, D_MODEL), 0)

        def blk(jj, lc):
            off = pl.multiple_of((nblk - 1 - jj) * 8, 8)
            c = c_scr[pl.ds(off, 8), :]
            u = l_scr[pl.ds(off, 8), :]
            for k in (1, 2, 4):
                m = row8 < 8 - k
                u = jnp.where(m, u + c * pltpu.roll(u, 8 - k, 0), u)
                c = jnp.where(m, c * pltpu.roll(c, 8 - k, 0), c)
            lamv = u + c * lc
            l_scr[pl.ds(off, 8), :] = lamv
            return jnp.broadcast_to(lamv[0:1, :], (8, D_MODEL))

        lamcar[...] = lax.fori_loop(0, nblk, blk, lamcar[...])

        hprev = jnp.where(rowt == 0, hh_ref[...][7:8, :] * keep, pltpu.roll(hv, 1, 0))
        for n in range(LRU_BLOCKS):
            sl = slice(n * LRU_BW, (n + 1) * LRU_BW)
            lamv = l_scr[:, sl]
            xcs = xc_scr[:, sl]
            r = r_scr[:, sl]
            i = i_scr[:, sl]
            a = a_scr[:, sl]
            mult = m_scr[:, sl]
            d_la = lamv * hprev[:, sl] * a - (lamv * i * xcs) * (a * a * rm_scr[:, sl])
            d_pr = d_la * (-LRU_C * sp[:, sl]) * r * (1.0 - r)
            d_pi = (lamv * mult * xcs) * i * (1.0 - i)
            gsm_ref[7:8, sl] += jnp.sum(d_la * r, axis=0, keepdims=True) * (-LRU_C) * dsp[:, sl]
            gsm_ref[5:6, sl] += jnp.sum(d_pr, axis=0, keepdims=True)
            gsm_ref[6:7, sl] += jnp.sum(d_pi, axis=0, keepdims=True)
            xb = xcs.astype(BF16)
            prb = d_pr.astype(BF16)
            pib = d_pi.astype(BF16)
            gwa_ref[n] += _dot(xb, prb, TN)
            gwx_ref[n] += _dot(xb, pib, TN)
            dxc_scr[:, sl] = lamv * mult * i + _dot(prb, wa_ref[n], NT) + _dot(pib, wx_ref[n], NT)

        dxc = dxc_scr[...]
        gsm_ref[4:5, :] += jnp.sum(dxc, axis=0, keepdims=True)
        last8 = lx[tile - 8:tile, :]
        first8 = dxc[0:8, :]
        dlx = cw_ref[3:4, :] * dxc
        gsm_ref[3:4, :] += jnp.sum(dxc * lx, axis=0, keepdims=True)
        fix = jnp.zeros((8, D_MODEL), F32)
        for j in (1, 2, 3):
            w = cw_ref[3 - j:4 - j, :]
            dlx = dlx + w * pltpu.roll(dxc, tile - j, 0)
            fix = fix + w * jnp.where(row8 + j >= 8,
                                      pltpu.roll(dxc8[...], 8 - j, 0) - pltpu.roll(first8, 8 - j, 0), 0.0)
            halo = jnp.where(row8 < j, pltpu.roll(prev8, j, 0) - pltpu.roll(last8, j, 0), 0.0)
            gsm_ref[3 - j:4 - j, :] += (jnp.sum(dxc * pltpu.roll(lx, j, 0), axis=0, keepdims=True)
                                        + jnp.sum(first8 * halo, axis=0, keepdims=True))
        dxc8[...] = first8
        dp_ref[:, 0:D_MODEL] = dlx.astype(BF16)
        top = tile - 8
        dp_ref[top:tile, 0:D_MODEL] = (dlx[top:tile, :] + fix).astype(BF16)

        @pl.when(step == nt - 1)
        def _():
            ex.wait()

    rev = lambda t: (nt - 1 - t, 0)
    halo_idx = lambda t: (jnp.maximum((nt - 1 - t) * t8 - 1, 0), 0)
    full = lambda shape: pl.BlockSpec(shape, lambda t: (0,) * len(shape))
    big = lambda: pltpu.VMEM((tile, D_MODEL), F32)
    return pl.pallas_call(
        body, name="lru_bwd", grid=(nt,),
        in_specs=[pl.BlockSpec((tile, D_MODEL), rev),
                  pl.BlockSpec((8, D_MODEL), halo_idx),
                  pl.BlockSpec((tile, D_MODEL), lambda t: (nt - 1 - t, 1)),
                  pl.BlockSpec((tile, D_MODEL), rev),
                  pl.BlockSpec((8, D_MODEL), halo_idx),
                  pl.BlockSpec((tile, D_MODEL), rev),
                  full((4, D_MODEL)), full((1, D_MODEL)),
                  full((LRU_BLOCKS, LRU_BW, LRU_BW)), full((LRU_BLOCKS, LRU_BW, LRU_BW)),
                  full((1, D_MODEL)), full((1, D_MODEL)), full((1, D_MODEL)), ANY],
        out_specs=[pl.BlockSpec((tile, 2 * D_MODEL), rev),
                   full((LRU_BLOCKS, LRU_BW, LRU_BW)), full((LRU_BLOCKS, LRU_BW, LRU_BW)),
                   full((8, D_MODEL)), ANY],
        out_shape=[SDS((seq, D_IN), BF16), SDS((LRU_BLOCKS, LRU_BW, LRU_BW), F32),
                   SDS((LRU_BLOCKS, LRU_BW, LRU_BW), F32), SDS((8, D_MODEL), F32), SDS(g_w_out.shape, F32)],
        scratch_shapes=[pltpu.VMEM((8, D_MODEL), F32), pltpu.VMEM((8, D_MODEL), F32),
                        pltpu.VMEM((8, D_MODEL), F32)] + [big() for _ in range(9)] + EXCHANGE_SEMS,
        compiler_params=_cparams(("arbitrary",)),
    )(p, p, p, h, h, dymix, conv_w, conv_b, wa, wx, ba, bx, lam, g_w_out)


def _tri_matmul(tri, g):
    hi = g.astype(BF16)
    lo = (g - hi.astype(F32)).astype(BF16)
    return _dot(tri, lo) + _dot(tri, hi)


def _hgrn_gate_terms(q, fr, lbl):
    lb = _sigmoid_pos(lbl[0:1, :] - lbl[1:2, :])
    half = 0.5 * (1.0 - lb)
    tf = jnp.tanh(0.5 * fr)
    f = (lb + half) + half * tf
    hq = 0.5 * q
    tq = jnp.tanh(hq)
    return lb, tf, f, tq, hq * tq + hq


def _hgrn_decay(bh):
    zero = jnp.zeros((1, bh.shape[1]), F32)
    rho = [zero] + [bh[s * SUB - 1:s * SUB, :] for s in range(1, N_SUB + 1)]
    start = _sub_rows(rho[0:N_SUB])
    end = _sub_rows(rho[1:N_SUB + 1])
    mid = 0.5 * (start + end)
    blast = rho[N_SUB]
    e_on = jnp.exp(bh - start)
    e_off = jnp.exp(end - bh)
    scales = [_sub_rows([jnp.exp(rho[i] - rho[j + 1]) if i > j else zero for i in range(N_SUB)])
              for j in range(N_SUB - 1)]
    return dict(eq0=jnp.exp(jnp.minimum(bh - mid, EXP_CLAMP)), ek0=jnp.exp(jnp.minimum(mid - bh, EXP_CLAMP)),
                e_on=e_on, e_off=e_off, scales=scales,
                eb=e_on * _sub_rows([jnp.exp(r) for r in rho[0:N_SUB]]),
                ekst=e_off * _sub_rows([jnp.exp(blast - r) for r in rho[1:N_SUB + 1]]),
                ebl=jnp.exp(blast))


def _sub_rows(vecs):
    return jnp.concatenate([jnp.broadcast_to(v, (SUB, v.shape[1])) for v in vecs], axis=0)


def _hgrn_operands(qs, k, dec, qt_scr, kt_scr):
    sub = jnp.right_shift(_iota(qs.shape, 0), 5)
    qon = qs * dec["e_on"]
    koff = k * dec["e_off"]
    qt_scr[0] = (qs * dec["eq0"]).astype(BF16)
    kt_scr[0] = (k * dec["ek0"]).astype(BF16)
    for j in range(N_SUB - 1):
        qt_scr[j + 1] = (qon * dec["scales"][j]).astype(BF16)
        kt_scr[j + 1] = jnp.where(sub == j, koff, 0.0).astype(BF16)
    return koff


def _hgrn_head_scores(qt_scr, kt_scr, sl, diag):
    a = jnp.where(diag, _dot(qt_scr[0, :, sl], kt_scr[0, :, sl], NT), 0.0)
    for j in range(1, N_SUB):
        a = a + _dot(qt_scr[j, :, sl], kt_scr[j, :, sl], NT)
    return a


def _hgrn_forward(p, lbl, gw, seq):
    nc = seq // CHUNK
    assert SUB == 32

    def body(q_ref, f_ref, v_ref, hg_ref, lbl_ref, gw_ref, y_ref, o_ref, st_ref,
             s_scr, qt_scr, kt_scr, qin_scr, kst_scr, vb_scr, a_scr):
        @pl.when(pl.program_id(0) == 0)
        def _():
            s_scr[...] = jnp.zeros_like(s_scr)

        r = _iota((CHUNK, CHUNK), 0)
        c = _iota((CHUNK, CHUNK), 1)
        tri = jnp.where(c <= r, 1.0, 0.0).astype(BF16)
        diag = (jnp.right_shift(r, 5) == jnp.right_shift(c, 5)) & (c <= r)
        heads = [slice(h * HEAD_D, (h + 1) * HEAD_D) for h in range(N_HEADS)]
        for cc in range(HGRN_FWD_STEP_CHUNKS):
            rows = slice(cc * CHUNK, (cc + 1) * CHUNK)
            q = q_ref[rows, :]
            _, _, f, _, qs = _hgrn_gate_terms(q, f_ref[rows, :], lbl_ref[...])
            k = 1.0 - f
            dec = _hgrn_decay(_tri_matmul(tri, jnp.log(f)))
            _hgrn_operands(qs, k, dec, qt_scr, kt_scr)
            qin_scr[...] = (qs * dec["eb"]).astype(BF16)
            kst_scr[...] = (k * dec["ekst"]).astype(BF16)
            vb_scr[...] = v_ref[rows, :].astype(BF16)
            ebl = dec["ebl"]
            hg = hg_ref[rows, :]
            gate = gw_ref[...] * (hg * _sigmoid(hg))
            stb = []
            for h, sl in enumerate(heads):
                st = s_scr[h]
                st_ref[cc, h] = st
                stb.append(st.astype(BF16))
                s_scr[h] = st * ebl[:, sl] + _dot(vb_scr[:, sl], kst_scr[:, sl], TN)
            for h, sl in enumerate(heads):
                a_scr[h] = _hgrn_head_scores(qt_scr, kt_scr, sl, diag).astype(BF16)
            for h, sl in enumerate(heads):
                o = _dot(a_scr[h], vb_scr[:, sl]) + _dot(qin_scr[:, sl], stb[h], NT)
                o_ref[rows, sl] = o
                rs = lax.rsqrt(jnp.mean(o * o, axis=-1, keepdims=True) + EPS)
                y_ref[rows, sl] = ((o * rs) * gate[:, sl]).astype(BF16)

    col = lambda j: pl.BlockSpec((HGRN_FWD_STEP_CHUNKS * CHUNK, D_MODEL), lambda c: (c, j))
    par = lambda rows: pl.BlockSpec((rows, D_MODEL), lambda c: (0, 0))
    return pl.pallas_call(
        body, name="hgrn_fwd", grid=(nc // HGRN_FWD_STEP_CHUNKS,),
        in_specs=[col(2), col(3), col(4), col(5), par(2), par(1)],
        out_specs=[col(0), col(0),
                   pl.BlockSpec((HGRN_FWD_STEP_CHUNKS, N_HEADS, HEAD_D, HEAD_D), lambda c: (c, 0, 0, 0))],
        out_shape=[SDS((seq, D_MODEL), BF16), SDS((seq, D_MODEL), F32),
                   SDS((nc, N_HEADS, HEAD_D, HEAD_D), F32)],
        scratch_shapes=[pltpu.VMEM((N_HEADS, HEAD_D, HEAD_D), F32),
                        pltpu.VMEM((N_SUB, CHUNK, D_MODEL), BF16), pltpu.VMEM((N_SUB, CHUNK, D_MODEL), BF16)]
                       + [pltpu.VMEM((CHUNK, D_MODEL), BF16)] * 3 + [pltpu.VMEM((N_HEADS, CHUNK, CHUNK), BF16)],
        compiler_params=_cparams(("arbitrary",)),
    )(p, p, p, p, lbl, gw)


def _hgrn_backward(p, o, states, dymix, lbl, gw, dp_full, g_small, seq):
    step_rows = HGRN_STEP_CHUNKS * CHUNK
    ns = seq // step_rows

    def body(q_ref, f_ref, v_ref, hg_ref, o_ref, st_ref, dy_ref, lbl_ref, gw_ref, dpin_ref, gs_ref,
             dpo_ref, gsm_ref, rs_ref, ds_scr, dp_buf, dp_sems, *rest):
        del dpin_ref
        scratch, sems = rest[:14], rest[14:]
        step = pl.program_id(0)
        slot = step % 2
        exs = [_SlotExchange(gs_ref, rs_ref, *sems, blocked=True)]

        @pl.when(step == 0)
        def _():
            for ex in exs:
                ex.start()

        def out_copy(s, blk):
            rows = pl.ds(pl.multiple_of(blk * step_rows, step_rows), step_rows)
            return pltpu.make_async_copy(dp_buf.at[s], dpo_ref.at[rows, pl.ds(2 * D_MODEL, 4 * D_MODEL)],
                                         dp_sems.at[s])

        @pl.when(step == 0)
        def _():
            ds_scr[...] = jnp.zeros_like(ds_scr)
            gsm_ref[...] = jnp.zeros_like(gsm_ref)

        @pl.when(step >= 2)
        def _():
            out_copy(slot, ns + 1 - step).wait()

        for cc in reversed(range(HGRN_STEP_CHUNKS)):
            chunk(cc, q_ref, f_ref, v_ref, hg_ref, o_ref, st_ref, dy_ref, lbl_ref, gw_ref, gsm_ref, ds_scr,
                  dp_buf.at[slot], *scratch)

        out_copy(slot, ns - 1 - step).start()

        @pl.when(step == ns - 1)
        def _():
            out_copy(1 - slot, 1).wait()
            out_copy(slot, 0).wait()
            for ex in exs:
                ex.wait()

    def chunk(cc, q_ref, f_ref, v_ref, hg_ref, o_ref, st_ref, dy_ref, lbl_ref, gw_ref, gsm_ref, ds_scr, dp_ref,
              qt_scr, kt_scr, qin_scr, kst_scr, vb_scr, dob_scr, g_scr, h_scr, dqi_scr, dks_scr, sd_scr,
              a_scr, da_scr, da0_scr):
        rows = slice(cc * CHUNK, (cc + 1) * CHUNK)
        r = _iota((CHUNK, CHUNK), 0)
        c = _iota((CHUNK, CHUNK), 1)
        tri = jnp.where(c <= r, 1.0, 0.0).astype(BF16)
        triu = jnp.where(c >= r, 1.0, 0.0).astype(BF16)
        diag = (jnp.right_shift(r, 5) == jnp.right_shift(c, 5)) & (c <= r)
        row = _iota((CHUNK, D_MODEL), 0)
        sub = jnp.right_shift(row, 5)

        q = q_ref[rows, :]
        lb, tf, f, tq, qs = _hgrn_gate_terms(q, f_ref[rows, :], lbl_ref[...])
        sig = 0.5 * tf + 0.5
        sq = 0.5 * tq + 0.5
        k = 1.0 - f
        dec = _hgrn_decay(_tri_matmul(tri, jnp.log(f)))
        eb, ekst, ebl = dec["eb"], dec["ekst"], dec["ebl"]
        koff = _hgrn_operands(qs, k, dec, qt_scr, kt_scr)
        qin_scr[...] = (qs * eb).astype(BF16)
        kst_scr[...] = (k * ekst).astype(BF16)
        vb_scr[...] = v_ref[rows, :].astype(BF16)
        hg = hg_ref[rows, :]
        sh = _sigmoid(hg)
        dy = dy_ref[rows, :]
        gwv = gw_ref[...]
        d_onw = dy * (hg * sh)
        d_on = d_onw * gwv
        d_gate = dy * gwv * (sh * (1.0 + hg * (1.0 - sh)))

        heads = [slice(h * HEAD_D, (h + 1) * HEAD_D) for h in range(N_HEADS)]
        for h, sl in enumerate(heads):
            o = o_ref[rows, sl]
            rs = lax.rsqrt(jnp.mean(o * o, axis=-1, keepdims=True) + EPS)
            on = o * rs
            dp_ref[rows, 3 * D_MODEL + h * HEAD_D:3 * D_MODEL + (h + 1) * HEAD_D] = (d_gate[:, sl] * on).astype(BF16)
            gsm_ref[1:2, sl] += jnp.sum(d_onw[:, sl] * on, axis=0, keepdims=True)
            d_onh = d_on[:, sl]
            dob_scr[:, sl] = (rs * (d_onh - on * jnp.mean(d_onh * on, axis=-1, keepdims=True))).astype(BF16)
        for h, sl in enumerate(heads):
            a_scr[h] = _hgrn_head_scores(qt_scr, kt_scr, sl, diag).astype(BF16)
            da = _dot(dob_scr[:, sl], vb_scr[:, sl], NT)
            da_scr[h] = da.astype(BF16)
            da0_scr[h] = jnp.where(diag, da, 0.0).astype(BF16)
        for h, sl in enumerate(heads):
            st = st_ref[cc, h]
            dst = ds_scr[h]
            dstb = dst.astype(BF16)
            dp_ref[rows, 2 * D_MODEL + h * HEAD_D:2 * D_MODEL + (h + 1) * HEAD_D] = (
                _dot(a_scr[h], dob_scr[:, sl], TN) + _dot(kst_scr[:, sl], dstb, NT)).astype(BF16)
            dqi_scr[:, sl] = _dot(dob_scr[:, sl], st.astype(BF16))
            dks_scr[:, sl] = _dot(vb_scr[:, sl], dstb)
            sd_scr[0:1, sl] = jnp.sum(st * dst, axis=0, keepdims=True)
            ds_scr[h] = dst * ebl[:, sl] + _dot(dob_scr[:, sl], qin_scr[:, sl], TN)
        for h, sl in enumerate(heads):
            g_scr[0, :, sl] = _dot(da0_scr[h], kt_scr[0, :, sl])
            h_scr[0, :, sl] = _dot(da0_scr[h], qt_scr[0, :, sl], TN)
            for j in range(1, N_SUB):
                g_scr[j, :, sl] = _dot(da_scr[h], kt_scr[j, :, sl])
                h_scr[j, :, sl] = _dot(da_scr[h], qt_scr[j, :, sl], TN)

        g0 = g_scr[0]
        h0 = h_scr[0]
        dq_inter = eb * dqi_scr[...]
        d_kst = ekst * dks_scr[...]
        db = qs * dq_inter - k * d_kst + qt_scr[0].astype(F32) * g0 - kt_scr[0].astype(F32) * h0
        gq = jnp.zeros((CHUNK, D_MODEL), F32)
        hsel = jnp.zeros((CHUNK, D_MODEL), F32)
        for j in range(N_SUB - 1):
            gj = g_scr[j + 1]
            gq = gq + dec["scales"][j] * gj
            db = db + qt_scr[j + 1].astype(F32) * gj
            hsel = jnp.where(sub == j, h_scr[j + 1], hsel)
        db = db - koff.astype(BF16).astype(F32) * hsel
        d_q = dec["eq0"] * g0 + dec["e_on"] * gq + dq_inter
        d_k = dec["ek0"] * h0 + dec["e_off"] * hsel + d_kst
        db_last = jnp.sum(k * d_kst, axis=0, keepdims=True) + ebl * sd_scr[0:1, :]
        db = db + jnp.where(row == CHUNK - 1, db_last, 0.0)
        dg = _tri_matmul(triu, db)
        d_f = dg / f - d_k
        dp_ref[rows, D_MODEL:2 * D_MODEL] = (d_f * (1.0 - lb) * sig * (1.0 - sig)).astype(BF16)
        gsm_ref[0:1, :] += jnp.sum(d_f * (1.0 - sig), axis=0, keepdims=True) * (lb * (1.0 - lb))
        dp_ref[rows, 0:D_MODEL] = (d_q * (sq * (1.0 + q * (1.0 - sq)))).astype(BF16)

    rc = lambda c: ns - 1 - c
    col = lambda j: pl.BlockSpec((step_rows, D_MODEL), lambda c: (rc(c), j))
    par = lambda rows: pl.BlockSpec((rows, D_MODEL), lambda c: (0, 0))
    return pl.pallas_call(
        body, name="hgrn_bwd", grid=(ns,),
        in_specs=[col(2), col(3), col(4), col(5), col(0),
                  pl.BlockSpec((HGRN_STEP_CHUNKS, N_HEADS, HEAD_D, HEAD_D), lambda c: (rc(c), 0, 0, 0)),
                  col(1), par(2), par(1), ANY, ANY],
        out_specs=[ANY, par(8), ANY],
        out_shape=[SDS((seq, D_IN), BF16), SDS((8, D_MODEL), F32), SDS(g_small.shape, F32)],
        input_output_aliases={9: 0},
        scratch_shapes=[pltpu.VMEM((N_HEADS, HEAD_D, HEAD_D), F32),
                        pltpu.VMEM((2, step_rows, 4 * D_MODEL), BF16), pltpu.SemaphoreType.DMA((2,)),
                        pltpu.VMEM((N_SUB, CHUNK, D_MODEL), BF16), pltpu.VMEM((N_SUB, CHUNK, D_MODEL), BF16)]
                       + [pltpu.VMEM((CHUNK, D_MODEL), BF16)] * 4
                       + [pltpu.VMEM((N_SUB, CHUNK, D_MODEL), F32)] * 2 + [pltpu.VMEM((CHUNK, D_MODEL), F32)] * 2
                       + [pltpu.VMEM((8, D_MODEL), F32)] + [pltpu.VMEM((N_HEADS, CHUNK, CHUNK), BF16)] * 3
                       + EXCHANGE_SEMS,
        compiler_params=_cparams(("arbitrary",)),
    )(p, p, p, p, o, states, dymix, lbl, gw, dp_full, g_small)


def _out_proj(yl, yh, wo, x, tgt, post_w, seq):
    tm = 512

    def body(yl_ref, yh_ref, wo_ref, x_ref, tg_ref, pw_ref, dymix_ref, dout_ref, gwo_ref, st_ref):
        @pl.when(pl.program_id(0) == 0)
        def _():
            gwo_ref[...] = jnp.zeros_like(gwo_ref)
            st_ref[...] = jnp.zeros_like(st_ref)

        ylv = yl_ref[...]
        yhv = yh_ref[...]
        y = _dot(ylv, wo_ref[0:D_MODEL, :]) + _dot(yhv, wo_ref[D_MODEL:D_MIX, :])
        r2 = lax.rsqrt(jnp.mean(y * y, axis=-1, keepdims=True) + EPS)
        yn = y * r2
        pw = pw_ref[...]
        e = (x_ref[...] + yn * pw) - tg_ref[...]
        st_ref[1:2, :] += jnp.sum(e * e, axis=0, keepdims=True) * (0.5 / D_MODEL)
        dout = e * (1.0 / D_MODEL)
        dout_ref[...] = dout
        st_ref[0:1, :] += jnp.sum(dout * yn, axis=0, keepdims=True)
        dyn = dout * pw
        dy = r2 * (dyn - yn * jnp.mean(dyn * yn, axis=-1, keepdims=True))
        dyb = dy.astype(BF16)
        dymix_ref[...] = _dot(dyb, wo_ref[...], NT)
        gwo_ref[0:D_MODEL, :] += _dot(ylv, dyb, TN)
        gwo_ref[D_MODEL:D_MIX, :] += _dot(yhv, dyb, TN)

    row = lambda w: pl.BlockSpec((tm, w), lambda m: (m, 0))
    full = lambda shape: pl.BlockSpec(shape, lambda m: (0,) * len(shape))
    once = lambda shape: pl.BlockSpec(shape, lambda m: (0,) * len(shape), pipeline_mode=pl.Buffered(1))
    return pl.pallas_call(
        body, name="out_proj", grid=(seq // tm,),
        in_specs=[row(D_MODEL), row(D_MODEL), once((D_MIX, D_MODEL)), row(D_MODEL), row(D_MODEL),
                  full((1, D_MODEL))],
        out_specs=[row(D_MIX), row(D_MODEL), once((D_MIX, D_MODEL)), full((8, D_MODEL))],
        out_shape=[SDS((seq, D_MIX), F32), SDS((seq, D_MODEL), F32), SDS((D_MIX, D_MODEL), F32),
                   SDS((8, D_MODEL), F32)],
        compiler_params=_cparams(("arbitrary",)),
    )(yl, yh, wo, x, tgt, post_w)


MESH = pl.DeviceIdType.MESH
ANY = pl.BlockSpec(memory_space=pl.ANY)
EXCHANGE_SEMS = [pltpu.SemaphoreType.DMA((N_DEV - 1,)), pltpu.SemaphoreType.DMA((N_DEV - 1,)),
                 pltpu.SemaphoreType.DMA(())]


def _mesh_pos():
    return lax.axis_index("x"), lax.axis_index("y"), lax.axis_index("c")


class _SlotExchange:
    def __init__(self, src_ref, dst_ref, send_sems, recv_sems, local_sem, blocked):
        x, y, c = _mesh_pos()
        me = 4 * x + 2 * y + c
        src = (lambda dest: src_ref.at[dest]) if blocked else (lambda dest: src_ref)
        self.local = pltpu.make_async_copy(src(me), dst_ref.at[me], local_sem)
        self.sends, self.recvs = [], []
        for k in range(1, N_DEV):
            px = 1 - x if (k >> 2) & 1 else x
            py = 1 - y if (k >> 1) & 1 else y
            pc = 1 - c if k & 1 else c
            peer = 4 * px + 2 * py + pc
            sems = dict(send_sem=send_sems.at[k - 1], recv_sem=recv_sems.at[k - 1],
                        device_id=(px, py, pc), device_id_type=MESH)
            self.sends.append(pltpu.make_async_remote_copy(src_ref=src(peer), dst_ref=dst_ref.at[me], **sems))
            self.recvs.append(pltpu.make_async_remote_copy(src_ref=dst_ref.at[peer], dst_ref=dst_ref.at[peer], **sems))

    def start(self):
        self.local.start()
        for cp in self.sends:
            cp.start()

    def wait(self):
        for cp in self.recvs:
            cp.wait_recv()
        for cp in self.sends:
            cp.wait_send()
        self.local.wait()


class _ChipExchange:
    def __init__(self, src_ref, dst_ref, send_sems, recv_sems, local_sem):
        x, y, c = _mesh_pos()
        chip = 2 * x + y
        self.local = pltpu.make_async_copy(src_ref.at[chip], dst_ref.at[chip], local_sem)
        self.sends, self.recvs = [], []
        for k in range(1, N_CHIPS):
            px = 1 - x if (k >> 1) & 1 else x
            py = 1 - y if k & 1 else y
            peer = 2 * px + py
            sems = dict(send_sem=send_sems.at[k - 1], recv_sem=recv_sems.at[k - 1],
                        device_id=(px, py, c), device_id_type=MESH)
            self.sends.append(pltpu.make_async_remote_copy(src_ref=src_ref.at[peer], dst_ref=dst_ref.at[chip], **sems))
            self.recvs.append(pltpu.make_async_remote_copy(src_ref=dst_ref.at[peer], dst_ref=dst_ref.at[peer], **sems))

    def start(self):
        self.local.start()
        for cp in self.sends:
            cp.start()

    def wait(self):
        for cp in self.recvs:
            cp.wait_recv()
        for cp in self.sends:
            cp.wait_send()
        self.local.wait()


GRAD_W_IN_TK = 2048


def _grad_w_in_sibling(u, dp, core, seq):
    tk = min(GRAD_W_IN_TK, seq)
    nk = seq // tk

    def body(core_ref, u_ref, dp_ref, g_ref):
        del core_ref

        @pl.when(pl.program_id(1) == 0)
        def _():
            g_ref[...] = jnp.zeros_like(g_ref)

        g_ref[0] += _dot(u_ref[...], dp_ref[...], TN)

    return pl.pallas_call(
        body, name="grad_w_in_sibling",
        grid_spec=pltpu.PrefetchScalarGridSpec(
            num_scalar_prefetch=1, grid=(N_CHIPS, nk),
            in_specs=[pl.BlockSpec((tk, D_MODEL), lambda n, k, c: (k, 0)),
                      pl.BlockSpec((tk, W_BLK), lambda n, k, c: (k, 2 * n + 1 - c[0]))],
            out_specs=pl.BlockSpec((1, D_MODEL, W_BLK), lambda n, k, c: (n, 0, 0))),
        out_shape=SDS((N_CHIPS, D_MODEL, W_BLK), F32),
        compiler_params=_cparams(("parallel", "arbitrary")),
    )(core, u, dp)


def _grad_w_in_own(u, dp, core, g_sib, seq):
    tk = min(GRAD_W_IN_TK, seq)
    nk = seq // tk

    def body(core_ref, u_ref, dp_ref, gsib_ref, g_ref, land, send_sem, recv_sem):
        del core_ref
        n = pl.program_id(0)
        k = pl.program_id(1)
        x, y, c = _mesh_pos()
        swap = pltpu.make_async_remote_copy(src_ref=gsib_ref, dst_ref=land, send_sem=send_sem, recv_sem=recv_sem,
                                            device_id=(x, y, 1 - c), device_id_type=MESH)

        @pl.when((n == 0) & (k == 0))
        def _():
            swap.start()

        @pl.when(k == 0)
        def _():
            g_ref[...] = jnp.zeros_like(g_ref)

        g_ref[0] += _dot(u_ref[...], dp_ref[...], TN)

        @pl.when((n == 0) & (k == nk - 1))
        def _():
            swap.wait_recv()

        @pl.when(k == nk - 1)
        def _():
            g_ref[0] += land[n]

        @pl.when((n == N_CHIPS - 1) & (k == nk - 1))
        def _():
            swap.wait_send()

    return pl.pallas_call(
        body, name="grad_w_in_own",
        grid_spec=pltpu.PrefetchScalarGridSpec(
            num_scalar_prefetch=1, grid=(N_CHIPS, nk),
            in_specs=[pl.BlockSpec((tk, D_MODEL), lambda n, k, c: (k, 0)),
                      pl.BlockSpec((tk, W_BLK), lambda n, k, c: (k, 2 * n + c[0])), ANY],
            out_specs=pl.BlockSpec((1, D_MODEL, W_BLK), lambda n, k, c: (n, 0, 0)),
            scratch_shapes=[pltpu.VMEM((N_CHIPS, D_MODEL, W_BLK), F32), pltpu.SemaphoreType.DMA(()),
                            pltpu.SemaphoreType.DMA(())]),
        out_shape=SDS((N_CHIPS, D_MODEL, W_BLK), F32),
        compiler_params=_cparams(("arbitrary", "arbitrary")),
    )(core, u, dp, g_sib)


def _grad_x(dp, w_all, x, pre_w, dout, g_chip, seq):
    tm = 512
    nm = seq // tm

    def body(dp_ref, w_ref, x_ref, pw_ref, do_ref, gsrc_ref, gx_ref, gpw_ref, recv_ref,
             send_sems, recv_sems, local_sem):
        m = pl.program_id(0)
        ex = _ChipExchange(gsrc_ref, recv_ref, send_sems, recv_sems, local_sem)

        @pl.when(m == 0)
        def _():
            ex.start()
            gpw_ref[...] = jnp.zeros_like(gpw_ref)

        du = _dot(dp_ref[:, 0:W_BLK], w_ref[0], NT)
        for j in range(1, N_DEV):
            du = du + _dot(dp_ref[:, j * W_BLK:(j + 1) * W_BLK], w_ref[j], NT)
        xv = x_ref[...]
        r1 = lax.rsqrt(jnp.mean(xv * xv, axis=-1, keepdims=True) + EPS)
        xn = xv * r1
        gpw_ref[0:1, :] += jnp.sum(du * xn, axis=0, keepdims=True)
        dxn = du * pw_ref[...]
        gx_ref[...] = r1 * (dxn - xn * jnp.mean(dxn * xn, axis=-1, keepdims=True)) + do_ref[...]

        @pl.when(m == nm - 1)
        def _():
            ex.wait()

    row = lambda w: pl.BlockSpec((tm, w), lambda m: (m, 0))
    return pl.pallas_call(
        body, name="grad_x", grid=(nm,),
        in_specs=[row(D_IN), pl.BlockSpec((N_DEV, D_MODEL, W_BLK), lambda m: (0, 0, 0), pipeline_mode=pl.Buffered(1)),
                  row(D_MODEL),
                  pl.BlockSpec((1, D_MODEL), lambda m: (0, 0)), row(D_MODEL), ANY],
        out_specs=[row(D_MODEL), pl.BlockSpec((8, D_MODEL), lambda m: (0, 0)), ANY],
        out_shape=[SDS((seq, D_MODEL), F32), SDS((8, D_MODEL), F32), SDS(g_chip.shape, F32)],
        scratch_shapes=[pltpu.SemaphoreType.DMA((N_CHIPS - 1,)), pltpu.SemaphoreType.DMA((N_CHIPS - 1,)),
                        pltpu.SemaphoreType.DMA(())],
        compiler_params=_cparams(("arbitrary",)),
    )(dp, w_all, x, pre_w, dout, g_chip)


def _local_step(x, tgt, p, u, conv_w, conv_b, wa, wx, ba, bx, lam, lbl, gnorm_w, w_out, post_w):
    seq = x.shape[0]
    h, y_lru = _lru_forward(p, conv_w, conv_b, wa, wx, ba, bx, lam, seq)
    y_hgrn, o, states = _hgrn_forward(p, lbl, gnorm_w, seq)
    dymix, dout, g_w_out, stats = _out_proj(y_lru, y_hgrn, w_out, x, tgt, post_w, seq)
    dp_lru, g_wa, g_wx, ls, r_out = _lru_backward(p, h, dymix, conv_w, conv_b, wa, wx, ba, bx, lam,
                                                  g_w_out.reshape(N_DEV, D_MIX // N_DEV, D_MODEL), seq)
    g_small = _pack_small(_shard_rows(g_wa, LRU_BLOCKS), _shard_rows(g_wx, LRU_BLOCKS),
                          _shard_rows(ls[0:4].reshape(4, D_MODEL, 1), 4).reshape(N_DEV, 4, 128),
                          _shard_rows(ls[5].reshape(4, LRU_BW, 1), 4).reshape(N_DEV, 4, 32),
                          _shard_rows(ls[6].reshape(4, LRU_BW, 1), 4).reshape(N_DEV, 4, 32))
    dp, hgrn_small, r_small = _hgrn_backward(p, o, states, dymix, lbl, gnorm_w, dp_lru, g_small, seq)
    return dict(u=u, dp=dp, dout=dout, r_out=r_out, r_small=r_small,
                lru_small=ls, hgrn_small=hgrn_small, stats=stats)


class _TwoLevelGather:
    def __init__(self, ins, outs, send_sems, recv_sems, local_sems):
        self.ins, self.outs = ins, outs
        self.send_sems, self.recv_sems, self.local_sems = send_sems, recv_sems, local_sems
        x, y, c = _mesh_pos()
        self.c = c
        self.me, self.sibling = (x, y, c), (x, y, 1 - c)
        self.chips = [(1 - x, y), (x, 1 - y), (1 - x, 1 - y)]
        n = len(ins)
        self.mine = [pltpu.make_async_copy(ins[a], self._slot(a, self.me), local_sems.at[a]) for a in range(n)]
        self.first = []
        for a in range(n):
            self.first.append(self._copy(a, 0, self.me, self.sibling, src=ins[a]))
            self.first += [self._copy(a, 1 + j, self.me, (*chip, c), src=ins[a])
                           for j, chip in enumerate(self.chips)]
        self.passed = [self._copy(a, 4 + j, (*chip, c), self.sibling)
                       for j, chip in enumerate(self.chips) for a in range(n)]

    def _slot(self, a, pos):
        return self.outs[a].at[4 * pos[0] + 2 * pos[1] + pos[2]]

    def _copy(self, a, k, block, to, src=None):
        dst = self._slot(a, block)
        return pltpu.make_async_remote_copy(
            src_ref=dst if src is None else src, dst_ref=dst,
            send_sem=self.send_sems.at[a, k], recv_sem=self.recv_sems.at[a, k],
            device_id=to, device_id_type=MESH)

    def start(self):
        for cp in self.mine + self.first:
            cp.start()

    def forward(self):
        n = len(self.ins)
        for j, chip in enumerate(self.chips):
            for a in range(n):
                self._copy(a, 1 + j, (*chip, self.c), self.me).wait_recv()
                self.passed[j * n + a].start()

    def finish(self):
        for a in range(len(self.ins)):
            self._copy(a, 0, self.sibling, self.me).wait_recv()
            for j, chip in enumerate(self.chips):
                self._copy(a, 4 + j, (*chip, 1 - self.c), self.me).wait_recv()
        for cp in self.first + self.passed:
            cp.wait_send()
        for cp in self.mine:
            cp.wait()


W_IN_DIRECT = (1, 2, 4, 6)
W_IN_PASSED = (2, 4, 6)

def _in_proj_gather(x, pre_w, w_in_blk, w_out_blk, small_blk, me, seq):
    tm = min(1024, seq)
    nm = seq // tm
    last = N_DEV - 1

    def body(me_ref, x_ref, pw_ref, wblk_ref, woblk_ref, smblk_ref,
             p_ref, u_ref, wall_ref, woall_ref, small_ref,
             u_all, w_vmem, own_sem, d_send, d_recv, f_send, f_recv, wb_sems, g_send, g_recv, g_local):
        i = pl.program_id(0)
        m = pl.program_id(1)
        idx = me_ref[0]
        x_, y_, c_ = _mesh_pos()
        aux = _TwoLevelGather([woblk_ref, smblk_ref], [woall_ref, small_ref], g_send, g_recv, g_local)

        def peer(k):
            return (1 - x_ if (k >> 2) & 1 else x_, 1 - y_ if (k >> 1) & 1 else y_, 1 - c_ if k & 1 else c_)

        def direct(k):
            f = W_IN_DIRECT.index(k)
            return (pltpu.make_async_remote_copy(src_ref=wblk_ref, dst_ref=w_vmem.at[idx], send_sem=d_send.at[f],
                                                 recv_sem=d_recv.at[f], device_id=peer(k), device_id_type=MESH),
                    pltpu.make_async_remote_copy(src_ref=w_vmem.at[idx ^ k], dst_ref=w_vmem.at[idx ^ k],
                                                 send_sem=d_send.at[f], recv_sem=d_recv.at[f], device_id=peer(k),
                                                 device_id_type=MESH))

        def passed(k):
            f = W_IN_PASSED.index(k)
            return (pltpu.make_async_remote_copy(src_ref=w_vmem.at[idx ^ k], dst_ref=w_vmem.at[idx ^ k],
                                                 send_sem=f_send.at[f], recv_sem=f_recv.at[f], device_id=peer(1),
                                                 device_id_type=MESH),
                    pltpu.make_async_remote_copy(src_ref=w_vmem.at[idx ^ (k + 1)], dst_ref=w_vmem.at[idx ^ (k + 1)],
                                                 send_sem=f_send.at[f], recv_sem=f_recv.at[f], device_id=peer(1),
                                                 device_id_type=MESH))

        def write_back(k):
            return pltpu.make_async_copy(w_vmem.at[idx ^ k], wall_ref.at[idx ^ k], wb_sems.at[k])

        own = pltpu.make_async_copy(wblk_ref, w_vmem.at[idx], own_sem)

        @pl.when((i == 0) & (m == 0))
        def _():
            own.start()
            for k in (1, 2, 4):
                direct(k)[0].start()
            own.wait()
            write_back(0).start()

        for k in range(1, N_DEV):
            @pl.when((i == k) & (m == 0))
            def _(k=k):
                if k in W_IN_DIRECT:
                    direct(k)[1].wait_recv()
                    if k in W_IN_PASSED:
                        passed(k)[0].start()
                else:
                    passed(k - 1)[1].wait_recv()
                write_back(k).start()
                if k == 2:
                    direct(6)[0].start()
                if k == 4:
                    aux.start()
                if k == N_DEV - 1:
                    aux.forward()

        rows = pl.ds(pl.multiple_of(m * tm, tm), tm)

        @pl.when(i == 0)
        def _():
            xv = x_ref[...]
            r = lax.rsqrt(jnp.mean(xv * xv, axis=-1, keepdims=True) + EPS)
            ub = (xv * r * pw_ref[...]).astype(BF16)
            u_all[rows, :] = ub
            u_ref[...] = ub

        p_ref[...] = _dot(u_all[rows, :], w_vmem[idx ^ i])

        @pl.when((i == last) & (m == nm - 1))
        def _():
            for k in W_IN_DIRECT:
                direct(k)[0].wait_send()
            for k in W_IN_PASSED:
                passed(k)[0].wait_send()
            for k in range(N_DEV):
                write_back(k).wait()
            aux.finish()

    first_pass = lambda i, m: jnp.where(i == 0, m, nm - 1)
    return pl.pallas_call(
        body, name="in_proj_gather",
        grid_spec=pltpu.PrefetchScalarGridSpec(
            num_scalar_prefetch=1, grid=(N_DEV, nm),
            in_specs=[pl.BlockSpec((tm, D_MODEL), lambda i, m, me: (first_pass(i, m), 0)),
                      pl.BlockSpec((1, D_MODEL), lambda i, m, me: (0, 0)), ANY, ANY, ANY],
            out_specs=[pl.BlockSpec((tm, W_BLK), lambda i, m, me: (m, me[0] ^ i)),
                       pl.BlockSpec((tm, D_MODEL), lambda i, m, me: (first_pass(i, m), 0)), ANY, ANY, ANY],
            scratch_shapes=[pltpu.VMEM((seq, D_MODEL), BF16), pltpu.VMEM((N_DEV, D_MODEL, W_BLK), BF16),
                            pltpu.SemaphoreType.DMA(()),
                            pltpu.SemaphoreType.DMA((len(W_IN_DIRECT),)), pltpu.SemaphoreType.DMA((len(W_IN_DIRECT),)),
                            pltpu.SemaphoreType.DMA((len(W_IN_PASSED),)), pltpu.SemaphoreType.DMA((len(W_IN_PASSED),)),
                            pltpu.SemaphoreType.DMA((N_DEV,)),
                            pltpu.SemaphoreType.DMA((2, 7)), pltpu.SemaphoreType.DMA((2, 7)),
                            pltpu.SemaphoreType.DMA((2,))]),
        out_shape=[SDS((seq, D_IN), F32), SDS((seq, D_MODEL), BF16), SDS((N_DEV, D_MODEL, W_BLK), BF16),
                   SDS((N_DEV,) + w_out_blk.shape, w_out_blk.dtype), SDS((N_DEV,) + small_blk.shape, small_blk.dtype)],
        compiler_params=_cparams(("arbitrary", "arbitrary")),
    )(me, x, pre_w, w_in_blk, w_out_blk, small_blk)


def _exchange_grads(blocks, repl):
    nb = len(blocks)
    n = nb + 1

    def body(*refs):
        ins, outs, sems = refs[:n], refs[n:2 * n], refs[2 * n:]
        exs = [_SlotExchange(ins[a], outs[a], *sems[3 * a:3 * a + 3], blocked=a < nb) for a in range(n)]
        for ex in exs:
            ex.start()
        for ex in exs:
            ex.wait()

    arrs = list(blocks) + [repl]
    shapes = [SDS(b.shape, b.dtype) for b in blocks] + [SDS((N_DEV,) + repl.shape, repl.dtype)]
    return pl.pallas_call(
        body, name="exchange_small", out_shape=shapes,
        in_specs=[ANY] * n, out_specs=[ANY] * n,
        scratch_shapes=EXCHANGE_SEMS * n,
    )(*arrs)


def _pack_rows(picks, name):
    arrs = [p[0] for p in picks]

    def body(*refs):
        out = refs[-1]
        out[...] = jnp.zeros_like(out)
        at = 0
        for ref, (_, row, rows, scale) in zip(refs[:-1], picks):
            out[at:at + rows, :] = ref[row:row + rows, :] * scale
            at += rows

    return pl.pallas_call(body, name=name, out_shape=SDS((8, D_MODEL), F32))(*arrs)


def _adamw(g, w, m, v):
    m2 = ADAM_B1 * m + (1.0 - ADAM_B1) * g
    v2 = ADAM_B2 * v + (1.0 - ADAM_B2) * (g * g)
    m_hat = m2 / (1.0 - ADAM_B1 ** ADAM_STEP)
    v_hat = v2 / (1.0 - ADAM_B2 ** ADAM_STEP)
    delta = -ADAM_LR * (m_hat / (jnp.sqrt(v_hat) + ADAM_EPS) + ADAM_WD * w)
    return delta, m2, v2


def _sum_slots(r_ref):
    g = r_ref[0]
    for s in range(1, r_ref.shape[0]):
        g = g + r_ref[s]
    return g


def _sum_adamw(recv, w, m, v, tr, name):
    rows, cols = w.shape

    def body(r_ref, w_ref, m_ref, v_ref, g_ref, d_ref, m2_ref, v2_ref):
        g = _sum_slots(r_ref)
        g_ref[...] = g
        d_ref[...], m2_ref[...], v2_ref[...] = _adamw(g, w_ref[...], m_ref[...], v_ref[...])

    blk = pl.BlockSpec((tr, cols), lambda i: (i, 0))
    return pl.pallas_call(
        body, name=name, grid=(rows // tr,),
        in_specs=[pl.BlockSpec((recv.shape[0], tr, cols), lambda i: (0, i, 0)), blk, blk, blk],
        out_specs=[blk] * 4, out_shape=[SDS((rows, cols), F32)] * 4,
        compiler_params=_cparams(("parallel",)),
    )(recv, w, m, v)


def _sum_adamw_pieces(recv, rows, ws, ms, vs, name, loss_row=None):
    n = len(ws)

    def body(r_ref, *refs):
        w_refs, m_refs, v_refs, outs = refs[:n], refs[n:2 * n], refs[2 * n:3 * n], refs[3 * n:]
        g = _sum_slots(r_ref)
        for i, (row, nrows) in enumerate(rows):
            gi = g[row:row + nrows, 0:ws[i].shape[1]]
            outs[i][...] = gi
            outs[n + i][...], outs[2 * n + i][...], outs[3 * n + i][...] = _adamw(
                gi, w_refs[i][...], m_refs[i][...], v_refs[i][...])
        if loss_row is not None:
            total = jnp.sum(g[loss_row:loss_row + 1, :], axis=-1, keepdims=True)
            outs[4 * n][...] = jnp.broadcast_to(total, outs[4 * n].shape)

    shapes = [SDS(w.shape, F32) for w in ws] * 4 + ([SDS((8, 128), F32)] if loss_row is not None else [])
    out = pl.pallas_call(body, name=name, out_shape=shapes)(recv, *ws, *ms, *vs)
    return [out[k * n:(k + 1) * n] for k in range(4)] + list(out[4 * n:])


def _shard_rows(t, lead):
    r = t.shape[1] // N_DEV
    t = t.reshape((lead, N_DEV, r) + t.shape[2:])
    return jnp.moveaxis(t, 1, 0)


def _pad_tile(t):
    return jnp.pad(t, ((0, 0), (0, 8 - t.shape[1]), (0, SM_LANES - t.shape[2])))


def _pack_small(wa, wx, cw, b_a, b_x):
    n = wa.shape[0]
    return jnp.concatenate([wa.reshape(n, 128, SM_LANES), wx.reshape(n, 128, SM_LANES),
                            _pad_tile(cw), _pad_tile(b_a), _pad_tile(b_x)], axis=1)


def _unpack_small(t):
    n = t.shape[0]
    return (t[:, SM_WA:SM_WA + 128].reshape(n, 4, 32, 256), t[:, SM_WX:SM_WX + 128].reshape(n, 4, 32, 256),
            t[:, SM_CW:SM_CW + 4, 0:128], t[:, SM_BA:SM_BA + 4, 0:32], t[:, SM_BX:SM_BX + 4, 0:32])


def kernel(x, pre_norm_w, w_in, conv_w, conv_b, lru_w_a, lru_b_a, lru_w_x, lru_b_x, lru_lambda, hgrn_lb_logits, hgrn_gnorm_w, w_out, post_norm_w, loss_target, m_pre_norm_w, m_w_in, m_conv_w, m_conv_b, m_lru_w_a, m_lru_b_a, m_lru_w_x, m_lru_b_x, m_lru_lambda, m_hgrn_lb_logits, m_hgrn_gnorm_w, m_w_out, m_post_norm_w, v_pre_norm_w, v_w_in, v_conv_w, v_conv_b, v_lru_w_a, v_lru_b_a, v_lru_w_x, v_lru_b_x, v_lru_lambda, v_hgrn_lb_logits, v_hgrn_gnorm_w, v_w_out, v_post_norm_w):
    seq = x.shape[1]
    x2 = x.reshape(seq, D_MODEL)
    tgt = loss_target.reshape(seq, D_MODEL)

    small_w = _pack_small(lru_w_a, lru_w_x, conv_w, lru_b_a, lru_b_x)[0]
    me = (4 * lax.axis_index("x") + 2 * lax.axis_index("y") + lax.axis_index("c")).astype(jnp.int32).reshape(1)
    p, u, w_in_all, w_out_all, small_all = _in_proj_gather(
        x2, pre_norm_w, w_in[0].astype(BF16), w_out[0].astype(BF16), small_w, me, seq)
    wa_s, wx_s, cw_s, ba_s, bx_s = _unpack_small(small_all)
    wa = jnp.moveaxis(wa_s, 0, 1).reshape(LRU_BLOCKS, LRU_BW, LRU_BW).astype(BF16)
    wx = jnp.moveaxis(wx_s, 0, 1).reshape(LRU_BLOCKS, LRU_BW, LRU_BW).astype(BF16)
    cw = jnp.moveaxis(cw_s, 0, 1).reshape(4, D_MODEL)
    ba = jnp.moveaxis(ba_s, 0, 1).reshape(1, D_MODEL)
    bx = jnp.moveaxis(bx_s, 0, 1).reshape(1, D_MODEL)

    loc = _local_step(x2, tgt, p, u, cw, conv_b, wa, wx, ba, bx, lru_lambda,
                      hgrn_lb_logits, hgrn_gnorm_w, w_out_all.reshape(D_MIX, D_MODEL), post_norm_w)

    ls, r_out, r_small = loc["lru_small"], loc["r_out"], loc["r_small"]
    core = lax.axis_index("c").astype(jnp.int32).reshape(1)
    g_sib = _grad_w_in_sibling(loc["u"], loc["dp"], core, seq)
    g_chip = _grad_w_in_own(loc["u"], loc["dp"], core, g_sib, seq)
    grad_x, pre_small, r_in = _grad_x(loc["dp"], w_in_all, x2, pre_norm_w, loc["dout"], g_chip, seq)
    g_repl = _pack_rows([(pre_small, 0, 1, 1.0), (ls, 4, 1, 1.0), (ls, 7, 1, 1.0),
                         (loc["hgrn_small"], 0, 1, 1.0), (loc["hgrn_small"], 0, 1, -1.0),
                         (loc["hgrn_small"], 1, 1, 1.0), (loc["stats"], 0, 2, 1.0)], "pack_grads")
    (r_repl,) = _exchange_grads([], g_repl)

    repl_rows = [(RP_PRE, 1), (RP_CB, 1), (RP_LAM, 1), (RP_LB0, 2), (RP_GN, 1), (RP_POST, 1)]
    o_repl = _sum_adamw_pieces(
        r_repl, repl_rows,
        [pre_norm_w, conv_b, lru_lambda, hgrn_lb_logits, hgrn_gnorm_w, post_norm_w],
        [m_pre_norm_w, m_conv_b, m_lru_lambda, m_hgrn_lb_logits, m_hgrn_gnorm_w, m_post_norm_w],
        [v_pre_norm_w, v_conv_b, v_lru_lambda, v_hgrn_lb_logits, v_hgrn_gnorm_w, v_post_norm_w],
        "adamw_repl", loss_row=RP_LOSS)
    loss = o_repl[4][0, 0]

    small_rows = [(SM_WA, 128), (SM_WX, 128), (SM_CW, 4), (SM_BA, 4), (SM_BX, 4)]
    as_rows = lambda wa_, wx_, cw_, ba_, bx_: [wa_.reshape(128, 256), wx_.reshape(128, 256), cw_.reshape(4, 128),
                                               ba_.reshape(4, 32), bx_.reshape(4, 32)]
    o_small = _sum_adamw_pieces(
        r_small, small_rows, as_rows(lru_w_a, lru_w_x, conv_w, lru_b_a, lru_b_x),
        as_rows(m_lru_w_a, m_lru_w_x, m_conv_w, m_lru_b_a, m_lru_b_x),
        as_rows(v_lru_w_a, v_lru_w_x, v_conv_w, v_lru_b_a, v_lru_b_x), "adamw_small")

    o_in = _sum_adamw(r_in, w_in[0], m_w_in[0], v_w_in[0], 128, "adamw_w_in")
    o_out = _sum_adamw(r_out, w_out[0], m_w_out[0], v_w_out[0], 64, "adamw_w_out")

    outs = [loss, grad_x.reshape(x.shape)]
    for kind in range(4):
        pre, cb, lam, lb, gn, post = o_repl[kind]
        swa, swx, scw, sba, sbx = o_small[kind]
        outs += [pre, o_in[kind][None], scw.reshape(conv_w.shape), cb, swa.reshape(lru_w_a.shape),
                 sba.reshape(lru_b_a.shape), swx.reshape(lru_w_x.shape), sbx.reshape(lru_b_x.shape),
                 lam, lb, gn, o_out[kind][None], post]
    return tuple(outs)
```

```python
import jax
import jax.numpy as jnp
from jax import lax
from jax.experimental import pallas as pl
from jax.experimental.pallas import tpu as pltpu

F32 = jnp.float32
BF16 = jnp.bfloat16
SDS = jax.ShapeDtypeStruct

D_MODEL = 1024
D_IN = 6144
N_DEV = 8
N_CHIPS = 4
W_BLK = D_IN // N_DEV
D_MIX = 2048
LRU_BLOCKS = 4
LRU_BW = 256
LRU_C = 8.0
N_HEADS = 8
HEAD_D = 128
CHUNK = 128
SUB = 32
N_SUB = CHUNK // SUB
HGRN_FWD_STEP_CHUNKS = 4
HGRN_STEP_CHUNKS = 2
EXP_CLAMP = 80.0
EPS = 1e-6

ADAM_LR = 0.001
ADAM_B1 = 0.9
ADAM_B2 = 0.999
ADAM_EPS = 1e-08
ADAM_WD = 0.01
ADAM_STEP = 10

VMEM_LIMIT = 56 * 1024 * 1024

NN = (((1,), (0,)), ((), ()))
NT = (((1,), (1,)), ((), ()))
TN = (((0,), (0,)), ((), ()))

SM_LANES = 256
SM_WA = 0
SM_WX = 128
SM_CW = 256
SM_BA = 264
SM_BX = 272
SM_ROWS = 280

RP_PRE, RP_CB, RP_LAM, RP_LB0, RP_LB1, RP_GN, RP_POST, RP_LOSS = range(8)


def _dot(a, b, dims=NN):
    return lax.dot_general(a, b, dims, preferred_element_type=F32)


def _sigmoid(x):
    return 0.5 * jnp.tanh(0.5 * x) + 0.5


def _sigmoid_pos(x):
    return 1.0 / (1.0 + jnp.exp(-x))


def _cparams(sem, vmem=VMEM_LIMIT):
    return pltpu.CompilerParams(dimension_semantics=sem, vmem_limit_bytes=vmem)


def _iota(shape, axis):
    return lax.broadcasted_iota(jnp.int32, shape, axis)


def _softplus_neg(lam):
    z = -lam
    e = jnp.exp(-jnp.abs(z))
    u = 1.0 + e
    log1p_e = jnp.where(u == 1.0, e, jnp.log(u) * (e / (u - 1.0)))
    sp = jnp.maximum(z, 0.0) + log1p_e
    dsp = -jnp.where(z >= 0.0, 1.0 / u, e / u)
    return sp, dsp


def _neg_expm1(x):
    poly = x * (1.0 + x * (1.0 / 2 + x * (1.0 / 6 + x * (1.0 / 24 + x * (1.0 / 120)))))
    return jnp.where(x > -1.0 / 16, -poly, 1.0 - jnp.exp(x))


def _conv_taps(lx, prev8, cw_ref, cb_ref, tile):
    xc = cb_ref[...] + cw_ref[3:4, :] * lx
    for j in (1, 2, 3):
        xc = xc + cw_ref[3 - j:4 - j, :] * pltpu.roll(lx, j, 0)
    row8 = _iota((8, D_MODEL), 0)
    last8 = lx[tile - 8:tile, :]
    fix = jnp.zeros((8, D_MODEL), F32)
    for j in (1, 2, 3):
        wrong = pltpu.roll(last8, j, 0)
        right = pltpu.roll(prev8, j, 0)
        fix = fix + cw_ref[3 - j:4 - j, :] * jnp.where(row8 < j, right - wrong, 0.0)
    return xc, fix


def _lru_gates(xcs, wa, wx, ba, bx, sp):
    xb = xcs.astype(BF16)
    r = _sigmoid_pos(_dot(xb, wa) + ba)
    i = _sigmoid(_dot(xb, wx) + bx)
    la = (-LRU_C * sp) * r
    a = jnp.exp(la)
    one_minus_a2 = _neg_expm1(2.0 * la)
    return r, i, a, one_minus_a2


def _lru_forward(p, conv_w, conv_b, wa, wx, ba, bx, lam, seq):
    tile = min(1024, seq // 2)
    nblk = tile // 8

    def body(lx_ref, gt_ref, cw_ref, cb_ref, wa_ref, wx_ref, ba_ref, bx_ref, lam_ref,
             h_ref, y_ref, ext, hcar, xc_scr, a_scr, u_scr):
        @pl.when(pl.program_id(0) == 0)
        def _():
            ext[0:8, :] = jnp.zeros((8, D_MODEL), F32)
            hcar[...] = jnp.zeros_like(hcar)

        lx = lx_ref[...]
        ext[8:8 + tile, :] = lx
        xc = cb_ref[...] + cw_ref[3:4, :] * lx
        for j in (1, 2, 3):
            xc = xc + cw_ref[3 - j:4 - j, :] * ext[8 - j:8 - j + tile, :]
        xc_scr[...] = xc
        ext[0:8, :] = lx_ref[tile - 8:tile, :]
        sp, _ = _softplus_neg(lam_ref[...])
        for n in range(LRU_BLOCKS):
            sl = slice(n * LRU_BW, (n + 1) * LRU_BW)
            xcs = xc_scr[:, sl]
            _, i, a, ne = _lru_gates(xcs, wa_ref[n], wx_ref[n], ba_ref[:, sl], bx_ref[:, sl], sp[:, sl])
            a_scr[:, sl] = a
            u_scr[:, sl] = jnp.sqrt(ne) * (i * xcs)

        row8 = _iota((8, D_MODEL), 0)

        def blk(j, hc):
            off = pl.multiple_of(j * 8, 8)
            a = a_scr[pl.ds(off, 8), :]
            u = u_scr[pl.ds(off, 8), :]
            for k in (1, 2, 4):
                m = row8 >= k
                u = jnp.where(m, u + a * pltpu.roll(u, k, 0), u)
                a = jnp.where(m, a * pltpu.roll(a, k, 0), a)
            h = u + a * hc
            h_ref[pl.ds(off, 8), :] = h
            return jnp.broadcast_to(h[7:8, :], (8, D_MODEL))

        hcar[...] = lax.fori_loop(0, nblk, blk, hcar[...])
        g = gt_ref[...]
        y_ref[...] = (h_ref[...] * (g * _sigmoid(g))).astype(BF16)

    full = lambda shape: pl.BlockSpec(shape, lambda t: (0,) * len(shape))
    return pl.pallas_call(
        body, name="lru_fwd", grid=(seq // tile,),
        in_specs=[pl.BlockSpec((tile, D_MODEL), lambda t: (t, 0)),
                  pl.BlockSpec((tile, D_MODEL), lambda t: (t, 1)),
                  full((4, D_MODEL)), full((1, D_MODEL)),
                  full((LRU_BLOCKS, LRU_BW, LRU_BW)), full((LRU_BLOCKS, LRU_BW, LRU_BW)),
                  full((1, D_MODEL)), full((1, D_MODEL)), full((1, D_MODEL))],
        out_specs=[pl.BlockSpec((tile, D_MODEL), lambda t: (t, 0)),
                   pl.BlockSpec((tile, D_MODEL), lambda t: (t, 0))],
        out_shape=[SDS((seq, D_MODEL), F32), SDS((seq, D_MODEL), BF16)],
        scratch_shapes=[pltpu.VMEM((tile + 8, D_MODEL), F32), pltpu.VMEM((8, D_MODEL), F32),
                        pltpu.VMEM((tile, D_MODEL), F32), pltpu.VMEM((tile, D_MODEL), F32),
                        pltpu.VMEM((tile, D_MODEL), F32)],
        compiler_params=_cparams(("arbitrary",)),
    )(p, p, conv_w, conv_b, wa, wx, ba, bx, lam)


def _lru_backward(p, h, dymix, conv_w, conv_b, wa, wx, ba, bx, lam, seq):
    tile = min(512, seq // 2)
    nt = seq // tile
    nblk = tile // 8
    t8 = tile // 8

    def body(lx_ref, lxh_ref, gt_ref, h_ref, hh_ref, dy_ref, cw_ref, cb_ref, wa_ref, wx_ref, ba_ref,
             bx_ref, lam_ref, dp_ref, gwa_ref, gwx_ref, gsm_ref,
             lamcar, anext, dxc8, xc_scr, r_scr, i_scr, a_scr, m_scr, rm_scr, c_scr, l_scr, dxc_scr):
        step = pl.program_id(0)
        first_tile = step == nt - 1

        @pl.when(step == 0)
        def _():
            lamcar[...] = jnp.zeros_like(lamcar)
            anext[...] = jnp.zeros_like(anext)
            dxc8[...] = jnp.zeros_like(dxc8)
            gwa_ref[...] = jnp.zeros_like(gwa_ref)
            gwx_ref[...] = jnp.zeros_like(gwx_ref)
            gsm_ref[...] = jnp.zeros_like(gsm_ref)

        keep = jnp.where(first_tile, 0.0, 1.0)
        lx = lx_ref[...]
        prev8 = lxh_ref[...] * keep
        xc, fix = _conv_taps(lx, prev8, cw_ref, cb_ref, tile)
        xc_scr[...] = xc
        xc_scr[0:8, :] = xc_scr[0:8, :] + fix
        sp, dsp = _softplus_neg(lam_ref[...])
        for n in range(LRU_BLOCKS):
            sl = slice(n * LRU_BW, (n + 1) * LRU_BW)
            r, i, a, ne = _lru_gates(xc_scr[:, sl], wa_ref[n], wx_ref[n], ba_ref[:, sl], bx_ref[:, sl],
                                     sp[:, sl])
            r_scr[:, sl] = r
            i_scr[:, sl] = i
            a_scr[:, sl] = a
            m_scr[:, sl] = jnp.sqrt(ne)
            rm_scr[:, sl] = lax.rsqrt(ne)

        g = gt_ref[...]
        sg = _sigmoid(g)
        dy = dy_ref[...]
        hv = h_ref[...]
        dp_ref[:, D_MODEL:2 * D_MODEL] = (dy * hv * (sg * (1.0 + g * (1.0 - sg)))).astype(BF16)

        rowt = _iota((tile, D_MODEL), 0)
        av = a_scr[...]
        l_scr[...] = dy * (g * sg)
        c_scr[...] = jnp.where(rowt == tile - 1, anext[...][0:1, :], pltpu.roll(av, tile - 1, 0))
        anext[...] = jnp.broadcast_to(av[0:1, :], (8, D_MODEL))
        row8 = _iota((8, D_MODEL), 0)

        def blk(jj, lc):
            off = pl.multiple_of((nblk - 1 - jj) * 8, 8)
            c = c_scr[pl.ds(off, 8), :]
            u = l_scr[pl.ds(off, 8), :]
            for k in (1, 2, 4):
                m = row8 < 8 - k
                u = jnp.where(m, u + c * pltpu.roll(u, 8 - k, 0), u)
                c = jnp.where(m, c * pltpu.roll(c, 8 - k, 0), c)
            lamv = u + c * lc
            l_scr[pl.ds(off, 8), :] = lamv
            return jnp.broadcast_to(lamv[0:1, :], (8, D_MODEL))

        lamcar[...] = lax.fori_loop(0, nblk, blk, lamcar[...])

        hprev = jnp.where(rowt == 0, hh_ref[...][7:8, :] * keep, pltpu.roll(hv, 1, 0))
        for n in range(LRU_BLOCKS):
            sl = slice(n * LRU_BW, (n + 1) * LRU_BW)
            lamv = l_scr[:, sl]
            xcs = xc_scr[:, sl]
            r = r_scr[:, sl]
            i = i_scr[:, sl]
            a = a_scr[:, sl]
            mult = m_scr[:, sl]
            d_la = lamv * hprev[:, sl] * a - (lamv * i * xcs) * (a * a * rm_scr[:, sl])
            d_pr = d_la * (-LRU_C * sp[:, sl]) * r * (1.0 - r)
            d_pi = (lamv * mult * xcs) * i * (1.0 - i)
            gsm_ref[7:8, sl] += jnp.sum(d_la * r, axis=0, keepdims=True) * (-LRU_C) * dsp[:, sl]
            gsm_ref[5:6, sl] += jnp.sum(d_pr, axis=0, keepdims=True)
            gsm_ref[6:7, sl] += jnp.sum(d_pi, axis=0, keepdims=True)
            xb = xcs.astype(BF16)
            prb = d_pr.astype(BF16)
            pib = d_pi.astype(BF16)
            gwa_ref[n] += _dot(xb, prb, TN)
            gwx_ref[n] += _dot(xb, pib, TN)
            dxc_scr[:, sl] = lamv * mult * i + _dot(prb, wa_ref[n], NT) + _dot(pib, wx_ref[n], NT)

        dxc = dxc_scr[...]
        gsm_ref[4:5, :] += jnp.sum(dxc, axis=0, keepdims=True)
        last8 = lx[tile - 8:tile, :]
        first8 = dxc[0:8, :]
        dlx = cw_ref[3:4, :] * dxc
        gsm_ref[3:4, :] += jnp.sum(dxc * lx, axis=0, keepdims=True)
        fix = jnp.zeros((8, D_MODEL), F32)
        for j in (1, 2, 3):
            w = cw_ref[3 - j:4 - j, :]
            dlx = dlx + w * pltpu.roll(dxc, tile - j, 0)
            fix = fix + w * jnp.where(row8 + j >= 8,
                                      pltpu.roll(dxc8[...], 8 - j, 0) - pltpu.roll(first8, 8 - j, 0), 0.0)
            halo = jnp.where(row8 < j, pltpu.roll(prev8, j, 0) - pltpu.roll(last8, j, 0), 0.0)
            gsm_ref[3 - j:4 - j, :] += (jnp.sum(dxc * pltpu.roll(lx, j, 0), axis=0, keepdims=True)
                                        + jnp.sum(first8 * halo, axis=0, keepdims=True))
        dxc8[...] = first8
        dp_ref[:, 0:D_MODEL] = dlx.astype(BF16)
        top = tile - 8
        dp_ref[top:tile, 0:D_MODEL] = (dlx[top:tile, :] + fix).astype(BF16)

    rev = lambda t: (nt - 1 - t, 0)
    halo_idx = lambda t: (jnp.maximum((nt - 1 - t) * t8 - 1, 0), 0)
    full = lambda shape: pl.BlockSpec(shape, lambda t: (0,) * len(shape))
    big = lambda: pltpu.VMEM((tile, D_MODEL), F32)
    return pl.pallas_call(
        body, name="lru_bwd", grid=(nt,),
        in_specs=[pl.BlockSpec((tile, D_MODEL), rev),
                  pl.BlockSpec((8, D_MODEL), halo_idx),
                  pl.BlockSpec((tile, D_MODEL), lambda t: (nt - 1 - t, 1)),
                  pl.BlockSpec((tile, D_MODEL), rev),
                  pl.BlockSpec((8, D_MODEL), halo_idx),
                  pl.BlockSpec((tile, D_MODEL), rev),
                  full((4, D_MODEL)), full((1, D_MODEL)),
                  full((LRU_BLOCKS, LRU_BW, LRU_BW)), full((LRU_BLOCKS, LRU_BW, LRU_BW)),
                  full((1, D_MODEL)), full((1, D_MODEL)), full((1, D_MODEL))],
        out_specs=[pl.BlockSpec((tile, 2 * D_MODEL), rev),
                   full((LRU_BLOCKS, LRU_BW, LRU_BW)), full((LRU_BLOCKS, LRU_BW, LRU_BW)),
                   full((8, D_MODEL))],
        out_shape=[SDS((seq, D_IN), BF16), SDS((LRU_BLOCKS, LRU_BW, LRU_BW), F32),
                   SDS((LRU_BLOCKS, LRU_BW, LRU_BW), F32), SDS((8, D_MODEL), F32)],
        scratch_shapes=[pltpu.VMEM((8, D_MODEL), F32), pltpu.VMEM((8, D_MODEL), F32),
                        pltpu.VMEM((8, D_MODEL), F32)] + [big() for _ in range(9)],
        compiler_params=_cparams(("arbitrary",)),
    )(p, p, p, h, h, dymix, conv_w, conv_b, wa, wx, ba, bx, lam)


def _tri_matmul(tri, g):
    hi = g.astype(BF16)
    lo = (g - hi.astype(F32)).astype(BF16)
    return _dot(tri, lo) + _dot(tri, hi)


def _hgrn_gate_terms(q, fr, lbl):
    lb = _sigmoid_pos(lbl[0:1, :] - lbl[1:2, :])
    half = 0.5 * (1.0 - lb)
    tf = jnp.tanh(0.5 * fr)
    f = (lb + half) + half * tf
    hq = 0.5 * q
    tq = jnp.tanh(hq)
    return lb, tf, f, tq, hq * tq + hq


def _hgrn_decay(bh):
    zero = jnp.zeros((1, bh.shape[1]), F32)
    rho = [zero] + [bh[s * SUB - 1:s * SUB, :] for s in range(1, N_SUB + 1)]
    start = _sub_rows(rho[0:N_SUB])
    end = _sub_rows(rho[1:N_SUB + 1])
    mid = 0.5 * (start + end)
    blast = rho[N_SUB]
    e_on = jnp.exp(bh - start)
    e_off = jnp.exp(end - bh)
    scales = [_sub_rows([jnp.exp(rho[i] - rho[j + 1]) if i > j else zero for i in range(N_SUB)])
              for j in range(N_SUB - 1)]
    return dict(eq0=jnp.exp(jnp.minimum(bh - mid, EXP_CLAMP)), ek0=jnp.exp(jnp.minimum(mid - bh, EXP_CLAMP)),
                e_on=e_on, e_off=e_off, scales=scales,
                eb=e_on * _sub_rows([jnp.exp(r) for r in rho[0:N_SUB]]),
                ekst=e_off * _sub_rows([jnp.exp(blast - r) for r in rho[1:N_SUB + 1]]),
                ebl=jnp.exp(blast))


def _sub_rows(vecs):
    return jnp.concatenate([jnp.broadcast_to(v, (SUB, v.shape[1])) for v in vecs], axis=0)


def _hgrn_operands(qs, k, dec, qt_scr, kt_scr):
    sub = jnp.right_shift(_iota(qs.shape, 0), 5)
    qon = qs * dec["e_on"]
    koff = k * dec["e_off"]
    qt_scr[0] = (qs * dec["eq0"]).astype(BF16)
    kt_scr[0] = (k * dec["ek0"]).astype(BF16)
    for j in range(N_SUB - 1):
        qt_scr[j + 1] = (qon * dec["scales"][j]).astype(BF16)
        kt_scr[j + 1] = jnp.where(sub == j, koff, 0.0).astype(BF16)
    return koff


def _hgrn_head_scores(qt_scr, kt_scr, sl, diag):
    a = jnp.where(diag, _dot(qt_scr[0, :, sl], kt_scr[0, :, sl], NT), 0.0)
    for j in range(1, N_SUB):
        a = a + _dot(qt_scr[j, :, sl], kt_scr[j, :, sl], NT)
    return a


def _hgrn_forward(p, lbl, gw, seq):
    nc = seq // CHUNK
    assert SUB == 32

    def body(q_ref, f_ref, v_ref, hg_ref, lbl_ref, gw_ref, y_ref, o_ref, st_ref,
             s_scr, qt_scr, kt_scr, qin_scr, kst_scr, vb_scr, a_scr):
        @pl.when(pl.program_id(0) == 0)
        def _():
            s_scr[...] = jnp.zeros_like(s_scr)

        r = _iota((CHUNK, CHUNK), 0)
        c = _iota((CHUNK, CHUNK), 1)
        tri = jnp.where(c <= r, 1.0, 0.0).astype(BF16)
        diag = (jnp.right_shift(r, 5) == jnp.right_shift(c, 5)) & (c <= r)
        heads = [slice(h * HEAD_D, (h + 1) * HEAD_D) for h in range(N_HEADS)]
        for cc in range(HGRN_FWD_STEP_CHUNKS):
            rows = slice(cc * CHUNK, (cc + 1) * CHUNK)
            q = q_ref[rows, :]
            _, _, f, _, qs = _hgrn_gate_terms(q, f_ref[rows, :], lbl_ref[...])
            k = 1.0 - f
            dec = _hgrn_decay(_tri_matmul(tri, jnp.log(f)))
            _hgrn_operands(qs, k, dec, qt_scr, kt_scr)
            qin_scr[...] = (qs * dec["eb"]).astype(BF16)
            kst_scr[...] = (k * dec["ekst"]).astype(BF16)
            vb_scr[...] = v_ref[rows, :].astype(BF16)
            ebl = dec["ebl"]
            hg = hg_ref[rows, :]
            gate = gw_ref[...] * (hg * _sigmoid(hg))
            stb = []
            for h, sl in enumerate(heads):
                st = s_scr[h]
                st_ref[cc, h] = st
                stb.append(st.astype(BF16))
                s_scr[h] = st * ebl[:, sl] + _dot(vb_scr[:, sl], kst_scr[:, sl], TN)
            for h, sl in enumerate(heads):
                a_scr[h] = _hgrn_head_scores(qt_scr, kt_scr, sl, diag).astype(BF16)
            for h, sl in enumerate(heads):
                o = _dot(a_scr[h], vb_scr[:, sl]) + _dot(qin_scr[:, sl], stb[h], NT)
                o_ref[rows, sl] = o
                rs = lax.rsqrt(jnp.mean(o * o, axis=-1, keepdims=True) + EPS)
                y_ref[rows, sl] = ((o * rs) * gate[:, sl]).astype(BF16)

    col = lambda j: pl.BlockSpec((HGRN_FWD_STEP_CHUNKS * CHUNK, D_MODEL), lambda c: (c, j))
    par = lambda rows: pl.BlockSpec((rows, D_MODEL), lambda c: (0, 0))
    return pl.pallas_call(
        body, name="hgrn_fwd", grid=(nc // HGRN_FWD_STEP_CHUNKS,),
        in_specs=[col(2), col(3), col(4), col(5), par(2), par(1)],
        out_specs=[col(0), col(0),
                   pl.BlockSpec((HGRN_FWD_STEP_CHUNKS, N_HEADS, HEAD_D, HEAD_D), lambda c: (c, 0, 0, 0))],
        out_shape=[SDS((seq, D_MODEL), BF16), SDS((seq, D_MODEL), F32),
                   SDS((nc, N_HEADS, HEAD_D, HEAD_D), F32)],
        scratch_shapes=[pltpu.VMEM((N_HEADS, HEAD_D, HEAD_D), F32),
                        pltpu.VMEM((N_SUB, CHUNK, D_MODEL), BF16), pltpu.VMEM((N_SUB, CHUNK, D_MODEL), BF16)]
                       + [pltpu.VMEM((CHUNK, D_MODEL), BF16)] * 3 + [pltpu.VMEM((N_HEADS, CHUNK, CHUNK), BF16)],
        compiler_params=_cparams(("arbitrary",)),
    )(p, p, p, p, lbl, gw)


def _hgrn_backward(p, o, states, dymix, lbl, gw, dp_full, g_w_out, g_small, seq):
    step_rows = HGRN_STEP_CHUNKS * CHUNK
    ns = seq // step_rows

    def body(q_ref, f_ref, v_ref, hg_ref, o_ref, st_ref, dy_ref, lbl_ref, gw_ref, dpin_ref, go_ref, gs_ref,
             dpo_ref, gsm_ref, ro_ref, rs_ref, ds_scr, dp_buf, dp_sems, *rest):
        del dpin_ref
        scratch, sems = rest[:14], rest[14:]
        step = pl.program_id(0)
        slot = step % 2
        exs = [_SlotExchange(go_ref, ro_ref, *sems[0:3], blocked=True),
               _SlotExchange(gs_ref, rs_ref, *sems[3:6], blocked=True)]

        @pl.when(step == 0)
        def _():
            for ex in exs:
                ex.start()

        def out_copy(s, blk):
            rows = pl.ds(pl.multiple_of(blk * step_rows, step_rows), step_rows)
            return pltpu.make_async_copy(dp_buf.at[s], dpo_ref.at[rows, pl.ds(2 * D_MODEL, 4 * D_MODEL)],
                                         dp_sems.at[s])

        @pl.when(step == 0)
        def _():
            ds_scr[...] = jnp.zeros_like(ds_scr)
            gsm_ref[...] = jnp.zeros_like(gsm_ref)

        @pl.when(step >= 2)
        def _():
            out_copy(slot, ns + 1 - step).wait()

        for cc in reversed(range(HGRN_STEP_CHUNKS)):
            chunk(cc, q_ref, f_ref, v_ref, hg_ref, o_ref, st_ref, dy_ref, lbl_ref, gw_ref, gsm_ref, ds_scr,
                  dp_buf.at[slot], *scratch)

        out_copy(slot, ns - 1 - step).start()

        @pl.when(step == ns - 1)
        def _():
            out_copy(1 - slot, 1).wait()
            out_copy(slot, 0).wait()
            for ex in exs:
                ex.wait()

    def chunk(cc, q_ref, f_ref, v_ref, hg_ref, o_ref, st_ref, dy_ref, lbl_ref, gw_ref, gsm_ref, ds_scr, dp_ref,
              qt_scr, kt_scr, qin_scr, kst_scr, vb_scr, dob_scr, g_scr, h_scr, dqi_scr, dks_scr, sd_scr,
              a_scr, da_scr, da0_scr):
        rows = slice(cc * CHUNK, (cc + 1) * CHUNK)
        r = _iota((CHUNK, CHUNK), 0)
        c = _iota((CHUNK, CHUNK), 1)
        tri = jnp.where(c <= r, 1.0, 0.0).astype(BF16)
        triu = jnp.where(c >= r, 1.0, 0.0).astype(BF16)
        diag = (jnp.right_shift(r, 5) == jnp.right_shift(c, 5)) & (c <= r)
        row = _iota((CHUNK, D_MODEL), 0)
        sub = jnp.right_shift(row, 5)

        q = q_ref[rows, :]
        lb, tf, f, tq, qs = _hgrn_gate_terms(q, f_ref[rows, :], lbl_ref[...])
        sig = 0.5 * tf + 0.5
        sq = 0.5 * tq + 0.5
        k = 1.0 - f
        dec = _hgrn_decay(_tri_matmul(tri, jnp.log(f)))
        eb, ekst, ebl = dec["eb"], dec["ekst"], dec["ebl"]
        koff = _hgrn_operands(qs, k, dec, qt_scr, kt_scr)
        qin_scr[...] = (qs * eb).astype(BF16)
        kst_scr[...] = (k * ekst).astype(BF16)
        vb_scr[...] = v_ref[rows, :].astype(BF16)
        hg = hg_ref[rows, :]
        sh = _sigmoid(hg)
        dy = dy_ref[rows, :]
        gwv = gw_ref[...]
        d_onw = dy * (hg * sh)
        d_on = d_onw * gwv
        d_gate = dy * gwv * (sh * (1.0 + hg * (1.0 - sh)))

        heads = [slice(h * HEAD_D, (h + 1) * HEAD_D) for h in range(N_HEADS)]
        for h, sl in enumerate(heads):
            o = o_ref[rows, sl]
            rs = lax.rsqrt(jnp.mean(o * o, axis=-1, keepdims=True) + EPS)
            on = o * rs
            dp_ref[rows, 3 * D_MODEL + h * HEAD_D:3 * D_MODEL + (h + 1) * HEAD_D] = (d_gate[:, sl] * on).astype(BF16)
            gsm_ref[1:2, sl] += jnp.sum(d_onw[:, sl] * on, axis=0, keepdims=True)
            d_onh = d_on[:, sl]
            dob_scr[:, sl] = (rs * (d_onh - on * jnp.mean(d_onh * on, axis=-1, keepdims=True))).astype(BF16)
        for h, sl in enumerate(heads):
            a_scr[h] = _hgrn_head_scores(qt_scr, kt_scr, sl, diag).astype(BF16)
            da = _dot(dob_scr[:, sl], vb_scr[:, sl], NT)
            da_scr[h] = da.astype(BF16)
            da0_scr[h] = jnp.where(diag, da, 0.0).astype(BF16)
        for h, sl in enumerate(heads):
            st = st_ref[cc, h]
            dst = ds_scr[h]
            dstb = dst.astype(BF16)
            dp_ref[rows, 2 * D_MODEL + h * HEAD_D:2 * D_MODEL + (h + 1) * HEAD_D] = (
                _dot(a_scr[h], dob_scr[:, sl], TN) + _dot(kst_scr[:, sl], dstb, NT)).astype(BF16)
            dqi_scr[:, sl] = _dot(dob_scr[:, sl], st.astype(BF16))
            dks_scr[:, sl] = _dot(vb_scr[:, sl], dstb)
            sd_scr[0:1, sl] = jnp.sum(st * dst, axis=0, keepdims=True)
            ds_scr[h] = dst * ebl[:, sl] + _dot(dob_scr[:, sl], qin_scr[:, sl], TN)
        for h, sl in enumerate(heads):
            g_scr[0, :, sl] = _dot(da0_scr[h], kt_scr[0, :, sl])
            h_scr[0, :, sl] = _dot(da0_scr[h], qt_scr[0, :, sl], TN)
            for j in range(1, N_SUB):
                g_scr[j, :, sl] = _dot(da_scr[h], kt_scr[j, :, sl])
                h_scr[j, :, sl] = _dot(da_scr[h], qt_scr[j, :, sl], TN)

        g0 = g_scr[0]
        h0 = h_scr[0]
        dq_inter = eb * dqi_scr[...]
        d_kst = ekst * dks_scr[...]
        db = qs * dq_inter - k * d_kst + qt_scr[0].astype(F32) * g0 - kt_scr[0].astype(F32) * h0
        gq = jnp.zeros((CHUNK, D_MODEL), F32)
        hsel = jnp.zeros((CHUNK, D_MODEL), F32)
        for j in range(N_SUB - 1):
            gj = g_scr[j + 1]
            gq = gq + dec["scales"][j] * gj
            db = db + qt_scr[j + 1].astype(F32) * gj
            hsel = jnp.where(sub == j, h_scr[j + 1], hsel)
        db = db - koff.astype(BF16).astype(F32) * hsel
        d_q = dec["eq0"] * g0 + dec["e_on"] * gq + dq_inter
        d_k = dec["ek0"] * h0 + dec["e_off"] * hsel + d_kst
        db_last = jnp.sum(k * d_kst, axis=0, keepdims=True) + ebl * sd_scr[0:1, :]
        db = db + jnp.where(row == CHUNK - 1, db_last, 0.0)
        dg = _tri_matmul(triu, db)
        d_f = dg / f - d_k
        dp_ref[rows, D_MODEL:2 * D_MODEL] = (d_f * (1.0 - lb) * sig * (1.0 - sig)).astype(BF16)
        gsm_ref[0:1, :] += jnp.sum(d_f * (1.0 - sig), axis=0, keepdims=True) * (lb * (1.0 - lb))
        dp_ref[rows, 0:D_MODEL] = (d_q * (sq * (1.0 + q * (1.0 - sq)))).astype(BF16)

    rc = lambda c: ns - 1 - c
    col = lambda j: pl.BlockSpec((step_rows, D_MODEL), lambda c: (rc(c), j))
    par = lambda rows: pl.BlockSpec((rows, D_MODEL), lambda c: (0, 0))
    return pl.pallas_call(
        body, name="hgrn_bwd", grid=(ns,),
        in_specs=[col(2), col(3), col(4), col(5), col(0),
                  pl.BlockSpec((HGRN_STEP_CHUNKS, N_HEADS, HEAD_D, HEAD_D), lambda c: (rc(c), 0, 0, 0)),
                  col(1), par(2), par(1), ANY, ANY, ANY],
        out_specs=[ANY, par(8), ANY, ANY],
        out_shape=[SDS((seq, D_IN), BF16), SDS((8, D_MODEL), F32), SDS(g_w_out.shape, F32),
                   SDS(g_small.shape, F32)],
        input_output_aliases={9: 0},
        scratch_shapes=[pltpu.VMEM((N_HEADS, HEAD_D, HEAD_D), F32),
                        pltpu.VMEM((2, step_rows, 4 * D_MODEL), BF16), pltpu.SemaphoreType.DMA((2,)),
                        pltpu.VMEM((N_SUB, CHUNK, D_MODEL), BF16), pltpu.VMEM((N_SUB, CHUNK, D_MODEL), BF16)]
                       + [pltpu.VMEM((CHUNK, D_MODEL), BF16)] * 4
                       + [pltpu.VMEM((N_SUB, CHUNK, D_MODEL), F32)] * 2 + [pltpu.VMEM((CHUNK, D_MODEL), F32)] * 2
                       + [pltpu.VMEM((8, D_MODEL), F32)] + [pltpu.VMEM((N_HEADS, CHUNK, CHUNK), BF16)] * 3
                       + EXCHANGE_SEMS * 2,
        compiler_params=_cparams(("arbitrary",)),
    )(p, p, p, p, o, states, dymix, lbl, gw, dp_full, g_w_out, g_small)


def _out_proj(yl, yh, wo, x, tgt, post_w, seq):
    tm = 512

    def body(yl_ref, yh_ref, wo_ref, x_ref, tg_ref, pw_ref, dymix_ref, dout_ref, gwo_ref, st_ref):
        @pl.when(pl.program_id(0) == 0)
        def _():
            gwo_ref[...] = jnp.zeros_like(gwo_ref)
            st_ref[...] = jnp.zeros_like(st_ref)

        ylv = yl_ref[...]
        yhv = yh_ref[...]
        y = _dot(ylv, wo_ref[0:D_MODEL, :]) + _dot(yhv, wo_ref[D_MODEL:D_MIX, :])
        r2 = lax.rsqrt(jnp.mean(y * y, axis=-1, keepdims=True) + EPS)
        yn = y * r2
        pw = pw_ref[...]
        e = (x_ref[...] + yn * pw) - tg_ref[...]
        st_ref[1:2, :] += jnp.sum(e * e, axis=0, keepdims=True) * (0.5 / D_MODEL)
        dout = e * (1.0 / D_MODEL)
        dout_ref[...] = dout
        st_ref[0:1, :] += jnp.sum(dout * yn, axis=0, keepdims=True)
        dyn = dout * pw
        dy = r2 * (dyn - yn * jnp.mean(dyn * yn, axis=-1, keepdims=True))
        dyb = dy.astype(BF16)
        dymix_ref[...] = _dot(dyb, wo_ref[...], NT)
        gwo_ref[0:D_MODEL, :] += _dot(ylv, dyb, TN)
        gwo_ref[D_MODEL:D_MIX, :] += _dot(yhv, dyb, TN)

    row = lambda w: pl.BlockSpec((tm, w), lambda m: (m, 0))
    full = lambda shape: pl.BlockSpec(shape, lambda m: (0,) * len(shape))
    once = lambda shape: pl.BlockSpec(shape, lambda m: (0,) * len(shape), pipeline_mode=pl.Buffered(1))
    return pl.pallas_call(
        body, name="out_proj", grid=(seq // tm,),
        in_specs=[row(D_MODEL), row(D_MODEL), once((D_MIX, D_MODEL)), row(D_MODEL), row(D_MODEL),
                  full((1, D_MODEL))],
        out_specs=[row(D_MIX), row(D_MODEL), once((D_MIX, D_MODEL)), full((8, D_MODEL))],
        out_shape=[SDS((seq, D_MIX), F32), SDS((seq, D_MODEL), F32), SDS((D_MIX, D_MODEL), F32),
                   SDS((8, D_MODEL), F32)],
        compiler_params=_cparams(("arbitrary",)),
    )(yl, yh, wo, x, tgt, post_w)


MESH = pl.DeviceIdType.MESH
ANY = pl.BlockSpec(memory_space=pl.ANY)
EXCHANGE_SEMS = [pltpu.SemaphoreType.DMA((N_DEV - 1,)), pltpu.SemaphoreType.DMA((N_DEV - 1,)),
                 pltpu.SemaphoreType.DMA(())]


def _mesh_pos():
    return lax.axis_index("x"), lax.axis_index("y"), lax.axis_index("c")


class _SlotExchange:
    def __init__(self, src_ref, dst_ref, send_sems, recv_sems, local_sem, blocked):
        x, y, c = _mesh_pos()
        me = 4 * x + 2 * y + c
        src = (lambda dest: src_ref.at[dest]) if blocked else (lambda dest: src_ref)
        self.local = pltpu.make_async_copy(src(me), dst_ref.at[me], local_sem)
        self.sends, self.recvs = [], []
        for k in range(1, N_DEV):
            px = 1 - x if (k >> 2) & 1 else x
            py = 1 - y if (k >> 1) & 1 else y
            pc = 1 - c if k & 1 else c
            peer = 4 * px + 2 * py + pc
            sems = dict(send_sem=send_sems.at[k - 1], recv_sem=recv_sems.at[k - 1],
                        device_id=(px, py, pc), device_id_type=MESH)
            self.sends.append(pltpu.make_async_remote_copy(src_ref=src(peer), dst_ref=dst_ref.at[me], **sems))
            self.recvs.append(pltpu.make_async_remote_copy(src_ref=dst_ref.at[peer], dst_ref=dst_ref.at[peer], **sems))

    def start(self):
        self.local.start()
        for cp in self.sends:
            cp.start()

    def wait(self):
        for cp in self.recvs:
            cp.wait_recv()
        for cp in self.sends:
            cp.wait_send()
        self.local.wait()


class _ChipExchange:
    def __init__(self, src_ref, dst_ref, send_sems, recv_sems, local_sem):
        x, y, c = _mesh_pos()
        chip = 2 * x + y
        self.local = pltpu.make_async_copy(src_ref.at[chip], dst_ref.at[chip], local_sem)
        self.sends, self.recvs = [], []
        for k in range(1, N_CHIPS):
            px = 1 - x if (k >> 1) & 1 else x
            py = 1 - y if k & 1 else y
            peer = 2 * px + py
            sems = dict(send_sem=send_sems.at[k - 1], recv_sem=recv_sems.at[k - 1],
                        device_id=(px, py, c), device_id_type=MESH)
            self.sends.append(pltpu.make_async_remote_copy(src_ref=src_ref.at[peer], dst_ref=dst_ref.at[chip], **sems))
            self.recvs.append(pltpu.make_async_remote_copy(src_ref=dst_ref.at[peer], dst_ref=dst_ref.at[peer], **sems))

    def start(self):
        self.local.start()
        for cp in self.sends:
            cp.start()

    def wait(self):
        for cp in self.recvs:
            cp.wait_recv()
        for cp in self.sends:
            cp.wait_send()
        self.local.wait()


GRAD_W_IN_TK = 2048


def _grad_w_in_sibling(u, dp, core, seq):
    tk = min(GRAD_W_IN_TK, seq)
    nk = seq // tk

    def body(core_ref, u_ref, dp_ref, g_ref):
        del core_ref

        @pl.when(pl.program_id(1) == 0)
        def _():
            g_ref[...] = jnp.zeros_like(g_ref)

        g_ref[0] += _dot(u_ref[...], dp_ref[...], TN)

    return pl.pallas_call(
        body, name="grad_w_in_sibling",
        grid_spec=pltpu.PrefetchScalarGridSpec(
            num_scalar_prefetch=1, grid=(N_CHIPS, nk),
            in_specs=[pl.BlockSpec((tk, D_MODEL), lambda n, k, c: (k, 0)),
                      pl.BlockSpec((tk, W_BLK), lambda n, k, c: (k, 2 * n + 1 - c[0]))],
            out_specs=pl.BlockSpec((1, D_MODEL, W_BLK), lambda n, k, c: (n, 0, 0))),
        out_shape=SDS((N_CHIPS, D_MODEL, W_BLK), F32),
        compiler_params=_cparams(("parallel", "arbitrary")),
    )(core, u, dp)


def _grad_w_in_own(u, dp, core, g_sib, seq):
    tk = min(GRAD_W_IN_TK, seq)
    nk = seq // tk

    def body(core_ref, u_ref, dp_ref, gsib_ref, g_ref, land, send_sem, recv_sem):
        del core_ref
        n = pl.program_id(0)
        k = pl.program_id(1)
        x, y, c = _mesh_pos()
        swap = pltpu.make_async_remote_copy(src_ref=gsib_ref, dst_ref=land, send_sem=send_sem, recv_sem=recv_sem,
                                            device_id=(x, y, 1 - c), device_id_type=MESH)

        @pl.when((n == 0) & (k == 0))
        def _():
            swap.start()

        @pl.when(k == 0)
        def _():
            g_ref[...] = jnp.zeros_like(g_ref)

        g_ref[0] += _dot(u_ref[...], dp_ref[...], TN)

        @pl.when((n == 0) & (k == nk - 1))
        def _():
            swap.wait_recv()

        @pl.when(k == nk - 1)
        def _():
            g_ref[0] += land[n]

        @pl.when((n == N_CHIPS - 1) & (k == nk - 1))
        def _():
            swap.wait_send()

    return pl.pallas_call(
        body, name="grad_w_in_own",
        grid_spec=pltpu.PrefetchScalarGridSpec(
            num_scalar_prefetch=1, grid=(N_CHIPS, nk),
            in_specs=[pl.BlockSpec((tk, D_MODEL), lambda n, k, c: (k, 0)),
                      pl.BlockSpec((tk, W_BLK), lambda n, k, c: (k, 2 * n + c[0])), ANY],
            out_specs=pl.BlockSpec((1, D_MODEL, W_BLK), lambda n, k, c: (n, 0, 0)),
            scratch_shapes=[pltpu.VMEM((N_CHIPS, D_MODEL, W_BLK), F32), pltpu.SemaphoreType.DMA(()),
                            pltpu.SemaphoreType.DMA(())]),
        out_shape=SDS((N_CHIPS, D_MODEL, W_BLK), F32),
        compiler_params=_cparams(("arbitrary", "arbitrary")),
    )(core, u, dp, g_sib)


def _grad_x(dp, w_all, x, pre_w, dout, g_chip, seq):
    tm = 512
    nm = seq // tm

    def body(dp_ref, w_ref, x_ref, pw_ref, do_ref, gsrc_ref, gx_ref, gpw_ref, recv_ref,
             send_sems, recv_sems, local_sem):
        m = pl.program_id(0)
        ex = _ChipExchange(gsrc_ref, recv_ref, send_sems, recv_sems, local_sem)

        @pl.when(m == 0)
        def _():
            ex.start()
            gpw_ref[...] = jnp.zeros_like(gpw_ref)

        du = _dot(dp_ref[:, 0:W_BLK], w_ref[0], NT)
        for j in range(1, N_DEV):
            du = du + _dot(dp_ref[:, j * W_BLK:(j + 1) * W_BLK], w_ref[j], NT)
        xv = x_ref[...]
        r1 = lax.rsqrt(jnp.mean(xv * xv, axis=-1, keepdims=True) + EPS)
        xn = xv * r1
        gpw_ref[0:1, :] += jnp.sum(du * xn, axis=0, keepdims=True)
        dxn = du * pw_ref[...]
        gx_ref[...] = r1 * (dxn - xn * jnp.mean(dxn * xn, axis=-1, keepdims=True)) + do_ref[...]

        @pl.when(m == nm - 1)
        def _():
            ex.wait()

    row = lambda w: pl.BlockSpec((tm, w), lambda m: (m, 0))
    return pl.pallas_call(
        body, name="grad_x", grid=(nm,),
        in_specs=[row(D_IN), pl.BlockSpec((N_DEV, D_MODEL, W_BLK), lambda m: (0, 0, 0), pipeline_mode=pl.Buffered(1)),
                  row(D_MODEL),
                  pl.BlockSpec((1, D_MODEL), lambda m: (0, 0)), row(D_MODEL), ANY],
        out_specs=[row(D_MODEL), pl.BlockSpec((8, D_MODEL), lambda m: (0, 0)), ANY],
        out_shape=[SDS((seq, D_MODEL), F32), SDS((8, D_MODEL), F32), SDS(g_chip.shape, F32)],
        scratch_shapes=[pltpu.SemaphoreType.DMA((N_CHIPS - 1,)), pltpu.SemaphoreType.DMA((N_CHIPS - 1,)),
                        pltpu.SemaphoreType.DMA(())],
        compiler_params=_cparams(("arbitrary",)),
    )(dp, w_all, x, pre_w, dout, g_chip)


def _local_step(x, tgt, p, u, conv_w, conv_b, wa, wx, ba, bx, lam, lbl, gnorm_w, w_out, post_w):
    seq = x.shape[0]
    h, y_lru = _lru_forward(p, conv_w, conv_b, wa, wx, ba, bx, lam, seq)
    y_hgrn, o, states = _hgrn_forward(p, lbl, gnorm_w, seq)
    dymix, dout, g_w_out, stats = _out_proj(y_lru, y_hgrn, w_out, x, tgt, post_w, seq)
    dp_lru, g_wa, g_wx, ls = _lru_backward(p, h, dymix, conv_w, conv_b, wa, wx, ba, bx, lam, seq)
    g_small = _pack_small(_shard_rows(g_wa, LRU_BLOCKS), _shard_rows(g_wx, LRU_BLOCKS),
                          _shard_rows(ls[0:4].reshape(4, D_MODEL, 1), 4).reshape(N_DEV, 4, 128),
                          _shard_rows(ls[5].reshape(4, LRU_BW, 1), 4).reshape(N_DEV, 4, 32),
                          _shard_rows(ls[6].reshape(4, LRU_BW, 1), 4).reshape(N_DEV, 4, 32))
    dp, hgrn_small, r_out, r_small = _hgrn_backward(
        p, o, states, dymix, lbl, gnorm_w, dp_lru, g_w_out.reshape(N_DEV, D_MIX // N_DEV, D_MODEL), g_small, seq)
    return dict(u=u, dp=dp, dout=dout, r_out=r_out, r_small=r_small,
                lru_small=ls, hgrn_small=hgrn_small, stats=stats)


class _TwoLevelGather:
    def __init__(self, ins, outs, send_sems, recv_sems, local_sems):
        self.ins, self.outs = ins, outs
        self.send_sems, self.recv_sems, self.local_sems = send_sems, recv_sems, local_sems
        x, y, c = _mesh_pos()
        self.c = c
        self.me, self.sibling = (x, y, c), (x, y, 1 - c)
        self.chips = [(1 - x, y), (x, 1 - y), (1 - x, 1 - y)]
        n = len(ins)
        self.mine = [pltpu.make_async_copy(ins[a], self._slot(a, self.me), local_sems.at[a]) for a in range(n)]
        self.first = []
        for a in range(n):
            self.first.append(self._copy(a, 0, self.me, self.sibling, src=ins[a]))
            self.first += [self._copy(a, 1 + j, self.me, (*chip, c), src=ins[a])
                           for j, chip in enumerate(self.chips)]
        self.passed = [self._copy(a, 4 + j, (*chip, c), self.sibling)
                       for j, chip in enumerate(self.chips) for a in range(n)]

    def _slot(self, a, pos):
        return self.outs[a].at[4 * pos[0] + 2 * pos[1] + pos[2]]

    def _copy(self, a, k, block, to, src=None):
        dst = self._slot(a, block)
        return pltpu.make_async_remote_copy(
            src_ref=dst if src is None else src, dst_ref=dst,
            send_sem=self.send_sems.at[a, k], recv_sem=self.recv_sems.at[a, k],
            device_id=to, device_id_type=MESH)

    def start(self):
        for cp in self.mine + self.first:
            cp.start()

    def forward(self):
        n = len(self.ins)
        for j, chip in enumerate(self.chips):
            for a in range(n):
                self._copy(a, 1 + j, (*chip, self.c), self.me).wait_recv()
                self.passed[j * n + a].start()

    def finish(self):
        for a in range(len(self.ins)):
            self._copy(a, 0, self.sibling, self.me).wait_recv()
            for j, chip in enumerate(self.chips):
                self._copy(a, 4 + j, (*chip, 1 - self.c), self.me).wait_recv()
        for cp in self.first + self.passed:
            cp.wait_send()
        for cp in self.mine:
            cp.wait()


W_IN_DIRECT = (1, 2, 4, 6)
W_IN_PASSED = (2, 4, 6)

def _in_proj_gather(x, pre_w, w_in_blk, w_out_blk, small_blk, me, seq):
    tm = min(1024, seq)
    nm = seq // tm
    last = N_DEV - 1

    def body(me_ref, x_ref, pw_ref, wblk_ref, woblk_ref, smblk_ref,
             p_ref, u_ref, wall_ref, woall_ref, small_ref,
             u_all, w_vmem, own_sem, d_send, d_recv, f_send, f_recv, wb_sems, g_send, g_recv, g_local):
        i = pl.program_id(0)
        m = pl.program_id(1)
        idx = me_ref[0]
        x_, y_, c_ = _mesh_pos()
        aux = _TwoLevelGather([woblk_ref, smblk_ref], [woall_ref, small_ref], g_send, g_recv, g_local)

        def peer(k):
            return (1 - x_ if (k >> 2) & 1 else x_, 1 - y_ if (k >> 1) & 1 else y_, 1 - c_ if k & 1 else c_)

        def direct(k):
            f = W_IN_DIRECT.index(k)
            return (pltpu.make_async_remote_copy(src_ref=wblk_ref, dst_ref=w_vmem.at[idx], send_sem=d_send.at[f],
                                                 recv_sem=d_recv.at[f], device_id=peer(k), device_id_type=MESH),
                    pltpu.make_async_remote_copy(src_ref=w_vmem.at[idx ^ k], dst_ref=w_vmem.at[idx ^ k],
                                                 send_sem=d_send.at[f], recv_sem=d_recv.at[f], device_id=peer(k),
                                                 device_id_type=MESH))

        def passed(k):
            f = W_IN_PASSED.index(k)
            return (pltpu.make_async_remote_copy(src_ref=w_vmem.at[idx ^ k], dst_ref=w_vmem.at[idx ^ k],
                                                 send_sem=f_send.at[f], recv_sem=f_recv.at[f], device_id=peer(1),
                                                 device_id_type=MESH),
                    pltpu.make_async_remote_copy(src_ref=w_vmem.at[idx ^ (k + 1)], dst_ref=w_vmem.at[idx ^ (k + 1)],
                                                 send_sem=f_send.at[f], recv_sem=f_recv.at[f], device_id=peer(1),
                                                 device_id_type=MESH))

        def write_back(k):
            return pltpu.make_async_copy(w_vmem.at[idx ^ k], wall_ref.at[idx ^ k], wb_sems.at[k])

        own = pltpu.make_async_copy(wblk_ref, w_vmem.at[idx], own_sem)

        @pl.when((i == 0) & (m == 0))
        def _():
            own.start()
            for k in (1, 2, 4):
                direct(k)[0].start()
            own.wait()
            write_back(0).start()

        for k in range(1, N_DEV):
            @pl.when((i == k) & (m == 0))
            def _(k=k):
                if k in W_IN_DIRECT:
                    direct(k)[1].wait_recv()
                    if k in W_IN_PASSED:
                        passed(k)[0].start()
                else:
                    passed(k - 1)[1].wait_recv()
                write_back(k).start()
                if k == 2:
                    direct(6)[0].start()
                if k == 4:
                    aux.start()
                if k == N_DEV - 1:
                    aux.forward()

        rows = pl.ds(pl.multiple_of(m * tm, tm), tm)

        @pl.when(i == 0)
        def _():
            xv = x_ref[...]
            r = lax.rsqrt(jnp.mean(xv * xv, axis=-1, keepdims=True) + EPS)
            ub = (xv * r * pw_ref[...]).astype(BF16)
            u_all[rows, :] = ub
            u_ref[...] = ub

        p_ref[...] = _dot(u_all[rows, :], w_vmem[idx ^ i])

        @pl.when((i == last) & (m == nm - 1))
        def _():
            for k in W_IN_DIRECT:
                direct(k)[0].wait_send()
            for k in W_IN_PASSED:
                passed(k)[0].wait_send()
            for k in range(N_DEV):
                write_back(k).wait()
            aux.finish()

    first_pass = lambda i, m: jnp.where(i == 0, m, nm - 1)
    return pl.pallas_call(
        body, name="in_proj_gather",
        grid_spec=pltpu.PrefetchScalarGridSpec(
            num_scalar_prefetch=1, grid=(N_DEV, nm),
            in_specs=[pl.BlockSpec((tm, D_MODEL), lambda i, m, me: (first_pass(i, m), 0)),
                      pl.BlockSpec((1, D_MODEL), lambda i, m, me: (0, 0)), ANY, ANY, ANY],
            out_specs=[pl.BlockSpec((tm, W_BLK), lambda i, m, me: (m, me[0] ^ i)),
                       pl.BlockSpec((tm, D_MODEL), lambda i, m, me: (first_pass(i, m), 0)), ANY, ANY, ANY],
            scratch_shapes=[pltpu.VMEM((seq, D_MODEL), BF16), pltpu.VMEM((N_DEV, D_MODEL, W_BLK), BF16),
                            pltpu.SemaphoreType.DMA(()),
                            pltpu.SemaphoreType.DMA((len(W_IN_DIRECT),)), pltpu.SemaphoreType.DMA((len(W_IN_DIRECT),)),
                            pltpu.SemaphoreType.DMA((len(W_IN_PASSED),)), pltpu.SemaphoreType.DMA((len(W_IN_PASSED),)),
                            pltpu.SemaphoreType.DMA((N_DEV,)),
                            pltpu.SemaphoreType.DMA((2, 7)), pltpu.SemaphoreType.DMA((2, 7)),
                            pltpu.SemaphoreType.DMA((2,))]),
        out_shape=[SDS((seq, D_IN), F32), SDS((seq, D_MODEL), BF16), SDS((N_DEV, D_MODEL, W_BLK), BF16),
                   SDS((N_DEV,) + w_out_blk.shape, w_out_blk.dtype), SDS((N_DEV,) + small_blk.shape, small_blk.dtype)],
        compiler_params=_cparams(("arbitrary", "arbitrary")),
    )(me, x, pre_w, w_in_blk, w_out_blk, small_blk)


def _exchange_grads(blocks, repl):
    nb = len(blocks)
    n = nb + 1

    def body(*refs):
        ins, outs, sems = refs[:n], refs[n:2 * n], refs[2 * n:]
        exs = [_SlotExchange(ins[a], outs[a], *sems[3 * a:3 * a + 3], blocked=a < nb) for a in range(n)]
        for ex in exs:
            ex.start()
        for ex in exs:
            ex.wait()

    arrs = list(blocks) + [repl]
    shapes = [SDS(b.shape, b.dtype) for b in blocks] + [SDS((N_DEV,) + repl.shape, repl.dtype)]
    return pl.pallas_call(
        body, name="exchange_small", out_shape=shapes,
        in_specs=[ANY] * n, out_specs=[ANY] * n,
        scratch_shapes=EXCHANGE_SEMS * n,
    )(*arrs)


def _pack_rows(picks, name):
    arrs = [p[0] for p in picks]

    def body(*refs):
        out = refs[-1]
        out[...] = jnp.zeros_like(out)
        at = 0
        for ref, (_, row, rows, scale) in zip(refs[:-1], picks):
            out[at:at + rows, :] = ref[row:row + rows, :] * scale
            at += rows

    return pl.pallas_call(body, name=name, out_shape=SDS((8, D_MODEL), F32))(*arrs)


def _adamw(g, w, m, v):
    m2 = ADAM_B1 * m + (1.0 - ADAM_B1) * g
    v2 = ADAM_B2 * v + (1.0 - ADAM_B2) * (g * g)
    m_hat = m2 / (1.0 - ADAM_B1 ** ADAM_STEP)
    v_hat = v2 / (1.0 - ADAM_B2 ** ADAM_STEP)
    delta = -ADAM_LR * (m_hat / (jnp.sqrt(v_hat) + ADAM_EPS) + ADAM_WD * w)
    return delta, m2, v2


def _sum_slots(r_ref):
    g = r_ref[0]
    for s in range(1, r_ref.shape[0]):
        g = g + r_ref[s]
    return g


def _sum_adamw(recv, w, m, v, tr, name):
    rows, cols = w.shape

    def body(r_ref, w_ref, m_ref, v_ref, g_ref, d_ref, m2_ref, v2_ref):
        g = _sum_slots(r_ref)
        g_ref[...] = g
        d_ref[...], m2_ref[...], v2_ref[...] = _adamw(g, w_ref[...], m_ref[...], v_ref[...])

    blk = pl.BlockSpec((tr, cols), lambda i: (i, 0))
    return pl.pallas_call(
        body, name=name, grid=(rows // tr,),
        in_specs=[pl.BlockSpec((recv.shape[0], tr, cols), lambda i: (0, i, 0)), blk, blk, blk],
        out_specs=[blk] * 4, out_shape=[SDS((rows, cols), F32)] * 4,
        compiler_params=_cparams(("parallel",)),
    )(recv, w, m, v)


def _sum_adamw_pieces(recv, rows, ws, ms, vs, name, loss_row=None):
    n = len(ws)

    def body(r_ref, *refs):
        w_refs, m_refs, v_refs, outs = refs[:n], refs[n:2 * n], refs[2 * n:3 * n], refs[3 * n:]
        g = _sum_slots(r_ref)
        for i, (row, nrows) in enumerate(rows):
            gi = g[row:row + nrows, 0:ws[i].shape[1]]
            outs[i][...] = gi
            outs[n + i][...], outs[2 * n + i][...], outs[3 * n + i][...] = _adamw(
                gi, w_refs[i][...], m_refs[i][...], v_refs[i][...])
        if loss_row is not None:
            total = jnp.sum(g[loss_row:loss_row + 1, :], axis=-1, keepdims=True)
            outs[4 * n][...] = jnp.broadcast_to(total, outs[4 * n].shape)

    shapes = [SDS(w.shape, F32) for w in ws] * 4 + ([SDS((8, 128), F32)] if loss_row is not None else [])
    out = pl.pallas_call(body, name=name, out_shape=shapes)(recv, *ws, *ms, *vs)
    return [out[k * n:(k + 1) * n] for k in range(4)] + list(out[4 * n:])


def _shard_rows(t, lead):
    r = t.shape[1] // N_DEV
    t = t.reshape((lead, N_DEV, r) + t.shape[2:])
    return jnp.moveaxis(t, 1, 0)


def _pad_tile(t):
    return jnp.pad(t, ((0, 0), (0, 8 - t.shape[1]), (0, SM_LANES - t.shape[2])))


def _pack_small(wa, wx, cw, b_a, b_x):
    n = wa.shape[0]
    return jnp.concatenate([wa.reshape(n, 128, SM_LANES), wx.reshape(n, 128, SM_LANES),
                            _pad_tile(cw), _pad_tile(b_a), _pad_tile(b_x)], axis=1)


def _unpack_small(t):
    n = t.shape[0]
    return (t[:, SM_WA:SM_WA + 128].reshape(n, 4, 32, 256), t[:, SM_WX:SM_WX + 128].reshape(n, 4, 32, 256),
            t[:, SM_CW:SM_CW + 4, 0:128], t[:, SM_BA:SM_BA + 4, 0:32], t[:, SM_BX:SM_BX + 4, 0:32])


def kernel(x, pre_norm_w, w_in, conv_w, conv_b, lru_w_a, lru_b_a, lru_w_x, lru_b_x, lru_lambda, hgrn_lb_logits, hgrn_gnorm_w, w_out, post_norm_w, loss_target, m_pre_norm_w, m_w_in, m_conv_w, m_conv_b, m_lru_w_a, m_lru_b_a, m_lru_w_x, m_lru_b_x, m_lru_lambda, m_hgrn_lb_logits, m_hgrn_gnorm_w, m_w_out, m_post_norm_w, v_pre_norm_w, v_w_in, v_conv_w, v_conv_b, v_lru_w_a, v_lru_b_a, v_lru_w_x, v_lru_b_x, v_lru_lambda, v_hgrn_lb_logits, v_hgrn_gnorm_w, v_w_out, v_post_norm_w):
    seq = x.shape[1]
    x2 = x.reshape(seq, D_MODEL)
    tgt = loss_target.reshape(seq, D_MODEL)

    small_w = _pack_small(lru_w_a, lru_w_x, conv_w, lru_b_a, lru_b_x)[0]
    me = (4 * lax.axis_index("x") + 2 * lax.axis_index("y") + lax.axis_index("c")).astype(jnp.int32).reshape(1)
    p, u, w_in_all, w_out_all, small_all = _in_proj_gather(
        x2, pre_norm_w, w_in[0].astype(BF16), w_out[0].astype(BF16), small_w, me, seq)
    wa_s, wx_s, cw_s, ba_s, bx_s = _unpack_small(small_all)
    wa = jnp.moveaxis(wa_s, 0, 1).reshape(LRU_BLOCKS, LRU_BW, LRU_BW).astype(BF16)
    wx = jnp.moveaxis(wx_s, 0, 1).reshape(LRU_BLOCKS, LRU_BW, LRU_BW).astype(BF16)
    cw = jnp.moveaxis(cw_s, 0, 1).reshape(4, D_MODEL)
    ba = jnp.moveaxis(ba_s, 0, 1).reshape(1, D_MODEL)
    bx = jnp.moveaxis(bx_s, 0, 1).reshape(1, D_MODEL)

    loc = _local_step(x2, tgt, p, u, cw, conv_b, wa, wx, ba, bx, lru_lambda,
                      hgrn_lb_logits, hgrn_gnorm_w, w_out_all.reshape(D_MIX, D_MODEL), post_norm_w)

    ls, r_out, r_small = loc["lru_small"], loc["r_out"], loc["r_small"]
    core = lax.axis_index("c").astype(jnp.int32).reshape(1)
    g_sib = _grad_w_in_sibling(loc["u"], loc["dp"], core, seq)
    g_chip = _grad_w_in_own(loc["u"], loc["dp"], core, g_sib, seq)
    grad_x, pre_small, r_in = _grad_x(loc["dp"], w_in_all, x2, pre_norm_w, loc["dout"], g_chip, seq)
    g_repl = _pack_rows([(pre_small, 0, 1, 1.0), (ls, 4, 1, 1.0), (ls, 7, 1, 1.0),
                         (loc["hgrn_small"], 0, 1, 1.0), (loc["hgrn_small"], 0, 1, -1.0),
                         (loc["hgrn_small"], 1, 1, 1.0), (loc["stats"], 0, 2, 1.0)], "pack_grads")
    (r_repl,) = _exchange_grads([], g_repl)

    repl_rows = [(RP_PRE, 1), (RP_CB, 1), (RP_LAM, 1), (RP_LB0, 2), (RP_GN, 1), (RP_POST, 1)]
    o_repl = _sum_adamw_pieces(
        r_repl, repl_rows,
        [pre_norm_w, conv_b, lru_lambda, hgrn_lb_logits, hgrn_gnorm_w, post_norm_w],
        [m_pre_norm_w, m_conv_b, m_lru_lambda, m_hgrn_lb_logits, m_hgrn_gnorm_w, m_post_norm_w],
        [v_pre_norm_w, v_conv_b, v_lru_lambda, v_hgrn_lb_logits, v_hgrn_gnorm_w, v_post_norm_w],
        "adamw_repl", loss_row=RP_LOSS)
    loss = o_repl[4][0, 0]

    small_rows = [(SM_WA, 128), (SM_WX, 128), (SM_CW, 4), (SM_BA, 4), (SM_BX, 4)]
    as_rows = lambda wa_, wx_, cw_, ba_, bx_: [wa_.reshape(128, 256), wx_.reshape(128, 256), cw_.reshape(4, 128),
                                               ba_.reshape(4, 32), bx_.reshape(4, 32)]
    o_small = _sum_adamw_pieces(
        r_small, small_rows, as_rows(lru_w_a, lru_w_x, conv_w, lru_b_a, lru_b_x),
        as_rows(m_lru_w_a, m_lru_w_x, m_conv_w, m_lru_b_a, m_lru_b_x),
        as_rows(v_lru_w_a, v_lru_w_x, v_conv_w, v_lru_b_a, v_lru_b_x), "adamw_small")

    o_in = _sum_adamw(r_in, w_in[0], m_w_in[0], v_w_in[0], 128, "adamw_w_in")
    o_out = _sum_adamw(r_out, w_out[0], m_w_out[0], v_w_out[0], 64, "adamw_w_out")

    outs = [loss, grad_x.reshape(x.shape)]
    for kind in range(4):
        pre, cb, lam, lb, gn, post = o_repl[kind]
        swa, swx, scw, sba, sbx = o_small[kind]
        outs += [pre, o_in[kind][None], scw.reshape(conv_w.shape), cb, swa.reshape(lru_w_a.shape),
                 sba.reshape(lru_b_a.shape), swx.reshape(lru_w_x.shape), sbx.reshape(lru_b_x.shape),
                 lam, lb, gn, o_out[kind][None], post]
    return tuple(outs)
```

```python
import jax
import jax.numpy as jnp
from jax import lax
from jax.experimental import pallas as pl
from jax.experimental.pallas import tpu as pltpu

F32 = jnp.float32
BF16 = jnp.bfloat16
SDS = jax.ShapeDtypeStruct

D_MODEL = 1024
D_IN = 6144
N_DEV = 8
N_CHIPS = 4
W_BLK = D_IN // N_DEV
D_MIX = 2048
LRU_BLOCKS = 4
LRU_BW = 256
LRU_C = 8.0
LANES = 128
LANE_GROUPS = D_MODEL // LANES
N_HEADS = 8
HEAD_D = 128
CHUNK = 128
SUB = 32
N_SUB = CHUNK // SUB
HGRN_FWD_STEP_CHUNKS = 4
HGRN_STEP_CHUNKS = 2
EXP_CLAMP = 80.0
EPS = 1e-6

ADAM_LR = 0.001
ADAM_B1 = 0.9
ADAM_B2 = 0.999
ADAM_EPS = 1e-08
ADAM_WD = 0.01
ADAM_STEP = 10

VMEM_LIMIT = 56 * 1024 * 1024

NN = (((1,), (0,)), ((), ()))
NT = (((1,), (1,)), ((), ()))
TN = (((0,), (0,)), ((), ()))

SM_LANES = 256
SM_WA = 0
SM_WX = 128
SM_CW = 256
SM_BA = 264
SM_BX = 272
SM_ROWS = 280

RP_PRE, RP_CB, RP_LAM, RP_LB0, RP_LB1, RP_GN, RP_POST, RP_LOSS = range(8)


def _dot(a, b, dims=NN):
    return lax.dot_general(a, b, dims, preferred_element_type=F32)


def _sigmoid(x):
    return 0.5 * jnp.tanh(0.5 * x) + 0.5


def _sigmoid_pos(x):
    return 1.0 / (1.0 + jnp.exp(-x))


def _cparams(sem, vmem=VMEM_LIMIT):
    return pltpu.CompilerParams(dimension_semantics=sem, vmem_limit_bytes=vmem)


def _iota(shape, axis):
    return lax.broadcasted_iota(jnp.int32, shape, axis)


def _softplus_neg(lam):
    z = -lam
    e = jnp.exp(-jnp.abs(z))
    u = 1.0 + e
    log1p_e = jnp.where(u == 1.0, e, jnp.log(u) * (e / (u - 1.0)))
    sp = jnp.maximum(z, 0.0) + log1p_e
    dsp = -jnp.where(z >= 0.0, 1.0 / u, e / u)
    return sp, dsp


def _neg_expm1(x):
    poly = x * (1.0 + x * (1.0 / 2 + x * (1.0 / 6 + x * (1.0 / 24 + x * (1.0 / 120)))))
    return jnp.where(x > -1.0 / 16, -poly, 1.0 - jnp.exp(x))


def _conv_taps(lx, prev8, cw_ref, cb_ref, tile):
    xc = cb_ref[...] + cw_ref[3:4, :] * lx
    for j in (1, 2, 3):
        xc = xc + cw_ref[3 - j:4 - j, :] * pltpu.roll(lx, j, 0)
    row8 = _iota((8, D_MODEL), 0)
    last8 = lx[tile - 8:tile, :]
    fix = jnp.zeros((8, D_MODEL), F32)
    for j in (1, 2, 3):
        wrong = pltpu.roll(last8, j, 0)
        right = pltpu.roll(prev8, j, 0)
        fix = fix + cw_ref[3 - j:4 - j, :] * jnp.where(row8 < j, right - wrong, 0.0)
    return xc, fix


def _lru_gates(xcs, wa, wx, ba, bx, sp):
    xb = xcs.astype(BF16)
    r = _sigmoid_pos(_dot(xb, wa) + ba)
    i = _sigmoid(_dot(xb, wx) + bx)
    la = (-LRU_C * sp) * r
    a = jnp.exp(la)
    one_minus_a2 = _neg_expm1(2.0 * la)
    return r, i, a, one_minus_a2


def _lru_forward(p, conv_w, conv_b, wa, wx, ba, bx, lam, seq):
    tile = min(1024, seq // 2)
    seg = tile // 8
    pitch = seg + 4

    def body(lx_ref, gt_ref, cw_ref, cb_ref, wa_ref, wx_ref, ba_ref, bx_ref, lam_ref,
             h_ref, y_ref, ext, hcar, xc_scr, a_scr, u_scr):
        @pl.when(pl.program_id(0) == 0)
        def _():
            ext[0:8, :] = jnp.zeros((8, D_MODEL), F32)
            hcar[...] = jnp.zeros_like(hcar)

        lx = lx_ref[...]
        ext[8:8 + tile, :] = lx
        xc = cb_ref[...] + cw_ref[3:4, :] * lx
        for j in (1, 2, 3):
            xc = xc + cw_ref[3 - j:4 - j, :] * ext[8 - j:8 - j + tile, :]
        xc_scr[...] = xc
        ext[0:8, :] = lx_ref[tile - 8:tile, :]
        sp, _ = _softplus_neg(lam_ref[...])
        for n in range(LRU_BLOCKS):
            sl = slice(n * LRU_BW, (n + 1) * LRU_BW)
            xcs = xc_scr[:, sl]
            _, i, a, ne = _lru_gates(xcs, wa_ref[n], wx_ref[n], ba_ref[:, sl], bx_ref[:, sl], sp[:, sl])
            u = jnp.sqrt(ne) * (i * xcs)
            for half in range(2):
                lanes = slice(half * LANES, (half + 1) * LANES)
                for s in range(8):
                    a_scr[2 * n + half, s * pitch:s * pitch + seg, :] = a[s * seg:(s + 1) * seg, lanes]
                    u_scr[2 * n + half, s * pitch:s * pitch + seg, :] = u[s * seg:(s + 1) * seg, lanes]

        def step(j, carry):
            rows = pl.ds(j, 8, stride=pitch)
            out = []
            for gi in range(LANE_GROUPS):
                hl, al = carry[gi]
                a = a_scr.at[gi][rows, :]
                hl = a * hl + u_scr.at[gi][rows, :]
                al = a * al
                u_scr.at[gi][rows, :] = hl
                a_scr.at[gi][rows, :] = al
                out.append((hl, al))
            return tuple(out)

        init = tuple((jnp.zeros((8, LANES), F32), jnp.ones((8, LANES), F32)) for _ in range(LANE_GROUPS))
        fin = lax.fori_loop(0, seg, step, init)
        hl = jnp.concatenate([f[0] for f in fin], axis=1)
        al = jnp.concatenate([f[1] for f in fin], axis=1)
        row8 = _iota((8, D_MODEL), 0)
        for k in (1, 2, 4):
            m = row8 >= k
            hl = jnp.where(m, hl + al * pltpu.roll(hl, k, 0), hl)
            al = jnp.where(m, al * pltpu.roll(al, k, 0), al)
        ends = hl + al * hcar[...]
        before = jnp.where(row8 == 0, hcar[...], pltpu.roll(ends, 1, 0))
        hcar[...] = jnp.broadcast_to(ends[7:8, :], (8, D_MODEL))
        for s in range(8):
            rows = slice(s * seg, (s + 1) * seg)
            for gi in range(LANE_GROUPS):
                lanes = slice(gi * LANES, (gi + 1) * LANES)
                g = gt_ref[rows, lanes]
                h = (u_scr[gi, s * pitch:s * pitch + seg, :]
                     + a_scr[gi, s * pitch:s * pitch + seg, :] * before[s:s + 1, lanes])
                h_ref[rows, lanes] = h
                y_ref[rows, lanes] = (h * (g * _sigmoid(g))).astype(BF16)

    full = lambda shape: pl.BlockSpec(shape, lambda t: (0,) * len(shape))
    return pl.pallas_call(
        body, name="lru_fwd", grid=(seq // tile,),
        in_specs=[pl.BlockSpec((tile, D_MODEL), lambda t: (t, 0)),
                  pl.BlockSpec((tile, D_MODEL), lambda t: (t, 1)),
                  full((4, D_MODEL)), full((1, D_MODEL)),
                  full((LRU_BLOCKS, LRU_BW, LRU_BW)), full((LRU_BLOCKS, LRU_BW, LRU_BW)),
                  full((1, D_MODEL)), full((1, D_MODEL)), full((1, D_MODEL))],
        out_specs=[pl.BlockSpec((tile, D_MODEL), lambda t: (t, 0)),
                   pl.BlockSpec((tile, D_MODEL), lambda t: (t, 0))],
        out_shape=[SDS((seq, D_MODEL), F32), SDS((seq, D_MODEL), BF16)],
        scratch_shapes=[pltpu.VMEM((tile + 8, D_MODEL), F32), pltpu.VMEM((8, D_MODEL), F32),
                        pltpu.VMEM((tile, D_MODEL), F32), pltpu.VMEM((LANE_GROUPS, 8 * pitch, LANES), F32),
                        pltpu.VMEM((LANE_GROUPS, 8 * pitch, LANES), F32)],
        compiler_params=_cparams(("arbitrary",)),
    )(p, p, conv_w, conv_b, wa, wx, ba, bx, lam)


def _lru_backward(p, h, dymix, conv_w, conv_b, wa, wx, ba, bx, lam, seq):
    tile = min(512, seq // 2)
    nt = seq // tile
    seg = tile // 8
    pitch = seg + 4
    t8 = tile // 8

    def body(lx_ref, lxh_ref, gt_ref, h_ref, hh_ref, dy_ref, cw_ref, cb_ref, wa_ref, wx_ref, ba_ref,
             bx_ref, lam_ref, dp_ref, gwa_ref, gwx_ref, gsm_ref,
             lamcar, anext, dxc8, xc_scr, r_scr, i_scr, a_scr, m_scr, rm_scr, l_scr, dxc_scr, c3, l3):
        step = pl.program_id(0)
        first_tile = step == nt - 1

        @pl.when(step == 0)
        def _():
            lamcar[...] = jnp.zeros_like(lamcar)
            anext[...] = jnp.zeros_like(anext)
            dxc8[...] = jnp.zeros_like(dxc8)
            gwa_ref[...] = jnp.zeros_like(gwa_ref)
            gwx_ref[...] = jnp.zeros_like(gwx_ref)
            gsm_ref[...] = jnp.zeros_like(gsm_ref)

        keep = jnp.where(first_tile, 0.0, 1.0)
        lx = lx_ref[...]
        prev8 = lxh_ref[...] * keep
        xc, fix = _conv_taps(lx, prev8, cw_ref, cb_ref, tile)
        xc_scr[...] = xc
        xc_scr[0:8, :] = xc_scr[0:8, :] + fix
        sp, dsp = _softplus_neg(lam_ref[...])
        for n in range(LRU_BLOCKS):
            sl = slice(n * LRU_BW, (n + 1) * LRU_BW)
            r, i, a, ne = _lru_gates(xc_scr[:, sl], wa_ref[n], wx_ref[n], ba_ref[:, sl], bx_ref[:, sl],
                                     sp[:, sl])
            r_scr[:, sl] = r
            i_scr[:, sl] = i
            a_scr[:, sl] = a
            m_scr[:, sl] = jnp.sqrt(ne)
            rm_scr[:, sl] = lax.rsqrt(ne)

        g = gt_ref[...]
        sg = _sigmoid(g)
        dy = dy_ref[...]
        hv = h_ref[...]
        dp_ref[:, D_MODEL:2 * D_MODEL] = (dy * hv * (sg * (1.0 + g * (1.0 - sg)))).astype(BF16)

        rowt = _iota((tile, D_MODEL), 0)
        av = a_scr[...]
        dh = dy * (g * sg)
        cnext = jnp.where(rowt == tile - 1, anext[...][0:1, :], pltpu.roll(av, tile - 1, 0))
        anext[...] = jnp.broadcast_to(av[0:1, :], (8, D_MODEL))
        for gi in range(LANE_GROUPS):
            lanes = slice(gi * LANES, (gi + 1) * LANES)
            for s in range(8):
                l3[gi, s * pitch:s * pitch + seg, :] = dh[s * seg:(s + 1) * seg, lanes]
                c3[gi, s * pitch:s * pitch + seg, :] = cnext[s * seg:(s + 1) * seg, lanes]

        def step(jj, carry):
            rows = pl.ds(seg - 1 - jj, 8, stride=pitch)
            out = []
            for gi in range(LANE_GROUPS):
                ll, cl = carry[gi]
                c = c3.at[gi][rows, :]
                ll = c * ll + l3.at[gi][rows, :]
                cl = c * cl
                l3.at[gi][rows, :] = ll
                c3.at[gi][rows, :] = cl
                out.append((ll, cl))
            return tuple(out)

        init = tuple((jnp.zeros((8, LANES), F32), jnp.ones((8, LANES), F32)) for _ in range(LANE_GROUPS))
        fin = lax.fori_loop(0, seg, step, init)
        ll = jnp.concatenate([f[0] for f in fin], axis=1)
        cl = jnp.concatenate([f[1] for f in fin], axis=1)
        row8 = _iota((8, D_MODEL), 0)
        for k in (1, 2, 4):
            m = row8 < 8 - k
            ll = jnp.where(m, ll + cl * pltpu.roll(ll, 8 - k, 0), ll)
            cl = jnp.where(m, cl * pltpu.roll(cl, 8 - k, 0), cl)
        firsts = ll + cl * lamcar[...]
        after = jnp.where(row8 == 7, lamcar[...], pltpu.roll(firsts, 7, 0))
        lamcar[...] = jnp.broadcast_to(firsts[0:1, :], (8, D_MODEL))
        for s in range(8):
            for gi in range(LANE_GROUPS):
                lanes = slice(gi * LANES, (gi + 1) * LANES)
                l_scr[s * seg:(s + 1) * seg, lanes] = (l3[gi, s * pitch:s * pitch + seg, :]
                                                        + c3[gi, s * pitch:s * pitch + seg, :] * after[s:s + 1, lanes])

        hprev = jnp.where(rowt == 0, hh_ref[...][7:8, :] * keep, pltpu.roll(hv, 1, 0))
        for n in range(LRU_BLOCKS):
            sl = slice(n * LRU_BW, (n + 1) * LRU_BW)
            lamv = l_scr[:, sl]
            xcs = xc_scr[:, sl]
            r = r_scr[:, sl]
            i = i_scr[:, sl]
            a = a_scr[:, sl]
            mult = m_scr[:, sl]
            d_la = lamv * hprev[:, sl] * a - (lamv * i * xcs) * (a * a * rm_scr[:, sl])
            d_pr = d_la * (-LRU_C * sp[:, sl]) * r * (1.0 - r)
            d_pi = (lamv * mult * xcs) * i * (1.0 - i)
            gsm_ref[7:8, sl] += jnp.sum(d_la * r, axis=0, keepdims=True) * (-LRU_C) * dsp[:, sl]
            gsm_ref[5:6, sl] += jnp.sum(d_pr, axis=0, keepdims=True)
            gsm_ref[6:7, sl] += jnp.sum(d_pi, axis=0, keepdims=True)
            xb = xcs.astype(BF16)
            prb = d_pr.astype(BF16)
            pib = d_pi.astype(BF16)
            gwa_ref[n] += _dot(xb, prb, TN)
            gwx_ref[n] += _dot(xb, pib, TN)
            dxc_scr[:, sl] = lamv * mult * i + _dot(prb, wa_ref[n], NT) + _dot(pib, wx_ref[n], NT)

        dxc = dxc_scr[...]
        gsm_ref[4:5, :] += jnp.sum(dxc, axis=0, keepdims=True)
        last8 = lx[tile - 8:tile, :]
        first8 = dxc[0:8, :]
        dlx = cw_ref[3:4, :] * dxc
        gsm_ref[3:4, :] += jnp.sum(dxc * lx, axis=0, keepdims=True)
        fix = jnp.zeros((8, D_MODEL), F32)
        for j in (1, 2, 3):
            w = cw_ref[3 - j:4 - j, :]
            dlx = dlx + w * pltpu.roll(dxc, tile - j, 0)
            fix = fix + w * jnp.where(row8 + j >= 8,
                                      pltpu.roll(dxc8[...], 8 - j, 0) - pltpu.roll(first8, 8 - j, 0), 0.0)
            halo = jnp.where(row8 < j, pltpu.roll(prev8, j, 0) - pltpu.roll(last8, j, 0), 0.0)
            gsm_ref[3 - j:4 - j, :] += (jnp.sum(dxc * pltpu.roll(lx, j, 0), axis=0, keepdims=True)
                                        + jnp.sum(first8 * halo, axis=0, keepdims=True))
        dxc8[...] = first8
        dp_ref[:, 0:D_MODEL] = dlx.astype(BF16)
        top = tile - 8
        dp_ref[top:tile, 0:D_MODEL] = (dlx[top:tile, :] + fix).astype(BF16)

    rev = lambda t: (nt - 1 - t, 0)
    halo_idx = lambda t: (jnp.maximum((nt - 1 - t) * t8 - 1, 0), 0)
    full = lambda shape: pl.BlockSpec(shape, lambda t: (0,) * len(shape))
    big = lambda: pltpu.VMEM((tile, D_MODEL), F32)
    return pl.pallas_call(
        body, name="lru_bwd", grid=(nt,),
        in_specs=[pl.BlockSpec((tile, D_MODEL), rev),
                  pl.BlockSpec((8, D_MODEL), halo_idx),
                  pl.BlockSpec((tile, D_MODEL), lambda t: (nt - 1 - t, 1)),
                  pl.BlockSpec((tile, D_MODEL), rev),
                  pl.BlockSpec((8, D_MODEL), halo_idx),
                  pl.BlockSpec((tile, D_MODEL), rev),
                  full((4, D_MODEL)), full((1, D_MODEL)),
                  full((LRU_BLOCKS, LRU_BW, LRU_BW)), full((LRU_BLOCKS, LRU_BW, LRU_BW)),
                  full((1, D_MODEL)), full((1, D_MODEL)), full((1, D_MODEL))],
        out_specs=[pl.BlockSpec((tile, 2 * D_MODEL), rev),
                   full((LRU_BLOCKS, LRU_BW, LRU_BW)), full((LRU_BLOCKS, LRU_BW, LRU_BW)),
                   full((8, D_MODEL))],
        out_shape=[SDS((seq, D_IN), BF16), SDS((LRU_BLOCKS, LRU_BW, LRU_BW), F32),
                   SDS((LRU_BLOCKS, LRU_BW, LRU_BW), F32), SDS((8, D_MODEL), F32)],
        scratch_shapes=[pltpu.VMEM((8, D_MODEL), F32), pltpu.VMEM((8, D_MODEL), F32),
                        pltpu.VMEM((8, D_MODEL), F32)] + [big() for _ in range(8)]
                       + [pltpu.VMEM((LANE_GROUPS, 8 * pitch, LANES), F32)] * 2,
        compiler_params=_cparams(("arbitrary",)),
    )(p, p, p, h, h, dymix, conv_w, conv_b, wa, wx, ba, bx, lam)


def _tri_matmul(tri, g):
    hi = g.astype(BF16)
    lo = (g - hi.astype(F32)).astype(BF16)
    return _dot(tri, lo) + _dot(tri, hi)


def _hgrn_gate_terms(q, fr, lbl):
    lb = _sigmoid_pos(lbl[0:1, :] - lbl[1:2, :])
    half = 0.5 * (1.0 - lb)
    tf = jnp.tanh(0.5 * fr)
    f = (lb + half) + half * tf
    hq = 0.5 * q
    tq = jnp.tanh(hq)
    return lb, tf, f, tq, hq * tq + hq


def _hgrn_decay(bh):
    zero = jnp.zeros((1, bh.shape[1]), F32)
    rho = [zero] + [bh[s * SUB - 1:s * SUB, :] for s in range(1, N_SUB + 1)]
    start = _sub_rows(rho[0:N_SUB])
    end = _sub_rows(rho[1:N_SUB + 1])
    mid = 0.5 * (start + end)
    blast = rho[N_SUB]
    e_on = jnp.exp(bh - start)
    e_off = jnp.exp(end - bh)
    scales = [_sub_rows([jnp.exp(rho[i] - rho[j + 1]) if i > j else zero for i in range(N_SUB)])
              for j in range(N_SUB - 1)]
    return dict(eq0=jnp.exp(jnp.minimum(bh - mid, EXP_CLAMP)), ek0=jnp.exp(jnp.minimum(mid - bh, EXP_CLAMP)),
                e_on=e_on, e_off=e_off, scales=scales,
                eb=e_on * _sub_rows([jnp.exp(r) for r in rho[0:N_SUB]]),
                ekst=e_off * _sub_rows([jnp.exp(blast - r) for r in rho[1:N_SUB + 1]]),
                ebl=jnp.exp(blast))


def _sub_rows(vecs):
    return jnp.concatenate([jnp.broadcast_to(v, (SUB, v.shape[1])) for v in vecs], axis=0)


def _hgrn_operands(qs, k, dec, qt_scr, kt_scr):
    sub = jnp.right_shift(_iota(qs.shape, 0), 5)
    qon = qs * dec["e_on"]
    koff = k * dec["e_off"]
    qt_scr[0] = (qs * dec["eq0"]).astype(BF16)
    kt_scr[0] = (k * dec["ek0"]).astype(BF16)
    for j in range(N_SUB - 1):
        qt_scr[j + 1] = (qon * dec["scales"][j]).astype(BF16)
        kt_scr[j + 1] = jnp.where(sub == j, koff, 0.0).astype(BF16)
    return koff


def _hgrn_head_scores(qt_scr, kt_scr, sl, diag):
    a = jnp.where(diag, _dot(qt_scr[0, :, sl], kt_scr[0, :, sl], NT), 0.0)
    for j in range(1, N_SUB):
        a = a + _dot(qt_scr[j, :, sl], kt_scr[j, :, sl], NT)
    return a


def _hgrn_forward(p, lbl, gw, seq):
    nc = seq // CHUNK
    assert SUB == 32

    def body(q_ref, f_ref, v_ref, hg_ref, lbl_ref, gw_ref, y_ref, o_ref, st_ref,
             s_scr, qt_scr, kt_scr, qin_scr, kst_scr, vb_scr, a_scr):
        @pl.when(pl.program_id(0) == 0)
        def _():
            s_scr[...] = jnp.zeros_like(s_scr)

        r = _iota((CHUNK, CHUNK), 0)
        c = _iota((CHUNK, CHUNK), 1)
        tri = jnp.where(c <= r, 1.0, 0.0).astype(BF16)
        diag = (jnp.right_shift(r, 5) == jnp.right_shift(c, 5)) & (c <= r)
        heads = [slice(h * HEAD_D, (h + 1) * HEAD_D) for h in range(N_HEADS)]
        for cc in range(HGRN_FWD_STEP_CHUNKS):
            rows = slice(cc * CHUNK, (cc + 1) * CHUNK)
            q = q_ref[rows, :]
            _, _, f, _, qs = _hgrn_gate_terms(q, f_ref[rows, :], lbl_ref[...])
            k = 1.0 - f
            dec = _hgrn_decay(_tri_matmul(tri, jnp.log(f)))
            _hgrn_operands(qs, k, dec, qt_scr, kt_scr)
            qin_scr[...] = (qs * dec["eb"]).astype(BF16)
            kst_scr[...] = (k * dec["ekst"]).astype(BF16)
            vb_scr[...] = v_ref[rows, :].astype(BF16)
            ebl = dec["ebl"]
            hg = hg_ref[rows, :]
            gate = gw_ref[...] * (hg * _sigmoid(hg))
            stb = []
            for h, sl in enumerate(heads):
                st = s_scr[h]
                st_ref[cc, h] = st
                stb.append(st.astype(BF16))
                s_scr[h] = st * ebl[:, sl] + _dot(vb_scr[:, sl], kst_scr[:, sl], TN)
            for h, sl in enumerate(heads):
                a_scr[h] = _hgrn_head_scores(qt_scr, kt_scr, sl, diag).astype(BF16)
            for h, sl in enumerate(heads):
                o = _dot(a_scr[h], vb_scr[:, sl]) + _dot(qin_scr[:, sl], stb[h], NT)
                o_ref[rows, sl] = o
                rs = lax.rsqrt(jnp.mean(o * o, axis=-1, keepdims=True) + EPS)
                y_ref[rows, sl] = ((o * rs) * gate[:, sl]).astype(BF16)

    col = lambda j: pl.BlockSpec((HGRN_FWD_STEP_CHUNKS * CHUNK, D_MODEL), lambda c: (c, j))
    par = lambda rows: pl.BlockSpec((rows, D_MODEL), lambda c: (0, 0))
    return pl.pallas_call(
        body, name="hgrn_fwd", grid=(nc // HGRN_FWD_STEP_CHUNKS,),
        in_specs=[col(2), col(3), col(4), col(5), par(2), par(1)],
        out_specs=[col(0), col(0),
                   pl.BlockSpec((HGRN_FWD_STEP_CHUNKS, N_HEADS, HEAD_D, HEAD_D), lambda c: (c, 0, 0, 0))],
        out_shape=[SDS((seq, D_MODEL), BF16), SDS((seq, D_MODEL), F32),
                   SDS((nc, N_HEADS, HEAD_D, HEAD_D), F32)],
        scratch_shapes=[pltpu.VMEM((N_HEADS, HEAD_D, HEAD_D), F32),
                        pltpu.VMEM((N_SUB, CHUNK, D_MODEL), BF16), pltpu.VMEM((N_SUB, CHUNK, D_MODEL), BF16)]
                       + [pltpu.VMEM((CHUNK, D_MODEL), BF16)] * 3 + [pltpu.VMEM((N_HEADS, CHUNK, CHUNK), BF16)],
        compiler_params=_cparams(("arbitrary",)),
    )(p, p, p, p, lbl, gw)


def _hgrn_backward(p, o, states, dymix, lbl, gw, dp_full, g_w_out, g_small, seq):
    step_rows = HGRN_STEP_CHUNKS * CHUNK
    ns = seq // step_rows

    def body(q_ref, f_ref, v_ref, hg_ref, o_ref, st_ref, dy_ref, lbl_ref, gw_ref, dpin_ref, go_ref, gs_ref,
             dpo_ref, gsm_ref, ro_ref, rs_ref, ds_scr, dp_buf, dp_sems, *rest):
        del dpin_ref
        scratch, sems = rest[:14], rest[14:]
        step = pl.program_id(0)
        slot = step % 2
        exs = [_SlotExchange(go_ref, ro_ref, *sems[0:3], blocked=True),
               _SlotExchange(gs_ref, rs_ref, *sems[3:6], blocked=True)]

        @pl.when(step == 0)
        def _():
            for ex in exs:
                ex.start()

        def out_copy(s, blk):
            rows = pl.ds(pl.multiple_of(blk * step_rows, step_rows), step_rows)
            return pltpu.make_async_copy(dp_buf.at[s], dpo_ref.at[rows, pl.ds(2 * D_MODEL, 4 * D_MODEL)],
                                         dp_sems.at[s])

        @pl.when(step == 0)
        def _():
            ds_scr[...] = jnp.zeros_like(ds_scr)
            gsm_ref[...] = jnp.zeros_like(gsm_ref)

        @pl.when(step >= 2)
        def _():
            out_copy(slot, ns + 1 - step).wait()

        for cc in reversed(range(HGRN_STEP_CHUNKS)):
            chunk(cc, q_ref, f_ref, v_ref, hg_ref, o_ref, st_ref, dy_ref, lbl_ref, gw_ref, gsm_ref, ds_scr,
                  dp_buf.at[slot], *scratch)

        out_copy(slot, ns - 1 - step).start()

        @pl.when(step == ns - 1)
        def _():
            out_copy(1 - slot, 1).wait()
            out_copy(slot, 0).wait()
            for ex in exs:
                ex.wait()

    def chunk(cc, q_ref, f_ref, v_ref, hg_ref, o_ref, st_ref, dy_ref, lbl_ref, gw_ref, gsm_ref, ds_scr, dp_ref,
              qt_scr, kt_scr, qin_scr, kst_scr, vb_scr, dob_scr, g_scr, h_scr, dqi_scr, dks_scr, sd_scr,
              a_scr, da_scr, da0_scr):
        rows = slice(cc * CHUNK, (cc + 1) * CHUNK)
        r = _iota((CHUNK, CHUNK), 0)
        c = _iota((CHUNK, CHUNK), 1)
        tri = jnp.where(c <= r, 1.0, 0.0).astype(BF16)
        triu = jnp.where(c >= r, 1.0, 0.0).astype(BF16)
        diag = (jnp.right_shift(r, 5) == jnp.right_shift(c, 5)) & (c <= r)
        row = _iota((CHUNK, D_MODEL), 0)
        sub = jnp.right_shift(row, 5)

        q = q_ref[rows, :]
        lb, tf, f, tq, qs = _hgrn_gate_terms(q, f_ref[rows, :], lbl_ref[...])
        sig = 0.5 * tf + 0.5
        sq = 0.5 * tq + 0.5
        k = 1.0 - f
        dec = _hgrn_decay(_tri_matmul(tri, jnp.log(f)))
        eb, ekst, ebl = dec["eb"], dec["ekst"], dec["ebl"]
        koff = _hgrn_operands(qs, k, dec, qt_scr, kt_scr)
        qin_scr[...] = (qs * eb).astype(BF16)
        kst_scr[...] = (k * ekst).astype(BF16)
        vb_scr[...] = v_ref[rows, :].astype(BF16)
        hg = hg_ref[rows, :]
        sh = _sigmoid(hg)
        dy = dy_ref[rows, :]
        gwv = gw_ref[...]
        d_onw = dy * (hg * sh)
        d_on = d_onw * gwv
        d_gate = dy * gwv * (sh * (1.0 + hg * (1.0 - sh)))

        heads = [slice(h * HEAD_D, (h + 1) * HEAD_D) for h in range(N_HEADS)]
        for h, sl in enumerate(heads):
            o = o_ref[rows, sl]
            rs = lax.rsqrt(jnp.mean(o * o, axis=-1, keepdims=True) + EPS)
            on = o * rs
            dp_ref[rows, 3 * D_MODEL + h * HEAD_D:3 * D_MODEL + (h + 1) * HEAD_D] = (d_gate[:, sl] * on).astype(BF16)
            gsm_ref[1:2, sl] += jnp.sum(d_onw[:, sl] * on, axis=0, keepdims=True)
            d_onh = d_on[:, sl]
            dob_scr[:, sl] = (rs * (d_onh - on * jnp.mean(d_onh * on, axis=-1, keepdims=True))).astype(BF16)
        for h, sl in enumerate(heads):
            a_scr[h] = _hgrn_head_scores(qt_scr, kt_scr, sl, diag).astype(BF16)
            da = _dot(dob_scr[:, sl], vb_scr[:, sl], NT)
            da_scr[h] = da.astype(BF16)
            da0_scr[h] = jnp.where(diag, da, 0.0).astype(BF16)
        for h, sl in enumerate(heads):
            st = st_ref[cc, h]
            dst = ds_scr[h]
            dstb = dst.astype(BF16)
            dp_ref[rows, 2 * D_MODEL + h * HEAD_D:2 * D_MODEL + (h + 1) * HEAD_D] = (
                _dot(a_scr[h], dob_scr[:, sl], TN) + _dot(kst_scr[:, sl], dstb, NT)).astype(BF16)
            dqi_scr[:, sl] = _dot(dob_scr[:, sl], st.astype(BF16))
            dks_scr[:, sl] = _dot(vb_scr[:, sl], dstb)
            sd_scr[0:1, sl] = jnp.sum(st * dst, axis=0, keepdims=True)
            ds_scr[h] = dst * ebl[:, sl] + _dot(dob_scr[:, sl], qin_scr[:, sl], TN)
        for h, sl in enumerate(heads):
            g_scr[0, :, sl] = _dot(da0_scr[h], kt_scr[0, :, sl])
            h_scr[0, :, sl] = _dot(da0_scr[h], qt_scr[0, :, sl], TN)
            for j in range(1, N_SUB):
                g_scr[j, :, sl] = _dot(da_scr[h], kt_scr[j, :, sl])
                h_scr[j, :, sl] = _dot(da_scr[h], qt_scr[j, :, sl], TN)

        g0 = g_scr[0]
        h0 = h_scr[0]
        dq_inter = eb * dqi_scr[...]
        d_kst = ekst * dks_scr[...]
        db = qs * dq_inter - k * d_kst + qt_scr[0].astype(F32) * g0 - kt_scr[0].astype(F32) * h0
        gq = jnp.zeros((CHUNK, D_MODEL), F32)
        hsel = jnp.zeros((CHUNK, D_MODEL), F32)
        for j in range(N_SUB - 1):
            gj = g_scr[j + 1]
            gq = gq + dec["scales"][j] * gj
            db = db + qt_scr[j + 1].astype(F32) * gj
            hsel = jnp.where(sub == j, h_scr[j + 1], hsel)
        db = db - koff.astype(BF16).astype(F32) * hsel
        d_q = dec["eq0"] * g0 + dec["e_on"] * gq + dq_inter
        d_k = dec["ek0"] * h0 + dec["e_off"] * hsel + d_kst
        db_last = jnp.sum(k * d_kst, axis=0, keepdims=True) + ebl * sd_scr[0:1, :]
        db = db + jnp.where(row == CHUNK - 1, db_last, 0.0)
        dg = _tri_matmul(triu, db)
        d_f = dg / f - d_k
        dp_ref[rows, D_MODEL:2 * D_MODEL] = (d_f * (1.0 - lb) * sig * (1.0 - sig)).astype(BF16)
        gsm_ref[0:1, :] += jnp.sum(d_f * (1.0 - sig), axis=0, keepdims=True) * (lb * (1.0 - lb))
        dp_ref[rows, 0:D_MODEL] = (d_q * (sq * (1.0 + q * (1.0 - sq)))).astype(BF16)

    rc = lambda c: ns - 1 - c
    col = lambda j: pl.BlockSpec((step_rows, D_MODEL), lambda c: (rc(c), j))
    par = lambda rows: pl.BlockSpec((rows, D_MODEL), lambda c: (0, 0))
    return pl.pallas_call(
        body, name="hgrn_bwd", grid=(ns,),
        in_specs=[col(2), col(3), col(4), col(5), col(0),
                  pl.BlockSpec((HGRN_STEP_CHUNKS, N_HEADS, HEAD_D, HEAD_D), lambda c: (rc(c), 0, 0, 0)),
                  col(1), par(2), par(1), ANY, ANY, ANY],
        out_specs=[ANY, par(8), ANY, ANY],
        out_shape=[SDS((seq, D_IN), BF16), SDS((8, D_MODEL), F32), SDS(g_w_out.shape, F32),
                   SDS(g_small.shape, F32)],
        input_output_aliases={9: 0},
        scratch_shapes=[pltpu.VMEM((N_HEADS, HEAD_D, HEAD_D), F32),
                        pltpu.VMEM((2, step_rows, 4 * D_MODEL), BF16), pltpu.SemaphoreType.DMA((2,)),
                        pltpu.VMEM((N_SUB, CHUNK, D_MODEL), BF16), pltpu.VMEM((N_SUB, CHUNK, D_MODEL), BF16)]
                       + [pltpu.VMEM((CHUNK, D_MODEL), BF16)] * 4
                       + [pltpu.VMEM((N_SUB, CHUNK, D_MODEL), F32)] * 2 + [pltpu.VMEM((CHUNK, D_MODEL), F32)] * 2
                       + [pltpu.VMEM((8, D_MODEL), F32)] + [pltpu.VMEM((N_HEADS, CHUNK, CHUNK), BF16)] * 3
                       + EXCHANGE_SEMS * 2,
        compiler_params=_cparams(("arbitrary",)),
    )(p, p, p, p, o, states, dymix, lbl, gw, dp_full, g_w_out, g_small)


def _out_proj(yl, yh, wo, x, tgt, post_w, seq):
    tm = 512

    def body(yl_ref, yh_ref, wo_ref, x_ref, tg_ref, pw_ref, dymix_ref, dout_ref, gwo_ref, st_ref):
        @pl.when(pl.program_id(0) == 0)
        def _():
            gwo_ref[...] = jnp.zeros_like(gwo_ref)
            st_ref[...] = jnp.zeros_like(st_ref)

        ylv = yl_ref[...]
        yhv = yh_ref[...]
        y = _dot(ylv, wo_ref[0:D_MODEL, :]) + _dot(yhv, wo_ref[D_MODEL:D_MIX, :])
        r2 = lax.rsqrt(jnp.mean(y * y, axis=-1, keepdims=True) + EPS)
        yn = y * r2
        pw = pw_ref[...]
        e = (x_ref[...] + yn * pw) - tg_ref[...]
        st_ref[1:2, :] += jnp.sum(e * e, axis=0, keepdims=True) * (0.5 / D_MODEL)
        dout = e * (1.0 / D_MODEL)
        dout_ref[...] = dout
        st_ref[0:1, :] += jnp.sum(dout * yn, axis=0, keepdims=True)
        dyn = dout * pw
        dy = r2 * (dyn - yn * jnp.mean(dyn * yn, axis=-1, keepdims=True))
        dyb = dy.astype(BF16)
        dymix_ref[...] = _dot(dyb, wo_ref[...], NT)
        gwo_ref[0:D_MODEL, :] += _dot(ylv, dyb, TN)
        gwo_ref[D_MODEL:D_MIX, :] += _dot(yhv, dyb, TN)

    row = lambda w: pl.BlockSpec((tm, w), lambda m: (m, 0))
    full = lambda shape: pl.BlockSpec(shape, lambda m: (0,) * len(shape))
    once = lambda shape: pl.BlockSpec(shape, lambda m: (0,) * len(shape), pipeline_mode=pl.Buffered(1))
    return pl.pallas_call(
        body, name="out_proj", grid=(seq // tm,),
        in_specs=[row(D_MODEL), row(D_MODEL), once((D_MIX, D_MODEL)), row(D_MODEL), row(D_MODEL),
                  full((1, D_MODEL))],
        out_specs=[row(D_MIX), row(D_MODEL), once((D_MIX, D_MODEL)), full((8, D_MODEL))],
        out_shape=[SDS((seq, D_MIX), F32), SDS((seq, D_MODEL), F32), SDS((D_MIX, D_MODEL), F32),
                   SDS((8, D_MODEL), F32)],
        compiler_params=_cparams(("arbitrary",)),
    )(yl, yh, wo, x, tgt, post_w)


MESH = pl.DeviceIdType.MESH
ANY = pl.BlockSpec(memory_space=pl.ANY)
EXCHANGE_SEMS = [pltpu.SemaphoreType.DMA((N_DEV - 1,)), pltpu.SemaphoreType.DMA((N_DEV - 1,)),
                 pltpu.SemaphoreType.DMA(())]


def _mesh_pos():
    return lax.axis_index("x"), lax.axis_index("y"), lax.axis_index("c")


class _SlotExchange:
    def __init__(self, src_ref, dst_ref, send_sems, recv_sems, local_sem, blocked):
        x, y, c = _mesh_pos()
        me = 4 * x + 2 * y + c
        src = (lambda dest: src_ref.at[dest]) if blocked else (lambda dest: src_ref)
        self.local = pltpu.make_async_copy(src(me), dst_ref.at[me], local_sem)
        self.sends, self.recvs = [], []
        for k in range(1, N_DEV):
            px = 1 - x if (k >> 2) & 1 else x
            py = 1 - y if (k >> 1) & 1 else y
            pc = 1 - c if k & 1 else c
            peer = 4 * px + 2 * py + pc
            sems = dict(send_sem=send_sems.at[k - 1], recv_sem=recv_sems.at[k - 1],
                        device_id=(px, py, pc), device_id_type=MESH)
            self.sends.append(pltpu.make_async_remote_copy(src_ref=src(peer), dst_ref=dst_ref.at[me], **sems))
            self.recvs.append(pltpu.make_async_remote_copy(src_ref=dst_ref.at[peer], dst_ref=dst_ref.at[peer], **sems))

    def start(self):
        self.local.start()
        for cp in self.sends:
            cp.start()

    def wait(self):
        for cp in self.recvs:
            cp.wait_recv()
        for cp in self.sends:
            cp.wait_send()
        self.local.wait()


class _ChipExchange:
    def __init__(self, src_ref, dst_ref, send_sems, recv_sems, local_sem):
        x, y, c = _mesh_pos()
        chip = 2 * x + y
        self.local = pltpu.make_async_copy(src_ref.at[chip], dst_ref.at[chip], local_sem)
        self.sends, self.recvs = [], []
        for k in range(1, N_CHIPS):
            px = 1 - x if (k >> 1) & 1 else x
            py = 1 - y if k & 1 else y
            peer = 2 * px + py
            sems = dict(send_sem=send_sems.at[k - 1], recv_sem=recv_sems.at[k - 1],
                        device_id=(px, py, c), device_id_type=MESH)
            self.sends.append(pltpu.make_async_remote_copy(src_ref=src_ref.at[peer], dst_ref=dst_ref.at[chip], **sems))
            self.recvs.append(pltpu.make_async_remote_copy(src_ref=dst_ref.at[peer], dst_ref=dst_ref.at[peer], **sems))

    def start(self):
        self.local.start()
        for cp in self.sends:
            cp.start()

    def wait(self):
        for cp in self.recvs:
            cp.wait_recv()
        for cp in self.sends:
            cp.wait_send()
        self.local.wait()


GRAD_W_IN_TK = 2048


def _grad_w_in_sibling(u, dp, core, seq):
    tk = min(GRAD_W_IN_TK, seq)
    nk = seq // tk

    def body(core_ref, u_ref, dp_ref, g_ref):
        del core_ref

        @pl.when(pl.program_id(1) == 0)
        def _():
            g_ref[...] = jnp.zeros_like(g_ref)

        g_ref[0] += _dot(u_ref[...], dp_ref[...], TN)

    return pl.pallas_call(
        body, name="grad_w_in_sibling",
        grid_spec=pltpu.PrefetchScalarGridSpec(
            num_scalar_prefetch=1, grid=(N_CHIPS, nk),
            in_specs=[pl.BlockSpec((tk, D_MODEL), lambda n, k, c: (k, 0)),
                      pl.BlockSpec((tk, W_BLK), lambda n, k, c: (k, 2 * n + 1 - c[0]))],
            out_specs=pl.BlockSpec((1, D_MODEL, W_BLK), lambda n, k, c: (n, 0, 0))),
        out_shape=SDS((N_CHIPS, D_MODEL, W_BLK), F32),
        compiler_params=_cparams(("parallel", "arbitrary")),
    )(core, u, dp)


def _grad_w_in_own(u, dp, core, g_sib, seq):
    tk = min(GRAD_W_IN_TK, seq)
    nk = seq // tk

    def body(core_ref, u_ref, dp_ref, gsib_ref, g_ref, land, send_sem, recv_sem):
        del core_ref
        n = pl.program_id(0)
        k = pl.program_id(1)
        x, y, c = _mesh_pos()
        swap = pltpu.make_async_remote_copy(src_ref=gsib_ref, dst_ref=land, send_sem=send_sem, recv_sem=recv_sem,
                                            device_id=(x, y, 1 - c), device_id_type=MESH)

        @pl.when((n == 0) & (k == 0))
        def _():
            swap.start()

        @pl.when(k == 0)
        def _():
            g_ref[...] = jnp.zeros_like(g_ref)

        g_ref[0] += _dot(u_ref[...], dp_ref[...], TN)

        @pl.when((n == 0) & (k == nk - 1))
        def _():
            swap.wait_recv()

        @pl.when(k == nk - 1)
        def _():
            g_ref[0] += land[n]

        @pl.when((n == N_CHIPS - 1) & (k == nk - 1))
        def _():
            swap.wait_send()

    return pl.pallas_call(
        body, name="grad_w_in_own",
        grid_spec=pltpu.PrefetchScalarGridSpec(
            num_scalar_prefetch=1, grid=(N_CHIPS, nk),
            in_specs=[pl.BlockSpec((tk, D_MODEL), lambda n, k, c: (k, 0)),
                      pl.BlockSpec((tk, W_BLK), lambda n, k, c: (k, 2 * n + c[0])), ANY],
            out_specs=pl.BlockSpec((1, D_MODEL, W_BLK), lambda n, k, c: (n, 0, 0)),
            scratch_shapes=[pltpu.VMEM((N_CHIPS, D_MODEL, W_BLK), F32), pltpu.SemaphoreType.DMA(()),
                            pltpu.SemaphoreType.DMA(())]),
        out_shape=SDS((N_CHIPS, D_MODEL, W_BLK), F32),
        compiler_params=_cparams(("arbitrary", "arbitrary")),
    )(core, u, dp, g_sib)


def _grad_x(dp, w_all, x, pre_w, dout, g_chip, seq):
    tm = 512
    nm = seq // tm

    def body(dp_ref, w_ref, x_ref, pw_ref, do_ref, gsrc_ref, gx_ref, gpw_ref, recv_ref,
             send_sems, recv_sems, local_sem):
        m = pl.program_id(0)
        ex = _ChipExchange(gsrc_ref, recv_ref, send_sems, recv_sems, local_sem)

        @pl.when(m == 0)
        def _():
            ex.start()
            gpw_ref[...] = jnp.zeros_like(gpw_ref)

        du = _dot(dp_ref[:, 0:W_BLK], w_ref[0], NT)
        for j in range(1, N_DEV):
            du = du + _dot(dp_ref[:, j * W_BLK:(j + 1) * W_BLK], w_ref[j], NT)
        xv = x_ref[...]
        r1 = lax.rsqrt(jnp.mean(xv * xv, axis=-1, keepdims=True) + EPS)
        xn = xv * r1
        gpw_ref[0:1, :] += jnp.sum(du * xn, axis=0, keepdims=True)
        dxn = du * pw_ref[...]
        gx_ref[...] = r1 * (dxn - xn * jnp.mean(dxn * xn, axis=-1, keepdims=True)) + do_ref[...]

        @pl.when(m == nm - 1)
        def _():
            ex.wait()

    row = lambda w: pl.BlockSpec((tm, w), lambda m: (m, 0))
    return pl.pallas_call(
        body, name="grad_x", grid=(nm,),
        in_specs=[row(D_IN), pl.BlockSpec((N_DEV, D_MODEL, W_BLK), lambda m: (0, 0, 0), pipeline_mode=pl.Buffered(1)),
                  row(D_MODEL),
                  pl.BlockSpec((1, D_MODEL), lambda m: (0, 0)), row(D_MODEL), ANY],
        out_specs=[row(D_MODEL), pl.BlockSpec((8, D_MODEL), lambda m: (0, 0)), ANY],
        out_shape=[SDS((seq, D_MODEL), F32), SDS((8, D_MODEL), F32), SDS(g_chip.shape, F32)],
        scratch_shapes=[pltpu.SemaphoreType.DMA((N_CHIPS - 1,)), pltpu.SemaphoreType.DMA((N_CHIPS - 1,)),
                        pltpu.SemaphoreType.DMA(())],
        compiler_params=_cparams(("arbitrary",)),
    )(dp, w_all, x, pre_w, dout, g_chip)


def _local_step(x, tgt, p, u, conv_w, conv_b, wa, wx, ba, bx, lam, lbl, gnorm_w, w_out, post_w):
    seq = x.shape[0]
    h, y_lru = _lru_forward(p, conv_w, conv_b, wa, wx, ba, bx, lam, seq)
    y_hgrn, o, states = _hgrn_forward(p, lbl, gnorm_w, seq)
    dymix, dout, g_w_out, stats = _out_proj(y_lru, y_hgrn, w_out, x, tgt, post_w, seq)
    dp_lru, g_wa, g_wx, ls = _lru_backward(p, h, dymix, conv_w, conv_b, wa, wx, ba, bx, lam, seq)
    g_small = _pack_small(_shard_rows(g_wa, LRU_BLOCKS), _shard_rows(g_wx, LRU_BLOCKS),
                          _shard_rows(ls[0:4].reshape(4, D_MODEL, 1), 4).reshape(N_DEV, 4, 128),
                          _shard_rows(ls[5].reshape(4, LRU_BW, 1), 4).reshape(N_DEV, 4, 32),
                          _shard_rows(ls[6].reshape(4, LRU_BW, 1), 4).reshape(N_DEV, 4, 32))
    dp, hgrn_small, r_out, r_small = _hgrn_backward(
        p, o, states, dymix, lbl, gnorm_w, dp_lru, g_w_out.reshape(N_DEV, D_MIX // N_DEV, D_MODEL), g_small, seq)
    return dict(u=u, dp=dp, dout=dout, r_out=r_out, r_small=r_small,
                lru_small=ls, hgrn_small=hgrn_small, stats=stats)


class _TwoLevelGather:
    def __init__(self, ins, outs, send_sems, recv_sems, local_sems):
        self.ins, self.outs = ins, outs
        self.send_sems, self.recv_sems, self.local_sems = send_sems, recv_sems, local_sems
        x, y, c = _mesh_pos()
        self.c = c
        self.me, self.sibling = (x, y, c), (x, y, 1 - c)
        self.chips = [(1 - x, y), (x, 1 - y), (1 - x, 1 - y)]
        n = len(ins)
        self.mine = [pltpu.make_async_copy(ins[a], self._slot(a, self.me), local_sems.at[a]) for a in range(n)]
        self.first = []
        for a in range(n):
            self.first.append(self._copy(a, 0, self.me, self.sibling, src=ins[a]))
            self.first += [self._copy(a, 1 + j, self.me, (*chip, c), src=ins[a])
                           for j, chip in enumerate(self.chips)]
        self.passed = [self._copy(a, 4 + j, (*chip, c), self.sibling)
                       for j, chip in enumerate(self.chips) for a in range(n)]

    def _slot(self, a, pos):
        return self.outs[a].at[4 * pos[0] + 2 * pos[1] + pos[2]]

    def _copy(self, a, k, block, to, src=None):
        dst = self._slot(a, block)
        return pltpu.make_async_remote_copy(
            src_ref=dst if src is None else src, dst_ref=dst,
            send_sem=self.send_sems.at[a, k], recv_sem=self.recv_sems.at[a, k],
            device_id=to, device_id_type=MESH)

    def start(self):
        for cp in self.mine + self.first:
            cp.start()

    def forward(self):
        n = len(self.ins)
        for j, chip in enumerate(self.chips):
            for a in range(n):
                self._copy(a, 1 + j, (*chip, self.c), self.me).wait_recv()
                self.passed[j * n + a].start()

    def finish(self):
        for a in range(len(self.ins)):
            self._copy(a, 0, self.sibling, self.me).wait_recv()
            for j, chip in enumerate(self.chips):
                self._copy(a, 4 + j, (*chip, 1 - self.c), self.me).wait_recv()
        for cp in self.first + self.passed:
            cp.wait_send()
        for cp in self.mine:
            cp.wait()


W_IN_DIRECT = (1, 2, 4, 6)
W_IN_PASSED = (2, 4, 6)

def _in_proj_gather(x, pre_w, w_in_blk, w_out_blk, small_blk, me, seq):
    tm = min(1024, seq)
    nm = seq // tm
    last = N_DEV - 1

    def body(me_ref, x_ref, pw_ref, wblk_ref, woblk_ref, smblk_ref,
             p_ref, u_ref, wall_ref, woall_ref, small_ref,
             u_all, w_vmem, own_sem, d_send, d_recv, f_send, f_recv, wb_sems, g_send, g_recv, g_local):
        i = pl.program_id(0)
        m = pl.program_id(1)
        idx = me_ref[0]
        x_, y_, c_ = _mesh_pos()
        aux = _TwoLevelGather([woblk_ref, smblk_ref], [woall_ref, small_ref], g_send, g_recv, g_local)

        def peer(k):
            return (1 - x_ if (k >> 2) & 1 else x_, 1 - y_ if (k >> 1) & 1 else y_, 1 - c_ if k & 1 else c_)

        def direct(k):
            f = W_IN_DIRECT.index(k)
            return (pltpu.make_async_remote_copy(src_ref=wblk_ref, dst_ref=w_vmem.at[idx], send_sem=d_send.at[f],
                                                 recv_sem=d_recv.at[f], device_id=peer(k), device_id_type=MESH),
                    pltpu.make_async_remote_copy(src_ref=w_vmem.at[idx ^ k], dst_ref=w_vmem.at[idx ^ k],
                                                 send_sem=d_send.at[f], recv_sem=d_recv.at[f], device_id=peer(k),
                                                 device_id_type=MESH))

        def passed(k):
            f = W_IN_PASSED.index(k)
            return (pltpu.make_async_remote_copy(src_ref=w_vmem.at[idx ^ k], dst_ref=w_vmem.at[idx ^ k],
                                                 send_sem=f_send.at[f], recv_sem=f_recv.at[f], device_id=peer(1),
                                                 device_id_type=MESH),
                    pltpu.make_async_remote_copy(src_ref=w_vmem.at[idx ^ (k + 1)], dst_ref=w_vmem.at[idx ^ (k + 1)],
                                                 send_sem=f_send.at[f], recv_sem=f_recv.at[f], device_id=peer(1),
                                                 device_id_type=MESH))

        def write_back(k):
            return pltpu.make_async_copy(w_vmem.at[idx ^ k], wall_ref.at[idx ^ k], wb_sems.at[k])

        own = pltpu.make_async_copy(wblk_ref, w_vmem.at[idx], own_sem)

        @pl.when((i == 0) & (m == 0))
        def _():
            own.start()
            for k in (1, 2, 4):
                direct(k)[0].start()
            own.wait()
            write_back(0).start()

        for k in range(1, N_DEV):
            @pl.when((i == k) & (m == 0))
            def _(k=k):
                if k in W_IN_DIRECT:
                    direct(k)[1].wait_recv()
                    if k in W_IN_PASSED:
                        passed(k)[0].start()
                else:
                    passed(k - 1)[1].wait_recv()
                write_back(k).start()
                if k == 2:
                    direct(6)[0].start()
                if k == 4:
                    aux.start()
                if k == N_DEV - 1:
                    aux.forward()

        rows = pl.ds(pl.multiple_of(m * tm, tm), tm)

        @pl.when(i == 0)
        def _():
            xv = x_ref[...]
            r = lax.rsqrt(jnp.mean(xv * xv, axis=-1, keepdims=True) + EPS)
            ub = (xv * r * pw_ref[...]).astype(BF16)
            u_all[rows, :] = ub
            u_ref[...] = ub

        p_ref[...] = _dot(u_all[rows, :], w_vmem[idx ^ i])

        @pl.when((i == last) & (m == nm - 1))
        def _():
            for k in W_IN_DIRECT:
                direct(k)[0].wait_send()
            for k in W_IN_PASSED:
                passed(k)[0].wait_send()
            for k in range(N_DEV):
                write_back(k).wait()
            aux.finish()

    first_pass = lambda i, m: jnp.where(i == 0, m, nm - 1)
    return pl.pallas_call(
        body, name="in_proj_gather",
        grid_spec=pltpu.PrefetchScalarGridSpec(
            num_scalar_prefetch=1, grid=(N_DEV, nm),
            in_specs=[pl.BlockSpec((tm, D_MODEL), lambda i, m, me: (first_pass(i, m), 0)),
                      pl.BlockSpec((1, D_MODEL), lambda i, m, me: (0, 0)), ANY, ANY, ANY],
            out_specs=[pl.BlockSpec((tm, W_BLK), lambda i, m, me: (m, me[0] ^ i)),
                       pl.BlockSpec((tm, D_MODEL), lambda i, m, me: (first_pass(i, m), 0)), ANY, ANY, ANY],
            scratch_shapes=[pltpu.VMEM((seq, D_MODEL), BF16), pltpu.VMEM((N_DEV, D_MODEL, W_BLK), BF16),
                            pltpu.SemaphoreType.DMA(()),
                            pltpu.SemaphoreType.DMA((len(W_IN_DIRECT),)), pltpu.SemaphoreType.DMA((len(W_IN_DIRECT),)),
                            pltpu.SemaphoreType.DMA((len(W_IN_PASSED),)), pltpu.SemaphoreType.DMA((len(W_IN_PASSED),)),
                            pltpu.SemaphoreType.DMA((N_DEV,)),
                            pltpu.SemaphoreType.DMA((2, 7)), pltpu.SemaphoreType.DMA((2, 7)),
                            pltpu.SemaphoreType.DMA((2,))]),
        out_shape=[SDS((seq, D_IN), F32), SDS((seq, D_MODEL), BF16), SDS((N_DEV, D_MODEL, W_BLK), BF16),
                   SDS((N_DEV,) + w_out_blk.shape, w_out_blk.dtype), SDS((N_DEV,) + small_blk.shape, small_blk.dtype)],
        compiler_params=_cparams(("arbitrary", "arbitrary")),
    )(me, x, pre_w, w_in_blk, w_out_blk, small_blk)


def _exchange_grads(blocks, repl):
    nb = len(blocks)
    n = nb + 1

    def body(*refs):
        ins, outs, sems = refs[:n], refs[n:2 * n], refs[2 * n:]
        exs = [_SlotExchange(ins[a], outs[a], *sems[3 * a:3 * a + 3], blocked=a < nb) for a in range(n)]
        for ex in exs:
            ex.start()
        for ex in exs:
            ex.wait()

    arrs = list(blocks) + [repl]
    shapes = [SDS(b.shape, b.dtype) for b in blocks] + [SDS((N_DEV,) + repl.shape, repl.dtype)]
    return pl.pallas_call(
        body, name="exchange_small", out_shape=shapes,
        in_specs=[ANY] * n, out_specs=[ANY] * n,
        scratch_shapes=EXCHANGE_SEMS * n,
    )(*arrs)


def _pack_rows(picks, name):
    arrs = [p[0] for p in picks]

    def body(*refs):
        out = refs[-1]
        out[...] = jnp.zeros_like(out)
        at = 0
        for ref, (_, row, rows, scale) in zip(refs[:-1], picks):
            out[at:at + rows, :] = ref[row:row + rows, :] * scale
            at += rows

    return pl.pallas_call(body, name=name, out_shape=SDS((8, D_MODEL), F32))(*arrs)


def _adamw(g, w, m, v):
    m2 = ADAM_B1 * m + (1.0 - ADAM_B1) * g
    v2 = ADAM_B2 * v + (1.0 - ADAM_B2) * (g * g)
    m_hat = m2 / (1.0 - ADAM_B1 ** ADAM_STEP)
    v_hat = v2 / (1.0 - ADAM_B2 ** ADAM_STEP)
    delta = -ADAM_LR * (m_hat / (jnp.sqrt(v_hat) + ADAM_EPS) + ADAM_WD * w)
    return delta, m2, v2


def _sum_slots(r_ref):
    g = r_ref[0]
    for s in range(1, r_ref.shape[0]):
        g = g + r_ref[s]
    return g


def _sum_adamw(recv, w, m, v, tr, name):
    rows, cols = w.shape

    def body(r_ref, w_ref, m_ref, v_ref, g_ref, d_ref, m2_ref, v2_ref):
        g = _sum_slots(r_ref)
        g_ref[...] = g
        d_ref[...], m2_ref[...], v2_ref[...] = _adamw(g, w_ref[...], m_ref[...], v_ref[...])

    blk = pl.BlockSpec((tr, cols), lambda i: (i, 0))
    return pl.pallas_call(
        body, name=name, grid=(rows // tr,),
        in_specs=[pl.BlockSpec((recv.shape[0], tr, cols), lambda i: (0, i, 0)), blk, blk, blk],
        out_specs=[blk] * 4, out_shape=[SDS((rows, cols), F32)] * 4,
        compiler_params=_cparams(("parallel",)),
    )(recv, w, m, v)


def _sum_adamw_pieces(recv, rows, ws, ms, vs, name, loss_row=None):
    n = len(ws)

    def body(r_ref, *refs):
        w_refs, m_refs, v_refs, outs = refs[:n], refs[n:2 * n], refs[2 * n:3 * n], refs[3 * n:]
        g = _sum_slots(r_ref)
        for i, (row, nrows) in enumerate(rows):
            gi = g[row:row + nrows, 0:ws[i].shape[1]]
            outs[i][...] = gi
            outs[n + i][...], outs[2 * n + i][...], outs[3 * n + i][...] = _adamw(
                gi, w_refs[i][...], m_refs[i][...], v_refs[i][...])
        if loss_row is not None:
            total = jnp.sum(g[loss_row:loss_row + 1, :], axis=-1, keepdims=True)
            outs[4 * n][...] = jnp.broadcast_to(total, outs[4 * n].shape)

    shapes = [SDS(w.shape, F32) for w in ws] * 4 + ([SDS((8, 128), F32)] if loss_row is not None else [])
    out = pl.pallas_call(body, name=name, out_shape=shapes)(recv, *ws, *ms, *vs)
    return [out[k * n:(k + 1) * n] for k in range(4)] + list(out[4 * n:])


def _shard_rows(t, lead):
    r = t.shape[1] // N_DEV
    t = t.reshape((lead, N_DEV, r) + t.shape[2:])
    return jnp.moveaxis(t, 1, 0)


def _pad_tile(t):
    return jnp.pad(t, ((0, 0), (0, 8 - t.shape[1]), (0, SM_LANES - t.shape[2])))


def _pack_small(wa, wx, cw, b_a, b_x):
    n = wa.shape[0]
    return jnp.concatenate([wa.reshape(n, 128, SM_LANES), wx.reshape(n, 128, SM_LANES),
                            _pad_tile(cw), _pad_tile(b_a), _pad_tile(b_x)], axis=1)


def _unpack_small(t):
    n = t.shape[0]
    return (t[:, SM_WA:SM_WA + 128].reshape(n, 4, 32, 256), t[:, SM_WX:SM_WX + 128].reshape(n, 4, 32, 256),
            t[:, SM_CW:SM_CW + 4, 0:128], t[:, SM_BA:SM_BA + 4, 0:32], t[:, SM_BX:SM_BX + 4, 0:32])


def kernel(x, pre_norm_w, w_in, conv_w, conv_b, lru_w_a, lru_b_a, lru_w_x, lru_b_x, lru_lambda, hgrn_lb_logits, hgrn_gnorm_w, w_out, post_norm_w, loss_target, m_pre_norm_w, m_w_in, m_conv_w, m_conv_b, m_lru_w_a, m_lru_b_a, m_lru_w_x, m_lru_b_x, m_lru_lambda, m_hgrn_lb_logits, m_hgrn_gnorm_w, m_w_out, m_post_norm_w, v_pre_norm_w, v_w_in, v_conv_w, v_conv_b, v_lru_w_a, v_lru_b_a, v_lru_w_x, v_lru_b_x, v_lru_lambda, v_hgrn_lb_logits, v_hgrn_gnorm_w, v_w_out, v_post_norm_w):
    seq = x.shape[1]
    x2 = x.reshape(seq, D_MODEL)
    tgt = loss_target.reshape(seq, D_MODEL)

    small_w = _pack_small(lru_w_a, lru_w_x, conv_w, lru_b_a, lru_b_x)[0]
    me = (4 * lax.axis_index("x") + 2 * lax.axis_index("y") + lax.axis_index("c")).astype(jnp.int32).reshape(1)
    p, u, w_in_all, w_out_all, small_all = _in_proj_gather(
        x2, pre_norm_w, w_in[0].astype(BF16), w_out[0].astype(BF16), small_w, me, seq)
    wa_s, wx_s, cw_s, ba_s, bx_s = _unpack_small(small_all)
    wa = jnp.moveaxis(wa_s, 0, 1).reshape(LRU_BLOCKS, LRU_BW, LRU_BW).astype(BF16)
    wx = jnp.moveaxis(wx_s, 0, 1).reshape(LRU_BLOCKS, LRU_BW, LRU_BW).astype(BF16)
    cw = jnp.moveaxis(cw_s, 0, 1).reshape(4, D_MODEL)
    ba = jnp.moveaxis(ba_s, 0, 1).reshape(1, D_MODEL)
    bx = jnp.moveaxis(bx_s, 0, 1).reshape(1, D_MODEL)

    loc = _local_step(x2, tgt, p, u, cw, conv_b, wa, wx, ba, bx, lru_lambda,
                      hgrn_lb_logits, hgrn_gnorm_w, w_out_all.reshape(D_MIX, D_MODEL), post_norm_w)

    ls, r_out, r_small = loc["lru_small"], loc["r_out"], loc["r_small"]
    core = lax.axis_index("c").astype(jnp.int32).reshape(1)
    g_sib = _grad_w_in_sibling(loc["u"], loc["dp"], core, seq)
    g_chip = _grad_w_in_own(loc["u"], loc["dp"], core, g_sib, seq)
    grad_x, pre_small, r_in = _grad_x(loc["dp"], w_in_all, x2, pre_norm_w, loc["dout"], g_chip, seq)
    g_repl = _pack_rows([(pre_small, 0, 1, 1.0), (ls, 4, 1, 1.0), (ls, 7, 1, 1.0),
                         (loc["hgrn_small"], 0, 1, 1.0), (loc["hgrn_small"], 0, 1, -1.0),
                         (loc["hgrn_small"], 1, 1, 1.0), (loc["stats"], 0, 2, 1.0)], "pack_grads")
    (r_repl,) = _exchange_grads([], g_repl)

    repl_rows = [(RP_PRE, 1), (RP_CB, 1), (RP_LAM, 1), (RP_LB0, 2), (RP_GN, 1), (RP_POST, 1)]
    o_repl = _sum_adamw_pieces(
        r_repl, repl_rows,
        [pre_norm_w, conv_b, lru_lambda, hgrn_lb_logits, hgrn_gnorm_w, post_norm_w],
        [m_pre_norm_w, m_conv_b, m_lru_lambda, m_hgrn_lb_logits, m_hgrn_gnorm_w, m_post_norm_w],
        [v_pre_norm_w, v_conv_b, v_lru_lambda, v_hgrn_lb_logits, v_hgrn_gnorm_w, v_post_norm_w],
        "adamw_repl", loss_row=RP_LOSS)
    loss = o_repl[4][0, 0]

    small_rows = [(SM_WA, 128), (SM_WX, 128), (SM_CW, 4), (SM_BA, 4), (SM_BX, 4)]
    as_rows = lambda wa_, wx_, cw_, ba_, bx_: [wa_.reshape(128, 256), wx_.reshape(128, 256), cw_.reshape(4, 128),
                                               ba_.reshape(4, 32), bx_.reshape(4, 32)]
    o_small = _sum_adamw_pieces(
        r_small, small_rows, as_rows(lru_w_a, lru_w_x, conv_w, lru_b_a, lru_b_x),
        as_rows(m_lru_w_a, m_lru_w_x, m_conv_w, m_lru_b_a, m_lru_b_x),
        as_rows(v_lru_w_a, v_lru_w_x, v_conv_w, v_lru_b_a, v_lru_b_x), "adamw_small")

    o_in = _sum_adamw(r_in, w_in[0], m_w_in[0], v_w_in[0], 128, "adamw_w_in")
    o_out = _sum_adamw(r_out, w_out[0], m_w_out[0], v_w_out[0], 64, "adamw_w_out")

    outs = [loss, grad_x.reshape(x.shape)]
    for kind in range(4):
        pre, cb, lam, lb, gn, post = o_repl[kind]
        swa, swx, scw, sba, sbx = o_small[kind]
        outs += [pre, o_in[kind][None], scw.reshape(conv_w.shape), cb, swa.reshape(lru_w_a.shape),
                 sba.reshape(lru_b_a.shape), swx.reshape(lru_w_x.shape), sbx.reshape(lru_b_x.shape),
                 lam, lb, gn, o_out[kind][None], post]
    return tuple(outs)
```

```python
import jax
import jax.numpy as jnp
from jax import lax
from jax.experimental import pallas as pl
from jax.experimental.pallas import tpu as pltpu

F32 = jnp.float32
BF16 = jnp.bfloat16
SDS = jax.ShapeDtypeStruct

D_MODEL = 1024
D_IN = 6144
N_DEV = 8
N_CHIPS = 4
W_BLK = D_IN // N_DEV
D_MIX = 2048
LRU_BLOCKS = 4
LRU_BW = 256
LRU_C = 8.0
LANES = 128
LANE_GROUPS = D_MODEL // LANES
N_HEADS = 8
HEAD_D = 128
CHUNK = 128
SUB = 32
N_SUB = CHUNK // SUB
HGRN_FWD_STEP_CHUNKS = 4
HGRN_STEP_CHUNKS = 2
EXP_CLAMP = 80.0
EPS = 1e-6

ADAM_LR = 0.001
ADAM_B1 = 0.9
ADAM_B2 = 0.999
ADAM_EPS = 1e-08
ADAM_WD = 0.01
ADAM_STEP = 10

VMEM_LIMIT = 56 * 1024 * 1024

NN = (((1,), (0,)), ((), ()))
NT = (((1,), (1,)), ((), ()))
TN = (((0,), (0,)), ((), ()))

SM_LANES = 256
SM_WA = 0
SM_WX = 128
SM_CW = 256
SM_BA = 264
SM_BX = 272
SM_ROWS = 280

RP_PRE, RP_CB, RP_LAM, RP_LB0, RP_LB1, RP_GN, RP_POST, RP_LOSS = range(8)


def _dot(a, b, dims=NN):
    return lax.dot_general(a, b, dims, preferred_element_type=F32)


def _sigmoid(x):
    return 0.5 * jnp.tanh(0.5 * x) + 0.5


def _sigmoid_pos(x):
    return 1.0 / (1.0 + jnp.exp(-x))


def _cparams(sem, vmem=VMEM_LIMIT):
    return pltpu.CompilerParams(dimension_semantics=sem, vmem_limit_bytes=vmem)


def _iota(shape, axis):
    return lax.broadcasted_iota(jnp.int32, shape, axis)


def _softplus_neg(lam):
    z = -lam
    e = jnp.exp(-jnp.abs(z))
    u = 1.0 + e
    log1p_e = jnp.where(u == 1.0, e, jnp.log(u) * (e / (u - 1.0)))
    sp = jnp.maximum(z, 0.0) + log1p_e
    dsp = -jnp.where(z >= 0.0, 1.0 / u, e / u)
    return sp, dsp


def _neg_expm1(x):
    poly = x * (1.0 + x * (1.0 / 2 + x * (1.0 / 6 + x * (1.0 / 24 + x * (1.0 / 120)))))
    return jnp.where(x > -1.0 / 16, -poly, 1.0 - jnp.exp(x))


def _conv_taps(lx, prev8, cw_ref, cb_ref, tile):
    xc = cb_ref[...] + cw_ref[3:4, :] * lx
    for j in (1, 2, 3):
        xc = xc + cw_ref[3 - j:4 - j, :] * pltpu.roll(lx, j, 0)
    row8 = _iota((8, D_MODEL), 0)
    last8 = lx[tile - 8:tile, :]
    fix = jnp.zeros((8, D_MODEL), F32)
    for j in (1, 2, 3):
        wrong = pltpu.roll(last8, j, 0)
        right = pltpu.roll(prev8, j, 0)
        fix = fix + cw_ref[3 - j:4 - j, :] * jnp.where(row8 < j, right - wrong, 0.0)
    return xc, fix


def _lru_gates(xcs, wa, wx, ba, bx, sp):
    xb = xcs.astype(BF16)
    r = _sigmoid_pos(_dot(xb, wa) + ba)
    i = _sigmoid(_dot(xb, wx) + bx)
    la = (-LRU_C * sp) * r
    a = jnp.exp(la)
    one_minus_a2 = _neg_expm1(2.0 * la)
    return r, i, a, one_minus_a2


def _lru_forward(p, conv_w, conv_b, wa, wx, ba, bx, lam, seq):
    tile = min(1024, seq // 2)
    seg = tile // 8
    pitch = seg + 4

    def body(lx_ref, gt_ref, cw_ref, cb_ref, wa_ref, wx_ref, ba_ref, bx_ref, lam_ref,
             h_ref, y_ref, ext, hcar, xc_scr, a_scr, u_scr):
        @pl.when(pl.program_id(0) == 0)
        def _():
            ext[0:8, :] = jnp.zeros((8, D_MODEL), F32)
            hcar[...] = jnp.zeros_like(hcar)

        lx = lx_ref[...]
        ext[8:8 + tile, :] = lx
        xc = cb_ref[...] + cw_ref[3:4, :] * lx
        for j in (1, 2, 3):
            xc = xc + cw_ref[3 - j:4 - j, :] * ext[8 - j:8 - j + tile, :]
        xc_scr[...] = xc
        ext[0:8, :] = lx_ref[tile - 8:tile, :]
        sp, _ = _softplus_neg(lam_ref[...])
        for n in range(LRU_BLOCKS):
            sl = slice(n * LRU_BW, (n + 1) * LRU_BW)
            xcs = xc_scr[:, sl]
            _, i, a, ne = _lru_gates(xcs, wa_ref[n], wx_ref[n], ba_ref[:, sl], bx_ref[:, sl], sp[:, sl])
            u = jnp.sqrt(ne) * (i * xcs)
            for half in range(2):
                lanes = slice(half * LANES, (half + 1) * LANES)
                for s in range(8):
                    a_scr[2 * n + half, s * pitch:s * pitch + seg, :] = a[s * seg:(s + 1) * seg, lanes]
                    u_scr[2 * n + half, s * pitch:s * pitch + seg, :] = u[s * seg:(s + 1) * seg, lanes]

        def step(j, carry):
            rows = pl.ds(j, 8, stride=pitch)
            out = []
            for gi in range(LANE_GROUPS):
                hl, al = carry[gi]
                a = a_scr.at[gi][rows, :]
                hl = a * hl + u_scr.at[gi][rows, :]
                al = a * al
                u_scr.at[gi][rows, :] = hl
                a_scr.at[gi][rows, :] = al
                out.append((hl, al))
            return tuple(out)

        init = tuple((jnp.zeros((8, LANES), F32), jnp.ones((8, LANES), F32)) for _ in range(LANE_GROUPS))
        fin = lax.fori_loop(0, seg, step, init)
        hl = jnp.concatenate([f[0] for f in fin], axis=1)
        al = jnp.concatenate([f[1] for f in fin], axis=1)
        row8 = _iota((8, D_MODEL), 0)
        for k in (1, 2, 4):
            m = row8 >= k
            hl = jnp.where(m, hl + al * pltpu.roll(hl, k, 0), hl)
            al = jnp.where(m, al * pltpu.roll(al, k, 0), al)
        ends = hl + al * hcar[...]
        before = jnp.where(row8 == 0, hcar[...], pltpu.roll(ends, 1, 0))
        hcar[...] = jnp.broadcast_to(ends[7:8, :], (8, D_MODEL))
        for s in range(8):
            rows = slice(s * seg, (s + 1) * seg)
            for gi in range(LANE_GROUPS):
                lanes = slice(gi * LANES, (gi + 1) * LANES)
                g = gt_ref[rows, lanes]
                h = (u_scr[gi, s * pitch:s * pitch + seg, :]
                     + a_scr[gi, s * pitch:s * pitch + seg, :] * before[s:s + 1, lanes])
                h_ref[rows, lanes] = h
                y_ref[rows, lanes] = (h * (g * _sigmoid(g))).astype(BF16)

    full = lambda shape: pl.BlockSpec(shape, lambda t: (0,) * len(shape))
    return pl.pallas_call(
        body, name="lru_fwd", grid=(seq // tile,),
        in_specs=[pl.BlockSpec((tile, D_MODEL), lambda t: (t, 0)),
                  pl.BlockSpec((tile, D_MODEL), lambda t: (t, 1)),
                  full((4, D_MODEL)), full((1, D_MODEL)),
                  full((LRU_BLOCKS, LRU_BW, LRU_BW)), full((LRU_BLOCKS, LRU_BW, LRU_BW)),
                  full((1, D_MODEL)), full((1, D_MODEL)), full((1, D_MODEL))],
        out_specs=[pl.BlockSpec((tile, D_MODEL), lambda t: (t, 0)),
                   pl.BlockSpec((tile, D_MODEL), lambda t: (t, 0))],
        out_shape=[SDS((seq, D_MODEL), F32), SDS((seq, D_MODEL), BF16)],
        scratch_shapes=[pltpu.VMEM((tile + 8, D_MODEL), F32), pltpu.VMEM((8, D_MODEL), F32),
                        pltpu.VMEM((tile, D_MODEL), F32), pltpu.VMEM((LANE_GROUPS, 8 * pitch, LANES), F32),
                        pltpu.VMEM((LANE_GROUPS, 8 * pitch, LANES), F32)],
        compiler_params=_cparams(("arbitrary",)),
    )(p, p, conv_w, conv_b, wa, wx, ba, bx, lam)


def _lru_backward(p, h, dymix, conv_w, conv_b, wa, wx, ba, bx, lam, seq):
    tile = min(512, seq // 2)
    nt = seq // tile
    seg = tile // 8
    pitch = seg + 4
    t8 = tile // 8

    def body(lx_ref, lxh_ref, gt_ref, h_ref, hh_ref, dy_ref, cw_ref, cb_ref, wa_ref, wx_ref, ba_ref,
             bx_ref, lam_ref, dp_ref, gwa_ref, gwx_ref, gsm_ref,
             lamcar, anext, dxc8, xc_scr, r_scr, i_scr, a_scr, m_scr, rm_scr, l_scr, dxc_scr, c3, l3):
        step = pl.program_id(0)
        first_tile = step == nt - 1

        @pl.when(step == 0)
        def _():
            lamcar[...] = jnp.zeros_like(lamcar)
            anext[...] = jnp.zeros_like(anext)
            dxc8[...] = jnp.zeros_like(dxc8)
            gwa_ref[...] = jnp.zeros_like(gwa_ref)
            gwx_ref[...] = jnp.zeros_like(gwx_ref)
            gsm_ref[...] = jnp.zeros_like(gsm_ref)

        keep = jnp.where(first_tile, 0.0, 1.0)
        lx = lx_ref[...]
        prev8 = lxh_ref[...] * keep
        xc, fix = _conv_taps(lx, prev8, cw_ref, cb_ref, tile)
        xc_scr[...] = xc
        xc_scr[0:8, :] = xc_scr[0:8, :] + fix
        sp, dsp = _softplus_neg(lam_ref[...])
        for n in range(LRU_BLOCKS):
            sl = slice(n * LRU_BW, (n + 1) * LRU_BW)
            r, i, a, ne = _lru_gates(xc_scr[:, sl], wa_ref[n], wx_ref[n], ba_ref[:, sl], bx_ref[:, sl],
                                     sp[:, sl])
            r_scr[:, sl] = r
            i_scr[:, sl] = i
            a_scr[:, sl] = a
            m_scr[:, sl] = jnp.sqrt(ne)
            rm_scr[:, sl] = lax.rsqrt(ne)

        g = gt_ref[...]
        sg = _sigmoid(g)
        dy = dy_ref[...]
        hv = h_ref[...]
        dp_ref[:, D_MODEL:2 * D_MODEL] = (dy * hv * (sg * (1.0 + g * (1.0 - sg)))).astype(BF16)

        rowt = _iota((tile, D_MODEL), 0)
        av = a_scr[...]
        dh = dy * (g * sg)
        cnext = jnp.where(rowt == tile - 1, anext[...][0:1, :], pltpu.roll(av, tile - 1, 0))
        anext[...] = jnp.broadcast_to(av[0:1, :], (8, D_MODEL))
        for gi in range(LANE_GROUPS):
            lanes = slice(gi * LANES, (gi + 1) * LANES)
            for s in range(8):
                l3[gi, s * pitch:s * pitch + seg, :] = dh[s * seg:(s + 1) * seg, lanes]
                c3[gi, s * pitch:s * pitch + seg, :] = cnext[s * seg:(s + 1) * seg, lanes]

        def step(jj, carry):
            rows = pl.ds(seg - 1 - jj, 8, stride=pitch)
            out = []
            for gi in range(LANE_GROUPS):
                ll, cl = carry[gi]
                c = c3.at[gi][rows, :]
                ll = c * ll + l3.at[gi][rows, :]
                cl = c * cl
                l3.at[gi][rows, :] = ll
                c3.at[gi][rows, :] = cl
                out.append((ll, cl))
            return tuple(out)

        init = tuple((jnp.zeros((8, LANES), F32), jnp.ones((8, LANES), F32)) for _ in range(LANE_GROUPS))
        fin = lax.fori_loop(0, seg, step, init)
        ll = jnp.concatenate([f[0] for f in fin], axis=1)
        cl = jnp.concatenate([f[1] for f in fin], axis=1)
        row8 = _iota((8, D_MODEL), 0)
        for k in (1, 2, 4):
            m = row8 < 8 - k
            ll = jnp.where(m, ll + cl * pltpu.roll(ll, 8 - k, 0), ll)
            cl = jnp.where(m, cl * pltpu.roll(cl, 8 - k, 0), cl)
        firsts = ll + cl * lamcar[...]
        after = jnp.where(row8 == 7, lamcar[...], pltpu.roll(firsts, 7, 0))
        lamcar[...] = jnp.broadcast_to(firsts[0:1, :], (8, D_MODEL))
        for s in range(8):
            for gi in range(LANE_GROUPS):
                lanes = slice(gi * LANES, (gi + 1) * LANES)
                l_scr[s * seg:(s + 1) * seg, lanes] = (l3[gi, s * pitch:s * pitch + seg, :]
                                                        + c3[gi, s * pitch:s * pitch + seg, :] * after[s:s + 1, lanes])

        hprev = jnp.where(rowt == 0, hh_ref[...][7:8, :] * keep, pltpu.roll(hv, 1, 0))
        for n in range(LRU_BLOCKS):
            sl = slice(n * LRU_BW, (n + 1) * LRU_BW)
            lamv = l_scr[:, sl]
            xcs = xc_scr[:, sl]
            r = r_scr[:, sl]
            i = i_scr[:, sl]
            a = a_scr[:, sl]
            mult = m_scr[:, sl]
            d_la = lamv * hprev[:, sl] * a - (lamv * i * xcs) * (a * a * rm_scr[:, sl])
            d_pr = d_la * (-LRU_C * sp[:, sl]) * r * (1.0 - r)
            d_pi = (lamv * mult * xcs) * i * (1.0 - i)
            gsm_ref[7:8, sl] += jnp.sum(d_la * r, axis=0, keepdims=True) * (-LRU_C) * dsp[:, sl]
            gsm_ref[5:6, sl] += jnp.sum(d_pr, axis=0, keepdims=True)
            gsm_ref[6:7, sl] += jnp.sum(d_pi, axis=0, keepdims=True)
            xb = xcs.astype(BF16)
            prb = d_pr.astype(BF16)
            pib = d_pi.astype(BF16)
            gwa_ref[n] += _dot(xb, prb, TN)
            gwx_ref[n] += _dot(xb, pib, TN)
            dxc_scr[:, sl] = lamv * mult * i + _dot(prb, wa_ref[n], NT) + _dot(pib, wx_ref[n], NT)

        dxc = dxc_scr[...]
        gsm_ref[4:5, :] += jnp.sum(dxc, axis=0, keepdims=True)
        last8 = lx[tile - 8:tile, :]
        first8 = dxc[0:8, :]
        dlx = cw_ref[3:4, :] * dxc
        gsm_ref[3:4, :] += jnp.sum(dxc * lx, axis=0, keepdims=True)
        fix = jnp.zeros((8, D_MODEL), F32)
        for j in (1, 2, 3):
            w = cw_ref[3 - j:4 - j, :]
            dlx = dlx + w * pltpu.roll(dxc, tile - j, 0)
            fix = fix + w * jnp.where(row8 + j >= 8,
                                      pltpu.roll(dxc8[...], 8 - j, 0) - pltpu.roll(first8, 8 - j, 0), 0.0)
            halo = jnp.where(row8 < j, pltpu.roll(prev8, j, 0) - pltpu.roll(last8, j, 0), 0.0)
            gsm_ref[3 - j:4 - j, :] += (jnp.sum(dxc * pltpu.roll(lx, j, 0), axis=0, keepdims=True)
                                        + jnp.sum(first8 * halo, axis=0, keepdims=True))
        dxc8[...] = first8
        dp_ref[:, 0:D_MODEL] = dlx.astype(BF16)
        top = tile - 8
        dp_ref[top:tile, 0:D_MODEL] = (dlx[top:tile, :] + fix).astype(BF16)

    rev = lambda t: (nt - 1 - t, 0)
    halo_idx = lambda t: (jnp.maximum((nt - 1 - t) * t8 - 1, 0), 0)
    full = lambda shape: pl.BlockSpec(shape, lambda t: (0,) * len(shape))
    big = lambda: pltpu.VMEM((tile, D_MODEL), F32)
    return pl.pallas_call(
        body, name="lru_bwd", grid=(nt,),
        in_specs=[pl.BlockSpec((tile, D_MODEL), rev),
                  pl.BlockSpec((8, D_MODEL), halo_idx),
                  pl.BlockSpec((tile, D_MODEL), lambda t: (nt - 1 - t, 1)),
                  pl.BlockSpec((tile, D_MODEL), rev),
                  pl.BlockSpec((8, D_MODEL), halo_idx),
                  pl.BlockSpec((tile, D_MODEL), rev),
                  full((4, D_MODEL)), full((1, D_MODEL)),
                  full((LRU_BLOCKS, LRU_BW, LRU_BW)), full((LRU_BLOCKS, LRU_BW, LRU_BW)),
                  full((1, D_MODEL)), full((1, D_MODEL)), full((1, D_MODEL))],
        out_specs=[pl.BlockSpec((tile, 2 * D_MODEL), rev),
                   full((LRU_BLOCKS, LRU_BW, LRU_BW)), full((LRU_BLOCKS, LRU_BW, LRU_BW)),
                   full((8, D_MODEL))],
        out_shape=[SDS((seq, D_IN), BF16), SDS((LRU_BLOCKS, LRU_BW, LRU_BW), F32),
                   SDS((LRU_BLOCKS, LRU_BW, LRU_BW), F32), SDS((8, D_MODEL), F32)],
        scratch_shapes=[pltpu.VMEM((8, D_MODEL), F32), pltpu.VMEM((8, D_MODEL), F32),
                        pltpu.VMEM((8, D_MODEL), F32)] + [big() for _ in range(8)]
                       + [pltpu.VMEM((LANE_GROUPS, 8 * pitch, LANES), F32)] * 2,
        compiler_params=_cparams(("arbitrary",)),
    )(p, p, p, h, h, dymix, conv_w, conv_b, wa, wx, ba, bx, lam)


def _tri_matmul(tri, g):
    hi = g.astype(BF16)
    lo = (g - hi.astype(F32)).astype(BF16)
    return _dot(tri, lo) + _dot(tri, hi)


def _hgrn_gate_terms(q, fr, lbl):
    lb = _sigmoid_pos(lbl[0:1, :] - lbl[1:2, :])
    half = 0.5 * (1.0 - lb)
    tf = jnp.tanh(0.5 * fr)
    f = (lb + half) + half * tf
    hq = 0.5 * q
    tq = jnp.tanh(hq)
    return lb, tf, f, tq, hq * tq + hq


def _hgrn_decay(bh):
    zero = jnp.zeros((1, bh.shape[1]), F32)
    rho = [zero] + [bh[s * SUB - 1:s * SUB, :] for s in range(1, N_SUB + 1)]
    start = _sub_rows(rho[0:N_SUB])
    end = _sub_rows(rho[1:N_SUB + 1])
    mid = 0.5 * (start + end)
    blast = rho[N_SUB]
    e_on = jnp.exp(bh - start)
    e_off = jnp.exp(end - bh)
    scales = [_sub_rows([jnp.exp(rho[i] - rho[j + 1]) if i > j else zero for i in range(N_SUB)])
              for j in range(N_SUB - 1)]
    return dict(eq0=jnp.exp(jnp.minimum(bh - mid, EXP_CLAMP)), ek0=jnp.exp(jnp.minimum(mid - bh, EXP_CLAMP)),
                e_on=e_on, e_off=e_off, scales=scales,
                eb=e_on * _sub_rows([jnp.exp(r) for r in rho[0:N_SUB]]),
                ekst=e_off * _sub_rows([jnp.exp(blast - r) for r in rho[1:N_SUB + 1]]),
                ebl=jnp.exp(blast))


def _sub_rows(vecs):
    return jnp.concatenate([jnp.broadcast_to(v, (SUB, v.shape[1])) for v in vecs], axis=0)


def _hgrn_operands(qs, k, dec, qt_scr, kt_scr):
    sub = jnp.right_shift(_iota(qs.shape, 0), 5)
    qon = qs * dec["e_on"]
    koff = k * dec["e_off"]
    qt_scr[0] = (qs * dec["eq0"]).astype(BF16)
    kt_scr[0] = (k * dec["ek0"]).astype(BF16)
    for j in range(N_SUB - 1):
        qt_scr[j + 1] = (qon * dec["scales"][j]).astype(BF16)
        kt_scr[j + 1] = jnp.where(sub == j, koff, 0.0).astype(BF16)
    return koff


def _hgrn_head_scores(qt_scr, kt_scr, sl, diag):
    a = jnp.where(diag, _dot(qt_scr[0, :, sl], kt_scr[0, :, sl], NT), 0.0)
    for j in range(1, N_SUB):
        a = a + _dot(qt_scr[j, :, sl], kt_scr[j, :, sl], NT)
    return a


def _hgrn_forward(p, lbl, gw, seq):
    nc = seq // CHUNK
    assert SUB == 32

    def body(q_ref, f_ref, v_ref, hg_ref, lbl_ref, gw_ref, y_ref, o_ref, st_ref,
             s_scr, qt_scr, kt_scr, qin_scr, kst_scr, vb_scr, a_scr):
        @pl.when(pl.program_id(0) == 0)
        def _():
            s_scr[...] = jnp.zeros_like(s_scr)

        r = _iota((CHUNK, CHUNK), 0)
        c = _iota((CHUNK, CHUNK), 1)
        tri = jnp.where(c <= r, 1.0, 0.0).astype(BF16)
        diag = (jnp.right_shift(r, 5) == jnp.right_shift(c, 5)) & (c <= r)
        heads = [slice(h * HEAD_D, (h + 1) * HEAD_D) for h in range(N_HEADS)]
        for cc in range(HGRN_FWD_STEP_CHUNKS):
            rows = slice(cc * CHUNK, (cc + 1) * CHUNK)
            q = q_ref[rows, :]
            _, _, f, _, qs = _hgrn_gate_terms(q, f_ref[rows, :], lbl_ref[...])
            k = 1.0 - f
            dec = _hgrn_decay(_tri_matmul(tri, jnp.log(f)))
            _hgrn_operands(qs, k, dec, qt_scr, kt_scr)
            qin_scr[...] = (qs * dec["eb"]).astype(BF16)
            kst_scr[...] = (k * dec["ekst"]).astype(BF16)
            vb_scr[...] = v_ref[rows, :].astype(BF16)
            ebl = dec["ebl"]
            hg = hg_ref[rows, :]
            gate = gw_ref[...] * (hg * _sigmoid(hg))
            stb = []
            for h, sl in enumerate(heads):
                st = s_scr[h]
                st_ref[cc, h] = st
                stb.append(st.astype(BF16))
                s_scr[h] = st * ebl[:, sl] + _dot(vb_scr[:, sl], kst_scr[:, sl], TN)
            for h, sl in enumerate(heads):
                a_scr[h] = _hgrn_head_scores(qt_scr, kt_scr, sl, diag).astype(BF16)
            for h, sl in enumerate(heads):
                o = _dot(a_scr[h], vb_scr[:, sl]) + _dot(qin_scr[:, sl], stb[h], NT)
                o_ref[rows, sl] = o
                rs = lax.rsqrt(jnp.mean(o * o, axis=-1, keepdims=True) + EPS)
                y_ref[rows, sl] = ((o * rs) * gate[:, sl]).astype(BF16)

    col = lambda j: pl.BlockSpec((HGRN_FWD_STEP_CHUNKS * CHUNK, D_MODEL), lambda c: (c, j))
    par = lambda rows: pl.BlockSpec((rows, D_MODEL), lambda c: (0, 0))
    return pl.pallas_call(
        body, name="hgrn_fwd", grid=(nc // HGRN_FWD_STEP_CHUNKS,),
        in_specs=[col(2), col(3), col(4), col(5), par(2), par(1)],
        out_specs=[col(0), col(0),
                   pl.BlockSpec((HGRN_FWD_STEP_CHUNKS, N_HEADS, HEAD_D, HEAD_D), lambda c: (c, 0, 0, 0))],
        out_shape=[SDS((seq, D_MODEL), BF16), SDS((seq, D_MODEL), F32),
                   SDS((nc, N_HEADS, HEAD_D, HEAD_D), F32)],
        scratch_shapes=[pltpu.VMEM((N_HEADS, HEAD_D, HEAD_D), F32),
                        pltpu.VMEM((N_SUB, CHUNK, D_MODEL), BF16), pltpu.VMEM((N_SUB, CHUNK, D_MODEL), BF16)]
                       + [pltpu.VMEM((CHUNK, D_MODEL), BF16)] * 3 + [pltpu.VMEM((N_HEADS, CHUNK, CHUNK), BF16)],
        compiler_params=_cparams(("arbitrary",)),
    )(p, p, p, p, lbl, gw)


def _hgrn_backward(p, o, states, dymix, lbl, gw, dp_full, g_w_out, g_small, seq):
    step_rows = HGRN_STEP_CHUNKS * CHUNK
    ns = seq // step_rows

    def body(q_ref, f_ref, v_ref, hg_ref, o_ref, st_ref, dy_ref, lbl_ref, gw_ref, dpin_ref, go_ref, gs_ref,
             dpo_ref, gsm_ref, ro_ref, rs_ref, ds_scr, dp_buf, dp_sems, *rest):
        del dpin_ref
        scratch, sems = rest[:14], rest[14:]
        step = pl.program_id(0)
        slot = step % 2
        exs = [_SlotExchange(go_ref, ro_ref, *sems[0:3], blocked=True),
               _SlotExchange(gs_ref, rs_ref, *sems[3:6], blocked=True)]

        @pl.when(step == 0)
        def _():
            for ex in exs:
                ex.start()

        def out_copy(s, blk):
            rows = pl.ds(pl.multiple_of(blk * step_rows, step_rows), step_rows)
            return pltpu.make_async_copy(dp_buf.at[s], dpo_ref.at[rows, pl.ds(2 * D_MODEL, 4 * D_MODEL)],
                                         dp_sems.at[s])

        @pl.when(step == 0)
        def _():
            ds_scr[...] = jnp.zeros_like(ds_scr)
            gsm_ref[...] = jnp.zeros_like(gsm_ref)

        @pl.when(step >= 2)
        def _():
            out_copy(slot, ns + 1 - step).wait()

        for cc in reversed(range(HGRN_STEP_CHUNKS)):
            chunk(cc, q_ref, f_ref, v_ref, hg_ref, o_ref, st_ref, dy_ref, lbl_ref, gw_ref, gsm_ref, ds_scr,
                  dp_buf.at[slot], *scratch)

        out_copy(slot, ns - 1 - step).start()

        @pl.when(step == ns - 1)
        def _():
            out_copy(1 - slot, 1).wait()
            out_copy(slot, 0).wait()
            for ex in exs:
                ex.wait()

    def chunk(cc, q_ref, f_ref, v_ref, hg_ref, o_ref, st_ref, dy_ref, lbl_ref, gw_ref, gsm_ref, ds_scr, dp_ref,
              qt_scr, kt_scr, qin_scr, kst_scr, vb_scr, dob_scr, g_scr, h_scr, dqi_scr, dks_scr, sd_scr,
              a_scr, da_scr, da0_scr):
        rows = slice(cc * CHUNK, (cc + 1) * CHUNK)
        r = _iota((CHUNK, CHUNK), 0)
        c = _iota((CHUNK, CHUNK), 1)
        tri = jnp.where(c <= r, 1.0, 0.0).astype(BF16)
        triu = jnp.where(c >= r, 1.0, 0.0).astype(BF16)
        diag = (jnp.right_shift(r, 5) == jnp.right_shift(c, 5)) & (c <= r)
        row = _iota((CHUNK, D_MODEL), 0)
        sub = jnp.right_shift(row, 5)

        q = q_ref[rows, :]
        lb, tf, f, tq, qs = _hgrn_gate_terms(q, f_ref[rows, :], lbl_ref[...])
        sig = 0.5 * tf + 0.5
        sq = 0.5 * tq + 0.5
        k = 1.0 - f
        log_f = jnp.log(f)
        dec = _hgrn_decay(_tri_matmul(tri, log_f))
        eb, ekst, ebl = dec["eb"], dec["ekst"], dec["ebl"]
        koff = _hgrn_operands(qs, k, dec, qt_scr, kt_scr)
        qin_scr[...] = (qs * eb).astype(BF16)
        kst_scr[...] = (k * ekst).astype(BF16)
        vb_scr[...] = v_ref[rows, :].astype(BF16)
        hg = hg_ref[rows, :]
        sh = _sigmoid(hg)
        dy = dy_ref[rows, :]
        gwv = gw_ref[...]
        d_onw = dy * (hg * sh)
        d_on = d_onw * gwv
        d_gate = dy * gwv * (sh * (1.0 + hg * (1.0 - sh)))

        heads = [slice(h * HEAD_D, (h + 1) * HEAD_D) for h in range(N_HEADS)]
        for h, sl in enumerate(heads):
            o = o_ref[rows, sl]
            rs = lax.rsqrt(jnp.mean(o * o, axis=-1, keepdims=True) + EPS)
            on = o * rs
            dp_ref[rows, 3 * D_MODEL + h * HEAD_D:3 * D_MODEL + (h + 1) * HEAD_D] = (d_gate[:, sl] * on).astype(BF16)
            gsm_ref[1:2, sl] += jnp.sum(d_onw[:, sl] * on, axis=0, keepdims=True)
            d_onh = d_on[:, sl]
            dob_scr[:, sl] = (rs * (d_onh - on * jnp.mean(d_onh * on, axis=-1, keepdims=True))).astype(BF16)
        for h, sl in enumerate(heads):
            a_scr[h] = _hgrn_head_scores(qt_scr, kt_scr, sl, diag).astype(BF16)
            da = _dot(dob_scr[:, sl], vb_scr[:, sl], NT)
            da_scr[h] = da.astype(BF16)
            da0_scr[h] = jnp.where(diag, da, 0.0).astype(BF16)
        for h, sl in enumerate(heads):
            st = st_ref[cc, h]
            dst = ds_scr[h]
            dstb = dst.astype(BF16)
            dp_ref[rows, 2 * D_MODEL + h * HEAD_D:2 * D_MODEL + (h + 1) * HEAD_D] = (
                _dot(a_scr[h], dob_scr[:, sl], TN) + _dot(kst_scr[:, sl], dstb, NT)).astype(BF16)
            dqi_scr[:, sl] = _dot(dob_scr[:, sl], st.astype(BF16))
            dks_scr[:, sl] = _dot(vb_scr[:, sl], dstb)
            sd_scr[0:1, sl] = jnp.sum(st * dst, axis=0, keepdims=True)
            ds_scr[h] = dst * ebl[:, sl] + _dot(dob_scr[:, sl], qin_scr[:, sl], TN)
        for h, sl in enumerate(heads):
            g_scr[0, :, sl] = _dot(da0_scr[h], kt_scr[0, :, sl])
            h_scr[0, :, sl] = _dot(da0_scr[h], qt_scr[0, :, sl], TN)
            for j in range(1, N_SUB):
                g_scr[j, :, sl] = _dot(da_scr[h], kt_scr[j, :, sl])
                h_scr[j, :, sl] = _dot(da_scr[h], qt_scr[j, :, sl], TN)

        g0 = g_scr[0]
        h0 = h_scr[0]
        dq_inter = eb * dqi_scr[...]
        d_kst = ekst * dks_scr[...]
        db = qs * dq_inter - k * d_kst + qt_scr[0].astype(F32) * g0 - kt_scr[0].astype(F32) * h0
        gq = jnp.zeros((CHUNK, D_MODEL), F32)
        hsel = jnp.zeros((CHUNK, D_MODEL), F32)
        for j in range(N_SUB - 1):
            gj = g_scr[j + 1]
            gq = gq + dec["scales"][j] * gj
            db = db + qt_scr[j + 1].astype(F32) * gj
            hsel = jnp.where(sub == j, h_scr[j + 1], hsel)
        db = db - koff.astype(BF16).astype(F32) * hsel
        d_q = dec["eq0"] * g0 + dec["e_on"] * gq + dq_inter
        d_k = dec["ek0"] * h0 + dec["e_off"] * hsel + d_kst
        db_last = jnp.sum(k * d_kst, axis=0, keepdims=True) + ebl * sd_scr[0:1, :]
        db = db + jnp.where(row == CHUNK - 1, db_last, 0.0)
        dg = _tri_matmul(triu, db)
        d_f = dg * jnp.exp(-log_f) - d_k
        dp_ref[rows, D_MODEL:2 * D_MODEL] = (d_f * (1.0 - lb) * sig * (1.0 - sig)).astype(BF16)
        gsm_ref[0:1, :] += jnp.sum(d_f * (1.0 - sig), axis=0, keepdims=True) * (lb * (1.0 - lb))
        dp_ref[rows, 0:D_MODEL] = (d_q * (sq * (1.0 + q * (1.0 - sq)))).astype(BF16)

    rc = lambda c: ns - 1 - c
    col = lambda j: pl.BlockSpec((step_rows, D_MODEL), lambda c: (rc(c), j))
    par = lambda rows: pl.BlockSpec((rows, D_MODEL), lambda c: (0, 0))
    return pl.pallas_call(
        body, name="hgrn_bwd", grid=(ns,),
        in_specs=[col(2), col(3), col(4), col(5), col(0),
                  pl.BlockSpec((HGRN_STEP_CHUNKS, N_HEADS, HEAD_D, HEAD_D), lambda c: (rc(c), 0, 0, 0)),
                  col(1), par(2), par(1), ANY, ANY, ANY],
        out_specs=[ANY, par(8), ANY, ANY],
        out_shape=[SDS((seq, D_IN), BF16), SDS((8, D_MODEL), F32), SDS(g_w_out.shape, F32),
                   SDS(g_small.shape, F32)],
        input_output_aliases={9: 0},
        scratch_shapes=[pltpu.VMEM((N_HEADS, HEAD_D, HEAD_D), F32),
                        pltpu.VMEM((2, step_rows, 4 * D_MODEL), BF16), pltpu.SemaphoreType.DMA((2,)),
                        pltpu.VMEM((N_SUB, CHUNK, D_MODEL), BF16), pltpu.VMEM((N_SUB, CHUNK, D_MODEL), BF16)]
                       + [pltpu.VMEM((CHUNK, D_MODEL), BF16)] * 4
                       + [pltpu.VMEM((N_SUB, CHUNK, D_MODEL), F32)] * 2 + [pltpu.VMEM((CHUNK, D_MODEL), F32)] * 2
                       + [pltpu.VMEM((8, D_MODEL), F32)] + [pltpu.VMEM((N_HEADS, CHUNK, CHUNK), BF16)] * 3
                       + EXCHANGE_SEMS * 2,
        compiler_params=_cparams(("arbitrary",)),
    )(p, p, p, p, o, states, dymix, lbl, gw, dp_full, g_w_out, g_small)


def _out_proj(yl, yh, wo, x, tgt, post_w, seq):
    tm = 512

    def body(yl_ref, yh_ref, wo_ref, x_ref, tg_ref, pw_ref, dymix_ref, dout_ref, gwo_ref, st_ref):
        @pl.when(pl.program_id(0) == 0)
        def _():
            gwo_ref[...] = jnp.zeros_like(gwo_ref)
            st_ref[...] = jnp.zeros_like(st_ref)

        ylv = yl_ref[...]
        yhv = yh_ref[...]
        y = _dot(ylv, wo_ref[0:D_MODEL, :]) + _dot(yhv, wo_ref[D_MODEL:D_MIX, :])
        r2 = lax.rsqrt(jnp.mean(y * y, axis=-1, keepdims=True) + EPS)
        yn = y * r2
        pw = pw_ref[...]
        e = (x_ref[...] + yn * pw) - tg_ref[...]
        st_ref[1:2, :] += jnp.sum(e * e, axis=0, keepdims=True) * (0.5 / D_MODEL)
        dout = e * (1.0 / D_MODEL)
        dout_ref[...] = dout
        st_ref[0:1, :] += jnp.sum(dout * yn, axis=0, keepdims=True)
        dyn = dout * pw
        dy = r2 * (dyn - yn * jnp.mean(dyn * yn, axis=-1, keepdims=True))
        dyb = dy.astype(BF16)
        dymix_ref[...] = _dot(dyb, wo_ref[...], NT)
        gwo_ref[0:D_MODEL, :] += _dot(ylv, dyb, TN)
        gwo_ref[D_MODEL:D_MIX, :] += _dot(yhv, dyb, TN)

    row = lambda w: pl.BlockSpec((tm, w), lambda m: (m, 0))
    full = lambda shape: pl.BlockSpec(shape, lambda m: (0,) * len(shape))
    once = lambda shape: pl.BlockSpec(shape, lambda m: (0,) * len(shape), pipeline_mode=pl.Buffered(1))
    return pl.pallas_call(
        body, name="out_proj", grid=(seq // tm,),
        in_specs=[row(D_MODEL), row(D_MODEL), once((D_MIX, D_MODEL)), row(D_MODEL), row(D_MODEL),
                  full((1, D_MODEL))],
        out_specs=[row(D_MIX), row(D_MODEL), once((D_MIX, D_MODEL)), full((8, D_MODEL))],
        out_shape=[SDS((seq, D_MIX), F32), SDS((seq, D_MODEL), F32), SDS((D_MIX, D_MODEL), F32),
                   SDS((8, D_MODEL), F32)],
        compiler_params=_cparams(("arbitrary",)),
    )(yl, yh, wo, x, tgt, post_w)


MESH = pl.DeviceIdType.MESH
ANY = pl.BlockSpec(memory_space=pl.ANY)
EXCHANGE_SEMS = [pltpu.SemaphoreType.DMA((N_DEV - 1,)), pltpu.SemaphoreType.DMA((N_DEV - 1,)),
                 pltpu.SemaphoreType.DMA(())]


def _mesh_pos():
    return lax.axis_index("x"), lax.axis_index("y"), lax.axis_index("c")


class _SlotExchange:
    def __init__(self, src_ref, dst_ref, send_sems, recv_sems, local_sem, blocked):
        x, y, c = _mesh_pos()
        me = 4 * x + 2 * y + c
        src = (lambda dest: src_ref.at[dest]) if blocked else (lambda dest: src_ref)
        self.local = pltpu.make_async_copy(src(me), dst_ref.at[me], local_sem)
        self.sends, self.recvs = [], []
        for k in range(1, N_DEV):
            px = 1 - x if (k >> 2) & 1 else x
            py = 1 - y if (k >> 1) & 1 else y
            pc = 1 - c if k & 1 else c
            peer = 4 * px + 2 * py + pc
            sems = dict(send_sem=send_sems.at[k - 1], recv_sem=recv_sems.at[k - 1],
                        device_id=(px, py, pc), device_id_type=MESH)
            self.sends.append(pltpu.make_async_remote_copy(src_ref=src(peer), dst_ref=dst_ref.at[me], **sems))
            self.recvs.append(pltpu.make_async_remote_copy(src_ref=dst_ref.at[peer], dst_ref=dst_ref.at[peer], **sems))

    def start(self):
        self.local.start()
        for cp in self.sends:
            cp.start()

    def wait(self):
        for cp in self.recvs:
            cp.wait_recv()
        for cp in self.sends:
            cp.wait_send()
        self.local.wait()


class _ChipExchange:
    def __init__(self, src_ref, dst_ref, send_sems, recv_sems, local_sem):
        x, y, c = _mesh_pos()
        chip = 2 * x + y
        self.local = pltpu.make_async_copy(src_ref.at[chip], dst_ref.at[chip], local_sem)
        self.sends, self.recvs = [], []
        for k in range(1, N_CHIPS):
            px = 1 - x if (k >> 1) & 1 else x
            py = 1 - y if k & 1 else y
            peer = 2 * px + py
            sems = dict(send_sem=send_sems.at[k - 1], recv_sem=recv_sems.at[k - 1],
                        device_id=(px, py, c), device_id_type=MESH)
            self.sends.append(pltpu.make_async_remote_copy(src_ref=src_ref.at[peer], dst_ref=dst_ref.at[chip], **sems))
            self.recvs.append(pltpu.make_async_remote_copy(src_ref=dst_ref.at[peer], dst_ref=dst_ref.at[peer], **sems))

    def start(self):
        self.local.start()
        for cp in self.sends:
            cp.start()

    def wait(self):
        for cp in self.recvs:
            cp.wait_recv()
        for cp in self.sends:
            cp.wait_send()
        self.local.wait()


GRAD_W_IN_TK = 2048


def _grad_w_in_sibling(u, dp, core, seq):
    tk = min(GRAD_W_IN_TK, seq)
    nk = seq // tk

    def body(core_ref, u_ref, dp_ref, g_ref):
        del core_ref

        @pl.when(pl.program_id(1) == 0)
        def _():
            g_ref[...] = jnp.zeros_like(g_ref)

        g_ref[0] += _dot(u_ref[...], dp_ref[...], TN)

    return pl.pallas_call(
        body, name="grad_w_in_sibling",
        grid_spec=pltpu.PrefetchScalarGridSpec(
            num_scalar_prefetch=1, grid=(N_CHIPS, nk),
            in_specs=[pl.BlockSpec((tk, D_MODEL), lambda n, k, c: (k, 0)),
                      pl.BlockSpec((tk, W_BLK), lambda n, k, c: (k, 2 * n + 1 - c[0]))],
            out_specs=pl.BlockSpec((1, D_MODEL, W_BLK), lambda n, k, c: (n, 0, 0))),
        out_shape=SDS((N_CHIPS, D_MODEL, W_BLK), F32),
        compiler_params=_cparams(("parallel", "arbitrary")),
    )(core, u, dp)


def _grad_w_in_own(u, dp, core, g_sib, seq):
    tk = min(GRAD_W_IN_TK, seq)
    nk = seq // tk

    def body(core_ref, u_ref, dp_ref, gsib_ref, g_ref, land, send_sem, recv_sem):
        del core_ref
        n = pl.program_id(0)
        k = pl.program_id(1)
        x, y, c = _mesh_pos()
        swap = pltpu.make_async_remote_copy(src_ref=gsib_ref, dst_ref=land, send_sem=send_sem, recv_sem=recv_sem,
                                            device_id=(x, y, 1 - c), device_id_type=MESH)

        @pl.when((n == 0) & (k == 0))
        def _():
            swap.start()

        @pl.when(k == 0)
        def _():
            g_ref[...] = jnp.zeros_like(g_ref)

        g_ref[0] += _dot(u_ref[...], dp_ref[...], TN)

        @pl.when((n == 0) & (k == nk - 1))
        def _():
            swap.wait_recv()

        @pl.when(k == nk - 1)
        def _():
            g_ref[0] += land[n]

        @pl.when((n == N_CHIPS - 1) & (k == nk - 1))
        def _():
            swap.wait_send()

    return pl.pallas_call(
        body, name="grad_w_in_own",
        grid_spec=pltpu.PrefetchScalarGridSpec(
            num_scalar_prefetch=1, grid=(N_CHIPS, nk),
            in_specs=[pl.BlockSpec((tk, D_MODEL), lambda n, k, c: (k, 0)),
                      pl.BlockSpec((tk, W_BLK), lambda n, k, c: (k, 2 * n + c[0])), ANY],
            out_specs=pl.BlockSpec((1, D_MODEL, W_BLK), lambda n, k, c: (n, 0, 0)),
            scratch_shapes=[pltpu.VMEM((N_CHIPS, D_MODEL, W_BLK), F32), pltpu.SemaphoreType.DMA(()),
                            pltpu.SemaphoreType.DMA(())]),
        out_shape=SDS((N_CHIPS, D_MODEL, W_BLK), F32),
        compiler_params=_cparams(("arbitrary", "arbitrary")),
    )(core, u, dp, g_sib)


def _grad_x(dp, w_all, x, pre_w, dout, g_chip, seq):
    tm = 512
    nm = seq // tm

    def body(dp_ref, w_ref, x_ref, pw_ref, do_ref, gsrc_ref, gx_ref, gpw_ref, recv_ref,
             send_sems, recv_sems, local_sem):
        m = pl.program_id(0)
        ex = _ChipExchange(gsrc_ref, recv_ref, send_sems, recv_sems, local_sem)

        @pl.when(m == 0)
        def _():
            ex.start()
            gpw_ref[...] = jnp.zeros_like(gpw_ref)

        du = _dot(dp_ref[:, 0:W_BLK], w_ref[0], NT)
        for j in range(1, N_DEV):
            du = du + _dot(dp_ref[:, j * W_BLK:(j + 1) * W_BLK], w_ref[j], NT)
        xv = x_ref[...]
        r1 = lax.rsqrt(jnp.mean(xv * xv, axis=-1, keepdims=True) + EPS)
        xn = xv * r1
        gpw_ref[0:1, :] += jnp.sum(du * xn, axis=0, keepdims=True)
        dxn = du * pw_ref[...]
        gx_ref[...] = r1 * (dxn - xn * jnp.mean(dxn * xn, axis=-1, keepdims=True)) + do_ref[...]

        @pl.when(m == nm - 1)
        def _():
            ex.wait()

    row = lambda w: pl.BlockSpec((tm, w), lambda m: (m, 0))
    return pl.pallas_call(
        body, name="grad_x", grid=(nm,),
        in_specs=[row(D_IN), pl.BlockSpec((N_DEV, D_MODEL, W_BLK), lambda m: (0, 0, 0), pipeline_mode=pl.Buffered(1)),
                  row(D_MODEL),
                  pl.BlockSpec((1, D_MODEL), lambda m: (0, 0)), row(D_MODEL), ANY],
        out_specs=[row(D_MODEL), pl.BlockSpec((8, D_MODEL), lambda m: (0, 0)), ANY],
        out_shape=[SDS((seq, D_MODEL), F32), SDS((8, D_MODEL), F32), SDS(g_chip.shape, F32)],
        scratch_shapes=[pltpu.SemaphoreType.DMA((N_CHIPS - 1,)), pltpu.SemaphoreType.DMA((N_CHIPS - 1,)),
                        pltpu.SemaphoreType.DMA(())],
        compiler_params=_cparams(("arbitrary",)),
    )(dp, w_all, x, pre_w, dout, g_chip)


def _local_step(x, tgt, p, u, conv_w, conv_b, wa, wx, ba, bx, lam, lbl, gnorm_w, w_out, post_w):
    seq = x.shape[0]
    h, y_lru = _lru_forward(p, conv_w, conv_b, wa, wx, ba, bx, lam, seq)
    y_hgrn, o, states = _hgrn_forward(p, lbl, gnorm_w, seq)
    dymix, dout, g_w_out, stats = _out_proj(y_lru, y_hgrn, w_out, x, tgt, post_w, seq)
    dp_lru, g_wa, g_wx, ls = _lru_backward(p, h, dymix, conv_w, conv_b, wa, wx, ba, bx, lam, seq)
    g_small = _pack_small(_shard_rows(g_wa, LRU_BLOCKS), _shard_rows(g_wx, LRU_BLOCKS),
                          _shard_rows(ls[0:4].reshape(4, D_MODEL, 1), 4).reshape(N_DEV, 4, 128),
                          _shard_rows(ls[5].reshape(4, LRU_BW, 1), 4).reshape(N_DEV, 4, 32),
                          _shard_rows(ls[6].reshape(4, LRU_BW, 1), 4).reshape(N_DEV, 4, 32))
    dp, hgrn_small, r_out, r_small = _hgrn_backward(
        p, o, states, dymix, lbl, gnorm_w, dp_lru, g_w_out.reshape(N_DEV, D_MIX // N_DEV, D_MODEL), g_small, seq)
    return dict(u=u, dp=dp, dout=dout, r_out=r_out, r_small=r_small,
                lru_small=ls, hgrn_small=hgrn_small, stats=stats)


class _TwoLevelGather:
    def __init__(self, ins, outs, send_sems, recv_sems, local_sems):
        self.ins, self.outs = ins, outs
        self.send_sems, self.recv_sems, self.local_sems = send_sems, recv_sems, local_sems
        x, y, c = _mesh_pos()
        self.c = c
        self.me, self.sibling = (x, y, c), (x, y, 1 - c)
        self.chips = [(1 - x, y), (x, 1 - y), (1 - x, 1 - y)]
        n = len(ins)
        self.mine = [pltpu.make_async_copy(ins[a], self._slot(a, self.me), local_sems.at[a]) for a in range(n)]
        self.first = []
        for a in range(n):
            self.first.append(self._copy(a, 0, self.me, self.sibling, src=ins[a]))
            self.first += [self._copy(a, 1 + j, self.me, (*chip, c), src=ins[a])
                           for j, chip in enumerate(self.chips)]
        self.passed = [self._copy(a, 4 + j, (*chip, c), self.sibling)
                       for j, chip in enumerate(self.chips) for a in range(n)]

    def _slot(self, a, pos):
        return self.outs[a].at[4 * pos[0] + 2 * pos[1] + pos[2]]

    def _copy(self, a, k, block, to, src=None):
        dst = self._slot(a, block)
        return pltpu.make_async_remote_copy(
            src_ref=dst if src is None else src, dst_ref=dst,
            send_sem=self.send_sems.at[a, k], recv_sem=self.recv_sems.at[a, k],
            device_id=to, device_id_type=MESH)

    def start(self):
        for cp in self.mine + self.first:
            cp.start()

    def forward(self):
        n = len(self.ins)
        for j, chip in enumerate(self.chips):
            for a in range(n):
                self._copy(a, 1 + j, (*chip, self.c), self.me).wait_recv()
                self.passed[j * n + a].start()

    def finish(self):
        for a in range(len(self.ins)):
            self._copy(a, 0, self.sibling, self.me).wait_recv()
            for j, chip in enumerate(self.chips):
                self._copy(a, 4 + j, (*chip, 1 - self.c), self.me).wait_recv()
        for cp in self.first + self.passed:
            cp.wait_send()
        for cp in self.mine:
            cp.wait()


W_IN_DIRECT = (1, 2, 4, 6)
W_IN_PASSED = (2, 4, 6)

def _in_proj_gather(x, pre_w, w_in_blk, w_out_blk, small_blk, me, seq):
    tm = min(1024, seq)
    nm = seq // tm
    last = N_DEV - 1

    def body(me_ref, x_ref, pw_ref, wblk_ref, woblk_ref, smblk_ref,
             p_ref, u_ref, wall_ref, woall_ref, small_ref,
             u_all, w_vmem, own_sem, d_send, d_recv, f_send, f_recv, wb_sems, g_send, g_recv, g_local):
        i = pl.program_id(0)
        m = pl.program_id(1)
        idx = me_ref[0]
        x_, y_, c_ = _mesh_pos()
        aux = _TwoLevelGather([woblk_ref, smblk_ref], [woall_ref, small_ref], g_send, g_recv, g_local)

        def peer(k):
            return (1 - x_ if (k >> 2) & 1 else x_, 1 - y_ if (k >> 1) & 1 else y_, 1 - c_ if k & 1 else c_)

        def direct(k):
            f = W_IN_DIRECT.index(k)
            return (pltpu.make_async_remote_copy(src_ref=wblk_ref, dst_ref=w_vmem.at[idx], send_sem=d_send.at[f],
                                                 recv_sem=d_recv.at[f], device_id=peer(k), device_id_type=MESH),
                    pltpu.make_async_remote_copy(src_ref=w_vmem.at[idx ^ k], dst_ref=w_vmem.at[idx ^ k],
                                                 send_sem=d_send.at[f], recv_sem=d_recv.at[f], device_id=peer(k),
                                                 device_id_type=MESH))

        def passed(k):
            f = W_IN_PASSED.index(k)
            return (pltpu.make_async_remote_copy(src_ref=w_vmem.at[idx ^ k], dst_ref=w_vmem.at[idx ^ k],
                                                 send_sem=f_send.at[f], recv_sem=f_recv.at[f], device_id=peer(1),
                                                 device_id_type=MESH),
                    pltpu.make_async_remote_copy(src_ref=w_vmem.at[idx ^ (k + 1)], dst_ref=w_vmem.at[idx ^ (k + 1)],
                                                 send_sem=f_send.at[f], recv_sem=f_recv.at[f], device_id=peer(1),
                                                 device_id_type=MESH))

        def write_back(k):
            return pltpu.make_async_copy(w_vmem.at[idx ^ k], wall_ref.at[idx ^ k], wb_sems.at[k])

        own = pltpu.make_async_copy(wblk_ref, w_vmem.at[idx], own_sem)

        @pl.when((i == 0) & (m == 0))
        def _():
            own.start()
            for k in (1, 2, 4):
                direct(k)[0].start()
            own.wait()
            write_back(0).start()

        for k in range(1, N_DEV):
            @pl.when((i == k) & (m == 0))
            def _(k=k):
                if k in W_IN_DIRECT:
                    direct(k)[1].wait_recv()
                    if k in W_IN_PASSED:
                        passed(k)[0].start()
                else:
                    passed(k - 1)[1].wait_recv()
                write_back(k).start()
                if k == 2:
                    direct(6)[0].start()
                if k == 4:
                    aux.start()
                if k == N_DEV - 1:
                    aux.forward()

        rows = pl.ds(pl.multiple_of(m * tm, tm), tm)

        @pl.when(i == 0)
        def _():
            xv = x_ref[...]
            r = lax.rsqrt(jnp.mean(xv * xv, axis=-1, keepdims=True) + EPS)
            ub = (xv * r * pw_ref[...]).astype(BF16)
            u_all[rows, :] = ub
            u_ref[...] = ub

        p_ref[...] = _dot(u_all[rows, :], w_vmem[idx ^ i])

        @pl.when((i == last) & (m == nm - 1))
        def _():
            for k in W_IN_DIRECT:
                direct(k)[0].wait_send()
            for k in W_IN_PASSED:
                passed(k)[0].wait_send()
            for k in range(N_DEV):
                write_back(k).wait()
            aux.finish()

    first_pass = lambda i, m: jnp.where(i == 0, m, nm - 1)
    return pl.pallas_call(
        body, name="in_proj_gather",
        grid_spec=pltpu.PrefetchScalarGridSpec(
            num_scalar_prefetch=1, grid=(N_DEV, nm),
            in_specs=[pl.BlockSpec((tm, D_MODEL), lambda i, m, me: (first_pass(i, m), 0)),
                      pl.BlockSpec((1, D_MODEL), lambda i, m, me: (0, 0)), ANY, ANY, ANY],
            out_specs=[pl.BlockSpec((tm, W_BLK), lambda i, m, me: (m, me[0] ^ i)),
                       pl.BlockSpec((tm, D_MODEL), lambda i, m, me: (first_pass(i, m), 0)), ANY, ANY, ANY],
            scratch_shapes=[pltpu.VMEM((seq, D_MODEL), BF16), pltpu.VMEM((N_DEV, D_MODEL, W_BLK), BF16),
                            pltpu.SemaphoreType.DMA(()),
                            pltpu.SemaphoreType.DMA((len(W_IN_DIRECT),)), pltpu.SemaphoreType.DMA((len(W_IN_DIRECT),)),
                            pltpu.SemaphoreType.DMA((len(W_IN_PASSED),)), pltpu.SemaphoreType.DMA((len(W_IN_PASSED),)),
                            pltpu.SemaphoreType.DMA((N_DEV,)),
                            pltpu.SemaphoreType.DMA((2, 7)), pltpu.SemaphoreType.DMA((2, 7)),
                            pltpu.SemaphoreType.DMA((2,))]),
        out_shape=[SDS((seq, D_IN), F32), SDS((seq, D_MODEL), BF16), SDS((N_DEV, D_MODEL, W_BLK), BF16),
                   SDS((N_DEV,) + w_out_blk.shape, w_out_blk.dtype), SDS((N_DEV,) + small_blk.shape, small_blk.dtype)],
        compiler_params=_cparams(("arbitrary", "arbitrary")),
    )(me, x, pre_w, w_in_blk, w_out_blk, small_blk)


def _exchange_grads(blocks, repl):
    nb = len(blocks)
    n = nb + 1

    def body(*refs):
        ins, outs, sems = refs[:n], refs[n:2 * n], refs[2 * n:]
        exs = [_SlotExchange(ins[a], outs[a], *sems[3 * a:3 * a + 3], blocked=a < nb) for a in range(n)]
        for ex in exs:
            ex.start()
        for ex in exs:
            ex.wait()

    arrs = list(blocks) + [repl]
    shapes = [SDS(b.shape, b.dtype) for b in blocks] + [SDS((N_DEV,) + repl.shape, repl.dtype)]
    return pl.pallas_call(
        body, name="exchange_small", out_shape=shapes,
        in_specs=[ANY] * n, out_specs=[ANY] * n,
        scratch_shapes=EXCHANGE_SEMS * n,
    )(*arrs)


def _pack_rows(picks, name):
    arrs = [p[0] for p in picks]

    def body(*refs):
        out = refs[-1]
        out[...] = jnp.zeros_like(out)
        at = 0
        for ref, (_, row, rows, scale) in zip(refs[:-1], picks):
            out[at:at + rows, :] = ref[row:row + rows, :] * scale
            at += rows

    return pl.pallas_call(body, name=name, out_shape=SDS((8, D_MODEL), F32))(*arrs)


def _adamw(g, w, m, v):
    m2 = ADAM_B1 * m + (1.0 - ADAM_B1) * g
    v2 = ADAM_B2 * v + (1.0 - ADAM_B2) * (g * g)
    m_hat = m2 / (1.0 - ADAM_B1 ** ADAM_STEP)
    v_hat = v2 / (1.0 - ADAM_B2 ** ADAM_STEP)
    delta = -ADAM_LR * (m_hat / (jnp.sqrt(v_hat) + ADAM_EPS) + ADAM_WD * w)
    return delta, m2, v2


def _sum_slots(r_ref):
    g = r_ref[0]
    for s in range(1, r_ref.shape[0]):
        g = g + r_ref[s]
    return g


def _sum_adamw(recv, w, m, v, tr, name):
    rows, cols = w.shape

    def body(r_ref, w_ref, m_ref, v_ref, g_ref, d_ref, m2_ref, v2_ref):
        g = _sum_slots(r_ref)
        g_ref[...] = g
        d_ref[...], m2_ref[...], v2_ref[...] = _adamw(g, w_ref[...], m_ref[...], v_ref[...])

    blk = pl.BlockSpec((tr, cols), lambda i: (i, 0))
    return pl.pallas_call(
        body, name=name, grid=(rows // tr,),
        in_specs=[pl.BlockSpec((recv.shape[0], tr, cols), lambda i: (0, i, 0)), blk, blk, blk],
        out_specs=[blk] * 4, out_shape=[SDS((rows, cols), F32)] * 4,
        compiler_params=_cparams(("parallel",)),
    )(recv, w, m, v)


def _sum_adamw_pieces(recv, rows, ws, ms, vs, name, loss_row=None):
    n = len(ws)

    def body(r_ref, *refs):
        w_refs, m_refs, v_refs, outs = refs[:n], refs[n:2 * n], refs[2 * n:3 * n], refs[3 * n:]
        g = _sum_slots(r_ref)
        for i, (row, nrows) in enumerate(rows):
            gi = g[row:row + nrows, 0:ws[i].shape[1]]
            outs[i][...] = gi
            outs[n + i][...], outs[2 * n + i][...], outs[3 * n + i][...] = _adamw(
                gi, w_refs[i][...], m_refs[i][...], v_refs[i][...])
        if loss_row is not None:
            total = jnp.sum(g[loss_row:loss_row + 1, :], axis=-1, keepdims=True)
            outs[4 * n][...] = jnp.broadcast_to(total, outs[4 * n].shape)

    shapes = [SDS(w.shape, F32) for w in ws] * 4 + ([SDS((8, 128), F32)] if loss_row is not None else [])
    out = pl.pallas_call(body, name=name, out_shape=shapes)(recv, *ws, *ms, *vs)
    return [out[k * n:(k + 1) * n] for k in range(4)] + list(out[4 * n:])


def _shard_rows(t, lead):
    r = t.shape[1] // N_DEV
    t = t.reshape((lead, N_DEV, r) + t.shape[2:])
    return jnp.moveaxis(t, 1, 0)


def _pad_tile(t):
    return jnp.pad(t, ((0, 0), (0, 8 - t.shape[1]), (0, SM_LANES - t.shape[2])))


def _pack_small(wa, wx, cw, b_a, b_x):
    n = wa.shape[0]
    return jnp.concatenate([wa.reshape(n, 128, SM_LANES), wx.reshape(n, 128, SM_LANES),
                            _pad_tile(cw), _pad_tile(b_a), _pad_tile(b_x)], axis=1)


def _unpack_small(t):
    n = t.shape[0]
    return (t[:, SM_WA:SM_WA + 128].reshape(n, 4, 32, 256), t[:, SM_WX:SM_WX + 128].reshape(n, 4, 32, 256),
            t[:, SM_CW:SM_CW + 4, 0:128], t[:, SM_BA:SM_BA + 4, 0:32], t[:, SM_BX:SM_BX + 4, 0:32])


def kernel(x, pre_norm_w, w_in, conv_w, conv_b, lru_w_a, lru_b_a, lru_w_x, lru_b_x, lru_lambda, hgrn_lb_logits, hgrn_gnorm_w, w_out, post_norm_w, loss_target, m_pre_norm_w, m_w_in, m_conv_w, m_conv_b, m_lru_w_a, m_lru_b_a, m_lru_w_x, m_lru_b_x, m_lru_lambda, m_hgrn_lb_logits, m_hgrn_gnorm_w, m_w_out, m_post_norm_w, v_pre_norm_w, v_w_in, v_conv_w, v_conv_b, v_lru_w_a, v_lru_b_a, v_lru_w_x, v_lru_b_x, v_lru_lambda, v_hgrn_lb_logits, v_hgrn_gnorm_w, v_w_out, v_post_norm_w):
    seq = x.shape[1]
    x2 = x.reshape(seq, D_MODEL)
    tgt = loss_target.reshape(seq, D_MODEL)

    small_w = _pack_small(lru_w_a, lru_w_x, conv_w, lru_b_a, lru_b_x)[0]
    me = (4 * lax.axis_index("x") + 2 * lax.axis_index("y") + lax.axis_index("c")).astype(jnp.int32).reshape(1)
    p, u, w_in_all, w_out_all, small_all = _in_proj_gather(
        x2, pre_norm_w, w_in[0].astype(BF16), w_out[0].astype(BF16), small_w, me, seq)
    wa_s, wx_s, cw_s, ba_s, bx_s = _unpack_small(small_all)
    wa = jnp.moveaxis(wa_s, 0, 1).reshape(LRU_BLOCKS, LRU_BW, LRU_BW).astype(BF16)
    wx = jnp.moveaxis(wx_s, 0, 1).reshape(LRU_BLOCKS, LRU_BW, LRU_BW).astype(BF16)
    cw = jnp.moveaxis(cw_s, 0, 1).reshape(4, D_MODEL)
    ba = jnp.moveaxis(ba_s, 0, 1).reshape(1, D_MODEL)
    bx = jnp.moveaxis(bx_s, 0, 1).reshape(1, D_MODEL)

    loc = _local_step(x2, tgt, p, u, cw, conv_b, wa, wx, ba, bx, lru_lambda,
                      hgrn_lb_logits, hgrn_gnorm_w, w_out_all.reshape(D_MIX, D_MODEL), post_norm_w)

    ls, r_out, r_small = loc["lru_small"], loc["r_out"], loc["r_small"]
    core = lax.axis_index("c").astype(jnp.int32).reshape(1)
    g_sib = _grad_w_in_sibling(loc["u"], loc["dp"], core, seq)
    g_chip = _grad_w_in_own(loc["u"], loc["dp"], core, g_sib, seq)
    grad_x, pre_small, r_in = _grad_x(loc["dp"], w_in_all, x2, pre_norm_w, loc["dout"], g_chip, seq)
    g_repl = _pack_rows([(pre_small, 0, 1, 1.0), (ls, 4, 1, 1.0), (ls, 7, 1, 1.0),
                         (loc["hgrn_small"], 0, 1, 1.0), (loc["hgrn_small"], 0, 1, -1.0),
                         (loc["hgrn_small"], 1, 1, 1.0), (loc["stats"], 0, 2, 1.0)], "pack_grads")
    (r_repl,) = _exchange_grads([], g_repl)

    repl_rows = [(RP_PRE, 1), (RP_CB, 1), (RP_LAM, 1), (RP_LB0, 2), (RP_GN, 1), (RP_POST, 1)]
    o_repl = _sum_adamw_pieces(
        r_repl, repl_rows,
        [pre_norm_w, conv_b, lru_lambda, hgrn_lb_logits, hgrn_gnorm_w, post_norm_w],
        [m_pre_norm_w, m_conv_b, m_lru_lambda, m_hgrn_lb_logits, m_hgrn_gnorm_w, m_post_norm_w],
        [v_pre_norm_w, v_conv_b, v_lru_lambda, v_hgrn_lb_logits, v_hgrn_gnorm_w, v_post_norm_w],
        "adamw_repl", loss_row=RP_LOSS)
    loss = o_repl[4][0, 0]

    small_rows = [(SM_WA, 128), (SM_WX, 128), (SM_CW, 4), (SM_BA, 4), (SM_BX, 4)]
    as_rows = lambda wa_, wx_, cw_, ba_, bx_: [wa_.reshape(128, 256), wx_.reshape(128, 256), cw_.reshape(4, 128),
                                               ba_.reshape(4, 32), bx_.reshape(4, 32)]
    o_small = _sum_adamw_pieces(
        r_small, small_rows, as_rows(lru_w_a, lru_w_x, conv_w, lru_b_a, lru_b_x),
        as_rows(m_lru_w_a, m_lru_w_x, m_conv_w, m_lru_b_a, m_lru_b_x),
        as_rows(v_lru_w_a, v_lru_w_x, v_conv_w, v_lru_b_a, v_lru_b_x), "adamw_small")

    o_in = _sum_adamw(r_in, w_in[0], m_w_in[0], v_w_in[0], 128, "adamw_w_in")
    o_out = _sum_adamw(r_out, w_out[0], m_w_out[0], v_w_out[0], 64, "adamw_w_out")

    outs = [loss, grad_x.reshape(x.shape)]
    for kind in range(4):
        pre, cb, lam, lb, gn, post = o_repl[kind]
        swa, swx, scw, sba, sbx = o_small[kind]
        outs += [pre, o_in[kind][None], scw.reshape(conv_w.shape), cb, swa.reshape(lru_w_a.shape),
                 sba.reshape(lru_b_a.shape), swx.reshape(lru_w_x.shape), sbx.reshape(lru_b_x.shape),
                 lam, lb, gn, o_out[kind][None], post]
    return tuple(outs)
```

```python
import jax
import jax.numpy as jnp
from jax import lax
from jax.experimental import pallas as pl
from jax.experimental.pallas import tpu as pltpu

F32 = jnp.float32
BF16 = jnp.bfloat16
SDS = jax.ShapeDtypeStruct

D_MODEL = 1024
D_IN = 6144
N_DEV = 8
N_CHIPS = 4
W_BLK = D_IN // N_DEV
D_MIX = 2048
LRU_BLOCKS = 4
LRU_BW = 256
LRU_C = 8.0
LANES = 128
LANE_GROUPS = D_MODEL // LANES
N_HEADS = 8
HEAD_D = 128
CHUNK = 128
SUB = 32
N_SUB = CHUNK // SUB
HGRN_FWD_STEP_CHUNKS = 4
HGRN_STEP_CHUNKS = 4
EXP_CLAMP = 80.0
EPS = 1e-6

ADAM_LR = 0.001
ADAM_B1 = 0.9
ADAM_B2 = 0.999
ADAM_EPS = 1e-08
ADAM_WD = 0.01
ADAM_STEP = 10

VMEM_LIMIT = 56 * 1024 * 1024

NN = (((1,), (0,)), ((), ()))
NT = (((1,), (1,)), ((), ()))
TN = (((0,), (0,)), ((), ()))

SM_LANES = 256
SM_WA = 0
SM_WX = 128
SM_CW = 256
SM_BA = 264
SM_BX = 272
SM_ROWS = 280

RP_PRE, RP_CB, RP_LAM, RP_LB0, RP_LB1, RP_GN, RP_POST, RP_LOSS = range(8)


def _dot(a, b, dims=NN):
    return lax.dot_general(a, b, dims, preferred_element_type=F32)


def _sigmoid(x):
    return 0.5 * jnp.tanh(0.5 * x) + 0.5


def _sigmoid_pos(x):
    return 1.0 / (1.0 + jnp.exp(-x))


def _cparams(sem, vmem=VMEM_LIMIT):
    return pltpu.CompilerParams(dimension_semantics=sem, vmem_limit_bytes=vmem)


def _iota(shape, axis):
    return lax.broadcasted_iota(jnp.int32, shape, axis)


def _softplus_neg(lam):
    z = -lam
    e = jnp.exp(-jnp.abs(z))
    u = 1.0 + e
    log1p_e = jnp.where(u == 1.0, e, jnp.log(u) * (e / (u - 1.0)))
    sp = jnp.maximum(z, 0.0) + log1p_e
    dsp = -jnp.where(z >= 0.0, 1.0 / u, e / u)
    return sp, dsp


def _neg_expm1(x):
    poly = x * (1.0 + x * (1.0 / 2 + x * (1.0 / 6 + x * (1.0 / 24 + x * (1.0 / 120)))))
    return jnp.where(x > -1.0 / 16, -poly, 1.0 - jnp.exp(x))


def _conv_taps(lx, prev8, cw_ref, cb_ref, tile):
    xc = cb_ref[...] + cw_ref[3:4, :] * lx
    for j in (1, 2, 3):
        xc = xc + cw_ref[3 - j:4 - j, :] * pltpu.roll(lx, j, 0)
    row8 = _iota((8, D_MODEL), 0)
    last8 = lx[tile - 8:tile, :]
    fix = jnp.zeros((8, D_MODEL), F32)
    for j in (1, 2, 3):
        wrong = pltpu.roll(last8, j, 0)
        right = pltpu.roll(prev8, j, 0)
        fix = fix + cw_ref[3 - j:4 - j, :] * jnp.where(row8 < j, right - wrong, 0.0)
    return xc, fix


def _lru_gates(xcs, wa, wx, ba, bx, sp):
    xb = xcs.astype(BF16)
    r = _sigmoid_pos(_dot(xb, wa) + ba)
    i = _sigmoid(_dot(xb, wx) + bx)
    la = (-LRU_C * sp) * r
    a = jnp.exp(la)
    one_minus_a2 = _neg_expm1(2.0 * la)
    return r, i, a, one_minus_a2


def _lru_forward(p, conv_w, conv_b, wa, wx, ba, bx, lam, seq):
    tile = min(1024, seq // 2)
    seg = tile // 8
    pitch = seg + 4

    def body(lx_ref, gt_ref, cw_ref, cb_ref, wa_ref, wx_ref, ba_ref, bx_ref, lam_ref,
             h_ref, y_ref, ext, hcar, xc_scr, a_scr, u_scr):
        @pl.when(pl.program_id(0) == 0)
        def _():
            ext[0:8, :] = jnp.zeros((8, D_MODEL), F32)
            hcar[...] = jnp.zeros_like(hcar)

        lx = lx_ref[...]
        ext[8:8 + tile, :] = lx
        xc = cb_ref[...] + cw_ref[3:4, :] * lx
        for j in (1, 2, 3):
            xc = xc + cw_ref[3 - j:4 - j, :] * ext[8 - j:8 - j + tile, :]
        xc_scr[...] = xc
        ext[0:8, :] = lx_ref[tile - 8:tile, :]
        sp, _ = _softplus_neg(lam_ref[...])
        for n in range(LRU_BLOCKS):
            sl = slice(n * LRU_BW, (n + 1) * LRU_BW)
            xcs = xc_scr[:, sl]
            _, i, a, ne = _lru_gates(xcs, wa_ref[n], wx_ref[n], ba_ref[:, sl], bx_ref[:, sl], sp[:, sl])
            u = jnp.sqrt(ne) * (i * xcs)
            for half in range(2):
                lanes = slice(half * LANES, (half + 1) * LANES)
                for s in range(8):
                    a_scr[2 * n + half, s * pitch:s * pitch + seg, :] = a[s * seg:(s + 1) * seg, lanes]
                    u_scr[2 * n + half, s * pitch:s * pitch + seg, :] = u[s * seg:(s + 1) * seg, lanes]

        def step(j, carry):
            rows = pl.ds(j, 8, stride=pitch)
            out = []
            for gi in range(LANE_GROUPS):
                hl, al = carry[gi]
                a = a_scr.at[gi][rows, :]
                hl = a * hl + u_scr.at[gi][rows, :]
                al = a * al
                u_scr.at[gi][rows, :] = hl
                a_scr.at[gi][rows, :] = al
                out.append((hl, al))
            return tuple(out)

        init = tuple((jnp.zeros((8, LANES), F32), jnp.ones((8, LANES), F32)) for _ in range(LANE_GROUPS))
        fin = lax.fori_loop(0, seg, step, init)
        hl = jnp.concatenate([f[0] for f in fin], axis=1)
        al = jnp.concatenate([f[1] for f in fin], axis=1)
        row8 = _iota((8, D_MODEL), 0)
        for k in (1, 2, 4):
            m = row8 >= k
            hl = jnp.where(m, hl + al * pltpu.roll(hl, k, 0), hl)
            al = jnp.where(m, al * pltpu.roll(al, k, 0), al)
        ends = hl + al * hcar[...]
        before = jnp.where(row8 == 0, hcar[...], pltpu.roll(ends, 1, 0))
        hcar[...] = jnp.broadcast_to(ends[7:8, :], (8, D_MODEL))
        for s in range(8):
            rows = slice(s * seg, (s + 1) * seg)
            for gi in range(LANE_GROUPS):
                lanes = slice(gi * LANES, (gi + 1) * LANES)
                g = gt_ref[rows, lanes]
                h = (u_scr[gi, s * pitch:s * pitch + seg, :]
                     + a_scr[gi, s * pitch:s * pitch + seg, :] * before[s:s + 1, lanes])
                h_ref[rows, lanes] = h
                y_ref[rows, lanes] = (h * (g * _sigmoid(g))).astype(BF16)

    full = lambda shape: pl.BlockSpec(shape, lambda t: (0,) * len(shape))
    return pl.pallas_call(
        body, name="lru_fwd", grid=(seq // tile,),
        in_specs=[pl.BlockSpec((tile, D_MODEL), lambda t: (t, 0)),
                  pl.BlockSpec((tile, D_MODEL), lambda t: (t, 1)),
                  full((4, D_MODEL)), full((1, D_MODEL)),
                  full((LRU_BLOCKS, LRU_BW, LRU_BW)), full((LRU_BLOCKS, LRU_BW, LRU_BW)),
                  full((1, D_MODEL)), full((1, D_MODEL)), full((1, D_MODEL))],
        out_specs=[pl.BlockSpec((tile, D_MODEL), lambda t: (t, 0)),
                   pl.BlockSpec((tile, D_MODEL), lambda t: (t, 0))],
        out_shape=[SDS((seq, D_MODEL), F32), SDS((seq, D_MODEL), BF16)],
        scratch_shapes=[pltpu.VMEM((tile + 8, D_MODEL), F32), pltpu.VMEM((8, D_MODEL), F32),
                        pltpu.VMEM((tile, D_MODEL), F32), pltpu.VMEM((LANE_GROUPS, 8 * pitch, LANES), F32),
                        pltpu.VMEM((LANE_GROUPS, 8 * pitch, LANES), F32)],
        compiler_params=_cparams(("arbitrary",)),
    )(p, p, conv_w, conv_b, wa, wx, ba, bx, lam)


def _lru_backward(p, h, dymix, conv_w, conv_b, wa, wx, ba, bx, lam, seq):
    tile = min(512, seq // 2)
    nt = seq // tile
    seg = tile // 8
    pitch = seg + 4
    t8 = tile // 8

    def body(lx_ref, lxh_ref, gt_ref, h_ref, hh_ref, dy_ref, cw_ref, cb_ref, wa_ref, wx_ref, ba_ref,
             bx_ref, lam_ref, dp_ref, gwa_ref, gwx_ref, gsm_ref,
             lamcar, anext, dxc8, xc_scr, r_scr, i_scr, a_scr, m_scr, rm_scr, l_scr, dxc_scr, c3, l3):
        step = pl.program_id(0)
        first_tile = step == nt - 1

        @pl.when(step == 0)
        def _():
            lamcar[...] = jnp.zeros_like(lamcar)
            anext[...] = jnp.zeros_like(anext)
            dxc8[...] = jnp.zeros_like(dxc8)
            gwa_ref[...] = jnp.zeros_like(gwa_ref)
            gwx_ref[...] = jnp.zeros_like(gwx_ref)
            gsm_ref[...] = jnp.zeros_like(gsm_ref)

        keep = jnp.where(first_tile, 0.0, 1.0)
        lx = lx_ref[...]
        prev8 = lxh_ref[...] * keep
        xc, fix = _conv_taps(lx, prev8, cw_ref, cb_ref, tile)
        xc_scr[...] = xc
        xc_scr[0:8, :] = xc_scr[0:8, :] + fix
        sp, dsp = _softplus_neg(lam_ref[...])
        for n in range(LRU_BLOCKS):
            sl = slice(n * LRU_BW, (n + 1) * LRU_BW)
            r, i, a, ne = _lru_gates(xc_scr[:, sl], wa_ref[n], wx_ref[n], ba_ref[:, sl], bx_ref[:, sl],
                                     sp[:, sl])
            r_scr[:, sl] = r
            i_scr[:, sl] = i
            a_scr[:, sl] = a
            m_scr[:, sl] = jnp.sqrt(ne)
            rm_scr[:, sl] = lax.rsqrt(ne)

        g = gt_ref[...]
        sg = _sigmoid(g)
        dy = dy_ref[...]
        hv = h_ref[...]
        dp_ref[:, D_MODEL:2 * D_MODEL] = (dy * hv * (sg * (1.0 + g * (1.0 - sg)))).astype(BF16)

        rowt = _iota((tile, D_MODEL), 0)
        av = a_scr[...]
        dh = dy * (g * sg)
        cnext = jnp.where(rowt == tile - 1, anext[...][0:1, :], pltpu.roll(av, tile - 1, 0))
        anext[...] = jnp.broadcast_to(av[0:1, :], (8, D_MODEL))
        for gi in range(LANE_GROUPS):
            lanes = slice(gi * LANES, (gi + 1) * LANES)
            for s in range(8):
                l3[gi, s * pitch:s * pitch + seg, :] = dh[s * seg:(s + 1) * seg, lanes]
                c3[gi, s * pitch:s * pitch + seg, :] = cnext[s * seg:(s + 1) * seg, lanes]

        def step(jj, carry):
            rows = pl.ds(seg - 1 - jj, 8, stride=pitch)
            out = []
            for gi in range(LANE_GROUPS):
                ll, cl = carry[gi]
                c = c3.at[gi][rows, :]
                ll = c * ll + l3.at[gi][rows, :]
                cl = c * cl
                l3.at[gi][rows, :] = ll
                c3.at[gi][rows, :] = cl
                out.append((ll, cl))
            return tuple(out)

        init = tuple((jnp.zeros((8, LANES), F32), jnp.ones((8, LANES), F32)) for _ in range(LANE_GROUPS))
        fin = lax.fori_loop(0, seg, step, init)
        ll = jnp.concatenate([f[0] for f in fin], axis=1)
        cl = jnp.concatenate([f[1] for f in fin], axis=1)
        row8 = _iota((8, D_MODEL), 0)
        for k in (1, 2, 4):
            m = row8 < 8 - k
            ll = jnp.where(m, ll + cl * pltpu.roll(ll, 8 - k, 0), ll)
            cl = jnp.where(m, cl * pltpu.roll(cl, 8 - k, 0), cl)
        firsts = ll + cl * lamcar[...]
        after = jnp.where(row8 == 7, lamcar[...], pltpu.roll(firsts, 7, 0))
        lamcar[...] = jnp.broadcast_to(firsts[0:1, :], (8, D_MODEL))
        for s in range(8):
            for gi in range(LANE_GROUPS):
                lanes = slice(gi * LANES, (gi + 1) * LANES)
                l_scr[s * seg:(s + 1) * seg, lanes] = (l3[gi, s * pitch:s * pitch + seg, :]
                                                        + c3[gi, s * pitch:s * pitch + seg, :] * after[s:s + 1, lanes])

        hprev = jnp.where(rowt == 0, hh_ref[...][7:8, :] * keep, pltpu.roll(hv, 1, 0))
        for n in range(LRU_BLOCKS):
            sl = slice(n * LRU_BW, (n + 1) * LRU_BW)
            lamv = l_scr[:, sl]
            xcs = xc_scr[:, sl]
            r = r_scr[:, sl]
            i = i_scr[:, sl]
            a = a_scr[:, sl]
            mult = m_scr[:, sl]
            d_la = lamv * hprev[:, sl] * a - (lamv * i * xcs) * (a * a * rm_scr[:, sl])
            d_pr = d_la * (-LRU_C * sp[:, sl]) * r * (1.0 - r)
            d_pi = (lamv * mult * xcs) * i * (1.0 - i)
            gsm_ref[7:8, sl] += jnp.sum(d_la * r, axis=0, keepdims=True) * (-LRU_C) * dsp[:, sl]
            gsm_ref[5:6, sl] += jnp.sum(d_pr, axis=0, keepdims=True)
            gsm_ref[6:7, sl] += jnp.sum(d_pi, axis=0, keepdims=True)
            xb = xcs.astype(BF16)
            prb = d_pr.astype(BF16)
            pib = d_pi.astype(BF16)
            gwa_ref[n] += _dot(xb, prb, TN)
            gwx_ref[n] += _dot(xb, pib, TN)
            dxc_scr[:, sl] = lamv * mult * i + _dot(prb, wa_ref[n], NT) + _dot(pib, wx_ref[n], NT)

        dxc = dxc_scr[...]
        gsm_ref[4:5, :] += jnp.sum(dxc, axis=0, keepdims=True)
        last8 = lx[tile - 8:tile, :]
        first8 = dxc[0:8, :]
        dlx = cw_ref[3:4, :] * dxc
        gsm_ref[3:4, :] += jnp.sum(dxc * lx, axis=0, keepdims=True)
        fix = jnp.zeros((8, D_MODEL), F32)
        for j in (1, 2, 3):
            w = cw_ref[3 - j:4 - j, :]
            dlx = dlx + w * pltpu.roll(dxc, tile - j, 0)
            fix = fix + w * jnp.where(row8 + j >= 8,
                                      pltpu.roll(dxc8[...], 8 - j, 0) - pltpu.roll(first8, 8 - j, 0), 0.0)
            halo = jnp.where(row8 < j, pltpu.roll(prev8, j, 0) - pltpu.roll(last8, j, 0), 0.0)
            gsm_ref[3 - j:4 - j, :] += (jnp.sum(dxc * pltpu.roll(lx, j, 0), axis=0, keepdims=True)
                                        + jnp.sum(first8 * halo, axis=0, keepdims=True))
        dxc8[...] = first8
        dp_ref[:, 0:D_MODEL] = dlx.astype(BF16)
        top = tile - 8
        dp_ref[top:tile, 0:D_MODEL] = (dlx[top:tile, :] + fix).astype(BF16)

    rev = lambda t: (nt - 1 - t, 0)
    halo_idx = lambda t: (jnp.maximum((nt - 1 - t) * t8 - 1, 0), 0)
    full = lambda shape: pl.BlockSpec(shape, lambda t: (0,) * len(shape))
    big = lambda: pltpu.VMEM((tile, D_MODEL), F32)
    return pl.pallas_call(
        body, name="lru_bwd", grid=(nt,),
        in_specs=[pl.BlockSpec((tile, D_MODEL), rev),
                  pl.BlockSpec((8, D_MODEL), halo_idx),
                  pl.BlockSpec((tile, D_MODEL), lambda t: (nt - 1 - t, 1)),
                  pl.BlockSpec((tile, D_MODEL), rev),
                  pl.BlockSpec((8, D_MODEL), halo_idx),
                  pl.BlockSpec((tile, D_MODEL), rev),
                  full((4, D_MODEL)), full((1, D_MODEL)),
                  full((LRU_BLOCKS, LRU_BW, LRU_BW)), full((LRU_BLOCKS, LRU_BW, LRU_BW)),
                  full((1, D_MODEL)), full((1, D_MODEL)), full((1, D_MODEL))],
        out_specs=[pl.BlockSpec((tile, 2 * D_MODEL), rev),
                   full((LRU_BLOCKS, LRU_BW, LRU_BW)), full((LRU_BLOCKS, LRU_BW, LRU_BW)),
                   full((8, D_MODEL))],
        out_shape=[SDS((seq, D_IN), BF16), SDS((LRU_BLOCKS, LRU_BW, LRU_BW), F32),
                   SDS((LRU_BLOCKS, LRU_BW, LRU_BW), F32), SDS((8, D_MODEL), F32)],
        scratch_shapes=[pltpu.VMEM((8, D_MODEL), F32), pltpu.VMEM((8, D_MODEL), F32),
                        pltpu.VMEM((8, D_MODEL), F32)] + [big() for _ in range(8)]
                       + [pltpu.VMEM((LANE_GROUPS, 8 * pitch, LANES), F32)] * 2,
        compiler_params=_cparams(("arbitrary",)),
    )(p, p, p, h, h, dymix, conv_w, conv_b, wa, wx, ba, bx, lam)


def _tri_matmul(tri, g):
    hi = g.astype(BF16)
    lo = (g - hi.astype(F32)).astype(BF16)
    return _dot(tri, lo) + _dot(tri, hi)


def _hgrn_gate_terms(q, fr, lbl):
    lb = _sigmoid_pos(lbl[0:1, :] - lbl[1:2, :])
    half = 0.5 * (1.0 - lb)
    tf = jnp.tanh(0.5 * fr)
    f = (lb + half) + half * tf
    hq = 0.5 * q
    tq = jnp.tanh(hq)
    return lb, tf, f, tq, hq * tq + hq


def _hgrn_decay(bh):
    zero = jnp.zeros((1, bh.shape[1]), F32)
    rho = [zero] + [bh[s * SUB - 1:s * SUB, :] for s in range(1, N_SUB + 1)]
    start = _sub_rows(rho[0:N_SUB])
    end = _sub_rows(rho[1:N_SUB + 1])
    mid = 0.5 * (start + end)
    blast = rho[N_SUB]
    e_on = jnp.exp(bh - start)
    e_off = jnp.exp(end - bh)
    scales = [_sub_rows([jnp.exp(rho[i] - rho[j + 1]) if i > j else zero for i in range(N_SUB)])
              for j in range(N_SUB - 1)]
    return dict(eq0=jnp.exp(jnp.minimum(bh - mid, EXP_CLAMP)), ek0=jnp.exp(jnp.minimum(mid - bh, EXP_CLAMP)),
                e_on=e_on, e_off=e_off, scales=scales,
                eb=e_on * _sub_rows([jnp.exp(r) for r in rho[0:N_SUB]]),
                ekst=e_off * _sub_rows([jnp.exp(blast - r) for r in rho[1:N_SUB + 1]]),
                ebl=jnp.exp(blast))


def _sub_rows(vecs):
    return jnp.concatenate([jnp.broadcast_to(v, (SUB, v.shape[1])) for v in vecs], axis=0)


def _hgrn_operands(qs, k, dec, qt_scr, kt_scr):
    sub = jnp.right_shift(_iota(qs.shape, 0), 5)
    qon = qs * dec["e_on"]
    koff = k * dec["e_off"]
    qt_scr[0] = (qs * dec["eq0"]).astype(BF16)
    kt_scr[0] = (k * dec["ek0"]).astype(BF16)
    for j in range(N_SUB - 1):
        qt_scr[j + 1] = (qon * dec["scales"][j]).astype(BF16)
        kt_scr[j + 1] = jnp.where(sub == j, koff, 0.0).astype(BF16)
    return koff


def _hgrn_head_scores(qt_scr, kt_scr, sl, diag):
    a = jnp.where(diag, _dot(qt_scr[0, :, sl], kt_scr[0, :, sl], NT), 0.0)
    for j in range(1, N_SUB):
        a = a + _dot(qt_scr[j, :, sl], kt_scr[j, :, sl], NT)
    return a


def _hgrn_forward(p, lbl, gw, seq):
    nc = seq // CHUNK
    assert SUB == 32

    def body(q_ref, f_ref, v_ref, hg_ref, lbl_ref, gw_ref, y_ref, o_ref, st_ref,
             s_scr, qt_scr, kt_scr, qin_scr, kst_scr, vb_scr, a_scr):
        @pl.when(pl.program_id(0) == 0)
        def _():
            s_scr[...] = jnp.zeros_like(s_scr)

        r = _iota((CHUNK, CHUNK), 0)
        c = _iota((CHUNK, CHUNK), 1)
        tri = jnp.where(c <= r, 1.0, 0.0).astype(BF16)
        diag = (jnp.right_shift(r, 5) == jnp.right_shift(c, 5)) & (c <= r)
        heads = [slice(h * HEAD_D, (h + 1) * HEAD_D) for h in range(N_HEADS)]
        for cc in range(HGRN_FWD_STEP_CHUNKS):
            rows = slice(cc * CHUNK, (cc + 1) * CHUNK)
            q = q_ref[rows, :]
            _, _, f, _, qs = _hgrn_gate_terms(q, f_ref[rows, :], lbl_ref[...])
            k = 1.0 - f
            dec = _hgrn_decay(_tri_matmul(tri, jnp.log(f)))
            _hgrn_operands(qs, k, dec, qt_scr, kt_scr)
            qin_scr[...] = (qs * dec["eb"]).astype(BF16)
            kst_scr[...] = (k * dec["ekst"]).astype(BF16)
            vb_scr[...] = v_ref[rows, :].astype(BF16)
            ebl = dec["ebl"]
            hg = hg_ref[rows, :]
            gate = gw_ref[...] * (hg * _sigmoid(hg))
            stb = []
            for h, sl in enumerate(heads):
                st = s_scr[h]
                st_ref[cc, h] = st
                stb.append(st.astype(BF16))
                s_scr[h] = st * ebl[:, sl] + _dot(vb_scr[:, sl], kst_scr[:, sl], TN)
            for h, sl in enumerate(heads):
                a_scr[h] = _hgrn_head_scores(qt_scr, kt_scr, sl, diag).astype(BF16)
            for h, sl in enumerate(heads):
                o = _dot(a_scr[h], vb_scr[:, sl]) + _dot(qin_scr[:, sl], stb[h], NT)
                o_ref[rows, sl] = o
                rs = lax.rsqrt(jnp.mean(o * o, axis=-1, keepdims=True) + EPS)
                y_ref[rows, sl] = ((o * rs) * gate[:, sl]).astype(BF16)

    col = lambda j: pl.BlockSpec((HGRN_FWD_STEP_CHUNKS * CHUNK, D_MODEL), lambda c: (c, j))
    par = lambda rows: pl.BlockSpec((rows, D_MODEL), lambda c: (0, 0))
    return pl.pallas_call(
        body, name="hgrn_fwd", grid=(nc // HGRN_FWD_STEP_CHUNKS,),
        in_specs=[col(2), col(3), col(4), col(5), par(2), par(1)],
        out_specs=[col(0), col(0),
                   pl.BlockSpec((HGRN_FWD_STEP_CHUNKS, N_HEADS, HEAD_D, HEAD_D), lambda c: (c, 0, 0, 0))],
        out_shape=[SDS((seq, D_MODEL), BF16), SDS((seq, D_MODEL), F32),
                   SDS((nc, N_HEADS, HEAD_D, HEAD_D), F32)],
        scratch_shapes=[pltpu.VMEM((N_HEADS, HEAD_D, HEAD_D), F32),
                        pltpu.VMEM((N_SUB, CHUNK, D_MODEL), BF16), pltpu.VMEM((N_SUB, CHUNK, D_MODEL), BF16)]
                       + [pltpu.VMEM((CHUNK, D_MODEL), BF16)] * 3 + [pltpu.VMEM((N_HEADS, CHUNK, CHUNK), BF16)],
        compiler_params=_cparams(("arbitrary",)),
    )(p, p, p, p, lbl, gw)


def _hgrn_backward(p, o, states, dymix, lbl, gw, dp_full, g_w_out, g_small, seq):
    step_rows = HGRN_STEP_CHUNKS * CHUNK
    ns = seq // step_rows

    def body(q_ref, f_ref, v_ref, hg_ref, o_ref, st_ref, dy_ref, lbl_ref, gw_ref, dpin_ref, go_ref, gs_ref,
             dpo_ref, gsm_ref, ro_ref, rs_ref, ds_scr, dp_buf, dp_sems, *rest):
        del dpin_ref
        scratch, sems = rest[:14], rest[14:]
        step = pl.program_id(0)
        slot = step % 2
        exs = [_SlotExchange(go_ref, ro_ref, *sems[0:3], blocked=True),
               _SlotExchange(gs_ref, rs_ref, *sems[3:6], blocked=True)]

        @pl.when(step == 0)
        def _():
            for ex in exs:
                ex.start()

        def out_copy(s, blk):
            rows = pl.ds(pl.multiple_of(blk * step_rows, step_rows), step_rows)
            return pltpu.make_async_copy(dp_buf.at[s], dpo_ref.at[rows, pl.ds(2 * D_MODEL, 4 * D_MODEL)],
                                         dp_sems.at[s])

        @pl.when(step == 0)
        def _():
            ds_scr[...] = jnp.zeros_like(ds_scr)
            gsm_ref[...] = jnp.zeros_like(gsm_ref)

        @pl.when(step >= 2)
        def _():
            out_copy(slot, ns + 1 - step).wait()

        for cc in reversed(range(HGRN_STEP_CHUNKS)):
            chunk(cc, q_ref, f_ref, v_ref, hg_ref, o_ref, st_ref, dy_ref, lbl_ref, gw_ref, gsm_ref, ds_scr,
                  dp_buf.at[slot], *scratch)

        out_copy(slot, ns - 1 - step).start()

        @pl.when(step == ns - 1)
        def _():
            out_copy(1 - slot, 1).wait()
            out_copy(slot, 0).wait()
            for ex in exs:
                ex.wait()

    def chunk(cc, q_ref, f_ref, v_ref, hg_ref, o_ref, st_ref, dy_ref, lbl_ref, gw_ref, gsm_ref, ds_scr, dp_ref,
              qt_scr, kt_scr, qin_scr, kst_scr, vb_scr, dob_scr, g_scr, h_scr, dqi_scr, dks_scr, sd_scr,
              a_scr, da_scr, da0_scr):
        rows = slice(cc * CHUNK, (cc + 1) * CHUNK)
        r = _iota((CHUNK, CHUNK), 0)
        c = _iota((CHUNK, CHUNK), 1)
        tri = jnp.where(c <= r, 1.0, 0.0).astype(BF16)
        triu = jnp.where(c >= r, 1.0, 0.0).astype(BF16)
        diag = (jnp.right_shift(r, 5) == jnp.right_shift(c, 5)) & (c <= r)
        row = _iota((CHUNK, D_MODEL), 0)
        sub = jnp.right_shift(row, 5)

        q = q_ref[rows, :]
        lb, tf, f, tq, qs = _hgrn_gate_terms(q, f_ref[rows, :], lbl_ref[...])
        sig = 0.5 * tf + 0.5
        sq = 0.5 * tq + 0.5
        k = 1.0 - f
        dec = _hgrn_decay(_tri_matmul(tri, jnp.log(f)))
        eb, ekst, ebl = dec["eb"], dec["ekst"], dec["ebl"]
        koff = _hgrn_operands(qs, k, dec, qt_scr, kt_scr)
        qin_scr[...] = (qs * eb).astype(BF16)
        kst_scr[...] = (k * ekst).astype(BF16)
        vb_scr[...] = v_ref[rows, :].astype(BF16)
        hg = hg_ref[rows, :]
        sh = _sigmoid(hg)
        dy = dy_ref[rows, :]
        gwv = gw_ref[...]
        d_onw = dy * (hg * sh)
        d_on = d_onw * gwv
        d_gate = dy * gwv * (sh * (1.0 + hg * (1.0 - sh)))

        heads = [slice(h * HEAD_D, (h + 1) * HEAD_D) for h in range(N_HEADS)]
        for h, sl in enumerate(heads):
            o = o_ref[rows, sl]
            rs = lax.rsqrt(jnp.mean(o * o, axis=-1, keepdims=True) + EPS)
            on = o * rs
            dp_ref[rows, 3 * D_MODEL + h * HEAD_D:3 * D_MODEL + (h + 1) * HEAD_D] = (d_gate[:, sl] * on).astype(BF16)
            gsm_ref[1:2, sl] += jnp.sum(d_onw[:, sl] * on, axis=0, keepdims=True)
            d_onh = d_on[:, sl]
            dob_scr[:, sl] = (rs * (d_onh - on * jnp.mean(d_onh * on, axis=-1, keepdims=True))).astype(BF16)
        for h, sl in enumerate(heads):
            a_scr[h] = _hgrn_head_scores(qt_scr, kt_scr, sl, diag).astype(BF16)
            da = _dot(dob_scr[:, sl], vb_scr[:, sl], NT)
            da_scr[h] = da.astype(BF16)
            da0_scr[h] = jnp.where(diag, da, 0.0).astype(BF16)
        for h, sl in enumerate(heads):
            st = st_ref[cc, h]
            dst = ds_scr[h]
            dstb = dst.astype(BF16)
            dp_ref[rows, 2 * D_MODEL + h * HEAD_D:2 * D_MODEL + (h + 1) * HEAD_D] = (
                _dot(a_scr[h], dob_scr[:, sl], TN) + _dot(kst_scr[:, sl], dstb, NT)).astype(BF16)
            dqi_scr[:, sl] = _dot(dob_scr[:, sl], st.astype(BF16))
            dks_scr[:, sl] = _dot(vb_scr[:, sl], dstb)
            sd_scr[0:1, sl] = jnp.sum(st * dst, axis=0, keepdims=True)
            ds_scr[h] = dst * ebl[:, sl] + _dot(dob_scr[:, sl], qin_scr[:, sl], TN)
        for h, sl in enumerate(heads):
            g_scr[0, :, sl] = _dot(da0_scr[h], kt_scr[0, :, sl])
            h_scr[0, :, sl] = _dot(da0_scr[h], qt_scr[0, :, sl], TN)
            for j in range(1, N_SUB):
                g_scr[j, :, sl] = _dot(da_scr[h], kt_scr[j, :, sl])
                h_scr[j, :, sl] = _dot(da_scr[h], qt_scr[j, :, sl], TN)

        g0 = g_scr[0]
        h0 = h_scr[0]
        dq_inter = eb * dqi_scr[...]
        d_kst = ekst * dks_scr[...]
        db = qs * dq_inter - k * d_kst + qt_scr[0].astype(F32) * g0 - kt_scr[0].astype(F32) * h0
        gq = jnp.zeros((CHUNK, D_MODEL), F32)
        hsel = jnp.zeros((CHUNK, D_MODEL), F32)
        for j in range(N_SUB - 1):
            gj = g_scr[j + 1]
            gq = gq + dec["scales"][j] * gj
            db = db + qt_scr[j + 1].astype(F32) * gj
            hsel = jnp.where(sub == j, h_scr[j + 1], hsel)
        db = db - koff.astype(BF16).astype(F32) * hsel
        d_q = dec["eq0"] * g0 + dec["e_on"] * gq + dq_inter
        d_k = dec["ek0"] * h0 + dec["e_off"] * hsel + d_kst
        db_last = jnp.sum(k * d_kst, axis=0, keepdims=True) + ebl * sd_scr[0:1, :]
        db = db + jnp.where(row == CHUNK - 1, db_last, 0.0)
        dg = _tri_matmul(triu, db)
        d_f = dg / f - d_k
        dp_ref[rows, D_MODEL:2 * D_MODEL] = (d_f * (1.0 - lb) * sig * (1.0 - sig)).astype(BF16)
        gsm_ref[0:1, :] += jnp.sum(d_f * (1.0 - sig), axis=0, keepdims=True) * (lb * (1.0 - lb))
        dp_ref[rows, 0:D_MODEL] = (d_q * (sq * (1.0 + q * (1.0 - sq)))).astype(BF16)

    rc = lambda c: ns - 1 - c
    col = lambda j: pl.BlockSpec((step_rows, D_MODEL), lambda c: (rc(c), j))
    par = lambda rows: pl.BlockSpec((rows, D_MODEL), lambda c: (0, 0))
    return pl.pallas_call(
        body, name="hgrn_bwd", grid=(ns,),
        in_specs=[col(2), col(3), col(4), col(5), col(0),
                  pl.BlockSpec((HGRN_STEP_CHUNKS, N_HEADS, HEAD_D, HEAD_D), lambda c: (rc(c), 0, 0, 0)),
                  col(1), par(2), par(1), ANY, ANY, ANY],
        out_specs=[ANY, par(8), ANY, ANY],
        out_shape=[SDS((seq, D_IN), BF16), SDS((8, D_MODEL), F32), SDS(g_w_out.shape, F32),
                   SDS(g_small.shape, F32)],
        input_output_aliases={9: 0},
        scratch_shapes=[pltpu.VMEM((N_HEADS, HEAD_D, HEAD_D), F32),
                        pltpu.VMEM((2, step_rows, 4 * D_MODEL), BF16), pltpu.SemaphoreType.DMA((2,)),
                        pltpu.VMEM((N_SUB, CHUNK, D_MODEL), BF16), pltpu.VMEM((N_SUB, CHUNK, D_MODEL), BF16)]
                       + [pltpu.VMEM((CHUNK, D_MODEL), BF16)] * 4
                       + [pltpu.VMEM((N_SUB, CHUNK, D_MODEL), F32)] * 2 + [pltpu.VMEM((CHUNK, D_MODEL), F32)] * 2
                       + [pltpu.VMEM((8, D_MODEL), F32)] + [pltpu.VMEM((N_HEADS, CHUNK, CHUNK), BF16)] * 3
                       + EXCHANGE_SEMS * 2,
        compiler_params=_cparams(("arbitrary",)),
    )(p, p, p, p, o, states, dymix, lbl, gw, dp_full, g_w_out, g_small)


def _out_proj(yl, yh, wo, x, tgt, post_w, seq):
    tm = 512

    def body(yl_ref, yh_ref, wo_ref, x_ref, tg_ref, pw_ref, dymix_ref, dout_ref, gwo_ref, st_ref):
        @pl.when(pl.program_id(0) == 0)
        def _():
            gwo_ref[...] = jnp.zeros_like(gwo_ref)
            st_ref[...] = jnp.zeros_like(st_ref)

        ylv = yl_ref[...]
        yhv = yh_ref[...]
        y = _dot(ylv, wo_ref[0:D_MODEL, :]) + _dot(yhv, wo_ref[D_MODEL:D_MIX, :])
        r2 = lax.rsqrt(jnp.mean(y * y, axis=-1, keepdims=True) + EPS)
        yn = y * r2
        pw = pw_ref[...]
        e = (x_ref[...] + yn * pw) - tg_ref[...]
        st_ref[1:2, :] += jnp.sum(e * e, axis=0, keepdims=True) * (0.5 / D_MODEL)
        dout = e * (1.0 / D_MODEL)
        dout_ref[...] = dout
        st_ref[0:1, :] += jnp.sum(dout * yn, axis=0, keepdims=True)
        dyn = dout * pw
        dy = r2 * (dyn - yn * jnp.mean(dyn * yn, axis=-1, keepdims=True))
        dyb = dy.astype(BF16)
        dymix_ref[...] = _dot(dyb, wo_ref[...], NT)
        gwo_ref[0:D_MODEL, :] += _dot(ylv, dyb, TN)
        gwo_ref[D_MODEL:D_MIX, :] += _dot(yhv, dyb, TN)

    row = lambda w: pl.BlockSpec((tm, w), lambda m: (m, 0))
    full = lambda shape: pl.BlockSpec(shape, lambda m: (0,) * len(shape))
    once = lambda shape: pl.BlockSpec(shape, lambda m: (0,) * len(shape), pipeline_mode=pl.Buffered(1))
    return pl.pallas_call(
        body, name="out_proj", grid=(seq // tm,),
        in_specs=[row(D_MODEL), row(D_MODEL), once((D_MIX, D_MODEL)), row(D_MODEL), row(D_MODEL),
                  full((1, D_MODEL))],
        out_specs=[row(D_MIX), row(D_MODEL), once((D_MIX, D_MODEL)), full((8, D_MODEL))],
        out_shape=[SDS((seq, D_MIX), F32), SDS((seq, D_MODEL), F32), SDS((D_MIX, D_MODEL), F32),
                   SDS((8, D_MODEL), F32)],
        compiler_params=_cparams(("arbitrary",)),
    )(yl, yh, wo, x, tgt, post_w)


MESH = pl.DeviceIdType.MESH
ANY = pl.BlockSpec(memory_space=pl.ANY)
EXCHANGE_SEMS = [pltpu.SemaphoreType.DMA((N_DEV - 1,)), pltpu.SemaphoreType.DMA((N_DEV - 1,)),
                 pltpu.SemaphoreType.DMA(())]


def _mesh_pos():
    return lax.axis_index("x"), lax.axis_index("y"), lax.axis_index("c")


class _SlotExchange:
    def __init__(self, src_ref, dst_ref, send_sems, recv_sems, local_sem, blocked):
        x, y, c = _mesh_pos()
        me = 4 * x + 2 * y + c
        src = (lambda dest: src_ref.at[dest]) if blocked else (lambda dest: src_ref)
        self.local = pltpu.make_async_copy(src(me), dst_ref.at[me], local_sem)
        self.sends, self.recvs = [], []
        for k in range(1, N_DEV):
            px = 1 - x if (k >> 2) & 1 else x
            py = 1 - y if (k >> 1) & 1 else y
            pc = 1 - c if k & 1 else c
            peer = 4 * px + 2 * py + pc
            sems = dict(send_sem=send_sems.at[k - 1], recv_sem=recv_sems.at[k - 1],
                        device_id=(px, py, pc), device_id_type=MESH)
            self.sends.append(pltpu.make_async_remote_copy(src_ref=src(peer), dst_ref=dst_ref.at[me], **sems))
            self.recvs.append(pltpu.make_async_remote_copy(src_ref=dst_ref.at[peer], dst_ref=dst_ref.at[peer], **sems))

    def start(self):
        self.local.start()
        for cp in self.sends:
            cp.start()

    def wait(self):
        for cp in self.recvs:
            cp.wait_recv()
        for cp in self.sends:
            cp.wait_send()
        self.local.wait()


class _ChipExchange:
    def __init__(self, src_ref, dst_ref, send_sems, recv_sems, local_sem):
        x, y, c = _mesh_pos()
        chip = 2 * x + y
        self.local = pltpu.make_async_copy(src_ref.at[chip], dst_ref.at[chip], local_sem)
        self.sends, self.recvs = [], []
        for k in range(1, N_CHIPS):
            px = 1 - x if (k >> 1) & 1 else x
            py = 1 - y if k & 1 else y
            peer = 2 * px + py
            sems = dict(send_sem=send_sems.at[k - 1], recv_sem=recv_sems.at[k - 1],
                        device_id=(px, py, c), device_id_type=MESH)
            self.sends.append(pltpu.make_async_remote_copy(src_ref=src_ref.at[peer], dst_ref=dst_ref.at[chip], **sems))
            self.recvs.append(pltpu.make_async_remote_copy(src_ref=dst_ref.at[peer], dst_ref=dst_ref.at[peer], **sems))

    def start(self):
        self.local.start()
        for cp in self.sends:
            cp.start()

    def wait(self):
        for cp in self.recvs:
            cp.wait_recv()
        for cp in self.sends:
            cp.wait_send()
        self.local.wait()


GRAD_W_IN_TK = 2048


def _grad_w_in_sibling(u, dp, core, seq):
    tk = min(GRAD_W_IN_TK, seq)
    nk = seq // tk

    def body(core_ref, u_ref, dp_ref, g_ref):
        del core_ref

        @pl.when(pl.program_id(1) == 0)
        def _():
            g_ref[...] = jnp.zeros_like(g_ref)

        g_ref[0] += _dot(u_ref[...], dp_ref[...], TN)

    return pl.pallas_call(
        body, name="grad_w_in_sibling",
        grid_spec=pltpu.PrefetchScalarGridSpec(
            num_scalar_prefetch=1, grid=(N_CHIPS, nk),
            in_specs=[pl.BlockSpec((tk, D_MODEL), lambda n, k, c: (k, 0)),
                      pl.BlockSpec((tk, W_BLK), lambda n, k, c: (k, 2 * n + 1 - c[0]))],
            out_specs=pl.BlockSpec((1, D_MODEL, W_BLK), lambda n, k, c: (n, 0, 0))),
        out_shape=SDS((N_CHIPS, D_MODEL, W_BLK), F32),
        compiler_params=_cparams(("parallel", "arbitrary")),
    )(core, u, dp)


def _grad_w_in_own(u, dp, core, g_sib, seq):
    tk = min(GRAD_W_IN_TK, seq)
    nk = seq // tk

    def body(core_ref, u_ref, dp_ref, gsib_ref, g_ref, land, send_sem, recv_sem):
        del core_ref
        n = pl.program_id(0)
        k = pl.program_id(1)
        x, y, c = _mesh_pos()
        swap = pltpu.make_async_remote_copy(src_ref=gsib_ref, dst_ref=land, send_sem=send_sem, recv_sem=recv_sem,
                                            device_id=(x, y, 1 - c), device_id_type=MESH)

        @pl.when((n == 0) & (k == 0))
        def _():
            swap.start()

        @pl.when(k == 0)
        def _():
            g_ref[...] = jnp.zeros_like(g_ref)

        g_ref[0] += _dot(u_ref[...], dp_ref[...], TN)

        @pl.when((n == 0) & (k == nk - 1))
        def _():
            swap.wait_recv()

        @pl.when(k == nk - 1)
        def _():
            g_ref[0] += land[n]

        @pl.when((n == N_CHIPS - 1) & (k == nk - 1))
        def _():
            swap.wait_send()

    return pl.pallas_call(
        body, name="grad_w_in_own",
        grid_spec=pltpu.PrefetchScalarGridSpec(
            num_scalar_prefetch=1, grid=(N_CHIPS, nk),
            in_specs=[pl.BlockSpec((tk, D_MODEL), lambda n, k, c: (k, 0)),
                      pl.BlockSpec((tk, W_BLK), lambda n, k, c: (k, 2 * n + c[0])), ANY],
            out_specs=pl.BlockSpec((1, D_MODEL, W_BLK), lambda n, k, c: (n, 0, 0)),
            scratch_shapes=[pltpu.VMEM((N_CHIPS, D_MODEL, W_BLK), F32), pltpu.SemaphoreType.DMA(()),
                            pltpu.SemaphoreType.DMA(())]),
        out_shape=SDS((N_CHIPS, D_MODEL, W_BLK), F32),
        compiler_params=_cparams(("arbitrary", "arbitrary")),
    )(core, u, dp, g_sib)


def _grad_x(dp, w_all, x, pre_w, dout, g_chip, seq):
    tm = 512
    nm = seq // tm

    def body(dp_ref, w_ref, x_ref, pw_ref, do_ref, gsrc_ref, gx_ref, gpw_ref, recv_ref,
             send_sems, recv_sems, local_sem):
        m = pl.program_id(0)
        ex = _ChipExchange(gsrc_ref, recv_ref, send_sems, recv_sems, local_sem)

        @pl.when(m == 0)
        def _():
            ex.start()
            gpw_ref[...] = jnp.zeros_like(gpw_ref)

        du = _dot(dp_ref[:, 0:W_BLK], w_ref[0], NT)
        for j in range(1, N_DEV):
            du = du + _dot(dp_ref[:, j * W_BLK:(j + 1) * W_BLK], w_ref[j], NT)
        xv = x_ref[...]
        r1 = lax.rsqrt(jnp.mean(xv * xv, axis=-1, keepdims=True) + EPS)
        xn = xv * r1
        gpw_ref[0:1, :] += jnp.sum(du * xn, axis=0, keepdims=True)
        dxn = du * pw_ref[...]
        gx_ref[...] = r1 * (dxn - xn * jnp.mean(dxn * xn, axis=-1, keepdims=True)) + do_ref[...]

        @pl.when(m == nm - 1)
        def _():
            ex.wait()

    row = lambda w: pl.BlockSpec((tm, w), lambda m: (m, 0))
    return pl.pallas_call(
        body, name="grad_x", grid=(nm,),
        in_specs=[row(D_IN), pl.BlockSpec((N_DEV, D_MODEL, W_BLK), lambda m: (0, 0, 0), pipeline_mode=pl.Buffered(1)),
                  row(D_MODEL),
                  pl.BlockSpec((1, D_MODEL), lambda m: (0, 0)), row(D_MODEL), ANY],
        out_specs=[row(D_MODEL), pl.BlockSpec((8, D_MODEL), lambda m: (0, 0)), ANY],
        out_shape=[SDS((seq, D_MODEL), F32), SDS((8, D_MODEL), F32), SDS(g_chip.shape, F32)],
        scratch_shapes=[pltpu.SemaphoreType.DMA((N_CHIPS - 1,)), pltpu.SemaphoreType.DMA((N_CHIPS - 1,)),
                        pltpu.SemaphoreType.DMA(())],
        compiler_params=_cparams(("arbitrary",)),
    )(dp, w_all, x, pre_w, dout, g_chip)


def _local_step(x, tgt, p, u, conv_w, conv_b, wa, wx, ba, bx, lam, lbl, gnorm_w, w_out, post_w):
    seq = x.shape[0]
    h, y_lru = _lru_forward(p, conv_w, conv_b, wa, wx, ba, bx, lam, seq)
    y_hgrn, o, states = _hgrn_forward(p, lbl, gnorm_w, seq)
    dymix, dout, g_w_out, stats = _out_proj(y_lru, y_hgrn, w_out, x, tgt, post_w, seq)
    dp_lru, g_wa, g_wx, ls = _lru_backward(p, h, dymix, conv_w, conv_b, wa, wx, ba, bx, lam, seq)
    g_small = _pack_small(_shard_rows(g_wa, LRU_BLOCKS), _shard_rows(g_wx, LRU_BLOCKS),
                          _shard_rows(ls[0:4].reshape(4, D_MODEL, 1), 4).reshape(N_DEV, 4, 128),
                          _shard_rows(ls[5].reshape(4, LRU_BW, 1), 4).reshape(N_DEV, 4, 32),
                          _shard_rows(ls[6].reshape(4, LRU_BW, 1), 4).reshape(N_DEV, 4, 32))
    dp, hgrn_small, r_out, r_small = _hgrn_backward(
        p, o, states, dymix, lbl, gnorm_w, dp_lru, g_w_out.reshape(N_DEV, D_MIX // N_DEV, D_MODEL), g_small, seq)
    return dict(u=u, dp=dp, dout=dout, r_out=r_out, r_small=r_small,
                lru_small=ls, hgrn_small=hgrn_small, stats=stats)


class _TwoLevelGather:
    def __init__(self, ins, outs, send_sems, recv_sems, local_sems):
        self.ins, self.outs = ins, outs
        self.send_sems, self.recv_sems, self.local_sems = send_sems, recv_sems, local_sems
        x, y, c = _mesh_pos()
        self.c = c
        self.me, self.sibling = (x, y, c), (x, y, 1 - c)
        self.chips = [(1 - x, y), (x, 1 - y), (1 - x, 1 - y)]
        n = len(ins)
        self.mine = [pltpu.make_async_copy(ins[a], self._slot(a, self.me), local_sems.at[a]) for a in range(n)]
        self.first = []
        for a in range(n):
            self.first.append(self._copy(a, 0, self.me, self.sibling, src=ins[a]))
            self.first += [self._copy(a, 1 + j, self.me, (*chip, c), src=ins[a])
                           for j, chip in enumerate(self.chips)]
        self.passed = [self._copy(a, 4 + j, (*chip, c), self.sibling)
                       for j, chip in enumerate(self.chips) for a in range(n)]

    def _slot(self, a, pos):
        return self.outs[a].at[4 * pos[0] + 2 * pos[1] + pos[2]]

    def _copy(self, a, k, block, to, src=None):
        dst = self._slot(a, block)
        return pltpu.make_async_remote_copy(
            src_ref=dst if src is None else src, dst_ref=dst,
            send_sem=self.send_sems.at[a, k], recv_sem=self.recv_sems.at[a, k],
            device_id=to, device_id_type=MESH)

    def start(self):
        for cp in self.mine + self.first:
            cp.start()

    def forward(self):
        n = len(self.ins)
        for j, chip in enumerate(self.chips):
            for a in range(n):
                self._copy(a, 1 + j, (*chip, self.c), self.me).wait_recv()
                self.passed[j * n + a].start()

    def finish(self):
        for a in range(len(self.ins)):
            self._copy(a, 0, self.sibling, self.me).wait_recv()
            for j, chip in enumerate(self.chips):
                self._copy(a, 4 + j, (*chip, 1 - self.c), self.me).wait_recv()
        for cp in self.first + self.passed:
            cp.wait_send()
        for cp in self.mine:
            cp.wait()


W_IN_DIRECT = (1, 2, 4, 6)
W_IN_PASSED = (2, 4, 6)

def _in_proj_gather(x, pre_w, w_in_blk, w_out_blk, small_blk, me, seq):
    tm = min(1024, seq)
    nm = seq // tm
    last = N_DEV - 1

    def body(me_ref, x_ref, pw_ref, wblk_ref, woblk_ref, smblk_ref,
             p_ref, u_ref, wall_ref, woall_ref, small_ref,
             u_all, w_vmem, own_sem, d_send, d_recv, f_send, f_recv, wb_sems, g_send, g_recv, g_local):
        i = pl.program_id(0)
        m = pl.program_id(1)
        idx = me_ref[0]
        x_, y_, c_ = _mesh_pos()
        aux = _TwoLevelGather([woblk_ref, smblk_ref], [woall_ref, small_ref], g_send, g_recv, g_local)

        def peer(k):
            return (1 - x_ if (k >> 2) & 1 else x_, 1 - y_ if (k >> 1) & 1 else y_, 1 - c_ if k & 1 else c_)

        def direct(k):
            f = W_IN_DIRECT.index(k)
            return (pltpu.make_async_remote_copy(src_ref=wblk_ref, dst_ref=w_vmem.at[idx], send_sem=d_send.at[f],
                                                 recv_sem=d_recv.at[f], device_id=peer(k), device_id_type=MESH),
                    pltpu.make_async_remote_copy(src_ref=w_vmem.at[idx ^ k], dst_ref=w_vmem.at[idx ^ k],
                                                 send_sem=d_send.at[f], recv_sem=d_recv.at[f], device_id=peer(k),
                                                 device_id_type=MESH))

        def passed(k):
            f = W_IN_PASSED.index(k)
            return (pltpu.make_async_remote_copy(src_ref=w_vmem.at[idx ^ k], dst_ref=w_vmem.at[idx ^ k],
                                                 send_sem=f_send.at[f], recv_sem=f_recv.at[f], device_id=peer(1),
                                                 device_id_type=MESH),
                    pltpu.make_async_remote_copy(src_ref=w_vmem.at[idx ^ (k + 1)], dst_ref=w_vmem.at[idx ^ (k + 1)],
                                                 send_sem=f_send.at[f], recv_sem=f_recv.at[f], device_id=peer(1),
                                                 device_id_type=MESH))

        def write_back(k):
            return pltpu.make_async_copy(w_vmem.at[idx ^ k], wall_ref.at[idx ^ k], wb_sems.at[k])

        own = pltpu.make_async_copy(wblk_ref, w_vmem.at[idx], own_sem)

        @pl.when((i == 0) & (m == 0))
        def _():
            own.start()
            for k in (1, 2, 4):
                direct(k)[0].start()
            own.wait()
            write_back(0).start()

        for k in range(1, N_DEV):
            @pl.when((i == k) & (m == 0))
            def _(k=k):
                if k in W_IN_DIRECT:
                    direct(k)[1].wait_recv()
                    if k in W_IN_PASSED:
                        passed(k)[0].start()
                else:
                    passed(k - 1)[1].wait_recv()
                write_back(k).start()
                if k == 2:
                    direct(6)[0].start()
                if k == 4:
                    aux.start()
                if k == N_DEV - 1:
                    aux.forward()

        rows = pl.ds(pl.multiple_of(m * tm, tm), tm)

        @pl.when(i == 0)
        def _():
            xv = x_ref[...]
            r = lax.rsqrt(jnp.mean(xv * xv, axis=-1, keepdims=True) + EPS)
            ub = (xv * r * pw_ref[...]).astype(BF16)
            u_all[rows, :] = ub
            u_ref[...] = ub

        p_ref[...] = _dot(u_all[rows, :], w_vmem[idx ^ i])

        @pl.when((i == last) & (m == nm - 1))
        def _():
            for k in W_IN_DIRECT:
                direct(k)[0].wait_send()
            for k in W_IN_PASSED:
                passed(k)[0].wait_send()
            for k in range(N_DEV):
                write_back(k).wait()
            aux.finish()

    first_pass = lambda i, m: jnp.where(i == 0, m, nm - 1)
    return pl.pallas_call(
        body, name="in_proj_gather",
        grid_spec=pltpu.PrefetchScalarGridSpec(
            num_scalar_prefetch=1, grid=(N_DEV, nm),
            in_specs=[pl.BlockSpec((tm, D_MODEL), lambda i, m, me: (first_pass(i, m), 0)),
                      pl.BlockSpec((1, D_MODEL), lambda i, m, me: (0, 0)), ANY, ANY, ANY],
            out_specs=[pl.BlockSpec((tm, W_BLK), lambda i, m, me: (m, me[0] ^ i)),
                       pl.BlockSpec((tm, D_MODEL), lambda i, m, me: (first_pass(i, m), 0)), ANY, ANY, ANY],
            scratch_shapes=[pltpu.VMEM((seq, D_MODEL), BF16), pltpu.VMEM((N_DEV, D_MODEL, W_BLK), BF16),
                            pltpu.SemaphoreType.DMA(()),
                            pltpu.SemaphoreType.DMA((len(W_IN_DIRECT),)), pltpu.SemaphoreType.DMA((len(W_IN_DIRECT),)),
                            pltpu.SemaphoreType.DMA((len(W_IN_PASSED),)), pltpu.SemaphoreType.DMA((len(W_IN_PASSED),)),
                            pltpu.SemaphoreType.DMA((N_DEV,)),
                            pltpu.SemaphoreType.DMA((2, 7)), pltpu.SemaphoreType.DMA((2, 7)),
                            pltpu.SemaphoreType.DMA((2,))]),
        out_shape=[SDS((seq, D_IN), F32), SDS((seq, D_MODEL), BF16), SDS((N_DEV, D_MODEL, W_BLK), BF16),
                   SDS((N_DEV,) + w_out_blk.shape, w_out_blk.dtype), SDS((N_DEV,) + small_blk.shape, small_blk.dtype)],
        compiler_params=_cparams(("arbitrary", "arbitrary")),
    )(me, x, pre_w, w_in_blk, w_out_blk, small_blk)


def _exchange_grads(blocks, repl):
    nb = len(blocks)
    n = nb + 1

    def body(*refs):
        ins, outs, sems = refs[:n], refs[n:2 * n], refs[2 * n:]
        exs = [_SlotExchange(ins[a], outs[a], *sems[3 * a:3 * a + 3], blocked=a < nb) for a in range(n)]
        for ex in exs:
            ex.start()
        for ex in exs:
            ex.wait()

    arrs = list(blocks) + [repl]
    shapes = [SDS(b.shape, b.dtype) for b in blocks] + [SDS((N_DEV,) + repl.shape, repl.dtype)]
    return pl.pallas_call(
        body, name="exchange_small", out_shape=shapes,
        in_specs=[ANY] * n, out_specs=[ANY] * n,
        scratch_shapes=EXCHANGE_SEMS * n,
    )(*arrs)


def _pack_rows(picks, name):
    arrs = [p[0] for p in picks]

    def body(*refs):
        out = refs[-1]
        out[...] = jnp.zeros_like(out)
        at = 0
        for ref, (_, row, rows, scale) in zip(refs[:-1], picks):
            out[at:at + rows, :] = ref[row:row + rows, :] * scale
            at += rows

    return pl.pallas_call(body, name=name, out_shape=SDS((8, D_MODEL), F32))(*arrs)


def _adamw(g, w, m, v):
    m2 = ADAM_B1 * m + (1.0 - ADAM_B1) * g
    v2 = ADAM_B2 * v + (1.0 - ADAM_B2) * (g * g)
    m_hat = m2 / (1.0 - ADAM_B1 ** ADAM_STEP)
    v_hat = v2 / (1.0 - ADAM_B2 ** ADAM_STEP)
    delta = -ADAM_LR * (m_hat / (jnp.sqrt(v_hat) + ADAM_EPS) + ADAM_WD * w)
    return delta, m2, v2


def _sum_slots(r_ref):
    g = r_ref[0]
    for s in range(1, r_ref.shape[0]):
        g = g + r_ref[s]
    return g


def _sum_adamw(recv, w, m, v, tr, name):
    rows, cols = w.shape

    def body(r_ref, w_ref, m_ref, v_ref, g_ref, d_ref, m2_ref, v2_ref):
        g = _sum_slots(r_ref)
        g_ref[...] = g
        d_ref[...], m2_ref[...], v2_ref[...] = _adamw(g, w_ref[...], m_ref[...], v_ref[...])

    blk = pl.BlockSpec((tr, cols), lambda i: (i, 0))
    return pl.pallas_call(
        body, name=name, grid=(rows // tr,),
        in_specs=[pl.BlockSpec((recv.shape[0], tr, cols), lambda i: (0, i, 0)), blk, blk, blk],
        out_specs=[blk] * 4, out_shape=[SDS((rows, cols), F32)] * 4,
        compiler_params=_cparams(("parallel",)),
    )(recv, w, m, v)


def _sum_adamw_pieces(recv, rows, ws, ms, vs, name, loss_row=None):
    n = len(ws)

    def body(r_ref, *refs):
        w_refs, m_refs, v_refs, outs = refs[:n], refs[n:2 * n], refs[2 * n:3 * n], refs[3 * n:]
        g = _sum_slots(r_ref)
        for i, (row, nrows) in enumerate(rows):
            gi = g[row:row + nrows, 0:ws[i].shape[1]]
            outs[i][...] = gi
            outs[n + i][...], outs[2 * n + i][...], outs[3 * n + i][...] = _adamw(
                gi, w_refs[i][...], m_refs[i][...], v_refs[i][...])
        if loss_row is not None:
            total = jnp.sum(g[loss_row:loss_row + 1, :], axis=-1, keepdims=True)
            outs[4 * n][...] = jnp.broadcast_to(total, outs[4 * n].shape)

    shapes = [SDS(w.shape, F32) for w in ws] * 4 + ([SDS((8, 128), F32)] if loss_row is not None else [])
    out = pl.pallas_call(body, name=name, out_shape=shapes)(recv, *ws, *ms, *vs)
    return [out[k * n:(k + 1) * n] for k in range(4)] + list(out[4 * n:])


def _shard_rows(t, lead):
    r = t.shape[1] // N_DEV
    t = t.reshape((lead, N_DEV, r) + t.shape[2:])
    return jnp.moveaxis(t, 1, 0)


def _pad_tile(t):
    return jnp.pad(t, ((0, 0), (0, 8 - t.shape[1]), (0, SM_LANES - t.shape[2])))


def _pack_small(wa, wx, cw, b_a, b_x):
    n = wa.shape[0]
    return jnp.concatenate([wa.reshape(n, 128, SM_LANES), wx.reshape(n, 128, SM_LANES),
                            _pad_tile(cw), _pad_tile(b_a), _pad_tile(b_x)], axis=1)


def _unpack_small(t):
    n = t.shape[0]
    return (t[:, SM_WA:SM_WA + 128].reshape(n, 4, 32, 256), t[:, SM_WX:SM_WX + 128].reshape(n, 4, 32, 256),
            t[:, SM_CW:SM_CW + 4, 0:128], t[:, SM_BA:SM_BA + 4, 0:32], t[:, SM_BX:SM_BX + 4, 0:32])


def kernel(x, pre_norm_w, w_in, conv_w, conv_b, lru_w_a, lru_b_a, lru_w_x, lru_b_x, lru_lambda, hgrn_lb_logits, hgrn_gnorm_w, w_out, post_norm_w, loss_target, m_pre_norm_w, m_w_in, m_conv_w, m_conv_b, m_lru_w_a, m_lru_b_a, m_lru_w_x, m_lru_b_x, m_lru_lambda, m_hgrn_lb_logits, m_hgrn_gnorm_w, m_w_out, m_post_norm_w, v_pre_norm_w, v_w_in, v_conv_w, v_conv_b, v_lru_w_a, v_lru_b_a, v_lru_w_x, v_lru_b_x, v_lru_lambda, v_hgrn_lb_logits, v_hgrn_gnorm_w, v_w_out, v_post_norm_w):
    seq = x.shape[1]
    x2 = x.reshape(seq, D_MODEL)
    tgt = loss_target.reshape(seq, D_MODEL)

    small_w = _pack_small(lru_w_a, lru_w_x, conv_w, lru_b_a, lru_b_x)[0]
    me = (4 * lax.axis_index("x") + 2 * lax.axis_index("y") + lax.axis_index("c")).astype(jnp.int32).reshape(1)
    p, u, w_in_all, w_out_all, small_all = _in_proj_gather(
        x2, pre_norm_w, w_in[0].astype(BF16), w_out[0].astype(BF16), small_w, me, seq)
    wa_s, wx_s, cw_s, ba_s, bx_s = _unpack_small(small_all)
    wa = jnp.moveaxis(wa_s, 0, 1).reshape(LRU_BLOCKS, LRU_BW, LRU_BW).astype(BF16)
    wx = jnp.moveaxis(wx_s, 0, 1).reshape(LRU_BLOCKS, LRU_BW, LRU_BW).astype(BF16)
    cw = jnp.moveaxis(cw_s, 0, 1).reshape(4, D_MODEL)
    ba = jnp.moveaxis(ba_s, 0, 1).reshape(1, D_MODEL)
    bx = jnp.moveaxis(bx_s, 0, 1).reshape(1, D_MODEL)

    loc = _local_step(x2, tgt, p, u, cw, conv_b, wa, wx, ba, bx, lru_lambda,
                      hgrn_lb_logits, hgrn_gnorm_w, w_out_all.reshape(D_MIX, D_MODEL), post_norm_w)

    ls, r_out, r_small = loc["lru_small"], loc["r_out"], loc["r_small"]
    core = lax.axis_index("c").astype(jnp.int32).reshape(1)
    g_sib = _grad_w_in_sibling(loc["u"], loc["dp"], core, seq)
    g_chip = _grad_w_in_own(loc["u"], loc["dp"], core, g_sib, seq)
    grad_x, pre_small, r_in = _grad_x(loc["dp"], w_in_all, x2, pre_norm_w, loc["dout"], g_chip, seq)
    g_repl = _pack_rows([(pre_small, 0, 1, 1.0), (ls, 4, 1, 1.0), (ls, 7, 1, 1.0),
                         (loc["hgrn_small"], 0, 1, 1.0), (loc["hgrn_small"], 0, 1, -1.0),
                         (loc["hgrn_small"], 1, 1, 1.0), (loc["stats"], 0, 2, 1.0)], "pack_grads")
    (r_repl,) = _exchange_grads([], g_repl)

    repl_rows = [(RP_PRE, 1), (RP_CB, 1), (RP_LAM, 1), (RP_LB0, 2), (RP_GN, 1), (RP_POST, 1)]
    o_repl = _sum_adamw_pieces(
        r_repl, repl_rows,
        [pre_norm_w, conv_b, lru_lambda, hgrn_lb_logits, hgrn_gnorm_w, post_norm_w],
        [m_pre_norm_w, m_conv_b, m_lru_lambda, m_hgrn_lb_logits, m_hgrn_gnorm_w, m_post_norm_w],
        [v_pre_norm_w, v_conv_b, v_lru_lambda, v_hgrn_lb_logits, v_hgrn_gnorm_w, v_post_norm_w],
        "adamw_repl", loss_row=RP_LOSS)
    loss = o_repl[4][0, 0]

    small_rows = [(SM_WA, 128), (SM_WX, 128), (SM_CW, 4), (SM_BA, 4), (SM_BX, 4)]
    as_rows = lambda wa_, wx_, cw_, ba_, bx_: [wa_.reshape(128, 256), wx_.reshape(128, 256), cw_.reshape(4, 128),
                                               ba_.reshape(4, 32), bx_.reshape(4, 32)]
    o_small = _sum_adamw_pieces(
        r_small, small_rows, as_rows(lru_w_a, lru_w_x, conv_w, lru_b_a, lru_b_x),
        as_rows(m_lru_w_a, m_lru_w_x, m_conv_w, m_lru_b_a, m_lru_b_x),
        as_rows(v_lru_w_a, v_lru_w_x, v_conv_w, v_lru_b_a, v_lru_b_x), "adamw_small")

    o_in = _sum_adamw(r_in, w_in[0], m_w_in[0], v_w_in[0], 128, "adamw_w_in")
    o_out = _sum_adamw(r_out, w_out[0], m_w_out[0], v_w_out[0], 64, "adamw_w_out")

    outs = [loss, grad_x.reshape(x.shape)]
    for kind in range(4):
        pre, cb, lam, lb, gn, post = o_repl[kind]
        swa, swx, scw, sba, sbx = o_small[kind]
        outs += [pre, o_in[kind][None], scw.reshape(conv_w.shape), cb, swa.reshape(lru_w_a.shape),
                 sba.reshape(lru_b_a.shape), swx.reshape(lru_w_x.shape), sbx.reshape(lru_b_x.shape),
                 lam, lb, gn, o_out[kind][None], post]
    return tuple(outs)
```

```python
import jax
import jax.numpy as jnp
from jax import lax
from jax.experimental import pallas as pl
from jax.experimental.pallas import tpu as pltpu

F32 = jnp.float32
BF16 = jnp.bfloat16
SDS = jax.ShapeDtypeStruct

D_MODEL = 1024
D_IN = 6144
N_DEV = 8
N_CHIPS = 4
W_BLK = D_IN // N_DEV
D_MIX = 2048
LRU_BLOCKS = 4
LRU_BW = 256
LRU_C = 8.0
LANES = 128
LANE_GROUPS = D_MODEL // LANES
N_HEADS = 8
HEAD_D = 128
CHUNK = 128
SUB = 32
N_SUB = CHUNK // SUB
HGRN_FWD_STEP_CHUNKS = 4
HGRN_STEP_CHUNKS = 2
EXP_CLAMP = 80.0
EPS = 1e-6

ADAM_LR = 0.001
ADAM_B1 = 0.9
ADAM_B2 = 0.999
ADAM_EPS = 1e-08
ADAM_WD = 0.01
ADAM_STEP = 10

VMEM_LIMIT = 56 * 1024 * 1024

NN = (((1,), (0,)), ((), ()))
NT = (((1,), (1,)), ((), ()))
TN = (((0,), (0,)), ((), ()))

SM_LANES = 256
SM_WA = 0
SM_WX = 128
SM_CW = 256
SM_BA = 264
SM_BX = 272
SM_ROWS = 280

RP_PRE, RP_CB, RP_LAM, RP_LB0, RP_LB1, RP_GN, RP_POST, RP_LOSS = range(8)


def _dot(a, b, dims=NN):
    return lax.dot_general(a, b, dims, preferred_element_type=F32)


def _sigmoid(x):
    return 0.5 * jnp.tanh(0.5 * x) + 0.5


def _sigmoid_pos(x):
    return 1.0 / (1.0 + jnp.exp(-x))


def _cparams(sem, vmem=VMEM_LIMIT):
    return pltpu.CompilerParams(dimension_semantics=sem, vmem_limit_bytes=vmem)


def _iota(shape, axis):
    return lax.broadcasted_iota(jnp.int32, shape, axis)


def _softplus_neg(lam):
    z = -lam
    e = jnp.exp(-jnp.abs(z))
    u = 1.0 + e
    log1p_e = jnp.where(u == 1.0, e, jnp.log(u) * (e / (u - 1.0)))
    sp = jnp.maximum(z, 0.0) + log1p_e
    dsp = -jnp.where(z >= 0.0, 1.0 / u, e / u)
    return sp, dsp


def _neg_expm1(x):
    poly = x * (1.0 + x * (1.0 / 2 + x * (1.0 / 6 + x * (1.0 / 24 + x * (1.0 / 120)))))
    return jnp.where(x > -1.0 / 16, -poly, 1.0 - jnp.exp(x))


def _conv_taps(lx, prev8, cw_ref, cb_ref, tile):
    xc = cb_ref[...] + cw_ref[3:4, :] * lx
    for j in (1, 2, 3):
        xc = xc + cw_ref[3 - j:4 - j, :] * pltpu.roll(lx, j, 0)
    row8 = _iota((8, D_MODEL), 0)
    last8 = lx[tile - 8:tile, :]
    fix = jnp.zeros((8, D_MODEL), F32)
    for j in (1, 2, 3):
        wrong = pltpu.roll(last8, j, 0)
        right = pltpu.roll(prev8, j, 0)
        fix = fix + cw_ref[3 - j:4 - j, :] * jnp.where(row8 < j, right - wrong, 0.0)
    return xc, fix


def _lru_gates(xcs, wa, wx, ba, bx, sp):
    xb = xcs.astype(BF16)
    r = _sigmoid_pos(_dot(xb, wa) + ba)
    i = _sigmoid(_dot(xb, wx) + bx)
    la = (-LRU_C * sp) * r
    a = jnp.exp(la)
    one_minus_a2 = _neg_expm1(2.0 * la)
    return r, i, a, one_minus_a2


def _lru_forward(p, conv_w, conv_b, wa, wx, ba, bx, lam, seq):
    tile = min(1024, seq // 2)
    seg = tile // 8
    pitch = seg + 4

    def body(lx_ref, gt_ref, cw_ref, cb_ref, wa_ref, wx_ref, ba_ref, bx_ref, lam_ref,
             h_ref, y_ref, ext, hcar, xc_scr, a_scr, u_scr):
        @pl.when(pl.program_id(0) == 0)
        def _():
            ext[0:8, :] = jnp.zeros((8, D_MODEL), F32)
            hcar[...] = jnp.zeros_like(hcar)

        lx = lx_ref[...]
        ext[8:8 + tile, :] = lx
        xc = cb_ref[...] + cw_ref[3:4, :] * lx
        for j in (1, 2, 3):
            xc = xc + cw_ref[3 - j:4 - j, :] * ext[8 - j:8 - j + tile, :]
        xc_scr[...] = xc
        ext[0:8, :] = lx_ref[tile - 8:tile, :]
        sp, _ = _softplus_neg(lam_ref[...])
        for n in range(LRU_BLOCKS):
            sl = slice(n * LRU_BW, (n + 1) * LRU_BW)
            xcs = xc_scr[:, sl]
            _, i, a, ne = _lru_gates(xcs, wa_ref[n], wx_ref[n], ba_ref[:, sl], bx_ref[:, sl], sp[:, sl])
            u = jnp.sqrt(ne) * (i * xcs)
            for half in range(2):
                lanes = slice(half * LANES, (half + 1) * LANES)
                for s in range(8):
                    a_scr[2 * n + half, s * pitch:s * pitch + seg, :] = a[s * seg:(s + 1) * seg, lanes]
                    u_scr[2 * n + half, s * pitch:s * pitch + seg, :] = u[s * seg:(s + 1) * seg, lanes]

        def step(j, carry):
            rows = pl.ds(j, 8, stride=pitch)
            out = []
            for gi in range(LANE_GROUPS):
                hl, al = carry[gi]
                a = a_scr.at[gi][rows, :]
                hl = a * hl + u_scr.at[gi][rows, :]
                al = a * al
                u_scr.at[gi][rows, :] = hl
                a_scr.at[gi][rows, :] = al
                out.append((hl, al))
            return tuple(out)

        init = tuple((jnp.zeros((8, LANES), F32), jnp.ones((8, LANES), F32)) for _ in range(LANE_GROUPS))
        fin = lax.fori_loop(0, seg, step, init)
        hl = jnp.concatenate([f[0] for f in fin], axis=1)
        al = jnp.concatenate([f[1] for f in fin], axis=1)
        row8 = _iota((8, D_MODEL), 0)
        for k in (1, 2, 4):
            m = row8 >= k
            hl = jnp.where(m, hl + al * pltpu.roll(hl, k, 0), hl)
            al = jnp.where(m, al * pltpu.roll(al, k, 0), al)
        ends = hl + al * hcar[...]
        before = jnp.where(row8 == 0, hcar[...], pltpu.roll(ends, 1, 0))
        hcar[...] = jnp.broadcast_to(ends[7:8, :], (8, D_MODEL))
        for s in range(8):
            rows = slice(s * seg, (s + 1) * seg)
            for gi in range(LANE_GROUPS):
                lanes = slice(gi * LANES, (gi + 1) * LANES)
                g = gt_ref[rows, lanes]
                h = (u_scr[gi, s * pitch:s * pitch + seg, :]
                     + a_scr[gi, s * pitch:s * pitch + seg, :] * before[s:s + 1, lanes])
                h_ref[rows, lanes] = h
                y_ref[rows, lanes] = (h * (g * _sigmoid(g))).astype(BF16)

    full = lambda shape: pl.BlockSpec(shape, lambda t: (0,) * len(shape))
    return pl.pallas_call(
        body, name="lru_fwd", grid=(seq // tile,),
        in_specs=[pl.BlockSpec((tile, D_MODEL), lambda t: (t, 0)),
                  pl.BlockSpec((tile, D_MODEL), lambda t: (t, 1)),
                  full((4, D_MODEL)), full((1, D_MODEL)),
                  full((LRU_BLOCKS, LRU_BW, LRU_BW)), full((LRU_BLOCKS, LRU_BW, LRU_BW)),
                  full((1, D_MODEL)), full((1, D_MODEL)), full((1, D_MODEL))],
        out_specs=[pl.BlockSpec((tile, D_MODEL), lambda t: (t, 0)),
                   pl.BlockSpec((tile, D_MODEL), lambda t: (t, 0))],
        out_shape=[SDS((seq, D_MODEL), F32), SDS((seq, D_MODEL), BF16)],
        scratch_shapes=[pltpu.VMEM((tile + 8, D_MODEL), F32), pltpu.VMEM((8, D_MODEL), F32),
                        pltpu.VMEM((tile, D_MODEL), F32), pltpu.VMEM((LANE_GROUPS, 8 * pitch, LANES), F32),
                        pltpu.VMEM((LANE_GROUPS, 8 * pitch, LANES), F32)],
        compiler_params=_cparams(("arbitrary",)),
    )(p, p, conv_w, conv_b, wa, wx, ba, bx, lam)


def _lru_backward(p, h, dymix, conv_w, conv_b, wa, wx, ba, bx, lam, seq):
    tile = min(512, seq // 2)
    nt = seq // tile
    seg = tile // 8
    pitch = seg + 4
    t8 = tile // 8

    def body(lx_ref, lxh_ref, gt_ref, h_ref, hh_ref, dy_ref, cw_ref, cb_ref, wa_ref, wx_ref, ba_ref,
             bx_ref, lam_ref, dp_ref, gwa_ref, gwx_ref, gsm_ref,
             lamcar, anext, dxc8, xc_scr, r_scr, i_scr, a_scr, m_scr, rm_scr, l_scr, dxc_scr, c3, l3):
        step = pl.program_id(0)
        first_tile = step == nt - 1

        @pl.when(step == 0)
        def _():
            lamcar[...] = jnp.zeros_like(lamcar)
            anext[...] = jnp.zeros_like(anext)
            dxc8[...] = jnp.zeros_like(dxc8)
            gwa_ref[...] = jnp.zeros_like(gwa_ref)
            gwx_ref[...] = jnp.zeros_like(gwx_ref)
            gsm_ref[...] = jnp.zeros_like(gsm_ref)

        keep = jnp.where(first_tile, 0.0, 1.0)
        lx = lx_ref[...]
        prev8 = lxh_ref[...] * keep
        xc, fix = _conv_taps(lx, prev8, cw_ref, cb_ref, tile)
        xc_scr[...] = xc
        xc_scr[0:8, :] = xc_scr[0:8, :] + fix
        sp, dsp = _softplus_neg(lam_ref[...])
        for n in range(LRU_BLOCKS):
            sl = slice(n * LRU_BW, (n + 1) * LRU_BW)
            r, i, a, ne = _lru_gates(xc_scr[:, sl], wa_ref[n], wx_ref[n], ba_ref[:, sl], bx_ref[:, sl],
                                     sp[:, sl])
            r_scr[:, sl] = r
            i_scr[:, sl] = i
            a_scr[:, sl] = a
            m_scr[:, sl] = jnp.sqrt(ne)
            rm_scr[:, sl] = lax.rsqrt(ne)

        g = gt_ref[...]
        sg = _sigmoid(g)
        dy = dy_ref[...]
        hv = h_ref[...]
        dp_ref[:, D_MODEL:2 * D_MODEL] = (dy * hv * (sg * (1.0 + g * (1.0 - sg)))).astype(BF16)

        rowt = _iota((tile, D_MODEL), 0)
        av = a_scr[...]
        dh = dy * (g * sg)
        cnext = jnp.where(rowt == tile - 1, anext[...][0:1, :], pltpu.roll(av, tile - 1, 0))
        anext[...] = jnp.broadcast_to(av[0:1, :], (8, D_MODEL))
        for gi in range(LANE_GROUPS):
            lanes = slice(gi * LANES, (gi + 1) * LANES)
            for s in range(8):
                l3[gi, s * pitch:s * pitch + seg, :] = dh[s * seg:(s + 1) * seg, lanes]
                c3[gi, s * pitch:s * pitch + seg, :] = cnext[s * seg:(s + 1) * seg, lanes]

        def step(jj, carry):
            rows = pl.ds(seg - 1 - jj, 8, stride=pitch)
            out = []
            for gi in range(LANE_GROUPS):
                ll, cl = carry[gi]
                c = c3.at[gi][rows, :]
                ll = c * ll + l3.at[gi][rows, :]
                cl = c * cl
                l3.at[gi][rows, :] = ll
                c3.at[gi][rows, :] = cl
                out.append((ll, cl))
            return tuple(out)

        init = tuple((jnp.zeros((8, LANES), F32), jnp.ones((8, LANES), F32)) for _ in range(LANE_GROUPS))
        fin = lax.fori_loop(0, seg, step, init)
        ll = jnp.concatenate([f[0] for f in fin], axis=1)
        cl = jnp.concatenate([f[1] for f in fin], axis=1)
        row8 = _iota((8, D_MODEL), 0)
        for k in (1, 2, 4):
            m = row8 < 8 - k
            ll = jnp.where(m, ll + cl * pltpu.roll(ll, 8 - k, 0), ll)
            cl = jnp.where(m, cl * pltpu.roll(cl, 8 - k, 0), cl)
        firsts = ll + cl * lamcar[...]
        after = jnp.where(row8 == 7, lamcar[...], pltpu.roll(firsts, 7, 0))
        lamcar[...] = jnp.broadcast_to(firsts[0:1, :], (8, D_MODEL))
        for s in range(8):
            for gi in range(LANE_GROUPS):
                lanes = slice(gi * LANES, (gi + 1) * LANES)
                l_scr[s * seg:(s + 1) * seg, lanes] = (l3[gi, s * pitch:s * pitch + seg, :]
                                                        + c3[gi, s * pitch:s * pitch + seg, :] * after[s:s + 1, lanes])

        hprev = jnp.where(rowt == 0, hh_ref[...][7:8, :] * keep, pltpu.roll(hv, 1, 0))
        for n in range(LRU_BLOCKS):
            sl = slice(n * LRU_BW, (n + 1) * LRU_BW)
            lamv = l_scr[:, sl]
            xcs = xc_scr[:, sl]
            r = r_scr[:, sl]
            i = i_scr[:, sl]
            a = a_scr[:, sl]
            mult = m_scr[:, sl]
            d_la = lamv * hprev[:, sl] * a - (lamv * i * xcs) * (a * a * rm_scr[:, sl])
            d_pr = d_la * (-LRU_C * sp[:, sl]) * r * (1.0 - r)
            d_pi = (lamv * mult * xcs) * i * (1.0 - i)
            gsm_ref[7:8, sl] += jnp.sum(d_la * r, axis=0, keepdims=True) * (-LRU_C) * dsp[:, sl]
            gsm_ref[5:6, sl] += jnp.sum(d_pr, axis=0, keepdims=True)
            gsm_ref[6:7, sl] += jnp.sum(d_pi, axis=0, keepdims=True)
            xb = xcs.astype(BF16)
            prb = d_pr.astype(BF16)
            pib = d_pi.astype(BF16)
            gwa_ref[n] += _dot(xb, prb, TN)
            gwx_ref[n] += _dot(xb, pib, TN)
            dxc_scr[:, sl] = lamv * mult * i + _dot(prb, wa_ref[n], NT) + _dot(pib, wx_ref[n], NT)

        dxc = dxc_scr[...]
        gsm_ref[4:5, :] += jnp.sum(dxc, axis=0, keepdims=True)
        last8 = lx[tile - 8:tile, :]
        first8 = dxc[0:8, :]
        dlx = cw_ref[3:4, :] * dxc
        gsm_ref[3:4, :] += jnp.sum(dxc * lx, axis=0, keepdims=True)
        fix = jnp.zeros((8, D_MODEL), F32)
        for j in (1, 2, 3):
            w = cw_ref[3 - j:4 - j, :]
            dlx = dlx + w * pltpu.roll(dxc, tile - j, 0)
            fix = fix + w * jnp.where(row8 + j >= 8,
                                      pltpu.roll(dxc8[...], 8 - j, 0) - pltpu.roll(first8, 8 - j, 0), 0.0)
            halo = jnp.where(row8 < j, pltpu.roll(prev8, j, 0) - pltpu.roll(last8, j, 0), 0.0)
            gsm_ref[3 - j:4 - j, :] += (jnp.sum(dxc * pltpu.roll(lx, j, 0), axis=0, keepdims=True)
                                        + jnp.sum(first8 * halo, axis=0, keepdims=True))
        dxc8[...] = first8
        dp_ref[:, 0:D_MODEL] = dlx.astype(BF16)
        top = tile - 8
        dp_ref[top:tile, 0:D_MODEL] = (dlx[top:tile, :] + fix).astype(BF16)

    rev = lambda t: (nt - 1 - t, 0)
    halo_idx = lambda t: (jnp.maximum((nt - 1 - t) * t8 - 1, 0), 0)
    full = lambda shape: pl.BlockSpec(shape, lambda t: (0,) * len(shape))
    big = lambda: pltpu.VMEM((tile, D_MODEL), F32)
    return pl.pallas_call(
        body, name="lru_bwd", grid=(nt,),
        in_specs=[pl.BlockSpec((tile, D_MODEL), rev),
                  pl.BlockSpec((8, D_MODEL), halo_idx),
                  pl.BlockSpec((tile, D_MODEL), lambda t: (nt - 1 - t, 1)),
                  pl.BlockSpec((tile, D_MODEL), rev),
                  pl.BlockSpec((8, D_MODEL), halo_idx),
                  pl.BlockSpec((tile, D_MODEL), rev),
                  full((4, D_MODEL)), full((1, D_MODEL)),
                  full((LRU_BLOCKS, LRU_BW, LRU_BW)), full((LRU_BLOCKS, LRU_BW, LRU_BW)),
                  full((1, D_MODEL)), full((1, D_MODEL)), full((1, D_MODEL))],
        out_specs=[pl.BlockSpec((tile, 2 * D_MODEL), rev),
                   full((LRU_BLOCKS, LRU_BW, LRU_BW)), full((LRU_BLOCKS, LRU_BW, LRU_BW)),
                   full((8, D_MODEL))],
        out_shape=[SDS((seq, D_IN), BF16), SDS((LRU_BLOCKS, LRU_BW, LRU_BW), F32),
                   SDS((LRU_BLOCKS, LRU_BW, LRU_BW), F32), SDS((8, D_MODEL), F32)],
        scratch_shapes=[pltpu.VMEM((8, D_MODEL), F32), pltpu.VMEM((8, D_MODEL), F32),
                        pltpu.VMEM((8, D_MODEL), F32)] + [big() for _ in range(8)]
                       + [pltpu.VMEM((LANE_GROUPS, 8 * pitch, LANES), F32)] * 2,
        compiler_params=_cparams(("arbitrary",)),
    )(p, p, p, h, h, dymix, conv_w, conv_b, wa, wx, ba, bx, lam)


def _tri_matmul(tri, g):
    hi = g.astype(BF16)
    lo = (g - hi.astype(F32)).astype(BF16)
    return _dot(tri, lo) + _dot(tri, hi)


def _hgrn_gate_terms(q, fr, lbl):
    lb = _sigmoid_pos(lbl[0:1, :] - lbl[1:2, :])
    half = 0.5 * (1.0 - lb)
    tf = jnp.tanh(0.5 * fr)
    f = (lb + half) + half * tf
    hq = 0.5 * q
    tq = jnp.tanh(hq)
    return lb, tf, f, tq, hq * tq + hq


def _hgrn_decay(bh):
    zero = jnp.zeros((1, bh.shape[1]), F32)
    rho = [zero] + [bh[s * SUB - 1:s * SUB, :] for s in range(1, N_SUB + 1)]
    start = _sub_rows(rho[0:N_SUB])
    end = _sub_rows(rho[1:N_SUB + 1])
    mid = 0.5 * (start + end)
    blast = rho[N_SUB]
    e_on = jnp.exp(bh - start)
    e_off = jnp.exp(end - bh)
    scales = [_sub_rows([jnp.exp(rho[i] - rho[j + 1]) if i > j else zero for i in range(N_SUB)])
              for j in range(N_SUB - 1)]
    return dict(eq0=jnp.exp(jnp.minimum(bh - mid, EXP_CLAMP)), ek0=jnp.exp(jnp.minimum(mid - bh, EXP_CLAMP)),
                e_on=e_on, e_off=e_off, scales=scales,
                eb=e_on * _sub_rows([jnp.exp(r) for r in rho[0:N_SUB]]),
                ekst=e_off * _sub_rows([jnp.exp(blast - r) for r in rho[1:N_SUB + 1]]),
                ebl=jnp.exp(blast))


def _sub_rows(vecs):
    return jnp.concatenate([jnp.broadcast_to(v, (SUB, v.shape[1])) for v in vecs], axis=0)


def _hgrn_operands(qs, k, dec, qt_scr, kt_scr):
    sub = jnp.right_shift(_iota(qs.shape, 0), 5)
    qon = qs * dec["e_on"]
    koff = k * dec["e_off"]
    qt_scr[0] = (qs * dec["eq0"]).astype(BF16)
    kt_scr[0] = (k * dec["ek0"]).astype(BF16)
    for j in range(N_SUB - 1):
        qt_scr[j + 1] = (qon * dec["scales"][j]).astype(BF16)
        kt_scr[j + 1] = jnp.where(sub == j, koff, 0.0).astype(BF16)
    return koff


def _hgrn_head_scores(qt_scr, kt_scr, sl, diag):
    a = jnp.where(diag, _dot(qt_scr[0, :, sl], kt_scr[0, :, sl], NT), 0.0)
    for j in range(1, N_SUB):
        a = a + _dot(qt_scr[j, :, sl], kt_scr[j, :, sl], NT)
    return a


def _hgrn_forward(p, lbl, gw, seq):
    nc = seq // CHUNK
    assert SUB == 32

    def body(q_ref, f_ref, v_ref, hg_ref, lbl_ref, gw_ref, y_ref, o_ref, st_ref,
             s_scr, qt_scr, kt_scr, qin_scr, kst_scr, vb_scr, a_scr):
        @pl.when(pl.program_id(0) == 0)
        def _():
            s_scr[...] = jnp.zeros_like(s_scr)

        r = _iota((CHUNK, CHUNK), 0)
        c = _iota((CHUNK, CHUNK), 1)
        tri = jnp.where(c <= r, 1.0, 0.0).astype(BF16)
        diag = (jnp.right_shift(r, 5) == jnp.right_shift(c, 5)) & (c <= r)
        heads = [slice(h * HEAD_D, (h + 1) * HEAD_D) for h in range(N_HEADS)]
        gates = []
        for cc in range(HGRN_FWD_STEP_CHUNKS):
            rows = slice(cc * CHUNK, (cc + 1) * CHUNK)
            _, _, f, _, qs = _hgrn_gate_terms(q_ref[rows, :], f_ref[rows, :], lbl_ref[...])
            gates.append((qs, 1.0 - f, _tri_matmul(tri, jnp.log(f))))
        for cc in range(HGRN_FWD_STEP_CHUNKS):
            rows = slice(cc * CHUNK, (cc + 1) * CHUNK)
            qs, k, bh = gates[cc]
            dec = _hgrn_decay(bh)
            _hgrn_operands(qs, k, dec, qt_scr, kt_scr)
            qin_scr[...] = (qs * dec["eb"]).astype(BF16)
            kst_scr[...] = (k * dec["ekst"]).astype(BF16)
            vb_scr[...] = v_ref[rows, :].astype(BF16)
            ebl = dec["ebl"]
            hg = hg_ref[rows, :]
            gate = gw_ref[...] * (hg * _sigmoid(hg))
            stb = []
            for h, sl in enumerate(heads):
                st = s_scr[h]
                st_ref[cc, h] = st
                stb.append(st.astype(BF16))
                s_scr[h] = st * ebl[:, sl] + _dot(vb_scr[:, sl], kst_scr[:, sl], TN)
            for h, sl in enumerate(heads):
                a_scr[h] = _hgrn_head_scores(qt_scr, kt_scr, sl, diag).astype(BF16)
            for h, sl in enumerate(heads):
                o = _dot(a_scr[h], vb_scr[:, sl]) + _dot(qin_scr[:, sl], stb[h], NT)
                o_ref[rows, sl] = o
                rs = lax.rsqrt(jnp.mean(o * o, axis=-1, keepdims=True) + EPS)
                y_ref[rows, sl] = ((o * rs) * gate[:, sl]).astype(BF16)

    col = lambda j: pl.BlockSpec((HGRN_FWD_STEP_CHUNKS * CHUNK, D_MODEL), lambda c: (c, j))
    par = lambda rows: pl.BlockSpec((rows, D_MODEL), lambda c: (0, 0))
    return pl.pallas_call(
        body, name="hgrn_fwd", grid=(nc // HGRN_FWD_STEP_CHUNKS,),
        in_specs=[col(2), col(3), col(4), col(5), par(2), par(1)],
        out_specs=[col(0), col(0),
                   pl.BlockSpec((HGRN_FWD_STEP_CHUNKS, N_HEADS, HEAD_D, HEAD_D), lambda c: (c, 0, 0, 0))],
        out_shape=[SDS((seq, D_MODEL), BF16), SDS((seq, D_MODEL), F32),
                   SDS((nc, N_HEADS, HEAD_D, HEAD_D), F32)],
        scratch_shapes=[pltpu.VMEM((N_HEADS, HEAD_D, HEAD_D), F32),
                        pltpu.VMEM((N_SUB, CHUNK, D_MODEL), BF16), pltpu.VMEM((N_SUB, CHUNK, D_MODEL), BF16)]
                       + [pltpu.VMEM((CHUNK, D_MODEL), BF16)] * 3 + [pltpu.VMEM((N_HEADS, CHUNK, CHUNK), BF16)],
        compiler_params=_cparams(("arbitrary",)),
    )(p, p, p, p, lbl, gw)


def _hgrn_backward(p, o, states, dymix, lbl, gw, dp_full, g_w_out, g_small, seq):
    step_rows = HGRN_STEP_CHUNKS * CHUNK
    ns = seq // step_rows

    def body(q_ref, f_ref, v_ref, hg_ref, o_ref, st_ref, dy_ref, lbl_ref, gw_ref, dpin_ref, go_ref, gs_ref,
             dpo_ref, gsm_ref, ro_ref, rs_ref, ds_scr, dp_buf, dp_sems, *rest):
        del dpin_ref
        scratch, sems = rest[:14], rest[14:]
        step = pl.program_id(0)
        slot = step % 2
        exs = [_SlotExchange(go_ref, ro_ref, *sems[0:3], blocked=True),
               _SlotExchange(gs_ref, rs_ref, *sems[3:6], blocked=True)]

        @pl.when(step == 0)
        def _():
            for ex in exs:
                ex.start()

        def out_copy(s, blk):
            rows = pl.ds(pl.multiple_of(blk * step_rows, step_rows), step_rows)
            return pltpu.make_async_copy(dp_buf.at[s], dpo_ref.at[rows, pl.ds(2 * D_MODEL, 4 * D_MODEL)],
                                         dp_sems.at[s])

        @pl.when(step == 0)
        def _():
            ds_scr[...] = jnp.zeros_like(ds_scr)
            gsm_ref[...] = jnp.zeros_like(gsm_ref)

        @pl.when(step >= 2)
        def _():
            out_copy(slot, ns + 1 - step).wait()

        r = _iota((CHUNK, CHUNK), 0)
        tri = jnp.where(_iota((CHUNK, CHUNK), 1) <= r, 1.0, 0.0).astype(BF16)
        gates = {}
        for cc in reversed(range(HGRN_STEP_CHUNKS)):
            rows = slice(cc * CHUNK, (cc + 1) * CHUNK)
            q = q_ref[rows, :]
            terms = _hgrn_gate_terms(q, f_ref[rows, :], lbl_ref[...])
            gates[cc] = (q,) + terms + (_tri_matmul(tri, jnp.log(terms[2])),)
        for cc in reversed(range(HGRN_STEP_CHUNKS)):
            chunk(cc, gates[cc], v_ref, hg_ref, o_ref, st_ref, dy_ref, gw_ref, gsm_ref, ds_scr,
                  dp_buf.at[slot], *scratch)

        out_copy(slot, ns - 1 - step).start()

        @pl.when(step == ns - 1)
        def _():
            out_copy(1 - slot, 1).wait()
            out_copy(slot, 0).wait()
            for ex in exs:
                ex.wait()

    def chunk(cc, gates, v_ref, hg_ref, o_ref, st_ref, dy_ref, gw_ref, gsm_ref, ds_scr, dp_ref,
              qt_scr, kt_scr, qin_scr, kst_scr, vb_scr, dob_scr, g_scr, h_scr, dqi_scr, dks_scr, sd_scr,
              a_scr, da_scr, da0_scr):
        rows = slice(cc * CHUNK, (cc + 1) * CHUNK)
        r = _iota((CHUNK, CHUNK), 0)
        c = _iota((CHUNK, CHUNK), 1)
        triu = jnp.where(c >= r, 1.0, 0.0).astype(BF16)
        diag = (jnp.right_shift(r, 5) == jnp.right_shift(c, 5)) & (c <= r)
        row = _iota((CHUNK, D_MODEL), 0)
        sub = jnp.right_shift(row, 5)

        q, lb, tf, f, tq, qs, bh = gates
        sig = 0.5 * tf + 0.5
        sq = 0.5 * tq + 0.5
        k = 1.0 - f
        dec = _hgrn_decay(bh)
        eb, ekst, ebl = dec["eb"], dec["ekst"], dec["ebl"]
        koff = _hgrn_operands(qs, k, dec, qt_scr, kt_scr)
        qin_scr[...] = (qs * eb).astype(BF16)
        kst_scr[...] = (k * ekst).astype(BF16)
        vb_scr[...] = v_ref[rows, :].astype(BF16)
        hg = hg_ref[rows, :]
        sh = _sigmoid(hg)
        dy = dy_ref[rows, :]
        gwv = gw_ref[...]
        d_onw = dy * (hg * sh)
        d_on = d_onw * gwv
        d_gate = dy * gwv * (sh * (1.0 + hg * (1.0 - sh)))

        heads = [slice(h * HEAD_D, (h + 1) * HEAD_D) for h in range(N_HEADS)]
        for h, sl in enumerate(heads):
            o = o_ref[rows, sl]
            rs = lax.rsqrt(jnp.mean(o * o, axis=-1, keepdims=True) + EPS)
            on = o * rs
            dp_ref[rows, 3 * D_MODEL + h * HEAD_D:3 * D_MODEL + (h + 1) * HEAD_D] = (d_gate[:, sl] * on).astype(BF16)
            gsm_ref[1:2, sl] += jnp.sum(d_onw[:, sl] * on, axis=0, keepdims=True)
            d_onh = d_on[:, sl]
            dob_scr[:, sl] = (rs * (d_onh - on * jnp.mean(d_onh * on, axis=-1, keepdims=True))).astype(BF16)
        for h, sl in enumerate(heads):
            a_scr[h] = _hgrn_head_scores(qt_scr, kt_scr, sl, diag).astype(BF16)
            da = _dot(dob_scr[:, sl], vb_scr[:, sl], NT)
            da_scr[h] = da.astype(BF16)
            da0_scr[h] = jnp.where(diag, da, 0.0).astype(BF16)
        for h, sl in enumerate(heads):
            st = st_ref[cc, h]
            dst = ds_scr[h]
            dstb = dst.astype(BF16)
            dp_ref[rows, 2 * D_MODEL + h * HEAD_D:2 * D_MODEL + (h + 1) * HEAD_D] = (
                _dot(a_scr[h], dob_scr[:, sl], TN) + _dot(kst_scr[:, sl], dstb, NT)).astype(BF16)
            dqi_scr[:, sl] = _dot(dob_scr[:, sl], st.astype(BF16))
            dks_scr[:, sl] = _dot(vb_scr[:, sl], dstb)
            sd_scr[0:1, sl] = jnp.sum(st * dst, axis=0, keepdims=True)
            ds_scr[h] = dst * ebl[:, sl] + _dot(dob_scr[:, sl], qin_scr[:, sl], TN)
        for h, sl in enumerate(heads):
            g_scr[0, :, sl] = _dot(da0_scr[h], kt_scr[0, :, sl])
            h_scr[0, :, sl] = _dot(da0_scr[h], qt_scr[0, :, sl], TN)
            for j in range(1, N_SUB):
                g_scr[j, :, sl] = _dot(da_scr[h], kt_scr[j, :, sl])
                h_scr[j, :, sl] = _dot(da_scr[h], qt_scr[j, :, sl], TN)

        g0 = g_scr[0]
        h0 = h_scr[0]
        dq_inter = eb * dqi_scr[...]
        d_kst = ekst * dks_scr[...]
        db = qs * dq_inter - k * d_kst + qt_scr[0].astype(F32) * g0 - kt_scr[0].astype(F32) * h0
        gq = jnp.zeros((CHUNK, D_MODEL), F32)
        hsel = jnp.zeros((CHUNK, D_MODEL), F32)
        for j in range(N_SUB - 1):
            gj = g_scr[j + 1]
            gq = gq + dec["scales"][j] * gj
            db = db + qt_scr[j + 1].astype(F32) * gj
            hsel = jnp.where(sub == j, h_scr[j + 1], hsel)
        db = db - koff.astype(BF16).astype(F32) * hsel
        d_q = dec["eq0"] * g0 + dec["e_on"] * gq + dq_inter
        d_k = dec["ek0"] * h0 + dec["e_off"] * hsel + d_kst
        db_last = jnp.sum(k * d_kst, axis=0, keepdims=True) + ebl * sd_scr[0:1, :]
        db = db + jnp.where(row == CHUNK - 1, db_last, 0.0)
        dg = _tri_matmul(triu, db)
        d_f = dg / f - d_k
        dp_ref[rows, D_MODEL:2 * D_MODEL] = (d_f * (1.0 - lb) * sig * (1.0 - sig)).astype(BF16)
        gsm_ref[0:1, :] += jnp.sum(d_f * (1.0 - sig), axis=0, keepdims=True) * (lb * (1.0 - lb))
        dp_ref[rows, 0:D_MODEL] = (d_q * (sq * (1.0 + q * (1.0 - sq)))).astype(BF16)

    rc = lambda c: ns - 1 - c
    col = lambda j: pl.BlockSpec((step_rows, D_MODEL), lambda c: (rc(c), j))
    par = lambda rows: pl.BlockSpec((rows, D_MODEL), lambda c: (0, 0))
    return pl.pallas_call(
        body, name="hgrn_bwd", grid=(ns,),
        in_specs=[col(2), col(3), col(4), col(5), col(0),
                  pl.BlockSpec((HGRN_STEP_CHUNKS, N_HEADS, HEAD_D, HEAD_D), lambda c: (rc(c), 0, 0, 0)),
                  col(1), par(2), par(1), ANY, ANY, ANY],
        out_specs=[ANY, par(8), ANY, ANY],
        out_shape=[SDS((seq, D_IN), BF16), SDS((8, D_MODEL), F32), SDS(g_w_out.shape, F32),
                   SDS(g_small.shape, F32)],
        input_output_aliases={9: 0},
        scratch_shapes=[pltpu.VMEM((N_HEADS, HEAD_D, HEAD_D), F32),
                        pltpu.VMEM((2, step_rows, 4 * D_MODEL), BF16), pltpu.SemaphoreType.DMA((2,)),
                        pltpu.VMEM((N_SUB, CHUNK, D_MODEL), BF16), pltpu.VMEM((N_SUB, CHUNK, D_MODEL), BF16)]
                       + [pltpu.VMEM((CHUNK, D_MODEL), BF16)] * 4
                       + [pltpu.VMEM((N_SUB, CHUNK, D_MODEL), F32)] * 2 + [pltpu.VMEM((CHUNK, D_MODEL), F32)] * 2
                       + [pltpu.VMEM((8, D_MODEL), F32)] + [pltpu.VMEM((N_HEADS, CHUNK, CHUNK), BF16)] * 3
                       + EXCHANGE_SEMS * 2,
        compiler_params=_cparams(("arbitrary",)),
    )(p, p, p, p, o, states, dymix, lbl, gw, dp_full, g_w_out, g_small)


def _out_proj(yl, yh, wo, x, tgt, post_w, seq):
    tm = 512

    def body(yl_ref, yh_ref, wo_ref, x_ref, tg_ref, pw_ref, dymix_ref, dout_ref, gwo_ref, st_ref):
        @pl.when(pl.program_id(0) == 0)
        def _():
            gwo_ref[...] = jnp.zeros_like(gwo_ref)
            st_ref[...] = jnp.zeros_like(st_ref)

        ylv = yl_ref[...]
        yhv = yh_ref[...]
        y = _dot(ylv, wo_ref[0:D_MODEL, :]) + _dot(yhv, wo_ref[D_MODEL:D_MIX, :])
        r2 = lax.rsqrt(jnp.mean(y * y, axis=-1, keepdims=True) + EPS)
        yn = y * r2
        pw = pw_ref[...]
        e = (x_ref[...] + yn * pw) - tg_ref[...]
        st_ref[1:2, :] += jnp.sum(e * e, axis=0, keepdims=True) * (0.5 / D_MODEL)
        dout = e * (1.0 / D_MODEL)
        dout_ref[...] = dout
        st_ref[0:1, :] += jnp.sum(dout * yn, axis=0, keepdims=True)
        dyn = dout * pw
        dy = r2 * (dyn - yn * jnp.mean(dyn * yn, axis=-1, keepdims=True))
        dyb = dy.astype(BF16)
        dymix_ref[...] = _dot(dyb, wo_ref[...], NT)
        gwo_ref[0:D_MODEL, :] += _dot(ylv, dyb, TN)
        gwo_ref[D_MODEL:D_MIX, :] += _dot(yhv, dyb, TN)

    row = lambda w: pl.BlockSpec((tm, w), lambda m: (m, 0))
    full = lambda shape: pl.BlockSpec(shape, lambda m: (0,) * len(shape))
    once = lambda shape: pl.BlockSpec(shape, lambda m: (0,) * len(shape), pipeline_mode=pl.Buffered(1))
    return pl.pallas_call(
        body, name="out_proj", grid=(seq // tm,),
        in_specs=[row(D_MODEL), row(D_MODEL), once((D_MIX, D_MODEL)), row(D_MODEL), row(D_MODEL),
                  full((1, D_MODEL))],
        out_specs=[row(D_MIX), row(D_MODEL), once((D_MIX, D_MODEL)), full((8, D_MODEL))],
        out_shape=[SDS((seq, D_MIX), F32), SDS((seq, D_MODEL), F32), SDS((D_MIX, D_MODEL), F32),
                   SDS((8, D_MODEL), F32)],
        compiler_params=_cparams(("arbitrary",)),
    )(yl, yh, wo, x, tgt, post_w)


MESH = pl.DeviceIdType.MESH
ANY = pl.BlockSpec(memory_space=pl.ANY)
EXCHANGE_SEMS = [pltpu.SemaphoreType.DMA((N_DEV - 1,)), pltpu.SemaphoreType.DMA((N_DEV - 1,)),
                 pltpu.SemaphoreType.DMA(())]


def _mesh_pos():
    return lax.axis_index("x"), lax.axis_index("y"), lax.axis_index("c")


class _SlotExchange:
    def __init__(self, src_ref, dst_ref, send_sems, recv_sems, local_sem, blocked):
        x, y, c = _mesh_pos()
        me = 4 * x + 2 * y + c
        src = (lambda dest: src_ref.at[dest]) if blocked else (lambda dest: src_ref)
        self.local = pltpu.make_async_copy(src(me), dst_ref.at[me], local_sem)
        self.sends, self.recvs = [], []
        for k in range(1, N_DEV):
            px = 1 - x if (k >> 2) & 1 else x
            py = 1 - y if (k >> 1) & 1 else y
            pc = 1 - c if k & 1 else c
            peer = 4 * px + 2 * py + pc
            sems = dict(send_sem=send_sems.at[k - 1], recv_sem=recv_sems.at[k - 1],
                        device_id=(px, py, pc), device_id_type=MESH)
            self.sends.append(pltpu.make_async_remote_copy(src_ref=src(peer), dst_ref=dst_ref.at[me], **sems))
            self.recvs.append(pltpu.make_async_remote_copy(src_ref=dst_ref.at[peer], dst_ref=dst_ref.at[peer], **sems))

    def start(self):
        self.local.start()
        for cp in self.sends:
            cp.start()

    def wait(self):
        for cp in self.recvs:
            cp.wait_recv()
        for cp in self.sends:
            cp.wait_send()
        self.local.wait()


class _ChipExchange:
    def __init__(self, src_ref, dst_ref, send_sems, recv_sems, local_sem):
        x, y, c = _mesh_pos()
        chip = 2 * x + y
        self.local = pltpu.make_async_copy(src_ref.at[chip], dst_ref.at[chip], local_sem)
        self.sends, self.recvs = [], []
        for k in range(1, N_CHIPS):
            px = 1 - x if (k >> 1) & 1 else x
            py = 1 - y if k & 1 else y
            peer = 2 * px + py
            sems = dict(send_sem=send_sems.at[k - 1], recv_sem=recv_sems.at[k - 1],
                        device_id=(px, py, c), device_id_type=MESH)
            self.sends.append(pltpu.make_async_remote_copy(src_ref=src_ref.at[peer], dst_ref=dst_ref.at[chip], **sems))
            self.recvs.append(pltpu.make_async_remote_copy(src_ref=dst_ref.at[peer], dst_ref=dst_ref.at[peer], **sems))

    def start(self):
        self.local.start()
        for cp in self.sends:
            cp.start()

    def wait(self):
        for cp in self.recvs:
            cp.wait_recv()
        for cp in self.sends:
            cp.wait_send()
        self.local.wait()


GRAD_W_IN_TK = 2048


def _grad_w_in_sibling(u, dp, core, seq):
    tk = min(GRAD_W_IN_TK, seq)
    nk = seq // tk

    def body(core_ref, u_ref, dp_ref, g_ref):
        del core_ref

        @pl.when(pl.program_id(1) == 0)
        def _():
            g_ref[...] = jnp.zeros_like(g_ref)

        g_ref[0] += _dot(u_ref[...], dp_ref[...], TN)

    return pl.pallas_call(
        body, name="grad_w_in_sibling",
        grid_spec=pltpu.PrefetchScalarGridSpec(
            num_scalar_prefetch=1, grid=(N_CHIPS, nk),
            in_specs=[pl.BlockSpec((tk, D_MODEL), lambda n, k, c: (k, 0)),
                      pl.BlockSpec((tk, W_BLK), lambda n, k, c: (k, 2 * n + 1 - c[0]))],
            out_specs=pl.BlockSpec((1, D_MODEL, W_BLK), lambda n, k, c: (n, 0, 0))),
        out_shape=SDS((N_CHIPS, D_MODEL, W_BLK), F32),
        compiler_params=_cparams(("parallel", "arbitrary")),
    )(core, u, dp)


def _grad_w_in_own(u, dp, core, g_sib, seq):
    tk = min(GRAD_W_IN_TK, seq)
    nk = seq // tk

    def body(core_ref, u_ref, dp_ref, gsib_ref, g_ref, land, send_sem, recv_sem):
        del core_ref
        n = pl.program_id(0)
        k = pl.program_id(1)
        x, y, c = _mesh_pos()
        swap = pltpu.make_async_remote_copy(src_ref=gsib_ref, dst_ref=land, send_sem=send_sem, recv_sem=recv_sem,
                                            device_id=(x, y, 1 - c), device_id_type=MESH)

        @pl.when((n == 0) & (k == 0))
        def _():
            swap.start()

        @pl.when(k == 0)
        def _():
            g_ref[...] = jnp.zeros_like(g_ref)

        g_ref[0] += _dot(u_ref[...], dp_ref[...], TN)

        @pl.when((n == 0) & (k == nk - 1))
        def _():
            swap.wait_recv()

        @pl.when(k == nk - 1)
        def _():
            g_ref[0] += land[n]

        @pl.when((n == N_CHIPS - 1) & (k == nk - 1))
        def _():
            swap.wait_send()

    return pl.pallas_call(
        body, name="grad_w_in_own",
        grid_spec=pltpu.PrefetchScalarGridSpec(
            num_scalar_prefetch=1, grid=(N_CHIPS, nk),
            in_specs=[pl.BlockSpec((tk, D_MODEL), lambda n, k, c: (k, 0)),
                      pl.BlockSpec((tk, W_BLK), lambda n, k, c: (k, 2 * n + c[0])), ANY],
            out_specs=pl.BlockSpec((1, D_MODEL, W_BLK), lambda n, k, c: (n, 0, 0)),
            scratch_shapes=[pltpu.VMEM((N_CHIPS, D_MODEL, W_BLK), F32), pltpu.SemaphoreType.DMA(()),
                            pltpu.SemaphoreType.DMA(())]),
        out_shape=SDS((N_CHIPS, D_MODEL, W_BLK), F32),
        compiler_params=_cparams(("arbitrary", "arbitrary")),
    )(core, u, dp, g_sib)


def _grad_x(dp, w_all, x, pre_w, dout, g_chip, seq):
    tm = 512
    nm = seq // tm

    def body(dp_ref, w_ref, x_ref, pw_ref, do_ref, gsrc_ref, gx_ref, gpw_ref, recv_ref,
             send_sems, recv_sems, local_sem):
        m = pl.program_id(0)
        ex = _ChipExchange(gsrc_ref, recv_ref, send_sems, recv_sems, local_sem)

        @pl.when(m == 0)
        def _():
            ex.start()
            gpw_ref[...] = jnp.zeros_like(gpw_ref)

        du = _dot(dp_ref[:, 0:W_BLK], w_ref[0], NT)
        for j in range(1, N_DEV):
            du = du + _dot(dp_ref[:, j * W_BLK:(j + 1) * W_BLK], w_ref[j], NT)
        xv = x_ref[...]
        r1 = lax.rsqrt(jnp.mean(xv * xv, axis=-1, keepdims=True) + EPS)
        xn = xv * r1
        gpw_ref[0:1, :] += jnp.sum(du * xn, axis=0, keepdims=True)
        dxn = du * pw_ref[...]
        gx_ref[...] = r1 * (dxn - xn * jnp.mean(dxn * xn, axis=-1, keepdims=True)) + do_ref[...]

        @pl.when(m == nm - 1)
        def _():
            ex.wait()

    row = lambda w: pl.BlockSpec((tm, w), lambda m: (m, 0))
    return pl.pallas_call(
        body, name="grad_x", grid=(nm,),
        in_specs=[row(D_IN), pl.BlockSpec((N_DEV, D_MODEL, W_BLK), lambda m: (0, 0, 0), pipeline_mode=pl.Buffered(1)),
                  row(D_MODEL),
                  pl.BlockSpec((1, D_MODEL), lambda m: (0, 0)), row(D_MODEL), ANY],
        out_specs=[row(D_MODEL), pl.BlockSpec((8, D_MODEL), lambda m: (0, 0)), ANY],
        out_shape=[SDS((seq, D_MODEL), F32), SDS((8, D_MODEL), F32), SDS(g_chip.shape, F32)],
        scratch_shapes=[pltpu.SemaphoreType.DMA((N_CHIPS - 1,)), pltpu.SemaphoreType.DMA((N_CHIPS - 1,)),
                        pltpu.SemaphoreType.DMA(())],
        compiler_params=_cparams(("arbitrary",)),
    )(dp, w_all, x, pre_w, dout, g_chip)


def _local_step(x, tgt, p, u, conv_w, conv_b, wa, wx, ba, bx, lam, lbl, gnorm_w, w_out, post_w):
    seq = x.shape[0]
    h, y_lru = _lru_forward(p, conv_w, conv_b, wa, wx, ba, bx, lam, seq)
    y_hgrn, o, states = _hgrn_forward(p, lbl, gnorm_w, seq)
    dymix, dout, g_w_out, stats = _out_proj(y_lru, y_hgrn, w_out, x, tgt, post_w, seq)
    dp_lru, g_wa, g_wx, ls = _lru_backward(p, h, dymix, conv_w, conv_b, wa, wx, ba, bx, lam, seq)
    g_small = _pack_small(_shard_rows(g_wa, LRU_BLOCKS), _shard_rows(g_wx, LRU_BLOCKS),
                          _shard_rows(ls[0:4].reshape(4, D_MODEL, 1), 4).reshape(N_DEV, 4, 128),
                          _shard_rows(ls[5].reshape(4, LRU_BW, 1), 4).reshape(N_DEV, 4, 32),
                          _shard_rows(ls[6].reshape(4, LRU_BW, 1), 4).reshape(N_DEV, 4, 32))
    dp, hgrn_small, r_out, r_small = _hgrn_backward(
        p, o, states, dymix, lbl, gnorm_w, dp_lru, g_w_out.reshape(N_DEV, D_MIX // N_DEV, D_MODEL), g_small, seq)
    return dict(u=u, dp=dp, dout=dout, r_out=r_out, r_small=r_small,
                lru_small=ls, hgrn_small=hgrn_small, stats=stats)


class _TwoLevelGather:
    def __init__(self, ins, outs, send_sems, recv_sems, local_sems):
        self.ins, self.outs = ins, outs
        self.send_sems, self.recv_sems, self.local_sems = send_sems, recv_sems, local_sems
        x, y, c = _mesh_pos()
        self.c = c
        self.me, self.sibling = (x, y, c), (x, y, 1 - c)
        self.chips = [(1 - x, y), (x, 1 - y), (1 - x, 1 - y)]
        n = len(ins)
        self.mine = [pltpu.make_async_copy(ins[a], self._slot(a, self.me), local_sems.at[a]) for a in range(n)]
        self.first = []
        for a in range(n):
            self.first.append(self._copy(a, 0, self.me, self.sibling, src=ins[a]))
            self.first += [self._copy(a, 1 + j, self.me, (*chip, c), src=ins[a])
                           for j, chip in enumerate(self.chips)]
        self.passed = [self._copy(a, 4 + j, (*chip, c), self.sibling)
                       for j, chip in enumerate(self.chips) for a in range(n)]

    def _slot(self, a, pos):
        return self.outs[a].at[4 * pos[0] + 2 * pos[1] + pos[2]]

    def _copy(self, a, k, block, to, src=None):
        dst = self._slot(a, block)
        return pltpu.make_async_remote_copy(
            src_ref=dst if src is None else src, dst_ref=dst,
            send_sem=self.send_sems.at[a, k], recv_sem=self.recv_sems.at[a, k],
            device_id=to, device_id_type=MESH)

    def start(self):
        for cp in self.mine + self.first:
            cp.start()

    def forward(self):
        n = len(self.ins)
        for j, chip in enumerate(self.chips):
            for a in range(n):
                self._copy(a, 1 + j, (*chip, self.c), self.me).wait_recv()
                self.passed[j * n + a].start()

    def finish(self):
        for a in range(len(self.ins)):
            self._copy(a, 0, self.sibling, self.me).wait_recv()
            for j, chip in enumerate(self.chips):
                self._copy(a, 4 + j, (*chip, 1 - self.c), self.me).wait_recv()
        for cp in self.first + self.passed:
            cp.wait_send()
        for cp in self.mine:
            cp.wait()


W_IN_DIRECT = (1, 2, 4, 6)
W_IN_PASSED = (2, 4, 6)

def _in_proj_gather(x, pre_w, w_in_blk, w_out_blk, small_blk, me, seq):
    tm = min(1024, seq)
    nm = seq // tm
    last = N_DEV - 1

    def body(me_ref, x_ref, pw_ref, wblk_ref, woblk_ref, smblk_ref,
             p_ref, u_ref, wall_ref, woall_ref, small_ref,
             u_all, w_vmem, own_sem, d_send, d_recv, f_send, f_recv, wb_sems, g_send, g_recv, g_local):
        i = pl.program_id(0)
        m = pl.program_id(1)
        idx = me_ref[0]
        x_, y_, c_ = _mesh_pos()
        aux = _TwoLevelGather([woblk_ref, smblk_ref], [woall_ref, small_ref], g_send, g_recv, g_local)

        def peer(k):
            return (1 - x_ if (k >> 2) & 1 else x_, 1 - y_ if (k >> 1) & 1 else y_, 1 - c_ if k & 1 else c_)

        def direct(k):
            f = W_IN_DIRECT.index(k)
            return (pltpu.make_async_remote_copy(src_ref=wblk_ref, dst_ref=w_vmem.at[idx], send_sem=d_send.at[f],
                                                 recv_sem=d_recv.at[f], device_id=peer(k), device_id_type=MESH),
                    pltpu.make_async_remote_copy(src_ref=w_vmem.at[idx ^ k], dst_ref=w_vmem.at[idx ^ k],
                                                 send_sem=d_send.at[f], recv_sem=d_recv.at[f], device_id=peer(k),
                                                 device_id_type=MESH))

        def passed(k):
            f = W_IN_PASSED.index(k)
            return (pltpu.make_async_remote_copy(src_ref=w_vmem.at[idx ^ k], dst_ref=w_vmem.at[idx ^ k],
                                                 send_sem=f_send.at[f], recv_sem=f_recv.at[f], device_id=peer(1),
                                                 device_id_type=MESH),
                    pltpu.make_async_remote_copy(src_ref=w_vmem.at[idx ^ (k + 1)], dst_ref=w_vmem.at[idx ^ (k + 1)],
                                                 send_sem=f_send.at[f], recv_sem=f_recv.at[f], device_id=peer(1),
                                                 device_id_type=MESH))

        def write_back(k):
            return pltpu.make_async_copy(w_vmem.at[idx ^ k], wall_ref.at[idx ^ k], wb_sems.at[k])

        own = pltpu.make_async_copy(wblk_ref, w_vmem.at[idx], own_sem)

        @pl.when((i == 0) & (m == 0))
        def _():
            own.start()
            for k in (1, 2, 4):
                direct(k)[0].start()
            own.wait()
            write_back(0).start()

        for k in range(1, N_DEV):
            @pl.when((i == k) & (m == 0))
            def _(k=k):
                if k in W_IN_DIRECT:
                    direct(k)[1].wait_recv()
                    if k in W_IN_PASSED:
                        passed(k)[0].start()
                else:
                    passed(k - 1)[1].wait_recv()
                write_back(k).start()
                if k == 2:
                    direct(6)[0].start()
                if k == 4:
                    aux.start()
                if k == N_DEV - 1:
                    aux.forward()

        rows = pl.ds(pl.multiple_of(m * tm, tm), tm)

        @pl.when(i == 0)
        def _():
            xv = x_ref[...]
            r = lax.rsqrt(jnp.mean(xv * xv, axis=-1, keepdims=True) + EPS)
            ub = (xv * r * pw_ref[...]).astype(BF16)
            u_all[rows, :] = ub
            u_ref[...] = ub

        p_ref[...] = _dot(u_all[rows, :], w_vmem[idx ^ i])

        @pl.when((i == last) & (m == nm - 1))
        def _():
            for k in W_IN_DIRECT:
                direct(k)[0].wait_send()
            for k in W_IN_PASSED:
                passed(k)[0].wait_send()
            for k in range(N_DEV):
                write_back(k).wait()
            aux.finish()

    first_pass = lambda i, m: jnp.where(i == 0, m, nm - 1)
    return pl.pallas_call(
        body, name="in_proj_gather",
        grid_spec=pltpu.PrefetchScalarGridSpec(
            num_scalar_prefetch=1, grid=(N_DEV, nm),
            in_specs=[pl.BlockSpec((tm, D_MODEL), lambda i, m, me: (first_pass(i, m), 0)),
                      pl.BlockSpec((1, D_MODEL), lambda i, m, me: (0, 0)), ANY, ANY, ANY],
            out_specs=[pl.BlockSpec((tm, W_BLK), lambda i, m, me: (m, me[0] ^ i)),
                       pl.BlockSpec((tm, D_MODEL), lambda i, m, me: (first_pass(i, m), 0)), ANY, ANY, ANY],
            scratch_shapes=[pltpu.VMEM((seq, D_MODEL), BF16), pltpu.VMEM((N_DEV, D_MODEL, W_BLK), BF16),
                            pltpu.SemaphoreType.DMA(()),
                            pltpu.SemaphoreType.DMA((len(W_IN_DIRECT),)), pltpu.SemaphoreType.DMA((len(W_IN_DIRECT),)),
                            pltpu.SemaphoreType.DMA((len(W_IN_PASSED),)), pltpu.SemaphoreType.DMA((len(W_IN_PASSED),)),
                            pltpu.SemaphoreType.DMA((N_DEV,)),
                            pltpu.SemaphoreType.DMA((2, 7)), pltpu.SemaphoreType.DMA((2, 7)),
                            pltpu.SemaphoreType.DMA((2,))]),
        out_shape=[SDS((seq, D_IN), F32), SDS((seq, D_MODEL), BF16), SDS((N_DEV, D_MODEL, W_BLK), BF16),
                   SDS((N_DEV,) + w_out_blk.shape, w_out_blk.dtype), SDS((N_DEV,) + small_blk.shape, small_blk.dtype)],
        compiler_params=_cparams(("arbitrary", "arbitrary")),
    )(me, x, pre_w, w_in_blk, w_out_blk, small_blk)


def _exchange_grads(blocks, repl):
    nb = len(blocks)
    n = nb + 1

    def body(*refs):
        ins, outs, sems = refs[:n], refs[n:2 * n], refs[2 * n:]
        exs = [_SlotExchange(ins[a], outs[a], *sems[3 * a:3 * a + 3], blocked=a < nb) for a in range(n)]
        for ex in exs:
            ex.start()
        for ex in exs:
            ex.wait()

    arrs = list(blocks) + [repl]
    shapes = [SDS(b.shape, b.dtype) for b in blocks] + [SDS((N_DEV,) + repl.shape, repl.dtype)]
    return pl.pallas_call(
        body, name="exchange_small", out_shape=shapes,
        in_specs=[ANY] * n, out_specs=[ANY] * n,
        scratch_shapes=EXCHANGE_SEMS * n,
    )(*arrs)


def _pack_rows(picks, name):
    arrs = [p[0] for p in picks]

    def body(*refs):
        out = refs[-1]
        out[...] = jnp.zeros_like(out)
        at = 0
        for ref, (_, row, rows, scale) in zip(refs[:-1], picks):
            out[at:at + rows, :] = ref[row:row + rows, :] * scale
            at += rows

    return pl.pallas_call(body, name=name, out_shape=SDS((8, D_MODEL), F32))(*arrs)


def _adamw(g, w, m, v):
    m2 = ADAM_B1 * m + (1.0 - ADAM_B1) * g
    v2 = ADAM_B2 * v + (1.0 - ADAM_B2) * (g * g)
    m_hat = m2 / (1.0 - ADAM_B1 ** ADAM_STEP)
    v_hat = v2 / (1.0 - ADAM_B2 ** ADAM_STEP)
    delta = -ADAM_LR * (m_hat / (jnp.sqrt(v_hat) + ADAM_EPS) + ADAM_WD * w)
    return delta, m2, v2


def _sum_slots(r_ref):
    g = r_ref[0]
    for s in range(1, r_ref.shape[0]):
        g = g + r_ref[s]
    return g


def _sum_adamw(recv, w, m, v, tr, name):
    rows, cols = w.shape

    def body(r_ref, w_ref, m_ref, v_ref, g_ref, d_ref, m2_ref, v2_ref):
        g = _sum_slots(r_ref)
        g_ref[...] = g
        d_ref[...], m2_ref[...], v2_ref[...] = _adamw(g, w_ref[...], m_ref[...], v_ref[...])

    blk = pl.BlockSpec((tr, cols), lambda i: (i, 0))
    return pl.pallas_call(
        body, name=name, grid=(rows // tr,),
        in_specs=[pl.BlockSpec((recv.shape[0], tr, cols), lambda i: (0, i, 0)), blk, blk, blk],
        out_specs=[blk] * 4, out_shape=[SDS((rows, cols), F32)] * 4,
        compiler_params=_cparams(("parallel",)),
    )(recv, w, m, v)


def _sum_adamw_pieces(recv, rows, ws, ms, vs, name, loss_row=None):
    n = len(ws)

    def body(r_ref, *refs):
        w_refs, m_refs, v_refs, outs = refs[:n], refs[n:2 * n], refs[2 * n:3 * n], refs[3 * n:]
        g = _sum_slots(r_ref)
        for i, (row, nrows) in enumerate(rows):
            gi = g[row:row + nrows, 0:ws[i].shape[1]]
            outs[i][...] = gi
            outs[n + i][...], outs[2 * n + i][...], outs[3 * n + i][...] = _adamw(
                gi, w_refs[i][...], m_refs[i][...], v_refs[i][...])
        if loss_row is not None:
            total = jnp.sum(g[loss_row:loss_row + 1, :], axis=-1, keepdims=True)
            outs[4 * n][...] = jnp.broadcast_to(total, outs[4 * n].shape)

    shapes = [SDS(w.shape, F32) for w in ws] * 4 + ([SDS((8, 128), F32)] if loss_row is not None else [])
    out = pl.pallas_call(body, name=name, out_shape=shapes)(recv, *ws, *ms, *vs)
    return [out[k * n:(k + 1) * n] for k in range(4)] + list(out[4 * n:])


def _shard_rows(t, lead):
    r = t.shape[1] // N_DEV
    t = t.reshape((lead, N_DEV, r) + t.shape[2:])
    return jnp.moveaxis(t, 1, 0)


def _pad_tile(t):
    return jnp.pad(t, ((0, 0), (0, 8 - t.shape[1]), (0, SM_LANES - t.shape[2])))


def _pack_small(wa, wx, cw, b_a, b_x):
    n = wa.shape[0]
    return jnp.concatenate([wa.reshape(n, 128, SM_LANES), wx.reshape(n, 128, SM_LANES),
                            _pad_tile(cw), _pad_tile(b_a), _pad_tile(b_x)], axis=1)


def _unpack_small(t):
    n = t.shape[0]
    return (t[:, SM_WA:SM_WA + 128].reshape(n, 4, 32, 256), t[:, SM_WX:SM_WX + 128].reshape(n, 4, 32, 256),
            t[:, SM_CW:SM_CW + 4, 0:128], t[:, SM_BA:SM_BA + 4, 0:32], t[:, SM_BX:SM_BX + 4, 0:32])


def kernel(x, pre_norm_w, w_in, conv_w, conv_b, lru_w_a, lru_b_a, lru_w_x, lru_b_x, lru_lambda, hgrn_lb_logits, hgrn_gnorm_w, w_out, post_norm_w, loss_target, m_pre_norm_w, m_w_in, m_conv_w, m_conv_b, m_lru_w_a, m_lru_b_a, m_lru_w_x, m_lru_b_x, m_lru_lambda, m_hgrn_lb_logits, m_hgrn_gnorm_w, m_w_out, m_post_norm_w, v_pre_norm_w, v_w_in, v_conv_w, v_conv_b, v_lru_w_a, v_lru_b_a, v_lru_w_x, v_lru_b_x, v_lru_lambda, v_hgrn_lb_logits, v_hgrn_gnorm_w, v_w_out, v_post_norm_w):
    seq = x.shape[1]
    x2 = x.reshape(seq, D_MODEL)
    tgt = loss_target.reshape(seq, D_MODEL)

    small_w = _pack_small(lru_w_a, lru_w_x, conv_w, lru_b_a, lru_b_x)[0]
    me = (4 * lax.axis_index("x") + 2 * lax.axis_index("y") + lax.axis_index("c")).astype(jnp.int32).reshape(1)
    p, u, w_in_all, w_out_all, small_all = _in_proj_gather(
        x2, pre_norm_w, w_in[0].astype(BF16), w_out[0].astype(BF16), small_w, me, seq)
    wa_s, wx_s, cw_s, ba_s, bx_s = _unpack_small(small_all)
    wa = jnp.moveaxis(wa_s, 0, 1).reshape(LRU_BLOCKS, LRU_BW, LRU_BW).astype(BF16)
    wx = jnp.moveaxis(wx_s, 0, 1).reshape(LRU_BLOCKS, LRU_BW, LRU_BW).astype(BF16)
    cw = jnp.moveaxis(cw_s, 0, 1).reshape(4, D_MODEL)
    ba = jnp.moveaxis(ba_s, 0, 1).reshape(1, D_MODEL)
    bx = jnp.moveaxis(bx_s, 0, 1).reshape(1, D_MODEL)

    loc = _local_step(x2, tgt, p, u, cw, conv_b, wa, wx, ba, bx, lru_lambda,
                      hgrn_lb_logits, hgrn_gnorm_w, w_out_all.reshape(D_MIX, D_MODEL), post_norm_w)

    ls, r_out, r_small = loc["lru_small"], loc["r_out"], loc["r_small"]
    core = lax.axis_index("c").astype(jnp.int32).reshape(1)
    g_sib = _grad_w_in_sibling(loc["u"], loc["dp"], core, seq)
    g_chip = _grad_w_in_own(loc["u"], loc["dp"], core, g_sib, seq)
    grad_x, pre_small, r_in = _grad_x(loc["dp"], w_in_all, x2, pre_norm_w, loc["dout"], g_chip, seq)
    g_repl = _pack_rows([(pre_small, 0, 1, 1.0), (ls, 4, 1, 1.0), (ls, 7, 1, 1.0),
                         (loc["hgrn_small"], 0, 1, 1.0), (loc["hgrn_small"], 0, 1, -1.0),
                         (loc["hgrn_small"], 1, 1, 1.0), (loc["stats"], 0, 2, 1.0)], "pack_grads")
    (r_repl,) = _exchange_grads([], g_repl)

    repl_rows = [(RP_PRE, 1), (RP_CB, 1), (RP_LAM, 1), (RP_LB0, 2), (RP_GN, 1), (RP_POST, 1)]
    o_repl = _sum_adamw_pieces(
        r_repl, repl_rows,
        [pre_norm_w, conv_b, lru_lambda, hgrn_lb_logits, hgrn_gnorm_w, post_norm_w],
        [m_pre_norm_w, m_conv_b, m_lru_lambda, m_hgrn_lb_logits, m_hgrn_gnorm_w, m_post_norm_w],
        [v_pre_norm_w, v_conv_b, v_lru_lambda, v_hgrn_lb_logits, v_hgrn_gnorm_w, v_post_norm_w],
        "adamw_repl", loss_row=RP_LOSS)
    loss = o_repl[4][0, 0]

    small_rows = [(SM_WA, 128), (SM_WX, 128), (SM_CW, 4), (SM_BA, 4), (SM_BX, 4)]
    as_rows = lambda wa_, wx_, cw_, ba_, bx_: [wa_.reshape(128, 256), wx_.reshape(128, 256), cw_.reshape(4, 128),
                                               ba_.reshape(4, 32), bx_.reshape(4, 32)]
    o_small = _sum_adamw_pieces(
        r_small, small_rows, as_rows(lru_w_a, lru_w_x, conv_w, lru_b_a, lru_b_x),
        as_rows(m_lru_w_a, m_lru_w_x, m_conv_w, m_lru_b_a, m_lru_b_x),
        as_rows(v_lru_w_a, v_lru_w_x, v_conv_w, v_lru_b_a, v_lru_b_x), "adamw_small")

    o_in = _sum_adamw(r_in, w_in[0], m_w_in[0], v_w_in[0], 128, "adamw_w_in")
    o_out = _sum_adamw(r_out, w_out[0], m_w_out[0], v_w_out[0], 64, "adamw_w_out")

    outs = [loss, grad_x.reshape(x.shape)]
    for kind in range(4):
        pre, cb, lam, lb, gn, post = o_repl[kind]
        swa, swx, scw, sba, sbx = o_small[kind]
        outs += [pre, o_in[kind][None], scw.reshape(conv_w.shape), cb, swa.reshape(lru_w_a.shape),
                 sba.reshape(lru_b_a.shape), swx.reshape(lru_w_x.shape), sbx.reshape(lru_b_x.shape),
                 lam, lb, gn, o_out[kind][None], post]
    return tuple(outs)
```
